```python
import jax, jax.numpy as jnp
from jax import lax
import numpy as np

D_MODEL = 2048
BATCH = 2
SEQ = 4096
DEPTH = 1
DEC_BATCH = 32
DEC_SEQ = 16
PAST_LEN = 2048

CHUNK = 64
D_MIX = D_MODEL
GLA_HEADS = 8
GLA_DK = 64
GLA_DV = D_MIX // 2 // GLA_HEADS
GLA_QK = GLA_HEADS * GLA_DK
GLA_VW = GLA_HEADS * GLA_DV
GLA_GATE_RANK = 16
GLA_TAU = 16.0
SWA_HEADS = 16
SWA_KV_HEADS = 2
SWA_GROUP = SWA_HEADS // SWA_KV_HEADS
SWA_HEAD_DIM = D_MIX // 2 // SWA_HEADS
SWA_QW = SWA_HEADS * SWA_HEAD_DIM
SWA_KVW = SWA_KV_HEADS * SWA_HEAD_DIM
SWA_WINDOW = 128
WINDOW_CHUNKS = SWA_WINDOW // CHUNK
MEM_TOKENS = 256
MEM_HEADS = 4
MEM_HEAD_DIM = D_MODEL // MEM_HEADS
N_EXPERTS = 64
TOP_K = 8
N_GROUPS = 8
TOPK_GROUPS = 4
D_EXPERT = 512
D_SHARED = 512
ROUTED_SCALE = 2.5
EXPERT_BLOCK = 128
DEEPNORM_ALPHA = (2 * DEPTH) ** 0.25
DEEPNORM_BETA = (8 * DEPTH) ** -0.25
LN_EPS = 1e-5
RMS_EPS = 1e-6
IN_SIZES = (GLA_QK, GLA_QK, GLA_VW, GLA_VW, GLA_GATE_RANK, SWA_QW, SWA_KVW, SWA_KVW)
IN_COLS = sum(IN_SIZES)

kernel_name = 'hybrid_gla_swa_moe_stream_step'

F32 = jnp.float32


def layer_norm(x, g, b):
    xf = x.astype(F32)
    mu = jnp.mean(xf, -1, keepdims=True)
    var = jnp.mean(jnp.square(xf - mu), -1, keepdims=True)
    return ((xf - mu) * lax.rsqrt(var + LN_EPS) * g.astype(F32) + b.astype(F32)).astype(x.dtype)


def split_in_proj(z):
    parts, start = [], 0
    for size in IN_SIZES:
        parts.append(z[..., start:start + size])
        start += size
    return parts


def alibi_slopes(n):
    return 2.0 ** (-8.0 * jnp.arange(1, n + 1, dtype=F32) / n)


def swiglu(x, wg, wu, wd):
    return (jax.nn.silu(x @ wg) * (x @ wu)) @ wd


def gla_inputs(gq, gk, gv, gg, wa2, ba):
    b, l, _ = gq.shape
    heads = lambda t, d: jnp.swapaxes(t.reshape(b, l, GLA_HEADS, d).astype(F32), 1, 2)
    log_a = jax.nn.log_sigmoid((gg @ wa2 + ba).astype(F32)) / GLA_TAU
    q = heads(gq, GLA_DK) * GLA_DK ** -0.5
    return q, heads(gk, GLA_DK), heads(gv, GLA_DV), heads(log_a, GLA_DK)


def gla_chunked(q, k, v, log_a, s0):
    c = q.shape[3]
    bcum = jnp.cumsum(log_a, axis=3)
    b_last = bcum[:, :, :, -1:, :]
    q_in = q * jnp.exp(bcum)
    k_in = k * jnp.exp(-bcum)
    k_out = k * jnp.exp(b_last - bcum)
    causal = jnp.tril(jnp.ones((c, c), bool))
    attn = jnp.where(causal, jnp.einsum('bhnqd,bhnsd->bhnqs', q_in, k_in), 0.0)
    o_intra = jnp.einsum('bhnqs,bhnse->bhnqe', attn, v)
    upd = jnp.einsum('bhnsd,bhnse->bhnde', k_out, v)
    decay = jnp.exp(b_last[:, :, :, 0, :])

    def step(s, inp):
        dec, u = inp
        return dec[..., None] * s + u, s

    s_final, s_prev = lax.scan(step, s0.astype(F32), (jnp.moveaxis(decay, 2, 0), jnp.moveaxis(upd, 2, 0)))
    s_prev = jnp.moveaxis(s_prev, 0, 2)
    o_inter = jnp.einsum('bhnqd,bhnde->bhnqe', q_in, s_prev)
    return o_intra + o_inter, s_final


def gla_output(o, r, norm_g):
    b, _, l, _ = o.shape
    o = jnp.swapaxes(o, 1, 2)
    o = o * lax.rsqrt(jnp.mean(jnp.square(o), -1, keepdims=True) + RMS_EPS) * norm_g.astype(F32)
    gate = jax.nn.silu(r.reshape(b, l, GLA_HEADS, GLA_DV).astype(F32))
    return (o * gate).reshape(b, l, GLA_VW).astype(r.dtype)


def sink_attention(q, k, v, qpos, kpos, sinks):
    slopes = alibi_slopes(SWA_HEADS).reshape(SWA_KV_HEADS, SWA_GROUP)
    s = jnp.einsum('bnqhgd,bnshd->bnhgqs', q.astype(F32), k.astype(F32)) * SWA_HEAD_DIM ** -0.5
    dist = jnp.abs(qpos[:, :, None] - kpos[:, None, :]).astype(F32)
    s = s - slopes[None, None, :, :, None, None] * dist[None, :, None, None]
    qc = qpos[:, :, None] // CHUNK
    kc = kpos[:, None, :] // CHUNK
    allowed = (kpos[:, None, :] >= 0) & (kc <= qc) & (kc >= qc - WINDOW_CHUNKS)
    s = jnp.where(allowed[None, :, None, None], s, -jnp.inf)
    sink = sinks.astype(F32).reshape(SWA_KV_HEADS, SWA_GROUP)[None, None, :, :, None, None]
    m = jnp.maximum(jnp.max(s, -1, keepdims=True), sink)
    p = jnp.exp(s - m)
    denom = jnp.sum(p, -1, keepdims=True) + jnp.exp(sink - m)
    return jnp.einsum('bnhgqs,bnshd->bnqhgd', p / denom, v.astype(F32))


def chunk_band(t, n):
    b = t.shape[0]
    tc = t.reshape(b, n, CHUNK, SWA_KV_HEADS, SWA_HEAD_DIM)
    tp = jnp.pad(tc, ((0, 0), (WINDOW_CHUNKS, 0), (0, 0), (0, 0), (0, 0)))
    return jnp.concatenate([tp[:, j:j + n] for j in range(WINDOW_CHUNKS + 1)], axis=2)


def mixer_prompt(h, w_in, wa2, ba, norm_g, sinks, w_out):
    b, l, _ = h.shape
    n = l // CHUNK
    gq, gk, gv, gr, gg, sq, sk, sv = split_in_proj(h @ w_in)
    q, k, v, log_a = gla_inputs(gq, gk, gv, gg, wa2, ba)
    blk = lambda t: t.reshape(b, GLA_HEADS, n, CHUNK, t.shape[-1])
    o, s_final = gla_chunked(blk(q), blk(k), blk(v), blk(log_a), jnp.zeros((b, GLA_HEADS, GLA_DK, GLA_DV), F32))
    o_gla = gla_output(o.reshape(b, GLA_HEADS, l, GLA_DV), gr, norm_g)
    qh = sq.reshape(b, n, CHUNK, SWA_KV_HEADS, SWA_GROUP, SWA_HEAD_DIM)
    kh = sk.reshape(b, l, SWA_KV_HEADS, SWA_HEAD_DIM)
    vh = sv.reshape(b, l, SWA_KV_HEADS, SWA_HEAD_DIM)
    qpos = jnp.arange(l).reshape(n, CHUNK)
    kpos = (jnp.arange(n)[:, None] - WINDOW_CHUNKS) * CHUNK + jnp.arange((WINDOW_CHUNKS + 1) * CHUNK)[None, :]
    o_swa = sink_attention(qh, chunk_band(kh, n), chunk_band(vh, n), qpos, kpos, sinks)
    o_swa = o_swa.reshape(b, l, SWA_QW).astype(h.dtype)
    out = jnp.concatenate([o_gla, o_swa], -1) @ w_out
    rows = min(SWA_WINDOW, l)
    return out, s_final.astype(h.dtype), kh[:, l - rows:], vh[:, l - rows:]


def mixer_sample(h, s0, ck, cv, w_in, wa2, ba, norm_g, sinks, w_out):
    b, t, _ = h.shape
    gq, gk, gv, gr, gg, sq, sk, sv = split_in_proj(h @ w_in)
    q, k, v, log_a = gla_inputs(gq, gk, gv, gg, wa2, ba)
    o, s_new = gla_chunked(q[:, :, None], k[:, :, None], v[:, :, None], log_a[:, :, None], s0)
    o_gla = gla_output(o[:, :, 0], gr, norm_g)
    qh = sq.reshape(b, 1, t, SWA_KV_HEADS, SWA_GROUP, SWA_HEAD_DIM)
    kh = sk.reshape(b, t, SWA_KV_HEADS, SWA_HEAD_DIM)
    vh = sv.reshape(b, t, SWA_KV_HEADS, SWA_HEAD_DIM)
    w = ck.shape[1]
    k_all = jnp.concatenate([ck.astype(kh.dtype), kh], 1)[:, None]
    v_all = jnp.concatenate([cv.astype(vh.dtype), vh], 1)[:, None]
    qpos = (PAST_LEN + jnp.arange(t))[None]
    kpos = jnp.concatenate([PAST_LEN - w + jnp.arange(w), PAST_LEN + jnp.arange(t)])[None]
    o_swa = sink_attention(qh, k_all, v_all, qpos, kpos, sinks).reshape(b, t, SWA_QW).astype(h.dtype)
    out = jnp.concatenate([o_gla, o_swa], -1) @ w_out
    return out, s_new.astype(h.dtype), kh, vh


def mem_kv(mem, wk, wv):
    b, m, _ = mem.shape
    return (mem @ wk).reshape(b, m, MEM_HEADS, MEM_HEAD_DIM), (mem @ wv).reshape(b, m, MEM_HEADS, MEM_HEAD_DIM)


def mem_attention(h, mk, mv, wq, wo):
    b, l, _ = h.shape
    q = (h @ wq).reshape(b, l, MEM_HEADS, MEM_HEAD_DIM).astype(F32)
    s = jnp.einsum('blhd,bmhd->bhlm', q, mk.astype(F32)) * MEM_HEAD_DIM ** -0.5
    p = jax.nn.softmax(s, axis=-1)
    o = jnp.einsum('bhlm,bmhd->blhd', p, mv.astype(F32)).reshape(b, l, D_MODEL).astype(h.dtype)
    return o @ wo


def route(x, w, bias):
    t = x.shape[0]
    scores = jax.nn.sigmoid(x.astype(F32) @ w.astype(F32))
    sel = scores + bias.astype(F32)
    grp = sel.reshape(t, N_GROUPS, N_EXPERTS // N_GROUPS)
    grp_score = jnp.sum(lax.top_k(grp, 2)[0], -1)
    _, gidx = lax.top_k(grp_score, TOPK_GROUPS)
    gmask = jnp.any(gidx[:, :, None] == jnp.arange(N_GROUPS)[None, None, :], axis=1)
    emask = jnp.repeat(gmask, N_EXPERTS // N_GROUPS, axis=1)
    _, idx = lax.top_k(jnp.where(emask, sel, -jnp.inf), TOP_K)
    w_sel = jnp.take_along_axis(scores, idx, 1)
    return idx, w_sel / jnp.sum(w_sel, -1, keepdims=True) * ROUTED_SCALE


def routed_experts(x, idx, gate, wg, wu, wd):
    t, d = x.shape
    a = t * TOP_K
    e_flat = idx.reshape(a)
    order = jnp.argsort(e_flat)
    e_sorted = e_flat[order]
    counts = jnp.zeros((N_EXPERTS,), jnp.int32).at[e_flat].add(1)
    padded = (counts + EXPERT_BLOCK - 1) // EXPERT_BLOCK * EXPERT_BLOCK
    pad_end = jnp.cumsum(padded)
    start = jnp.cumsum(counts) - counts
    dest = (pad_end - padded)[e_sorted] + jnp.arange(a) - start[e_sorted]
    n_blocks = (a + N_EXPERTS * (EXPERT_BLOCK - 1)) // EXPERT_BLOCK + 1
    rows = n_blocks * EXPERT_BLOCK
    row_tok = jnp.full((rows,), t, jnp.int32).at[dest].set((order // TOP_K).astype(jnp.int32))
    row_gate = jnp.zeros((rows,), F32).at[dest].set(gate.reshape(a)[order])
    block_expert = jnp.minimum(jnp.searchsorted(pad_end, jnp.arange(n_blocks) * EXPERT_BLOCK, side='right'), N_EXPERTS - 1)
    x_pad = jnp.concatenate([x, jnp.zeros((1, d), x.dtype)], 0)

    def expert_block(args):
        tok, g, e = args
        return swiglu(x_pad[tok], wg[e], wu[e], wd[e]) * g[:, None].astype(x.dtype)

    out = lax.map(expert_block, (row_tok.reshape(n_blocks, EXPERT_BLOCK), row_gate.reshape(n_blocks, EXPERT_BLOCK), block_expert))
    return jax.ops.segment_sum(out.reshape(rows, d), row_tok, num_segments=t + 1)[:t]


def moe(h, router_w, router_bias, wg, wu, wd, swg, swu, swd):
    b, l, d = h.shape
    x = h.reshape(b * l, d)
    idx, gate = route(x, router_w, router_bias)
    y = routed_experts(x, idx, gate, wg, wu, wd) + swiglu(x, swg, swu, swd)
    return y.reshape(b, l, d)


def setup_inputs(seed: int = 0) -> dict:
    key = jax.random.key(seed)
    ks = iter(jax.random.split(key, 40))
    nrm = lambda shape, scale: jax.random.normal(next(ks), shape, F32) * scale
    swa_rows = min(SWA_WINDOW, PAST_LEN)
    d = D_MODEL
    return {
        'x_prompt': nrm((BATCH, SEQ, d), 1.0),
        'x_sample': nrm((DEC_BATCH, DEC_SEQ, d), 1.0),
        'mem_prompt': nrm((BATCH, MEM_TOKENS, d), 1.0),
        'state_gla': nrm((DEPTH, DEC_BATCH, GLA_HEADS, GLA_DK, GLA_DV), 0.3),
        'cache_swa_k': nrm((DEPTH, DEC_BATCH, swa_rows, SWA_KV_HEADS, SWA_HEAD_DIM), 1.0),
        'cache_swa_v': nrm((DEPTH, DEC_BATCH, swa_rows, SWA_KV_HEADS, SWA_HEAD_DIM), 1.0),
        'cache_mem_k': nrm((DEPTH, DEC_BATCH, MEM_TOKENS, MEM_HEADS, MEM_HEAD_DIM), 1.0),
        'cache_mem_v': nrm((DEPTH, DEC_BATCH, MEM_TOKENS, MEM_HEADS, MEM_HEAD_DIM), 1.0),
        'w_in': nrm((DEPTH, d, IN_COLS), d ** -0.5),
        'gla_w_gate2': nrm((DEPTH, GLA_GATE_RANK, GLA_QK), GLA_GATE_RANK ** -0.5),
        'gla_b_gate': nrm((DEPTH, GLA_QK), 0.1),
        'gla_norm_g': 1.0 + nrm((DEPTH, GLA_DV), 0.02),
        'swa_sinks': nrm((DEPTH, SWA_HEADS), 0.5),
        'w_mix_out': nrm((DEPTH, D_MIX, d), D_MIX ** -0.5 * DEEPNORM_BETA),
        'ln1_g': 1.0 + nrm((DEPTH, d), 0.02),
        'ln1_b': nrm((DEPTH, d), 0.02),
        'mem_wq': nrm((DEPTH, d, d), d ** -0.5),
        'mem_wk': nrm((DEPTH, d, d), d ** -0.5),
        'mem_wv': nrm((DEPTH, d, d), d ** -0.5),
        'mem_wo': nrm((DEPTH, d, d), d ** -0.5 * DEEPNORM_BETA),
        'ln2_g': 1.0 + nrm((DEPTH, d), 0.02),
        'ln2_b': nrm((DEPTH, d), 0.02),
        'router_w': nrm((DEPTH, d, N_EXPERTS), d ** -0.5),
        'router_bias': nrm((DEPTH, N_EXPERTS), 0.01),
        'exp_w_gate': nrm((DEPTH, N_EXPERTS, d, D_EXPERT), d ** -0.5),
        'exp_w_up': nrm((DEPTH, N_EXPERTS, d, D_EXPERT), d ** -0.5),
        'exp_w_down': nrm((DEPTH, N_EXPERTS, D_EXPERT, d), D_EXPERT ** -0.5 * DEEPNORM_BETA),
        'sh_w_gate': nrm((DEPTH, d, D_SHARED), d ** -0.5),
        'sh_w_up': nrm((DEPTH, d, D_SHARED), d ** -0.5),
        'sh_w_down': nrm((DEPTH, D_SHARED, d), D_SHARED ** -0.5 * DEEPNORM_BETA),
        'ln3_g': 1.0 + nrm((DEPTH, d), 0.02),
        'ln3_b': nrm((DEPTH, d), 0.02),
    }


def reference(x_prompt, x_sample, mem_prompt, state_gla, cache_swa_k, cache_swa_v, cache_mem_k, cache_mem_v,
              w_in, gla_w_gate2, gla_b_gate, gla_norm_g, swa_sinks, w_mix_out, ln1_g, ln1_b,
              mem_wq, mem_wk, mem_wv, mem_wo, ln2_g, ln2_b,
              router_w, router_bias, exp_w_gate, exp_w_up, exp_w_down, sh_w_gate, sh_w_up, sh_w_down,
              ln3_g, ln3_b):
    a = DEEPNORM_ALPHA
    hp, hs = x_prompt, x_sample
    gla_p, kp, vp, mkp, mvp, gla_s, ksl, vsl = [], [], [], [], [], [], [], []
    for l in range(DEPTH):
        mix_w = (w_in[l], gla_w_gate2[l], gla_b_gate[l], gla_norm_g[l], swa_sinks[l], w_mix_out[l])
        mix, s_fin, k_rows, v_rows = mixer_prompt(hp, *mix_w)
        hp = layer_norm(a * hp + mix, ln1_g[l], ln1_b[l])
        mix, s_new, k_new, v_new = mixer_sample(hs, state_gla[l], cache_swa_k[l], cache_swa_v[l], *mix_w)
        hs = layer_norm(a * hs + mix, ln1_g[l], ln1_b[l])
        mk, mv = mem_kv(mem_prompt, mem_wk[l], mem_wv[l])
        hp = layer_norm(a * hp + mem_attention(hp, mk, mv, mem_wq[l], mem_wo[l]), ln2_g[l], ln2_b[l])
        hs = layer_norm(a * hs + mem_attention(hs, cache_mem_k[l], cache_mem_v[l], mem_wq[l], mem_wo[l]), ln2_g[l], ln2_b[l])
        moe_w = (router_w[l], router_bias[l], exp_w_gate[l], exp_w_up[l], exp_w_down[l], sh_w_gate[l], sh_w_up[l], sh_w_down[l])
        hp = layer_norm(a * hp + moe(hp, *moe_w), ln3_g[l], ln3_b[l])
        hs = layer_norm(a * hs + moe(hs, *moe_w), ln3_g[l], ln3_b[l])
        gla_p.append(s_fin); kp.append(k_rows); vp.append(v_rows); mkp.append(mk); mvp.append(mv)
        gla_s.append(s_new); ksl.append(k_new); vsl.append(v_new)
    gla_state_prompt = jnp.stack(gla_p, 0)
    swa_k_prompt = jnp.stack(kp, 0)
    swa_v_prompt = jnp.stack(vp, 0)
    mem_k_prompt = jnp.stack(mkp, 0)
    mem_v_prompt = jnp.stack(mvp, 0)
    gla_state_sample = jnp.stack(gla_s, 0)
    swa_k_sample = jnp.stack(ksl, 0)
    swa_v_sample = jnp.stack(vsl, 0)
    return (hp, hs, gla_state_prompt, swa_k_prompt, swa_v_prompt, mem_k_prompt, mem_v_prompt, gla_state_sample, swa_k_sample, swa_v_sample)
```

```python
import functools
import math

import jax
import jax.numpy as jnp
from jax import lax
from jax.experimental import pallas as pl
from jax.experimental.pallas import tpu as pltpu

F32 = jnp.float32
BF16 = jnp.bfloat16
I32 = jnp.int32

CHUNK = 64
PAST_LEN = 2048
GLA_HEADS = 8
GLA_DK = 64
GLA_DV = 128
GLA_QK = GLA_HEADS * GLA_DK
GLA_VW = GLA_HEADS * GLA_DV
GLA_GATE_RANK = 16
GLA_TAU = 16.0
SWA_HEADS = 16
SWA_KV_HEADS = 2
SWA_GROUP = SWA_HEADS // SWA_KV_HEADS
SWA_HEAD_DIM = 64
SWA_QW = SWA_HEADS * SWA_HEAD_DIM
SWA_KVW = SWA_KV_HEADS * SWA_HEAD_DIM
SWA_WINDOW = 128
WINDOW_CHUNKS = SWA_WINDOW // CHUNK
MEM_HEADS = 4
N_EXPERTS = 64
TOP_K = 8
N_GROUPS = 8
GROUP_SIZE = N_EXPERTS // N_GROUPS
TOPK_GROUPS = 4
ROUTED_SCALE = 2.5
DEPTH = 1
DEEPNORM_ALPHA = (2 * DEPTH) ** 0.25
LN_EPS = 1e-5
RMS_EPS = 1e-6

LANES = 128
SUBLANES = 8
VMEM_LIMIT_BYTES = 56 * 1024 * 1024

COL_GQ = 0
COL_GK = COL_GQ + GLA_QK
COL_GV = COL_GK + GLA_QK
COL_GR = COL_GV + GLA_VW
COL_SQ = COL_GR + GLA_VW
COL_SK = COL_SQ + SWA_QW
COL_SV = COL_SK + SWA_KVW
COL_GG = COL_SV + SWA_KVW
IN_COLS_PADDED = COL_GG + LANES

EXPERT_ROWS = 256

NT_DIMS = (((1,), (1,)), ((), ()))
TN_DIMS = (((0,), (0,)), ((), ()))


def _params(n_axes):
    return pltpu.CompilerParams(dimension_semantics=("arbitrary",) * n_axes,
                                vmem_limit_bytes=VMEM_LIMIT_BYTES)


def _sigmoid(x):
    return 1.0 / (1.0 + jnp.exp(-x))


def _silu(x):
    return x * _sigmoid(x)


def _layer_norm(x, g, b):
    mu = jnp.mean(x, axis=-1, keepdims=True)
    xc = x - mu
    var = jnp.mean(xc * xc, axis=-1, keepdims=True)
    return xc * lax.rsqrt(var + LN_EPS) * g + b


def _divisor_tile(n, pref, mult):
    t = min(pref, n)
    while t > mult and (n % t or t % mult):
        t -= mult
    assert n % t == 0 and t % mult == 0, (n, pref, mult)
    return t


def _mm_kernel(x_ref, w_ref, o_ref, xb_ref):
    @pl.when(pl.program_id(1) == 0)
    def _():
        xb_ref[...] = x_ref[...].astype(BF16)

    o_ref[...] = jnp.dot(xb_ref[...], w_ref[...], preferred_element_type=F32).astype(o_ref.dtype)


def _matmul(x, w, tm, tn, out_dtype):
    m, k = x.shape
    n = w.shape[1]
    return pl.pallas_call(
        _mm_kernel,
        grid=(m // tm, n // tn),
        in_specs=[pl.BlockSpec((tm, k), lambda i, j: (i, 0)),
                  pl.BlockSpec((k, tn), lambda i, j: (0, j))],
        out_specs=pl.BlockSpec((tm, tn), lambda i, j: (i, j)),
        out_shape=jax.ShapeDtypeStruct((m, n), out_dtype),
        scratch_shapes=[pltpu.VMEM((tm, k), BF16)],
        compiler_params=_params(2),
        name="matmul",
    )(x, w)


def _mm_res_ln_kernel(x_ref, w_ref, res_ref, g_ref, b_ref, o_ref):
    y = jnp.dot(x_ref[...], w_ref[...], preferred_element_type=F32)
    o_ref[...] = _layer_norm(DEEPNORM_ALPHA * res_ref[...] + y, g_ref[...], b_ref[...])


def _matmul_res_ln(x, w, res, g, b, tm):
    m, k = x.shape
    n = w.shape[1]
    return pl.pallas_call(
        _mm_res_ln_kernel,
        grid=(m // tm,),
        in_specs=[pl.BlockSpec((tm, k), lambda i: (i, 0)),
                  pl.BlockSpec((k, n), lambda i: (0, 0)),
                  pl.BlockSpec((tm, n), lambda i: (i, 0)),
                  pl.BlockSpec((1, n), lambda i: (0, 0)),
                  pl.BlockSpec((1, n), lambda i: (0, 0))],
        out_specs=pl.BlockSpec((tm, n), lambda i: (i, 0)),
        out_shape=jax.ShapeDtypeStruct((m, n), F32),
        compiler_params=_params(1),
        name="matmul_res_ln",
    )(x, w, res, g.reshape(1, n), b.reshape(1, n))


def _gla_kernel(q_ref, k_ref, v_ref, r_ref, gg_ref, wa2_ref, ba_ref, ng_ref, s0_ref, mix_in_ref,
                o_ref, sfin_ref, st_ref, *, rows):
    del mix_in_ref
    c = pl.program_id(1)

    @pl.when(c == 0)
    def _():
        st_ref[...] = s0_ref[0]

    gate = jnp.dot(gg_ref[...], wa2_ref[...], precision=lax.Precision.HIGHEST,
                   preferred_element_type=F32) + ba_ref[...]
    log_a = (jnp.minimum(gate, 0.0) - jnp.log1p(jnp.exp(-jnp.abs(gate)))) / GLA_TAU
    ri = lax.broadcasted_iota(I32, (rows, rows), 0)
    ci = lax.broadcasted_iota(I32, (rows, rows), 1)
    causal = ri >= ci
    bcum = jnp.dot(causal.astype(F32), log_a, precision=lax.Precision.HIGHEST,
                   preferred_element_type=F32)
    b_last = bcum[rows - 1:rows, :]
    q = q_ref[...] * (GLA_DK ** -0.5)
    k = k_ref[...]
    q_in = (q * jnp.exp(bcum)).astype(BF16)
    k_in = (k * jnp.exp(-bcum)).astype(BF16)
    k_out = (k * jnp.exp(b_last - bcum)).astype(BF16)
    decay = jnp.exp(b_last)
    ng = ng_ref[...]
    for h in range(GLA_HEADS):
        ks = slice(h * GLA_DK, (h + 1) * GLA_DK)
        vs = slice(h * GLA_DV, (h + 1) * GLA_DV)
        vh = v_ref[:, vs].astype(BF16)
        st = st_ref[h]
        attn = lax.dot_general(q_in[:, ks], k_in[:, ks], NT_DIMS, preferred_element_type=F32)
        attn = jnp.where(causal, attn, 0.0).astype(BF16)
        o = jnp.dot(attn, vh, preferred_element_type=F32)
        o = o + lax.dot_general(q_in[:, ks], st.astype(BF16), NT_DIMS, preferred_element_type=F32)
        upd_t = lax.dot_general(vh, k_out[:, ks], TN_DIMS, preferred_element_type=F32)
        st_ref[h] = st * decay[:, ks] + upd_t
        o = o * lax.rsqrt(jnp.mean(o * o, axis=-1, keepdims=True) + RMS_EPS) * ng
        o_ref[:, vs] = (o * _silu(r_ref[:, vs])).astype(o_ref.dtype)
    sfin_ref[0] = st_ref[...]


def _gla(z, wa2_pad, ba, norm_g, s0_t, mix_in, *, nb, n_chunks, rows, row0):
    t = z.shape[0]
    rb0 = row0 // rows
    rmap = lambda colblk: (lambda b, c: (rb0 + b * n_chunks + c, colblk))
    in_specs = [
        pl.BlockSpec((rows, GLA_QK), rmap(COL_GQ // GLA_QK)),
        pl.BlockSpec((rows, GLA_QK), rmap(COL_GK // GLA_QK)),
        pl.BlockSpec((rows, GLA_VW), rmap(COL_GV // GLA_VW)),
        pl.BlockSpec((rows, GLA_VW), rmap(COL_GR // GLA_VW)),
        pl.BlockSpec((rows, LANES), rmap(COL_GG // LANES)),
        pl.BlockSpec((LANES, GLA_QK), lambda b, c: (0, 0)),
        pl.BlockSpec((1, GLA_QK), lambda b, c: (0, 0)),
        pl.BlockSpec((1, GLA_DV), lambda b, c: (0, 0)),
        pl.BlockSpec((1, GLA_HEADS, GLA_DV, GLA_DK), lambda b, c: (b, 0, 0, 0)),
    ]
    args = [z, z, z, z, z, wa2_pad, ba.reshape(1, GLA_QK), norm_g.reshape(1, GLA_DV), s0_t]
    aliases = {}
    if mix_in is None:
        mix_in = jnp.zeros((SUBLANES, LANES), BF16)
        in_specs.append(pl.BlockSpec(memory_space=pl.ANY))
    else:
        in_specs.append(pl.BlockSpec(memory_space=pl.ANY))
        aliases = {len(args): 0}
    args.append(mix_in)
    mix, s_fin = pl.pallas_call(
        functools.partial(_gla_kernel, rows=rows),
        grid=(nb, n_chunks),
        in_specs=in_specs,
        out_specs=[pl.BlockSpec((rows, GLA_VW), rmap(0)),
                   pl.BlockSpec((1, GLA_HEADS, GLA_DV, GLA_DK), lambda b, c: (b, 0, 0, 0))],
        out_shape=[jax.ShapeDtypeStruct((t, GLA_VW + SWA_QW), BF16),
                   jax.ShapeDtypeStruct((nb, GLA_HEADS, GLA_DV, GLA_DK), F32)],
        scratch_shapes=[pltpu.VMEM((GLA_HEADS, GLA_DV, GLA_DK), F32)],
        input_output_aliases=aliases,
        compiler_params=_params(2),
        name="gla",
    )(*args)
    return mix, s_fin


def _swa_kernel(sinks_ref, q_ref, kc_ref, vc_ref, kp_ref, vp_ref, mix_in_ref, o_ref, *,
                rows, q_base, q_stride):
    del mix_in_ref
    i = pl.program_id(1)
    q0 = q_base + i * q_stride
    nk = SWA_WINDOW + rows
    keys = jnp.concatenate([kp_ref[...], kc_ref[...]], axis=0).astype(BF16)
    vals = jnp.concatenate([vp_ref[...], vc_ref[...]], axis=0).astype(BF16)
    qpos = q0 + lax.broadcasted_iota(I32, (rows, nk), 0)
    kpos = q0 - SWA_WINDOW + lax.broadcasted_iota(I32, (rows, nk), 1)
    dist = jnp.abs(qpos - kpos).astype(F32)
    qc = lax.shift_right_arithmetic(qpos, CHUNK.bit_length() - 1)
    kc = lax.shift_right_arithmetic(kpos, CHUNK.bit_length() - 1)
    allowed = (kpos >= 0) & (kc <= qc) & (kc >= qc - WINDOW_CHUNKS)
    outs = []
    for h in range(SWA_HEADS):
        g = h // SWA_GROUP
        slope = 2.0 ** (-8.0 * (h + 1) / SWA_HEADS)
        sink = sinks_ref[h]
        kv = slice(g * SWA_HEAD_DIM, (g + 1) * SWA_HEAD_DIM)
        qh = q_ref[:, h * SWA_HEAD_DIM:(h + 1) * SWA_HEAD_DIM].astype(BF16)
        s = lax.dot_general(qh, keys[:, kv], NT_DIMS, preferred_element_type=F32)
        s = s * (SWA_HEAD_DIM ** -0.5) - slope * dist
        s = jnp.where(allowed, s, -jnp.inf)
        m = jnp.maximum(jnp.max(s, axis=-1, keepdims=True), sink)
        p = jnp.exp(s - m)
        denom = jnp.sum(p, axis=-1, keepdims=True) + jnp.exp(sink - m)
        o = jnp.dot(p.astype(BF16), vals[:, kv], preferred_element_type=F32) / denom
        outs.append(o)
    for j in range(SWA_HEADS // 2):
        o_ref[:, j * LANES:(j + 1) * LANES] = jnp.concatenate(
            [outs[2 * j], outs[2 * j + 1]], axis=-1).astype(o_ref.dtype)


def _swa(z, sinks, k_prev, v_prev, prev_map, mix_in, *, nb, n_blocks, rows, row0, q_base, q_stride):
    rb0 = row0 // rows
    rmap = lambda colblk: (lambda b, i: (rb0 + b * n_blocks + i, colblk))
    return pl.pallas_call(
        functools.partial(_swa_kernel, rows=rows, q_base=q_base, q_stride=q_stride),
        grid=(nb, n_blocks),
        in_specs=[pl.BlockSpec(memory_space=pltpu.SMEM),
                  pl.BlockSpec((rows, SWA_QW), rmap(COL_SQ // SWA_QW)),
                  pl.BlockSpec((rows, SWA_KVW), rmap(COL_SK // SWA_KVW)),
                  pl.BlockSpec((rows, SWA_KVW), rmap(COL_SV // SWA_KVW)),
                  pl.BlockSpec((SWA_WINDOW, SWA_KVW), prev_map[0]),
                  pl.BlockSpec((SWA_WINDOW, SWA_KVW), prev_map[1]),
                  pl.BlockSpec(memory_space=pl.ANY)],
        out_specs=pl.BlockSpec((rows, SWA_QW), rmap(GLA_VW // SWA_QW)),
        out_shape=jax.ShapeDtypeStruct(mix_in.shape, mix_in.dtype),
        input_output_aliases={6: 0},
        compiler_params=_params(2),
        name="swa",
    )(sinks, z, z, z, k_prev, v_prev, mix_in)


def _mem_attn_kernel(q_ref, k_ref, v_ref, o_in_ref, o_ref, *, head_dim):
    del o_in_ref
    for h in range(MEM_HEADS):
        hs = slice(h * head_dim, (h + 1) * head_dim)
        kh = k_ref[:, hs].astype(BF16)
        vh = v_ref[:, hs].astype(BF16)
        s = lax.dot_general(q_ref[:, hs], kh, NT_DIMS, preferred_element_type=F32) * (head_dim ** -0.5)
        m = jnp.max(s, axis=-1, keepdims=True)
        p = jnp.exp(s - m)
        denom = jnp.sum(p, axis=-1, keepdims=True)
        o = jnp.dot(p.astype(BF16), vh, preferred_element_type=F32) / denom
        o_ref[:, hs] = o.astype(o_ref.dtype)


def _mem_attn(q, mk, mv, o_in, *, nb, n_tiles, rows, row0, mem_tokens):
    t, d = q.shape
    rb0 = row0 // rows
    qmap = lambda b, i: (rb0 + b * n_tiles + i, 0)
    in_specs = [pl.BlockSpec((rows, d), qmap),
                pl.BlockSpec((mem_tokens, d), lambda b, i: (b, 0)),
                pl.BlockSpec((mem_tokens, d), lambda b, i: (b, 0)),
                pl.BlockSpec(memory_space=pl.ANY)]
    aliases = {}
    if o_in is None:
        o_in = jnp.zeros((SUBLANES, LANES), BF16)
    else:
        aliases = {3: 0}
    return pl.pallas_call(
        functools.partial(_mem_attn_kernel, head_dim=d // MEM_HEADS),
        grid=(nb, n_tiles),
        in_specs=in_specs,
        out_specs=pl.BlockSpec((rows, d), qmap),
        out_shape=jax.ShapeDtypeStruct((t, d), BF16),
        input_output_aliases=aliases,
        compiler_params=_params(2),
        name="mem_attn",
    )(q, mk, mv, o_in)


def _pick_first_max(cur, iota, sentinel):
    mx = jnp.max(cur, axis=0, keepdims=True)
    first = jnp.min(jnp.where(cur == mx, iota, sentinel), axis=0, keepdims=True)
    return iota == first, first


def _router_kernel(x_ref, wh_ref, wl_ref, bias_ref, idx_ref, gate_ref, rank_ref, cnt_ref, run_ref, *, tm):
    i = pl.program_id(0)

    @pl.when(i == 0)
    def _():
        run_ref[...] = jnp.zeros_like(run_ref)

    x = x_ref[...]
    xh = x.astype(BF16)
    xl = (x - xh.astype(F32)).astype(BF16)
    wh = wh_ref[...]
    logits = (lax.dot_general(wh, xh, NT_DIMS, preferred_element_type=F32)
              + lax.dot_general(wh, xl, NT_DIMS, preferred_element_type=F32)
              + lax.dot_general(wl_ref[...], xh, NT_DIMS, preferred_element_type=F32))
    scores = _sigmoid(logits)
    sel = scores + bias_ref[...]
    neg_inf = -jnp.inf

    li = lax.broadcasted_iota(I32, (GROUP_SIZE, tm), 0)
    grp_rows = []
    for g in range(N_GROUPS):
        blk = sel[g * GROUP_SIZE:(g + 1) * GROUP_SIZE, :]
        pick, _ = _pick_first_max(blk, li, GROUP_SIZE)
        m1 = jnp.max(blk, axis=0, keepdims=True)
        m2 = jnp.max(jnp.where(pick, neg_inf, blk), axis=0, keepdims=True)
        grp_rows.append(m1 + m2)
    grp = jnp.concatenate(grp_rows, axis=0)

    gi = lax.broadcasted_iota(I32, (N_GROUPS, tm), 0)
    gsel = jnp.zeros((N_GROUPS, tm), F32)
    cur = grp
    for _ in range(TOPK_GROUPS):
        pick, _ = _pick_first_max(cur, gi, N_GROUPS)
        gsel = jnp.where(pick, 1.0, gsel)
        cur = jnp.where(pick, neg_inf, cur)
    emask = jnp.concatenate(
        [jnp.broadcast_to(gsel[g:g + 1, :], (GROUP_SIZE, tm)) for g in range(N_GROUPS)], axis=0)

    ei = lax.broadcasted_iota(I32, (N_EXPERTS, tm), 0)
    cur = jnp.where(emask > 0.5, sel, neg_inf)
    chosen = jnp.zeros((N_EXPERTS, tm), F32)
    idx_rows, w_rows = [], []
    for _ in range(TOP_K):
        pick, first = _pick_first_max(cur, ei, N_EXPERTS)
        idx_rows.append(first)
        w_rows.append(jnp.sum(jnp.where(pick, scores, 0.0), axis=0, keepdims=True))
        chosen = jnp.where(pick, 1.0, chosen)
        cur = jnp.where(pick, neg_inf, cur)
    idx = jnp.concatenate(idx_rows, axis=0)
    w = jnp.concatenate(w_rows, axis=0)
    gate_ref[...] = w / jnp.sum(w, axis=0, keepdims=True) * ROUTED_SCALE
    idx_ref[...] = idx

    ti = lax.broadcasted_iota(I32, (tm, tm), 0)
    tj = lax.broadcasted_iota(I32, (tm, tm), 1)
    before = jnp.where(ti < tj, 1.0, 0.0).astype(BF16)
    local = jnp.dot(chosen.astype(BF16), before, preferred_element_type=F32)
    total = local + run_ref[:, 0:1]
    rank_rows = [jnp.sum(jnp.where(ei == idx_rows[k], total, 0.0), axis=0, keepdims=True)
                 for k in range(TOP_K)]
    rank_ref[...] = jnp.concatenate(rank_rows, axis=0).astype(I32)
    run_ref[...] = run_ref[...] + jnp.sum(chosen, axis=1, keepdims=True)
    cnt_ref[...] = run_ref[...].astype(I32)


def _router(x, w_hi_t, w_lo_t, bias, tm):
    t, d = x.shape
    return pl.pallas_call(
        functools.partial(_router_kernel, tm=tm),
        grid=(t // tm,),
        in_specs=[pl.BlockSpec((tm, d), lambda i: (i, 0)),
                  pl.BlockSpec((N_EXPERTS, d), lambda i: (0, 0)),
                  pl.BlockSpec((N_EXPERTS, d), lambda i: (0, 0)),
                  pl.BlockSpec((N_EXPERTS, 1), lambda i: (0, 0))],
        out_specs=[pl.BlockSpec((TOP_K, tm), lambda i: (0, i)),
                   pl.BlockSpec((TOP_K, tm), lambda i: (0, i)),
                   pl.BlockSpec((TOP_K, tm), lambda i: (0, i)),
                   pl.BlockSpec((N_EXPERTS, LANES), lambda i: (0, 0))],
        out_shape=[jax.ShapeDtypeStruct((TOP_K, t), I32),
                   jax.ShapeDtypeStruct((TOP_K, t), F32),
                   jax.ShapeDtypeStruct((TOP_K, t), I32),
                   jax.ShapeDtypeStruct((N_EXPERTS, LANES), I32)],
        scratch_shapes=[pltpu.VMEM((N_EXPERTS, LANES), F32)],
        compiler_params=_params(1),
        name="router",
    )(x, w_hi_t, w_lo_t, bias.reshape(N_EXPERTS, 1))


PAD_BITS = tuple(1 << s for s in reversed(range(SUBLANES.bit_length() - 1, EXPERT_ROWS.bit_length() - 1)))


def _dispatch_kernel(zstart_ref, zcount_ref, dest_ref, x_ref, xs_ref, zero_ref, sem, zsem, *, tm):
    i = pl.program_id(0)

    def row_copy(t, k):
        d = dest_ref[k * tm + t]
        return pltpu.make_async_copy(x_ref.at[pl.ds(t, 1), :], xs_ref.at[pl.ds(d, 1), :], sem)

    def start_row(t, carry):
        for k in range(TOP_K):
            row_copy(t, k).start()
        return carry

    def wait_row(t, carry):
        for k in range(TOP_K):
            row_copy(t, k).wait()
        return carry

    lax.fori_loop(0, tm, start_row, 0)
    lax.fori_loop(0, tm, wait_row, 0)

    @pl.when(i == pl.num_programs(0) - 1)
    def _():
        zero_ref[...] = jnp.zeros_like(zero_ref)

        def fill(e, wait):
            def fill_rows(pos, n, pred):
                @pl.when(pred)
                def _():
                    cp = pltpu.make_async_copy(zero_ref.at[pl.ds(0, n), :],
                                               xs_ref.at[pl.ds(pos, n), :], zsem)
                    if wait:
                        cp.wait()
                    else:
                        cp.start()

            pos = zstart_ref[e]
            cnt = zcount_ref[e]
            head = jnp.minimum((-pos) & (SUBLANES - 1), cnt)
            for j in range(SUBLANES - 1):
                fill_rows(pos + j, 1, j < head)
            pos = pos + head
            cnt = cnt - head
            for bit in PAD_BITS:
                has = (cnt & bit) != 0
                fill_rows(pl.multiple_of(pos, SUBLANES), bit, has)
                pos = pos + jnp.where(has, bit, 0)

        def fill_start(e, carry):
            fill(e, False)
            return carry

        def fill_wait(e, carry):
            fill(e, True)
            return carry

        lax.fori_loop(0, N_EXPERTS, fill_start, 0)
        lax.fori_loop(0, N_EXPERTS, fill_wait, 0)


def _dispatch(x, dest_tiles, zstart, zcount, n_rows, tm):
    t, d = x.shape
    return pl.pallas_call(
        functools.partial(_dispatch_kernel, tm=tm),
        grid_spec=pltpu.PrefetchScalarGridSpec(
            num_scalar_prefetch=2,
            grid=(t // tm,),
            in_specs=[pl.BlockSpec((TOP_K * tm,), lambda i, zs, zc: (i,), memory_space=pltpu.SMEM),
                      pl.BlockSpec((tm, d), lambda i, zs, zc: (i, 0))],
            out_specs=pl.BlockSpec(memory_space=pl.ANY),
            scratch_shapes=[pltpu.VMEM((EXPERT_ROWS // 2, d), F32),
                            pltpu.SemaphoreType.DMA(()),
                            pltpu.SemaphoreType.DMA(())]),
        out_shape=jax.ShapeDtypeStruct((n_rows, d), F32),
        compiler_params=_params(1),
        name="moe_dispatch",
    )(zstart, zcount, dest_tiles, x)


def _expert_kernel(be_ref, nv_ref, x_ref, wg_ref, wu_ref, wd_ref, y_ref, wgb_ref, wub_ref, wdb_ref):
    b = pl.program_id(0)
    prev = be_ref[jnp.maximum(b - 1, 0)]
    fresh = jnp.logical_or(b == 0, be_ref[b] != prev)

    @pl.when(fresh)
    def _():
        wgb_ref[...] = wg_ref[0].astype(BF16)
        wub_ref[...] = wu_ref[0].astype(BF16)
        wdb_ref[...] = wd_ref[0].astype(BF16)

    @pl.when(b < nv_ref[0])
    def _():
        xb = x_ref[...].astype(BF16)
        hg = jnp.dot(xb, wgb_ref[...], preferred_element_type=F32)
        hu = jnp.dot(xb, wub_ref[...], preferred_element_type=F32)
        act = (_silu(hg) * hu).astype(BF16)
        y_ref[...] = jnp.dot(act, wdb_ref[...], preferred_element_type=F32)


def _experts(xs, wg, wu, wd, block_expert, n_valid):
    n_rows, d = xs.shape
    de = wg.shape[2]
    nb = n_rows // EXPERT_ROWS
    xmap = lambda b, be, nv: (jnp.minimum(b, nv[0] - 1), 0)
    return pl.pallas_call(
        _expert_kernel,
        grid_spec=pltpu.PrefetchScalarGridSpec(
            num_scalar_prefetch=2,
            grid=(nb,),
            in_specs=[pl.BlockSpec((EXPERT_ROWS, d), xmap),
                      pl.BlockSpec((1, d, de), lambda b, be, nv: (be[b], 0, 0)),
                      pl.BlockSpec((1, d, de), lambda b, be, nv: (be[b], 0, 0)),
                      pl.BlockSpec((1, de, d), lambda b, be, nv: (be[b], 0, 0))],
            out_specs=pl.BlockSpec((EXPERT_ROWS, d), xmap),
            scratch_shapes=[pltpu.VMEM((d, de), BF16), pltpu.VMEM((d, de), BF16),
                            pltpu.VMEM((de, d), BF16)]),
        out_shape=jax.ShapeDtypeStruct((n_rows, d), F32),
        compiler_params=_params(1),
        name="moe_experts",
    )(block_expert, n_valid, xs, wg, wu, wd)


def _combine_kernel(dest_ref, x_ref, gate_ref, ys_ref, swg_ref, swu_ref, swd_ref, g_ref, b_ref,
                    o_ref, buf_ref, sem, *, tm):
    def row_copy(t, k):
        d = dest_ref[k * tm + t]
        return pltpu.make_async_copy(ys_ref.at[pl.ds(d, 1), :], buf_ref.at[k, pl.ds(t, 1), :], sem)

    def start_row(t, carry):
        for k in range(TOP_K):
            row_copy(t, k).start()
        return carry

    def wait_row(t, carry):
        for k in range(TOP_K):
            row_copy(t, k).wait()
        return carry

    lax.fori_loop(0, tm, start_row, 0)
    x = x_ref[...]
    xb = x.astype(BF16)
    act = (_silu(jnp.dot(xb, swg_ref[...], preferred_element_type=F32))
           * jnp.dot(xb, swu_ref[...], preferred_element_type=F32)).astype(BF16)
    acc = jnp.dot(act, swd_ref[...], preferred_element_type=F32)
    lax.fori_loop(0, tm, wait_row, 0)
    gate = gate_ref[...]
    for k in range(TOP_K):
        acc = acc + gate[:, k:k + 1] * buf_ref[k]
    o_ref[...] = _layer_norm(DEEPNORM_ALPHA * x + acc, g_ref[...], b_ref[...])


def _combine(x, gate_t, dest_tiles, ys, swg, swu, swd, g, b, tm):
    t, d = x.shape
    ds = swg.shape[1]
    const = lambda i: (0, 0)
    return pl.pallas_call(
        functools.partial(_combine_kernel, tm=tm),
        grid=(t // tm,),
        in_specs=[pl.BlockSpec((TOP_K * tm,), lambda i: (i,), memory_space=pltpu.SMEM),
                  pl.BlockSpec((tm, d), lambda i: (i, 0)),
                  pl.BlockSpec((tm, TOP_K), lambda i: (i, 0)),
                  pl.BlockSpec(memory_space=pl.ANY),
                  pl.BlockSpec((d, ds), const),
                  pl.BlockSpec((d, ds), const),
                  pl.BlockSpec((ds, d), const),
                  pl.BlockSpec((1, d), const),
                  pl.BlockSpec((1, d), const)],
        out_specs=pl.BlockSpec((tm, d), lambda i: (i, 0)),
        out_shape=jax.ShapeDtypeStruct((t, d), F32),
        scratch_shapes=[pltpu.VMEM((TOP_K, tm, d), F32), pltpu.SemaphoreType.DMA(())],
        compiler_params=_params(1),
        name="moe_combine",
    )(dest_tiles, x, gate_t, ys, swg, swu, swd, g.reshape(1, d), b.reshape(1, d))


def _tile_major(a, tm):
    k, t = a.shape
    return a.reshape(k, t // tm, tm).transpose(1, 0, 2).reshape(-1)


def _moe_ln(x, router_w, router_bias, wg, wu, wd, swg, swu, swd, g, b, tm):
    t, d = x.shape
    w_t = router_w.T
    w_hi = w_t.astype(BF16)
    w_lo = (w_t - w_hi.astype(F32)).astype(BF16)
    tr = _divisor_tile(t, 512, LANES)
    idx, gate, rank, cnt = _router(x, w_hi, w_lo, router_bias, tr)

    counts = cnt[:, 0]
    padded = (counts + EXPERT_ROWS - 1) // EXPERT_ROWS * EXPERT_ROWS
    pad_end = jnp.cumsum(padded)
    pad_start = pad_end - padded
    dest = pad_start[idx] + rank
    n_blocks = (t * TOP_K + N_EXPERTS * (EXPERT_ROWS - 1)) // EXPERT_ROWS + 1
    n_valid = (pad_end[-1] // EXPERT_ROWS).astype(I32)
    blk = jnp.minimum(jnp.arange(n_blocks, dtype=I32), n_valid - 1)
    block_expert = jnp.minimum(
        jnp.searchsorted(pad_end, blk * EXPERT_ROWS, side="right"), N_EXPERTS - 1).astype(I32)

    td = _divisor_tile(t, 256, SUBLANES)
    xs = _dispatch(x, _tile_major(dest, td), (pad_start + counts).astype(I32),
                   (padded - counts).astype(I32), n_blocks * EXPERT_ROWS, td)
    ys = _experts(xs, wg, wu, wd, block_expert, n_valid.reshape(1))
    tc = _divisor_tile(t, 128, SUBLANES)
    return _combine(x, gate.T, _tile_major(dest, tc), ys, swg.astype(BF16), swu.astype(BF16),
                    swd.astype(BF16), g, b, tc)


def kernel(x_prompt, x_sample, mem_prompt, state_gla, cache_swa_k, cache_swa_v, cache_mem_k, cache_mem_v,
           w_in, gla_w_gate2, gla_b_gate, gla_norm_g, swa_sinks, w_mix_out, ln1_g, ln1_b,
           mem_wq, mem_wk, mem_wv, mem_wo, ln2_g, ln2_b,
           router_w, router_bias, exp_w_gate, exp_w_up, exp_w_down, sh_w_gate, sh_w_up, sh_w_down,
           ln3_g, ln3_b):
    nbp, seq, d = x_prompt.shape
    nbs, dec, _ = x_sample.shape
    tp, ts = nbp * seq, nbs * dec
    t = tp + ts
    mem_tokens = mem_prompt.shape[1]
    assert state_gla.shape[0] == DEPTH and seq % SWA_WINDOW == 0 and dec % SUBLANES == 0
    assert cache_swa_k.shape[2] == SWA_WINDOW and tp % dec == 0
    tm = _divisor_tile(math.gcd(tp, ts), 512, 16)
    l = 0

    x = jnp.concatenate([x_prompt.reshape(tp, d), x_sample.reshape(ts, d)], axis=0)

    wi = w_in[l]
    o = 0
    seg = []
    for size in (GLA_QK, GLA_QK, GLA_VW, GLA_VW, GLA_GATE_RANK, SWA_QW, SWA_KVW, SWA_KVW):
        seg.append(wi[:, o:o + size])
        o += size
    gq, gk, gv, gr, gg, sq, sk, sv = seg
    w_in_r = jnp.concatenate(
        [gq, gk, gv, gr, sq, sk, sv, gg, jnp.zeros((d, LANES - GLA_GATE_RANK), wi.dtype)], axis=1).astype(BF16)
    z = _matmul(x, w_in_r, tm, _divisor_tile(IN_COLS_PADDED, 640, LANES), F32)

    wa2_pad = jnp.concatenate(
        [gla_w_gate2[l], jnp.zeros((LANES - GLA_GATE_RANK, GLA_QK), F32)], axis=0)
    zero_state = jnp.zeros((nbp, GLA_HEADS, GLA_DV, GLA_DK), F32)
    mix, st_p = _gla(z, wa2_pad, gla_b_gate[l], gla_norm_g[l], zero_state, None,
                     nb=nbp, n_chunks=seq // CHUNK, rows=CHUNK, row0=0)
    mix, st_s = _gla(z, wa2_pad, gla_b_gate[l], gla_norm_g[l], jnp.swapaxes(state_gla[l], -1, -2), mix,
                     nb=nbs, n_chunks=1, rows=dec, row0=tp)
    nblk = seq // SWA_WINDOW
    prev_p = lambda col: (lambda b, i: (b * nblk + jnp.maximum(i - 1, 0), col))
    mix = _swa(z, swa_sinks[l], z, z, (prev_p(COL_SK // SWA_KVW), prev_p(COL_SV // SWA_KVW)), mix,
               nb=nbp, n_blocks=nblk, rows=SWA_WINDOW, row0=0, q_base=0, q_stride=SWA_WINDOW)
    ck = cache_swa_k[l].reshape(nbs * SWA_WINDOW, SWA_KVW)
    cv = cache_swa_v[l].reshape(nbs * SWA_WINDOW, SWA_KVW)
    prev_s = lambda b, i: (b, 0)
    mix = _swa(z, swa_sinks[l], ck, cv, (prev_s, prev_s), mix,
               nb=nbs, n_blocks=1, rows=dec, row0=tp, q_base=PAST_LEN, q_stride=0)
    h1 = _matmul_res_ln(mix, w_mix_out[l].astype(BF16), x, ln1_g[l], ln1_b[l], tm)

    mem = mem_prompt.reshape(nbp * mem_tokens, d)
    tmem = _divisor_tile(nbp * mem_tokens, 512, 16)
    mk = _matmul(mem, mem_wk[l].astype(BF16), tmem, _divisor_tile(d, 1024, LANES), F32)
    mv = _matmul(mem, mem_wv[l].astype(BF16), tmem, _divisor_tile(d, 1024, LANES), F32)
    q = _matmul(h1, mem_wq[l].astype(BF16), tm, _divisor_tile(d, 1024, LANES), BF16)
    tq = _divisor_tile(seq, 512, 16)
    att = _mem_attn(q, mk, mv, None, nb=nbp, n_tiles=seq // tq, rows=tq, row0=0, mem_tokens=mem_tokens)
    att = _mem_attn(q, cache_mem_k[l].reshape(nbs * mem_tokens, d), cache_mem_v[l].reshape(nbs * mem_tokens, d),
                    att, nb=nbs, n_tiles=1, rows=dec, row0=tp, mem_tokens=mem_tokens)
    h2 = _matmul_res_ln(att, mem_wo[l].astype(BF16), h1, ln2_g[l], ln2_b[l], tm)

    out = _moe_ln(h2, router_w[l], router_bias[l], exp_w_gate[l], exp_w_up[l], exp_w_down[l],
                  sh_w_gate[l], sh_w_up[l], sh_w_down[l], ln3_g[l], ln3_b[l], tm)

    y_prompt = out[:tp].reshape(nbp, seq, d)
    y_sample = out[tp:].reshape(nbs, dec, d)
    zp = z[:tp].reshape(nbp, seq, IN_COLS_PADDED)[:, seq - SWA_WINDOW:]
    zs = z[tp:].reshape(nbs, dec, IN_COLS_PADDED)
    kv_shape = lambda a: a.reshape(a.shape[0], a.shape[1], SWA_KV_HEADS, SWA_HEAD_DIM)[None]
    mem_shape = lambda a: a.reshape(nbp, mem_tokens, MEM_HEADS, d // MEM_HEADS)[None]
    return (y_prompt, y_sample,
            jnp.swapaxes(st_p, -1, -2)[None],
            kv_shape(zp[..., COL_SK:COL_SK + SWA_KVW]), kv_shape(zp[..., COL_SV:COL_SV + SWA_KVW]),
            mem_shape(mk), mem_shape(mv),
            jnp.swapaxes(st_s, -1, -2)[None],
            kv_shape(zs[..., COL_SK:COL_SK + SWA_KVW]), kv_shape(zs[..., COL_SV:COL_SV + SWA_KVW]))
```

```python
import functools
import math

import jax
import jax.numpy as jnp
from jax import lax
from jax.experimental import pallas as pl
from jax.experimental.pallas import tpu as pltpu

F32 = jnp.float32
BF16 = jnp.bfloat16
I32 = jnp.int32

CHUNK = 64
PAST_LEN = 2048
GLA_HEADS = 8
GLA_DK = 64
GLA_DV = 128
GLA_QK = GLA_HEADS * GLA_DK
GLA_VW = GLA_HEADS * GLA_DV
GLA_GATE_RANK = 16
GLA_TAU = 16.0
SWA_HEADS = 16
SWA_KV_HEADS = 2
SWA_GROUP = SWA_HEADS // SWA_KV_HEADS
SWA_HEAD_DIM = 64
SWA_QW = SWA_HEADS * SWA_HEAD_DIM
SWA_KVW = SWA_KV_HEADS * SWA_HEAD_DIM
SWA_WINDOW = 128
WINDOW_CHUNKS = SWA_WINDOW // CHUNK
MEM_HEADS = 4
N_EXPERTS = 64
TOP_K = 8
N_GROUPS = 8
GROUP_SIZE = N_EXPERTS // N_GROUPS
TOPK_GROUPS = 4
ROUTED_SCALE = 2.5
DEPTH = 1
DEEPNORM_ALPHA = (2 * DEPTH) ** 0.25
LN_EPS = 1e-5
RMS_EPS = 1e-6

LANES = 128
SUBLANES = 8
VMEM_LIMIT_BYTES = 56 * 1024 * 1024

COL_GQ = 0
COL_GK = COL_GQ + GLA_QK
COL_GV = COL_GK + GLA_QK
COL_GR = COL_GV + GLA_VW
COL_SQ = COL_GR + GLA_VW
COL_SK = COL_SQ + SWA_QW
COL_SV = COL_SK + SWA_KVW
COL_GG = COL_SV + SWA_KVW
IN_COLS_PADDED = COL_GG + LANES

EXPERT_ROWS = 256
ISSUE_UNROLL = 8

NT_DIMS = (((1,), (1,)), ((), ()))
TN_DIMS = (((0,), (0,)), ((), ()))


def _params(n_axes):
    return pltpu.CompilerParams(dimension_semantics=("arbitrary",) * n_axes,
                                vmem_limit_bytes=VMEM_LIMIT_BYTES)


def _sigmoid(x):
    return 1.0 / (1.0 + jnp.exp(-x))


def _silu(x):
    return x * _sigmoid(x)


def _layer_norm(x, g, b):
    mu = jnp.mean(x, axis=-1, keepdims=True)
    xc = x - mu
    var = jnp.mean(xc * xc, axis=-1, keepdims=True)
    return xc * lax.rsqrt(var + LN_EPS) * g + b


HI_MASK = -65536


def _round_to_bf16_bits(x):
    b = lax.bitcast_convert_type(x, I32)
    return b + (0x7FFF + (lax.shift_right_logical(b, 16) & 1))


def _pack_halves(x):
    c = x.shape[1] // 2
    lo = lax.shift_right_logical(_round_to_bf16_bits(x[:, :c]), 16)
    hi = _round_to_bf16_bits(x[:, c:]) & HI_MASK
    return lo | hi


def _unpack_halves(w):
    return (lax.bitcast_convert_type(lax.shift_left(w, 16), F32),
            lax.bitcast_convert_type(w & HI_MASK, F32))


def _divisor_tile(n, pref, mult):
    t = min(pref, n)
    while t > mult and (n % t or t % mult):
        t -= mult
    assert n % t == 0 and t % mult == 0, (n, pref, mult)
    return t


def _mm_kernel(x_ref, w_ref, o_ref, xb_ref):
    @pl.when(pl.program_id(1) == 0)
    def _():
        xb_ref[...] = x_ref[...].astype(BF16)

    o_ref[...] = jnp.dot(xb_ref[...], w_ref[...], preferred_element_type=F32).astype(o_ref.dtype)


def _matmul(x, w, tm, tn, out_dtype):
    m, k = x.shape
    n = w.shape[1]
    return pl.pallas_call(
        _mm_kernel,
        grid=(m // tm, n // tn),
        in_specs=[pl.BlockSpec((tm, k), lambda i, j: (i, 0)),
                  pl.BlockSpec((k, tn), lambda i, j: (0, j))],
        out_specs=pl.BlockSpec((tm, tn), lambda i, j: (i, j)),
        out_shape=jax.ShapeDtypeStruct((m, n), out_dtype),
        scratch_shapes=[pltpu.VMEM((tm, k), BF16)],
        compiler_params=_params(2),
        name="matmul",
    )(x, w)


def _mm_res_ln_kernel(x_ref, w_ref, res_ref, g_ref, b_ref, o_ref, *packed_ref):
    y = jnp.dot(x_ref[...], w_ref[...], preferred_element_type=F32)
    h = _layer_norm(DEEPNORM_ALPHA * res_ref[...] + y, g_ref[...], b_ref[...])
    o_ref[...] = h
    if packed_ref:
        packed_ref[0][...] = _pack_halves(h)


def _matmul_res_ln(x, w, res, g, b, tm, emit_packed=False):
    m, k = x.shape
    n = w.shape[1]
    out_specs = [pl.BlockSpec((tm, n), lambda i: (i, 0))]
    out_shape = [jax.ShapeDtypeStruct((m, n), F32)]
    if emit_packed:
        out_specs.append(pl.BlockSpec((tm, n // 2), lambda i: (i, 0)))
        out_shape.append(jax.ShapeDtypeStruct((m, n // 2), I32))
    outs = pl.pallas_call(
        _mm_res_ln_kernel,
        grid=(m // tm,),
        in_specs=[pl.BlockSpec((tm, k), lambda i: (i, 0)),
                  pl.BlockSpec((k, n), lambda i: (0, 0)),
                  pl.BlockSpec((tm, n), lambda i: (i, 0)),
                  pl.BlockSpec((1, n), lambda i: (0, 0)),
                  pl.BlockSpec((1, n), lambda i: (0, 0))],
        out_specs=out_specs,
        out_shape=out_shape,
        compiler_params=_params(1),
        name="matmul_res_ln",
    )(x, w, res, g.reshape(1, n), b.reshape(1, n))
    return outs if emit_packed else outs[0]


def _gla_kernel(q_ref, k_ref, v_ref, r_ref, gg_ref, wa2_ref, ba_ref, ng_ref, s0_ref, mix_in_ref,
                o_ref, sfin_ref, st_ref, *, rows):
    del mix_in_ref
    c = pl.program_id(1)

    @pl.when(c == 0)
    def _():
        st_ref[...] = s0_ref[0]

    gate = jnp.dot(gg_ref[...], wa2_ref[...], precision=lax.Precision.HIGHEST,
                   preferred_element_type=F32) + ba_ref[...]
    log_a = (jnp.minimum(gate, 0.0) - jnp.log1p(jnp.exp(-jnp.abs(gate)))) / GLA_TAU
    ri = lax.broadcasted_iota(I32, (rows, rows), 0)
    ci = lax.broadcasted_iota(I32, (rows, rows), 1)
    causal = ri >= ci
    bcum = jnp.dot(causal.astype(F32), log_a, precision=lax.Precision.HIGHEST,
                   preferred_element_type=F32)
    b_last = bcum[rows - 1:rows, :]
    q = q_ref[...] * (GLA_DK ** -0.5)
    k = k_ref[...]
    q_in = (q * jnp.exp(bcum)).astype(BF16)
    k_in = (k * jnp.exp(-bcum)).astype(BF16)
    k_out = (k * jnp.exp(b_last - bcum)).astype(BF16)
    decay = jnp.exp(b_last)
    ng = ng_ref[...]
    for h in range(GLA_HEADS):
        ks = slice(h * GLA_DK, (h + 1) * GLA_DK)
        vs = slice(h * GLA_DV, (h + 1) * GLA_DV)
        vh = v_ref[:, vs].astype(BF16)
        st = st_ref[h]
        attn = lax.dot_general(q_in[:, ks], k_in[:, ks], NT_DIMS, preferred_element_type=F32)
        attn = jnp.where(causal, attn, 0.0).astype(BF16)
        o = jnp.dot(attn, vh, preferred_element_type=F32)
        o = o + lax.dot_general(q_in[:, ks], st.astype(BF16), NT_DIMS, preferred_element_type=F32)
        upd_t = lax.dot_general(vh, k_out[:, ks], TN_DIMS, preferred_element_type=F32)
        st_ref[h] = st * decay[:, ks] + upd_t
        o = o * lax.rsqrt(jnp.mean(o * o, axis=-1, keepdims=True) + RMS_EPS) * ng
        o_ref[:, vs] = (o * _silu(r_ref[:, vs])).astype(o_ref.dtype)
    sfin_ref[0] = st_ref[...]


def _gla(z, wa2_pad, ba, norm_g, s0_t, mix_in, *, nb, n_chunks, rows, row0):
    t = z.shape[0]
    rb0 = row0 // rows
    rmap = lambda colblk: (lambda b, c: (rb0 + b * n_chunks + c, colblk))
    in_specs = [
        pl.BlockSpec((rows, GLA_QK), rmap(COL_GQ // GLA_QK)),
        pl.BlockSpec((rows, GLA_QK), rmap(COL_GK // GLA_QK)),
        pl.BlockSpec((rows, GLA_VW), rmap(COL_GV // GLA_VW)),
        pl.BlockSpec((rows, GLA_VW), rmap(COL_GR // GLA_VW)),
        pl.BlockSpec((rows, LANES), rmap(COL_GG // LANES)),
        pl.BlockSpec((LANES, GLA_QK), lambda b, c: (0, 0)),
        pl.BlockSpec((1, GLA_QK), lambda b, c: (0, 0)),
        pl.BlockSpec((1, GLA_DV), lambda b, c: (0, 0)),
        pl.BlockSpec((1, GLA_HEADS, GLA_DV, GLA_DK), lambda b, c: (b, 0, 0, 0)),
    ]
    args = [z, z, z, z, z, wa2_pad, ba.reshape(1, GLA_QK), norm_g.reshape(1, GLA_DV), s0_t]
    aliases = {}
    if mix_in is None:
        mix_in = jnp.zeros((SUBLANES, LANES), BF16)
        in_specs.append(pl.BlockSpec(memory_space=pl.ANY))
    else:
        in_specs.append(pl.BlockSpec(memory_space=pl.ANY))
        aliases = {len(args): 0}
    args.append(mix_in)
    mix, s_fin = pl.pallas_call(
        functools.partial(_gla_kernel, rows=rows),
        grid=(nb, n_chunks),
        in_specs=in_specs,
        out_specs=[pl.BlockSpec((rows, GLA_VW), rmap(0)),
                   pl.BlockSpec((1, GLA_HEADS, GLA_DV, GLA_DK), lambda b, c: (b, 0, 0, 0))],
        out_shape=[jax.ShapeDtypeStruct((t, GLA_VW + SWA_QW), BF16),
                   jax.ShapeDtypeStruct((nb, GLA_HEADS, GLA_DV, GLA_DK), F32)],
        scratch_shapes=[pltpu.VMEM((GLA_HEADS, GLA_DV, GLA_DK), F32)],
        input_output_aliases=aliases,
        compiler_params=_params(2),
        name="gla",
    )(*args)
    return mix, s_fin


def _swa_kernel(sinks_ref, q_ref, kc_ref, vc_ref, kp_ref, vp_ref, mix_in_ref, o_ref, *,
                rows, q_base, q_stride):
    del mix_in_ref
    i = pl.program_id(1)
    q0 = q_base + i * q_stride
    nk = SWA_WINDOW + rows
    keys = jnp.concatenate([kp_ref[...], kc_ref[...]], axis=0).astype(BF16)
    vals = jnp.concatenate([vp_ref[...], vc_ref[...]], axis=0).astype(BF16)
    sr = SWA_GROUP * rows
    row = lax.broadcasted_iota(I32, (sr, nk), 0)
    qpos = q0 + (row & (rows - 1))
    kpos = q0 - SWA_WINDOW + lax.broadcasted_iota(I32, (sr, nk), 1)
    dist = jnp.abs(qpos - kpos).astype(F32)
    qc = lax.shift_right_arithmetic(qpos, CHUNK.bit_length() - 1)
    kc = lax.shift_right_arithmetic(kpos, CHUNK.bit_length() - 1)
    allowed = (kpos >= 0) & (kc <= qc) & (kc >= qc - WINDOW_CHUNKS)
    assert rows & (rows - 1) == 0
    head_in_group = lax.shift_right_logical(lax.broadcasted_iota(I32, (sr, 1), 0), rows.bit_length() - 1)
    for g in range(SWA_KV_HEADS):
        slope = jnp.zeros((sr, 1), F32)
        sink = jnp.zeros((sr, 1), F32)
        for j in range(SWA_GROUP):
            h = g * SWA_GROUP + j
            slope = jnp.where(head_in_group == j, 2.0 ** (-8.0 * (h + 1) / SWA_HEADS), slope)
            sink = jnp.where(head_in_group == j, sinks_ref[h], sink)
        kv = slice(g * SWA_HEAD_DIM, (g + 1) * SWA_HEAD_DIM)
        qg = jnp.concatenate(
            [q_ref[:, (g * SWA_GROUP + j) * SWA_HEAD_DIM:(g * SWA_GROUP + j + 1) * SWA_HEAD_DIM]
             for j in range(SWA_GROUP)], axis=0).astype(BF16)
        s = lax.dot_general(qg, keys[:, kv], NT_DIMS, preferred_element_type=F32)
        s = s * (SWA_HEAD_DIM ** -0.5) - slope * dist
        s = jnp.where(allowed, s, -jnp.inf)
        m = jnp.maximum(jnp.max(s, axis=-1, keepdims=True), sink)
        p = jnp.exp(s - m)
        denom = jnp.sum(p, axis=-1, keepdims=True) + jnp.exp(sink - m)
        o = jnp.dot(p.astype(BF16), vals[:, kv], preferred_element_type=F32) / denom
        for j in range(0, SWA_GROUP, 2):
            c0 = (g * SWA_GROUP + j) * SWA_HEAD_DIM
            o_ref[:, c0:c0 + LANES] = jnp.concatenate(
                [o[j * rows:(j + 1) * rows], o[(j + 1) * rows:(j + 2) * rows]], axis=-1).astype(o_ref.dtype)


def _swa(z, sinks, k_prev, v_prev, prev_map, mix_in, *, nb, n_blocks, rows, row0, q_base, q_stride):
    rb0 = row0 // rows
    rmap = lambda colblk: (lambda b, i: (rb0 + b * n_blocks + i, colblk))
    return pl.pallas_call(
        functools.partial(_swa_kernel, rows=rows, q_base=q_base, q_stride=q_stride),
        grid=(nb, n_blocks),
        in_specs=[pl.BlockSpec(memory_space=pltpu.SMEM),
                  pl.BlockSpec((rows, SWA_QW), rmap(COL_SQ // SWA_QW)),
                  pl.BlockSpec((rows, SWA_KVW), rmap(COL_SK // SWA_KVW)),
                  pl.BlockSpec((rows, SWA_KVW), rmap(COL_SV // SWA_KVW)),
                  pl.BlockSpec((SWA_WINDOW, SWA_KVW), prev_map[0]),
                  pl.BlockSpec((SWA_WINDOW, SWA_KVW), prev_map[1]),
                  pl.BlockSpec(memory_space=pl.ANY)],
        out_specs=pl.BlockSpec((rows, SWA_QW), rmap(GLA_VW // SWA_QW)),
        out_shape=jax.ShapeDtypeStruct(mix_in.shape, mix_in.dtype),
        input_output_aliases={6: 0},
        compiler_params=_params(2),
        name="swa",
    )(sinks, z, z, z, k_prev, v_prev, mix_in)


def _mem_attn_kernel(q_ref, k_ref, v_ref, o_in_ref, o_ref, *, head_dim):
    del o_in_ref
    for h in range(MEM_HEADS):
        hs = slice(h * head_dim, (h + 1) * head_dim)
        kh = k_ref[:, hs].astype(BF16)
        vh = v_ref[:, hs].astype(BF16)
        s = lax.dot_general(q_ref[:, hs], kh, NT_DIMS, preferred_element_type=F32) * (head_dim ** -0.5)
        m = jnp.max(s, axis=-1, keepdims=True)
        p = jnp.exp(s - m)
        denom = jnp.sum(p, axis=-1, keepdims=True)
        o = jnp.dot(p.astype(BF16), vh, preferred_element_type=F32) / denom
        o_ref[:, hs] = o.astype(o_ref.dtype)


def _mem_attn(q, mk, mv, o_in, *, nb, n_tiles, rows, row0, mem_tokens):
    t, d = q.shape
    rb0 = row0 // rows
    qmap = lambda b, i: (rb0 + b * n_tiles + i, 0)
    in_specs = [pl.BlockSpec((rows, d), qmap),
                pl.BlockSpec((mem_tokens, d), lambda b, i: (b, 0)),
                pl.BlockSpec((mem_tokens, d), lambda b, i: (b, 0)),
                pl.BlockSpec(memory_space=pl.ANY)]
    aliases = {}
    if o_in is None:
        o_in = jnp.zeros((SUBLANES, LANES), BF16)
    else:
        aliases = {3: 0}
    return pl.pallas_call(
        functools.partial(_mem_attn_kernel, head_dim=d // MEM_HEADS),
        grid=(nb, n_tiles),
        in_specs=in_specs,
        out_specs=pl.BlockSpec((rows, d), qmap),
        out_shape=jax.ShapeDtypeStruct((t, d), BF16),
        input_output_aliases=aliases,
        compiler_params=_params(2),
        name="mem_attn",
    )(q, mk, mv, o_in)


def _pick_first_max(cur, iota, sentinel):
    mx = jnp.max(cur, axis=0, keepdims=True)
    first = jnp.min(jnp.where(cur == mx, iota, sentinel), axis=0, keepdims=True)
    return iota == first, first


def _router_kernel(x_ref, wh_ref, wl_ref, bias_ref, idx_ref, gate_ref, rank_ref, cnt_ref, run_ref, *, tm):
    i = pl.program_id(0)

    @pl.when(i == 0)
    def _():
        run_ref[...] = jnp.zeros_like(run_ref)

    x = x_ref[...]
    xh = x.astype(BF16)
    xl = (x - xh.astype(F32)).astype(BF16)
    wh = wh_ref[...]
    logits = (lax.dot_general(wh, xh, NT_DIMS, preferred_element_type=F32)
              + lax.dot_general(wh, xl, NT_DIMS, preferred_element_type=F32)
              + lax.dot_general(wl_ref[...], xh, NT_DIMS, preferred_element_type=F32))
    scores = _sigmoid(logits)
    sel = scores + bias_ref[...]
    neg_inf = -jnp.inf

    li = lax.broadcasted_iota(I32, (GROUP_SIZE, tm), 0)
    grp_rows = []
    for g in range(N_GROUPS):
        blk = sel[g * GROUP_SIZE:(g + 1) * GROUP_SIZE, :]
        pick, _ = _pick_first_max(blk, li, GROUP_SIZE)
        m1 = jnp.max(blk, axis=0, keepdims=True)
        m2 = jnp.max(jnp.where(pick, neg_inf, blk), axis=0, keepdims=True)
        grp_rows.append(m1 + m2)
    grp = jnp.concatenate(grp_rows, axis=0)

    gi = lax.broadcasted_iota(I32, (N_GROUPS, tm), 0)
    gsel = jnp.zeros((N_GROUPS, tm), F32)
    cur = grp
    for _ in range(TOPK_GROUPS):
        pick, _ = _pick_first_max(cur, gi, N_GROUPS)
        gsel = jnp.where(pick, 1.0, gsel)
        cur = jnp.where(pick, neg_inf, cur)
    emask = jnp.concatenate(
        [jnp.broadcast_to(gsel[g:g + 1, :], (GROUP_SIZE, tm)) for g in range(N_GROUPS)], axis=0)

    ei = lax.broadcasted_iota(I32, (N_EXPERTS, tm), 0)
    cur = jnp.where(emask > 0.5, sel, neg_inf)
    chosen = jnp.zeros((N_EXPERTS, tm), F32)
    idx_rows, w_rows = [], []
    for _ in range(TOP_K):
        pick, first = _pick_first_max(cur, ei, N_EXPERTS)
        idx_rows.append(first)
        w_rows.append(jnp.sum(jnp.where(pick, scores, 0.0), axis=0, keepdims=True))
        chosen = jnp.where(pick, 1.0, chosen)
        cur = jnp.where(pick, neg_inf, cur)
    idx = jnp.concatenate(idx_rows, axis=0)
    w = jnp.concatenate(w_rows, axis=0)
    gate_ref[...] = w / jnp.sum(w, axis=0, keepdims=True) * ROUTED_SCALE
    idx_ref[...] = idx

    ti = lax.broadcasted_iota(I32, (tm, tm), 0)
    tj = lax.broadcasted_iota(I32, (tm, tm), 1)
    before = jnp.where(ti < tj, 1.0, 0.0).astype(BF16)
    local = jnp.dot(chosen.astype(BF16), before, preferred_element_type=F32)
    total = local + run_ref[:, 0:1]
    rank_rows = [jnp.sum(jnp.where(ei == idx_rows[k], total, 0.0), axis=0, keepdims=True)
                 for k in range(TOP_K)]
    rank_ref[...] = jnp.concatenate(rank_rows, axis=0).astype(I32)
    run_ref[...] = run_ref[...] + jnp.sum(chosen, axis=1, keepdims=True)
    cnt_ref[...] = run_ref[...].astype(I32)


def _router(x, w_hi_t, w_lo_t, bias, tm):
    t, d = x.shape
    return pl.pallas_call(
        functools.partial(_router_kernel, tm=tm),
        grid=(t // tm,),
        in_specs=[pl.BlockSpec((tm, d), lambda i: (i, 0)),
                  pl.BlockSpec((N_EXPERTS, d), lambda i: (0, 0)),
                  pl.BlockSpec((N_EXPERTS, d), lambda i: (0, 0)),
                  pl.BlockSpec((N_EXPERTS, 1), lambda i: (0, 0))],
        out_specs=[pl.BlockSpec((TOP_K, tm), lambda i: (0, i)),
                   pl.BlockSpec((TOP_K, tm), lambda i: (0, i)),
                   pl.BlockSpec((TOP_K, tm), lambda i: (0, i)),
                   pl.BlockSpec((N_EXPERTS, LANES), lambda i: (0, 0))],
        out_shape=[jax.ShapeDtypeStruct((TOP_K, t), I32),
                   jax.ShapeDtypeStruct((TOP_K, t), F32),
                   jax.ShapeDtypeStruct((TOP_K, t), I32),
                   jax.ShapeDtypeStruct((N_EXPERTS, LANES), I32)],
        scratch_shapes=[pltpu.VMEM((N_EXPERTS, LANES), F32)],
        compiler_params=_params(1),
        name="router",
    )(x, w_hi_t, w_lo_t, bias.reshape(N_EXPERTS, 1))


PAD_BITS = tuple(1 << s for s in reversed(range(SUBLANES.bit_length() - 1, EXPERT_ROWS.bit_length() - 1)))


def _dispatch_kernel(zstart_ref, zcount_ref, dest_ref, x_ref, xs_ref, zero_ref, sem, zsem, *, tm):
    i = pl.program_id(0)

    def start_row(t, carry):
        for k in range(TOP_K):
            d = dest_ref[k * tm + t]
            pltpu.make_async_copy(x_ref.at[pl.ds(t, 1), :], xs_ref.at[pl.ds(d, 1), :], sem).start()
        return carry

    lax.fori_loop(0, tm, start_row, 0, unroll=ISSUE_UNROLL)
    for k in range(TOP_K):
        pltpu.make_async_copy(x_ref, xs_ref.at[pl.ds(0, tm), :], sem).wait()

    @pl.when(i == pl.num_programs(0) - 1)
    def _():
        zero_ref[...] = jnp.zeros_like(zero_ref)

        def fill(e, wait):
            def fill_rows(pos, n, pred):
                @pl.when(pred)
                def _():
                    cp = pltpu.make_async_copy(zero_ref.at[pl.ds(0, n), :],
                                               xs_ref.at[pl.ds(pos, n), :], zsem)
                    if wait:
                        cp.wait()
                    else:
                        cp.start()

            pos = zstart_ref[e]
            cnt = zcount_ref[e]
            head = jnp.minimum((-pos) & (SUBLANES - 1), cnt)
            for j in range(SUBLANES - 1):
                fill_rows(pos + j, 1, j < head)
            pos = pos + head
            cnt = cnt - head
            for bit in PAD_BITS:
                has = (cnt & bit) != 0
                fill_rows(pl.multiple_of(pos, SUBLANES), bit, has)
                pos = pos + jnp.where(has, bit, 0)

        def fill_start(e, carry):
            fill(e, False)
            return carry

        def fill_wait(e, carry):
            fill(e, True)
            return carry

        lax.fori_loop(0, N_EXPERTS, fill_start, 0)
        lax.fori_loop(0, N_EXPERTS, fill_wait, 0)


def _dispatch(x, dest_tiles, zstart, zcount, n_rows, tm):
    t, d = x.shape
    return pl.pallas_call(
        functools.partial(_dispatch_kernel, tm=tm),
        grid_spec=pltpu.PrefetchScalarGridSpec(
            num_scalar_prefetch=2,
            grid=(t // tm,),
            in_specs=[pl.BlockSpec((TOP_K * tm,), lambda i, zs, zc: (i,), memory_space=pltpu.SMEM),
                      pl.BlockSpec((tm, d), lambda i, zs, zc: (i, 0))],
            out_specs=pl.BlockSpec(memory_space=pl.ANY),
            scratch_shapes=[pltpu.VMEM((EXPERT_ROWS // 2, d), x.dtype),
                            pltpu.SemaphoreType.DMA(()),
                            pltpu.SemaphoreType.DMA(())]),
        out_shape=jax.ShapeDtypeStruct((n_rows, d), x.dtype),
        compiler_params=_params(1),
        name="moe_dispatch",
    )(zstart, zcount, dest_tiles, x)


def _expert_kernel(be_ref, nv_ref, x_ref, wg_ref, wu_ref, wd_ref, y_ref, wgb_ref, wub_ref, wdb_ref):
    b = pl.program_id(0)
    prev = be_ref[jnp.maximum(b - 1, 0)]
    fresh = jnp.logical_or(b == 0, be_ref[b] != prev)

    @pl.when(fresh)
    def _():
        wgb_ref[...] = wg_ref[0].astype(BF16)
        wub_ref[...] = wu_ref[0].astype(BF16)
        wdb_ref[...] = wd_ref[0].astype(BF16)

    @pl.when(b < nv_ref[0])
    def _():
        half = x_ref.shape[1]
        x_lo, x_hi = _unpack_halves(x_ref[...])
        x_lo = x_lo.astype(BF16)
        x_hi = x_hi.astype(BF16)
        hg = (jnp.dot(x_lo, wgb_ref[:half, :], preferred_element_type=F32)
              + jnp.dot(x_hi, wgb_ref[half:, :], preferred_element_type=F32))
        hu = (jnp.dot(x_lo, wub_ref[:half, :], preferred_element_type=F32)
              + jnp.dot(x_hi, wub_ref[half:, :], preferred_element_type=F32))
        act = (_silu(hg) * hu).astype(BF16)
        y_ref[...] = _pack_halves(jnp.dot(act, wdb_ref[...], preferred_element_type=F32))


def _experts(xs, wg, wu, wd, block_expert, n_valid):
    n_rows = xs.shape[0]
    d = wg.shape[1]
    de = wg.shape[2]
    nb = n_rows // EXPERT_ROWS
    xmap = lambda b, be, nv: (jnp.minimum(b, nv[0] - 1), 0)
    return pl.pallas_call(
        _expert_kernel,
        grid_spec=pltpu.PrefetchScalarGridSpec(
            num_scalar_prefetch=2,
            grid=(nb,),
            in_specs=[pl.BlockSpec((EXPERT_ROWS, d // 2), xmap),
                      pl.BlockSpec((1, d, de), lambda b, be, nv: (be[b], 0, 0)),
                      pl.BlockSpec((1, d, de), lambda b, be, nv: (be[b], 0, 0)),
                      pl.BlockSpec((1, de, d), lambda b, be, nv: (be[b], 0, 0))],
            out_specs=pl.BlockSpec((EXPERT_ROWS, d // 2), xmap),
            scratch_shapes=[pltpu.VMEM((d, de), BF16), pltpu.VMEM((d, de), BF16),
                            pltpu.VMEM((de, d), BF16)]),
        out_shape=jax.ShapeDtypeStruct((n_rows, d // 2), I32),
        compiler_params=_params(1),
        name="moe_experts",
    )(block_expert, n_valid, xs, wg, wu, wd)


def _combine_kernel(dest_ref, x_ref, gate_ref, ys_ref, swg_ref, swu_ref, swd_ref, g_ref, b_ref,
                    op_ref, os_ref, buf_ref, sem, *, tm, n_prompt_tiles):
    i = pl.program_id(0)

    def start_row(t, carry):
        for k in range(TOP_K):
            d = dest_ref[k * tm + t]
            pltpu.make_async_copy(ys_ref.at[pl.ds(d, 1), :], buf_ref.at[k, pl.ds(t, 1), :], sem).start()
        return carry

    lax.fori_loop(0, tm, start_row, 0, unroll=ISSUE_UNROLL)
    x = x_ref[...]
    xb = x.astype(BF16)
    act = (_silu(jnp.dot(xb, swg_ref[...], preferred_element_type=F32))
           * jnp.dot(xb, swu_ref[...], preferred_element_type=F32)).astype(BF16)
    shared = jnp.dot(act, swd_ref[...], preferred_element_type=F32)
    for k in range(TOP_K):
        pltpu.make_async_copy(ys_ref.at[pl.ds(0, tm), :], buf_ref.at[k], sem).wait()
    gate = gate_ref[...]
    acc_lo = None
    for k in range(TOP_K):
        y_lo, y_hi = _unpack_halves(buf_ref[k])
        gk = gate[:, k:k + 1]
        acc_lo = gk * y_lo if acc_lo is None else acc_lo + gk * y_lo
        acc_hi = gk * y_hi if k == 0 else acc_hi + gk * y_hi
    moe = jnp.concatenate([acc_lo, acc_hi], axis=-1) + shared
    out = _layer_norm(DEEPNORM_ALPHA * x + moe, g_ref[...], b_ref[...])

    @pl.when(i < n_prompt_tiles)
    def _():
        op_ref[...] = out

    @pl.when(i >= n_prompt_tiles)
    def _():
        os_ref[...] = out


def _combine(x, gate_t, dest_tiles, ys, swg, swu, swd, g, b, tm, n_prompt_rows):
    t, d = x.shape
    ds = swg.shape[1]
    npt = n_prompt_rows // tm
    const = lambda i: (0, 0)
    return pl.pallas_call(
        functools.partial(_combine_kernel, tm=tm, n_prompt_tiles=npt),
        grid=(t // tm,),
        in_specs=[pl.BlockSpec((TOP_K * tm,), lambda i: (i,), memory_space=pltpu.SMEM),
                  pl.BlockSpec((tm, d), lambda i: (i, 0)),
                  pl.BlockSpec((tm, TOP_K), lambda i: (i, 0)),
                  pl.BlockSpec(memory_space=pl.ANY),
                  pl.BlockSpec((d, ds), const),
                  pl.BlockSpec((d, ds), const),
                  pl.BlockSpec((ds, d), const),
                  pl.BlockSpec((1, d), const),
                  pl.BlockSpec((1, d), const)],
        out_specs=[pl.BlockSpec((tm, d), lambda i: (jnp.minimum(i, npt - 1), 0)),
                   pl.BlockSpec((tm, d), lambda i: (jnp.maximum(i - npt, 0), 0))],
        out_shape=[jax.ShapeDtypeStruct((n_prompt_rows, d), F32),
                   jax.ShapeDtypeStruct((t - n_prompt_rows, d), F32)],
        scratch_shapes=[pltpu.VMEM((TOP_K, tm, d // 2), I32), pltpu.SemaphoreType.DMA(())],
        compiler_params=_params(1),
        name="moe_combine",
    )(dest_tiles, x, gate_t, ys, swg, swu, swd, g.reshape(1, d), b.reshape(1, d))


def _tile_major(a, tm):
    k, t = a.shape
    return a.reshape(k, t // tm, tm).transpose(1, 0, 2).reshape(-1)


def _moe_ln(x, x_packed, router_w, router_bias, wg, wu, wd, swg, swu, swd, g, b, n_prompt_rows):
    t, d = x.shape
    w_t = router_w.T
    w_hi = w_t.astype(BF16)
    w_lo = (w_t - w_hi.astype(F32)).astype(BF16)
    tr = _divisor_tile(t, 512, LANES)
    idx, gate, rank, cnt = _router(x, w_hi, w_lo, router_bias, tr)

    counts = cnt[:, 0]
    padded = (counts + EXPERT_ROWS - 1) // EXPERT_ROWS * EXPERT_ROWS
    pad_end = jnp.cumsum(padded)
    pad_start = pad_end - padded
    experts = jnp.arange(N_EXPERTS, dtype=I32)
    onehot = idx[None] == experts[:, None, None]
    dest = jnp.sum(jnp.where(onehot, pad_start[:, None, None], 0), axis=0) + rank
    n_blocks = (t * TOP_K + N_EXPERTS * (EXPERT_ROWS - 1)) // EXPERT_ROWS + 1
    n_valid = (pad_end[-1] // EXPERT_ROWS).astype(I32)
    blk = jnp.minimum(jnp.arange(n_blocks, dtype=I32), n_valid - 1)
    block_expert = jnp.minimum(
        jnp.sum(pad_end[None, :] <= (blk * EXPERT_ROWS)[:, None], axis=1), N_EXPERTS - 1).astype(I32)

    td = _divisor_tile(t, 256, SUBLANES)
    xs = _dispatch(x_packed, _tile_major(dest, td), (pad_start + counts).astype(I32),
                   (padded - counts).astype(I32), n_blocks * EXPERT_ROWS, td)
    ys = _experts(xs, wg, wu, wd, block_expert, n_valid.reshape(1))
    tc = _divisor_tile(math.gcd(n_prompt_rows, t - n_prompt_rows), 128, SUBLANES)
    return _combine(x, gate.T, _tile_major(dest, tc), ys, swg.astype(BF16), swu.astype(BF16),
                    swd.astype(BF16), g, b, tc, n_prompt_rows)


def kernel(x_prompt, x_sample, mem_prompt, state_gla, cache_swa_k, cache_swa_v, cache_mem_k, cache_mem_v,
           w_in, gla_w_gate2, gla_b_gate, gla_norm_g, swa_sinks, w_mix_out, ln1_g, ln1_b,
           mem_wq, mem_wk, mem_wv, mem_wo, ln2_g, ln2_b,
           router_w, router_bias, exp_w_gate, exp_w_up, exp_w_down, sh_w_gate, sh_w_up, sh_w_down,
           ln3_g, ln3_b):
    nbp, seq, d = x_prompt.shape
    nbs, dec, _ = x_sample.shape
    tp, ts = nbp * seq, nbs * dec
    t = tp + ts
    mem_tokens = mem_prompt.shape[1]
    assert state_gla.shape[0] == DEPTH and seq % SWA_WINDOW == 0 and dec % SUBLANES == 0
    assert cache_swa_k.shape[2] == SWA_WINDOW and tp % dec == 0
    tm = _divisor_tile(math.gcd(tp, ts), 512, 16)
    l = 0

    x = jnp.concatenate([x_prompt.reshape(tp, d), x_sample.reshape(ts, d)], axis=0)

    wi = w_in[l]
    o = 0
    seg = []
    for size in (GLA_QK, GLA_QK, GLA_VW, GLA_VW, GLA_GATE_RANK, SWA_QW, SWA_KVW, SWA_KVW):
        seg.append(wi[:, o:o + size])
        o += size
    gq, gk, gv, gr, gg, sq, sk, sv = seg
    w_in_r = jnp.concatenate(
        [gq, gk, gv, gr, sq, sk, sv, gg, jnp.zeros((d, LANES - GLA_GATE_RANK), wi.dtype)], axis=1).astype(BF16)
    z = _matmul(x, w_in_r, tm, _divisor_tile(IN_COLS_PADDED, 640, LANES), F32)

    wa2_pad = jnp.concatenate(
        [gla_w_gate2[l], jnp.zeros((LANES - GLA_GATE_RANK, GLA_QK), F32)], axis=0)
    zero_state = jnp.zeros((nbp, GLA_HEADS, GLA_DV, GLA_DK), F32)
    mix, st_p = _gla(z, wa2_pad, gla_b_gate[l], gla_norm_g[l], zero_state, None,
                     nb=nbp, n_chunks=seq // CHUNK, rows=CHUNK, row0=0)
    mix, st_s = _gla(z, wa2_pad, gla_b_gate[l], gla_norm_g[l], jnp.swapaxes(state_gla[l], -1, -2), mix,
                     nb=nbs, n_chunks=1, rows=dec, row0=tp)
    nblk = seq // SWA_WINDOW
    prev_p = lambda col: (lambda b, i: (b * nblk + jnp.maximum(i - 1, 0), col))
    mix = _swa(z, swa_sinks[l], z, z, (prev_p(COL_SK // SWA_KVW), prev_p(COL_SV // SWA_KVW)), mix,
               nb=nbp, n_blocks=nblk, rows=SWA_WINDOW, row0=0, q_base=0, q_stride=SWA_WINDOW)
    ck = cache_swa_k[l].reshape(nbs * SWA_WINDOW, SWA_KVW)
    cv = cache_swa_v[l].reshape(nbs * SWA_WINDOW, SWA_KVW)
    prev_s = lambda b, i: (b, 0)
    mix = _swa(z, swa_sinks[l], ck, cv, (prev_s, prev_s), mix,
               nb=nbs, n_blocks=1, rows=dec, row0=tp, q_base=PAST_LEN, q_stride=0)
    h1 = _matmul_res_ln(mix, w_mix_out[l].astype(BF16), x, ln1_g[l], ln1_b[l], tm)

    mem = mem_prompt.reshape(nbp * mem_tokens, d)
    tmem = _divisor_tile(nbp * mem_tokens, 512, 16)
    mk = _matmul(mem, mem_wk[l].astype(BF16), tmem, _divisor_tile(d, 1024, LANES), F32)
    mv = _matmul(mem, mem_wv[l].astype(BF16), tmem, _divisor_tile(d, 1024, LANES), F32)
    q = _matmul(h1, mem_wq[l].astype(BF16), tm, _divisor_tile(d, 1024, LANES), BF16)
    tq = _divisor_tile(seq, 512, 16)
    att = _mem_attn(q, mk, mv, None, nb=nbp, n_tiles=seq // tq, rows=tq, row0=0, mem_tokens=mem_tokens)
    att = _mem_attn(q, cache_mem_k[l].reshape(nbs * mem_tokens, d), cache_mem_v[l].reshape(nbs * mem_tokens, d),
                    att, nb=nbs, n_tiles=1, rows=dec, row0=tp, mem_tokens=mem_tokens)
    h2, h2_packed = _matmul_res_ln(att, mem_wo[l].astype(BF16), h1, ln2_g[l], ln2_b[l], tm, emit_packed=True)

    out_p, out_s = _moe_ln(h2, h2_packed, router_w[l], router_bias[l], exp_w_gate[l], exp_w_up[l],
                           exp_w_down[l], sh_w_gate[l], sh_w_up[l], sh_w_down[l], ln3_g[l], ln3_b[l], tp)

    y_prompt = out_p.reshape(nbp, seq, d)
    y_sample = out_s.reshape(nbs, dec, d)
    zp = z[:tp].reshape(nbp, seq, IN_COLS_PADDED)[:, seq - SWA_WINDOW:]
    zs = z[tp:].reshape(nbs, dec, IN_COLS_PADDED)
    kv_shape = lambda a: a.reshape(a.shape[0], a.shape[1], SWA_KV_HEADS, SWA_HEAD_DIM)[None]
    mem_shape = lambda a: a.reshape(nbp, mem_tokens, MEM_HEADS, d // MEM_HEADS)[None]
    return (y_prompt, y_sample,
            jnp.swapaxes(st_p, -1, -2)[None],
            kv_shape(zp[..., COL_SK:COL_SK + SWA_KVW]), kv_shape(zp[..., COL_SV:COL_SV + SWA_KVW]),
            mem_shape(mk), mem_shape(mv),
            jnp.swapaxes(st_s, -1, -2)[None],
            kv_shape(zs[..., COL_SK:COL_SK + SWA_KVW]), kv_shape(zs[..., COL_SV:COL_SV + SWA_KVW]))
```

```python
import functools
import math

import jax
import jax.numpy as jnp
from jax import lax
from jax.experimental import pallas as pl
from jax.experimental.pallas import tpu as pltpu

F32 = jnp.float32
BF16 = jnp.bfloat16
I32 = jnp.int32

CHUNK = 64
PAST_LEN = 2048
GLA_HEADS = 8
GLA_DK = 64
GLA_DV = 128
GLA_QK = GLA_HEADS * GLA_DK
GLA_VW = GLA_HEADS * GLA_DV
GLA_GATE_RANK = 16
GLA_TAU = 16.0
SWA_HEADS = 16
SWA_KV_HEADS = 2
SWA_GROUP = SWA_HEADS // SWA_KV_HEADS
SWA_HEAD_DIM = 64
SWA_QW = SWA_HEADS * SWA_HEAD_DIM
SWA_KVW = SWA_KV_HEADS * SWA_HEAD_DIM
SWA_WINDOW = 128
WINDOW_CHUNKS = SWA_WINDOW // CHUNK
MEM_HEADS = 4
N_EXPERTS = 64
TOP_K = 8
N_GROUPS = 8
GROUP_SIZE = N_EXPERTS // N_GROUPS
TOPK_GROUPS = 4
ROUTED_SCALE = 2.5
DEPTH = 1
DEEPNORM_ALPHA = (2 * DEPTH) ** 0.25
LN_EPS = 1e-5
RMS_EPS = 1e-6

LANES = 128
SUBLANES = 8
VMEM_LIMIT_BYTES = 56 * 1024 * 1024

COL_GQ = 0
COL_GK = COL_GQ + GLA_QK
COL_GV = COL_GK + GLA_QK
COL_GR = COL_GV + GLA_VW
COL_SQ = COL_GR + GLA_VW
COL_SK = COL_SQ + SWA_QW
COL_SV = COL_SK + SWA_KVW
COL_GG = COL_SV + SWA_KVW
MXU_COLS = 256
IN_COLS_PADDED = -(-(COL_GG + LANES) // (2 * MXU_COLS)) * (2 * MXU_COLS)

EXPERT_ROWS = 256
ISSUE_UNROLL = 8

NT_DIMS = (((1,), (1,)), ((), ()))
TN_DIMS = (((0,), (0,)), ((), ()))


def _params(n_axes):
    return pltpu.CompilerParams(dimension_semantics=("arbitrary",) * n_axes,
                                vmem_limit_bytes=VMEM_LIMIT_BYTES)


def _sigmoid(x):
    return 1.0 / (1.0 + jnp.exp(-x))


def _silu(x):
    return x * _sigmoid(x)


def _layer_norm(x, g, b):
    mu = jnp.mean(x, axis=-1, keepdims=True)
    xc = x - mu
    var = jnp.mean(xc * xc, axis=-1, keepdims=True)
    return xc * lax.rsqrt(var + LN_EPS) * g + b


HI_MASK = -65536


def _round_to_bf16_bits(x):
    b = lax.bitcast_convert_type(x, I32)
    return b + (0x7FFF + (lax.shift_right_logical(b, 16) & 1))


def _pack_halves(x):
    c = x.shape[1] // 2
    lo = lax.shift_right_logical(_round_to_bf16_bits(x[:, :c]), 16)
    hi = _round_to_bf16_bits(x[:, c:]) & HI_MASK
    return lo | hi


def _unpack_halves(w):
    return (lax.bitcast_convert_type(lax.shift_left(w, 16), F32),
            lax.bitcast_convert_type(w & HI_MASK, F32))


def _divisor_tile(n, pref, mult):
    t = min(pref, n)
    while t > mult and (n % t or t % mult):
        t -= mult
    assert n % t == 0 and t % mult == 0, (n, pref, mult)
    return t


def _mm_kernel(x_ref, w_ref, o_ref, xb_ref):
    @pl.when(pl.program_id(1) == 0)
    def _():
        xb_ref[...] = x_ref[...].astype(BF16)

    o_ref[...] = jnp.dot(xb_ref[...], w_ref[...], preferred_element_type=F32).astype(o_ref.dtype)


def _matmul(x, w, tm, tn, out_dtype):
    m, k = x.shape
    n = w.shape[1]
    return pl.pallas_call(
        _mm_kernel,
        grid=(m // tm, n // tn),
        in_specs=[pl.BlockSpec((tm, k), lambda i, j: (i, 0)),
                  pl.BlockSpec((k, tn), lambda i, j: (0, j))],
        out_specs=pl.BlockSpec((tm, tn), lambda i, j: (i, j)),
        out_shape=jax.ShapeDtypeStruct((m, n), out_dtype),
        scratch_shapes=[pltpu.VMEM((tm, k), BF16)],
        compiler_params=_params(2),
        name="matmul",
    )(x, w)


def _split_rows_specs(tm, k, n_first_tiles, n_grid_axes):
    if n_grid_axes == 1:
        return [pl.BlockSpec((tm, k), lambda i: (jnp.minimum(i, n_first_tiles - 1), 0)),
                pl.BlockSpec((tm, k), lambda i: (jnp.maximum(i - n_first_tiles, 0), 0))]
    return [pl.BlockSpec((tm, k), lambda i, j: (jnp.minimum(i, n_first_tiles - 1), 0)),
            pl.BlockSpec((tm, k), lambda i, j: (jnp.maximum(i - n_first_tiles, 0), 0))]


def _mm_split_kernel(xa_ref, xb_ref, w_ref, o_ref, xbf_ref, *, n_first_tiles):
    i = pl.program_id(0)
    first_col = pl.program_id(1) == 0

    @pl.when(first_col & (i < n_first_tiles))
    def _():
        xbf_ref[...] = xa_ref[...].astype(BF16)

    @pl.when(first_col & (i >= n_first_tiles))
    def _():
        xbf_ref[...] = xb_ref[...].astype(BF16)

    o_ref[...] = jnp.dot(xbf_ref[...], w_ref[...], preferred_element_type=F32).astype(o_ref.dtype)


def _matmul_split(xa, xb, w, tm, tn, out_dtype):
    ma, k = xa.shape
    m = ma + xb.shape[0]
    n = w.shape[1]
    return pl.pallas_call(
        functools.partial(_mm_split_kernel, n_first_tiles=ma // tm),
        grid=(m // tm, n // tn),
        in_specs=_split_rows_specs(tm, k, ma // tm, 2) + [pl.BlockSpec((k, tn), lambda i, j: (0, j))],
        out_specs=pl.BlockSpec((tm, tn), lambda i, j: (i, j)),
        out_shape=jax.ShapeDtypeStruct((m, n), out_dtype),
        scratch_shapes=[pltpu.VMEM((tm, k), BF16)],
        compiler_params=_params(2),
        name="matmul_split",
    )(xa, xb, w)


def _mm_res_ln_kernel(x_ref, w_ref, *refs, n_res, n_first_tiles):
    res_refs, (g_ref, b_ref, o_ref, *packed_ref) = refs[:n_res], refs[n_res:]
    y = jnp.dot(x_ref[...], w_ref[...], preferred_element_type=F32)

    def finish(res):
        h = _layer_norm(DEEPNORM_ALPHA * res + y, g_ref[...], b_ref[...])
        o_ref[...] = h
        if packed_ref:
            packed_ref[0][...] = _pack_halves(h)

    if n_res == 1:
        finish(res_refs[0][...])
    else:
        i = pl.program_id(0)

        @pl.when(i < n_first_tiles)
        def _():
            finish(res_refs[0][...])

        @pl.when(i >= n_first_tiles)
        def _():
            finish(res_refs[1][...])


def _matmul_res_ln(x, w, res, g, b, tm, emit_packed=False):
    m, k = x.shape
    n = w.shape[1]
    if isinstance(res, tuple):
        n_first_tiles = res[0].shape[0] // tm
        res_specs = _split_rows_specs(tm, n, n_first_tiles, 1)
    else:
        n_first_tiles = 0
        res = (res,)
        res_specs = [pl.BlockSpec((tm, n), lambda i: (i, 0))]
    out_specs = [pl.BlockSpec((tm, n), lambda i: (i, 0))]
    out_shape = [jax.ShapeDtypeStruct((m, n), F32)]
    if emit_packed:
        out_specs.append(pl.BlockSpec((tm, n // 2), lambda i: (i, 0)))
        out_shape.append(jax.ShapeDtypeStruct((m, n // 2), I32))
    outs = pl.pallas_call(
        functools.partial(_mm_res_ln_kernel, n_res=len(res), n_first_tiles=n_first_tiles),
        grid=(m // tm,),
        in_specs=[pl.BlockSpec((tm, k), lambda i: (i, 0)),
                  pl.BlockSpec((k, n), lambda i: (0, 0))] + res_specs + [
                  pl.BlockSpec((1, n), lambda i: (0, 0)),
                  pl.BlockSpec((1, n), lambda i: (0, 0))],
        out_specs=out_specs,
        out_shape=out_shape,
        compiler_params=_params(1),
        name="matmul_res_ln",
    )(x, w, *res, g.reshape(1, n), b.reshape(1, n))
    return outs if emit_packed else outs[0]


def _gla_kernel(q_ref, k_ref, v_ref, r_ref, gg_ref, wa2_ref, ba_ref, ng_ref, s0_ref, mix_in_ref,
                o_ref, sfin_ref, st_ref, *, rows):
    del mix_in_ref
    c = pl.program_id(1)

    @pl.when(c == 0)
    def _():
        st_ref[...] = s0_ref[0]

    gate = jnp.dot(gg_ref[...], wa2_ref[...], precision=lax.Precision.HIGHEST,
                   preferred_element_type=F32) + ba_ref[...]
    log_a = (jnp.minimum(gate, 0.0) - jnp.log1p(jnp.exp(-jnp.abs(gate)))) / GLA_TAU
    ri = lax.broadcasted_iota(I32, (rows, rows), 0)
    ci = lax.broadcasted_iota(I32, (rows, rows), 1)
    causal = ri >= ci
    bcum = jnp.dot(causal.astype(F32), log_a, precision=lax.Precision.HIGHEST,
                   preferred_element_type=F32)
    b_last = bcum[rows - 1:rows, :]
    q = q_ref[...] * (GLA_DK ** -0.5)
    k = k_ref[...]
    q_in = (q * jnp.exp(bcum)).astype(BF16)
    k_in = (k * jnp.exp(-bcum)).astype(BF16)
    k_out = (k * jnp.exp(b_last - bcum)).astype(BF16)
    decay = jnp.exp(b_last)
    ng = ng_ref[...]
    for h in range(GLA_HEADS):
        ks = slice(h * GLA_DK, (h + 1) * GLA_DK)
        vs = slice(h * GLA_DV, (h + 1) * GLA_DV)
        vh = v_ref[:, vs].astype(BF16)
        st = st_ref[h]
        attn = lax.dot_general(q_in[:, ks], k_in[:, ks], NT_DIMS, preferred_element_type=F32)
        attn = jnp.where(causal, attn, 0.0).astype(BF16)
        o = jnp.dot(attn, vh, preferred_element_type=F32)
        o = o + lax.dot_general(q_in[:, ks], st.astype(BF16), NT_DIMS, preferred_element_type=F32)
        upd_t = lax.dot_general(vh, k_out[:, ks], TN_DIMS, preferred_element_type=F32)
        st_ref[h] = st * decay[:, ks] + upd_t
        o = o * lax.rsqrt(jnp.mean(o * o, axis=-1, keepdims=True) + RMS_EPS) * ng
        o_ref[:, vs] = (o * _silu(r_ref[:, vs])).astype(o_ref.dtype)
    sfin_ref[0] = st_ref[...]


def _gla(z, wa2_pad, ba, norm_g, s0_t, mix_in, *, nb, n_chunks, rows, row0):
    t = z.shape[0]
    rb0 = row0 // rows
    rmap = lambda colblk: (lambda b, c: (rb0 + b * n_chunks + c, colblk))
    in_specs = [
        pl.BlockSpec((rows, GLA_QK), rmap(COL_GQ // GLA_QK)),
        pl.BlockSpec((rows, GLA_QK), rmap(COL_GK // GLA_QK)),
        pl.BlockSpec((rows, GLA_VW), rmap(COL_GV // GLA_VW)),
        pl.BlockSpec((rows, GLA_VW), rmap(COL_GR // GLA_VW)),
        pl.BlockSpec((rows, LANES), rmap(COL_GG // LANES)),
        pl.BlockSpec((LANES, GLA_QK), lambda b, c: (0, 0)),
        pl.BlockSpec((1, GLA_QK), lambda b, c: (0, 0)),
        pl.BlockSpec((1, GLA_DV), lambda b, c: (0, 0)),
        pl.BlockSpec((1, GLA_HEADS, GLA_DV, GLA_DK), lambda b, c: (b, 0, 0, 0)),
    ]
    args = [z, z, z, z, z, wa2_pad, ba.reshape(1, GLA_QK), norm_g.reshape(1, GLA_DV), s0_t]
    aliases = {}
    if mix_in is None:
        mix_in = jnp.zeros((SUBLANES, LANES), BF16)
        in_specs.append(pl.BlockSpec(memory_space=pl.ANY))
    else:
        in_specs.append(pl.BlockSpec(memory_space=pl.ANY))
        aliases = {len(args): 0}
    args.append(mix_in)
    mix, s_fin = pl.pallas_call(
        functools.partial(_gla_kernel, rows=rows),
        grid=(nb, n_chunks),
        in_specs=in_specs,
        out_specs=[pl.BlockSpec((rows, GLA_VW), rmap(0)),
                   pl.BlockSpec((1, GLA_HEADS, GLA_DV, GLA_DK), lambda b, c: (b, 0, 0, 0))],
        out_shape=[jax.ShapeDtypeStruct((t, GLA_VW + SWA_QW), BF16),
                   jax.ShapeDtypeStruct((nb, GLA_HEADS, GLA_DV, GLA_DK), F32)],
        scratch_shapes=[pltpu.VMEM((GLA_HEADS, GLA_DV, GLA_DK), F32)],
        input_output_aliases=aliases,
        compiler_params=_params(2),
        name="gla",
    )(*args)
    return mix, s_fin


def _swa_kernel(sinks_ref, q_ref, kc_ref, vc_ref, kp_ref, vp_ref, mix_in_ref, o_ref, *,
                rows, q_base, q_stride):
    del mix_in_ref
    i = pl.program_id(1)
    q0 = q_base + i * q_stride
    nk = SWA_WINDOW + rows
    keys = jnp.concatenate([kp_ref[...], kc_ref[...]], axis=0).astype(BF16)
    vals = jnp.concatenate([vp_ref[...], vc_ref[...]], axis=0).astype(BF16)
    sr = SWA_GROUP * rows
    row = lax.broadcasted_iota(I32, (sr, nk), 0)
    qpos = q0 + (row & (rows - 1))
    kpos = q0 - SWA_WINDOW + lax.broadcasted_iota(I32, (sr, nk), 1)
    dist = jnp.abs(qpos - kpos).astype(F32)
    qc = lax.shift_right_arithmetic(qpos, CHUNK.bit_length() - 1)
    kc = lax.shift_right_arithmetic(kpos, CHUNK.bit_length() - 1)
    allowed = (kpos >= 0) & (kc <= qc) & (kc >= qc - WINDOW_CHUNKS)
    assert rows & (rows - 1) == 0
    head_in_group = lax.shift_right_logical(lax.broadcasted_iota(I32, (sr, 1), 0), rows.bit_length() - 1)
    for g in range(SWA_KV_HEADS):
        slope = jnp.zeros((sr, 1), F32)
        sink = jnp.zeros((sr, 1), F32)
        for j in range(SWA_GROUP):
            h = g * SWA_GROUP + j
            slope = jnp.where(head_in_group == j, 2.0 ** (-8.0 * (h + 1) / SWA_HEADS), slope)
            sink = jnp.where(head_in_group == j, sinks_ref[h], sink)
        kv = slice(g * SWA_HEAD_DIM, (g + 1) * SWA_HEAD_DIM)
        qg = jnp.concatenate(
            [q_ref[:, (g * SWA_GROUP + j) * SWA_HEAD_DIM:(g * SWA_GROUP + j + 1) * SWA_HEAD_DIM]
             for j in range(SWA_GROUP)], axis=0).astype(BF16)
        s = lax.dot_general(qg, keys[:, kv], NT_DIMS, preferred_element_type=F32)
        s = s * (SWA_HEAD_DIM ** -0.5) - slope * dist
        s = jnp.where(allowed, s, -jnp.inf)
        m = jnp.maximum(jnp.max(s, axis=-1, keepdims=True), sink)
        p = jnp.exp(s - m)
        denom = jnp.sum(p, axis=-1, keepdims=True) + jnp.exp(sink - m)
        o = jnp.dot(p.astype(BF16), vals[:, kv], preferred_element_type=F32) / denom
        for j in range(0, SWA_GROUP, 2):
            c0 = (g * SWA_GROUP + j) * SWA_HEAD_DIM
            o_ref[:, c0:c0 + LANES] = jnp.concatenate(
                [o[j * rows:(j + 1) * rows], o[(j + 1) * rows:(j + 2) * rows]], axis=-1).astype(o_ref.dtype)


def _swa(z, sinks, k_prev, v_prev, prev_map, mix_in, *, nb, n_blocks, rows, row0, q_base, q_stride):
    rb0 = row0 // rows
    rmap = lambda colblk: (lambda b, i: (rb0 + b * n_blocks + i, colblk))
    return pl.pallas_call(
        functools.partial(_swa_kernel, rows=rows, q_base=q_base, q_stride=q_stride),
        grid=(nb, n_blocks),
        in_specs=[pl.BlockSpec(memory_space=pltpu.SMEM),
                  pl.BlockSpec((rows, SWA_QW), rmap(COL_SQ // SWA_QW)),
                  pl.BlockSpec((rows, SWA_KVW), rmap(COL_SK // SWA_KVW)),
                  pl.BlockSpec((rows, SWA_KVW), rmap(COL_SV // SWA_KVW)),
                  pl.BlockSpec((SWA_WINDOW, SWA_KVW), prev_map[0]),
                  pl.BlockSpec((SWA_WINDOW, SWA_KVW), prev_map[1]),
                  pl.BlockSpec(memory_space=pl.ANY)],
        out_specs=pl.BlockSpec((rows, SWA_QW), rmap(GLA_VW // SWA_QW)),
        out_shape=jax.ShapeDtypeStruct(mix_in.shape, mix_in.dtype),
        input_output_aliases={6: 0},
        compiler_params=_params(2),
        name="swa",
    )(sinks, z, z, z, k_prev, v_prev, mix_in)


def _mem_attn_kernel(q_ref, k_ref, v_ref, o_in_ref, o_ref, *, head_dim):
    del o_in_ref
    for h in range(MEM_HEADS):
        hs = slice(h * head_dim, (h + 1) * head_dim)
        kh = k_ref[:, hs].astype(BF16)
        vh = v_ref[:, hs].astype(BF16)
        s = lax.dot_general(q_ref[:, hs], kh, NT_DIMS, preferred_element_type=F32) * (head_dim ** -0.5)
        m = jnp.max(s, axis=-1, keepdims=True)
        p = jnp.exp(s - m)
        denom = jnp.sum(p, axis=-1, keepdims=True)
        o = jnp.dot(p.astype(BF16), vh, preferred_element_type=F32) / denom
        o_ref[:, hs] = o.astype(o_ref.dtype)


def _mem_attn(q, mk, mv, o_in, *, nb, n_tiles, rows, row0, mem_tokens):
    t, d = q.shape
    rb0 = row0 // rows
    qmap = lambda b, i: (rb0 + b * n_tiles + i, 0)
    in_specs = [pl.BlockSpec((rows, d), qmap),
                pl.BlockSpec((mem_tokens, d), lambda b, i: (b, 0)),
                pl.BlockSpec((mem_tokens, d), lambda b, i: (b, 0)),
                pl.BlockSpec(memory_space=pl.ANY)]
    aliases = {}
    if o_in is None:
        o_in = jnp.zeros((SUBLANES, LANES), BF16)
    else:
        aliases = {3: 0}
    return pl.pallas_call(
        functools.partial(_mem_attn_kernel, head_dim=d // MEM_HEADS),
        grid=(nb, n_tiles),
        in_specs=in_specs,
        out_specs=pl.BlockSpec((rows, d), qmap),
        out_shape=jax.ShapeDtypeStruct((t, d), BF16),
        input_output_aliases=aliases,
        compiler_params=_params(2),
        name="mem_attn",
    )(q, mk, mv, o_in)


def _pick_first_max(cur, iota, sentinel):
    mx = jnp.max(cur, axis=0, keepdims=True)
    first = jnp.min(jnp.where(cur == mx, iota, sentinel), axis=0, keepdims=True)
    return iota == first, first


def _router_kernel(x_ref, wh_ref, wl_ref, bias_ref, idx_ref, gate_ref, rank_ref, cnt_ref, run_ref, *, tm):
    i = pl.program_id(0)

    @pl.when(i == 0)
    def _():
        run_ref[...] = jnp.zeros_like(run_ref)

    x = x_ref[...]
    xh = x.astype(BF16)
    xl = (x - xh.astype(F32)).astype(BF16)
    wh = wh_ref[...]
    logits = (lax.dot_general(wh, xh, NT_DIMS, preferred_element_type=F32)
              + lax.dot_general(wh, xl, NT_DIMS, preferred_element_type=F32)
              + lax.dot_general(wl_ref[...], xh, NT_DIMS, preferred_element_type=F32))
    scores = _sigmoid(logits)
    sel = scores + bias_ref[...]
    neg_inf = -jnp.inf

    li = lax.broadcasted_iota(I32, (GROUP_SIZE, tm), 0)
    grp_rows = []
    for g in range(N_GROUPS):
        blk = sel[g * GROUP_SIZE:(g + 1) * GROUP_SIZE, :]
        pick, _ = _pick_first_max(blk, li, GROUP_SIZE)
        m1 = jnp.max(blk, axis=0, keepdims=True)
        m2 = jnp.max(jnp.where(pick, neg_inf, blk), axis=0, keepdims=True)
        grp_rows.append(m1 + m2)
    grp = jnp.concatenate(grp_rows, axis=0)

    gi = lax.broadcasted_iota(I32, (N_GROUPS, tm), 0)
    gsel = jnp.zeros((N_GROUPS, tm), F32)
    cur = grp
    for _ in range(TOPK_GROUPS):
        pick, _ = _pick_first_max(cur, gi, N_GROUPS)
        gsel = jnp.where(pick, 1.0, gsel)
        cur = jnp.where(pick, neg_inf, cur)
    emask = jnp.concatenate(
        [jnp.broadcast_to(gsel[g:g + 1, :], (GROUP_SIZE, tm)) for g in range(N_GROUPS)], axis=0)

    ei = lax.broadcasted_iota(I32, (N_EXPERTS, tm), 0)
    cur = jnp.where(emask > 0.5, sel, neg_inf)
    chosen = jnp.zeros((N_EXPERTS, tm), F32)
    idx_rows, w_rows = [], []
    for _ in range(TOP_K):
        pick, first = _pick_first_max(cur, ei, N_EXPERTS)
        idx_rows.append(first)
        w_rows.append(jnp.sum(jnp.where(pick, scores, 0.0), axis=0, keepdims=True))
        chosen = jnp.where(pick, 1.0, chosen)
        cur = jnp.where(pick, neg_inf, cur)
    idx = jnp.concatenate(idx_rows, axis=0)
    w = jnp.concatenate(w_rows, axis=0)
    gate_ref[...] = w / jnp.sum(w, axis=0, keepdims=True) * ROUTED_SCALE
    idx_ref[...] = idx

    ti = lax.broadcasted_iota(I32, (tm, tm), 0)
    tj = lax.broadcasted_iota(I32, (tm, tm), 1)
    before = jnp.where(ti < tj, 1.0, 0.0).astype(BF16)
    local = jnp.dot(chosen.astype(BF16), before, preferred_element_type=F32)
    total = local + run_ref[:, 0:1]
    rank_rows = [jnp.sum(jnp.where(ei == idx_rows[k], total, 0.0), axis=0, keepdims=True)
                 for k in range(TOP_K)]
    rank_ref[...] = jnp.concatenate(rank_rows, axis=0).astype(I32)
    run_ref[...] = run_ref[...] + jnp.sum(chosen, axis=1, keepdims=True)
    cnt_ref[...] = run_ref[...].astype(I32)


def _router(x, w_hi_t, w_lo_t, bias, tm):
    t, d = x.shape
    return pl.pallas_call(
        functools.partial(_router_kernel, tm=tm),
        grid=(t // tm,),
        in_specs=[pl.BlockSpec((tm, d), lambda i: (i, 0)),
                  pl.BlockSpec((N_EXPERTS, d), lambda i: (0, 0)),
                  pl.BlockSpec((N_EXPERTS, d), lambda i: (0, 0)),
                  pl.BlockSpec((N_EXPERTS, 1), lambda i: (0, 0))],
        out_specs=[pl.BlockSpec((TOP_K, tm), lambda i: (0, i)),
                   pl.BlockSpec((TOP_K, tm), lambda i: (0, i)),
                   pl.BlockSpec((TOP_K, tm), lambda i: (0, i)),
                   pl.BlockSpec((N_EXPERTS, LANES), lambda i: (0, 0))],
        out_shape=[jax.ShapeDtypeStruct((TOP_K, t), I32),
                   jax.ShapeDtypeStruct((TOP_K, t), F32),
                   jax.ShapeDtypeStruct((TOP_K, t), I32),
                   jax.ShapeDtypeStruct((N_EXPERTS, LANES), I32)],
        scratch_shapes=[pltpu.VMEM((N_EXPERTS, LANES), F32)],
        compiler_params=_params(1),
        name="router",
    )(x, w_hi_t, w_lo_t, bias.reshape(N_EXPERTS, 1))


PAD_BITS = tuple(1 << s for s in reversed(range(SUBLANES.bit_length() - 1, EXPERT_ROWS.bit_length() - 1)))


def _dispatch_kernel(zstart_ref, zcount_ref, dest_ref, x_ref, xs_ref, zero_ref, sem, zsem, *, tm):
    i = pl.program_id(0)

    def start_row(t, carry):
        for k in range(TOP_K):
            d = dest_ref[k * tm + t]
            pltpu.make_async_copy(x_ref.at[pl.ds(t, 1), :], xs_ref.at[pl.ds(d, 1), :], sem).start()
        return carry

    lax.fori_loop(0, tm, start_row, 0, unroll=ISSUE_UNROLL)
    for k in range(TOP_K):
        pltpu.make_async_copy(x_ref, xs_ref.at[pl.ds(0, tm), :], sem).wait()

    @pl.when(i == pl.num_programs(0) - 1)
    def _():
        zero_ref[...] = jnp.zeros_like(zero_ref)

        def fill(e, wait):
            def fill_rows(pos, n, pred):
                @pl.when(pred)
                def _():
                    cp = pltpu.make_async_copy(zero_ref.at[pl.ds(0, n), :],
                                               xs_ref.at[pl.ds(pos, n), :], zsem)
                    if wait:
                        cp.wait()
                    else:
                        cp.start()

            pos = zstart_ref[e]
            cnt = zcount_ref[e]
            head = jnp.minimum((-pos) & (SUBLANES - 1), cnt)
            for j in range(SUBLANES - 1):
                fill_rows(pos + j, 1, j < head)
            pos = pos + head
            cnt = cnt - head
            for bit in PAD_BITS:
                has = (cnt & bit) != 0
                fill_rows(pl.multiple_of(pos, SUBLANES), bit, has)
                pos = pos + jnp.where(has, bit, 0)

        def fill_start(e, carry):
            fill(e, False)
            return carry

        def fill_wait(e, carry):
            fill(e, True)
            return carry

        lax.fori_loop(0, N_EXPERTS, fill_start, 0)
        lax.fori_loop(0, N_EXPERTS, fill_wait, 0)


def _dispatch(x, dest_tiles, zstart, zcount, n_rows, tm):
    t, d = x.shape
    return pl.pallas_call(
        functools.partial(_dispatch_kernel, tm=tm),
        grid_spec=pltpu.PrefetchScalarGridSpec(
            num_scalar_prefetch=2,
            grid=(t // tm,),
            in_specs=[pl.BlockSpec((TOP_K * tm,), lambda i, zs, zc: (i,), memory_space=pltpu.SMEM),
                      pl.BlockSpec((tm, d), lambda i, zs, zc: (i, 0))],
            out_specs=pl.BlockSpec(memory_space=pl.ANY),
            scratch_shapes=[pltpu.VMEM((EXPERT_ROWS // 2, d), x.dtype),
                            pltpu.SemaphoreType.DMA(()),
                            pltpu.SemaphoreType.DMA(())]),
        out_shape=jax.ShapeDtypeStruct((n_rows, d), x.dtype),
        compiler_params=_params(1),
        name="moe_dispatch",
    )(zstart, zcount, dest_tiles, x)


def _expert_kernel(be_ref, nxt_ref, nv_ref, x_ref, wg_hbm, wu_hbm, wd_hbm, y_ref,
                   wgs_ref, wus_ref, wds_ref, wgb_ref, wub_ref, wdb_ref, sems):
    b = pl.program_id(0)
    e = be_ref[b]
    prev = be_ref[jnp.maximum(b - 1, 0)]
    fresh = jnp.logical_or(b == 0, e != prev)

    def weight_copies(expert):
        return (pltpu.make_async_copy(wg_hbm.at[expert], wgs_ref, sems.at[0]),
                pltpu.make_async_copy(wu_hbm.at[expert], wus_ref, sems.at[1]),
                pltpu.make_async_copy(wd_hbm.at[expert], wds_ref, sems.at[2]))

    @pl.when(b == 0)
    def _():
        for cp in weight_copies(e):
            cp.start()

    @pl.when(fresh)
    def _():
        for cp in weight_copies(e):
            cp.wait()
        wgb_ref[...] = wgs_ref[...].astype(BF16)
        wub_ref[...] = wus_ref[...].astype(BF16)
        wdb_ref[...] = wds_ref[...].astype(BF16)
        nxt = nxt_ref[b]

        @pl.when(nxt >= 0)
        def _():
            for cp in weight_copies(nxt):
                cp.start()

    @pl.when(b < nv_ref[0])
    def _():
        half = x_ref.shape[1]
        x_lo, x_hi = _unpack_halves(x_ref[...])
        x_lo = x_lo.astype(BF16)
        x_hi = x_hi.astype(BF16)
        hg = (jnp.dot(x_lo, wgb_ref[:half, :], preferred_element_type=F32)
              + jnp.dot(x_hi, wgb_ref[half:, :], preferred_element_type=F32))
        hu = (jnp.dot(x_lo, wub_ref[:half, :], preferred_element_type=F32)
              + jnp.dot(x_hi, wub_ref[half:, :], preferred_element_type=F32))
        act = (_silu(hg) * hu).astype(BF16)
        y_ref[...] = _pack_halves(jnp.dot(act, wdb_ref[...], preferred_element_type=F32))


def _experts(xs, wg, wu, wd, block_expert, next_expert, n_valid):
    n_rows = xs.shape[0]
    d = wg.shape[1]
    de = wg.shape[2]
    nb = n_rows // EXPERT_ROWS
    xmap = lambda b, be, nx, nv: (jnp.minimum(b, nv[0] - 1), 0)
    return pl.pallas_call(
        _expert_kernel,
        grid_spec=pltpu.PrefetchScalarGridSpec(
            num_scalar_prefetch=3,
            grid=(nb,),
            in_specs=[pl.BlockSpec((EXPERT_ROWS, d // 2), xmap),
                      pl.BlockSpec(memory_space=pl.ANY),
                      pl.BlockSpec(memory_space=pl.ANY),
                      pl.BlockSpec(memory_space=pl.ANY)],
            out_specs=pl.BlockSpec((EXPERT_ROWS, d // 2), xmap),
            scratch_shapes=[pltpu.VMEM((d, de), F32), pltpu.VMEM((d, de), F32), pltpu.VMEM((de, d), F32),
                            pltpu.VMEM((d, de), BF16), pltpu.VMEM((d, de), BF16), pltpu.VMEM((de, d), BF16),
                            pltpu.SemaphoreType.DMA((3,))]),
        out_shape=jax.ShapeDtypeStruct((n_rows, d // 2), I32),
        compiler_params=_params(1),
        name="moe_experts",
    )(block_expert, next_expert, n_valid, xs, wg, wu, wd)


def _combine_kernel(dest_ref, x_ref, gate_ref, ys_ref, swg_ref, swu_ref, swd_ref, g_ref, b_ref,
                    op_ref, os_ref, buf_ref, sem, *, tm, n_prompt_tiles):
    i = pl.program_id(0)

    def start_row(t, carry):
        for k in range(TOP_K):
            d = dest_ref[k * tm + t]
            pltpu.make_async_copy(ys_ref.at[pl.ds(d, 1), :], buf_ref.at[k, pl.ds(t, 1), :], sem).start()
        return carry

    lax.fori_loop(0, tm, start_row, 0, unroll=ISSUE_UNROLL)
    x = x_ref[...]
    xb = x.astype(BF16)
    act = (_silu(jnp.dot(xb, swg_ref[...], preferred_element_type=F32))
           * jnp.dot(xb, swu_ref[...], preferred_element_type=F32)).astype(BF16)
    shared = jnp.dot(act, swd_ref[...], preferred_element_type=F32)
    for k in range(TOP_K):
        pltpu.make_async_copy(ys_ref.at[pl.ds(0, tm), :], buf_ref.at[k], sem).wait()
    gate = gate_ref[...]
    acc_lo = None
    for k in range(TOP_K):
        y_lo, y_hi = _unpack_halves(buf_ref[k])
        gk = gate[:, k:k + 1]
        acc_lo = gk * y_lo if acc_lo is None else acc_lo + gk * y_lo
        acc_hi = gk * y_hi if k == 0 else acc_hi + gk * y_hi
    moe = jnp.concatenate([acc_lo, acc_hi], axis=-1) + shared
    out = _layer_norm(DEEPNORM_ALPHA * x + moe, g_ref[...], b_ref[...])

    @pl.when(i < n_prompt_tiles)
    def _():
        op_ref[...] = out

    @pl.when(i >= n_prompt_tiles)
    def _():
        os_ref[...] = out


def _combine(x, gate_t, dest_tiles, ys, swg, swu, swd, g, b, tm, n_prompt_rows):
    t, d = x.shape
    ds = swg.shape[1]
    npt = n_prompt_rows // tm
    const = lambda i: (0, 0)
    return pl.pallas_call(
        functools.partial(_combine_kernel, tm=tm, n_prompt_tiles=npt),
        grid=(t // tm,),
        in_specs=[pl.BlockSpec((TOP_K * tm,), lambda i: (i,), memory_space=pltpu.SMEM),
                  pl.BlockSpec((tm, d), lambda i: (i, 0)),
                  pl.BlockSpec((tm, TOP_K), lambda i: (i, 0)),
                  pl.BlockSpec(memory_space=pl.ANY),
                  pl.BlockSpec((d, ds), const),
                  pl.BlockSpec((d, ds), const),
                  pl.BlockSpec((ds, d), const),
                  pl.BlockSpec((1, d), const),
                  pl.BlockSpec((1, d), const)],
        out_specs=[pl.BlockSpec((tm, d), lambda i: (jnp.minimum(i, npt - 1), 0)),
                   pl.BlockSpec((tm, d), lambda i: (jnp.maximum(i - npt, 0), 0))],
        out_shape=[jax.ShapeDtypeStruct((n_prompt_rows, d), F32),
                   jax.ShapeDtypeStruct((t - n_prompt_rows, d), F32)],
        scratch_shapes=[pltpu.VMEM((TOP_K, tm, d // 2), I32), pltpu.SemaphoreType.DMA(())],
        compiler_params=_params(1),
        name="moe_combine",
    )(dest_tiles, x, gate_t, ys, swg, swu, swd, g.reshape(1, d), b.reshape(1, d))


def _tile_major(a, tm):
    k, t = a.shape
    return a.reshape(k, t // tm, tm).transpose(1, 0, 2).reshape(-1)


def _moe_ln(x, x_packed, router_w, router_bias, wg, wu, wd, swg, swu, swd, g, b, n_prompt_rows):
    t, d = x.shape
    w_t = router_w.T
    w_hi = w_t.astype(BF16)
    w_lo = (w_t - w_hi.astype(F32)).astype(BF16)
    tr = _divisor_tile(t, 512, LANES)
    idx, gate, rank, cnt = _router(x, w_hi, w_lo, router_bias, tr)

    counts = cnt[:, 0]
    padded = (counts + EXPERT_ROWS - 1) // EXPERT_ROWS * EXPERT_ROWS
    pad_end = jnp.cumsum(padded)
    pad_start = pad_end - padded
    experts = jnp.arange(N_EXPERTS, dtype=I32)
    onehot = idx[None] == experts[:, None, None]
    dest = jnp.sum(jnp.where(onehot, pad_start[:, None, None], 0), axis=0) + rank
    n_blocks = (t * TOP_K + N_EXPERTS * (EXPERT_ROWS - 1)) // EXPERT_ROWS + 1
    n_valid = (pad_end[-1] // EXPERT_ROWS).astype(I32)
    blk = jnp.minimum(jnp.arange(n_blocks, dtype=I32), n_valid - 1)
    block_expert = jnp.minimum(
        jnp.sum(pad_end[None, :] <= (blk * EXPERT_ROWS)[:, None], axis=1), N_EXPERTS - 1).astype(I32)
    later_used = (experts[None, :] > experts[:, None]) & (counts[None, :] > 0)
    next_used = jnp.min(jnp.where(later_used, experts[None, :], N_EXPERTS), axis=1)
    next_used = jnp.where(next_used == N_EXPERTS, -1, next_used)
    next_expert = jnp.sum(
        jnp.where(block_expert[:, None] == experts[None, :], next_used[None, :], 0), axis=1).astype(I32)

    td = _divisor_tile(t, 256, SUBLANES)
    xs = _dispatch(x_packed, _tile_major(dest, td), (pad_start + counts).astype(I32),
                   (padded - counts).astype(I32), n_blocks * EXPERT_ROWS, td)
    ys = _experts(xs, wg, wu, wd, block_expert, next_expert, n_valid.reshape(1))
    tc = _divisor_tile(math.gcd(n_prompt_rows, t - n_prompt_rows), 128, SUBLANES)
    return _combine(x, gate.T, _tile_major(dest, tc), ys, swg.astype(BF16), swu.astype(BF16),
                    swd.astype(BF16), g, b, tc, n_prompt_rows)


def kernel(x_prompt, x_sample, mem_prompt, state_gla, cache_swa_k, cache_swa_v, cache_mem_k, cache_mem_v,
           w_in, gla_w_gate2, gla_b_gate, gla_norm_g, swa_sinks, w_mix_out, ln1_g, ln1_b,
           mem_wq, mem_wk, mem_wv, mem_wo, ln2_g, ln2_b,
           router_w, router_bias, exp_w_gate, exp_w_up, exp_w_down, sh_w_gate, sh_w_up, sh_w_down,
           ln3_g, ln3_b):
    nbp, seq, d = x_prompt.shape
    nbs, dec, _ = x_sample.shape
    tp, ts = nbp * seq, nbs * dec
    t = tp + ts
    mem_tokens = mem_prompt.shape[1]
    assert state_gla.shape[0] == DEPTH and seq % SWA_WINDOW == 0 and dec % SUBLANES == 0
    assert cache_swa_k.shape[2] == SWA_WINDOW and tp % dec == 0
    tm = _divisor_tile(math.gcd(tp, ts), 512, 16)
    l = 0

    xp = x_prompt.reshape(tp, d)
    xs = x_sample.reshape(ts, d)

    wi = w_in[l]
    o = 0
    seg = []
    for size in (GLA_QK, GLA_QK, GLA_VW, GLA_VW, GLA_GATE_RANK, SWA_QW, SWA_KVW, SWA_KVW):
        seg.append(wi[:, o:o + size])
        o += size
    gq, gk, gv, gr, gg, sq, sk, sv = seg
    w_in_r = jnp.concatenate(
        [gq, gk, gv, gr, sq, sk, sv, gg,
         jnp.zeros((d, IN_COLS_PADDED - COL_GG - GLA_GATE_RANK), wi.dtype)], axis=1).astype(BF16)
    z = _matmul_split(xp, xs, w_in_r, tm, IN_COLS_PADDED // 2, F32)

    wa2_pad = jnp.concatenate(
        [gla_w_gate2[l], jnp.zeros((LANES - GLA_GATE_RANK, GLA_QK), F32)], axis=0)
    zero_state = jnp.zeros((nbp, GLA_HEADS, GLA_DV, GLA_DK), F32)
    mix, st_p = _gla(z, wa2_pad, gla_b_gate[l], gla_norm_g[l], zero_state, None,
                     nb=nbp, n_chunks=seq // CHUNK, rows=CHUNK, row0=0)
    mix, st_s = _gla(z, wa2_pad, gla_b_gate[l], gla_norm_g[l], jnp.swapaxes(state_gla[l], -1, -2), mix,
                     nb=nbs, n_chunks=1, rows=dec, row0=tp)
    nblk = seq // SWA_WINDOW
    prev_p = lambda col: (lambda b, i: (b * nblk + jnp.maximum(i - 1, 0), col))
    mix = _swa(z, swa_sinks[l], z, z, (prev_p(COL_SK // SWA_KVW), prev_p(COL_SV // SWA_KVW)), mix,
               nb=nbp, n_blocks=nblk, rows=SWA_WINDOW, row0=0, q_base=0, q_stride=SWA_WINDOW)
    ck = cache_swa_k[l].reshape(nbs * SWA_WINDOW, SWA_KVW)
    cv = cache_swa_v[l].reshape(nbs * SWA_WINDOW, SWA_KVW)
    prev_s = lambda b, i: (b, 0)
    mix = _swa(z, swa_sinks[l], ck, cv, (prev_s, prev_s), mix,
               nb=nbs, n_blocks=1, rows=dec, row0=tp, q_base=PAST_LEN, q_stride=0)
    h1 = _matmul_res_ln(mix, w_mix_out[l].astype(BF16), (xp, xs), ln1_g[l], ln1_b[l], tm)

    mem = mem_prompt.reshape(nbp * mem_tokens, d)
    tmem = _divisor_tile(nbp * mem_tokens, 512, 16)
    mk = _matmul(mem, mem_wk[l].astype(BF16), tmem, _divisor_tile(d, 1024, LANES), F32)
    mv = _matmul(mem, mem_wv[l].astype(BF16), tmem, _divisor_tile(d, 1024, LANES), F32)
    q = _matmul(h1, mem_wq[l].astype(BF16), tm, d, BF16)
    tq = _divisor_tile(seq, 512, 16)
    att = _mem_attn(q, mk, mv, None, nb=nbp, n_tiles=seq // tq, rows=tq, row0=0, mem_tokens=mem_tokens)
    att = _mem_attn(q, cache_mem_k[l].reshape(nbs * mem_tokens, d), cache_mem_v[l].reshape(nbs * mem_tokens, d),
                    att, nb=nbs, n_tiles=1, rows=dec, row0=tp, mem_tokens=mem_tokens)
    h2, h2_packed = _matmul_res_ln(att, mem_wo[l].astype(BF16), h1, ln2_g[l], ln2_b[l], tm, emit_packed=True)

    out_p, out_s = _moe_ln(h2, h2_packed, router_w[l], router_bias[l], exp_w_gate[l], exp_w_up[l],
                           exp_w_down[l], sh_w_gate[l], sh_w_up[l], sh_w_down[l], ln3_g[l], ln3_b[l], tp)

    y_prompt = out_p.reshape(nbp, seq, d)
    y_sample = out_s.reshape(nbs, dec, d)
    kv = z[:, COL_SK:COL_SK + 2 * SWA_KVW]
    kvp = kv[:tp].reshape(nbp, seq, 2 * SWA_KVW)[:, seq - SWA_WINDOW:]
    kvs = kv[tp:].reshape(nbs, dec, 2 * SWA_KVW)
    kv_shape = lambda a: a.reshape(a.shape[0], a.shape[1], SWA_KV_HEADS, SWA_HEAD_DIM)[None]
    mem_shape = lambda a: a.reshape(nbp, mem_tokens, MEM_HEADS, d // MEM_HEADS)[None]
    return (y_prompt, y_sample,
            jnp.swapaxes(st_p, -1, -2)[None],
            kv_shape(kvp[..., :SWA_KVW]), kv_shape(kvp[..., SWA_KVW:]),
            mem_shape(mk), mem_shape(mv),
            jnp.swapaxes(st_s, -1, -2)[None],
            kv_shape(kvs[..., :SWA_KVW]), kv_shape(kvs[..., SWA_KVW:]))
```

```python
import functools
import math

import jax
import jax.numpy as jnp
from jax import lax
from jax.experimental import pallas as pl
from jax.experimental.pallas import tpu as pltpu

F32 = jnp.float32
BF16 = jnp.bfloat16
I32 = jnp.int32

CHUNK = 64
PAST_LEN = 2048
GLA_HEADS = 8
GLA_DK = 64
GLA_DV = 128
GLA_QK = GLA_HEADS * GLA_DK
GLA_VW = GLA_HEADS * GLA_DV
GLA_GATE_RANK = 16
GLA_TAU = 16.0
SWA_HEADS = 16
SWA_KV_HEADS = 2
SWA_GROUP = SWA_HEADS // SWA_KV_HEADS
SWA_HEAD_DIM = 64
SWA_QW = SWA_HEADS * SWA_HEAD_DIM
SWA_KVW = SWA_KV_HEADS * SWA_HEAD_DIM
SWA_WINDOW = 128
WINDOW_CHUNKS = SWA_WINDOW // CHUNK
MEM_HEADS = 4
N_EXPERTS = 64
TOP_K = 8
N_GROUPS = 8
GROUP_SIZE = N_EXPERTS // N_GROUPS
TOPK_GROUPS = 4
ROUTED_SCALE = 2.5
DEPTH = 1
DEEPNORM_ALPHA = (2 * DEPTH) ** 0.25
LN_EPS = 1e-5
RMS_EPS = 1e-6

LANES = 128
SUBLANES = 8
VMEM_LIMIT_BYTES = 56 * 1024 * 1024

COL_GQ = 0
COL_GK = COL_GQ + GLA_QK
COL_GV = COL_GK + GLA_QK
COL_GR = COL_GV + GLA_VW
COL_SQ = COL_GR + GLA_VW
COL_SK = COL_SQ + SWA_QW
COL_SV = COL_SK + SWA_KVW
COL_GG = COL_SV + SWA_KVW
MXU_COLS = 256
IN_COLS_PADDED = -(-(COL_GG + LANES) // (2 * MXU_COLS)) * (2 * MXU_COLS)

EXPERT_ROWS = 256
ISSUE_UNROLL = 8
GLA_STEP_ROWS = 256

NT_DIMS = (((1,), (1,)), ((), ()))
TN_DIMS = (((0,), (0,)), ((), ()))


def _params(n_axes):
    return pltpu.CompilerParams(dimension_semantics=("arbitrary",) * n_axes,
                                vmem_limit_bytes=VMEM_LIMIT_BYTES)


def _sigmoid(x):
    return 1.0 / (1.0 + jnp.exp(-x))


def _silu(x):
    return x * _sigmoid(x)


def _layer_norm(x, g, b):
    mu = jnp.mean(x, axis=-1, keepdims=True)
    xc = x - mu
    var = jnp.mean(xc * xc, axis=-1, keepdims=True)
    return xc * lax.rsqrt(var + LN_EPS) * g + b


HI_MASK = -65536


def _round_to_bf16_bits(x):
    b = lax.bitcast_convert_type(x, I32)
    return b + (0x7FFF + (lax.shift_right_logical(b, 16) & 1))


def _pack_halves(x):
    c = x.shape[1] // 2
    lo = lax.shift_right_logical(_round_to_bf16_bits(x[:, :c]), 16)
    hi = _round_to_bf16_bits(x[:, c:]) & HI_MASK
    return lo | hi


def _unpack_halves(w):
    return (lax.bitcast_convert_type(lax.shift_left(w, 16), F32),
            lax.bitcast_convert_type(w & HI_MASK, F32))


def _divisor_tile(n, pref, mult):
    t = min(pref, n)
    while t > mult and (n % t or t % mult):
        t -= mult
    assert n % t == 0 and t % mult == 0, (n, pref, mult)
    return t


def _mm_kernel(x_ref, w_ref, o_ref, xb_ref):
    @pl.when(pl.program_id(1) == 0)
    def _():
        xb_ref[...] = x_ref[...].astype(BF16)

    o_ref[...] = jnp.dot(xb_ref[...], w_ref[...], preferred_element_type=F32).astype(o_ref.dtype)


def _matmul(x, w, tm, tn, out_dtype):
    m, k = x.shape
    n = w.shape[1]
    return pl.pallas_call(
        _mm_kernel,
        grid=(m // tm, n // tn),
        in_specs=[pl.BlockSpec((tm, k), lambda i, j: (i, 0)),
                  pl.BlockSpec((k, tn), lambda i, j: (0, j))],
        out_specs=pl.BlockSpec((tm, tn), lambda i, j: (i, j)),
        out_shape=jax.ShapeDtypeStruct((m, n), out_dtype),
        scratch_shapes=[pltpu.VMEM((tm, k), BF16)],
        compiler_params=_params(2),
        name="matmul",
    )(x, w)


def _split_rows_specs(tm, k, n_first_tiles, n_grid_axes):
    if n_grid_axes == 1:
        return [pl.BlockSpec((tm, k), lambda i: (jnp.minimum(i, n_first_tiles - 1), 0)),
                pl.BlockSpec((tm, k), lambda i: (jnp.maximum(i - n_first_tiles, 0), 0))]
    return [pl.BlockSpec((tm, k), lambda i, j: (jnp.minimum(i, n_first_tiles - 1), 0)),
            pl.BlockSpec((tm, k), lambda i, j: (jnp.maximum(i - n_first_tiles, 0), 0))]


def _mm_split_kernel(xa_ref, xb_ref, w_ref, o_ref, xbf_ref, *, n_first_tiles):
    i = pl.program_id(0)
    first_col = pl.program_id(1) == 0

    @pl.when(first_col & (i < n_first_tiles))
    def _():
        xbf_ref[...] = xa_ref[...].astype(BF16)

    @pl.when(first_col & (i >= n_first_tiles))
    def _():
        xbf_ref[...] = xb_ref[...].astype(BF16)

    o_ref[...] = jnp.dot(xbf_ref[...], w_ref[...], preferred_element_type=F32).astype(o_ref.dtype)


def _matmul_split(xa, xb, w, tm, tn, out_dtype):
    ma, k = xa.shape
    m = ma + xb.shape[0]
    n = w.shape[1]
    return pl.pallas_call(
        functools.partial(_mm_split_kernel, n_first_tiles=ma // tm),
        grid=(m // tm, n // tn),
        in_specs=_split_rows_specs(tm, k, ma // tm, 2) + [pl.BlockSpec((k, tn), lambda i, j: (0, j))],
        out_specs=pl.BlockSpec((tm, tn), lambda i, j: (i, j)),
        out_shape=jax.ShapeDtypeStruct((m, n), out_dtype),
        scratch_shapes=[pltpu.VMEM((tm, k), BF16)],
        compiler_params=_params(2),
        name="matmul_split",
    )(xa, xb, w)


def _mm_res_ln_kernel(x_ref, w_ref, *refs, n_res, n_first_tiles):
    res_refs, (g_ref, b_ref, o_ref, *packed_ref) = refs[:n_res], refs[n_res:]
    y = jnp.dot(x_ref[...], w_ref[...], preferred_element_type=F32)

    def finish(res):
        h = _layer_norm(DEEPNORM_ALPHA * res + y, g_ref[...], b_ref[...])
        o_ref[...] = h
        if packed_ref:
            packed_ref[0][...] = _pack_halves(h)

    if n_res == 1:
        finish(res_refs[0][...])
    else:
        i = pl.program_id(0)

        @pl.when(i < n_first_tiles)
        def _():
            finish(res_refs[0][...])

        @pl.when(i >= n_first_tiles)
        def _():
            finish(res_refs[1][...])


def _matmul_res_ln(x, w, res, g, b, tm, emit_packed=False):
    m, k = x.shape
    n = w.shape[1]
    if isinstance(res, tuple):
        n_first_tiles = res[0].shape[0] // tm
        res_specs = _split_rows_specs(tm, n, n_first_tiles, 1)
    else:
        n_first_tiles = 0
        res = (res,)
        res_specs = [pl.BlockSpec((tm, n), lambda i: (i, 0))]
    out_specs = [pl.BlockSpec((tm, n), lambda i: (i, 0))]
    out_shape = [jax.ShapeDtypeStruct((m, n), F32)]
    if emit_packed:
        out_specs.append(pl.BlockSpec((tm, n // 2), lambda i: (i, 0)))
        out_shape.append(jax.ShapeDtypeStruct((m, n // 2), I32))
    outs = pl.pallas_call(
        functools.partial(_mm_res_ln_kernel, n_res=len(res), n_first_tiles=n_first_tiles),
        grid=(m // tm,),
        in_specs=[pl.BlockSpec((tm, k), lambda i: (i, 0)),
                  pl.BlockSpec((k, n), lambda i: (0, 0))] + res_specs + [
                  pl.BlockSpec((1, n), lambda i: (0, 0)),
                  pl.BlockSpec((1, n), lambda i: (0, 0))],
        out_specs=out_specs,
        out_shape=out_shape,
        compiler_params=_params(1),
        name="matmul_res_ln",
    )(x, w, *res, g.reshape(1, n), b.reshape(1, n))
    return outs if emit_packed else outs[0]


def _split3_bf16(x):
    hi = x.astype(BF16)
    r1 = x - hi.astype(F32)
    mid = r1.astype(BF16)
    lo = (r1 - mid.astype(F32)).astype(BF16)
    return jnp.concatenate([hi, mid, lo], axis=-1)


def _gla_kernel(q_ref, k_ref, v_ref, r_ref, gg_ref, wa2_ref, ba_ref, ng_ref, s0_ref, mix_in_ref,
                o_ref, sfin_ref, st_ref, *, rows, chunk):
    del mix_in_ref
    n_sub = rows // chunk
    shift = chunk.bit_length() - 1
    assert chunk == 1 << shift

    @pl.when(pl.program_id(1) == 0)
    def _():
        st_ref[...] = s0_ref[0]

    gg = gg_ref[...]
    gg_hi = gg.astype(BF16)
    gg_lo = (gg - gg_hi.astype(F32)).astype(BF16)
    gate = jnp.dot(jnp.concatenate([gg_hi, gg_hi, gg_lo], axis=-1), wa2_ref[...],
                   preferred_element_type=F32) + ba_ref[...]
    log_a = (jnp.minimum(gate, 0.0) - jnp.log1p(jnp.exp(-jnp.abs(gate)))) / GLA_TAU

    ri = lax.broadcasted_iota(I32, (rows, rows), 0)
    ci = lax.broadcasted_iota(I32, (rows, rows), 1)
    same_chunk = lax.shift_right_logical(ri, shift) == lax.shift_right_logical(ci, shift)
    causal = same_chunk & (ri >= ci)
    ones = jnp.concatenate([jnp.where(causal, 1.0, 0.0), jnp.where(same_chunk, 1.0, 0.0)], axis=0).astype(BF16)
    sums = jnp.dot(ones, _split3_bf16(log_a), preferred_element_type=F32)
    sums = sums[:, :GLA_QK] + sums[:, GLA_QK:2 * GLA_QK] + sums[:, 2 * GLA_QK:]
    bcum = sums[:rows]
    b_last = sums[rows:]
    q = q_ref[...] * (GLA_DK ** -0.5)
    k = k_ref[...]
    q_in = (q * jnp.exp(bcum)).astype(BF16)
    k_in = (k * jnp.exp(-bcum)).astype(BF16)
    k_out = (k * jnp.exp(b_last - bcum)).astype(BF16)
    decay = jnp.exp(b_last)
    ng = ng_ref[...]
    blk_r = lax.shift_right_logical(lax.broadcasted_iota(I32, (rows, n_sub * GLA_DK), 0), shift)
    blk_c = lax.shift_right_logical(lax.broadcasted_iota(I32, (rows, n_sub * GLA_DK), 1),
                                    GLA_DK.bit_length() - 1)
    for h in range(GLA_HEADS):
        ks = slice(h * GLA_DK, (h + 1) * GLA_DK)
        vs = slice(h * GLA_DV, (h + 1) * GLA_DV)
        vh = v_ref[:, vs].astype(BF16)
        qh = q_in[:, ks]
        attn = lax.dot_general(qh, k_in[:, ks], NT_DIMS, preferred_element_type=F32)
        attn = jnp.where(causal, attn, 0.0).astype(BF16)
        o = jnp.dot(attn, vh, preferred_element_type=F32)
        ko = k_out[:, ks]
        if n_sub > 1:
            ko = jnp.where(blk_r == blk_c, jnp.concatenate([ko] * n_sub, axis=-1), 0.0)
        upd_t = lax.dot_general(vh, ko, TN_DIMS, preferred_element_type=F32)
        st = st_ref[h]
        states = []
        for c in range(n_sub):
            states.append(st.astype(BF16))
            st = st * decay[c * chunk:c * chunk + 1, ks] + upd_t[:, c * GLA_DK:(c + 1) * GLA_DK]
        st_ref[h] = st
        o_all = lax.dot_general(qh, jnp.concatenate(states, axis=0), NT_DIMS, preferred_element_type=F32)
        o = o + jnp.concatenate(
            [o_all[c * chunk:(c + 1) * chunk, c * GLA_DV:(c + 1) * GLA_DV] for c in range(n_sub)], axis=0)
        o = o * lax.rsqrt(jnp.mean(o * o, axis=-1, keepdims=True) + RMS_EPS) * ng
        o_ref[:, vs] = (o * _silu(r_ref[:, vs])).astype(o_ref.dtype)
    sfin_ref[0] = st_ref[...]


def _gla(z, wa2_stack, ba, norm_g, s0_t, mix_in, *, nb, n_chunks, rows, chunk, row0):
    t = z.shape[0]
    rb0 = row0 // rows
    rmap = lambda colblk: (lambda b, c: (rb0 + b * n_chunks + c, colblk))
    in_specs = [
        pl.BlockSpec((rows, GLA_QK), rmap(COL_GQ // GLA_QK)),
        pl.BlockSpec((rows, GLA_QK), rmap(COL_GK // GLA_QK)),
        pl.BlockSpec((rows, GLA_VW), rmap(COL_GV // GLA_VW)),
        pl.BlockSpec((rows, GLA_VW), rmap(COL_GR // GLA_VW)),
        pl.BlockSpec((rows, LANES), rmap(COL_GG // LANES)),
        pl.BlockSpec((3 * LANES, GLA_QK), lambda b, c: (0, 0)),
        pl.BlockSpec((1, GLA_QK), lambda b, c: (0, 0)),
        pl.BlockSpec((1, GLA_DV), lambda b, c: (0, 0)),
        pl.BlockSpec((1, GLA_HEADS, GLA_DV, GLA_DK), lambda b, c: (b, 0, 0, 0)),
    ]
    args = [z, z, z, z, z, wa2_stack, ba.reshape(1, GLA_QK), norm_g.reshape(1, GLA_DV), s0_t]
    aliases = {}
    if mix_in is None:
        mix_in = jnp.zeros((SUBLANES, LANES), BF16)
        in_specs.append(pl.BlockSpec(memory_space=pl.ANY))
    else:
        in_specs.append(pl.BlockSpec(memory_space=pl.ANY))
        aliases = {len(args): 0}
    args.append(mix_in)
    mix, s_fin = pl.pallas_call(
        functools.partial(_gla_kernel, rows=rows, chunk=chunk),
        grid=(nb, n_chunks),
        in_specs=in_specs,
        out_specs=[pl.BlockSpec((rows, GLA_VW), rmap(0)),
                   pl.BlockSpec((1, GLA_HEADS, GLA_DV, GLA_DK), lambda b, c: (b, 0, 0, 0))],
        out_shape=[jax.ShapeDtypeStruct((t, GLA_VW + SWA_QW), BF16),
                   jax.ShapeDtypeStruct((nb, GLA_HEADS, GLA_DV, GLA_DK), F32)],
        scratch_shapes=[pltpu.VMEM((GLA_HEADS, GLA_DV, GLA_DK), F32)],
        input_output_aliases=aliases,
        compiler_params=_params(2),
        name="gla",
    )(*args)
    return mix, s_fin


def _swa_kernel(sinks_ref, q_ref, kc_ref, vc_ref, kp_ref, vp_ref, mix_in_ref, o_ref, *,
                rows, q_base, q_stride):
    del mix_in_ref
    i = pl.program_id(1)
    q0 = q_base + i * q_stride
    nk = SWA_WINDOW + rows
    keys = jnp.concatenate([kp_ref[...], kc_ref[...]], axis=0).astype(BF16)
    vals = jnp.concatenate([vp_ref[...], vc_ref[...]], axis=0).astype(BF16)
    sr = SWA_GROUP * rows
    row = lax.broadcasted_iota(I32, (sr, nk), 0)
    qpos = q0 + (row & (rows - 1))
    kpos = q0 - SWA_WINDOW + lax.broadcasted_iota(I32, (sr, nk), 1)
    dist = jnp.abs(qpos - kpos).astype(F32)
    qc = lax.shift_right_arithmetic(qpos, CHUNK.bit_length() - 1)
    kc = lax.shift_right_arithmetic(kpos, CHUNK.bit_length() - 1)
    allowed = (kpos >= 0) & (kc <= qc) & (kc >= qc - WINDOW_CHUNKS)
    assert rows & (rows - 1) == 0
    head_in_group = lax.shift_right_logical(lax.broadcasted_iota(I32, (sr, 1), 0), rows.bit_length() - 1)
    for g in range(SWA_KV_HEADS):
        slope = jnp.zeros((sr, 1), F32)
        sink = jnp.zeros((sr, 1), F32)
        for j in range(SWA_GROUP):
            h = g * SWA_GROUP + j
            slope = jnp.where(head_in_group == j, 2.0 ** (-8.0 * (h + 1) / SWA_HEADS), slope)
            sink = jnp.where(head_in_group == j, sinks_ref[h], sink)
        kv = slice(g * SWA_HEAD_DIM, (g + 1) * SWA_HEAD_DIM)
        qg = jnp.concatenate(
            [q_ref[:, (g * SWA_GROUP + j) * SWA_HEAD_DIM:(g * SWA_GROUP + j + 1) * SWA_HEAD_DIM]
             for j in range(SWA_GROUP)], axis=0).astype(BF16)
        s = lax.dot_general(qg, keys[:, kv], NT_DIMS, preferred_element_type=F32)
        s = s * (SWA_HEAD_DIM ** -0.5) - slope * dist
        s = jnp.where(allowed, s, -jnp.inf)
        m = jnp.maximum(jnp.max(s, axis=-1, keepdims=True), sink)
        p = jnp.exp(s - m)
        denom = jnp.sum(p, axis=-1, keepdims=True) + jnp.exp(sink - m)
        o = jnp.dot(p.astype(BF16), vals[:, kv], preferred_element_type=F32) / denom
        for j in range(0, SWA_GROUP, 2):
            c0 = (g * SWA_GROUP + j) * SWA_HEAD_DIM
            o_ref[:, c0:c0 + LANES] = jnp.concatenate(
                [o[j * rows:(j + 1) * rows], o[(j + 1) * rows:(j + 2) * rows]], axis=-1).astype(o_ref.dtype)


def _swa(z, sinks, k_prev, v_prev, prev_map, mix_in, *, nb, n_blocks, rows, row0, q_base, q_stride):
    rb0 = row0 // rows
    rmap = lambda colblk: (lambda b, i: (rb0 + b * n_blocks + i, colblk))
    return pl.pallas_call(
        functools.partial(_swa_kernel, rows=rows, q_base=q_base, q_stride=q_stride),
        grid=(nb, n_blocks),
        in_specs=[pl.BlockSpec(memory_space=pltpu.SMEM),
                  pl.BlockSpec((rows, SWA_QW), rmap(COL_SQ // SWA_QW)),
                  pl.BlockSpec((rows, SWA_KVW), rmap(COL_SK // SWA_KVW)),
                  pl.BlockSpec((rows, SWA_KVW), rmap(COL_SV // SWA_KVW)),
                  pl.BlockSpec((SWA_WINDOW, SWA_KVW), prev_map[0]),
                  pl.BlockSpec((SWA_WINDOW, SWA_KVW), prev_map[1]),
                  pl.BlockSpec(memory_space=pl.ANY)],
        out_specs=pl.BlockSpec((rows, SWA_QW), rmap(GLA_VW // SWA_QW)),
        out_shape=jax.ShapeDtypeStruct(mix_in.shape, mix_in.dtype),
        input_output_aliases={6: 0},
        compiler_params=_params(2),
        name="swa",
    )(sinks, z, z, z, k_prev, v_prev, mix_in)


def _mem_attn_kernel(q_ref, k_ref, v_ref, o_in_ref, o_ref, *, head_dim):
    del o_in_ref
    for h in range(MEM_HEADS):
        hs = slice(h * head_dim, (h + 1) * head_dim)
        kh = k_ref[:, hs].astype(BF16)
        vh = v_ref[:, hs].astype(BF16)
        s = lax.dot_general(q_ref[:, hs], kh, NT_DIMS, preferred_element_type=F32) * (head_dim ** -0.5)
        m = jnp.max(s, axis=-1, keepdims=True)
        p = jnp.exp(s - m)
        denom = jnp.sum(p, axis=-1, keepdims=True)
        o = jnp.dot(p.astype(BF16), vh, preferred_element_type=F32) / denom
        o_ref[:, hs] = o.astype(o_ref.dtype)


def _mem_attn(q, mk, mv, o_in, *, nb, n_tiles, rows, row0, mem_tokens):
    t, d = q.shape
    rb0 = row0 // rows
    qmap = lambda b, i: (rb0 + b * n_tiles + i, 0)
    in_specs = [pl.BlockSpec((rows, d), qmap),
                pl.BlockSpec((mem_tokens, d), lambda b, i: (b, 0)),
                pl.BlockSpec((mem_tokens, d), lambda b, i: (b, 0)),
                pl.BlockSpec(memory_space=pl.ANY)]
    aliases = {}
    if o_in is None:
        o_in = jnp.zeros((SUBLANES, LANES), BF16)
    else:
        aliases = {3: 0}
    return pl.pallas_call(
        functools.partial(_mem_attn_kernel, head_dim=d // MEM_HEADS),
        grid=(nb, n_tiles),
        in_specs=in_specs,
        out_specs=pl.BlockSpec((rows, d), qmap),
        out_shape=jax.ShapeDtypeStruct((t, d), BF16),
        input_output_aliases=aliases,
        compiler_params=_params(2),
        name="mem_attn",
    )(q, mk, mv, o_in)


def _pick_first_max(cur, iota, sentinel):
    mx = jnp.max(cur, axis=0, keepdims=True)
    first = jnp.min(jnp.where(cur == mx, iota, sentinel), axis=0, keepdims=True)
    return iota == first, first


def _router_kernel(x_ref, wh_ref, wl_ref, bias_ref, idx_ref, gate_ref, rank_ref, cnt_ref, run_ref, *, tm):
    i = pl.program_id(0)

    @pl.when(i == 0)
    def _():
        run_ref[...] = jnp.zeros_like(run_ref)

    x = x_ref[...]
    xh = x.astype(BF16)
    xl = (x - xh.astype(F32)).astype(BF16)
    wh = wh_ref[...]
    logits = (lax.dot_general(wh, xh, NT_DIMS, preferred_element_type=F32)
              + lax.dot_general(wh, xl, NT_DIMS, preferred_element_type=F32)
              + lax.dot_general(wl_ref[...], xh, NT_DIMS, preferred_element_type=F32))
    scores = _sigmoid(logits)
    sel = scores + bias_ref[...]
    neg_inf = -jnp.inf

    li = lax.broadcasted_iota(I32, (GROUP_SIZE, tm), 0)
    grp_rows = []
    for g in range(N_GROUPS):
        blk = sel[g * GROUP_SIZE:(g + 1) * GROUP_SIZE, :]
        pick, _ = _pick_first_max(blk, li, GROUP_SIZE)
        m1 = jnp.max(blk, axis=0, keepdims=True)
        m2 = jnp.max(jnp.where(pick, neg_inf, blk), axis=0, keepdims=True)
        grp_rows.append(m1 + m2)
    grp = jnp.concatenate(grp_rows, axis=0)

    gi = lax.broadcasted_iota(I32, (N_GROUPS, tm), 0)
    gsel = jnp.zeros((N_GROUPS, tm), F32)
    cur = grp
    for _ in range(TOPK_GROUPS):
        pick, _ = _pick_first_max(cur, gi, N_GROUPS)
        gsel = jnp.where(pick, 1.0, gsel)
        cur = jnp.where(pick, neg_inf, cur)
    emask = jnp.concatenate(
        [jnp.broadcast_to(gsel[g:g + 1, :], (GROUP_SIZE, tm)) for g in range(N_GROUPS)], axis=0)

    ei = lax.broadcasted_iota(I32, (N_EXPERTS, tm), 0)
    cur = jnp.where(emask > 0.5, sel, neg_inf)
    chosen = jnp.zeros((N_EXPERTS, tm), F32)
    idx_rows, w_rows = [], []
    for _ in range(TOP_K):
        pick, first = _pick_first_max(cur, ei, N_EXPERTS)
        idx_rows.append(first)
        w_rows.append(jnp.sum(jnp.where(pick, scores, 0.0), axis=0, keepdims=True))
        chosen = jnp.where(pick, 1.0, chosen)
        cur = jnp.where(pick, neg_inf, cur)
    idx = jnp.concatenate(idx_rows, axis=0)
    w = jnp.concatenate(w_rows, axis=0)
    gate_ref[...] = w / jnp.sum(w, axis=0, keepdims=True) * ROUTED_SCALE
    idx_ref[...] = idx

    ti = lax.broadcasted_iota(I32, (tm, tm), 0)
    tj = lax.broadcasted_iota(I32, (tm, tm), 1)
    before = jnp.where(ti < tj, 1.0, 0.0).astype(BF16)
    local = jnp.dot(chosen.astype(BF16), before, preferred_element_type=F32)
    total = local + run_ref[:, 0:1]
    rank_rows = [jnp.sum(jnp.where(ei == idx_rows[k], total, 0.0), axis=0, keepdims=True)
                 for k in range(TOP_K)]
    rank_ref[...] = jnp.concatenate(rank_rows, axis=0).astype(I32)
    run_ref[...] = run_ref[...] + jnp.sum(chosen, axis=1, keepdims=True)
    cnt_ref[...] = run_ref[...].astype(I32)


def _router(x, w_hi_t, w_lo_t, bias, tm):
    t, d = x.shape
    return pl.pallas_call(
        functools.partial(_router_kernel, tm=tm),
        grid=(t // tm,),
        in_specs=[pl.BlockSpec((tm, d), lambda i: (i, 0)),
                  pl.BlockSpec((N_EXPERTS, d), lambda i: (0, 0)),
                  pl.BlockSpec((N_EXPERTS, d), lambda i: (0, 0)),
                  pl.BlockSpec((N_EXPERTS, 1), lambda i: (0, 0))],
        out_specs=[pl.BlockSpec((TOP_K, tm), lambda i: (0, i)),
                   pl.BlockSpec((TOP_K, tm), lambda i: (0, i)),
                   pl.BlockSpec((TOP_K, tm), lambda i: (0, i)),
                   pl.BlockSpec((N_EXPERTS, LANES), lambda i: (0, 0))],
        out_shape=[jax.ShapeDtypeStruct((TOP_K, t), I32),
                   jax.ShapeDtypeStruct((TOP_K, t), F32),
                   jax.ShapeDtypeStruct((TOP_K, t), I32),
                   jax.ShapeDtypeStruct((N_EXPERTS, LANES), I32)],
        scratch_shapes=[pltpu.VMEM((N_EXPERTS, LANES), F32)],
        compiler_params=_params(1),
        name="router",
    )(x, w_hi_t, w_lo_t, bias.reshape(N_EXPERTS, 1))


PAD_BITS = tuple(1 << s for s in reversed(range(SUBLANES.bit_length() - 1, EXPERT_ROWS.bit_length() - 1)))


def _dispatch_kernel(zstart_ref, zcount_ref, dest_ref, x_ref, xs_ref, zero_ref, sem, zsem, *, tm):
    i = pl.program_id(0)

    def start_row(t, carry):
        for k in range(TOP_K):
            d = dest_ref[k * tm + t]
            pltpu.make_async_copy(x_ref.at[pl.ds(t, 1), :], xs_ref.at[pl.ds(d, 1), :], sem).start(
                priority=k % 2)
        return carry

    lax.fori_loop(0, tm, start_row, 0, unroll=ISSUE_UNROLL)
    for k in range(TOP_K):
        pltpu.make_async_copy(x_ref, xs_ref.at[pl.ds(0, tm), :], sem).wait()

    @pl.when(i == pl.num_programs(0) - 1)
    def _():
        zero_ref[...] = jnp.zeros_like(zero_ref)

        def fill(e, wait):
            def fill_rows(pos, n, pred):
                @pl.when(pred)
                def _():
                    cp = pltpu.make_async_copy(zero_ref.at[pl.ds(0, n), :],
                                               xs_ref.at[pl.ds(pos, n), :], zsem)
                    if wait:
                        cp.wait()
                    else:
                        cp.start()

            pos = zstart_ref[e]
            cnt = zcount_ref[e]
            head = jnp.minimum((-pos) & (SUBLANES - 1), cnt)
            for j in range(SUBLANES - 1):
                fill_rows(pos + j, 1, j < head)
            pos = pos + head
            cnt = cnt - head
            for bit in PAD_BITS:
                has = (cnt & bit) != 0
                fill_rows(pl.multiple_of(pos, SUBLANES), bit, has)
                pos = pos + jnp.where(has, bit, 0)

        def fill_start(e, carry):
            fill(e, False)
            return carry

        def fill_wait(e, carry):
            fill(e, True)
            return carry

        lax.fori_loop(0, N_EXPERTS, fill_start, 0)
        lax.fori_loop(0, N_EXPERTS, fill_wait, 0)


def _dispatch(x, dest_tiles, zstart, zcount, n_rows, tm):
    t, d = x.shape
    return pl.pallas_call(
        functools.partial(_dispatch_kernel, tm=tm),
        grid_spec=pltpu.PrefetchScalarGridSpec(
            num_scalar_prefetch=2,
            grid=(t // tm,),
            in_specs=[pl.BlockSpec((TOP_K * tm,), lambda i, zs, zc: (i,), memory_space=pltpu.SMEM),
                      pl.BlockSpec((tm, d), lambda i, zs, zc: (i, 0))],
            out_specs=pl.BlockSpec(memory_space=pl.ANY),
            scratch_shapes=[pltpu.VMEM((EXPERT_ROWS // 2, d), x.dtype),
                            pltpu.SemaphoreType.DMA(()),
                            pltpu.SemaphoreType.DMA(())]),
        out_shape=jax.ShapeDtypeStruct((n_rows, d), x.dtype),
        compiler_params=_params(1),
        name="moe_dispatch",
    )(zstart, zcount, dest_tiles, x)


def _expert_kernel(be_ref, nxt_ref, nv_ref, x_ref, wg_hbm, wu_hbm, wd_hbm, y_ref,
                   wgs_ref, wus_ref, wds_ref, wgb_ref, wub_ref, wdb_ref, sems):
    b = pl.program_id(0)
    e = be_ref[b]
    prev = be_ref[jnp.maximum(b - 1, 0)]
    fresh = jnp.logical_or(b == 0, e != prev)

    def weight_copies(expert):
        return (pltpu.make_async_copy(wg_hbm.at[expert], wgs_ref, sems.at[0]),
                pltpu.make_async_copy(wu_hbm.at[expert], wus_ref, sems.at[1]),
                pltpu.make_async_copy(wd_hbm.at[expert], wds_ref, sems.at[2]))

    @pl.when(b == 0)
    def _():
        for cp in weight_copies(e):
            cp.start()

    @pl.when(fresh)
    def _():
        for cp in weight_copies(e):
            cp.wait()
        wgb_ref[...] = wgs_ref[...].astype(BF16)
        wub_ref[...] = wus_ref[...].astype(BF16)
        wdb_ref[...] = wds_ref[...].astype(BF16)
        nxt = nxt_ref[b]

        @pl.when(nxt >= 0)
        def _():
            for cp in weight_copies(nxt):
                cp.start()

    @pl.when(b < nv_ref[0])
    def _():
        half = x_ref.shape[1]
        x_lo, x_hi = _unpack_halves(x_ref[...])
        x_lo = x_lo.astype(BF16)
        x_hi = x_hi.astype(BF16)
        hg = (jnp.dot(x_lo, wgb_ref[:half, :], preferred_element_type=F32)
              + jnp.dot(x_hi, wgb_ref[half:, :], preferred_element_type=F32))
        hu = (jnp.dot(x_lo, wub_ref[:half, :], preferred_element_type=F32)
              + jnp.dot(x_hi, wub_ref[half:, :], preferred_element_type=F32))
        act = (_silu(hg) * hu).astype(BF16)
        y_ref[...] = _pack_halves(jnp.dot(act, wdb_ref[...], preferred_element_type=F32))


def _experts(xs, wg, wu, wd, block_expert, next_expert, n_valid):
    n_rows = xs.shape[0]
    d = wg.shape[1]
    de = wg.shape[2]
    nb = n_rows // EXPERT_ROWS
    xmap = lambda b, be, nx, nv: (jnp.minimum(b, nv[0] - 1), 0)
    return pl.pallas_call(
        _expert_kernel,
        grid_spec=pltpu.PrefetchScalarGridSpec(
            num_scalar_prefetch=3,
            grid=(nb,),
            in_specs=[pl.BlockSpec((EXPERT_ROWS, d // 2), xmap),
                      pl.BlockSpec(memory_space=pl.ANY),
                      pl.BlockSpec(memory_space=pl.ANY),
                      pl.BlockSpec(memory_space=pl.ANY)],
            out_specs=pl.BlockSpec((EXPERT_ROWS, d // 2), xmap),
            scratch_shapes=[pltpu.VMEM((d, de), F32), pltpu.VMEM((d, de), F32), pltpu.VMEM((de, d), F32),
                            pltpu.VMEM((d, de), BF16), pltpu.VMEM((d, de), BF16), pltpu.VMEM((de, d), BF16),
                            pltpu.SemaphoreType.DMA((3,))]),
        out_shape=jax.ShapeDtypeStruct((n_rows, d // 2), I32),
        compiler_params=_params(1),
        name="moe_experts",
    )(block_expert, next_expert, n_valid, xs, wg, wu, wd)


def _combine_kernel(dest_ref, x_ref, gate_ref, ys_ref, swg_ref, swu_ref, swd_ref, g_ref, b_ref,
                    op_ref, os_ref, buf_ref, sem, *, tm, n_prompt_tiles):
    i = pl.program_id(0)

    def start_row(t, carry):
        for k in range(TOP_K):
            d = dest_ref[k * tm + t]
            pltpu.make_async_copy(ys_ref.at[pl.ds(d, 1), :], buf_ref.at[k, pl.ds(t, 1), :], sem).start(
                priority=k % 2)
        return carry

    lax.fori_loop(0, tm, start_row, 0, unroll=ISSUE_UNROLL)
    x = x_ref[...]
    xb = x.astype(BF16)
    act = (_silu(jnp.dot(xb, swg_ref[...], preferred_element_type=F32))
           * jnp.dot(xb, swu_ref[...], preferred_element_type=F32)).astype(BF16)
    shared = jnp.dot(act, swd_ref[...], preferred_element_type=F32)
    for k in range(TOP_K):
        pltpu.make_async_copy(ys_ref.at[pl.ds(0, tm), :], buf_ref.at[k], sem).wait()
    gate = gate_ref[...]
    acc_lo = None
    for k in range(TOP_K):
        y_lo, y_hi = _unpack_halves(buf_ref[k])
        gk = gate[:, k:k + 1]
        acc_lo = gk * y_lo if acc_lo is None else acc_lo + gk * y_lo
        acc_hi = gk * y_hi if k == 0 else acc_hi + gk * y_hi
    moe = jnp.concatenate([acc_lo, acc_hi], axis=-1) + shared
    out = _layer_norm(DEEPNORM_ALPHA * x + moe, g_ref[...], b_ref[...])

    @pl.when(i < n_prompt_tiles)
    def _():
        op_ref[...] = out

    @pl.when(i >= n_prompt_tiles)
    def _():
        os_ref[...] = out


def _combine(x, gate_t, dest_tiles, ys, swg, swu, swd, g, b, tm, n_prompt_rows):
    t, d = x.shape
    ds = swg.shape[1]
    npt = n_prompt_rows // tm
    const = lambda i: (0, 0)
    return pl.pallas_call(
        functools.partial(_combine_kernel, tm=tm, n_prompt_tiles=npt),
        grid=(t // tm,),
        in_specs=[pl.BlockSpec((TOP_K * tm,), lambda i: (i,), memory_space=pltpu.SMEM),
                  pl.BlockSpec((tm, d), lambda i: (i, 0)),
                  pl.BlockSpec((tm, TOP_K), lambda i: (i, 0)),
                  pl.BlockSpec(memory_space=pl.ANY),
                  pl.BlockSpec((d, ds), const),
                  pl.BlockSpec((d, ds), const),
                  pl.BlockSpec((ds, d), const),
                  pl.BlockSpec((1, d), const),
                  pl.BlockSpec((1, d), const)],
        out_specs=[pl.BlockSpec((tm, d), lambda i: (jnp.minimum(i, npt - 1), 0)),
                   pl.BlockSpec((tm, d), lambda i: (jnp.maximum(i - npt, 0), 0))],
        out_shape=[jax.ShapeDtypeStruct((n_prompt_rows, d), F32),
                   jax.ShapeDtypeStruct((t - n_prompt_rows, d), F32)],
        scratch_shapes=[pltpu.VMEM((TOP_K, tm, d // 2), I32), pltpu.SemaphoreType.DMA(())],
        compiler_params=_params(1),
        name="moe_combine",
    )(dest_tiles, x, gate_t, ys, swg, swu, swd, g.reshape(1, d), b.reshape(1, d))


def _tile_major(a, tm):
    k, t = a.shape
    return a.reshape(k, t // tm, tm).transpose(1, 0, 2).reshape(-1)


def _moe_ln(x, x_packed, router_w, router_bias, wg, wu, wd, swg, swu, swd, g, b, n_prompt_rows):
    t, d = x.shape
    w_t = router_w.T
    w_hi = w_t.astype(BF16)
    w_lo = (w_t - w_hi.astype(F32)).astype(BF16)
    tr = _divisor_tile(t, 512, LANES)
    idx, gate, rank, cnt = _router(x, w_hi, w_lo, router_bias, tr)

    counts = cnt[:, 0]
    padded = (counts + EXPERT_ROWS - 1) // EXPERT_ROWS * EXPERT_ROWS
    pad_end = jnp.cumsum(padded)
    pad_start = pad_end - padded
    experts = jnp.arange(N_EXPERTS, dtype=I32)
    onehot = idx[None] == experts[:, None, None]
    dest = jnp.sum(jnp.where(onehot, pad_start[:, None, None], 0), axis=0) + rank
    n_blocks = (t * TOP_K + N_EXPERTS * (EXPERT_ROWS - 1)) // EXPERT_ROWS + 1
    n_valid = (pad_end[-1] // EXPERT_ROWS).astype(I32)
    blk = jnp.minimum(jnp.arange(n_blocks, dtype=I32), n_valid - 1)
    block_expert = jnp.minimum(
        jnp.sum(pad_end[None, :] <= (blk * EXPERT_ROWS)[:, None], axis=1), N_EXPERTS - 1).astype(I32)
    later_used = (experts[None, :] > experts[:, None]) & (counts[None, :] > 0)
    next_used = jnp.min(jnp.where(later_used, experts[None, :], N_EXPERTS), axis=1)
    next_used = jnp.where(next_used == N_EXPERTS, -1, next_used)
    next_expert = jnp.sum(
        jnp.where(block_expert[:, None] == experts[None, :], next_used[None, :], 0), axis=1).astype(I32)

    td = _divisor_tile(t, 256, SUBLANES)
    xs = _dispatch(x_packed, _tile_major(dest, td), (pad_start + counts).astype(I32),
                   (padded - counts).astype(I32), n_blocks * EXPERT_ROWS, td)
    ys = _experts(xs, wg, wu, wd, block_expert, next_expert, n_valid.reshape(1))
    tc = _divisor_tile(math.gcd(n_prompt_rows, t - n_prompt_rows), 128, SUBLANES)
    return _combine(x, gate.T, _tile_major(dest, tc), ys, swg.astype(BF16), swu.astype(BF16),
                    swd.astype(BF16), g, b, tc, n_prompt_rows)


def kernel(x_prompt, x_sample, mem_prompt, state_gla, cache_swa_k, cache_swa_v, cache_mem_k, cache_mem_v,
           w_in, gla_w_gate2, gla_b_gate, gla_norm_g, swa_sinks, w_mix_out, ln1_g, ln1_b,
           mem_wq, mem_wk, mem_wv, mem_wo, ln2_g, ln2_b,
           router_w, router_bias, exp_w_gate, exp_w_up, exp_w_down, sh_w_gate, sh_w_up, sh_w_down,
           ln3_g, ln3_b):
    nbp, seq, d = x_prompt.shape
    nbs, dec, _ = x_sample.shape
    tp, ts = nbp * seq, nbs * dec
    t = tp + ts
    mem_tokens = mem_prompt.shape[1]
    assert state_gla.shape[0] == DEPTH and seq % SWA_WINDOW == 0 and dec % SUBLANES == 0
    assert cache_swa_k.shape[2] == SWA_WINDOW and tp % dec == 0
    tm = _divisor_tile(math.gcd(tp, ts), 512, 16)
    l = 0

    xp = x_prompt.reshape(tp, d)
    xs = x_sample.reshape(ts, d)

    wi = w_in[l]
    o = 0
    seg = []
    for size in (GLA_QK, GLA_QK, GLA_VW, GLA_VW, GLA_GATE_RANK, SWA_QW, SWA_KVW, SWA_KVW):
        seg.append(wi[:, o:o + size])
        o += size
    gq, gk, gv, gr, gg, sq, sk, sv = seg
    w_in_r = jnp.concatenate(
        [gq, gk, gv, gr, sq, sk, sv, gg,
         jnp.zeros((d, IN_COLS_PADDED - COL_GG - GLA_GATE_RANK), wi.dtype)], axis=1).astype(BF16)
    z = _matmul_split(xp, xs, w_in_r, tm, IN_COLS_PADDED // 2, F32)

    wa2_pad = jnp.concatenate(
        [gla_w_gate2[l], jnp.zeros((LANES - GLA_GATE_RANK, GLA_QK), F32)], axis=0)
    wa2_hi = wa2_pad.astype(BF16)
    wa2_lo = (wa2_pad - wa2_hi.astype(F32)).astype(BF16)
    wa2_stack = jnp.concatenate([wa2_hi, wa2_lo, wa2_hi], axis=0)
    zero_state = jnp.zeros((nbp, GLA_HEADS, GLA_DV, GLA_DK), F32)
    gla_rows = _divisor_tile(seq, GLA_STEP_ROWS, CHUNK)
    mix, st_p = _gla(z, wa2_stack, gla_b_gate[l], gla_norm_g[l], zero_state, None,
                     nb=nbp, n_chunks=seq // gla_rows, rows=gla_rows, chunk=CHUNK, row0=0)
    mix, st_s = _gla(z, wa2_stack, gla_b_gate[l], gla_norm_g[l], jnp.swapaxes(state_gla[l], -1, -2), mix,
                     nb=nbs, n_chunks=1, rows=dec, chunk=dec, row0=tp)
    nblk = seq // SWA_WINDOW
    prev_p = lambda col: (lambda b, i: (b * nblk + jnp.maximum(i - 1, 0), col))
    mix = _swa(z, swa_sinks[l], z, z, (prev_p(COL_SK // SWA_KVW), prev_p(COL_SV // SWA_KVW)), mix,
               nb=nbp, n_blocks=nblk, rows=SWA_WINDOW, row0=0, q_base=0, q_stride=SWA_WINDOW)
    ck = cache_swa_k[l].reshape(nbs * SWA_WINDOW, SWA_KVW)
    cv = cache_swa_v[l].reshape(nbs * SWA_WINDOW, SWA_KVW)
    prev_s = lambda b, i: (b, 0)
    mix = _swa(z, swa_sinks[l], ck, cv, (prev_s, prev_s), mix,
               nb=nbs, n_blocks=1, rows=dec, row0=tp, q_base=PAST_LEN, q_stride=0)
    h1 = _matmul_res_ln(mix, w_mix_out[l].astype(BF16), (xp, xs), ln1_g[l], ln1_b[l], tm)

    mem = mem_prompt.reshape(nbp * mem_tokens, d)
    tmem = _divisor_tile(nbp * mem_tokens, 512, 16)
    mk = _matmul(mem, mem_wk[l].astype(BF16), tmem, _divisor_tile(d, 1024, LANES), F32)
    mv = _matmul(mem, mem_wv[l].astype(BF16), tmem, _divisor_tile(d, 1024, LANES), F32)
    q = _matmul(h1, mem_wq[l].astype(BF16), tm, d, BF16)
    tq = _divisor_tile(seq, 512, 16)
    att = _mem_attn(q, mk, mv, None, nb=nbp, n_tiles=seq // tq, rows=tq, row0=0, mem_tokens=mem_tokens)
    att = _mem_attn(q, cache_mem_k[l].reshape(nbs * mem_tokens, d), cache_mem_v[l].reshape(nbs * mem_tokens, d),
                    att, nb=nbs, n_tiles=1, rows=dec, row0=tp, mem_tokens=mem_tokens)
    h2, h2_packed = _matmul_res_ln(att, mem_wo[l].astype(BF16), h1, ln2_g[l], ln2_b[l], tm, emit_packed=True)

    out_p, out_s = _moe_ln(h2, h2_packed, router_w[l], router_bias[l], exp_w_gate[l], exp_w_up[l],
                           exp_w_down[l], sh_w_gate[l], sh_w_up[l], sh_w_down[l], ln3_g[l], ln3_b[l], tp)

    y_prompt = out_p.reshape(nbp, seq, d)
    y_sample = out_s.reshape(nbs, dec, d)
    kv = z[:, COL_SK:COL_SK + 2 * SWA_KVW]
    kvp = kv[:tp].reshape(nbp, seq, 2 * SWA_KVW)[:, seq - SWA_WINDOW:]
    kvs = kv[tp:].reshape(nbs, dec, 2 * SWA_KVW)
    kv_shape = lambda a: a.reshape(a.shape[0], a.shape[1], SWA_KV_HEADS, SWA_HEAD_DIM)[None]
    mem_shape = lambda a: a.reshape(nbp, mem_tokens, MEM_HEADS, d // MEM_HEADS)[None]
    return (y_prompt, y_sample,
            jnp.swapaxes(st_p, -1, -2)[None],
            kv_shape(kvp[..., :SWA_KVW]), kv_shape(kvp[..., SWA_KVW:]),
            mem_shape(mk), mem_shape(mv),
            jnp.swapaxes(st_s, -1, -2)[None],
            kv_shape(kvs[..., :SWA_KVW]), kv_shape(kvs[..., SWA_KVW:]))
```

```python
import functools
import math

import jax
import jax.numpy as jnp
from jax import lax
from jax.experimental import pallas as pl
from jax.experimental.pallas import tpu as pltpu

F32 = jnp.float32
BF16 = jnp.bfloat16
I32 = jnp.int32

CHUNK = 64
PAST_LEN = 2048
GLA_HEADS = 8
GLA_DK = 64
GLA_DV = 128
GLA_QK = GLA_HEADS * GLA_DK
GLA_VW = GLA_HEADS * GLA_DV
GLA_GATE_RANK = 16
GLA_TAU = 16.0
SWA_HEADS = 16
SWA_KV_HEADS = 2
SWA_GROUP = SWA_HEADS // SWA_KV_HEADS
SWA_HEAD_DIM = 64
SWA_QW = SWA_HEADS * SWA_HEAD_DIM
SWA_KVW = SWA_KV_HEADS * SWA_HEAD_DIM
SWA_WINDOW = 128
WINDOW_CHUNKS = SWA_WINDOW // CHUNK
MEM_HEADS = 4
N_EXPERTS = 64
TOP_K = 8
N_GROUPS = 8
GROUP_SIZE = N_EXPERTS // N_GROUPS
TOPK_GROUPS = 4
ROUTED_SCALE = 2.5
DEPTH = 1
DEEPNORM_ALPHA = (2 * DEPTH) ** 0.25
LN_EPS = 1e-5
RMS_EPS = 1e-6

LANES = 128
SUBLANES = 8
VMEM_LIMIT_BYTES = 56 * 1024 * 1024

COL_GQ = 0
COL_GK = COL_GQ + GLA_QK
COL_GV = COL_GK + GLA_QK
COL_GR = COL_GV + GLA_VW
COL_SQ = COL_GR + GLA_VW
COL_SK = COL_SQ + SWA_QW
COL_SV = COL_SK + SWA_KVW
COL_GG = COL_SV + SWA_KVW
MXU_COLS = 256
IN_COLS_PADDED = -(-(COL_GG + LANES) // (2 * MXU_COLS)) * (2 * MXU_COLS)

EXPERT_ROWS = 256
ISSUE_UNROLL = 8
COMBINE_GROUP = 8
GLA_STEP_ROWS = 256

NT_DIMS = (((1,), (1,)), ((), ()))
TN_DIMS = (((0,), (0,)), ((), ()))


def _params(n_axes):
    return pltpu.CompilerParams(dimension_semantics=("arbitrary",) * n_axes,
                                vmem_limit_bytes=VMEM_LIMIT_BYTES)


def _sigmoid(x):
    return 1.0 / (1.0 + jnp.exp(-x))


def _silu(x):
    return x * _sigmoid(x)


def _layer_norm(x, g, b):
    mu = jnp.mean(x, axis=-1, keepdims=True)
    xc = x - mu
    var = jnp.mean(xc * xc, axis=-1, keepdims=True)
    return xc * lax.rsqrt(var + LN_EPS) * g + b


HI_MASK = -65536


def _round_to_bf16_bits(x):
    b = lax.bitcast_convert_type(x, I32)
    return b + (0x7FFF + (lax.shift_right_logical(b, 16) & 1))


def _pack_halves(x):
    c = x.shape[1] // 2
    lo = lax.shift_right_logical(_round_to_bf16_bits(x[:, :c]), 16)
    hi = _round_to_bf16_bits(x[:, c:]) & HI_MASK
    return lo | hi


def _unpack_halves(w):
    return (lax.bitcast_convert_type(lax.shift_left(w, 16), F32),
            lax.bitcast_convert_type(w & HI_MASK, F32))


def _divisor_tile(n, pref, mult):
    t = min(pref, n)
    while t > mult and (n % t or t % mult):
        t -= mult
    assert n % t == 0 and t % mult == 0, (n, pref, mult)
    return t


def _mm_kernel(x_ref, w_ref, o_ref, xb_ref):
    @pl.when(pl.program_id(1) == 0)
    def _():
        xb_ref[...] = x_ref[...].astype(BF16)

    o_ref[...] = jnp.dot(xb_ref[...], w_ref[...], preferred_element_type=F32).astype(o_ref.dtype)


def _matmul(x, w, tm, tn, out_dtype):
    m, k = x.shape
    n = w.shape[1]
    return pl.pallas_call(
        _mm_kernel,
        grid=(m // tm, n // tn),
        in_specs=[pl.BlockSpec((tm, k), lambda i, j: (i, 0)),
                  pl.BlockSpec((k, tn), lambda i, j: (0, j))],
        out_specs=pl.BlockSpec((tm, tn), lambda i, j: (i, j)),
        out_shape=jax.ShapeDtypeStruct((m, n), out_dtype),
        scratch_shapes=[pltpu.VMEM((tm, k), BF16)],
        compiler_params=_params(2),
        name="matmul",
    )(x, w)


def _split_rows_specs(tm, k, n_first_tiles, n_grid_axes):
    if n_grid_axes == 1:
        return [pl.BlockSpec((tm, k), lambda i: (jnp.minimum(i, n_first_tiles - 1), 0)),
                pl.BlockSpec((tm, k), lambda i: (jnp.maximum(i - n_first_tiles, 0), 0))]
    return [pl.BlockSpec((tm, k), lambda j, i: (jnp.minimum(i, n_first_tiles - 1), 0)),
            pl.BlockSpec((tm, k), lambda j, i: (jnp.maximum(i - n_first_tiles, 0), 0))]


def _mm_split_kernel(xa_ref, xb_ref, wt_ref, o_ref, *, n_first_tiles):
    i = pl.program_id(1)

    def product(x_ref):
        o_ref[...] = lax.dot_general(x_ref[...].astype(BF16), wt_ref[...], NT_DIMS,
                                     preferred_element_type=F32).astype(o_ref.dtype)

    @pl.when(i < n_first_tiles)
    def _():
        product(xa_ref)

    @pl.when(i >= n_first_tiles)
    def _():
        product(xb_ref)


def _matmul_split(xa, xb, w_t, tm, tn, out_dtype):
    ma, k = xa.shape
    m = ma + xb.shape[0]
    n = w_t.shape[0]
    return pl.pallas_call(
        functools.partial(_mm_split_kernel, n_first_tiles=ma // tm),
        grid=(n // tn, m // tm),
        in_specs=_split_rows_specs(tm, k, ma // tm, 2) + [pl.BlockSpec((tn, k), lambda j, i: (j, 0))],
        out_specs=pl.BlockSpec((tm, tn), lambda j, i: (i, j)),
        out_shape=jax.ShapeDtypeStruct((m, n), out_dtype),
        compiler_params=_params(2),
        name="matmul_split",
    )(xa, xb, w_t)


def _mm_res_ln_kernel(x_ref, w_ref, *refs, n_res, n_first_tiles):
    res_refs, (g_ref, b_ref, o_ref, *packed_ref) = refs[:n_res], refs[n_res:]
    y = jnp.dot(x_ref[...], w_ref[...], preferred_element_type=F32)

    def finish(res):
        h = _layer_norm(DEEPNORM_ALPHA * res + y, g_ref[...], b_ref[...])
        o_ref[...] = h
        if packed_ref:
            packed_ref[0][...] = _pack_halves(h)

    if n_res == 1:
        finish(res_refs[0][...])
    else:
        i = pl.program_id(0)

        @pl.when(i < n_first_tiles)
        def _():
            finish(res_refs[0][...])

        @pl.when(i >= n_first_tiles)
        def _():
            finish(res_refs[1][...])


def _matmul_res_ln(x, w, res, g, b, tm, emit_packed=False):
    m, k = x.shape
    n = w.shape[1]
    if isinstance(res, tuple):
        n_first_tiles = res[0].shape[0] // tm
        res_specs = _split_rows_specs(tm, n, n_first_tiles, 1)
    else:
        n_first_tiles = 0
        res = (res,)
        res_specs = [pl.BlockSpec((tm, n), lambda i: (i, 0))]
    out_specs = [pl.BlockSpec((tm, n), lambda i: (i, 0))]
    out_shape = [jax.ShapeDtypeStruct((m, n), F32)]
    if emit_packed:
        out_specs.append(pl.BlockSpec((tm, n // 2), lambda i: (i, 0)))
        out_shape.append(jax.ShapeDtypeStruct((m, n // 2), I32))
    outs = pl.pallas_call(
        functools.partial(_mm_res_ln_kernel, n_res=len(res), n_first_tiles=n_first_tiles),
        grid=(m // tm,),
        in_specs=[pl.BlockSpec((tm, k), lambda i: (i, 0)),
                  pl.BlockSpec((k, n), lambda i: (0, 0))] + res_specs + [
                  pl.BlockSpec((1, n), lambda i: (0, 0)),
                  pl.BlockSpec((1, n), lambda i: (0, 0))],
        out_specs=out_specs,
        out_shape=out_shape,
        compiler_params=_params(1),
        name="matmul_res_ln",
    )(x, w, *res, g.reshape(1, n), b.reshape(1, n))
    return outs if emit_packed else outs[0]


def _split3_bf16(x):
    hi = x.astype(BF16)
    r1 = x - hi.astype(F32)
    mid = r1.astype(BF16)
    lo = (r1 - mid.astype(F32)).astype(BF16)
    return jnp.concatenate([hi, mid, lo], axis=-1)


def _gla_kernel(q_ref, k_ref, v_ref, r_ref, gg_ref, wa2_ref, ba_ref, ng_ref, s0_ref, mix_in_ref,
                o_ref, sfin_ref, st_ref, *, rows, chunk):
    del mix_in_ref
    n_sub = rows // chunk
    shift = chunk.bit_length() - 1
    assert chunk == 1 << shift

    @pl.when(pl.program_id(1) == 0)
    def _():
        st_ref[...] = s0_ref[0]

    gg = gg_ref[...]
    gg_hi = gg.astype(BF16)
    gg_lo = (gg - gg_hi.astype(F32)).astype(BF16)
    gate = jnp.dot(jnp.concatenate([gg_hi, gg_hi, gg_lo], axis=-1), wa2_ref[...],
                   preferred_element_type=F32) + ba_ref[...]
    log_a = (jnp.minimum(gate, 0.0) - jnp.log1p(jnp.exp(-jnp.abs(gate)))) / GLA_TAU

    ri = lax.broadcasted_iota(I32, (rows, rows), 0)
    ci = lax.broadcasted_iota(I32, (rows, rows), 1)
    same_chunk = lax.shift_right_logical(ri, shift) == lax.shift_right_logical(ci, shift)
    causal = same_chunk & (ri >= ci)
    ones = jnp.concatenate([jnp.where(causal, 1.0, 0.0), jnp.where(same_chunk, 1.0, 0.0)], axis=0).astype(BF16)
    sums = jnp.dot(ones, _split3_bf16(log_a), preferred_element_type=F32)
    sums = sums[:, :GLA_QK] + sums[:, GLA_QK:2 * GLA_QK] + sums[:, 2 * GLA_QK:]
    bcum = sums[:rows]
    b_last = sums[rows:]
    q = q_ref[...] * (GLA_DK ** -0.5)
    k = k_ref[...]
    q_in = (q * jnp.exp(bcum)).astype(BF16)
    k_in = (k * jnp.exp(-bcum)).astype(BF16)
    k_out = (k * jnp.exp(b_last - bcum)).astype(BF16)
    decay = jnp.exp(b_last)
    ng = ng_ref[...]
    blk_r = lax.shift_right_logical(lax.broadcasted_iota(I32, (rows, n_sub * GLA_DK), 0), shift)
    blk_c = lax.shift_right_logical(lax.broadcasted_iota(I32, (rows, n_sub * GLA_DK), 1),
                                    GLA_DK.bit_length() - 1)
    for h in range(GLA_HEADS):
        ks = slice(h * GLA_DK, (h + 1) * GLA_DK)
        vs = slice(h * GLA_DV, (h + 1) * GLA_DV)
        vh = v_ref[:, vs].astype(BF16)
        qh = q_in[:, ks]
        attn = lax.dot_general(qh, k_in[:, ks], NT_DIMS, preferred_element_type=F32)
        attn = jnp.where(causal, attn, 0.0).astype(BF16)
        o = jnp.dot(attn, vh, preferred_element_type=F32)
        ko = k_out[:, ks]
        if n_sub > 1:
            ko = jnp.where(blk_r == blk_c, jnp.concatenate([ko] * n_sub, axis=-1), 0.0)
        upd_t = lax.dot_general(vh, ko, TN_DIMS, preferred_element_type=F32)
        st = st_ref[h]
        states = []
        for c in range(n_sub):
            states.append(st.astype(BF16))
            st = st * decay[c * chunk:c * chunk + 1, ks] + upd_t[:, c * GLA_DK:(c + 1) * GLA_DK]
        st_ref[h] = st
        o_all = lax.dot_general(qh, jnp.concatenate(states, axis=0), NT_DIMS, preferred_element_type=F32)
        o = o + jnp.concatenate(
            [o_all[c * chunk:(c + 1) * chunk, c * GLA_DV:(c + 1) * GLA_DV] for c in range(n_sub)], axis=0)
        o = o * lax.rsqrt(jnp.mean(o * o, axis=-1, keepdims=True) + RMS_EPS) * ng
        o_ref[:, vs] = (o * _silu(r_ref[:, vs])).astype(o_ref.dtype)
    sfin_ref[0] = st_ref[...]


def _gla(z, wa2_stack, ba, norm_g, s0_t, mix_in, *, nb, n_chunks, rows, chunk, row0):
    t = z.shape[0]
    rb0 = row0 // rows
    rmap = lambda colblk: (lambda b, c: (rb0 + b * n_chunks + c, colblk))
    in_specs = [
        pl.BlockSpec((rows, GLA_QK), rmap(COL_GQ // GLA_QK)),
        pl.BlockSpec((rows, GLA_QK), rmap(COL_GK // GLA_QK)),
        pl.BlockSpec((rows, GLA_VW), rmap(COL_GV // GLA_VW)),
        pl.BlockSpec((rows, GLA_VW), rmap(COL_GR // GLA_VW)),
        pl.BlockSpec((rows, LANES), rmap(COL_GG // LANES)),
        pl.BlockSpec((3 * LANES, GLA_QK), lambda b, c: (0, 0)),
        pl.BlockSpec((1, GLA_QK), lambda b, c: (0, 0)),
        pl.BlockSpec((1, GLA_DV), lambda b, c: (0, 0)),
        pl.BlockSpec((1, GLA_HEADS, GLA_DV, GLA_DK), lambda b, c: (b, 0, 0, 0)),
    ]
    args = [z, z, z, z, z, wa2_stack, ba.reshape(1, GLA_QK), norm_g.reshape(1, GLA_DV), s0_t]
    aliases = {}
    if mix_in is None:
        mix_in = jnp.zeros((SUBLANES, LANES), BF16)
        in_specs.append(pl.BlockSpec(memory_space=pl.ANY))
    else:
        in_specs.append(pl.BlockSpec(memory_space=pl.ANY))
        aliases = {len(args): 0}
    args.append(mix_in)
    mix, s_fin = pl.pallas_call(
        functools.partial(_gla_kernel, rows=rows, chunk=chunk),
        grid=(nb, n_chunks),
        in_specs=in_specs,
        out_specs=[pl.BlockSpec((rows, GLA_VW), rmap(0)),
                   pl.BlockSpec((1, GLA_HEADS, GLA_DV, GLA_DK), lambda b, c: (b, 0, 0, 0))],
        out_shape=[jax.ShapeDtypeStruct((t, GLA_VW + SWA_QW), BF16),
                   jax.ShapeDtypeStruct((nb, GLA_HEADS, GLA_DV, GLA_DK), F32)],
        scratch_shapes=[pltpu.VMEM((GLA_HEADS, GLA_DV, GLA_DK), F32)],
        input_output_aliases=aliases,
        compiler_params=_params(2),
        name="gla",
    )(*args)
    return mix, s_fin


def _swa_kernel(sinks_ref, q_ref, kc_ref, vc_ref, kp_ref, vp_ref, mix_in_ref, o_ref, *,
                rows, q_base, q_stride):
    del mix_in_ref
    i = pl.program_id(1)
    q0 = q_base + i * q_stride
    nk = SWA_WINDOW + rows
    keys = jnp.concatenate([kp_ref[...], kc_ref[...]], axis=0).astype(BF16)
    vals = jnp.concatenate([vp_ref[...], vc_ref[...]], axis=0).astype(BF16)
    sr = SWA_GROUP * rows
    row = lax.broadcasted_iota(I32, (sr, nk), 0)
    qpos = q0 + (row & (rows - 1))
    kpos = q0 - SWA_WINDOW + lax.broadcasted_iota(I32, (sr, nk), 1)
    dist = jnp.abs(qpos - kpos).astype(F32)
    qc = lax.shift_right_arithmetic(qpos, CHUNK.bit_length() - 1)
    kc = lax.shift_right_arithmetic(kpos, CHUNK.bit_length() - 1)
    allowed = (kpos >= 0) & (kc <= qc) & (kc >= qc - WINDOW_CHUNKS)
    dist = jnp.where(allowed, dist, jnp.inf)
    assert rows & (rows - 1) == 0
    head_in_group = lax.shift_right_logical(lax.broadcasted_iota(I32, (sr, 1), 0), rows.bit_length() - 1)
    for g in range(SWA_KV_HEADS):
        slope = jnp.zeros((sr, 1), F32)
        sink = jnp.zeros((sr, 1), F32)
        for j in range(SWA_GROUP):
            h = g * SWA_GROUP + j
            slope = jnp.where(head_in_group == j, 2.0 ** (-8.0 * (h + 1) / SWA_HEADS), slope)
            sink = jnp.where(head_in_group == j, sinks_ref[h], sink)
        kv = slice(g * SWA_HEAD_DIM, (g + 1) * SWA_HEAD_DIM)
        qg = (jnp.concatenate(
            [q_ref[:, (g * SWA_GROUP + j) * SWA_HEAD_DIM:(g * SWA_GROUP + j + 1) * SWA_HEAD_DIM]
             for j in range(SWA_GROUP)], axis=0) * (SWA_HEAD_DIM ** -0.5)).astype(BF16)
        s = lax.dot_general(qg, keys[:, kv], NT_DIMS, preferred_element_type=F32) - slope * dist
        m = jnp.maximum(jnp.max(s, axis=-1, keepdims=True), sink)
        p = jnp.exp(s - m)
        denom = jnp.sum(p, axis=-1, keepdims=True) + jnp.exp(sink - m)
        o = jnp.dot(p.astype(BF16), vals[:, kv], preferred_element_type=F32) / denom
        for j in range(0, SWA_GROUP, 2):
            c0 = (g * SWA_GROUP + j) * SWA_HEAD_DIM
            o_ref[:, c0:c0 + LANES] = jnp.concatenate(
                [o[j * rows:(j + 1) * rows], o[(j + 1) * rows:(j + 2) * rows]], axis=-1).astype(o_ref.dtype)


def _swa(z, sinks, k_prev, v_prev, prev_map, mix_in, *, nb, n_blocks, rows, row0, q_base, q_stride):
    rb0 = row0 // rows
    rmap = lambda colblk: (lambda b, i: (rb0 + b * n_blocks + i, colblk))
    return pl.pallas_call(
        functools.partial(_swa_kernel, rows=rows, q_base=q_base, q_stride=q_stride),
        grid=(nb, n_blocks),
        in_specs=[pl.BlockSpec(memory_space=pltpu.SMEM),
                  pl.BlockSpec((rows, SWA_QW), rmap(COL_SQ // SWA_QW)),
                  pl.BlockSpec((rows, SWA_KVW), rmap(COL_SK // SWA_KVW)),
                  pl.BlockSpec((rows, SWA_KVW), rmap(COL_SV // SWA_KVW)),
                  pl.BlockSpec((SWA_WINDOW, SWA_KVW), prev_map[0]),
                  pl.BlockSpec((SWA_WINDOW, SWA_KVW), prev_map[1]),
                  pl.BlockSpec(memory_space=pl.ANY)],
        out_specs=pl.BlockSpec((rows, SWA_QW), rmap(GLA_VW // SWA_QW)),
        out_shape=jax.ShapeDtypeStruct(mix_in.shape, mix_in.dtype),
        input_output_aliases={6: 0},
        compiler_params=_params(2),
        name="swa",
    )(sinks, z, z, z, k_prev, v_prev, mix_in)


def _mem_attn_kernel(q_ref, k_ref, v_ref, o_in_ref, o_ref, *, head_dim):
    del o_in_ref
    for h in range(MEM_HEADS):
        hs = slice(h * head_dim, (h + 1) * head_dim)
        kh = k_ref[:, hs].astype(BF16)
        vh = v_ref[:, hs].astype(BF16)
        s = lax.dot_general(q_ref[:, hs], kh, NT_DIMS, preferred_element_type=F32) * (head_dim ** -0.5)
        m = jnp.max(s, axis=-1, keepdims=True)
        p = jnp.exp(s - m)
        denom = jnp.sum(p, axis=-1, keepdims=True)
        o = jnp.dot(p.astype(BF16), vh, preferred_element_type=F32) / denom
        o_ref[:, hs] = o.astype(o_ref.dtype)


def _mem_attn(q, mk, mv, o_in, *, nb, n_tiles, rows, row0, mem_tokens):
    t, d = q.shape
    rb0 = row0 // rows
    qmap = lambda b, i: (rb0 + b * n_tiles + i, 0)
    in_specs = [pl.BlockSpec((rows, d), qmap),
                pl.BlockSpec((mem_tokens, d), lambda b, i: (b, 0)),
                pl.BlockSpec((mem_tokens, d), lambda b, i: (b, 0)),
                pl.BlockSpec(memory_space=pl.ANY)]
    aliases = {}
    if o_in is None:
        o_in = jnp.zeros((SUBLANES, LANES), BF16)
    else:
        aliases = {3: 0}
    return pl.pallas_call(
        functools.partial(_mem_attn_kernel, head_dim=d // MEM_HEADS),
        grid=(nb, n_tiles),
        in_specs=in_specs,
        out_specs=pl.BlockSpec((rows, d), qmap),
        out_shape=jax.ShapeDtypeStruct((t, d), BF16),
        input_output_aliases=aliases,
        compiler_params=_params(2),
        name="mem_attn",
    )(q, mk, mv, o_in)


def _pick_first_max(cur, iota, sentinel):
    mx = jnp.max(cur, axis=0, keepdims=True)
    first = jnp.min(jnp.where(cur == mx, iota, sentinel), axis=0, keepdims=True)
    return iota == first, first


def _router_kernel(x_ref, wh_ref, wl_ref, bias_ref, idx_ref, gate_ref, rank_ref, cnt_ref, run_ref, *, tm):
    i = pl.program_id(0)

    @pl.when(i == 0)
    def _():
        run_ref[...] = jnp.zeros_like(run_ref)

    x = x_ref[...]
    xh = x.astype(BF16)
    xl = (x - xh.astype(F32)).astype(BF16)
    wh = wh_ref[...]
    logits = (lax.dot_general(wh, xh, NT_DIMS, preferred_element_type=F32)
              + lax.dot_general(wh, xl, NT_DIMS, preferred_element_type=F32)
              + lax.dot_general(wl_ref[...], xh, NT_DIMS, preferred_element_type=F32))
    scores = _sigmoid(logits)
    sel = scores + bias_ref[...]
    neg_inf = -jnp.inf

    li = lax.broadcasted_iota(I32, (GROUP_SIZE, tm), 0)
    grp_rows = []
    for g in range(N_GROUPS):
        blk = sel[g * GROUP_SIZE:(g + 1) * GROUP_SIZE, :]
        pick, _ = _pick_first_max(blk, li, GROUP_SIZE)
        m1 = jnp.max(blk, axis=0, keepdims=True)
        m2 = jnp.max(jnp.where(pick, neg_inf, blk), axis=0, keepdims=True)
        grp_rows.append(m1 + m2)
    grp = jnp.concatenate(grp_rows, axis=0)

    gi = lax.broadcasted_iota(I32, (N_GROUPS, tm), 0)
    gsel = jnp.zeros((N_GROUPS, tm), F32)
    cur = grp
    for _ in range(TOPK_GROUPS):
        pick, _ = _pick_first_max(cur, gi, N_GROUPS)
        gsel = jnp.where(pick, 1.0, gsel)
        cur = jnp.where(pick, neg_inf, cur)
    emask = jnp.concatenate(
        [jnp.broadcast_to(gsel[g:g + 1, :], (GROUP_SIZE, tm)) for g in range(N_GROUPS)], axis=0)

    ei = lax.broadcasted_iota(I32, (N_EXPERTS, tm), 0)
    cur = jnp.where(emask > 0.5, sel, neg_inf)
    chosen = jnp.zeros((N_EXPERTS, tm), F32)
    idx_rows, w_rows = [], []
    for _ in range(TOP_K):
        pick, first = _pick_first_max(cur, ei, N_EXPERTS)
        idx_rows.append(first)
        w_rows.append(jnp.sum(jnp.where(pick, scores, 0.0), axis=0, keepdims=True))
        chosen = jnp.where(pick, 1.0, chosen)
        cur = jnp.where(pick, neg_inf, cur)
    idx = jnp.concatenate(idx_rows, axis=0)
    w = jnp.concatenate(w_rows, axis=0)
    gate_ref[...] = w / jnp.sum(w, axis=0, keepdims=True) * ROUTED_SCALE
    idx_ref[...] = idx

    ti = lax.broadcasted_iota(I32, (tm, tm), 0)
    tj = lax.broadcasted_iota(I32, (tm, tm), 1)
    before = jnp.where(ti < tj, 1.0, 0.0).astype(BF16)
    local = jnp.dot(chosen.astype(BF16), before, preferred_element_type=F32)
    total = local + run_ref[:, 0:1]
    rank_rows = [jnp.sum(jnp.where(ei == idx_rows[k], total, 0.0), axis=0, keepdims=True)
                 for k in range(TOP_K)]
    rank_ref[...] = jnp.concatenate(rank_rows, axis=0).astype(I32)
    run_ref[...] = run_ref[...] + jnp.sum(chosen, axis=1, keepdims=True)
    cnt_ref[...] = run_ref[...].astype(I32)


def _router(x, w_hi_t, w_lo_t, bias, tm):
    t, d = x.shape
    return pl.pallas_call(
        functools.partial(_router_kernel, tm=tm),
        grid=(t // tm,),
        in_specs=[pl.BlockSpec((tm, d), lambda i: (i, 0)),
                  pl.BlockSpec((N_EXPERTS, d), lambda i: (0, 0)),
                  pl.BlockSpec((N_EXPERTS, d), lambda i: (0, 0)),
                  pl.BlockSpec((N_EXPERTS, 1), lambda i: (0, 0))],
        out_specs=[pl.BlockSpec((TOP_K, tm), lambda i: (0, i)),
                   pl.BlockSpec((TOP_K, tm), lambda i: (0, i)),
                   pl.BlockSpec((TOP_K, tm), lambda i: (0, i)),
                   pl.BlockSpec((N_EXPERTS, LANES), lambda i: (0, 0))],
        out_shape=[jax.ShapeDtypeStruct((TOP_K, t), I32),
                   jax.ShapeDtypeStruct((TOP_K, t), F32),
                   jax.ShapeDtypeStruct((TOP_K, t), I32),
                   jax.ShapeDtypeStruct((N_EXPERTS, LANES), I32)],
        scratch_shapes=[pltpu.VMEM((N_EXPERTS, LANES), F32)],
        compiler_params=_params(1),
        name="router",
    )(x, w_hi_t, w_lo_t, bias.reshape(N_EXPERTS, 1))


PAD_BITS = tuple(1 << s for s in reversed(range(SUBLANES.bit_length() - 1, EXPERT_ROWS.bit_length() - 1)))


def _dispatch_kernel(zstart_ref, zcount_ref, dest_ref, x_ref, xs_ref, zero_ref, sem, zsem, *, tm):
    i = pl.program_id(0)

    def start_row(t, carry):
        for k in range(TOP_K):
            d = dest_ref[k * tm + t]
            pltpu.make_async_copy(x_ref.at[pl.ds(t, 1), :], xs_ref.at[pl.ds(d, 1), :], sem).start(
                priority=k % 2)
        return carry

    lax.fori_loop(0, tm, start_row, 0, unroll=ISSUE_UNROLL)
    for k in range(TOP_K):
        pltpu.make_async_copy(x_ref, xs_ref.at[pl.ds(0, tm), :], sem).wait()

    @pl.when(i == pl.num_programs(0) - 1)
    def _():
        zero_ref[...] = jnp.zeros_like(zero_ref)

        def fill(e, wait):
            def fill_rows(pos, n, pred):
                @pl.when(pred)
                def _():
                    cp = pltpu.make_async_copy(zero_ref.at[pl.ds(0, n), :],
                                               xs_ref.at[pl.ds(pos, n), :], zsem)
                    if wait:
                        cp.wait()
                    else:
                        cp.start()

            pos = zstart_ref[e]
            cnt = zcount_ref[e]
            head = jnp.minimum((-pos) & (SUBLANES - 1), cnt)
            for j in range(SUBLANES - 1):
                fill_rows(pos + j, 1, j < head)
            pos = pos + head
            cnt = cnt - head
            for bit in PAD_BITS:
                has = (cnt & bit) != 0
                fill_rows(pl.multiple_of(pos, SUBLANES), bit, has)
                pos = pos + jnp.where(has, bit, 0)

        def fill_start(e, carry):
            fill(e, False)
            return carry

        def fill_wait(e, carry):
            fill(e, True)
            return carry

        lax.fori_loop(0, N_EXPERTS, fill_start, 0)
        lax.fori_loop(0, N_EXPERTS, fill_wait, 0)


def _dispatch(x, dest_tiles, zstart, zcount, n_rows, tm):
    t, d = x.shape
    return pl.pallas_call(
        functools.partial(_dispatch_kernel, tm=tm),
        grid_spec=pltpu.PrefetchScalarGridSpec(
            num_scalar_prefetch=2,
            grid=(t // tm,),
            in_specs=[pl.BlockSpec((TOP_K * tm,), lambda i, zs, zc: (i,), memory_space=pltpu.SMEM),
                      pl.BlockSpec((tm, d), lambda i, zs, zc: (i, 0))],
            out_specs=pl.BlockSpec(memory_space=pl.ANY),
            scratch_shapes=[pltpu.VMEM((EXPERT_ROWS // 2, d), x.dtype),
                            pltpu.SemaphoreType.DMA(()),
                            pltpu.SemaphoreType.DMA(())]),
        out_shape=jax.ShapeDtypeStruct((n_rows, d), x.dtype),
        compiler_params=_params(1),
        name="moe_dispatch",
    )(zstart, zcount, dest_tiles, x)


def _expert_kernel(be_ref, nxt_ref, nv_ref, x_ref, wg_hbm, wu_hbm, wd_hbm, y_ref,
                   wgs_ref, wus_ref, wds_ref, wgb_ref, wub_ref, wdb_ref, sems):
    b = pl.program_id(0)
    e = be_ref[b]
    prev = be_ref[jnp.maximum(b - 1, 0)]
    fresh = jnp.logical_or(b == 0, e != prev)

    def weight_copies(expert):
        return (pltpu.make_async_copy(wg_hbm.at[expert], wgs_ref, sems.at[0]),
                pltpu.make_async_copy(wu_hbm.at[expert], wus_ref, sems.at[1]),
                pltpu.make_async_copy(wd_hbm.at[expert], wds_ref, sems.at[2]))

    @pl.when(b == 0)
    def _():
        for cp in weight_copies(e):
            cp.start(priority=1)

    @pl.when(fresh)
    def _():
        for cp in weight_copies(e):
            cp.wait()
        wgb_ref[...] = wgs_ref[...].astype(BF16)
        wub_ref[...] = wus_ref[...].astype(BF16)
        wdb_ref[...] = wds_ref[...].astype(BF16)
        nxt = nxt_ref[b]

        @pl.when(nxt >= 0)
        def _():
            for cp in weight_copies(nxt):
                cp.start(priority=1)

    @pl.when(b < nv_ref[0])
    def _():
        half = x_ref.shape[1]
        x_lo, x_hi = _unpack_halves(x_ref[...])
        x_lo = x_lo.astype(BF16)
        x_hi = x_hi.astype(BF16)
        hg = (jnp.dot(x_lo, wgb_ref[:half, :], preferred_element_type=F32)
              + jnp.dot(x_hi, wgb_ref[half:, :], preferred_element_type=F32))
        hu = (jnp.dot(x_lo, wub_ref[:half, :], preferred_element_type=F32)
              + jnp.dot(x_hi, wub_ref[half:, :], preferred_element_type=F32))
        act = (_silu(hg) * hu).astype(BF16)
        y_ref[...] = _pack_halves(jnp.dot(act, wdb_ref[...], preferred_element_type=F32))


def _experts(xs, wg, wu, wd, block_expert, next_expert, n_valid):
    n_rows = xs.shape[0]
    d = wg.shape[1]
    de = wg.shape[2]
    nb = n_rows // EXPERT_ROWS
    xmap = lambda b, be, nx, nv: (jnp.minimum(b, nv[0] - 1), 0)
    return pl.pallas_call(
        _expert_kernel,
        grid_spec=pltpu.PrefetchScalarGridSpec(
            num_scalar_prefetch=3,
            grid=(nb,),
            in_specs=[pl.BlockSpec((EXPERT_ROWS, d // 2), xmap),
                      pl.BlockSpec(memory_space=pl.ANY),
                      pl.BlockSpec(memory_space=pl.ANY),
                      pl.BlockSpec(memory_space=pl.ANY)],
            out_specs=pl.BlockSpec((EXPERT_ROWS, d // 2), xmap),
            scratch_shapes=[pltpu.VMEM((d, de), F32), pltpu.VMEM((d, de), F32), pltpu.VMEM((de, d), F32),
                            pltpu.VMEM((d, de), BF16), pltpu.VMEM((d, de), BF16), pltpu.VMEM((de, d), BF16),
                            pltpu.SemaphoreType.DMA((3,))]),
        out_shape=jax.ShapeDtypeStruct((n_rows, d // 2), I32),
        compiler_params=_params(1),
        name="moe_experts",
    )(block_expert, next_expert, n_valid, xs, wg, wu, wd)


def _combine_kernel(dest0_ref, destn_ref, x_ref, gate_ref, ys_ref, swg_ref, swu_ref, swd_ref, g_ref, b_ref,
                    op_ref, os_ref, buf_ref, acc_ref, sems, *, tm, n_prompt_tiles):
    i = pl.program_id(0)
    n = pl.num_programs(0)
    slot = lax.rem(i, 2)
    half = buf_ref.shape[-1]

    def issue_group(dest_ref, to_slot, g):
        for j in range(COMBINE_GROUP):
            t = g * COMBINE_GROUP + j
            for k in range(TOP_K):
                d = dest_ref[k * tm + t]
                pltpu.make_async_copy(ys_ref.at[pl.ds(d, 1), :], buf_ref.at[to_slot, k, pl.ds(t, 1), :],
                                      sems.at[to_slot]).start(priority=k % 2)

    @pl.when(i == 0)
    def _():
        def first(g, carry):
            issue_group(dest0_ref, 0, g)
            return carry
        lax.fori_loop(0, tm // COMBINE_GROUP, first, 0)

    for k in range(TOP_K):
        pltpu.make_async_copy(ys_ref.at[pl.ds(0, tm), :], buf_ref.at[slot, k], sems.at[slot]).wait()

    def reduce_group(g):
        r0 = pl.multiple_of(g * COMBINE_GROUP, COMBINE_GROUP)
        gate = gate_ref[pl.ds(r0, COMBINE_GROUP), :]
        acc_lo = acc_hi = None
        for k in range(TOP_K):
            y_lo, y_hi = _unpack_halves(buf_ref[slot, k, pl.ds(r0, COMBINE_GROUP), :])
            gk = gate[:, k:k + 1]
            acc_lo = gk * y_lo if k == 0 else acc_lo + gk * y_lo
            acc_hi = gk * y_hi if k == 0 else acc_hi + gk * y_hi
        acc_ref[pl.ds(r0, COMBINE_GROUP), :half] = acc_lo
        acc_ref[pl.ds(r0, COMBINE_GROUP), half:] = acc_hi

    @pl.when(i + 1 < n)
    def _():
        def body(g, carry):
            issue_group(destn_ref, 1 - slot, g)
            reduce_group(g)
            return carry
        lax.fori_loop(0, tm // COMBINE_GROUP, body, 0)

    @pl.when(i + 1 >= n)
    def _():
        def body(g, carry):
            reduce_group(g)
            return carry
        lax.fori_loop(0, tm // COMBINE_GROUP, body, 0)

    x = x_ref[...]
    xb = x.astype(BF16)
    act = (_silu(jnp.dot(xb, swg_ref[...], preferred_element_type=F32))
           * jnp.dot(xb, swu_ref[...], preferred_element_type=F32)).astype(BF16)
    shared = jnp.dot(act, swd_ref[...], preferred_element_type=F32)
    out = _layer_norm(DEEPNORM_ALPHA * x + (acc_ref[...] + shared), g_ref[...], b_ref[...])

    @pl.when(i < n_prompt_tiles)
    def _():
        op_ref[...] = out

    @pl.when(i >= n_prompt_tiles)
    def _():
        os_ref[...] = out


def _combine(x, gate_t, dest_tiles, ys, swg, swu, swd, g, b, tm, n_prompt_rows):
    t, d = x.shape
    ds = swg.shape[1]
    n_tiles = t // tm
    npt = n_prompt_rows // tm
    const = lambda i: (0, 0)
    return pl.pallas_call(
        functools.partial(_combine_kernel, tm=tm, n_prompt_tiles=npt),
        grid=(n_tiles,),
        in_specs=[pl.BlockSpec((TOP_K * tm,), lambda i: (0,), memory_space=pltpu.SMEM),
                  pl.BlockSpec((TOP_K * tm,), lambda i: (jnp.minimum(i + 1, n_tiles - 1),),
                               memory_space=pltpu.SMEM),
                  pl.BlockSpec((tm, d), lambda i: (i, 0)),
                  pl.BlockSpec((tm, TOP_K), lambda i: (i, 0)),
                  pl.BlockSpec(memory_space=pl.ANY),
                  pl.BlockSpec((d, ds), const),
                  pl.BlockSpec((d, ds), const),
                  pl.BlockSpec((ds, d), const),
                  pl.BlockSpec((1, d), const),
                  pl.BlockSpec((1, d), const)],
        out_specs=[pl.BlockSpec((tm, d), lambda i: (jnp.minimum(i, npt - 1), 0)),
                   pl.BlockSpec((tm, d), lambda i: (jnp.maximum(i - npt, 0), 0))],
        out_shape=[jax.ShapeDtypeStruct((n_prompt_rows, d), F32),
                   jax.ShapeDtypeStruct((t - n_prompt_rows, d), F32)],
        scratch_shapes=[pltpu.VMEM((2, TOP_K, tm, d // 2), I32), pltpu.VMEM((tm, d), F32),
                        pltpu.SemaphoreType.DMA((2,))],
        compiler_params=_params(1),
        name="moe_combine",
    )(dest_tiles, dest_tiles, x, gate_t, ys, swg, swu, swd, g.reshape(1, d), b.reshape(1, d))


def _tile_major(a, tm):
    k, t = a.shape
    return a.reshape(k, t // tm, tm).transpose(1, 0, 2).reshape(-1)


def _moe_ln(x, x_packed, router_w, router_bias, wg, wu, wd, swg, swu, swd, g, b, n_prompt_rows):
    t, d = x.shape
    w_t = router_w.T
    w_hi = w_t.astype(BF16)
    w_lo = (w_t - w_hi.astype(F32)).astype(BF16)
    tr = _divisor_tile(t, 512, LANES)
    idx, gate, rank, cnt = _router(x, w_hi, w_lo, router_bias, tr)

    counts = cnt[:, 0]
    padded = (counts + EXPERT_ROWS - 1) // EXPERT_ROWS * EXPERT_ROWS
    pad_end = jnp.cumsum(padded)
    pad_start = pad_end - padded
    experts = jnp.arange(N_EXPERTS, dtype=I32)
    onehot = idx[None] == experts[:, None, None]
    dest = jnp.sum(jnp.where(onehot, pad_start[:, None, None], 0), axis=0) + rank
    n_blocks = (t * TOP_K + N_EXPERTS * (EXPERT_ROWS - 1)) // EXPERT_ROWS + 1
    n_valid = (pad_end[-1] // EXPERT_ROWS).astype(I32)
    blk = jnp.minimum(jnp.arange(n_blocks, dtype=I32), n_valid - 1)
    block_expert = jnp.minimum(
        jnp.sum(pad_end[None, :] <= (blk * EXPERT_ROWS)[:, None], axis=1), N_EXPERTS - 1).astype(I32)
    later_used = (experts[None, :] > experts[:, None]) & (counts[None, :] > 0)
    next_used = jnp.min(jnp.where(later_used, experts[None, :], N_EXPERTS), axis=1)
    next_used = jnp.where(next_used == N_EXPERTS, -1, next_used)
    next_expert = jnp.sum(
        jnp.where(block_expert[:, None] == experts[None, :], next_used[None, :], 0), axis=1).astype(I32)

    td = _divisor_tile(t, 256, SUBLANES)
    xs = _dispatch(x_packed, _tile_major(dest, td), (pad_start + counts).astype(I32),
                   (padded - counts).astype(I32), n_blocks * EXPERT_ROWS, td)
    ys = _experts(xs, wg, wu, wd, block_expert, next_expert, n_valid.reshape(1))
    tc = _divisor_tile(math.gcd(n_prompt_rows, t - n_prompt_rows), 128, SUBLANES)
    return _combine(x, gate.T, _tile_major(dest, tc), ys, swg.astype(BF16), swu.astype(BF16),
                    swd.astype(BF16), g, b, tc, n_prompt_rows)


def kernel(x_prompt, x_sample, mem_prompt, state_gla, cache_swa_k, cache_swa_v, cache_mem_k, cache_mem_v,
           w_in, gla_w_gate2, gla_b_gate, gla_norm_g, swa_sinks, w_mix_out, ln1_g, ln1_b,
           mem_wq, mem_wk, mem_wv, mem_wo, ln2_g, ln2_b,
           router_w, router_bias, exp_w_gate, exp_w_up, exp_w_down, sh_w_gate, sh_w_up, sh_w_down,
           ln3_g, ln3_b):
    nbp, seq, d = x_prompt.shape
    nbs, dec, _ = x_sample.shape
    tp, ts = nbp * seq, nbs * dec
    t = tp + ts
    mem_tokens = mem_prompt.shape[1]
    assert state_gla.shape[0] == DEPTH and seq % SWA_WINDOW == 0 and dec % SUBLANES == 0
    assert cache_swa_k.shape[2] == SWA_WINDOW and tp % dec == 0
    tm = _divisor_tile(math.gcd(tp, ts), 512, 16)
    l = 0

    xp = x_prompt.reshape(tp, d)
    xs = x_sample.reshape(ts, d)

    wi_t = w_in[l].T
    o = 0
    seg = []
    for size in (GLA_QK, GLA_QK, GLA_VW, GLA_VW, GLA_GATE_RANK, SWA_QW, SWA_KVW, SWA_KVW):
        seg.append(wi_t[o:o + size])
        o += size
    gq, gk, gv, gr, gg, sq, sk, sv = seg
    w_in_rt = jnp.concatenate(
        [gq, gk, gv, gr, sq, sk, sv, gg,
         jnp.zeros((IN_COLS_PADDED - COL_GG - GLA_GATE_RANK, d), wi_t.dtype)], axis=0).astype(BF16)
    z = _matmul_split(xp, xs, w_in_rt, tm, IN_COLS_PADDED // 2, F32)

    wa2_pad = jnp.concatenate(
        [gla_w_gate2[l], jnp.zeros((LANES - GLA_GATE_RANK, GLA_QK), F32)], axis=0)
    wa2_hi = wa2_pad.astype(BF16)
    wa2_lo = (wa2_pad - wa2_hi.astype(F32)).astype(BF16)
    wa2_stack = jnp.concatenate([wa2_hi, wa2_lo, wa2_hi], axis=0)
    zero_state = jnp.zeros((nbp, GLA_HEADS, GLA_DV, GLA_DK), F32)
    gla_rows = _divisor_tile(seq, GLA_STEP_ROWS, CHUNK)
    mix, st_p = _gla(z, wa2_stack, gla_b_gate[l], gla_norm_g[l], zero_state, None,
                     nb=nbp, n_chunks=seq // gla_rows, rows=gla_rows, chunk=CHUNK, row0=0)
    mix, st_s = _gla(z, wa2_stack, gla_b_gate[l], gla_norm_g[l], jnp.swapaxes(state_gla[l], -1, -2), mix,
                     nb=nbs, n_chunks=1, rows=dec, chunk=dec, row0=tp)
    nblk = seq // SWA_WINDOW
    prev_p = lambda col: (lambda b, i: (b * nblk + jnp.maximum(i - 1, 0), col))
    mix = _swa(z, swa_sinks[l], z, z, (prev_p(COL_SK // SWA_KVW), prev_p(COL_SV // SWA_KVW)), mix,
               nb=nbp, n_blocks=nblk, rows=SWA_WINDOW, row0=0, q_base=0, q_stride=SWA_WINDOW)
    ck = cache_swa_k[l].reshape(nbs * SWA_WINDOW, SWA_KVW)
    cv = cache_swa_v[l].reshape(nbs * SWA_WINDOW, SWA_KVW)
    prev_s = lambda b, i: (b, 0)
    mix = _swa(z, swa_sinks[l], ck, cv, (prev_s, prev_s), mix,
               nb=nbs, n_blocks=1, rows=dec, row0=tp, q_base=PAST_LEN, q_stride=0)
    h1 = _matmul_res_ln(mix, w_mix_out[l].astype(BF16), (xp, xs), ln1_g[l], ln1_b[l], tm)

    mem = mem_prompt.reshape(nbp * mem_tokens, d)
    tmem = _divisor_tile(nbp * mem_tokens, 512, 16)
    mk = _matmul(mem, mem_wk[l].astype(BF16), tmem, _divisor_tile(d, 1024, LANES), F32)
    mv = _matmul(mem, mem_wv[l].astype(BF16), tmem, _divisor_tile(d, 1024, LANES), F32)
    q = _matmul(h1, mem_wq[l].astype(BF16), tm, d, BF16)
    tq = _divisor_tile(seq, 512, 16)
    att = _mem_attn(q, mk, mv, None, nb=nbp, n_tiles=seq // tq, rows=tq, row0=0, mem_tokens=mem_tokens)
    att = _mem_attn(q, cache_mem_k[l].reshape(nbs * mem_tokens, d), cache_mem_v[l].reshape(nbs * mem_tokens, d),
                    att, nb=nbs, n_tiles=1, rows=dec, row0=tp, mem_tokens=mem_tokens)
    h2, h2_packed = _matmul_res_ln(att, mem_wo[l].astype(BF16), h1, ln2_g[l], ln2_b[l], tm, emit_packed=True)

    out_p, out_s = _moe_ln(h2, h2_packed, router_w[l], router_bias[l], exp_w_gate[l], exp_w_up[l],
                           exp_w_down[l], sh_w_gate[l], sh_w_up[l], sh_w_down[l], ln3_g[l], ln3_b[l], tp)

    y_prompt = out_p.reshape(nbp, seq, d)
    y_sample = out_s.reshape(nbs, dec, d)
    kv = z[:, COL_SK:COL_SK + 2 * SWA_KVW]
    kvp = kv[:tp].reshape(nbp, seq, 2 * SWA_KVW)[:, seq - SWA_WINDOW:]
    kvs = kv[tp:].reshape(nbs, dec, 2 * SWA_KVW)
    kv_shape = lambda a: a.reshape(a.shape[0], a.shape[1], SWA_KV_HEADS, SWA_HEAD_DIM)[None]
    mem_shape = lambda a: a.reshape(nbp, mem_tokens, MEM_HEADS, d // MEM_HEADS)[None]
    return (y_prompt, y_sample,
            jnp.swapaxes(st_p, -1, -2)[None],
            kv_shape(kvp[..., :SWA_KVW]), kv_shape(kvp[..., SWA_KVW:]),
            mem_shape(mk), mem_shape(mv),
            jnp.swapaxes(st_s, -1, -2)[None],
            kv_shape(kvs[..., :SWA_KVW]), kv_shape(kvs[..., SWA_KVW:]))
```

```python
import functools
import math

import jax
import jax.numpy as jnp
from jax import lax
from jax.experimental import pallas as pl
from jax.experimental.pallas import tpu as pltpu

F32 = jnp.float32
BF16 = jnp.bfloat16
I32 = jnp.int32

CHUNK = 64
PAST_LEN = 2048
GLA_HEADS = 8
GLA_DK = 64
GLA_DV = 128
GLA_QK = GLA_HEADS * GLA_DK
GLA_VW = GLA_HEADS * GLA_DV
GLA_GATE_RANK = 16
GLA_TAU = 16.0
SWA_HEADS = 16
SWA_KV_HEADS = 2
SWA_GROUP = SWA_HEADS // SWA_KV_HEADS
SWA_HEAD_DIM = 64
SWA_QW = SWA_HEADS * SWA_HEAD_DIM
SWA_KVW = SWA_KV_HEADS * SWA_HEAD_DIM
SWA_WINDOW = 128
WINDOW_CHUNKS = SWA_WINDOW // CHUNK
MEM_HEADS = 4
N_EXPERTS = 64
TOP_K = 8
N_GROUPS = 8
GROUP_SIZE = N_EXPERTS // N_GROUPS
TOPK_GROUPS = 4
ROUTED_SCALE = 2.5
DEPTH = 1
DEEPNORM_ALPHA = (2 * DEPTH) ** 0.25
LN_EPS = 1e-5
RMS_EPS = 1e-6

LANES = 128
SUBLANES = 8
VMEM_LIMIT_BYTES = 56 * 1024 * 1024

COL_GQ = 0
COL_GK = COL_GQ + GLA_QK
COL_GV = COL_GK + GLA_QK
COL_GR = COL_GV + GLA_VW
COL_SQ = COL_GR + GLA_VW
COL_SK = COL_SQ + SWA_QW
COL_SV = COL_SK + SWA_KVW
COL_GG = COL_SV + SWA_KVW
MXU_COLS = 256
IN_COLS_PADDED = -(-(COL_GG + LANES) // (2 * MXU_COLS)) * (2 * MXU_COLS)

EXPERT_ROWS = 256
COMBINE_GROUP = SUBLANES
GLA_STEP_ROWS = 256

NT_DIMS = (((1,), (1,)), ((), ()))
TN_DIMS = (((0,), (0,)), ((), ()))


def _params(n_axes):
    return pltpu.CompilerParams(dimension_semantics=("arbitrary",) * n_axes,
                                vmem_limit_bytes=VMEM_LIMIT_BYTES)


def _sigmoid(x):
    return 1.0 / (1.0 + jnp.exp(-x))


def _silu(x):
    return x * _sigmoid(x)


def _layer_norm(x, g, b):
    mu = jnp.mean(x, axis=-1, keepdims=True)
    xc = x - mu
    var = jnp.mean(xc * xc, axis=-1, keepdims=True)
    return xc * lax.rsqrt(var + LN_EPS) * g + b


HI_MASK = -65536


def _round_to_bf16_bits(x):
    b = lax.bitcast_convert_type(x, I32)
    return b + (0x7FFF + (lax.shift_right_logical(b, 16) & 1))


def _pack_halves(x):
    c = x.shape[1] // 2
    lo = lax.shift_right_logical(_round_to_bf16_bits(x[:, :c]), 16)
    hi = _round_to_bf16_bits(x[:, c:]) & HI_MASK
    return lo | hi


def _unpack_halves(w):
    return (lax.bitcast_convert_type(lax.shift_left(w, 16), F32),
            lax.bitcast_convert_type(w & HI_MASK, F32))


def _divisor_tile(n, pref, mult):
    t = min(pref, n)
    while t > mult and (n % t or t % mult):
        t -= mult
    assert n % t == 0 and t % mult == 0, (n, pref, mult)
    return t


def _mm_kernel(x_ref, w_ref, o_ref, xb_ref):
    @pl.when(pl.program_id(1) == 0)
    def _():
        xb_ref[...] = x_ref[...].astype(BF16)

    o_ref[...] = jnp.dot(xb_ref[...], w_ref[...], preferred_element_type=F32).astype(o_ref.dtype)


def _matmul(x, w, tm, tn, out_dtype):
    m, k = x.shape
    n = w.shape[1]
    return pl.pallas_call(
        _mm_kernel,
        grid=(m // tm, n // tn),
        in_specs=[pl.BlockSpec((tm, k), lambda i, j: (i, 0)),
                  pl.BlockSpec((k, tn), lambda i, j: (0, j))],
        out_specs=pl.BlockSpec((tm, tn), lambda i, j: (i, j)),
        out_shape=jax.ShapeDtypeStruct((m, n), out_dtype),
        scratch_shapes=[pltpu.VMEM((tm, k), BF16)],
        compiler_params=_params(2),
        name="matmul",
    )(x, w)


def _split_rows_specs(tm, k, n_first_tiles, n_grid_axes):
    if n_grid_axes == 1:
        return [pl.BlockSpec((tm, k), lambda i: (jnp.minimum(i, n_first_tiles - 1), 0)),
                pl.BlockSpec((tm, k), lambda i: (jnp.maximum(i - n_first_tiles, 0), 0))]
    return [pl.BlockSpec((tm, k), lambda j, i: (jnp.minimum(i, n_first_tiles - 1), 0)),
            pl.BlockSpec((tm, k), lambda j, i: (jnp.maximum(i - n_first_tiles, 0), 0))]


def _mm_split_kernel(xa_ref, xb_ref, wt_ref, o_ref, *, n_first_tiles):
    i = pl.program_id(1)

    def product(x_ref):
        o_ref[...] = lax.dot_general(x_ref[...].astype(BF16), wt_ref[...], NT_DIMS,
                                     preferred_element_type=F32).astype(o_ref.dtype)

    @pl.when(i < n_first_tiles)
    def _():
        product(xa_ref)

    @pl.when(i >= n_first_tiles)
    def _():
        product(xb_ref)


def _matmul_split(xa, xb, w_t, tm, tn, out_dtype):
    ma, k = xa.shape
    m = ma + xb.shape[0]
    n = w_t.shape[0]
    return pl.pallas_call(
        functools.partial(_mm_split_kernel, n_first_tiles=ma // tm),
        grid=(n // tn, m // tm),
        in_specs=_split_rows_specs(tm, k, ma // tm, 2) + [pl.BlockSpec((tn, k), lambda j, i: (j, 0))],
        out_specs=pl.BlockSpec((tm, tn), lambda j, i: (i, j)),
        out_shape=jax.ShapeDtypeStruct((m, n), out_dtype),
        compiler_params=_params(2),
        name="matmul_split",
    )(xa, xb, w_t)


def _mm_res_ln_kernel(x_ref, w_ref, *refs, n_res, n_first_tiles):
    res_refs, (g_ref, b_ref, o_ref, *packed_ref) = refs[:n_res], refs[n_res:]
    y = jnp.dot(x_ref[...], w_ref[...], preferred_element_type=F32)

    def finish(res):
        h = _layer_norm(DEEPNORM_ALPHA * res + y, g_ref[...], b_ref[...])
        o_ref[...] = h
        if packed_ref:
            packed_ref[0][...] = _pack_halves(h).reshape(packed_ref[0].shape)

    if n_res == 1:
        finish(res_refs[0][...])
    else:
        i = pl.program_id(0)

        @pl.when(i < n_first_tiles)
        def _():
            finish(res_refs[0][...])

        @pl.when(i >= n_first_tiles)
        def _():
            finish(res_refs[1][...])


def _matmul_res_ln(x, w, res, g, b, tm, emit_packed=False):
    m, k = x.shape
    n = w.shape[1]
    if isinstance(res, tuple):
        n_first_tiles = res[0].shape[0] // tm
        res_specs = _split_rows_specs(tm, n, n_first_tiles, 1)
    else:
        n_first_tiles = 0
        res = (res,)
        res_specs = [pl.BlockSpec((tm, n), lambda i: (i, 0))]
    out_specs = [pl.BlockSpec((tm, n), lambda i: (i, 0))]
    out_shape = [jax.ShapeDtypeStruct((m, n), F32)]
    if emit_packed:
        out_specs.append(pl.BlockSpec((tm // SUBLANES, SUBLANES, n // 2), lambda i: (i, 0, 0)))
        out_shape.append(jax.ShapeDtypeStruct((m // SUBLANES, SUBLANES, n // 2), I32))
    outs = pl.pallas_call(
        functools.partial(_mm_res_ln_kernel, n_res=len(res), n_first_tiles=n_first_tiles),
        grid=(m // tm,),
        in_specs=[pl.BlockSpec((tm, k), lambda i: (i, 0)),
                  pl.BlockSpec((k, n), lambda i: (0, 0))] + res_specs + [
                  pl.BlockSpec((1, n), lambda i: (0, 0)),
                  pl.BlockSpec((1, n), lambda i: (0, 0))],
        out_specs=out_specs,
        out_shape=out_shape,
        compiler_params=_params(1),
        name="matmul_res_ln",
    )(x, w, *res, g.reshape(1, n), b.reshape(1, n))
    return outs if emit_packed else outs[0]


def _split3_bf16(x):
    hi = x.astype(BF16)
    r1 = x - hi.astype(F32)
    mid = r1.astype(BF16)
    lo = (r1 - mid.astype(F32)).astype(BF16)
    return jnp.concatenate([hi, mid, lo], axis=-1)


def _gla_kernel(q_ref, k_ref, v_ref, r_ref, gg_ref, wa2_ref, ba_ref, ng_ref, s0_ref, mix_in_ref,
                o_ref, sfin_ref, st_ref, *, rows, chunk):
    del mix_in_ref
    n_sub = rows // chunk
    shift = chunk.bit_length() - 1
    assert chunk == 1 << shift

    @pl.when(pl.program_id(1) == 0)
    def _():
        st_ref[...] = s0_ref[0]

    gg = gg_ref[...]
    gg_hi = gg.astype(BF16)
    gg_lo = (gg - gg_hi.astype(F32)).astype(BF16)
    gate = jnp.dot(jnp.concatenate([gg_hi, gg_hi, gg_lo], axis=-1), wa2_ref[...],
                   preferred_element_type=F32) + ba_ref[...]
    log_a = (jnp.minimum(gate, 0.0) - jnp.log1p(jnp.exp(-jnp.abs(gate)))) / GLA_TAU

    ri = lax.broadcasted_iota(I32, (rows, rows), 0)
    ci = lax.broadcasted_iota(I32, (rows, rows), 1)
    same_chunk = lax.shift_right_logical(ri, shift) == lax.shift_right_logical(ci, shift)
    causal = same_chunk & (ri >= ci)
    ones = jnp.concatenate([jnp.where(causal, 1.0, 0.0), jnp.where(same_chunk, 1.0, 0.0)], axis=0).astype(BF16)
    sums = jnp.dot(ones, _split3_bf16(log_a), preferred_element_type=F32)
    sums = sums[:, :GLA_QK] + sums[:, GLA_QK:2 * GLA_QK] + sums[:, 2 * GLA_QK:]
    bcum = sums[:rows]
    b_last = sums[rows:]
    q = q_ref[...] * (GLA_DK ** -0.5)
    k = k_ref[...]
    q_in = (q * jnp.exp(bcum)).astype(BF16)
    k_in = (k * jnp.exp(-bcum)).astype(BF16)
    k_out = (k * jnp.exp(b_last - bcum)).astype(BF16)
    decay = jnp.exp(b_last)
    ng = ng_ref[...]
    blk_r = lax.shift_right_logical(lax.broadcasted_iota(I32, (rows, n_sub * GLA_DK), 0), shift)
    blk_c = lax.shift_right_logical(lax.broadcasted_iota(I32, (rows, n_sub * GLA_DK), 1),
                                    GLA_DK.bit_length() - 1)
    for h in range(GLA_HEADS):
        ks = slice(h * GLA_DK, (h + 1) * GLA_DK)
        vs = slice(h * GLA_DV, (h + 1) * GLA_DV)
        vh = v_ref[:, vs].astype(BF16)
        qh = q_in[:, ks]
        attn = lax.dot_general(qh, k_in[:, ks], NT_DIMS, preferred_element_type=F32)
        attn = jnp.where(causal, attn, 0.0).astype(BF16)
        o = jnp.dot(attn, vh, preferred_element_type=F32)
        ko = k_out[:, ks]
        if n_sub > 1:
            ko = jnp.where(blk_r == blk_c, jnp.concatenate([ko] * n_sub, axis=-1), 0.0)
        upd_t = lax.dot_general(vh, ko, TN_DIMS, preferred_element_type=F32)
        st = st_ref[h]
        states = []
        for c in range(n_sub):
            states.append(st.astype(BF16))
            st = st * decay[c * chunk:c * chunk + 1, ks] + upd_t[:, c * GLA_DK:(c + 1) * GLA_DK]
        st_ref[h] = st
        o_all = lax.dot_general(qh, jnp.concatenate(states, axis=0), NT_DIMS, preferred_element_type=F32)
        o = o + jnp.concatenate(
            [o_all[c * chunk:(c + 1) * chunk, c * GLA_DV:(c + 1) * GLA_DV] for c in range(n_sub)], axis=0)
        o = o * lax.rsqrt(jnp.mean(o * o, axis=-1, keepdims=True) + RMS_EPS) * ng
        o_ref[:, vs] = (o * _silu(r_ref[:, vs])).astype(o_ref.dtype)
    sfin_ref[0] = st_ref[...]


def _gla(z, wa2_stack, ba, norm_g, s0_t, mix_in, *, nb, n_chunks, rows, chunk, row0):
    t = z.shape[0]
    rb0 = row0 // rows
    rmap = lambda colblk: (lambda b, c: (rb0 + b * n_chunks + c, colblk))
    in_specs = [
        pl.BlockSpec((rows, GLA_QK), rmap(COL_GQ // GLA_QK)),
        pl.BlockSpec((rows, GLA_QK), rmap(COL_GK // GLA_QK)),
        pl.BlockSpec((rows, GLA_VW), rmap(COL_GV // GLA_VW)),
        pl.BlockSpec((rows, GLA_VW), rmap(COL_GR // GLA_VW)),
        pl.BlockSpec((rows, LANES), rmap(COL_GG // LANES)),
        pl.BlockSpec((3 * LANES, GLA_QK), lambda b, c: (0, 0)),
        pl.BlockSpec((1, GLA_QK), lambda b, c: (0, 0)),
        pl.BlockSpec((1, GLA_DV), lambda b, c: (0, 0)),
        pl.BlockSpec((1, GLA_HEADS, GLA_DV, GLA_DK), lambda b, c: (b, 0, 0, 0)),
    ]
    args = [z, z, z, z, z, wa2_stack, ba.reshape(1, GLA_QK), norm_g.reshape(1, GLA_DV), s0_t]
    aliases = {}
    if mix_in is None:
        mix_in = jnp.zeros((SUBLANES, LANES), BF16)
        in_specs.append(pl.BlockSpec(memory_space=pl.ANY))
    else:
        in_specs.append(pl.BlockSpec(memory_space=pl.ANY))
        aliases = {len(args): 0}
    args.append(mix_in)
    mix, s_fin = pl.pallas_call(
        functools.partial(_gla_kernel, rows=rows, chunk=chunk),
        grid=(nb, n_chunks),
        in_specs=in_specs,
        out_specs=[pl.BlockSpec((rows, GLA_VW), rmap(0)),
                   pl.BlockSpec((1, GLA_HEADS, GLA_DV, GLA_DK), lambda b, c: (b, 0, 0, 0))],
        out_shape=[jax.ShapeDtypeStruct((t, GLA_VW + SWA_QW), BF16),
                   jax.ShapeDtypeStruct((nb, GLA_HEADS, GLA_DV, GLA_DK), F32)],
        scratch_shapes=[pltpu.VMEM((GLA_HEADS, GLA_DV, GLA_DK), F32)],
        input_output_aliases=aliases,
        compiler_params=_params(2),
        name="gla",
    )(*args)
    return mix, s_fin


def _swa_kernel(sinks_ref, q_ref, kc_ref, vc_ref, kp_ref, vp_ref, mix_in_ref, o_ref, *,
                rows, q_base, q_stride):
    del mix_in_ref
    i = pl.program_id(1)
    q0 = q_base + i * q_stride
    nk = SWA_WINDOW + rows
    keys = jnp.concatenate([kp_ref[...], kc_ref[...]], axis=0).astype(BF16)
    vals = jnp.concatenate([vp_ref[...], vc_ref[...]], axis=0).astype(BF16)
    sr = SWA_GROUP * rows
    row = lax.broadcasted_iota(I32, (sr, nk), 0)
    qpos = q0 + (row & (rows - 1))
    kpos = q0 - SWA_WINDOW + lax.broadcasted_iota(I32, (sr, nk), 1)
    dist = jnp.abs(qpos - kpos).astype(F32)
    qc = lax.shift_right_arithmetic(qpos, CHUNK.bit_length() - 1)
    kc = lax.shift_right_arithmetic(kpos, CHUNK.bit_length() - 1)
    allowed = (kpos >= 0) & (kc <= qc) & (kc >= qc - WINDOW_CHUNKS)
    assert rows & (rows - 1) == 0
    head_in_group = lax.shift_right_logical(lax.broadcasted_iota(I32, (sr, 1), 0), rows.bit_length() - 1)
    for g in range(SWA_KV_HEADS):
        slope = jnp.zeros((sr, 1), F32)
        sink = jnp.zeros((sr, 1), F32)
        for j in range(SWA_GROUP):
            h = g * SWA_GROUP + j
            slope = jnp.where(head_in_group == j, 2.0 ** (-8.0 * (h + 1) / SWA_HEADS), slope)
            sink = jnp.where(head_in_group == j, sinks_ref[h], sink)
        kv = slice(g * SWA_HEAD_DIM, (g + 1) * SWA_HEAD_DIM)
        qg = jnp.concatenate(
            [q_ref[:, (g * SWA_GROUP + j) * SWA_HEAD_DIM:(g * SWA_GROUP + j + 1) * SWA_HEAD_DIM]
             for j in range(SWA_GROUP)], axis=0).astype(BF16)
        s = lax.dot_general(qg, keys[:, kv], NT_DIMS, preferred_element_type=F32)
        s = s * (SWA_HEAD_DIM ** -0.5) - slope * dist
        s = jnp.where(allowed, s, -jnp.inf)
        m = jnp.maximum(jnp.max(s, axis=-1, keepdims=True), sink)
        p = jnp.exp(s - m)
        denom = jnp.sum(p, axis=-1, keepdims=True) + jnp.exp(sink - m)
        o = jnp.dot(p.astype(BF16), vals[:, kv], preferred_element_type=F32) / denom
        for j in range(0, SWA_GROUP, 2):
            c0 = (g * SWA_GROUP + j) * SWA_HEAD_DIM
            o_ref[:, c0:c0 + LANES] = jnp.concatenate(
                [o[j * rows:(j + 1) * rows], o[(j + 1) * rows:(j + 2) * rows]], axis=-1).astype(o_ref.dtype)


def _swa(z, sinks, k_prev, v_prev, prev_map, mix_in, *, nb, n_blocks, rows, row0, q_base, q_stride):
    rb0 = row0 // rows
    rmap = lambda colblk: (lambda b, i: (rb0 + b * n_blocks + i, colblk))
    return pl.pallas_call(
        functools.partial(_swa_kernel, rows=rows, q_base=q_base, q_stride=q_stride),
        grid=(nb, n_blocks),
        in_specs=[pl.BlockSpec(memory_space=pltpu.SMEM),
                  pl.BlockSpec((rows, SWA_QW), rmap(COL_SQ // SWA_QW)),
                  pl.BlockSpec((rows, SWA_KVW), rmap(COL_SK // SWA_KVW)),
                  pl.BlockSpec((rows, SWA_KVW), rmap(COL_SV // SWA_KVW)),
                  pl.BlockSpec((SWA_WINDOW, SWA_KVW), prev_map[0]),
                  pl.BlockSpec((SWA_WINDOW, SWA_KVW), prev_map[1]),
                  pl.BlockSpec(memory_space=pl.ANY)],
        out_specs=pl.BlockSpec((rows, SWA_QW), rmap(GLA_VW // SWA_QW)),
        out_shape=jax.ShapeDtypeStruct(mix_in.shape, mix_in.dtype),
        input_output_aliases={6: 0},
        compiler_params=_params(2),
        name="swa",
    )(sinks, z, z, z, k_prev, v_prev, mix_in)


def _mem_attn_kernel(q_ref, k_ref, v_ref, o_in_ref, o_ref, *, head_dim):
    del o_in_ref
    for h in range(MEM_HEADS):
        hs = slice(h * head_dim, (h + 1) * head_dim)
        kh = k_ref[:, hs].astype(BF16)
        vh = v_ref[:, hs].astype(BF16)
        s = lax.dot_general(q_ref[:, hs], kh, NT_DIMS, preferred_element_type=F32) * (head_dim ** -0.5)
        m = jnp.max(s, axis=-1, keepdims=True)
        p = jnp.exp(s - m)
        denom = jnp.sum(p, axis=-1, keepdims=True)
        o = jnp.dot(p.astype(BF16), vh, preferred_element_type=F32) / denom
        o_ref[:, hs] = o.astype(o_ref.dtype)


def _mem_attn(q, mk, mv, o_in, *, nb, n_tiles, rows, row0, mem_tokens):
    t, d = q.shape
    rb0 = row0 // rows
    qmap = lambda b, i: (rb0 + b * n_tiles + i, 0)
    in_specs = [pl.BlockSpec((rows, d), qmap),
                pl.BlockSpec((mem_tokens, d), lambda b, i: (b, 0)),
                pl.BlockSpec((mem_tokens, d), lambda b, i: (b, 0)),
                pl.BlockSpec(memory_space=pl.ANY)]
    aliases = {}
    if o_in is None:
        o_in = jnp.zeros((SUBLANES, LANES), BF16)
    else:
        aliases = {3: 0}
    return pl.pallas_call(
        functools.partial(_mem_attn_kernel, head_dim=d // MEM_HEADS),
        grid=(nb, n_tiles),
        in_specs=in_specs,
        out_specs=pl.BlockSpec((rows, d), qmap),
        out_shape=jax.ShapeDtypeStruct((t, d), BF16),
        input_output_aliases=aliases,
        compiler_params=_params(2),
        name="mem_attn",
    )(q, mk, mv, o_in)


def _pick_first_max(cur, iota, sentinel):
    mx = jnp.max(cur, axis=0, keepdims=True)
    first = jnp.min(jnp.where(cur == mx, iota, sentinel), axis=0, keepdims=True)
    return iota == first, first


def _router_kernel(x_ref, wh_ref, wl_ref, bias_ref, idx_ref, gate_ref, rank_ref, cnt_ref, run_ref, *, tm):
    i = pl.program_id(0)

    @pl.when(i == 0)
    def _():
        run_ref[...] = jnp.zeros_like(run_ref)

    x = x_ref[...]
    xh = x.astype(BF16)
    xl = (x - xh.astype(F32)).astype(BF16)
    wh = wh_ref[...]
    logits = (lax.dot_general(wh, xh, NT_DIMS, preferred_element_type=F32)
              + lax.dot_general(wh, xl, NT_DIMS, preferred_element_type=F32)
              + lax.dot_general(wl_ref[...], xh, NT_DIMS, preferred_element_type=F32))
    scores = _sigmoid(logits)
    sel = scores + bias_ref[...]
    neg_inf = -jnp.inf

    li = lax.broadcasted_iota(I32, (GROUP_SIZE, tm), 0)
    grp_rows = []
    for g in range(N_GROUPS):
        blk = sel[g * GROUP_SIZE:(g + 1) * GROUP_SIZE, :]
        pick, _ = _pick_first_max(blk, li, GROUP_SIZE)
        m1 = jnp.max(blk, axis=0, keepdims=True)
        m2 = jnp.max(jnp.where(pick, neg_inf, blk), axis=0, keepdims=True)
        grp_rows.append(m1 + m2)
    grp = jnp.concatenate(grp_rows, axis=0)

    gi = lax.broadcasted_iota(I32, (N_GROUPS, tm), 0)
    gsel = jnp.zeros((N_GROUPS, tm), F32)
    cur = grp
    for _ in range(TOPK_GROUPS):
        pick, _ = _pick_first_max(cur, gi, N_GROUPS)
        gsel = jnp.where(pick, 1.0, gsel)
        cur = jnp.where(pick, neg_inf, cur)
    emask = jnp.concatenate(
        [jnp.broadcast_to(gsel[g:g + 1, :], (GROUP_SIZE, tm)) for g in range(N_GROUPS)], axis=0)

    ei = lax.broadcasted_iota(I32, (N_EXPERTS, tm), 0)
    cur = jnp.where(emask > 0.5, sel, neg_inf)
    chosen = jnp.zeros((N_EXPERTS, tm), F32)
    idx_rows, w_rows = [], []
    for _ in range(TOP_K):
        pick, first = _pick_first_max(cur, ei, N_EXPERTS)
        idx_rows.append(first)
        w_rows.append(jnp.sum(jnp.where(pick, scores, 0.0), axis=0, keepdims=True))
        chosen = jnp.where(pick, 1.0, chosen)
        cur = jnp.where(pick, neg_inf, cur)
    idx = jnp.concatenate(idx_rows, axis=0)
    w = jnp.concatenate(w_rows, axis=0)
    gate_ref[...] = w / jnp.sum(w, axis=0, keepdims=True) * ROUTED_SCALE
    idx_ref[...] = idx

    ti = lax.broadcasted_iota(I32, (tm, tm), 0)
    tj = lax.broadcasted_iota(I32, (tm, tm), 1)
    before = jnp.where(ti < tj, 1.0, 0.0).astype(BF16)
    local = jnp.dot(chosen.astype(BF16), before, preferred_element_type=F32)
    total = local + run_ref[:, 0:1]
    rank_rows = [jnp.sum(jnp.where(ei == idx_rows[k], total, 0.0), axis=0, keepdims=True)
                 for k in range(TOP_K)]
    rank_ref[...] = jnp.concatenate(rank_rows, axis=0).astype(I32)
    run_ref[...] = run_ref[...] + jnp.sum(chosen, axis=1, keepdims=True)
    cnt_ref[...] = run_ref[...].astype(I32)


def _router(x, w_hi_t, w_lo_t, bias, tm):
    t, d = x.shape
    return pl.pallas_call(
        functools.partial(_router_kernel, tm=tm),
        grid=(t // tm,),
        in_specs=[pl.BlockSpec((tm, d), lambda i: (i, 0)),
                  pl.BlockSpec((N_EXPERTS, d), lambda i: (0, 0)),
                  pl.BlockSpec((N_EXPERTS, d), lambda i: (0, 0)),
                  pl.BlockSpec((N_EXPERTS, 1), lambda i: (0, 0))],
        out_specs=[pl.BlockSpec((TOP_K, tm), lambda i: (0, i)),
                   pl.BlockSpec((TOP_K, tm), lambda i: (0, i)),
                   pl.BlockSpec((TOP_K, tm), lambda i: (0, i)),
                   pl.BlockSpec((N_EXPERTS, LANES), lambda i: (0, 0))],
        out_shape=[jax.ShapeDtypeStruct((TOP_K, t), I32),
                   jax.ShapeDtypeStruct((TOP_K, t), F32),
                   jax.ShapeDtypeStruct((TOP_K, t), I32),
                   jax.ShapeDtypeStruct((N_EXPERTS, LANES), I32)],
        scratch_shapes=[pltpu.VMEM((N_EXPERTS, LANES), F32)],
        compiler_params=_params(1),
        name="router",
    )(x, w_hi_t, w_lo_t, bias.reshape(N_EXPERTS, 1))


PAD_BITS = tuple(1 << s for s in reversed(range(EXPERT_ROWS.bit_length() - 1)))


def _dispatch_kernel(zstart_ref, zcount_ref, dest_ref, x_ref, xs_ref, zero_ref, sem, zsem, *, tm):
    i = pl.program_id(0)

    def start_group(g, carry):
        for j in range(SUBLANES):
            for k in range(TOP_K):
                d = dest_ref[k * tm + g * SUBLANES + j]
                pltpu.make_async_copy(x_ref.at[g, pl.ds(j, 1), :], xs_ref.at[d], sem).start(priority=k % 2)
        return carry

    lax.fori_loop(0, tm // SUBLANES, start_group, 0)
    for k in range(TOP_K):
        pltpu.make_async_copy(xs_ref.at[pl.ds(0, tm)], xs_ref.at[pl.ds(0, tm)], sem).wait()

    @pl.when(i == pl.num_programs(0) - 1)
    def _():
        zero_ref[...] = jnp.zeros_like(zero_ref)

        def fill(e, wait):
            pos = zstart_ref[e]
            cnt = zcount_ref[e]
            for bit in PAD_BITS:
                has = (cnt & bit) != 0

                @pl.when(has)
                def _():
                    cp = pltpu.make_async_copy(zero_ref.at[pl.ds(0, bit)], xs_ref.at[pl.ds(pos, bit)], zsem)
                    if wait:
                        cp.wait()
                    else:
                        cp.start()

                pos = pos + jnp.where(has, bit, 0)

        def fill_start(e, carry):
            fill(e, False)
            return carry

        def fill_wait(e, carry):
            fill(e, True)
            return carry

        lax.fori_loop(0, N_EXPERTS, fill_start, 0)
        lax.fori_loop(0, N_EXPERTS, fill_wait, 0)


def _dispatch(x, dest_tiles, zstart, zcount, n_rows, tm):
    t = x.shape[0] * SUBLANES
    d = x.shape[2]
    return pl.pallas_call(
        functools.partial(_dispatch_kernel, tm=tm),
        grid_spec=pltpu.PrefetchScalarGridSpec(
            num_scalar_prefetch=2,
            grid=(t // tm,),
            in_specs=[pl.BlockSpec((TOP_K * tm,), lambda i, zs, zc: (i,), memory_space=pltpu.SMEM),
                      pl.BlockSpec((tm // SUBLANES, SUBLANES, d), lambda i, zs, zc: (i, 0, 0))],
            out_specs=pl.BlockSpec(memory_space=pl.ANY),
            scratch_shapes=[pltpu.VMEM((EXPERT_ROWS // 2, 1, d), x.dtype),
                            pltpu.SemaphoreType.DMA(()),
                            pltpu.SemaphoreType.DMA(())]),
        out_shape=jax.ShapeDtypeStruct((n_rows, 1, d), x.dtype),
        compiler_params=_params(1),
        name="moe_dispatch",
    )(zstart, zcount, dest_tiles, x)


def _expert_kernel(be_ref, nxt_ref, nv_ref, xs_hbm, wg_hbm, wu_hbm, wd_hbm, ys_hbm,
                   wgs_ref, wus_ref, wds_ref, wgb_ref, wub_ref, wdb_ref, xbuf_ref, ybuf_ref,
                   wsems, xsems, ysems):
    b = pl.program_id(0)
    n_valid = nv_ref[0]
    e = be_ref[b]
    prev = be_ref[jnp.maximum(b - 1, 0)]
    fresh = jnp.logical_or(b == 0, e != prev)
    slot = lax.rem(b, 2)

    def weight_copies(expert):
        return (pltpu.make_async_copy(wg_hbm.at[expert], wgs_ref, wsems.at[0]),
                pltpu.make_async_copy(wu_hbm.at[expert], wus_ref, wsems.at[1]),
                pltpu.make_async_copy(wd_hbm.at[expert], wds_ref, wsems.at[2]))

    def rows_in(blk, to_slot):
        r0 = pl.multiple_of(blk * EXPERT_ROWS, EXPERT_ROWS)
        return pltpu.make_async_copy(xs_hbm.at[pl.ds(r0, EXPERT_ROWS), 0, :], xbuf_ref.at[to_slot],
                                     xsems.at[to_slot])

    def rows_out(blk, from_slot):
        r0 = pl.multiple_of(blk * EXPERT_ROWS, EXPERT_ROWS)
        return pltpu.make_async_copy(ybuf_ref.at[from_slot], ys_hbm.at[pl.ds(r0, EXPERT_ROWS), 0, :],
                                     ysems.at[from_slot])

    @pl.when(b == 0)
    def _():
        for cp in weight_copies(e):
            cp.start(priority=1)
        rows_in(b, slot).start()

    @pl.when(fresh)
    def _():
        for cp in weight_copies(e):
            cp.wait()
        wgb_ref[...] = wgs_ref[...].astype(BF16)
        wub_ref[...] = wus_ref[...].astype(BF16)
        wdb_ref[...] = wds_ref[...].astype(BF16)
        nxt = nxt_ref[b]

        @pl.when(nxt >= 0)
        def _():
            for cp in weight_copies(nxt):
                cp.start(priority=1)

    @pl.when(b < n_valid)
    def _():
        rows_in(b, slot).wait()

        @pl.when(b + 1 < n_valid)
        def _():
            rows_in(b + 1, 1 - slot).start()

        @pl.when(b >= 2)
        def _():
            rows_out(b - 2, slot).wait()

        half = xbuf_ref.shape[-1]
        x_lo, x_hi = _unpack_halves(xbuf_ref[slot])
        x_lo = x_lo.astype(BF16)
        x_hi = x_hi.astype(BF16)
        hg = (jnp.dot(x_lo, wgb_ref[:half, :], preferred_element_type=F32)
              + jnp.dot(x_hi, wgb_ref[half:, :], preferred_element_type=F32))
        hu = (jnp.dot(x_lo, wub_ref[:half, :], preferred_element_type=F32)
              + jnp.dot(x_hi, wub_ref[half:, :], preferred_element_type=F32))
        act = (_silu(hg) * hu).astype(BF16)
        ybuf_ref[slot] = _pack_halves(jnp.dot(act, wdb_ref[...], preferred_element_type=F32))
        rows_out(b, slot).start()

        @pl.when(b == n_valid - 1)
        def _():
            rows_out(b, slot).wait()

            @pl.when(b >= 1)
            def _():
                rows_out(b - 1, 1 - slot).wait()


def _experts(xs, wg, wu, wd, block_expert, next_expert, n_valid):
    n_rows = xs.shape[0]
    d = wg.shape[1]
    de = wg.shape[2]
    nb = n_rows // EXPERT_ROWS
    hbm = pl.BlockSpec(memory_space=pl.ANY)
    return pl.pallas_call(
        _expert_kernel,
        grid_spec=pltpu.PrefetchScalarGridSpec(
            num_scalar_prefetch=3,
            grid=(nb,),
            in_specs=[hbm, hbm, hbm, hbm],
            out_specs=hbm,
            scratch_shapes=[pltpu.VMEM((d, de), F32), pltpu.VMEM((d, de), F32), pltpu.VMEM((de, d), F32),
                            pltpu.VMEM((d, de), BF16), pltpu.VMEM((d, de), BF16), pltpu.VMEM((de, d), BF16),
                            pltpu.VMEM((2, EXPERT_ROWS, d // 2), I32), pltpu.VMEM((2, EXPERT_ROWS, d // 2), I32),
                            pltpu.SemaphoreType.DMA((3,)), pltpu.SemaphoreType.DMA((2,)),
                            pltpu.SemaphoreType.DMA((2,))]),
        out_shape=jax.ShapeDtypeStruct((n_rows, 1, d // 2), I32),
        compiler_params=_params(1),
        name="moe_experts",
    )(block_expert, next_expert, n_valid, xs, wg, wu, wd)


def _combine_kernel(dest0_ref, destn_ref, x_ref, gate_ref, ys_ref, swg_ref, swu_ref, swd_ref, g_ref, b_ref,
                    op_ref, os_ref, buf_ref, acc_ref, sems, *, tm, n_prompt_tiles):
    i = pl.program_id(0)
    n = pl.num_programs(0)
    slot = lax.rem(i, 2)
    half = buf_ref.shape[-1]

    def issue_group(dest_ref, to_slot, g):
        for j in range(COMBINE_GROUP):
            t = g * COMBINE_GROUP + j
            for k in range(TOP_K):
                d = dest_ref[k * tm + t]
                pltpu.make_async_copy(ys_ref.at[d], buf_ref.at[to_slot, k, g, pl.ds(j, 1), :],
                                      sems.at[to_slot]).start(priority=k % 2)

    @pl.when(i == 0)
    def _():
        def first(g, carry):
            issue_group(dest0_ref, 0, g)
            return carry
        lax.fori_loop(0, tm // COMBINE_GROUP, first, 0)

    for k in range(TOP_K):
        pltpu.make_async_copy(ys_ref.at[pl.ds(0, tm)], ys_ref.at[pl.ds(0, tm)], sems.at[slot]).wait()

    def reduce_group(g):
        r0 = pl.multiple_of(g * COMBINE_GROUP, COMBINE_GROUP)
        gate = gate_ref[pl.ds(r0, COMBINE_GROUP), :]
        acc_lo = acc_hi = None
        for k in range(TOP_K):
            y_lo, y_hi = _unpack_halves(buf_ref[slot, k, g])
            gk = gate[:, k:k + 1]
            acc_lo = gk * y_lo if k == 0 else acc_lo + gk * y_lo
            acc_hi = gk * y_hi if k == 0 else acc_hi + gk * y_hi
        acc_ref[pl.ds(r0, COMBINE_GROUP), :half] = acc_lo
        acc_ref[pl.ds(r0, COMBINE_GROUP), half:] = acc_hi

    @pl.when(i + 1 < n)
    def _():
        def body(g, carry):
            issue_group(destn_ref, 1 - slot, g)
            reduce_group(g)
            return carry
        lax.fori_loop(0, tm // COMBINE_GROUP, body, 0)

    @pl.when(i + 1 >= n)
    def _():
        def body(g, carry):
            reduce_group(g)
            return carry
        lax.fori_loop(0, tm // COMBINE_GROUP, body, 0)

    x = x_ref[...]
    xb = x.astype(BF16)
    act = (_silu(jnp.dot(xb, swg_ref[...], preferred_element_type=F32))
           * jnp.dot(xb, swu_ref[...], preferred_element_type=F32)).astype(BF16)
    shared = jnp.dot(act, swd_ref[...], preferred_element_type=F32)
    out = _layer_norm(DEEPNORM_ALPHA * x + (acc_ref[...] + shared), g_ref[...], b_ref[...])

    @pl.when(i < n_prompt_tiles)
    def _():
        op_ref[...] = out

    @pl.when(i >= n_prompt_tiles)
    def _():
        os_ref[...] = out


def _combine(x, gate_t, dest_tiles, ys, swg, swu, swd, g, b, tm, n_prompt_rows):
    t, d = x.shape
    ds = swg.shape[1]
    n_tiles = t // tm
    npt = n_prompt_rows // tm
    const = lambda i: (0, 0)
    return pl.pallas_call(
        functools.partial(_combine_kernel, tm=tm, n_prompt_tiles=npt),
        grid=(n_tiles,),
        in_specs=[pl.BlockSpec((TOP_K * tm,), lambda i: (0,), memory_space=pltpu.SMEM),
                  pl.BlockSpec((TOP_K * tm,), lambda i: (jnp.minimum(i + 1, n_tiles - 1),),
                               memory_space=pltpu.SMEM),
                  pl.BlockSpec((tm, d), lambda i: (i, 0)),
                  pl.BlockSpec((tm, TOP_K), lambda i: (i, 0)),
                  pl.BlockSpec(memory_space=pl.ANY),
                  pl.BlockSpec((d, ds), const),
                  pl.BlockSpec((d, ds), const),
                  pl.BlockSpec((ds, d), const),
                  pl.BlockSpec((1, d), const),
                  pl.BlockSpec((1, d), const)],
        out_specs=[pl.BlockSpec((tm, d), lambda i: (jnp.minimum(i, npt - 1), 0)),
                   pl.BlockSpec((tm, d), lambda i: (jnp.maximum(i - npt, 0), 0))],
        out_shape=[jax.ShapeDtypeStruct((n_prompt_rows, d), F32),
                   jax.ShapeDtypeStruct((t - n_prompt_rows, d), F32)],
        scratch_shapes=[pltpu.VMEM((2, TOP_K, tm // COMBINE_GROUP, COMBINE_GROUP, d // 2), I32),
                        pltpu.VMEM((tm, d), F32),
                        pltpu.SemaphoreType.DMA((2,))],
        compiler_params=_params(1),
        name="moe_combine",
    )(dest_tiles, dest_tiles, x, gate_t, ys, swg, swu, swd, g.reshape(1, d), b.reshape(1, d))


def _tile_major(a, tm):
    k, t = a.shape
    return a.reshape(k, t // tm, tm).transpose(1, 0, 2).reshape(-1)


def _moe_ln(x, x_packed, router_w, router_bias, wg, wu, wd, swg, swu, swd, g, b, n_prompt_rows):
    t, d = x.shape
    w_t = router_w.T
    w_hi = w_t.astype(BF16)
    w_lo = (w_t - w_hi.astype(F32)).astype(BF16)
    tr = _divisor_tile(t, 512, LANES)
    idx, gate, rank, cnt = _router(x, w_hi, w_lo, router_bias, tr)

    counts = cnt[:, 0]
    padded = (counts + EXPERT_ROWS - 1) // EXPERT_ROWS * EXPERT_ROWS
    pad_end = jnp.cumsum(padded)
    pad_start = pad_end - padded
    experts = jnp.arange(N_EXPERTS, dtype=I32)
    onehot = idx[None] == experts[:, None, None]
    dest = jnp.sum(jnp.where(onehot, pad_start[:, None, None], 0), axis=0) + rank
    n_blocks = (t * TOP_K + N_EXPERTS * (EXPERT_ROWS - 1)) // EXPERT_ROWS + 1
    n_valid = (pad_end[-1] // EXPERT_ROWS).astype(I32)
    blk = jnp.minimum(jnp.arange(n_blocks, dtype=I32), n_valid - 1)
    block_expert = jnp.minimum(
        jnp.sum(pad_end[None, :] <= (blk * EXPERT_ROWS)[:, None], axis=1), N_EXPERTS - 1).astype(I32)
    later_used = (experts[None, :] > experts[:, None]) & (counts[None, :] > 0)
    next_used = jnp.min(jnp.where(later_used, experts[None, :], N_EXPERTS), axis=1)
    next_used = jnp.where(next_used == N_EXPERTS, -1, next_used)
    next_expert = jnp.sum(
        jnp.where(block_expert[:, None] == experts[None, :], next_used[None, :], 0), axis=1).astype(I32)

    td = _divisor_tile(t, 256, SUBLANES)
    xs = _dispatch(x_packed, _tile_major(dest, td), (pad_start + counts).astype(I32),
                   (padded - counts).astype(I32), n_blocks * EXPERT_ROWS, td)
    ys = _experts(xs, wg, wu, wd, block_expert, next_expert, n_valid.reshape(1))
    tc = _divisor_tile(math.gcd(n_prompt_rows, t - n_prompt_rows), 128, SUBLANES)
    return _combine(x, gate.T, _tile_major(dest, tc), ys, swg.astype(BF16), swu.astype(BF16),
                    swd.astype(BF16), g, b, tc, n_prompt_rows)


def kernel(x_prompt, x_sample, mem_prompt, state_gla, cache_swa_k, cache_swa_v, cache_mem_k, cache_mem_v,
           w_in, gla_w_gate2, gla_b_gate, gla_norm_g, swa_sinks, w_mix_out, ln1_g, ln1_b,
           mem_wq, mem_wk, mem_wv, mem_wo, ln2_g, ln2_b,
           router_w, router_bias, exp_w_gate, exp_w_up, exp_w_down, sh_w_gate, sh_w_up, sh_w_down,
           ln3_g, ln3_b):
    nbp, seq, d = x_prompt.shape
    nbs, dec, _ = x_sample.shape
    tp, ts = nbp * seq, nbs * dec
    t = tp + ts
    mem_tokens = mem_prompt.shape[1]
    assert state_gla.shape[0] == DEPTH and seq % SWA_WINDOW == 0 and dec % SUBLANES == 0
    assert cache_swa_k.shape[2] == SWA_WINDOW and tp % dec == 0
    tm = _divisor_tile(math.gcd(tp, ts), 512, 16)
    l = 0

    xp = x_prompt.reshape(tp, d)
    xs = x_sample.reshape(ts, d)

    wi_t = w_in[l].T
    o = 0
    seg = []
    for size in (GLA_QK, GLA_QK, GLA_VW, GLA_VW, GLA_GATE_RANK, SWA_QW, SWA_KVW, SWA_KVW):
        seg.append(wi_t[o:o + size])
        o += size
    gq, gk, gv, gr, gg, sq, sk, sv = seg
    w_in_rt = jnp.concatenate(
        [gq, gk, gv, gr, sq, sk, sv, gg,
         jnp.zeros((IN_COLS_PADDED - COL_GG - GLA_GATE_RANK, d), wi_t.dtype)], axis=0).astype(BF16)
    z = _matmul_split(xp, xs, w_in_rt, tm, IN_COLS_PADDED // 2, F32)

    wa2_pad = jnp.concatenate(
        [gla_w_gate2[l], jnp.zeros((LANES - GLA_GATE_RANK, GLA_QK), F32)], axis=0)
    wa2_hi = wa2_pad.astype(BF16)
    wa2_lo = (wa2_pad - wa2_hi.astype(F32)).astype(BF16)
    wa2_stack = jnp.concatenate([wa2_hi, wa2_lo, wa2_hi], axis=0)
    zero_state = jnp.zeros((nbp, GLA_HEADS, GLA_DV, GLA_DK), F32)
    gla_rows = _divisor_tile(seq, GLA_STEP_ROWS, CHUNK)
    mix, st_p = _gla(z, wa2_stack, gla_b_gate[l], gla_norm_g[l], zero_state, None,
                     nb=nbp, n_chunks=seq // gla_rows, rows=gla_rows, chunk=CHUNK, row0=0)
    mix, st_s = _gla(z, wa2_stack, gla_b_gate[l], gla_norm_g[l], jnp.swapaxes(state_gla[l], -1, -2), mix,
                     nb=nbs, n_chunks=1, rows=dec, chunk=dec, row0=tp)
    nblk = seq // SWA_WINDOW
    prev_p = lambda col: (lambda b, i: (b * nblk + jnp.maximum(i - 1, 0), col))
    mix = _swa(z, swa_sinks[l], z, z, (prev_p(COL_SK // SWA_KVW), prev_p(COL_SV // SWA_KVW)), mix,
               nb=nbp, n_blocks=nblk, rows=SWA_WINDOW, row0=0, q_base=0, q_stride=SWA_WINDOW)
    ck = cache_swa_k[l].reshape(nbs * SWA_WINDOW, SWA_KVW)
    cv = cache_swa_v[l].reshape(nbs * SWA_WINDOW, SWA_KVW)
    prev_s = lambda b, i: (b, 0)
    mix = _swa(z, swa_sinks[l], ck, cv, (prev_s, prev_s), mix,
               nb=nbs, n_blocks=1, rows=dec, row0=tp, q_base=PAST_LEN, q_stride=0)
    h1 = _matmul_res_ln(mix, w_mix_out[l].astype(BF16), (xp, xs), ln1_g[l], ln1_b[l], tm)

    mem = mem_prompt.reshape(nbp * mem_tokens, d)
    tmem = _divisor_tile(nbp * mem_tokens, 512, 16)
    mk = _matmul(mem, mem_wk[l].astype(BF16), tmem, _divisor_tile(d, 1024, LANES), F32)
    mv = _matmul(mem, mem_wv[l].astype(BF16), tmem, _divisor_tile(d, 1024, LANES), F32)
    q = _matmul(h1, mem_wq[l].astype(BF16), tm, d, BF16)
    tq = _divisor_tile(seq, 512, 16)
    att = _mem_attn(q, mk, mv, None, nb=nbp, n_tiles=seq // tq, rows=tq, row0=0, mem_tokens=mem_tokens)
    att = _mem_attn(q, cache_mem_k[l].reshape(nbs * mem_tokens, d), cache_mem_v[l].reshape(nbs * mem_tokens, d),
                    att, nb=nbs, n_tiles=1, rows=dec, row0=tp, mem_tokens=mem_tokens)
    h2, h2_packed = _matmul_res_ln(att, mem_wo[l].astype(BF16), h1, ln2_g[l], ln2_b[l], tm, emit_packed=True)

    out_p, out_s = _moe_ln(h2, h2_packed, router_w[l], router_bias[l], exp_w_gate[l], exp_w_up[l],
                           exp_w_down[l], sh_w_gate[l], sh_w_up[l], sh_w_down[l], ln3_g[l], ln3_b[l], tp)

    y_prompt = out_p.reshape(nbp, seq, d)
    y_sample = out_s.reshape(nbs, dec, d)
    kv = z[:, COL_SK:COL_SK + 2 * SWA_KVW]
    kvp = kv[:tp].reshape(nbp, seq, 2 * SWA_KVW)[:, seq - SWA_WINDOW:]
    kvs = kv[tp:].reshape(nbs, dec, 2 * SWA_KVW)
    kv_shape = lambda a: a.reshape(a.shape[0], a.shape[1], SWA_KV_HEADS, SWA_HEAD_DIM)[None]
    mem_shape = lambda a: a.reshape(nbp, mem_tokens, MEM_HEADS, d // MEM_HEADS)[None]
    return (y_prompt, y_sample,
            jnp.swapaxes(st_p, -1, -2)[None],
            kv_shape(kvp[..., :SWA_KVW]), kv_shape(kvp[..., SWA_KVW:]),
            mem_shape(mk), mem_shape(mv),
            jnp.swapaxes(st_s, -1, -2)[None],
            kv_shape(kvs[..., :SWA_KVW]), kv_shape(kvs[..., SWA_KVW:]))
```

```python
import functools
import math

import jax
import jax.numpy as jnp
from jax import lax
from jax.experimental import pallas as pl
from jax.experimental.pallas import tpu as pltpu

F32 = jnp.float32
BF16 = jnp.bfloat16
I32 = jnp.int32

CHUNK = 64
PAST_LEN = 2048
GLA_HEADS = 8
GLA_DK = 64
GLA_DV = 128
GLA_QK = GLA_HEADS * GLA_DK
GLA_VW = GLA_HEADS * GLA_DV
GLA_GATE_RANK = 16
GLA_TAU = 16.0
SWA_HEADS = 16
SWA_KV_HEADS = 2
SWA_GROUP = SWA_HEADS // SWA_KV_HEADS
SWA_HEAD_DIM = 64
SWA_QW = SWA_HEADS * SWA_HEAD_DIM
SWA_KVW = SWA_KV_HEADS * SWA_HEAD_DIM
SWA_WINDOW = 128
WINDOW_CHUNKS = SWA_WINDOW // CHUNK
MEM_HEADS = 4
N_EXPERTS = 64
TOP_K = 8
N_GROUPS = 8
GROUP_SIZE = N_EXPERTS // N_GROUPS
TOPK_GROUPS = 4
ROUTED_SCALE = 2.5
DEPTH = 1
DEEPNORM_ALPHA = (2 * DEPTH) ** 0.25
LN_EPS = 1e-5
RMS_EPS = 1e-6

LANES = 128
SUBLANES = 8
VMEM_LIMIT_BYTES = 56 * 1024 * 1024

COL_GQ = 0
COL_GK = COL_GQ + GLA_QK
COL_GV = COL_GK + GLA_QK
COL_GR = COL_GV + GLA_VW
COL_SQ = COL_GR + GLA_VW
COL_SK = COL_SQ + SWA_QW
COL_SV = COL_SK + SWA_KVW
COL_GG = COL_SV + SWA_KVW
MXU_COLS = 256
IN_COLS_PADDED = -(-(COL_GG + LANES) // (2 * MXU_COLS)) * (2 * MXU_COLS)

EXPERT_ROWS = 256
COMBINE_GROUP = SUBLANES
LN_SUB_ROWS = 128
GLA_STEP_ROWS = 256

NT_DIMS = (((1,), (1,)), ((), ()))
TN_DIMS = (((0,), (0,)), ((), ()))


def _params(n_axes):
    return pltpu.CompilerParams(dimension_semantics=("arbitrary",) * n_axes,
                                vmem_limit_bytes=VMEM_LIMIT_BYTES)


def _sigmoid(x):
    return 1.0 / (1.0 + jnp.exp(-x))


def _silu(x):
    return x * _sigmoid(x)


def _layer_norm(x, g, b):
    mu = jnp.mean(x, axis=-1, keepdims=True)
    xc = x - mu
    var = jnp.mean(xc * xc, axis=-1, keepdims=True)
    return xc * lax.rsqrt(var + LN_EPS) * g + b


HI_MASK = -65536


def _round_to_bf16_bits(x):
    b = lax.bitcast_convert_type(x, I32)
    return b + (0x7FFF + (lax.shift_right_logical(b, 16) & 1))


def _pack_halves(x):
    c = x.shape[1] // 2
    lo = lax.shift_right_logical(_round_to_bf16_bits(x[:, :c]), 16)
    hi = _round_to_bf16_bits(x[:, c:]) & HI_MASK
    return lo | hi


def _unpack_halves(w):
    return (lax.bitcast_convert_type(lax.shift_left(w, 16), F32),
            lax.bitcast_convert_type(w & HI_MASK, F32))


def _divisor_tile(n, pref, mult):
    t = min(pref, n)
    while t > mult and (n % t or t % mult):
        t -= mult
    assert n % t == 0 and t % mult == 0, (n, pref, mult)
    return t


def _mm_kernel(x_ref, w_ref, o_ref, xb_ref):
    @pl.when(pl.program_id(1) == 0)
    def _():
        xb_ref[...] = x_ref[...].astype(BF16)

    o_ref[...] = jnp.dot(xb_ref[...], w_ref[...], preferred_element_type=F32).astype(o_ref.dtype)


def _matmul(x, w, tm, tn, out_dtype):
    m, k = x.shape
    n = w.shape[1]
    return pl.pallas_call(
        _mm_kernel,
        grid=(m // tm, n // tn),
        in_specs=[pl.BlockSpec((tm, k), lambda i, j: (i, 0)),
                  pl.BlockSpec((k, tn), lambda i, j: (0, j))],
        out_specs=pl.BlockSpec((tm, tn), lambda i, j: (i, j)),
        out_shape=jax.ShapeDtypeStruct((m, n), out_dtype),
        scratch_shapes=[pltpu.VMEM((tm, k), BF16)],
        compiler_params=_params(2),
        name="matmul",
    )(x, w)


def _split_rows_specs(tm, k, n_first_tiles, n_grid_axes):
    if n_grid_axes == 1:
        return [pl.BlockSpec((tm, k), lambda i: (jnp.minimum(i, n_first_tiles - 1), 0)),
                pl.BlockSpec((tm, k), lambda i: (jnp.maximum(i - n_first_tiles, 0), 0))]
    return [pl.BlockSpec((tm, k), lambda j, i: (jnp.minimum(i, n_first_tiles - 1), 0)),
            pl.BlockSpec((tm, k), lambda j, i: (jnp.maximum(i - n_first_tiles, 0), 0))]


def _mm_split_kernel(xa_ref, xb_ref, wt_ref, o_ref, *, n_first_tiles):
    i = pl.program_id(1)

    def product(x_ref):
        o_ref[...] = lax.dot_general(x_ref[...].astype(BF16), wt_ref[...], NT_DIMS,
                                     preferred_element_type=F32).astype(o_ref.dtype)

    @pl.when(i < n_first_tiles)
    def _():
        product(xa_ref)

    @pl.when(i >= n_first_tiles)
    def _():
        product(xb_ref)


def _matmul_split(xa, xb, w_t, tm, tn, out_dtype):
    ma, k = xa.shape
    m = ma + xb.shape[0]
    n = w_t.shape[0]
    return pl.pallas_call(
        functools.partial(_mm_split_kernel, n_first_tiles=ma // tm),
        grid=(n // tn, m // tm),
        in_specs=_split_rows_specs(tm, k, ma // tm, 2) + [pl.BlockSpec((tn, k), lambda j, i: (j, 0))],
        out_specs=pl.BlockSpec((tm, tn), lambda j, i: (i, j)),
        out_shape=jax.ShapeDtypeStruct((m, n), out_dtype),
        compiler_params=_params(2),
        name="matmul_split",
    )(xa, xb, w_t)


def _mm_res_ln_kernel(x_ref, w_ref, *refs, n_res, n_first_tiles):
    res_refs, (g_ref, b_ref, o_ref, *packed_ref) = refs[:n_res], refs[n_res:]
    tm = x_ref.shape[0]
    sub = min(tm, LN_SUB_ROWS)
    use_first = pl.program_id(0) < n_first_tiles
    for r in range(0, tm, sub):
        rows = slice(r, r + sub)
        y = jnp.dot(x_ref[rows, :], w_ref[...], preferred_element_type=F32)
        res = res_refs[0][rows, :]
        if n_res == 2:
            res = jnp.where(use_first, res, res_refs[1][rows, :])
        h = _layer_norm(DEEPNORM_ALPHA * res + y, g_ref[...], b_ref[...])
        o_ref[rows, :] = h
        if packed_ref:
            groups = slice(r // SUBLANES, (r + sub) // SUBLANES)
            packed_ref[0][groups] = _pack_halves(h).reshape(sub // SUBLANES, SUBLANES, h.shape[1] // 2)


def _matmul_res_ln(x, w, res, g, b, tm, emit_packed=False):
    m, k = x.shape
    n = w.shape[1]
    if isinstance(res, tuple):
        n_first_tiles = res[0].shape[0] // tm
        res_specs = _split_rows_specs(tm, n, n_first_tiles, 1)
    else:
        n_first_tiles = 0
        res = (res,)
        res_specs = [pl.BlockSpec((tm, n), lambda i: (i, 0))]
    out_specs = [pl.BlockSpec((tm, n), lambda i: (i, 0))]
    out_shape = [jax.ShapeDtypeStruct((m, n), F32)]
    if emit_packed:
        out_specs.append(pl.BlockSpec((tm // SUBLANES, SUBLANES, n // 2), lambda i: (i, 0, 0)))
        out_shape.append(jax.ShapeDtypeStruct((m // SUBLANES, SUBLANES, n // 2), I32))
    outs = pl.pallas_call(
        functools.partial(_mm_res_ln_kernel, n_res=len(res), n_first_tiles=n_first_tiles),
        grid=(m // tm,),
        in_specs=[pl.BlockSpec((tm, k), lambda i: (i, 0)),
                  pl.BlockSpec((k, n), lambda i: (0, 0))] + res_specs + [
                  pl.BlockSpec((1, n), lambda i: (0, 0)),
                  pl.BlockSpec((1, n), lambda i: (0, 0))],
        out_specs=out_specs,
        out_shape=out_shape,
        compiler_params=_params(1),
        name="matmul_res_ln",
    )(x, w, *res, g.reshape(1, n), b.reshape(1, n))
    return outs if emit_packed else outs[0]


def _split3_bf16(x):
    hi = x.astype(BF16)
    r1 = x - hi.astype(F32)
    mid = r1.astype(BF16)
    lo = (r1 - mid.astype(F32)).astype(BF16)
    return jnp.concatenate([hi, mid, lo], axis=-1)


def _gla_kernel(q_ref, k_ref, v_ref, r_ref, gg_ref, wa2_ref, ba_ref, ng_ref, s0_ref, mix_in_ref,
                o_ref, sfin_ref, st_ref, *, rows, chunk):
    del mix_in_ref
    n_sub = rows // chunk
    shift = chunk.bit_length() - 1
    assert chunk == 1 << shift

    @pl.when(pl.program_id(1) == 0)
    def _():
        st_ref[...] = s0_ref[0]

    gg = gg_ref[...]
    gg_hi = gg.astype(BF16)
    gg_lo = (gg - gg_hi.astype(F32)).astype(BF16)
    gate = jnp.dot(jnp.concatenate([gg_hi, gg_hi, gg_lo], axis=-1), wa2_ref[...],
                   preferred_element_type=F32) + ba_ref[...]
    log_a = (jnp.minimum(gate, 0.0) - jnp.log1p(jnp.exp(-jnp.abs(gate)))) / GLA_TAU

    ri = lax.broadcasted_iota(I32, (rows, rows), 0)
    ci = lax.broadcasted_iota(I32, (rows, rows), 1)
    same_chunk = lax.shift_right_logical(ri, shift) == lax.shift_right_logical(ci, shift)
    causal = same_chunk & (ri >= ci)
    ones = jnp.concatenate([jnp.where(causal, 1.0, 0.0), jnp.where(same_chunk, 1.0, 0.0)], axis=0).astype(BF16)
    sums = jnp.dot(ones, _split3_bf16(log_a), preferred_element_type=F32)
    sums = sums[:, :GLA_QK] + sums[:, GLA_QK:2 * GLA_QK] + sums[:, 2 * GLA_QK:]
    bcum = sums[:rows]
    b_last = sums[rows:]
    q = q_ref[...] * (GLA_DK ** -0.5)
    k = k_ref[...]
    q_in = (q * jnp.exp(bcum)).astype(BF16)
    k_in = (k * jnp.exp(-bcum)).astype(BF16)
    k_out = (k * jnp.exp(b_last - bcum)).astype(BF16)
    decay = jnp.exp(b_last)
    ng = ng_ref[...]
    blk_r = lax.shift_right_logical(lax.broadcasted_iota(I32, (rows, n_sub * GLA_DK), 0), shift)
    blk_c = lax.shift_right_logical(lax.broadcasted_iota(I32, (rows, n_sub * GLA_DK), 1),
                                    GLA_DK.bit_length() - 1)
    for h in range(GLA_HEADS):
        ks = slice(h * GLA_DK, (h + 1) * GLA_DK)
        vs = slice(h * GLA_DV, (h + 1) * GLA_DV)
        vh = v_ref[:, vs].astype(BF16)
        qh = q_in[:, ks]
        attn = lax.dot_general(qh, k_in[:, ks], NT_DIMS, preferred_element_type=F32)
        attn = jnp.where(causal, attn, 0.0).astype(BF16)
        o = jnp.dot(attn, vh, preferred_element_type=F32)
        ko = k_out[:, ks]
        if n_sub > 1:
            ko = jnp.where(blk_r == blk_c, jnp.concatenate([ko] * n_sub, axis=-1), 0.0)
        upd_t = lax.dot_general(vh, ko, TN_DIMS, preferred_element_type=F32)
        st = st_ref[h]
        states = []
        for c in range(n_sub):
            states.append(st.astype(BF16))
            st = st * decay[c * chunk:c * chunk + 1, ks] + upd_t[:, c * GLA_DK:(c + 1) * GLA_DK]
        st_ref[h] = st
        o_all = lax.dot_general(qh, jnp.concatenate(states, axis=0), NT_DIMS, preferred_element_type=F32)
        o = o + jnp.concatenate(
            [o_all[c * chunk:(c + 1) * chunk, c * GLA_DV:(c + 1) * GLA_DV] for c in range(n_sub)], axis=0)
        o = o * lax.rsqrt(jnp.mean(o * o, axis=-1, keepdims=True) + RMS_EPS) * ng
        o_ref[:, vs] = (o * _silu(r_ref[:, vs])).astype(o_ref.dtype)
    sfin_ref[0] = st_ref[...]


def _gla(z, wa2_stack, ba, norm_g, s0_t, mix_in, *, nb, n_chunks, rows, chunk, row0):
    t = z.shape[0]
    rb0 = row0 // rows
    rmap = lambda colblk: (lambda b, c: (rb0 + b * n_chunks + c, colblk))
    in_specs = [
        pl.BlockSpec((rows, GLA_QK), rmap(COL_GQ // GLA_QK)),
        pl.BlockSpec((rows, GLA_QK), rmap(COL_GK // GLA_QK)),
        pl.BlockSpec((rows, GLA_VW), rmap(COL_GV // GLA_VW)),
        pl.BlockSpec((rows, GLA_VW), rmap(COL_GR // GLA_VW)),
        pl.BlockSpec((rows, LANES), rmap(COL_GG // LANES)),
        pl.BlockSpec((3 * LANES, GLA_QK), lambda b, c: (0, 0)),
        pl.BlockSpec((1, GLA_QK), lambda b, c: (0, 0)),
        pl.BlockSpec((1, GLA_DV), lambda b, c: (0, 0)),
        pl.BlockSpec((1, GLA_HEADS, GLA_DV, GLA_DK), lambda b, c: (b, 0, 0, 0)),
    ]
    args = [z, z, z, z, z, wa2_stack, ba.reshape(1, GLA_QK), norm_g.reshape(1, GLA_DV), s0_t]
    aliases = {}
    if mix_in is None:
        mix_in = jnp.zeros((SUBLANES, LANES), BF16)
        in_specs.append(pl.BlockSpec(memory_space=pl.ANY))
    else:
        in_specs.append(pl.BlockSpec(memory_space=pl.ANY))
        aliases = {len(args): 0}
    args.append(mix_in)
    mix, s_fin = pl.pallas_call(
        functools.partial(_gla_kernel, rows=rows, chunk=chunk),
        grid=(nb, n_chunks),
        in_specs=in_specs,
        out_specs=[pl.BlockSpec((rows, GLA_VW), rmap(0)),
                   pl.BlockSpec((1, GLA_HEADS, GLA_DV, GLA_DK), lambda b, c: (b, 0, 0, 0))],
        out_shape=[jax.ShapeDtypeStruct((t, GLA_VW + SWA_QW), BF16),
                   jax.ShapeDtypeStruct((nb, GLA_HEADS, GLA_DV, GLA_DK), F32)],
        scratch_shapes=[pltpu.VMEM((GLA_HEADS, GLA_DV, GLA_DK), F32)],
        input_output_aliases=aliases,
        compiler_params=_params(2),
        name="gla",
    )(*args)
    return mix, s_fin


def _swa_kernel(sinks_ref, q_ref, kc_ref, vc_ref, kp_ref, vp_ref, mix_in_ref, o_ref, *,
                rows, q_base, q_stride):
    del mix_in_ref
    i = pl.program_id(1)
    q0 = q_base + i * q_stride
    nk = SWA_WINDOW + rows
    keys = jnp.concatenate([kp_ref[...], kc_ref[...]], axis=0).astype(BF16)
    vals = jnp.concatenate([vp_ref[...], vc_ref[...]], axis=0).astype(BF16)
    sr = SWA_GROUP * rows
    row = lax.broadcasted_iota(I32, (sr, nk), 0)
    qpos = q0 + (row & (rows - 1))
    kpos = q0 - SWA_WINDOW + lax.broadcasted_iota(I32, (sr, nk), 1)
    dist = jnp.abs(qpos - kpos).astype(F32)
    qc = lax.shift_right_arithmetic(qpos, CHUNK.bit_length() - 1)
    kc = lax.shift_right_arithmetic(kpos, CHUNK.bit_length() - 1)
    allowed = (kpos >= 0) & (kc <= qc) & (kc >= qc - WINDOW_CHUNKS)
    assert rows & (rows - 1) == 0
    head_in_group = lax.shift_right_logical(lax.broadcasted_iota(I32, (sr, 1), 0), rows.bit_length() - 1)
    for g in range(SWA_KV_HEADS):
        slope = jnp.zeros((sr, 1), F32)
        sink = jnp.zeros((sr, 1), F32)
        for j in range(SWA_GROUP):
            h = g * SWA_GROUP + j
            slope = jnp.where(head_in_group == j, 2.0 ** (-8.0 * (h + 1) / SWA_HEADS), slope)
            sink = jnp.where(head_in_group == j, sinks_ref[h], sink)
        kv = slice(g * SWA_HEAD_DIM, (g + 1) * SWA_HEAD_DIM)
        qg = jnp.concatenate(
            [q_ref[:, (g * SWA_GROUP + j) * SWA_HEAD_DIM:(g * SWA_GROUP + j + 1) * SWA_HEAD_DIM]
             for j in range(SWA_GROUP)], axis=0).astype(BF16)
        s = lax.dot_general(qg, keys[:, kv], NT_DIMS, preferred_element_type=F32)
        s = s * (SWA_HEAD_DIM ** -0.5) - slope * dist
        s = jnp.where(allowed, s, -jnp.inf)
        m = jnp.maximum(jnp.max(s, axis=-1, keepdims=True), sink)
        p = jnp.exp(s - m)
        denom = jnp.sum(p, axis=-1, keepdims=True) + jnp.exp(sink - m)
        o = jnp.dot(p.astype(BF16), vals[:, kv], preferred_element_type=F32) / denom
        for j in range(0, SWA_GROUP, 2):
            c0 = (g * SWA_GROUP + j) * SWA_HEAD_DIM
            o_ref[:, c0:c0 + LANES] = jnp.concatenate(
                [o[j * rows:(j + 1) * rows], o[(j + 1) * rows:(j + 2) * rows]], axis=-1).astype(o_ref.dtype)


def _swa(z, sinks, k_prev, v_prev, prev_map, mix_in, *, nb, n_blocks, rows, row0, q_base, q_stride):
    rb0 = row0 // rows
    rmap = lambda colblk: (lambda b, i: (rb0 + b * n_blocks + i, colblk))
    return pl.pallas_call(
        functools.partial(_swa_kernel, rows=rows, q_base=q_base, q_stride=q_stride),
        grid=(nb, n_blocks),
        in_specs=[pl.BlockSpec(memory_space=pltpu.SMEM),
                  pl.BlockSpec((rows, SWA_QW), rmap(COL_SQ // SWA_QW)),
                  pl.BlockSpec((rows, SWA_KVW), rmap(COL_SK // SWA_KVW)),
                  pl.BlockSpec((rows, SWA_KVW), rmap(COL_SV // SWA_KVW)),
                  pl.BlockSpec((SWA_WINDOW, SWA_KVW), prev_map[0]),
                  pl.BlockSpec((SWA_WINDOW, SWA_KVW), prev_map[1]),
                  pl.BlockSpec(memory_space=pl.ANY)],
        out_specs=pl.BlockSpec((rows, SWA_QW), rmap(GLA_VW // SWA_QW)),
        out_shape=jax.ShapeDtypeStruct(mix_in.shape, mix_in.dtype),
        input_output_aliases={6: 0},
        compiler_params=_params(2),
        name="swa",
    )(sinks, z, z, z, k_prev, v_prev, mix_in)


def _mem_attn_kernel(q_ref, k_ref, v_ref, o_in_ref, o_ref, *, head_dim):
    del o_in_ref
    for h in range(MEM_HEADS):
        hs = slice(h * head_dim, (h + 1) * head_dim)
        kh = k_ref[:, hs].astype(BF16)
        vh = v_ref[:, hs].astype(BF16)
        s = lax.dot_general(q_ref[:, hs], kh, NT_DIMS, preferred_element_type=F32) * (head_dim ** -0.5)
        m = jnp.max(s, axis=-1, keepdims=True)
        p = jnp.exp(s - m)
        denom = jnp.sum(p, axis=-1, keepdims=True)
        o = jnp.dot(p.astype(BF16), vh, preferred_element_type=F32) / denom
        o_ref[:, hs] = o.astype(o_ref.dtype)


def _mem_attn(q, mk, mv, o_in, *, nb, n_tiles, rows, row0, mem_tokens):
    t, d = q.shape
    rb0 = row0 // rows
    qmap = lambda b, i: (rb0 + b * n_tiles + i, 0)
    in_specs = [pl.BlockSpec((rows, d), qmap),
                pl.BlockSpec((mem_tokens, d), lambda b, i: (b, 0)),
                pl.BlockSpec((mem_tokens, d), lambda b, i: (b, 0)),
                pl.BlockSpec(memory_space=pl.ANY)]
    aliases = {}
    if o_in is None:
        o_in = jnp.zeros((SUBLANES, LANES), BF16)
    else:
        aliases = {3: 0}
    return pl.pallas_call(
        functools.partial(_mem_attn_kernel, head_dim=d // MEM_HEADS),
        grid=(nb, n_tiles),
        in_specs=in_specs,
        out_specs=pl.BlockSpec((rows, d), qmap),
        out_shape=jax.ShapeDtypeStruct((t, d), BF16),
        input_output_aliases=aliases,
        compiler_params=_params(2),
        name="mem_attn",
    )(q, mk, mv, o_in)


def _pick_first_max(cur, iota, sentinel):
    mx = jnp.max(cur, axis=0, keepdims=True)
    first = jnp.min(jnp.where(cur == mx, iota, sentinel), axis=0, keepdims=True)
    return iota == first, first


def _router_kernel(x_ref, wh_ref, wl_ref, bias_ref, idx_ref, gate_ref, rank_ref, cnt_ref, run_ref, *, tm):
    i = pl.program_id(0)

    @pl.when(i == 0)
    def _():
        run_ref[...] = jnp.zeros_like(run_ref)

    x = x_ref[...]
    xh = x.astype(BF16)
    xl = (x - xh.astype(F32)).astype(BF16)
    wh = wh_ref[...]
    logits = (lax.dot_general(wh, xh, NT_DIMS, preferred_element_type=F32)
              + lax.dot_general(wh, xl, NT_DIMS, preferred_element_type=F32)
              + lax.dot_general(wl_ref[...], xh, NT_DIMS, preferred_element_type=F32))
    scores = _sigmoid(logits)
    sel = scores + bias_ref[...]
    neg_inf = -jnp.inf

    li = lax.broadcasted_iota(I32, (GROUP_SIZE, tm), 0)
    grp_rows = []
    for g in range(N_GROUPS):
        blk = sel[g * GROUP_SIZE:(g + 1) * GROUP_SIZE, :]
        pick, _ = _pick_first_max(blk, li, GROUP_SIZE)
        m1 = jnp.max(blk, axis=0, keepdims=True)
        m2 = jnp.max(jnp.where(pick, neg_inf, blk), axis=0, keepdims=True)
        grp_rows.append(m1 + m2)
    grp = jnp.concatenate(grp_rows, axis=0)

    gi = lax.broadcasted_iota(I32, (N_GROUPS, tm), 0)
    gsel = jnp.zeros((N_GROUPS, tm), F32)
    cur = grp
    for _ in range(TOPK_GROUPS):
        pick, _ = _pick_first_max(cur, gi, N_GROUPS)
        gsel = jnp.where(pick, 1.0, gsel)
        cur = jnp.where(pick, neg_inf, cur)
    emask = jnp.concatenate(
        [jnp.broadcast_to(gsel[g:g + 1, :], (GROUP_SIZE, tm)) for g in range(N_GROUPS)], axis=0)

    ei = lax.broadcasted_iota(I32, (N_EXPERTS, tm), 0)
    cur = jnp.where(emask > 0.5, sel, neg_inf)
    chosen = jnp.zeros((N_EXPERTS, tm), F32)
    idx_rows, w_rows = [], []
    for _ in range(TOP_K):
        pick, first = _pick_first_max(cur, ei, N_EXPERTS)
        idx_rows.append(first)
        w_rows.append(jnp.sum(jnp.where(pick, scores, 0.0), axis=0, keepdims=True))
        chosen = jnp.where(pick, 1.0, chosen)
        cur = jnp.where(pick, neg_inf, cur)
    idx = jnp.concatenate(idx_rows, axis=0)
    w = jnp.concatenate(w_rows, axis=0)
    gate_ref[...] = w / jnp.sum(w, axis=0, keepdims=True) * ROUTED_SCALE
    idx_ref[...] = idx

    ti = lax.broadcasted_iota(I32, (tm, tm), 0)
    tj = lax.broadcasted_iota(I32, (tm, tm), 1)
    before = jnp.where(ti < tj, 1.0, 0.0).astype(BF16)
    local = jnp.dot(chosen.astype(BF16), before, preferred_element_type=F32)
    total = local + run_ref[:, 0:1]
    rank_rows = [jnp.sum(jnp.where(ei == idx_rows[k], total, 0.0), axis=0, keepdims=True)
                 for k in range(TOP_K)]
    rank_ref[...] = jnp.concatenate(rank_rows, axis=0).astype(I32)
    run_ref[...] = run_ref[...] + jnp.sum(chosen, axis=1, keepdims=True)
    cnt_ref[...] = run_ref[...].astype(I32)


def _router(x, w_hi_t, w_lo_t, bias, tm):
    t, d = x.shape
    return pl.pallas_call(
        functools.partial(_router_kernel, tm=tm),
        grid=(t // tm,),
        in_specs=[pl.BlockSpec((tm, d), lambda i: (i, 0)),
                  pl.BlockSpec((N_EXPERTS, d), lambda i: (0, 0)),
                  pl.BlockSpec((N_EXPERTS, d), lambda i: (0, 0)),
                  pl.BlockSpec((N_EXPERTS, 1), lambda i: (0, 0))],
        out_specs=[pl.BlockSpec((TOP_K, tm), lambda i: (0, i)),
                   pl.BlockSpec((TOP_K, tm), lambda i: (0, i)),
                   pl.BlockSpec((TOP_K, tm), lambda i: (0, i)),
                   pl.BlockSpec((N_EXPERTS, LANES), lambda i: (0, 0))],
        out_shape=[jax.ShapeDtypeStruct((TOP_K, t), I32),
                   jax.ShapeDtypeStruct((TOP_K, t), F32),
                   jax.ShapeDtypeStruct((TOP_K, t), I32),
                   jax.ShapeDtypeStruct((N_EXPERTS, LANES), I32)],
        scratch_shapes=[pltpu.VMEM((N_EXPERTS, LANES), F32)],
        compiler_params=_params(1),
        name="router",
    )(x, w_hi_t, w_lo_t, bias.reshape(N_EXPERTS, 1))


PAD_BITS = tuple(1 << s for s in reversed(range(EXPERT_ROWS.bit_length() - 1)))


def _dispatch_kernel(zstart_ref, zcount_ref, dest_ref, x_ref, swg_ref, swu_ref, swd_ref, xs_ref, shared_ref,
                     zero_ref, sem, zsem, *, tm):
    i = pl.program_id(0)

    def start_group(g, carry):
        for j in range(SUBLANES):
            for k in range(TOP_K):
                d = dest_ref[k * tm + g * SUBLANES + j]
                pltpu.make_async_copy(x_ref.at[g, pl.ds(j, 1), :], xs_ref.at[d], sem).start(priority=k % 2)
        return carry

    lax.fori_loop(0, tm // SUBLANES, start_group, 0)

    half = x_ref.shape[-1]
    x_lo, x_hi = _unpack_halves(x_ref[...].reshape(tm, half))
    x_lo = x_lo.astype(BF16)
    x_hi = x_hi.astype(BF16)
    hg = (jnp.dot(x_lo, swg_ref[:half, :], preferred_element_type=F32)
          + jnp.dot(x_hi, swg_ref[half:, :], preferred_element_type=F32))
    hu = (jnp.dot(x_lo, swu_ref[:half, :], preferred_element_type=F32)
          + jnp.dot(x_hi, swu_ref[half:, :], preferred_element_type=F32))
    shared_ref[...] = jnp.dot((_silu(hg) * hu).astype(BF16), swd_ref[...], preferred_element_type=F32)

    for k in range(TOP_K):
        pltpu.make_async_copy(xs_ref.at[pl.ds(0, tm)], xs_ref.at[pl.ds(0, tm)], sem).wait()

    @pl.when(i == pl.num_programs(0) - 1)
    def _():
        zero_ref[...] = jnp.zeros_like(zero_ref)

        def fill(e, wait):
            pos = zstart_ref[e]
            cnt = zcount_ref[e]
            for bit in PAD_BITS:
                has = (cnt & bit) != 0

                @pl.when(has)
                def _():
                    cp = pltpu.make_async_copy(zero_ref.at[pl.ds(0, bit)], xs_ref.at[pl.ds(pos, bit)], zsem)
                    if wait:
                        cp.wait()
                    else:
                        cp.start()

                pos = pos + jnp.where(has, bit, 0)

        def fill_start(e, carry):
            fill(e, False)
            return carry

        def fill_wait(e, carry):
            fill(e, True)
            return carry

        lax.fori_loop(0, N_EXPERTS, fill_start, 0)
        lax.fori_loop(0, N_EXPERTS, fill_wait, 0)


def _dispatch(x, dest_tiles, zstart, zcount, swg, swu, swd, n_rows, tm):
    t = x.shape[0] * SUBLANES
    d = x.shape[2]
    ds = swg.shape[1]
    const = lambda i, zs, zc: (0, 0)
    return pl.pallas_call(
        functools.partial(_dispatch_kernel, tm=tm),
        grid_spec=pltpu.PrefetchScalarGridSpec(
            num_scalar_prefetch=2,
            grid=(t // tm,),
            in_specs=[pl.BlockSpec((TOP_K * tm,), lambda i, zs, zc: (i,), memory_space=pltpu.SMEM),
                      pl.BlockSpec((tm // SUBLANES, SUBLANES, d), lambda i, zs, zc: (i, 0, 0)),
                      pl.BlockSpec((2 * d, ds), const),
                      pl.BlockSpec((2 * d, ds), const),
                      pl.BlockSpec((ds, 2 * d), const)],
            out_specs=[pl.BlockSpec(memory_space=pl.ANY),
                       pl.BlockSpec((tm, 2 * d), lambda i, zs, zc: (i, 0))],
            scratch_shapes=[pltpu.VMEM((EXPERT_ROWS // 2, 1, d), x.dtype),
                            pltpu.SemaphoreType.DMA(()),
                            pltpu.SemaphoreType.DMA(())]),
        out_shape=[jax.ShapeDtypeStruct((n_rows, 1, d), x.dtype),
                   jax.ShapeDtypeStruct((t, 2 * d), F32)],
        compiler_params=_params(1),
        name="moe_dispatch",
    )(zstart, zcount, dest_tiles, x, swg, swu, swd)


def _expert_kernel(be_ref, nxt_ref, nv_ref, xs_hbm, wg_hbm, wu_hbm, wd_hbm, ys_hbm,
                   wgs_ref, wus_ref, wds_ref, wgb_ref, wub_ref, wdb_ref, xbuf_ref, ybuf_ref,
                   wsems, xsems, ysems):
    b = pl.program_id(0)
    n_valid = nv_ref[0]
    e = be_ref[b]
    prev = be_ref[jnp.maximum(b - 1, 0)]
    fresh = jnp.logical_or(b == 0, e != prev)
    slot = lax.rem(b, 2)

    def weight_copies(expert):
        return (pltpu.make_async_copy(wg_hbm.at[expert], wgs_ref, wsems.at[0]),
                pltpu.make_async_copy(wu_hbm.at[expert], wus_ref, wsems.at[1]),
                pltpu.make_async_copy(wd_hbm.at[expert], wds_ref, wsems.at[2]))

    def rows_in(blk, to_slot):
        r0 = pl.multiple_of(blk * EXPERT_ROWS, EXPERT_ROWS)
        return pltpu.make_async_copy(xs_hbm.at[pl.ds(r0, EXPERT_ROWS), 0, :], xbuf_ref.at[to_slot],
                                     xsems.at[to_slot])

    def rows_out(blk, from_slot):
        r0 = pl.multiple_of(blk * EXPERT_ROWS, EXPERT_ROWS)
        return pltpu.make_async_copy(ybuf_ref.at[from_slot], ys_hbm.at[pl.ds(r0, EXPERT_ROWS), 0, :],
                                     ysems.at[from_slot])

    @pl.when(b == 0)
    def _():
        for cp in weight_copies(e):
            cp.start(priority=1)
        rows_in(b, slot).start()

    @pl.when(fresh)
    def _():
        for cp in weight_copies(e):
            cp.wait()
        wgb_ref[...] = wgs_ref[...].astype(BF16)
        wub_ref[...] = wus_ref[...].astype(BF16)
        wdb_ref[...] = wds_ref[...].astype(BF16)
        nxt = nxt_ref[b]

        @pl.when(nxt >= 0)
        def _():
            for cp in weight_copies(nxt):
                cp.start(priority=1)

    @pl.when(b < n_valid)
    def _():
        rows_in(b, slot).wait()

        @pl.when(b + 1 < n_valid)
        def _():
            rows_in(b + 1, 1 - slot).start()

        @pl.when(b >= 2)
        def _():
            rows_out(b - 2, slot).wait()

        half = xbuf_ref.shape[-1]
        x_lo, x_hi = _unpack_halves(xbuf_ref[slot])
        x_lo = x_lo.astype(BF16)
        x_hi = x_hi.astype(BF16)
        hg = (jnp.dot(x_lo, wgb_ref[:half, :], preferred_element_type=F32)
              + jnp.dot(x_hi, wgb_ref[half:, :], preferred_element_type=F32))
        hu = (jnp.dot(x_lo, wub_ref[:half, :], preferred_element_type=F32)
              + jnp.dot(x_hi, wub_ref[half:, :], preferred_element_type=F32))
        act = (_silu(hg) * hu).astype(BF16)
        ybuf_ref[slot] = _pack_halves(jnp.dot(act, wdb_ref[...], preferred_element_type=F32))
        rows_out(b, slot).start()

        @pl.when(b == n_valid - 1)
        def _():
            rows_out(b, slot).wait()

            @pl.when(b >= 1)
            def _():
                rows_out(b - 1, 1 - slot).wait()


def _experts(xs, wg, wu, wd, block_expert, next_expert, n_valid):
    n_rows = xs.shape[0]
    d = wg.shape[1]
    de = wg.shape[2]
    nb = n_rows // EXPERT_ROWS
    hbm = pl.BlockSpec(memory_space=pl.ANY)
    return pl.pallas_call(
        _expert_kernel,
        grid_spec=pltpu.PrefetchScalarGridSpec(
            num_scalar_prefetch=3,
            grid=(nb,),
            in_specs=[hbm, hbm, hbm, hbm],
            out_specs=hbm,
            scratch_shapes=[pltpu.VMEM((d, de), F32), pltpu.VMEM((d, de), F32), pltpu.VMEM((de, d), F32),
                            pltpu.VMEM((d, de), BF16), pltpu.VMEM((d, de), BF16), pltpu.VMEM((de, d), BF16),
                            pltpu.VMEM((2, EXPERT_ROWS, d // 2), I32), pltpu.VMEM((2, EXPERT_ROWS, d // 2), I32),
                            pltpu.SemaphoreType.DMA((3,)), pltpu.SemaphoreType.DMA((2,)),
                            pltpu.SemaphoreType.DMA((2,))]),
        out_shape=jax.ShapeDtypeStruct((n_rows, 1, d // 2), I32),
        compiler_params=_params(1),
        name="moe_experts",
    )(block_expert, next_expert, n_valid, xs, wg, wu, wd)


def _combine_kernel(dest0_ref, destn_ref, x_ref, shared_ref, gate_ref, ys_ref, g_ref, b_ref,
                    op_ref, os_ref, buf_ref, acc_ref, sems, *, tm, n_prompt_tiles):
    i = pl.program_id(0)
    n = pl.num_programs(0)
    half = buf_ref.shape[-1]
    n_groups = tm // COMBINE_GROUP

    def issue_group(dest_ref, to_slot, g):
        for j in range(COMBINE_GROUP):
            t = g * COMBINE_GROUP + j
            for k in range(TOP_K):
                d = dest_ref[k * tm + t]
                pltpu.make_async_copy(ys_ref.at[d], buf_ref.at[to_slot, k, g, pl.ds(j, 1), :],
                                      sems.at[to_slot]).start(priority=k % 2)

    @pl.when(i == 0)
    def _():
        def first(g, carry):
            issue_group(dest0_ref, 0, g)
            return carry
        lax.fori_loop(0, n_groups, first, 0)

    def run_tile(slot):
        for k in range(TOP_K):
            pltpu.make_async_copy(ys_ref.at[pl.ds(0, tm)], ys_ref.at[pl.ds(0, tm)], sems.at[slot]).wait()

        def reduce_group(g):
            r0 = pl.multiple_of(g * COMBINE_GROUP, COMBINE_GROUP)
            gate = gate_ref[pl.ds(r0, COMBINE_GROUP), :]
            acc_lo = acc_hi = None
            for k in range(TOP_K):
                y_lo, y_hi = _unpack_halves(buf_ref[slot, k, g])
                gk = gate[:, k:k + 1]
                acc_lo = gk * y_lo if k == 0 else acc_lo + gk * y_lo
                acc_hi = gk * y_hi if k == 0 else acc_hi + gk * y_hi
            acc_ref[pl.ds(r0, COMBINE_GROUP), :half] = acc_lo
            acc_ref[pl.ds(r0, COMBINE_GROUP), half:] = acc_hi

        @pl.when(i + 1 < n)
        def _():
            def body(g, carry):
                issue_group(destn_ref, 1 - slot, g)
                reduce_group(g)
                return carry
            lax.fori_loop(0, n_groups, body, 0)

        @pl.when(i + 1 >= n)
        def _():
            def body(g, carry):
                reduce_group(g)
                return carry
            lax.fori_loop(0, n_groups, body, 0)

    for s in range(2):
        pl.when(lax.rem(i, 2) == s)(functools.partial(run_tile, s))

    x = x_ref[...]
    out = _layer_norm(DEEPNORM_ALPHA * x + (acc_ref[...] + shared_ref[...]), g_ref[...], b_ref[...])

    @pl.when(i < n_prompt_tiles)
    def _():
        op_ref[...] = out

    @pl.when(i >= n_prompt_tiles)
    def _():
        os_ref[...] = out


def _combine(x, shared, gate_t, dest_tiles, ys, g, b, tm, n_prompt_rows):
    t, d = x.shape
    n_tiles = t // tm
    npt = n_prompt_rows // tm
    const = lambda i: (0, 0)
    return pl.pallas_call(
        functools.partial(_combine_kernel, tm=tm, n_prompt_tiles=npt),
        grid=(n_tiles,),
        in_specs=[pl.BlockSpec((TOP_K * tm,), lambda i: (0,), memory_space=pltpu.SMEM),
                  pl.BlockSpec((TOP_K * tm,), lambda i: (jnp.minimum(i + 1, n_tiles - 1),),
                               memory_space=pltpu.SMEM),
                  pl.BlockSpec((tm, d), lambda i: (i, 0)),
                  pl.BlockSpec((tm, d), lambda i: (i, 0)),
                  pl.BlockSpec((tm, TOP_K), lambda i: (i, 0)),
                  pl.BlockSpec(memory_space=pl.ANY),
                  pl.BlockSpec((1, d), const),
                  pl.BlockSpec((1, d), const)],
        out_specs=[pl.BlockSpec((tm, d), lambda i: (jnp.minimum(i, npt - 1), 0)),
                   pl.BlockSpec((tm, d), lambda i: (jnp.maximum(i - npt, 0), 0))],
        out_shape=[jax.ShapeDtypeStruct((n_prompt_rows, d), F32),
                   jax.ShapeDtypeStruct((t - n_prompt_rows, d), F32)],
        scratch_shapes=[pltpu.VMEM((2, TOP_K, tm // COMBINE_GROUP, COMBINE_GROUP, d // 2), I32),
                        pltpu.VMEM((tm, d), F32),
                        pltpu.SemaphoreType.DMA((2,))],
        compiler_params=_params(1),
        name="moe_combine",
    )(dest_tiles, dest_tiles, x, shared, gate_t, ys, g.reshape(1, d), b.reshape(1, d))


def _tile_major(a, tm):
    k, t = a.shape
    return a.reshape(k, t // tm, tm).transpose(1, 0, 2).reshape(-1)


def _moe_ln(x, x_packed, router_w, router_bias, wg, wu, wd, swg, swu, swd, g, b, n_prompt_rows):
    t, d = x.shape
    w_t = router_w.T
    w_hi = w_t.astype(BF16)
    w_lo = (w_t - w_hi.astype(F32)).astype(BF16)
    tr = _divisor_tile(t, 512, LANES)
    idx, gate, rank, cnt = _router(x, w_hi, w_lo, router_bias, tr)

    counts = cnt[:, 0]
    padded = (counts + EXPERT_ROWS - 1) // EXPERT_ROWS * EXPERT_ROWS
    pad_end = jnp.cumsum(padded)
    pad_start = pad_end - padded
    experts = jnp.arange(N_EXPERTS, dtype=I32)
    onehot = idx[None] == experts[:, None, None]
    dest = jnp.sum(jnp.where(onehot, pad_start[:, None, None], 0), axis=0) + rank
    n_blocks = (t * TOP_K + N_EXPERTS * (EXPERT_ROWS - 1)) // EXPERT_ROWS + 1
    n_valid = (pad_end[-1] // EXPERT_ROWS).astype(I32)
    blk = jnp.minimum(jnp.arange(n_blocks, dtype=I32), n_valid - 1)
    block_expert = jnp.minimum(
        jnp.sum(pad_end[None, :] <= (blk * EXPERT_ROWS)[:, None], axis=1), N_EXPERTS - 1).astype(I32)
    later_used = (experts[None, :] > experts[:, None]) & (counts[None, :] > 0)
    next_used = jnp.min(jnp.where(later_used, experts[None, :], N_EXPERTS), axis=1)
    next_used = jnp.where(next_used == N_EXPERTS, -1, next_used)
    next_expert = jnp.sum(
        jnp.where(block_expert[:, None] == experts[None, :], next_used[None, :], 0), axis=1).astype(I32)

    td = _divisor_tile(t, 256, SUBLANES)
    xs, shared = _dispatch(x_packed, _tile_major(dest, td), (pad_start + counts).astype(I32),
                           (padded - counts).astype(I32), swg.astype(BF16), swu.astype(BF16),
                           swd.astype(BF16), n_blocks * EXPERT_ROWS, td)
    ys = _experts(xs, wg, wu, wd, block_expert, next_expert, n_valid.reshape(1))
    tc = _divisor_tile(math.gcd(n_prompt_rows, t - n_prompt_rows), 128, SUBLANES)
    return _combine(x, shared, gate.T, _tile_major(dest, tc), ys, g, b, tc, n_prompt_rows)


def kernel(x_prompt, x_sample, mem_prompt, state_gla, cache_swa_k, cache_swa_v, cache_mem_k, cache_mem_v,
           w_in, gla_w_gate2, gla_b_gate, gla_norm_g, swa_sinks, w_mix_out, ln1_g, ln1_b,
           mem_wq, mem_wk, mem_wv, mem_wo, ln2_g, ln2_b,
           router_w, router_bias, exp_w_gate, exp_w_up, exp_w_down, sh_w_gate, sh_w_up, sh_w_down,
           ln3_g, ln3_b):
    nbp, seq, d = x_prompt.shape
    nbs, dec, _ = x_sample.shape
    tp, ts = nbp * seq, nbs * dec
    t = tp + ts
    mem_tokens = mem_prompt.shape[1]
    assert state_gla.shape[0] == DEPTH and seq % SWA_WINDOW == 0 and dec % SUBLANES == 0
    assert cache_swa_k.shape[2] == SWA_WINDOW and tp % dec == 0
    tm = _divisor_tile(math.gcd(tp, ts), 512, 16)
    l = 0

    xp = x_prompt.reshape(tp, d)
    xs = x_sample.reshape(ts, d)

    wi_t = w_in[l].T
    o = 0
    seg = []
    for size in (GLA_QK, GLA_QK, GLA_VW, GLA_VW, GLA_GATE_RANK, SWA_QW, SWA_KVW, SWA_KVW):
        seg.append(wi_t[o:o + size])
        o += size
    gq, gk, gv, gr, gg, sq, sk, sv = seg
    w_in_rt = jnp.concatenate(
        [gq, gk, gv, gr, sq, sk, sv, gg,
         jnp.zeros((IN_COLS_PADDED - COL_GG - GLA_GATE_RANK, d), wi_t.dtype)], axis=0).astype(BF16)
    z = _matmul_split(xp, xs, w_in_rt, tm, IN_COLS_PADDED // 2, F32)

    wa2_pad = jnp.concatenate(
        [gla_w_gate2[l], jnp.zeros((LANES - GLA_GATE_RANK, GLA_QK), F32)], axis=0)
    wa2_hi = wa2_pad.astype(BF16)
    wa2_lo = (wa2_pad - wa2_hi.astype(F32)).astype(BF16)
    wa2_stack = jnp.concatenate([wa2_hi, wa2_lo, wa2_hi], axis=0)
    zero_state = jnp.zeros((nbp, GLA_HEADS, GLA_DV, GLA_DK), F32)
    gla_rows = _divisor_tile(seq, GLA_STEP_ROWS, CHUNK)
    mix, st_p = _gla(z, wa2_stack, gla_b_gate[l], gla_norm_g[l], zero_state, None,
                     nb=nbp, n_chunks=seq // gla_rows, rows=gla_rows, chunk=CHUNK, row0=0)
    mix, st_s = _gla(z, wa2_stack, gla_b_gate[l], gla_norm_g[l], jnp.swapaxes(state_gla[l], -1, -2), mix,
                     nb=nbs, n_chunks=1, rows=dec, chunk=dec, row0=tp)
    nblk = seq // SWA_WINDOW
    prev_p = lambda col: (lambda b, i: (b * nblk + jnp.maximum(i - 1, 0), col))
    mix = _swa(z, swa_sinks[l], z, z, (prev_p(COL_SK // SWA_KVW), prev_p(COL_SV // SWA_KVW)), mix,
               nb=nbp, n_blocks=nblk, rows=SWA_WINDOW, row0=0, q_base=0, q_stride=SWA_WINDOW)
    ck = cache_swa_k[l].reshape(nbs * SWA_WINDOW, SWA_KVW)
    cv = cache_swa_v[l].reshape(nbs * SWA_WINDOW, SWA_KVW)
    prev_s = lambda b, i: (b, 0)
    mix = _swa(z, swa_sinks[l], ck, cv, (prev_s, prev_s), mix,
               nb=nbs, n_blocks=1, rows=dec, row0=tp, q_base=PAST_LEN, q_stride=0)
    h1 = _matmul_res_ln(mix, w_mix_out[l].astype(BF16), (xp, xs), ln1_g[l], ln1_b[l], tm)

    mem = mem_prompt.reshape(nbp * mem_tokens, d)
    tmem = _divisor_tile(nbp * mem_tokens, 512, 16)
    mk = _matmul(mem, mem_wk[l].astype(BF16), tmem, _divisor_tile(d, 1024, LANES), F32)
    mv = _matmul(mem, mem_wv[l].astype(BF16), tmem, _divisor_tile(d, 1024, LANES), F32)
    q = _matmul(h1, mem_wq[l].astype(BF16), tm, d, BF16)
    tq = _divisor_tile(seq, 512, 16)
    att = _mem_attn(q, mk, mv, None, nb=nbp, n_tiles=seq // tq, rows=tq, row0=0, mem_tokens=mem_tokens)
    att = _mem_attn(q, cache_mem_k[l].reshape(nbs * mem_tokens, d), cache_mem_v[l].reshape(nbs * mem_tokens, d),
                    att, nb=nbs, n_tiles=1, rows=dec, row0=tp, mem_tokens=mem_tokens)
    h2, h2_packed = _matmul_res_ln(att, mem_wo[l].astype(BF16), h1, ln2_g[l], ln2_b[l], tm, emit_packed=True)

    out_p, out_s = _moe_ln(h2, h2_packed, router_w[l], router_bias[l], exp_w_gate[l], exp_w_up[l],
                           exp_w_down[l], sh_w_gate[l], sh_w_up[l], sh_w_down[l], ln3_g[l], ln3_b[l], tp)

    y_prompt = out_p.reshape(nbp, seq, d)
    y_sample = out_s.reshape(nbs, dec, d)
    kv = z[:, COL_SK:COL_SK + 2 * SWA_KVW]
    kvp = kv[:tp].reshape(nbp, seq, 2 * SWA_KVW)[:, seq - SWA_WINDOW:]
    kvs = kv[tp:].reshape(nbs, dec, 2 * SWA_KVW)
    kv_shape = lambda a: a.reshape(a.shape[0], a.shape[1], SWA_KV_HEADS, SWA_HEAD_DIM)[None]
    mem_shape = lambda a: a.reshape(nbp, mem_tokens, MEM_HEADS, d // MEM_HEADS)[None]
    return (y_prompt, y_sample,
            jnp.swapaxes(st_p, -1, -2)[None],
            kv_shape(kvp[..., :SWA_KVW]), kv_shape(kvp[..., SWA_KVW:]),
            mem_shape(mk), mem_shape(mv),
            jnp.swapaxes(st_s, -1, -2)[None],
            kv_shape(kvs[..., :SWA_KVW]), kv_shape(kvs[..., SWA_KVW:]))
```

```python
import functools
import math

import jax
import jax.numpy as jnp
from jax import lax
from jax.experimental import pallas as pl
from jax.experimental.pallas import tpu as pltpu

F32 = jnp.float32
BF16 = jnp.bfloat16
I32 = jnp.int32

CHUNK = 64
PAST_LEN = 2048
GLA_HEADS = 8
GLA_DK = 64
GLA_DV = 128
GLA_QK = GLA_HEADS * GLA_DK
GLA_VW = GLA_HEADS * GLA_DV
GLA_GATE_RANK = 16
GLA_TAU = 16.0
SWA_HEADS = 16
SWA_KV_HEADS = 2
SWA_GROUP = SWA_HEADS // SWA_KV_HEADS
SWA_HEAD_DIM = 64
SWA_QW = SWA_HEADS * SWA_HEAD_DIM
SWA_KVW = SWA_KV_HEADS * SWA_HEAD_DIM
SWA_WINDOW = 128
WINDOW_CHUNKS = SWA_WINDOW // CHUNK
MEM_HEADS = 4
N_EXPERTS = 64
TOP_K = 8
N_GROUPS = 8
GROUP_SIZE = N_EXPERTS // N_GROUPS
TOPK_GROUPS = 4
ROUTED_SCALE = 2.5
DEPTH = 1
DEEPNORM_ALPHA = (2 * DEPTH) ** 0.25
LN_EPS = 1e-5
RMS_EPS = 1e-6

LANES = 128
SUBLANES = 8
VMEM_LIMIT_BYTES = 56 * 1024 * 1024

COL_GQ = 0
COL_GK = COL_GQ + GLA_QK
COL_GV = COL_GK + GLA_QK
COL_GR = COL_GV + GLA_VW
COL_SQ = COL_GR + GLA_VW
COL_SK = COL_SQ + SWA_QW
COL_SV = COL_SK + SWA_KVW
COL_GG = COL_SV + SWA_KVW
MXU_COLS = 256
IN_COLS_PADDED = -(-(COL_GG + LANES) // (2 * MXU_COLS)) * (2 * MXU_COLS)

EXPERT_ROWS = 256
COMBINE_GROUP = SUBLANES
LN_SUB_ROWS = 128
GLA_STEP_ROWS = 256

NT_DIMS = (((1,), (1,)), ((), ()))
TN_DIMS = (((0,), (0,)), ((), ()))


def _params(n_axes):
    return pltpu.CompilerParams(dimension_semantics=("arbitrary",) * n_axes,
                                vmem_limit_bytes=VMEM_LIMIT_BYTES)


def _sigmoid(x):
    return 1.0 / (1.0 + jnp.exp(-x))


def _silu(x):
    return x * _sigmoid(x)


def _layer_norm(x, g, b):
    mu = jnp.mean(x, axis=-1, keepdims=True)
    xc = x - mu
    var = jnp.mean(xc * xc, axis=-1, keepdims=True)
    return xc * lax.rsqrt(var + LN_EPS) * g + b


HI_MASK = -65536


def _round_to_bf16_bits(x):
    b = lax.bitcast_convert_type(x, I32)
    return b + (0x7FFF + (lax.shift_right_logical(b, 16) & 1))


def _pack_halves(x):
    c = x.shape[1] // 2
    lo = lax.shift_right_logical(_round_to_bf16_bits(x[:, :c]), 16)
    hi = _round_to_bf16_bits(x[:, c:]) & HI_MASK
    return lo | hi


def _unpack_halves(w):
    return (lax.bitcast_convert_type(lax.shift_left(w, 16), F32),
            lax.bitcast_convert_type(w & HI_MASK, F32))


def _divisor_tile(n, pref, mult):
    t = min(pref, n)
    while t > mult and (n % t or t % mult):
        t -= mult
    assert n % t == 0 and t % mult == 0, (n, pref, mult)
    return t


def _mm_kernel(x_ref, w_ref, o_ref, xb_ref):
    @pl.when(pl.program_id(1) == 0)
    def _():
        xb_ref[...] = x_ref[...].astype(BF16)

    o_ref[...] = jnp.dot(xb_ref[...], w_ref[...], preferred_element_type=F32).astype(o_ref.dtype)


def _matmul(x, w, tm, tn, out_dtype):
    m, k = x.shape
    n = w.shape[1]
    return pl.pallas_call(
        _mm_kernel,
        grid=(m // tm, n // tn),
        in_specs=[pl.BlockSpec((tm, k), lambda i, j: (i, 0)),
                  pl.BlockSpec((k, tn), lambda i, j: (0, j))],
        out_specs=pl.BlockSpec((tm, tn), lambda i, j: (i, j)),
        out_shape=jax.ShapeDtypeStruct((m, n), out_dtype),
        scratch_shapes=[pltpu.VMEM((tm, k), BF16)],
        compiler_params=_params(2),
        name="matmul",
    )(x, w)


def _split_rows_specs(tm, k, n_first_tiles, n_grid_axes):
    if n_grid_axes == 1:
        return [pl.BlockSpec((tm, k), lambda i: (jnp.minimum(i, n_first_tiles - 1), 0)),
                pl.BlockSpec((tm, k), lambda i: (jnp.maximum(i - n_first_tiles, 0), 0))]
    return [pl.BlockSpec((tm, k), lambda j, i: (jnp.minimum(i, n_first_tiles - 1), 0)),
            pl.BlockSpec((tm, k), lambda j, i: (jnp.maximum(i - n_first_tiles, 0), 0))]


def _mm_split_kernel(xa_ref, xb_ref, wt_ref, o_ref, *, n_first_tiles):
    i = pl.program_id(1)

    def product(x_ref):
        o_ref[...] = lax.dot_general(x_ref[...].astype(BF16), wt_ref[...], NT_DIMS,
                                     preferred_element_type=F32).astype(o_ref.dtype)

    @pl.when(i < n_first_tiles)
    def _():
        product(xa_ref)

    @pl.when(i >= n_first_tiles)
    def _():
        product(xb_ref)


def _matmul_split(xa, xb, w_t, tm, tn, out_dtype):
    ma, k = xa.shape
    m = ma + xb.shape[0]
    n = w_t.shape[0]
    return pl.pallas_call(
        functools.partial(_mm_split_kernel, n_first_tiles=ma // tm),
        grid=(n // tn, m // tm),
        in_specs=_split_rows_specs(tm, k, ma // tm, 2) + [pl.BlockSpec((tn, k), lambda j, i: (j, 0))],
        out_specs=pl.BlockSpec((tm, tn), lambda j, i: (i, j)),
        out_shape=jax.ShapeDtypeStruct((m, n), out_dtype),
        compiler_params=_params(2),
        name="matmul_split",
    )(xa, xb, w_t)


def _reorder_cast_kernel(src_hbm, dst_hbm, stage_ref, out_ref, in_sems, out_sems, *, chunk_pieces, chunk):
    n = len(chunk_pieces)

    def in_copies(c):
        return [pltpu.make_async_copy(src_hbm.at[pl.ds(s0, ln)], stage_ref.at[c % 2, pl.ds(o0, ln)],
                                      in_sems.at[c % 2]) for s0, o0, ln in chunk_pieces[c]]

    def out_copy(c):
        return pltpu.make_async_copy(out_ref.at[c % 2], dst_hbm.at[pl.ds(c * chunk, chunk)], out_sems.at[c % 2])

    for cp in in_copies(0):
        cp.start()
    for c in range(n):
        if c + 1 < n:
            for cp in in_copies(c + 1):
                cp.start()
        for cp in in_copies(c):
            cp.wait()
        if c >= 2:
            out_copy(c - 2).wait()
        covered = sum(ln for _, _, ln in chunk_pieces[c])
        if covered < chunk:
            stage_ref[c % 2, covered:, :] = jnp.zeros((chunk - covered, stage_ref.shape[-1]), stage_ref.dtype)
        out_ref[c % 2] = stage_ref[c % 2].astype(out_ref.dtype)
        out_copy(c).start()
    for c in range(max(n - 2, 0), n):
        out_copy(c).wait()


def _reorder_cast(src, runs, n_out_rows, chunk):
    k = src.shape[1]
    chunk_pieces = []
    for c0 in range(0, n_out_rows, chunk):
        pieces = []
        for s0, d0, ln in runs:
            lo, hi = max(d0, c0), min(d0 + ln, c0 + chunk)
            if lo < hi:
                assert (lo - c0) % 16 == 0 and (s0 + lo - d0) % SUBLANES == 0 and (hi - lo) % 16 == 0
                pieces.append((s0 + lo - d0, lo - c0, hi - lo))
        chunk_pieces.append(pieces)
    hbm = pl.BlockSpec(memory_space=pl.ANY)
    return pl.pallas_call(
        functools.partial(_reorder_cast_kernel, chunk_pieces=chunk_pieces, chunk=chunk),
        in_specs=[hbm],
        out_specs=hbm,
        out_shape=jax.ShapeDtypeStruct((n_out_rows, k), BF16),
        scratch_shapes=[pltpu.VMEM((2, chunk, k), src.dtype), pltpu.VMEM((2, chunk, k), BF16),
                        pltpu.SemaphoreType.DMA((2,)), pltpu.SemaphoreType.DMA((2,))],
        compiler_params=pltpu.CompilerParams(vmem_limit_bytes=VMEM_LIMIT_BYTES),
        name="reorder_cast",
    )(src)


def _mm_res_ln_kernel(x_ref, w_ref, *refs, n_res, n_first_tiles):
    res_refs, (g_ref, b_ref, o_ref, *packed_ref) = refs[:n_res], refs[n_res:]
    tm = x_ref.shape[0]
    sub = min(tm, LN_SUB_ROWS)
    use_first = pl.program_id(0) < n_first_tiles
    for r in range(0, tm, sub):
        rows = slice(r, r + sub)
        y = jnp.dot(x_ref[rows, :], w_ref[...], preferred_element_type=F32)
        res = res_refs[0][rows, :]
        if n_res == 2:
            res = jnp.where(use_first, res, res_refs[1][rows, :])
        h = _layer_norm(DEEPNORM_ALPHA * res + y, g_ref[...], b_ref[...])
        o_ref[rows, :] = h
        if packed_ref:
            groups = slice(r // SUBLANES, (r + sub) // SUBLANES)
            packed_ref[0][groups] = _pack_halves(h).reshape(sub // SUBLANES, SUBLANES, h.shape[1] // 2)


def _matmul_res_ln(x, w, res, g, b, tm, emit_packed=False):
    m, k = x.shape
    n = w.shape[1]
    if isinstance(res, tuple):
        n_first_tiles = res[0].shape[0] // tm
        res_specs = _split_rows_specs(tm, n, n_first_tiles, 1)
    else:
        n_first_tiles = 0
        res = (res,)
        res_specs = [pl.BlockSpec((tm, n), lambda i: (i, 0))]
    out_specs = [pl.BlockSpec((tm, n), lambda i: (i, 0))]
    out_shape = [jax.ShapeDtypeStruct((m, n), F32)]
    if emit_packed:
        out_specs.append(pl.BlockSpec((tm // SUBLANES, SUBLANES, n // 2), lambda i: (i, 0, 0)))
        out_shape.append(jax.ShapeDtypeStruct((m // SUBLANES, SUBLANES, n // 2), I32))
    outs = pl.pallas_call(
        functools.partial(_mm_res_ln_kernel, n_res=len(res), n_first_tiles=n_first_tiles),
        grid=(m // tm,),
        in_specs=[pl.BlockSpec((tm, k), lambda i: (i, 0)),
                  pl.BlockSpec((k, n), lambda i: (0, 0))] + res_specs + [
                  pl.BlockSpec((1, n), lambda i: (0, 0)),
                  pl.BlockSpec((1, n), lambda i: (0, 0))],
        out_specs=out_specs,
        out_shape=out_shape,
        compiler_params=_params(1),
        name="matmul_res_ln",
    )(x, w, *res, g.reshape(1, n), b.reshape(1, n))
    return outs if emit_packed else outs[0]


def _split3_bf16(x):
    hi = x.astype(BF16)
    r1 = x - hi.astype(F32)
    mid = r1.astype(BF16)
    lo = (r1 - mid.astype(F32)).astype(BF16)
    return jnp.concatenate([hi, mid, lo], axis=-1)


def _gla_kernel(q_ref, k_ref, v_ref, r_ref, gg_ref, wa2_ref, ba_ref, ng_ref, s0_ref, mix_in_ref,
                o_ref, sfin_ref, st_ref, *, rows, chunk):
    del mix_in_ref
    n_sub = rows // chunk
    shift = chunk.bit_length() - 1
    assert chunk == 1 << shift

    @pl.when(pl.program_id(1) == 0)
    def _():
        st_ref[...] = s0_ref[0]

    gg = gg_ref[...]
    gg_hi = gg.astype(BF16)
    gg_lo = (gg - gg_hi.astype(F32)).astype(BF16)
    gate = jnp.dot(jnp.concatenate([gg_hi, gg_hi, gg_lo], axis=-1), wa2_ref[...],
                   preferred_element_type=F32) + ba_ref[...]
    log_a = (jnp.minimum(gate, 0.0) - jnp.log1p(jnp.exp(-jnp.abs(gate)))) / GLA_TAU

    ri = lax.broadcasted_iota(I32, (rows, rows), 0)
    ci = lax.broadcasted_iota(I32, (rows, rows), 1)
    same_chunk = lax.shift_right_logical(ri, shift) == lax.shift_right_logical(ci, shift)
    causal = same_chunk & (ri >= ci)
    ones = jnp.concatenate([jnp.where(causal, 1.0, 0.0), jnp.where(same_chunk, 1.0, 0.0)], axis=0).astype(BF16)
    sums = jnp.dot(ones, _split3_bf16(log_a), preferred_element_type=F32)
    sums = sums[:, :GLA_QK] + sums[:, GLA_QK:2 * GLA_QK] + sums[:, 2 * GLA_QK:]
    bcum = sums[:rows]
    b_last = sums[rows:]
    q = q_ref[...] * (GLA_DK ** -0.5)
    k = k_ref[...]
    q_in = (q * jnp.exp(bcum)).astype(BF16)
    k_in = (k * jnp.exp(-bcum)).astype(BF16)
    k_out = (k * jnp.exp(b_last - bcum)).astype(BF16)
    decay = jnp.exp(b_last)
    ng = ng_ref[...]
    blk_r = lax.shift_right_logical(lax.broadcasted_iota(I32, (rows, n_sub * GLA_DK), 0), shift)
    blk_c = lax.shift_right_logical(lax.broadcasted_iota(I32, (rows, n_sub * GLA_DK), 1),
                                    GLA_DK.bit_length() - 1)
    for h in range(GLA_HEADS):
        ks = slice(h * GLA_DK, (h + 1) * GLA_DK)
        vs = slice(h * GLA_DV, (h + 1) * GLA_DV)
        vh = v_ref[:, vs].astype(BF16)
        qh = q_in[:, ks]
        attn = lax.dot_general(qh, k_in[:, ks], NT_DIMS, preferred_element_type=F32)
        attn = jnp.where(causal, attn, 0.0).astype(BF16)
        o = jnp.dot(attn, vh, preferred_element_type=F32)
        ko = k_out[:, ks]
        if n_sub > 1:
            ko = jnp.where(blk_r == blk_c, jnp.concatenate([ko] * n_sub, axis=-1), 0.0)
        upd_t = lax.dot_general(vh, ko, TN_DIMS, preferred_element_type=F32)
        st = st_ref[h]
        states = []
        for c in range(n_sub):
            states.append(st.astype(BF16))
            st = st * decay[c * chunk:c * chunk + 1, ks] + upd_t[:, c * GLA_DK:(c + 1) * GLA_DK]
        st_ref[h] = st
        o_all = lax.dot_general(qh, jnp.concatenate(states, axis=0), NT_DIMS, preferred_element_type=F32)
        o = o + jnp.concatenate(
            [o_all[c * chunk:(c + 1) * chunk, c * GLA_DV:(c + 1) * GLA_DV] for c in range(n_sub)], axis=0)
        o = o * lax.rsqrt(jnp.mean(o * o, axis=-1, keepdims=True) + RMS_EPS) * ng
        o_ref[:, vs] = (o * _silu(r_ref[:, vs])).astype(o_ref.dtype)
    sfin_ref[0] = st_ref[...]


def _gla(z, wa2_stack, ba, norm_g, s0_t, mix_in, *, nb, n_chunks, rows, chunk, row0):
    t = z.shape[0]
    rb0 = row0 // rows
    rmap = lambda colblk: (lambda b, c: (rb0 + b * n_chunks + c, colblk))
    in_specs = [
        pl.BlockSpec((rows, GLA_QK), rmap(COL_GQ // GLA_QK)),
        pl.BlockSpec((rows, GLA_QK), rmap(COL_GK // GLA_QK)),
        pl.BlockSpec((rows, GLA_VW), rmap(COL_GV // GLA_VW)),
        pl.BlockSpec((rows, GLA_VW), rmap(COL_GR // GLA_VW)),
        pl.BlockSpec((rows, LANES), rmap(COL_GG // LANES)),
        pl.BlockSpec((3 * LANES, GLA_QK), lambda b, c: (0, 0)),
        pl.BlockSpec((1, GLA_QK), lambda b, c: (0, 0)),
        pl.BlockSpec((1, GLA_DV), lambda b, c: (0, 0)),
        pl.BlockSpec((1, GLA_HEADS, GLA_DV, GLA_DK), lambda b, c: (b, 0, 0, 0)),
    ]
    args = [z, z, z, z, z, wa2_stack, ba.reshape(1, GLA_QK), norm_g.reshape(1, GLA_DV), s0_t]
    aliases = {}
    if mix_in is None:
        mix_in = jnp.zeros((SUBLANES, LANES), BF16)
        in_specs.append(pl.BlockSpec(memory_space=pl.ANY))
    else:
        in_specs.append(pl.BlockSpec(memory_space=pl.ANY))
        aliases = {len(args): 0}
    args.append(mix_in)
    mix, s_fin = pl.pallas_call(
        functools.partial(_gla_kernel, rows=rows, chunk=chunk),
        grid=(nb, n_chunks),
        in_specs=in_specs,
        out_specs=[pl.BlockSpec((rows, GLA_VW), rmap(0)),
                   pl.BlockSpec((1, GLA_HEADS, GLA_DV, GLA_DK), lambda b, c: (b, 0, 0, 0))],
        out_shape=[jax.ShapeDtypeStruct((t, GLA_VW + SWA_QW), BF16),
                   jax.ShapeDtypeStruct((nb, GLA_HEADS, GLA_DV, GLA_DK), F32)],
        scratch_shapes=[pltpu.VMEM((GLA_HEADS, GLA_DV, GLA_DK), F32)],
        input_output_aliases=aliases,
        compiler_params=_params(2),
        name="gla",
    )(*args)
    return mix, s_fin


def _swa_kernel(sinks_ref, q_ref, kc_ref, vc_ref, kp_ref, vp_ref, mix_in_ref, o_ref, *,
                rows, q_base, q_stride):
    del mix_in_ref
    i = pl.program_id(1)
    q0 = q_base + i * q_stride
    nk = SWA_WINDOW + rows
    keys = jnp.concatenate([kp_ref[...], kc_ref[...]], axis=0).astype(BF16)
    vals = jnp.concatenate([vp_ref[...], vc_ref[...]], axis=0).astype(BF16)
    sr = SWA_GROUP * rows
    row = lax.broadcasted_iota(I32, (sr, nk), 0)
    qpos = q0 + (row & (rows - 1))
    kpos = q0 - SWA_WINDOW + lax.broadcasted_iota(I32, (sr, nk), 1)
    dist = jnp.abs(qpos - kpos).astype(F32)
    qc = lax.shift_right_arithmetic(qpos, CHUNK.bit_length() - 1)
    kc = lax.shift_right_arithmetic(kpos, CHUNK.bit_length() - 1)
    allowed = (kpos >= 0) & (kc <= qc) & (kc >= qc - WINDOW_CHUNKS)
    assert rows & (rows - 1) == 0
    head_in_group = lax.shift_right_logical(lax.broadcasted_iota(I32, (sr, 1), 0), rows.bit_length() - 1)
    for g in range(SWA_KV_HEADS):
        slope = jnp.zeros((sr, 1), F32)
        sink = jnp.zeros((sr, 1), F32)
        for j in range(SWA_GROUP):
            h = g * SWA_GROUP + j
            slope = jnp.where(head_in_group == j, 2.0 ** (-8.0 * (h + 1) / SWA_HEADS), slope)
            sink = jnp.where(head_in_group == j, sinks_ref[h], sink)
        kv = slice(g * SWA_HEAD_DIM, (g + 1) * SWA_HEAD_DIM)
        qg = jnp.concatenate(
            [q_ref[:, (g * SWA_GROUP + j) * SWA_HEAD_DIM:(g * SWA_GROUP + j + 1) * SWA_HEAD_DIM]
             for j in range(SWA_GROUP)], axis=0).astype(BF16)
        s = lax.dot_general(qg, keys[:, kv], NT_DIMS, preferred_element_type=F32)
        s = s * (SWA_HEAD_DIM ** -0.5) - slope * dist
        s = jnp.where(allowed, s, -jnp.inf)
        m = jnp.maximum(jnp.max(s, axis=-1, keepdims=True), sink)
        p = jnp.exp(s - m)
        denom = jnp.sum(p, axis=-1, keepdims=True) + jnp.exp(sink - m)
        o = jnp.dot(p.astype(BF16), vals[:, kv], preferred_element_type=F32) / denom
        for j in range(0, SWA_GROUP, 2):
            c0 = (g * SWA_GROUP + j) * SWA_HEAD_DIM
            o_ref[:, c0:c0 + LANES] = jnp.concatenate(
                [o[j * rows:(j + 1) * rows], o[(j + 1) * rows:(j + 2) * rows]], axis=-1).astype(o_ref.dtype)


def _swa(z, sinks, k_prev, v_prev, prev_map, mix_in, *, nb, n_blocks, rows, row0, q_base, q_stride):
    rb0 = row0 // rows
    rmap = lambda colblk: (lambda b, i: (rb0 + b * n_blocks + i, colblk))
    return pl.pallas_call(
        functools.partial(_swa_kernel, rows=rows, q_base=q_base, q_stride=q_stride),
        grid=(nb, n_blocks),
        in_specs=[pl.BlockSpec(memory_space=pltpu.SMEM),
                  pl.BlockSpec((rows, SWA_QW), rmap(COL_SQ // SWA_QW)),
                  pl.BlockSpec((rows, SWA_KVW), rmap(COL_SK // SWA_KVW)),
                  pl.BlockSpec((rows, SWA_KVW), rmap(COL_SV // SWA_KVW)),
                  pl.BlockSpec((SWA_WINDOW, SWA_KVW), prev_map[0]),
                  pl.BlockSpec((SWA_WINDOW, SWA_KVW), prev_map[1]),
                  pl.BlockSpec(memory_space=pl.ANY)],
        out_specs=pl.BlockSpec((rows, SWA_QW), rmap(GLA_VW // SWA_QW)),
        out_shape=jax.ShapeDtypeStruct(mix_in.shape, mix_in.dtype),
        input_output_aliases={6: 0},
        compiler_params=_params(2),
        name="swa",
    )(sinks, z, z, z, k_prev, v_prev, mix_in)


def _mem_attn_kernel(q_ref, k_ref, v_ref, o_in_ref, o_ref, *, head_dim):
    del o_in_ref
    for h in range(MEM_HEADS):
        hs = slice(h * head_dim, (h + 1) * head_dim)
        kh = k_ref[:, hs].astype(BF16)
        vh = v_ref[:, hs].astype(BF16)
        s = lax.dot_general(q_ref[:, hs], kh, NT_DIMS, preferred_element_type=F32) * (head_dim ** -0.5)
        m = jnp.max(s, axis=-1, keepdims=True)
        p = jnp.exp(s - m)
        denom = jnp.sum(p, axis=-1, keepdims=True)
        o = jnp.dot(p.astype(BF16), vh, preferred_element_type=F32) / denom
        o_ref[:, hs] = o.astype(o_ref.dtype)


def _mem_attn(q, mk, mv, o_in, *, nb, n_tiles, rows, row0, mem_tokens):
    t, d = q.shape
    rb0 = row0 // rows
    qmap = lambda b, i: (rb0 + b * n_tiles + i, 0)
    in_specs = [pl.BlockSpec((rows, d), qmap),
                pl.BlockSpec((mem_tokens, d), lambda b, i: (b, 0)),
                pl.BlockSpec((mem_tokens, d), lambda b, i: (b, 0)),
                pl.BlockSpec(memory_space=pl.ANY)]
    aliases = {}
    if o_in is None:
        o_in = jnp.zeros((SUBLANES, LANES), BF16)
    else:
        aliases = {3: 0}
    return pl.pallas_call(
        functools.partial(_mem_attn_kernel, head_dim=d // MEM_HEADS),
        grid=(nb, n_tiles),
        in_specs=in_specs,
        out_specs=pl.BlockSpec((rows, d), qmap),
        out_shape=jax.ShapeDtypeStruct((t, d), BF16),
        input_output_aliases=aliases,
        compiler_params=_params(2),
        name="mem_attn",
    )(q, mk, mv, o_in)


def _pick_first_max(cur, iota, sentinel):
    mx = jnp.max(cur, axis=0, keepdims=True)
    first = jnp.min(jnp.where(cur == mx, iota, sentinel), axis=0, keepdims=True)
    return iota == first, first


def _router_kernel(x_ref, wh_ref, wl_ref, bias_ref, idx_ref, gate_ref, rank_ref, cnt_ref, run_ref, *, tm):
    i = pl.program_id(0)

    @pl.when(i == 0)
    def _():
        run_ref[...] = jnp.zeros_like(run_ref)

    x = x_ref[...]
    xh = x.astype(BF16)
    xl = (x - xh.astype(F32)).astype(BF16)
    wh = wh_ref[...]
    logits = (lax.dot_general(wh, xh, NT_DIMS, preferred_element_type=F32)
              + lax.dot_general(wh, xl, NT_DIMS, preferred_element_type=F32)
              + lax.dot_general(wl_ref[...], xh, NT_DIMS, preferred_element_type=F32))
    scores = _sigmoid(logits)
    sel = scores + bias_ref[...]
    neg_inf = -jnp.inf

    li = lax.broadcasted_iota(I32, (GROUP_SIZE, tm), 0)
    grp_rows = []
    for g in range(N_GROUPS):
        blk = sel[g * GROUP_SIZE:(g + 1) * GROUP_SIZE, :]
        pick, _ = _pick_first_max(blk, li, GROUP_SIZE)
        m1 = jnp.max(blk, axis=0, keepdims=True)
        m2 = jnp.max(jnp.where(pick, neg_inf, blk), axis=0, keepdims=True)
        grp_rows.append(m1 + m2)
    grp = jnp.concatenate(grp_rows, axis=0)

    gi = lax.broadcasted_iota(I32, (N_GROUPS, tm), 0)
    gsel = jnp.zeros((N_GROUPS, tm), F32)
    cur = grp
    for _ in range(TOPK_GROUPS):
        pick, _ = _pick_first_max(cur, gi, N_GROUPS)
        gsel = jnp.where(pick, 1.0, gsel)
        cur = jnp.where(pick, neg_inf, cur)
    emask = jnp.concatenate(
        [jnp.broadcast_to(gsel[g:g + 1, :], (GROUP_SIZE, tm)) for g in range(N_GROUPS)], axis=0)

    ei = lax.broadcasted_iota(I32, (N_EXPERTS, tm), 0)
    cur = jnp.where(emask > 0.5, sel, neg_inf)
    chosen = jnp.zeros((N_EXPERTS, tm), F32)
    idx_rows, w_rows = [], []
    for _ in range(TOP_K):
        pick, first = _pick_first_max(cur, ei, N_EXPERTS)
        idx_rows.append(first)
        w_rows.append(jnp.sum(jnp.where(pick, scores, 0.0), axis=0, keepdims=True))
        chosen = jnp.where(pick, 1.0, chosen)
        cur = jnp.where(pick, neg_inf, cur)
    idx = jnp.concatenate(idx_rows, axis=0)
    w = jnp.concatenate(w_rows, axis=0)
    gate_ref[...] = w / jnp.sum(w, axis=0, keepdims=True) * ROUTED_SCALE
    idx_ref[...] = idx

    ti = lax.broadcasted_iota(I32, (tm, tm), 0)
    tj = lax.broadcasted_iota(I32, (tm, tm), 1)
    before = jnp.where(ti < tj, 1.0, 0.0).astype(BF16)
    local = jnp.dot(chosen.astype(BF16), before, preferred_element_type=F32)
    total = local + run_ref[:, 0:1]
    rank_rows = [jnp.sum(jnp.where(ei == idx_rows[k], total, 0.0), axis=0, keepdims=True)
                 for k in range(TOP_K)]
    rank_ref[...] = jnp.concatenate(rank_rows, axis=0).astype(I32)
    run_ref[...] = run_ref[...] + jnp.sum(chosen, axis=1, keepdims=True)
    cnt_ref[...] = run_ref[...].astype(I32)


def _router(x, w_hi_t, w_lo_t, bias, tm):
    t, d = x.shape
    return pl.pallas_call(
        functools.partial(_router_kernel, tm=tm),
        grid=(t // tm,),
        in_specs=[pl.BlockSpec((tm, d), lambda i: (i, 0)),
                  pl.BlockSpec((N_EXPERTS, d), lambda i: (0, 0)),
                  pl.BlockSpec((N_EXPERTS, d), lambda i: (0, 0)),
                  pl.BlockSpec((N_EXPERTS, 1), lambda i: (0, 0))],
        out_specs=[pl.BlockSpec((TOP_K, tm), lambda i: (0, i)),
                   pl.BlockSpec((TOP_K, tm), lambda i: (0, i)),
                   pl.BlockSpec((TOP_K, tm), lambda i: (0, i)),
                   pl.BlockSpec((N_EXPERTS, LANES), lambda i: (0, 0))],
        out_shape=[jax.ShapeDtypeStruct((TOP_K, t), I32),
                   jax.ShapeDtypeStruct((TOP_K, t), F32),
                   jax.ShapeDtypeStruct((TOP_K, t), I32),
                   jax.ShapeDtypeStruct((N_EXPERTS, LANES), I32)],
        scratch_shapes=[pltpu.VMEM((N_EXPERTS, LANES), F32)],
        compiler_params=_params(1),
        name="router",
    )(x, w_hi_t, w_lo_t, bias.reshape(N_EXPERTS, 1))


PAD_BITS = tuple(1 << s for s in reversed(range(EXPERT_ROWS.bit_length() - 1)))


def _dispatch_kernel(zstart_ref, zcount_ref, dest_ref, x_ref, swg_ref, swu_ref, swd_ref, xs_ref, shared_ref,
                     zero_ref, sem, zsem, *, tm):
    i = pl.program_id(0)

    def start_group(g, carry):
        for j in range(SUBLANES):
            for k in range(TOP_K):
                d = dest_ref[k * tm + g * SUBLANES + j]
                pltpu.make_async_copy(x_ref.at[g, pl.ds(j, 1), :], xs_ref.at[d], sem).start(priority=k % 2)
        return carry

    lax.fori_loop(0, tm // SUBLANES, start_group, 0)

    half = x_ref.shape[-1]
    x_lo, x_hi = _unpack_halves(x_ref[...].reshape(tm, half))
    x_lo = x_lo.astype(BF16)
    x_hi = x_hi.astype(BF16)
    hg = (jnp.dot(x_lo, swg_ref[:half, :], preferred_element_type=F32)
          + jnp.dot(x_hi, swg_ref[half:, :], preferred_element_type=F32))
    hu = (jnp.dot(x_lo, swu_ref[:half, :], preferred_element_type=F32)
          + jnp.dot(x_hi, swu_ref[half:, :], preferred_element_type=F32))
    shared_ref[...] = jnp.dot((_silu(hg) * hu).astype(BF16), swd_ref[...],
                              preferred_element_type=F32).astype(shared_ref.dtype)

    for k in range(TOP_K):
        pltpu.make_async_copy(xs_ref.at[pl.ds(0, tm)], xs_ref.at[pl.ds(0, tm)], sem).wait()

    @pl.when(i == pl.num_programs(0) - 1)
    def _():
        zero_ref[...] = jnp.zeros_like(zero_ref)

        def fill(e, wait):
            pos = zstart_ref[e]
            cnt = zcount_ref[e]
            for bit in PAD_BITS:
                has = (cnt & bit) != 0

                @pl.when(has)
                def _():
                    cp = pltpu.make_async_copy(zero_ref.at[pl.ds(0, bit)], xs_ref.at[pl.ds(pos, bit)], zsem)
                    if wait:
                        cp.wait()
                    else:
                        cp.start()

                pos = pos + jnp.where(has, bit, 0)

        def fill_start(e, carry):
            fill(e, False)
            return carry

        def fill_wait(e, carry):
            fill(e, True)
            return carry

        lax.fori_loop(0, N_EXPERTS, fill_start, 0)
        lax.fori_loop(0, N_EXPERTS, fill_wait, 0)


def _dispatch(x, dest_tiles, zstart, zcount, swg, swu, swd, n_rows, tm):
    t = x.shape[0] * SUBLANES
    d = x.shape[2]
    ds = swg.shape[1]
    const = lambda i, zs, zc: (0, 0)
    return pl.pallas_call(
        functools.partial(_dispatch_kernel, tm=tm),
        grid_spec=pltpu.PrefetchScalarGridSpec(
            num_scalar_prefetch=2,
            grid=(t // tm,),
            in_specs=[pl.BlockSpec((TOP_K * tm,), lambda i, zs, zc: (i,), memory_space=pltpu.SMEM),
                      pl.BlockSpec((tm // SUBLANES, SUBLANES, d), lambda i, zs, zc: (i, 0, 0)),
                      pl.BlockSpec((2 * d, ds), const),
                      pl.BlockSpec((2 * d, ds), const),
                      pl.BlockSpec((ds, 2 * d), const)],
            out_specs=[pl.BlockSpec(memory_space=pl.ANY),
                       pl.BlockSpec((tm, 2 * d), lambda i, zs, zc: (i, 0))],
            scratch_shapes=[pltpu.VMEM((EXPERT_ROWS // 2, 1, d), x.dtype),
                            pltpu.SemaphoreType.DMA(()),
                            pltpu.SemaphoreType.DMA(())]),
        out_shape=[jax.ShapeDtypeStruct((n_rows, 1, d), x.dtype),
                   jax.ShapeDtypeStruct((t, 2 * d), BF16)],
        compiler_params=_params(1),
        name="moe_dispatch",
    )(zstart, zcount, dest_tiles, x, swg, swu, swd)


def _expert_kernel(be_ref, nxt_ref, short_ref, nv_ref, xs_hbm, wg_hbm, wu_hbm, wd_hbm, ys_hbm,
                   wgs_ref, wus_ref, wds_ref, wgb_ref, wub_ref, wdb_ref, xbuf_ref, ybuf_ref,
                   wsems, xsems, ysems):
    b = pl.program_id(0)
    n_valid = nv_ref[0]
    e = be_ref[b]
    prev = be_ref[jnp.maximum(b - 1, 0)]
    fresh = jnp.logical_or(b == 0, e != prev)
    slot = lax.rem(b, 2)

    def weight_copies(expert):
        return (pltpu.make_async_copy(wg_hbm.at[expert], wgs_ref, wsems.at[0]),
                pltpu.make_async_copy(wu_hbm.at[expert], wus_ref, wsems.at[1]),
                pltpu.make_async_copy(wd_hbm.at[expert], wds_ref, wsems.at[2]))

    def rows_in(blk, to_slot):
        r0 = pl.multiple_of(blk * EXPERT_ROWS, EXPERT_ROWS)
        return pltpu.make_async_copy(xs_hbm.at[pl.ds(r0, EXPERT_ROWS), 0, :], xbuf_ref.at[to_slot],
                                     xsems.at[to_slot])

    def rows_out(blk, from_slot):
        r0 = pl.multiple_of(blk * EXPERT_ROWS, EXPERT_ROWS)
        return pltpu.make_async_copy(ybuf_ref.at[from_slot], ys_hbm.at[pl.ds(r0, EXPERT_ROWS), 0, :],
                                     ysems.at[from_slot])

    @pl.when(b == 0)
    def _():
        for cp in weight_copies(e):
            cp.start(priority=1)
        rows_in(b, slot).start()

    @pl.when(fresh)
    def _():
        for cp in weight_copies(e):
            cp.wait()
        wgb_ref[...] = wgs_ref[...].astype(BF16)
        wub_ref[...] = wus_ref[...].astype(BF16)
        wdb_ref[...] = wds_ref[...].astype(BF16)
        nxt = nxt_ref[b]

        @pl.when(nxt >= 0)
        def _():
            for cp in weight_copies(nxt):
                cp.start(priority=1)

    @pl.when(b < n_valid)
    def _():
        rows_in(b, slot).wait()

        @pl.when(b + 1 < n_valid)
        def _():
            rows_in(b + 1, 1 - slot).start()

        @pl.when(b >= 2)
        def _():
            rows_out(b - 2, slot).wait()

        def swiglu_rows(rows):
            half = xbuf_ref.shape[-1]
            x_lo, x_hi = _unpack_halves(xbuf_ref[slot, :rows, :])
            x_lo = x_lo.astype(BF16)
            x_hi = x_hi.astype(BF16)
            hg = (jnp.dot(x_lo, wgb_ref[:half, :], preferred_element_type=F32)
                  + jnp.dot(x_hi, wgb_ref[half:, :], preferred_element_type=F32))
            hu = (jnp.dot(x_lo, wub_ref[:half, :], preferred_element_type=F32)
                  + jnp.dot(x_hi, wub_ref[half:, :], preferred_element_type=F32))
            act = (_silu(hg) * hu).astype(BF16)
            ybuf_ref[slot, :rows, :] = _pack_halves(jnp.dot(act, wdb_ref[...], preferred_element_type=F32))
            if rows < EXPERT_ROWS:
                ybuf_ref[slot, rows:, :] = jnp.zeros((EXPERT_ROWS - rows, half), ybuf_ref.dtype)

        pl.when(short_ref[b] == 1)(functools.partial(swiglu_rows, EXPERT_ROWS // 2))
        pl.when(short_ref[b] == 0)(functools.partial(swiglu_rows, EXPERT_ROWS))
        rows_out(b, slot).start()

        @pl.when(b == n_valid - 1)
        def _():
            rows_out(b, slot).wait()

            @pl.when(b >= 1)
            def _():
                rows_out(b - 1, 1 - slot).wait()


def _experts(xs, wg, wu, wd, block_expert, next_expert, short_block, n_valid):
    n_rows = xs.shape[0]
    d = wg.shape[1]
    de = wg.shape[2]
    nb = n_rows // EXPERT_ROWS
    hbm = pl.BlockSpec(memory_space=pl.ANY)
    return pl.pallas_call(
        _expert_kernel,
        grid_spec=pltpu.PrefetchScalarGridSpec(
            num_scalar_prefetch=4,
            grid=(nb,),
            in_specs=[hbm, hbm, hbm, hbm],
            out_specs=hbm,
            scratch_shapes=[pltpu.VMEM((d, de), F32), pltpu.VMEM((d, de), F32), pltpu.VMEM((de, d), F32),
                            pltpu.VMEM((d, de), BF16), pltpu.VMEM((d, de), BF16), pltpu.VMEM((de, d), BF16),
                            pltpu.VMEM((2, EXPERT_ROWS, d // 2), I32), pltpu.VMEM((2, EXPERT_ROWS, d // 2), I32),
                            pltpu.SemaphoreType.DMA((3,)), pltpu.SemaphoreType.DMA((2,)),
                            pltpu.SemaphoreType.DMA((2,))]),
        out_shape=jax.ShapeDtypeStruct((n_rows, 1, d // 2), I32),
        compiler_params=_params(1),
        name="moe_experts",
    )(block_expert, next_expert, short_block, n_valid, xs, wg, wu, wd)


def _combine_kernel(dest0_ref, destn_ref, x_ref, shared_ref, gate_ref, ys_ref, g_ref, b_ref,
                    op_ref, os_ref, buf_ref, acc_ref, sems, *, tm, n_prompt_tiles):
    i = pl.program_id(0)
    n = pl.num_programs(0)
    half = buf_ref.shape[-1]
    n_groups = tm // COMBINE_GROUP

    def issue_group(dest_ref, to_slot, g):
        for j in range(COMBINE_GROUP):
            t = g * COMBINE_GROUP + j
            for k in range(TOP_K):
                d = dest_ref[k * tm + t]
                pltpu.make_async_copy(ys_ref.at[d], buf_ref.at[to_slot, k, g, pl.ds(j, 1), :],
                                      sems.at[to_slot]).start(priority=k % 2)

    @pl.when(i == 0)
    def _():
        def first(g, carry):
            issue_group(dest0_ref, 0, g)
            return carry
        lax.fori_loop(0, n_groups, first, 0)

    def run_tile(slot):
        for k in range(TOP_K):
            pltpu.make_async_copy(ys_ref.at[pl.ds(0, tm)], ys_ref.at[pl.ds(0, tm)], sems.at[slot]).wait()

        def reduce_group(g):
            r0 = pl.multiple_of(g * COMBINE_GROUP, COMBINE_GROUP)
            gate = gate_ref[pl.ds(r0, COMBINE_GROUP), :]
            acc_lo = acc_hi = None
            for k in range(TOP_K):
                y_lo, y_hi = _unpack_halves(buf_ref[slot, k, g])
                gk = gate[:, k:k + 1]
                acc_lo = gk * y_lo if k == 0 else acc_lo + gk * y_lo
                acc_hi = gk * y_hi if k == 0 else acc_hi + gk * y_hi
            acc_ref[pl.ds(r0, COMBINE_GROUP), :half] = acc_lo
            acc_ref[pl.ds(r0, COMBINE_GROUP), half:] = acc_hi

        @pl.when(i + 1 < n)
        def _():
            def body(g, carry):
                issue_group(destn_ref, 1 - slot, g)
                reduce_group(g)
                return carry
            lax.fori_loop(0, n_groups, body, 0)

        @pl.when(i + 1 >= n)
        def _():
            def body(g, carry):
                reduce_group(g)
                return carry
            lax.fori_loop(0, n_groups, body, 0)

    for s in range(2):
        pl.when(lax.rem(i, 2) == s)(functools.partial(run_tile, s))

    x = x_ref[...]
    out = _layer_norm(DEEPNORM_ALPHA * x + (acc_ref[...] + shared_ref[...].astype(F32)),
                      g_ref[...], b_ref[...])

    @pl.when(i < n_prompt_tiles)
    def _():
        op_ref[...] = out

    @pl.when(i >= n_prompt_tiles)
    def _():
        os_ref[...] = out


def _combine(x, shared, gate_t, dest_tiles, ys, g, b, tm, n_prompt_rows):
    t, d = x.shape
    n_tiles = t // tm
    npt = n_prompt_rows // tm
    const = lambda i: (0, 0)
    return pl.pallas_call(
        functools.partial(_combine_kernel, tm=tm, n_prompt_tiles=npt),
        grid=(n_tiles,),
        in_specs=[pl.BlockSpec((TOP_K * tm,), lambda i: (0,), memory_space=pltpu.SMEM),
                  pl.BlockSpec((TOP_K * tm,), lambda i: (jnp.minimum(i + 1, n_tiles - 1),),
                               memory_space=pltpu.SMEM),
                  pl.BlockSpec((tm, d), lambda i: (i, 0)),
                  pl.BlockSpec((tm, d), lambda i: (i, 0)),
                  pl.BlockSpec((tm, TOP_K), lambda i: (i, 0)),
                  pl.BlockSpec(memory_space=pl.ANY),
                  pl.BlockSpec((1, d), const),
                  pl.BlockSpec((1, d), const)],
        out_specs=[pl.BlockSpec((tm, d), lambda i: (jnp.minimum(i, npt - 1), 0)),
                   pl.BlockSpec((tm, d), lambda i: (jnp.maximum(i - npt, 0), 0))],
        out_shape=[jax.ShapeDtypeStruct((n_prompt_rows, d), F32),
                   jax.ShapeDtypeStruct((t - n_prompt_rows, d), F32)],
        scratch_shapes=[pltpu.VMEM((2, TOP_K, tm // COMBINE_GROUP, COMBINE_GROUP, d // 2), I32),
                        pltpu.VMEM((tm, d), F32),
                        pltpu.SemaphoreType.DMA((2,))],
        compiler_params=_params(1),
        name="moe_combine",
    )(dest_tiles, dest_tiles, x, shared, gate_t, ys, g.reshape(1, d), b.reshape(1, d))


def _tile_major(a, tm):
    k, t = a.shape
    return a.reshape(k, t // tm, tm).transpose(1, 0, 2).reshape(-1)


def _moe_ln(x, x_packed, router_w, router_bias, wg, wu, wd, swg, swu, swd, g, b, n_prompt_rows):
    t, d = x.shape
    w_t = router_w.T
    w_hi = w_t.astype(BF16)
    w_lo = (w_t - w_hi.astype(F32)).astype(BF16)
    tr = _divisor_tile(t, 512, LANES)
    idx, gate, rank, cnt = _router(x, w_hi, w_lo, router_bias, tr)

    counts = cnt[:, 0]
    padded = (counts + EXPERT_ROWS - 1) // EXPERT_ROWS * EXPERT_ROWS
    pad_end = jnp.cumsum(padded)
    pad_start = pad_end - padded
    experts = jnp.arange(N_EXPERTS, dtype=I32)
    onehot = idx[None] == experts[:, None, None]
    dest = jnp.sum(jnp.where(onehot, pad_start[:, None, None], 0), axis=0) + rank
    n_blocks = (t * TOP_K + N_EXPERTS * (EXPERT_ROWS - 1)) // EXPERT_ROWS + 1
    n_valid = (pad_end[-1] // EXPERT_ROWS).astype(I32)
    blk = jnp.minimum(jnp.arange(n_blocks, dtype=I32), n_valid - 1)
    block_expert = jnp.minimum(
        jnp.sum(pad_end[None, :] <= (blk * EXPERT_ROWS)[:, None], axis=1), N_EXPERTS - 1).astype(I32)
    later_used = (experts[None, :] > experts[:, None]) & (counts[None, :] > 0)
    next_used = jnp.min(jnp.where(later_used, experts[None, :], N_EXPERTS), axis=1)
    next_used = jnp.where(next_used == N_EXPERTS, -1, next_used)
    per_block = lambda table: jnp.sum(
        jnp.where(block_expert[:, None] == experts[None, :], table[None, :], 0), axis=1).astype(I32)
    next_expert = per_block(next_used)
    rows_used = per_block(pad_start + counts) - blk * EXPERT_ROWS
    short_block = (rows_used <= EXPERT_ROWS // 2).astype(I32)

    td = _divisor_tile(t, 256, SUBLANES)
    xs, shared = _dispatch(x_packed, _tile_major(dest, td), (pad_start + counts).astype(I32),
                           (padded - counts).astype(I32), swg.astype(BF16), swu.astype(BF16),
                           swd.astype(BF16), n_blocks * EXPERT_ROWS, td)
    ys = _experts(xs, wg, wu, wd, block_expert, next_expert, short_block, n_valid.reshape(1))
    tc = _divisor_tile(math.gcd(n_prompt_rows, t - n_prompt_rows), 128, SUBLANES)
    return _combine(x, shared, gate.T, _tile_major(dest, tc), ys, g, b, tc, n_prompt_rows)


def kernel(x_prompt, x_sample, mem_prompt, state_gla, cache_swa_k, cache_swa_v, cache_mem_k, cache_mem_v,
           w_in, gla_w_gate2, gla_b_gate, gla_norm_g, swa_sinks, w_mix_out, ln1_g, ln1_b,
           mem_wq, mem_wk, mem_wv, mem_wo, ln2_g, ln2_b,
           router_w, router_bias, exp_w_gate, exp_w_up, exp_w_down, sh_w_gate, sh_w_up, sh_w_down,
           ln3_g, ln3_b):
    nbp, seq, d = x_prompt.shape
    nbs, dec, _ = x_sample.shape
    tp, ts = nbp * seq, nbs * dec
    t = tp + ts
    mem_tokens = mem_prompt.shape[1]
    assert state_gla.shape[0] == DEPTH and seq % SWA_WINDOW == 0 and dec % SUBLANES == 0
    assert cache_swa_k.shape[2] == SWA_WINDOW and tp % dec == 0
    tm = _divisor_tile(math.gcd(tp, ts), 512, 16)
    l = 0

    xp = x_prompt.reshape(tp, d)
    xs = x_sample.reshape(ts, d)

    gate_src = GLA_QK + GLA_QK + GLA_VW + GLA_VW
    swa_rows = SWA_QW + 2 * SWA_KVW
    w_in_rt = _reorder_cast(
        w_in[l].T,
        [(0, 0, gate_src),
         (gate_src + GLA_GATE_RANK, COL_SQ, swa_rows),
         (gate_src, COL_GG, GLA_GATE_RANK)],
        IN_COLS_PADDED, _divisor_tile(IN_COLS_PADDED, 512, 16))
    z = _matmul_split(xp, xs, w_in_rt, tm, IN_COLS_PADDED // 2, F32)

    wa2_pad = jnp.concatenate(
        [gla_w_gate2[l], jnp.zeros((LANES - GLA_GATE_RANK, GLA_QK), F32)], axis=0)
    wa2_hi = wa2_pad.astype(BF16)
    wa2_lo = (wa2_pad - wa2_hi.astype(F32)).astype(BF16)
    wa2_stack = jnp.concatenate([wa2_hi, wa2_lo, wa2_hi], axis=0)
    zero_state = jnp.zeros((nbp, GLA_HEADS, GLA_DV, GLA_DK), F32)
    gla_rows = _divisor_tile(seq, GLA_STEP_ROWS, CHUNK)
    mix, st_p = _gla(z, wa2_stack, gla_b_gate[l], gla_norm_g[l], zero_state, None,
                     nb=nbp, n_chunks=seq // gla_rows, rows=gla_rows, chunk=CHUNK, row0=0)
    mix, st_s = _gla(z, wa2_stack, gla_b_gate[l], gla_norm_g[l], jnp.swapaxes(state_gla[l], -1, -2), mix,
                     nb=nbs, n_chunks=1, rows=dec, chunk=dec, row0=tp)
    nblk = seq // SWA_WINDOW
    prev_p = lambda col: (lambda b, i: (b * nblk + jnp.maximum(i - 1, 0), col))
    mix = _swa(z, swa_sinks[l], z, z, (prev_p(COL_SK // SWA_KVW), prev_p(COL_SV // SWA_KVW)), mix,
               nb=nbp, n_blocks=nblk, rows=SWA_WINDOW, row0=0, q_base=0, q_stride=SWA_WINDOW)
    ck = cache_swa_k[l].reshape(nbs * SWA_WINDOW, SWA_KVW)
    cv = cache_swa_v[l].reshape(nbs * SWA_WINDOW, SWA_KVW)
    prev_s = lambda b, i: (b, 0)
    mix = _swa(z, swa_sinks[l], ck, cv, (prev_s, prev_s), mix,
               nb=nbs, n_blocks=1, rows=dec, row0=tp, q_base=PAST_LEN, q_stride=0)
    h1 = _matmul_res_ln(mix, w_mix_out[l].astype(BF16), (xp, xs), ln1_g[l], ln1_b[l], tm)

    mem = mem_prompt.reshape(nbp * mem_tokens, d)
    tmem = _divisor_tile(nbp * mem_tokens, 512, 16)
    mk = _matmul(mem, mem_wk[l].astype(BF16), tmem, _divisor_tile(d, 1024, LANES), F32)
    mv = _matmul(mem, mem_wv[l].astype(BF16), tmem, _divisor_tile(d, 1024, LANES), F32)
    q = _matmul(h1, mem_wq[l].astype(BF16), tm, d, BF16)
    tq = _divisor_tile(seq, 512, 16)
    att = _mem_attn(q, mk, mv, None, nb=nbp, n_tiles=seq // tq, rows=tq, row0=0, mem_tokens=mem_tokens)
    att = _mem_attn(q, cache_mem_k[l].reshape(nbs * mem_tokens, d), cache_mem_v[l].reshape(nbs * mem_tokens, d),
                    att, nb=nbs, n_tiles=1, rows=dec, row0=tp, mem_tokens=mem_tokens)
    h2, h2_packed = _matmul_res_ln(att, mem_wo[l].astype(BF16), h1, ln2_g[l], ln2_b[l], tm, emit_packed=True)

    out_p, out_s = _moe_ln(h2, h2_packed, router_w[l], router_bias[l], exp_w_gate[l], exp_w_up[l],
                           exp_w_down[l], sh_w_gate[l], sh_w_up[l], sh_w_down[l], ln3_g[l], ln3_b[l], tp)

    y_prompt = out_p.reshape(nbp, seq, d)
    y_sample = out_s.reshape(nbs, dec, d)
    kv = z[:, COL_SK:COL_SK + 2 * SWA_KVW]
    kvp = kv[:tp].reshape(nbp, seq, 2 * SWA_KVW)[:, seq - SWA_WINDOW:]
    kvs = kv[tp:].reshape(nbs, dec, 2 * SWA_KVW)
    kv_shape = lambda a: a.reshape(a.shape[0], a.shape[1], SWA_KV_HEADS, SWA_HEAD_DIM)[None]
    mem_shape = lambda a: a.reshape(nbp, mem_tokens, MEM_HEADS, d // MEM_HEADS)[None]
    return (y_prompt, y_sample,
            jnp.swapaxes(st_p, -1, -2)[None],
            kv_shape(kvp[..., :SWA_KVW]), kv_shape(kvp[..., SWA_KVW:]),
            mem_shape(mk), mem_shape(mv),
            jnp.swapaxes(st_s, -1, -2)[None],
            kv_shape(kvs[..., :SWA_KVW]), kv_shape(kvs[..., SWA_KVW:]))
```

```python
import functools
import math

import jax
import jax.numpy as jnp
from jax import lax
from jax.experimental import pallas as pl
from jax.experimental.pallas import tpu as pltpu

F32 = jnp.float32
BF16 = jnp.bfloat16
I32 = jnp.int32

CHUNK = 64
PAST_LEN = 2048
GLA_HEADS = 8
GLA_DK = 64
GLA_DV = 128
GLA_QK = GLA_HEADS * GLA_DK
GLA_VW = GLA_HEADS * GLA_DV
GLA_GATE_RANK = 16
GLA_TAU = 16.0
SWA_HEADS = 16
SWA_KV_HEADS = 2
SWA_GROUP = SWA_HEADS // SWA_KV_HEADS
SWA_HEAD_DIM = 64
SWA_QW = SWA_HEADS * SWA_HEAD_DIM
SWA_KVW = SWA_KV_HEADS * SWA_HEAD_DIM
SWA_WINDOW = 128
WINDOW_CHUNKS = SWA_WINDOW // CHUNK
MEM_HEADS = 4
N_EXPERTS = 64
TOP_K = 8
N_GROUPS = 8
GROUP_SIZE = N_EXPERTS // N_GROUPS
TOPK_GROUPS = 4
ROUTED_SCALE = 2.5
DEPTH = 1
DEEPNORM_ALPHA = (2 * DEPTH) ** 0.25
LN_EPS = 1e-5
RMS_EPS = 1e-6

LANES = 128
SUBLANES = 8
VMEM_LIMIT_BYTES = 56 * 1024 * 1024

COL_GQ = 0
COL_GK = COL_GQ + GLA_QK
COL_GV = COL_GK + GLA_QK
COL_GR = COL_GV + GLA_VW
COL_SQ = COL_GR + GLA_VW
COL_SK = COL_SQ + SWA_QW
COL_SV = COL_SK + SWA_KVW
COL_GG = COL_SV + SWA_KVW
MXU_COLS = 256
IN_COLS_PADDED = -(-(COL_GG + LANES) // (2 * MXU_COLS)) * (2 * MXU_COLS)

EXPERT_ROWS = 256
COMBINE_GROUP = SUBLANES
LN_SUB_ROWS = 128
GLA_STEP_ROWS = 256

NT_DIMS = (((1,), (1,)), ((), ()))
TN_DIMS = (((0,), (0,)), ((), ()))


def _params(n_axes):
    return pltpu.CompilerParams(dimension_semantics=("arbitrary",) * n_axes,
                                vmem_limit_bytes=VMEM_LIMIT_BYTES)


def _sigmoid(x):
    return 1.0 / (1.0 + jnp.exp(-x))


def _silu(x):
    return x * _sigmoid(x)


def _layer_norm(x, g, b):
    mu = jnp.mean(x, axis=-1, keepdims=True)
    xc = x - mu
    var = jnp.mean(xc * xc, axis=-1, keepdims=True)
    return xc * lax.rsqrt(var + LN_EPS) * g + b


HI_MASK = -65536


def _round_to_bf16_bits(x):
    b = lax.bitcast_convert_type(x, I32)
    return b + (0x7FFF + (lax.shift_right_logical(b, 16) & 1))


def _pack_halves(x):
    c = x.shape[1] // 2
    lo = lax.shift_right_logical(_round_to_bf16_bits(x[:, :c]), 16)
    hi = _round_to_bf16_bits(x[:, c:]) & HI_MASK
    return lo | hi


def _unpack_halves(w):
    return (lax.bitcast_convert_type(lax.shift_left(w, 16), F32),
            lax.bitcast_convert_type(w & HI_MASK, F32))


def _divisor_tile(n, pref, mult):
    t = min(pref, n)
    while t > mult and (n % t or t % mult):
        t -= mult
    assert n % t == 0 and t % mult == 0, (n, pref, mult)
    return t


def _mm_kernel(x_ref, w_ref, o_ref, xb_ref):
    @pl.when(pl.program_id(1) == 0)
    def _():
        xb_ref[...] = x_ref[...].astype(BF16)

    o_ref[...] = jnp.dot(xb_ref[...], w_ref[...], preferred_element_type=F32).astype(o_ref.dtype)


def _matmul(x, w, tm, tn, out_dtype):
    m, k = x.shape
    n = w.shape[1]
    return pl.pallas_call(
        _mm_kernel,
        grid=(m // tm, n // tn),
        in_specs=[pl.BlockSpec((tm, k), lambda i, j: (i, 0)),
                  pl.BlockSpec((k, tn), lambda i, j: (0, j))],
        out_specs=pl.BlockSpec((tm, tn), lambda i, j: (i, j)),
        out_shape=jax.ShapeDtypeStruct((m, n), out_dtype),
        scratch_shapes=[pltpu.VMEM((tm, k), BF16)],
        compiler_params=_params(2),
        name="matmul",
    )(x, w)


def _split_rows_specs(tm, k, n_first_tiles, n_grid_axes):
    if n_grid_axes == 1:
        return [pl.BlockSpec((tm, k), lambda i: (jnp.minimum(i, n_first_tiles - 1), 0)),
                pl.BlockSpec((tm, k), lambda i: (jnp.maximum(i - n_first_tiles, 0), 0))]
    return [pl.BlockSpec((tm, k), lambda j, i: (jnp.minimum(i, n_first_tiles - 1), 0)),
            pl.BlockSpec((tm, k), lambda j, i: (jnp.maximum(i - n_first_tiles, 0), 0))]


def _mm_split_kernel(xa_ref, xb_ref, wt_ref, o_ref, *, n_first_tiles):
    i = pl.program_id(1)

    def product(x_ref):
        o_ref[...] = lax.dot_general(x_ref[...].astype(BF16), wt_ref[...], NT_DIMS,
                                     preferred_element_type=F32).astype(o_ref.dtype)

    @pl.when(i < n_first_tiles)
    def _():
        product(xa_ref)

    @pl.when(i >= n_first_tiles)
    def _():
        product(xb_ref)


def _matmul_split(xa, xb, w_t, tm, tn, out_dtype):
    ma, k = xa.shape
    m = ma + xb.shape[0]
    n = w_t.shape[0]
    return pl.pallas_call(
        functools.partial(_mm_split_kernel, n_first_tiles=ma // tm),
        grid=(n // tn, m // tm),
        in_specs=_split_rows_specs(tm, k, ma // tm, 2) + [pl.BlockSpec((tn, k), lambda j, i: (j, 0))],
        out_specs=pl.BlockSpec((tm, tn), lambda j, i: (i, j)),
        out_shape=jax.ShapeDtypeStruct((m, n), out_dtype),
        compiler_params=_params(2),
        name="matmul_split",
    )(xa, xb, w_t)


def _reorder_cast_kernel(src_hbm, dst_hbm, stage_ref, out_ref, in_sems, out_sems, *, chunk_pieces, chunk):
    n = len(chunk_pieces)

    def in_copies(c):
        return [pltpu.make_async_copy(src_hbm.at[pl.ds(s0, ln)], stage_ref.at[c % 2, pl.ds(o0, ln)],
                                      in_sems.at[c % 2]) for s0, o0, ln in chunk_pieces[c]]

    def out_copy(c):
        return pltpu.make_async_copy(out_ref.at[c % 2], dst_hbm.at[pl.ds(c * chunk, chunk)], out_sems.at[c % 2])

    for cp in in_copies(0):
        cp.start()
    for c in range(n):
        if c + 1 < n:
            for cp in in_copies(c + 1):
                cp.start()
        for cp in in_copies(c):
            cp.wait()
        if c >= 2:
            out_copy(c - 2).wait()
        covered = sum(ln for _, _, ln in chunk_pieces[c])
        if covered < chunk:
            stage_ref[c % 2, covered:, :] = jnp.zeros((chunk - covered, stage_ref.shape[-1]), stage_ref.dtype)
        out_ref[c % 2] = stage_ref[c % 2].astype(out_ref.dtype)
        out_copy(c).start()
    for c in range(max(n - 2, 0), n):
        out_copy(c).wait()


def _reorder_cast(src, runs, n_out_rows, chunk):
    k = src.shape[1]
    chunk_pieces = []
    for c0 in range(0, n_out_rows, chunk):
        pieces = []
        for s0, d0, ln in runs:
            lo, hi = max(d0, c0), min(d0 + ln, c0 + chunk)
            if lo < hi:
                assert (lo - c0) % 16 == 0 and (s0 + lo - d0) % SUBLANES == 0 and (hi - lo) % 16 == 0
                pieces.append((s0 + lo - d0, lo - c0, hi - lo))
        chunk_pieces.append(pieces)
    hbm = pl.BlockSpec(memory_space=pl.ANY)
    return pl.pallas_call(
        functools.partial(_reorder_cast_kernel, chunk_pieces=chunk_pieces, chunk=chunk),
        in_specs=[hbm],
        out_specs=hbm,
        out_shape=jax.ShapeDtypeStruct((n_out_rows, k), BF16),
        scratch_shapes=[pltpu.VMEM((2, chunk, k), src.dtype), pltpu.VMEM((2, chunk, k), BF16),
                        pltpu.SemaphoreType.DMA((2,)), pltpu.SemaphoreType.DMA((2,))],
        compiler_params=pltpu.CompilerParams(vmem_limit_bytes=VMEM_LIMIT_BYTES),
        name="reorder_cast",
    )(src)


def _mm_res_ln_kernel(x_ref, w_ref, *refs, n_res, n_first_tiles):
    res_refs, (g_ref, b_ref, o_ref, *packed_ref) = refs[:n_res], refs[n_res:]
    tm = x_ref.shape[0]
    sub = min(tm, LN_SUB_ROWS)
    use_first = pl.program_id(0) < n_first_tiles
    for r in range(0, tm, sub):
        rows = slice(r, r + sub)
        y = jnp.dot(x_ref[rows, :], w_ref[...], preferred_element_type=F32)
        res = res_refs[0][rows, :]
        if n_res == 2:
            res = jnp.where(use_first, res, res_refs[1][rows, :])
        h = _layer_norm(DEEPNORM_ALPHA * res + y, g_ref[...], b_ref[...])
        o_ref[rows, :] = h
        if packed_ref:
            groups = slice(r // SUBLANES, (r + sub) // SUBLANES)
            packed_ref[0][groups] = _pack_halves(h).reshape(sub // SUBLANES, SUBLANES, h.shape[1] // 2)


def _matmul_res_ln(x, w, res, g, b, tm, emit_packed=False):
    m, k = x.shape
    n = w.shape[1]
    if isinstance(res, tuple):
        n_first_tiles = res[0].shape[0] // tm
        res_specs = _split_rows_specs(tm, n, n_first_tiles, 1)
    else:
        n_first_tiles = 0
        res = (res,)
        res_specs = [pl.BlockSpec((tm, n), lambda i: (i, 0))]
    out_specs = [pl.BlockSpec((tm, n), lambda i: (i, 0))]
    out_shape = [jax.ShapeDtypeStruct((m, n), F32)]
    if emit_packed:
        out_specs.append(pl.BlockSpec((tm // SUBLANES, SUBLANES, n // 2), lambda i: (i, 0, 0)))
        out_shape.append(jax.ShapeDtypeStruct((m // SUBLANES, SUBLANES, n // 2), I32))
    outs = pl.pallas_call(
        functools.partial(_mm_res_ln_kernel, n_res=len(res), n_first_tiles=n_first_tiles),
        grid=(m // tm,),
        in_specs=[pl.BlockSpec((tm, k), lambda i: (i, 0)),
                  pl.BlockSpec((k, n), lambda i: (0, 0))] + res_specs + [
                  pl.BlockSpec((1, n), lambda i: (0, 0)),
                  pl.BlockSpec((1, n), lambda i: (0, 0))],
        out_specs=out_specs,
        out_shape=out_shape,
        compiler_params=_params(1),
        name="matmul_res_ln",
    )(x, w, *res, g.reshape(1, n), b.reshape(1, n))
    return outs if emit_packed else outs[0]


def _split3_bf16(x):
    hi = x.astype(BF16)
    r1 = x - hi.astype(F32)
    mid = r1.astype(BF16)
    lo = (r1 - mid.astype(F32)).astype(BF16)
    return jnp.concatenate([hi, mid, lo], axis=-1)


def _gla_kernel(q_ref, k_ref, v_ref, r_ref, gg_ref, wa2_ref, ba_ref, ng_ref, s0_ref, mix_in_ref,
                o_ref, sfin_ref, st_ref, *, rows, chunk):
    del mix_in_ref
    n_sub = rows // chunk
    shift = chunk.bit_length() - 1
    assert chunk == 1 << shift

    @pl.when(pl.program_id(1) == 0)
    def _():
        for h in range(GLA_HEADS):
            st_ref[h] = s0_ref[0, h].T

    gg = gg_ref[...]
    gg_hi = gg.astype(BF16)
    gg_lo = (gg - gg_hi.astype(F32)).astype(BF16)
    gate = jnp.dot(jnp.concatenate([gg_hi, gg_hi, gg_lo], axis=-1), wa2_ref[...],
                   preferred_element_type=F32) + ba_ref[...]
    log_a = (jnp.minimum(gate, 0.0) - jnp.log1p(jnp.exp(-jnp.abs(gate)))) / GLA_TAU

    ri = lax.broadcasted_iota(I32, (rows, rows), 0)
    ci = lax.broadcasted_iota(I32, (rows, rows), 1)
    same_chunk = lax.shift_right_logical(ri, shift) == lax.shift_right_logical(ci, shift)
    causal = same_chunk & (ri >= ci)
    ones = jnp.concatenate([jnp.where(causal, 1.0, 0.0), jnp.where(same_chunk, 1.0, 0.0)], axis=0).astype(BF16)
    sums = jnp.dot(ones, _split3_bf16(log_a), preferred_element_type=F32)
    sums = sums[:, :GLA_QK] + sums[:, GLA_QK:2 * GLA_QK] + sums[:, 2 * GLA_QK:]
    bcum = sums[:rows]
    b_last = sums[rows:]
    q = q_ref[...] * (GLA_DK ** -0.5)
    k = k_ref[...]
    q_in = (q * jnp.exp(bcum)).astype(BF16)
    k_in = (k * jnp.exp(-bcum)).astype(BF16)
    k_out = (k * jnp.exp(b_last - bcum)).astype(BF16)
    decay = jnp.exp(b_last)
    ng = ng_ref[...]
    blk_r = lax.shift_right_logical(lax.broadcasted_iota(I32, (rows, n_sub * GLA_DK), 0), shift)
    blk_c = lax.shift_right_logical(lax.broadcasted_iota(I32, (rows, n_sub * GLA_DK), 1),
                                    GLA_DK.bit_length() - 1)
    for h in range(GLA_HEADS):
        ks = slice(h * GLA_DK, (h + 1) * GLA_DK)
        vs = slice(h * GLA_DV, (h + 1) * GLA_DV)
        vh = v_ref[:, vs].astype(BF16)
        qh = q_in[:, ks]
        attn = lax.dot_general(qh, k_in[:, ks], NT_DIMS, preferred_element_type=F32)
        attn = jnp.where(causal, attn, 0.0).astype(BF16)
        o = jnp.dot(attn, vh, preferred_element_type=F32)
        ko = k_out[:, ks]
        if n_sub > 1:
            ko = jnp.where(blk_r == blk_c, jnp.concatenate([ko] * n_sub, axis=-1), 0.0)
        upd_t = lax.dot_general(vh, ko, TN_DIMS, preferred_element_type=F32)
        st = st_ref[h]
        states = []
        for c in range(n_sub):
            states.append(st.astype(BF16))
            st = st * decay[c * chunk:c * chunk + 1, ks] + upd_t[:, c * GLA_DK:(c + 1) * GLA_DK]
        st_ref[h] = st
        o_all = lax.dot_general(qh, jnp.concatenate(states, axis=0), NT_DIMS, preferred_element_type=F32)
        o = o + jnp.concatenate(
            [o_all[c * chunk:(c + 1) * chunk, c * GLA_DV:(c + 1) * GLA_DV] for c in range(n_sub)], axis=0)
        o = o * lax.rsqrt(jnp.mean(o * o, axis=-1, keepdims=True) + RMS_EPS) * ng
        o_ref[:, vs] = (o * _silu(r_ref[:, vs])).astype(o_ref.dtype)
    @pl.when(pl.program_id(1) == pl.num_programs(1) - 1)
    def _():
        for h in range(GLA_HEADS):
            sfin_ref[0, h] = st_ref[h].T


def _gla(z, wa2_stack, ba, norm_g, s0, mix_in, *, nb, n_chunks, rows, chunk, row0):
    t = z.shape[0]
    rb0 = row0 // rows
    rmap = lambda colblk: (lambda b, c: (rb0 + b * n_chunks + c, colblk))
    in_specs = [
        pl.BlockSpec((rows, GLA_QK), rmap(COL_GQ // GLA_QK)),
        pl.BlockSpec((rows, GLA_QK), rmap(COL_GK // GLA_QK)),
        pl.BlockSpec((rows, GLA_VW), rmap(COL_GV // GLA_VW)),
        pl.BlockSpec((rows, GLA_VW), rmap(COL_GR // GLA_VW)),
        pl.BlockSpec((rows, LANES), rmap(COL_GG // LANES)),
        pl.BlockSpec((3 * LANES, GLA_QK), lambda b, c: (0, 0)),
        pl.BlockSpec((1, GLA_QK), lambda b, c: (0, 0)),
        pl.BlockSpec((1, GLA_DV), lambda b, c: (0, 0)),
        pl.BlockSpec((1, GLA_HEADS, GLA_DK, GLA_DV), lambda b, c: (b, 0, 0, 0)),
    ]
    args = [z, z, z, z, z, wa2_stack, ba.reshape(1, GLA_QK), norm_g.reshape(1, GLA_DV), s0]
    aliases = {}
    if mix_in is None:
        mix_in = jnp.zeros((SUBLANES, LANES), BF16)
        in_specs.append(pl.BlockSpec(memory_space=pl.ANY))
    else:
        in_specs.append(pl.BlockSpec(memory_space=pl.ANY))
        aliases = {len(args): 0}
    args.append(mix_in)
    mix, s_fin = pl.pallas_call(
        functools.partial(_gla_kernel, rows=rows, chunk=chunk),
        grid=(nb, n_chunks),
        in_specs=in_specs,
        out_specs=[pl.BlockSpec((rows, GLA_VW), rmap(0)),
                   pl.BlockSpec((1, GLA_HEADS, GLA_DK, GLA_DV), lambda b, c: (b, 0, 0, 0))],
        out_shape=[jax.ShapeDtypeStruct((t, GLA_VW + SWA_QW), BF16),
                   jax.ShapeDtypeStruct((nb, GLA_HEADS, GLA_DK, GLA_DV), F32)],
        scratch_shapes=[pltpu.VMEM((GLA_HEADS, GLA_DV, GLA_DK), F32)],
        input_output_aliases=aliases,
        compiler_params=_params(2),
        name="gla",
    )(*args)
    return mix, s_fin


def _swa_kernel(sinks_ref, q_ref, kc_ref, vc_ref, kp_ref, vp_ref, mix_in_ref, o_ref, bias_ref, *,
                rows, q_base, q_stride):
    del mix_in_ref
    i = pl.program_id(1)
    n_tables = bias_ref.shape[0]
    nk = SWA_WINDOW + rows
    sr = SWA_GROUP * rows
    assert rows & (rows - 1) == 0 and q_base % CHUNK == 0 and q_stride % CHUNK == 0
    assert n_tables == 1 or q_base + q_stride >= SWA_WINDOW
    head_in_group = lax.shift_right_logical(lax.broadcasted_iota(I32, (sr, 1), 0), rows.bit_length() - 1)

    @pl.when(jnp.logical_and(pl.program_id(0) == 0, i == 0))
    def _():
        row = lax.broadcasted_iota(I32, (sr, nk), 0)
        col = lax.broadcasted_iota(I32, (sr, nk), 1)
        for t in range(n_tables):
            q0 = q_base + t * q_stride
            qpos = q0 + (row & (rows - 1))
            kpos = q0 - SWA_WINDOW + col
            dist = jnp.abs(qpos - kpos).astype(F32)
            qc = lax.shift_right_arithmetic(qpos, CHUNK.bit_length() - 1)
            kc = lax.shift_right_arithmetic(kpos, CHUNK.bit_length() - 1)
            allowed = (kpos >= 0) & (kc <= qc) & (kc >= qc - WINDOW_CHUNKS)
            for g in range(SWA_KV_HEADS):
                slope = jnp.zeros((sr, 1), F32)
                for j in range(SWA_GROUP):
                    h = g * SWA_GROUP + j
                    slope = jnp.where(head_in_group == j, 2.0 ** (-8.0 * (h + 1) / SWA_HEADS), slope)
                bias_ref[t, g] = jnp.where(allowed, -(slope * dist), -jnp.inf)

    table = jnp.minimum(i, n_tables - 1)
    keys = jnp.concatenate([kp_ref[...], kc_ref[...]], axis=0).astype(BF16)
    vals = jnp.concatenate([vp_ref[...], vc_ref[...]], axis=0).astype(BF16)
    for g in range(SWA_KV_HEADS):
        sink = jnp.zeros((sr, 1), F32)
        for j in range(SWA_GROUP):
            sink = jnp.where(head_in_group == j, sinks_ref[g * SWA_GROUP + j], sink)
        kv = slice(g * SWA_HEAD_DIM, (g + 1) * SWA_HEAD_DIM)
        qg = jnp.concatenate(
            [q_ref[:, (g * SWA_GROUP + j) * SWA_HEAD_DIM:(g * SWA_GROUP + j + 1) * SWA_HEAD_DIM]
             for j in range(SWA_GROUP)], axis=0).astype(BF16)
        s = lax.dot_general(qg, keys[:, kv], NT_DIMS, preferred_element_type=F32)
        s = s * (SWA_HEAD_DIM ** -0.5) + bias_ref[table, g]
        m = jnp.maximum(jnp.max(s, axis=-1, keepdims=True), sink)
        p = jnp.exp(s - m)
        denom = jnp.sum(p, axis=-1, keepdims=True) + jnp.exp(sink - m)
        o = jnp.dot(p.astype(BF16), vals[:, kv], preferred_element_type=F32) / denom
        for j in range(0, SWA_GROUP, 2):
            c0 = (g * SWA_GROUP + j) * SWA_HEAD_DIM
            o_ref[:, c0:c0 + LANES] = jnp.concatenate(
                [o[j * rows:(j + 1) * rows], o[(j + 1) * rows:(j + 2) * rows]], axis=-1).astype(o_ref.dtype)


def _swa(z, sinks, k_prev, v_prev, prev_map, mix_in, *, nb, n_blocks, rows, row0, q_base, q_stride):
    rb0 = row0 // rows
    rmap = lambda colblk: (lambda b, i: (rb0 + b * n_blocks + i, colblk))
    return pl.pallas_call(
        functools.partial(_swa_kernel, rows=rows, q_base=q_base, q_stride=q_stride),
        grid=(nb, n_blocks),
        in_specs=[pl.BlockSpec(memory_space=pltpu.SMEM),
                  pl.BlockSpec((rows, SWA_QW), rmap(COL_SQ // SWA_QW)),
                  pl.BlockSpec((rows, SWA_KVW), rmap(COL_SK // SWA_KVW)),
                  pl.BlockSpec((rows, SWA_KVW), rmap(COL_SV // SWA_KVW)),
                  pl.BlockSpec((SWA_WINDOW, SWA_KVW), prev_map[0]),
                  pl.BlockSpec((SWA_WINDOW, SWA_KVW), prev_map[1]),
                  pl.BlockSpec(memory_space=pl.ANY)],
        out_specs=pl.BlockSpec((rows, SWA_QW), rmap(GLA_VW // SWA_QW)),
        out_shape=jax.ShapeDtypeStruct(mix_in.shape, mix_in.dtype),
        scratch_shapes=[pltpu.VMEM((min(2, n_blocks), SWA_KV_HEADS, SWA_GROUP * rows, SWA_WINDOW + rows), F32)],
        input_output_aliases={6: 0},
        compiler_params=_params(2),
        name="swa",
    )(sinks, z, z, z, k_prev, v_prev, mix_in)


def _mem_attn_kernel(q_ref, k_ref, v_ref, o_in_ref, o_ref, *, head_dim):
    del o_in_ref
    for h in range(MEM_HEADS):
        hs = slice(h * head_dim, (h + 1) * head_dim)
        kh = k_ref[:, hs].astype(BF16)
        vh = v_ref[:, hs].astype(BF16)
        s = lax.dot_general(q_ref[:, hs], kh, NT_DIMS, preferred_element_type=F32) * (head_dim ** -0.5)
        m = jnp.max(s, axis=-1, keepdims=True)
        p = jnp.exp(s - m)
        denom = jnp.sum(p, axis=-1, keepdims=True)
        o = jnp.dot(p.astype(BF16), vh, preferred_element_type=F32) / denom
        o_ref[:, hs] = o.astype(o_ref.dtype)


def _mem_attn(q, mk, mv, o_in, *, nb, n_tiles, rows, row0, mem_tokens):
    t, d = q.shape
    rb0 = row0 // rows
    qmap = lambda b, i: (rb0 + b * n_tiles + i, 0)
    in_specs = [pl.BlockSpec((rows, d), qmap),
                pl.BlockSpec((mem_tokens, d), lambda b, i: (b, 0)),
                pl.BlockSpec((mem_tokens, d), lambda b, i: (b, 0)),
                pl.BlockSpec(memory_space=pl.ANY)]
    aliases = {}
    if o_in is None:
        o_in = jnp.zeros((SUBLANES, LANES), BF16)
    else:
        aliases = {3: 0}
    return pl.pallas_call(
        functools.partial(_mem_attn_kernel, head_dim=d // MEM_HEADS),
        grid=(nb, n_tiles),
        in_specs=in_specs,
        out_specs=pl.BlockSpec((rows, d), qmap),
        out_shape=jax.ShapeDtypeStruct((t, d), BF16),
        input_output_aliases=aliases,
        compiler_params=_params(2),
        name="mem_attn",
    )(q, mk, mv, o_in)


def _pick_first_max(cur, iota, sentinel):
    mx = jnp.max(cur, axis=0, keepdims=True)
    first = jnp.min(jnp.where(cur == mx, iota, sentinel), axis=0, keepdims=True)
    return iota == first, first


def _router_kernel(x_ref, wh_ref, wl_ref, bias_ref, idx_ref, gate_ref, rank_ref, cnt_ref, run_ref, *, tm):
    i = pl.program_id(0)

    @pl.when(i == 0)
    def _():
        run_ref[...] = jnp.zeros_like(run_ref)

    x = x_ref[...]
    xh = x.astype(BF16)
    xl = (x - xh.astype(F32)).astype(BF16)
    wh = wh_ref[...]
    logits = (lax.dot_general(wh, xh, NT_DIMS, preferred_element_type=F32)
              + lax.dot_general(wh, xl, NT_DIMS, preferred_element_type=F32)
              + lax.dot_general(wl_ref[...], xh, NT_DIMS, preferred_element_type=F32))
    scores = _sigmoid(logits)
    sel = scores + bias_ref[...]
    neg_inf = -jnp.inf

    li = lax.broadcasted_iota(I32, (GROUP_SIZE, tm), 0)
    grp_rows = []
    for g in range(N_GROUPS):
        blk = sel[g * GROUP_SIZE:(g + 1) * GROUP_SIZE, :]
        pick, _ = _pick_first_max(blk, li, GROUP_SIZE)
        m1 = jnp.max(blk, axis=0, keepdims=True)
        m2 = jnp.max(jnp.where(pick, neg_inf, blk), axis=0, keepdims=True)
        grp_rows.append(m1 + m2)
    grp = jnp.concatenate(grp_rows, axis=0)

    gi = lax.broadcasted_iota(I32, (N_GROUPS, tm), 0)
    gsel = jnp.zeros((N_GROUPS, tm), F32)
    cur = grp
    for _ in range(TOPK_GROUPS):
        pick, _ = _pick_first_max(cur, gi, N_GROUPS)
        gsel = jnp.where(pick, 1.0, gsel)
        cur = jnp.where(pick, neg_inf, cur)
    emask = jnp.concatenate(
        [jnp.broadcast_to(gsel[g:g + 1, :], (GROUP_SIZE, tm)) for g in range(N_GROUPS)], axis=0)

    ei = lax.broadcasted_iota(I32, (N_EXPERTS, tm), 0)
    cur = jnp.where(emask > 0.5, sel, neg_inf)
    chosen = jnp.zeros((N_EXPERTS, tm), F32)
    idx_rows, w_rows = [], []
    for _ in range(TOP_K):
        pick, first = _pick_first_max(cur, ei, N_EXPERTS)
        idx_rows.append(first)
        w_rows.append(jnp.sum(jnp.where(pick, scores, 0.0), axis=0, keepdims=True))
        chosen = jnp.where(pick, 1.0, chosen)
        cur = jnp.where(pick, neg_inf, cur)
    idx = jnp.concatenate(idx_rows, axis=0)
    w = jnp.concatenate(w_rows, axis=0)
    gate_ref[...] = w / jnp.sum(w, axis=0, keepdims=True) * ROUTED_SCALE
    idx_ref[...] = idx

    ti = lax.broadcasted_iota(I32, (tm, tm), 0)
    tj = lax.broadcasted_iota(I32, (tm, tm), 1)
    before = jnp.where(ti < tj, 1.0, 0.0).astype(BF16)
    local = jnp.dot(chosen.astype(BF16), before, preferred_element_type=F32)
    total = local + run_ref[:, 0:1]
    rank_rows = [jnp.sum(jnp.where(ei == idx_rows[k], total, 0.0), axis=0, keepdims=True)
                 for k in range(TOP_K)]
    rank_ref[...] = jnp.concatenate(rank_rows, axis=0).astype(I32)
    run_ref[...] = run_ref[...] + jnp.sum(chosen, axis=1, keepdims=True)
    cnt_ref[...] = run_ref[...].astype(I32)


def _router(x, w_hi_t, w_lo_t, bias, tm):
    t, d = x.shape
    return pl.pallas_call(
        functools.partial(_router_kernel, tm=tm),
        grid=(t // tm,),
        in_specs=[pl.BlockSpec((tm, d), lambda i: (i, 0)),
                  pl.BlockSpec((N_EXPERTS, d), lambda i: (0, 0)),
                  pl.BlockSpec((N_EXPERTS, d), lambda i: (0, 0)),
                  pl.BlockSpec((N_EXPERTS, 1), lambda i: (0, 0))],
        out_specs=[pl.BlockSpec((TOP_K, tm), lambda i: (0, i)),
                   pl.BlockSpec((TOP_K, tm), lambda i: (0, i)),
                   pl.BlockSpec((TOP_K, tm), lambda i: (0, i)),
                   pl.BlockSpec((N_EXPERTS, LANES), lambda i: (0, 0))],
        out_shape=[jax.ShapeDtypeStruct((TOP_K, t), I32),
                   jax.ShapeDtypeStruct((TOP_K, t), F32),
                   jax.ShapeDtypeStruct((TOP_K, t), I32),
                   jax.ShapeDtypeStruct((N_EXPERTS, LANES), I32)],
        scratch_shapes=[pltpu.VMEM((N_EXPERTS, LANES), F32)],
        compiler_params=_params(1),
        name="router",
    )(x, w_hi_t, w_lo_t, bias.reshape(N_EXPERTS, 1))


PAD_BITS = tuple(1 << s for s in reversed(range(EXPERT_ROWS.bit_length() - 1)))


def _dispatch_kernel(zstart_ref, zcount_ref, dest_ref, x_ref, swg_ref, swu_ref, swd_ref, xs_ref, shared_ref,
                     zero_ref, sem, zsem, *, tm):
    i = pl.program_id(0)

    def start_group(g, carry):
        for j in range(SUBLANES):
            for k in range(TOP_K):
                d = dest_ref[k * tm + g * SUBLANES + j]
                pltpu.make_async_copy(x_ref.at[g, pl.ds(j, 1), :], xs_ref.at[d], sem).start(priority=k % 2)
        return carry

    lax.fori_loop(0, tm // SUBLANES, start_group, 0)

    half = x_ref.shape[-1]
    x_lo, x_hi = _unpack_halves(x_ref[...].reshape(tm, half))
    xb = jnp.concatenate([x_lo.astype(BF16), x_hi.astype(BF16)], axis=-1)
    hg = jnp.dot(xb, swg_ref[...], preferred_element_type=F32)
    hu = jnp.dot(xb, swu_ref[...], preferred_element_type=F32)
    shared_ref[...] = jnp.dot((_silu(hg) * hu).astype(BF16), swd_ref[...],
                              preferred_element_type=F32).astype(shared_ref.dtype)

    for k in range(TOP_K):
        pltpu.make_async_copy(xs_ref.at[pl.ds(0, tm)], xs_ref.at[pl.ds(0, tm)], sem).wait()

    @pl.when(i == pl.num_programs(0) - 1)
    def _():
        zero_ref[...] = jnp.zeros_like(zero_ref)

        def fill(e, wait):
            pos = zstart_ref[e]
            cnt = zcount_ref[e]
            for bit in PAD_BITS:
                has = (cnt & bit) != 0

                @pl.when(has)
                def _():
                    cp = pltpu.make_async_copy(zero_ref.at[pl.ds(0, bit)], xs_ref.at[pl.ds(pos, bit)], zsem)
                    if wait:
                        cp.wait()
                    else:
                        cp.start()

                pos = pos + jnp.where(has, bit, 0)

        def fill_start(e, carry):
            fill(e, False)
            return carry

        def fill_wait(e, carry):
            fill(e, True)
            return carry

        lax.fori_loop(0, N_EXPERTS, fill_start, 0)
        lax.fori_loop(0, N_EXPERTS, fill_wait, 0)


def _dispatch(x, dest_tiles, zstart, zcount, swg, swu, swd, n_rows, tm):
    t = x.shape[0] * SUBLANES
    d = x.shape[2]
    ds = swg.shape[1]
    const = lambda i, zs, zc: (0, 0)
    return pl.pallas_call(
        functools.partial(_dispatch_kernel, tm=tm),
        grid_spec=pltpu.PrefetchScalarGridSpec(
            num_scalar_prefetch=2,
            grid=(t // tm,),
            in_specs=[pl.BlockSpec((TOP_K * tm,), lambda i, zs, zc: (i,), memory_space=pltpu.SMEM),
                      pl.BlockSpec((tm // SUBLANES, SUBLANES, d), lambda i, zs, zc: (i, 0, 0)),
                      pl.BlockSpec((2 * d, ds), const),
                      pl.BlockSpec((2 * d, ds), const),
                      pl.BlockSpec((ds, 2 * d), const)],
            out_specs=[pl.BlockSpec(memory_space=pl.ANY),
                       pl.BlockSpec((tm, 2 * d), lambda i, zs, zc: (i, 0))],
            scratch_shapes=[pltpu.VMEM((EXPERT_ROWS // 2, 1, d), x.dtype),
                            pltpu.SemaphoreType.DMA(()),
                            pltpu.SemaphoreType.DMA(())]),
        out_shape=[jax.ShapeDtypeStruct((n_rows, 1, d), x.dtype),
                   jax.ShapeDtypeStruct((t, 2 * d), BF16)],
        compiler_params=_params(1),
        name="moe_dispatch",
    )(zstart, zcount, dest_tiles, x, swg, swu, swd)


def _expert_kernel(be_ref, nxt_ref, short_ref, nv_ref, xs_hbm, wg_hbm, wu_hbm, wd_hbm, ys_hbm,
                   wgs_ref, wus_ref, wds_ref, wgb_ref, wub_ref, wdb_ref, xbuf_ref, ybuf_ref,
                   wsems, xsems, ysems):
    b = pl.program_id(0)
    n_valid = nv_ref[0]
    e = be_ref[b]
    prev = be_ref[jnp.maximum(b - 1, 0)]
    fresh = jnp.logical_or(b == 0, e != prev)
    slot = lax.rem(b, 2)

    def weight_copies(expert):
        return (pltpu.make_async_copy(wg_hbm.at[expert], wgs_ref, wsems.at[0]),
                pltpu.make_async_copy(wu_hbm.at[expert], wus_ref, wsems.at[1]),
                pltpu.make_async_copy(wd_hbm.at[expert], wds_ref, wsems.at[2]))

    def rows_in(blk, to_slot):
        r0 = pl.multiple_of(blk * EXPERT_ROWS, EXPERT_ROWS)
        return pltpu.make_async_copy(xs_hbm.at[pl.ds(r0, EXPERT_ROWS), 0, :], xbuf_ref.at[to_slot],
                                     xsems.at[to_slot])

    def rows_out(blk, from_slot):
        r0 = pl.multiple_of(blk * EXPERT_ROWS, EXPERT_ROWS)
        return pltpu.make_async_copy(ybuf_ref.at[from_slot], ys_hbm.at[pl.ds(r0, EXPERT_ROWS), 0, :],
                                     ysems.at[from_slot])

    @pl.when(b == 0)
    def _():
        for cp in weight_copies(e):
            cp.start(priority=1)
        rows_in(b, slot).start()

    @pl.when(fresh)
    def _():
        for cp in weight_copies(e):
            cp.wait()
        wgb_ref[...] = wgs_ref[...].astype(BF16)
        wub_ref[...] = wus_ref[...].astype(BF16)
        wdb_ref[...] = wds_ref[...].astype(BF16)
        nxt = nxt_ref[b]

        @pl.when(nxt >= 0)
        def _():
            for cp in weight_copies(nxt):
                cp.start(priority=1)

    @pl.when(b < n_valid)
    def _():
        rows_in(b, slot).wait()

        @pl.when(b + 1 < n_valid)
        def _():
            rows_in(b + 1, 1 - slot).start()

        @pl.when(b >= 2)
        def _():
            rows_out(b - 2, slot).wait()

        def swiglu_rows(rows):
            half = xbuf_ref.shape[-1]
            x_lo, x_hi = _unpack_halves(xbuf_ref[slot, :rows, :])
            xb = jnp.concatenate([x_lo.astype(BF16), x_hi.astype(BF16)], axis=-1)
            hg = jnp.dot(xb, wgb_ref[...], preferred_element_type=F32)
            hu = jnp.dot(xb, wub_ref[...], preferred_element_type=F32)
            act = (_silu(hg) * hu).astype(BF16)
            ybuf_ref[slot, :rows, :] = _pack_halves(jnp.dot(act, wdb_ref[...], preferred_element_type=F32))
            if rows < EXPERT_ROWS:
                ybuf_ref[slot, rows:, :] = jnp.zeros((EXPERT_ROWS - rows, half), ybuf_ref.dtype)

        pl.when(short_ref[b] == 1)(functools.partial(swiglu_rows, EXPERT_ROWS // 2))
        pl.when(short_ref[b] == 0)(functools.partial(swiglu_rows, EXPERT_ROWS))
        rows_out(b, slot).start()

        @pl.when(b == n_valid - 1)
        def _():
            rows_out(b, slot).wait()

            @pl.when(b >= 1)
            def _():
                rows_out(b - 1, 1 - slot).wait()


def _experts(xs, wg, wu, wd, block_expert, next_expert, short_block, n_valid):
    n_rows = xs.shape[0]
    d = wg.shape[1]
    de = wg.shape[2]
    nb = n_rows // EXPERT_ROWS
    hbm = pl.BlockSpec(memory_space=pl.ANY)
    return pl.pallas_call(
        _expert_kernel,
        grid_spec=pltpu.PrefetchScalarGridSpec(
            num_scalar_prefetch=4,
            grid=(nb,),
            in_specs=[hbm, hbm, hbm, hbm],
            out_specs=hbm,
            scratch_shapes=[pltpu.VMEM((d, de), F32), pltpu.VMEM((d, de), F32), pltpu.VMEM((de, d), F32),
                            pltpu.VMEM((d, de), BF16), pltpu.VMEM((d, de), BF16), pltpu.VMEM((de, d), BF16),
                            pltpu.VMEM((2, EXPERT_ROWS, d // 2), I32), pltpu.VMEM((2, EXPERT_ROWS, d // 2), I32),
                            pltpu.SemaphoreType.DMA((3,)), pltpu.SemaphoreType.DMA((2,)),
                            pltpu.SemaphoreType.DMA((2,))]),
        out_shape=jax.ShapeDtypeStruct((n_rows, 1, d // 2), I32),
        compiler_params=_params(1),
        name="moe_experts",
    )(block_expert, next_expert, short_block, n_valid, xs, wg, wu, wd)


def _combine_kernel(dest0_ref, destn_ref, x_ref, shared_ref, gate_ref, ys_ref, g_ref, b_ref,
                    op_ref, os_ref, buf_ref, acc_ref, sems, *, tm, n_prompt_tiles):
    i = pl.program_id(0)
    n = pl.num_programs(0)
    half = buf_ref.shape[-1]
    n_groups = tm // COMBINE_GROUP

    def issue_group(dest_ref, to_slot, g):
        for j in range(COMBINE_GROUP):
            t = g * COMBINE_GROUP + j
            for k in range(TOP_K):
                d = dest_ref[k * tm + t]
                pltpu.make_async_copy(ys_ref.at[d], buf_ref.at[to_slot, k, g, pl.ds(j, 1), :],
                                      sems.at[to_slot]).start(priority=k % 2)

    @pl.when(i == 0)
    def _():
        def first(g, carry):
            issue_group(dest0_ref, 0, g)
            return carry
        lax.fori_loop(0, n_groups, first, 0)

    def run_tile(slot):
        for k in range(TOP_K):
            pltpu.make_async_copy(ys_ref.at[pl.ds(0, tm)], ys_ref.at[pl.ds(0, tm)], sems.at[slot]).wait()

        def reduce_group(g):
            r0 = pl.multiple_of(g * COMBINE_GROUP, COMBINE_GROUP)
            gate = gate_ref[pl.ds(r0, COMBINE_GROUP), :]
            acc_lo = acc_hi = None
            for k in range(TOP_K):
                y_lo, y_hi = _unpack_halves(buf_ref[slot, k, g])
                gk = gate[:, k:k + 1]
                acc_lo = gk * y_lo if k == 0 else acc_lo + gk * y_lo
                acc_hi = gk * y_hi if k == 0 else acc_hi + gk * y_hi
            acc_ref[pl.ds(r0, COMBINE_GROUP), :half] = acc_lo
            acc_ref[pl.ds(r0, COMBINE_GROUP), half:] = acc_hi

        @pl.when(i + 1 < n)
        def _():
            def body(g, carry):
                issue_group(destn_ref, 1 - slot, g)
                reduce_group(g)
                return carry
            lax.fori_loop(0, n_groups, body, 0)

        @pl.when(i + 1 >= n)
        def _():
            def body(g, carry):
                reduce_group(g)
                return carry
            lax.fori_loop(0, n_groups, body, 0)

    for s in range(2):
        pl.when(lax.rem(i, 2) == s)(functools.partial(run_tile, s))

    x = x_ref[...]
    out = _layer_norm(DEEPNORM_ALPHA * x + (acc_ref[...] + shared_ref[...].astype(F32)),
                      g_ref[...], b_ref[...])

    @pl.when(i < n_prompt_tiles)
    def _():
        op_ref[...] = out

    @pl.when(i >= n_prompt_tiles)
    def _():
        os_ref[...] = out


def _combine(x, shared, gate_t, dest_tiles, ys, g, b, tm, n_prompt_rows):
    t, d = x.shape
    n_tiles = t // tm
    npt = n_prompt_rows // tm
    const = lambda i: (0, 0)
    return pl.pallas_call(
        functools.partial(_combine_kernel, tm=tm, n_prompt_tiles=npt),
        grid=(n_tiles,),
        in_specs=[pl.BlockSpec((TOP_K * tm,), lambda i: (0,), memory_space=pltpu.SMEM),
                  pl.BlockSpec((TOP_K * tm,), lambda i: (jnp.minimum(i + 1, n_tiles - 1),),
                               memory_space=pltpu.SMEM),
                  pl.BlockSpec((tm, d), lambda i: (i, 0)),
                  pl.BlockSpec((tm, d), lambda i: (i, 0)),
                  pl.BlockSpec((tm, TOP_K), lambda i: (i, 0)),
                  pl.BlockSpec(memory_space=pl.ANY),
                  pl.BlockSpec((1, d), const),
                  pl.BlockSpec((1, d), const)],
        out_specs=[pl.BlockSpec((tm, d), lambda i: (jnp.minimum(i, npt - 1), 0)),
                   pl.BlockSpec((tm, d), lambda i: (jnp.maximum(i - npt, 0), 0))],
        out_shape=[jax.ShapeDtypeStruct((n_prompt_rows, d), F32),
                   jax.ShapeDtypeStruct((t - n_prompt_rows, d), F32)],
        scratch_shapes=[pltpu.VMEM((2, TOP_K, tm // COMBINE_GROUP, COMBINE_GROUP, d // 2), I32),
                        pltpu.VMEM((tm, d), F32),
                        pltpu.SemaphoreType.DMA((2,))],
        compiler_params=_params(1),
        name="moe_combine",
    )(dest_tiles, dest_tiles, x, shared, gate_t, ys, g.reshape(1, d), b.reshape(1, d))


def _tile_major(a, tm):
    k, t = a.shape
    return a.reshape(k, t // tm, tm).transpose(1, 0, 2).reshape(-1)


def _moe_ln(x, x_packed, router_w, router_bias, wg, wu, wd, swg, swu, swd, g, b, n_prompt_rows):
    t, d = x.shape
    w_t = router_w.T
    w_hi = w_t.astype(BF16)
    w_lo = (w_t - w_hi.astype(F32)).astype(BF16)
    tr = _divisor_tile(t, 512, LANES)
    idx, gate, rank, cnt = _router(x, w_hi, w_lo, router_bias, tr)

    counts = cnt[:, 0]
    padded = (counts + EXPERT_ROWS - 1) // EXPERT_ROWS * EXPERT_ROWS
    pad_end = jnp.cumsum(padded)
    pad_start = pad_end - padded
    experts = jnp.arange(N_EXPERTS, dtype=I32)
    onehot = idx[None] == experts[:, None, None]
    dest = jnp.sum(jnp.where(onehot, pad_start[:, None, None], 0), axis=0) + rank
    n_blocks = (t * TOP_K + N_EXPERTS * (EXPERT_ROWS - 1)) // EXPERT_ROWS + 1
    n_valid = (pad_end[-1] // EXPERT_ROWS).astype(I32)
    blk = jnp.minimum(jnp.arange(n_blocks, dtype=I32), n_valid - 1)
    block_expert = jnp.minimum(
        jnp.sum(pad_end[None, :] <= (blk * EXPERT_ROWS)[:, None], axis=1), N_EXPERTS - 1).astype(I32)
    later_used = (experts[None, :] > experts[:, None]) & (counts[None, :] > 0)
    next_used = jnp.min(jnp.where(later_used, experts[None, :], N_EXPERTS), axis=1)
    next_used = jnp.where(next_used == N_EXPERTS, -1, next_used)
    per_block = lambda table: jnp.sum(
        jnp.where(block_expert[:, None] == experts[None, :], table[None, :], 0), axis=1).astype(I32)
    next_expert = per_block(next_used)
    rows_used = per_block(pad_start + counts) - blk * EXPERT_ROWS
    short_block = (rows_used <= EXPERT_ROWS // 2).astype(I32)

    td = _divisor_tile(t, 256, SUBLANES)
    xs, shared = _dispatch(x_packed, _tile_major(dest, td), (pad_start + counts).astype(I32),
                           (padded - counts).astype(I32), swg.astype(BF16), swu.astype(BF16),
                           swd.astype(BF16), n_blocks * EXPERT_ROWS, td)
    ys = _experts(xs, wg, wu, wd, block_expert, next_expert, short_block, n_valid.reshape(1))
    tc = _divisor_tile(math.gcd(n_prompt_rows, t - n_prompt_rows), 128, SUBLANES)
    return _combine(x, shared, gate.T, _tile_major(dest, tc), ys, g, b, tc, n_prompt_rows)


def kernel(x_prompt, x_sample, mem_prompt, state_gla, cache_swa_k, cache_swa_v, cache_mem_k, cache_mem_v,
           w_in, gla_w_gate2, gla_b_gate, gla_norm_g, swa_sinks, w_mix_out, ln1_g, ln1_b,
           mem_wq, mem_wk, mem_wv, mem_wo, ln2_g, ln2_b,
           router_w, router_bias, exp_w_gate, exp_w_up, exp_w_down, sh_w_gate, sh_w_up, sh_w_down,
           ln3_g, ln3_b):
    nbp, seq, d = x_prompt.shape
    nbs, dec, _ = x_sample.shape
    tp, ts = nbp * seq, nbs * dec
    t = tp + ts
    mem_tokens = mem_prompt.shape[1]
    assert state_gla.shape[0] == DEPTH and seq % SWA_WINDOW == 0 and dec % SUBLANES == 0
    assert cache_swa_k.shape[2] == SWA_WINDOW and tp % dec == 0
    tm = _divisor_tile(math.gcd(tp, ts), 512, 16)
    l = 0

    xp = x_prompt.reshape(tp, d)
    xs = x_sample.reshape(ts, d)

    gate_src = GLA_QK + GLA_QK + GLA_VW + GLA_VW
    swa_rows = SWA_QW + 2 * SWA_KVW
    w_in_rt = _reorder_cast(
        w_in[l].T,
        [(0, 0, gate_src),
         (gate_src + GLA_GATE_RANK, COL_SQ, swa_rows),
         (gate_src, COL_GG, GLA_GATE_RANK)],
        IN_COLS_PADDED, _divisor_tile(IN_COLS_PADDED, 512, 16))
    z = _matmul_split(xp, xs, w_in_rt, tm, IN_COLS_PADDED // 2, F32)

    wa2_pad = jnp.concatenate(
        [gla_w_gate2[l], jnp.zeros((LANES - GLA_GATE_RANK, GLA_QK), F32)], axis=0)
    wa2_hi = wa2_pad.astype(BF16)
    wa2_lo = (wa2_pad - wa2_hi.astype(F32)).astype(BF16)
    wa2_stack = jnp.concatenate([wa2_hi, wa2_lo, wa2_hi], axis=0)
    zero_state = jnp.zeros((nbp, GLA_HEADS, GLA_DK, GLA_DV), F32)
    gla_rows = _divisor_tile(seq, GLA_STEP_ROWS, CHUNK)
    mix, st_p = _gla(z, wa2_stack, gla_b_gate[l], gla_norm_g[l], zero_state, None,
                     nb=nbp, n_chunks=seq // gla_rows, rows=gla_rows, chunk=CHUNK, row0=0)
    mix, st_s = _gla(z, wa2_stack, gla_b_gate[l], gla_norm_g[l], state_gla[l], mix,
                     nb=nbs, n_chunks=1, rows=dec, chunk=dec, row0=tp)
    nblk = seq // SWA_WINDOW
    prev_p = lambda col: (lambda b, i: (b * nblk + jnp.maximum(i - 1, 0), col))
    mix = _swa(z, swa_sinks[l], z, z, (prev_p(COL_SK // SWA_KVW), prev_p(COL_SV // SWA_KVW)), mix,
               nb=nbp, n_blocks=nblk, rows=SWA_WINDOW, row0=0, q_base=0, q_stride=SWA_WINDOW)
    ck = cache_swa_k[l].reshape(nbs * SWA_WINDOW, SWA_KVW)
    cv = cache_swa_v[l].reshape(nbs * SWA_WINDOW, SWA_KVW)
    prev_s = lambda b, i: (b, 0)
    mix = _swa(z, swa_sinks[l], ck, cv, (prev_s, prev_s), mix,
               nb=nbs, n_blocks=1, rows=dec, row0=tp, q_base=PAST_LEN, q_stride=0)
    h1 = _matmul_res_ln(mix, w_mix_out[l].astype(BF16), (xp, xs), ln1_g[l], ln1_b[l], tm)

    mem = mem_prompt.reshape(nbp * mem_tokens, d)
    tmem = _divisor_tile(nbp * mem_tokens, 512, 16)
    mk = _matmul(mem, mem_wk[l].astype(BF16), tmem, _divisor_tile(d, 1024, LANES), F32)
    mv = _matmul(mem, mem_wv[l].astype(BF16), tmem, _divisor_tile(d, 1024, LANES), F32)
    q = _matmul(h1, mem_wq[l].astype(BF16), tm, d, BF16)
    tq = _divisor_tile(seq, 512, 16)
    att = _mem_attn(q, mk, mv, None, nb=nbp, n_tiles=seq // tq, rows=tq, row0=0, mem_tokens=mem_tokens)
    att = _mem_attn(q, cache_mem_k[l].reshape(nbs * mem_tokens, d), cache_mem_v[l].reshape(nbs * mem_tokens, d),
                    att, nb=nbs, n_tiles=1, rows=dec, row0=tp, mem_tokens=mem_tokens)
    h2, h2_packed = _matmul_res_ln(att, mem_wo[l].astype(BF16), h1, ln2_g[l], ln2_b[l], tm, emit_packed=True)

    out_p, out_s = _moe_ln(h2, h2_packed, router_w[l], router_bias[l], exp_w_gate[l], exp_w_up[l],
                           exp_w_down[l], sh_w_gate[l], sh_w_up[l], sh_w_down[l], ln3_g[l], ln3_b[l], tp)

    y_prompt = out_p.reshape(nbp, seq, d)
    y_sample = out_s.reshape(nbs, dec, d)
    kv = z[:, COL_SK:COL_SK + 2 * SWA_KVW]
    kvp = kv[:tp].reshape(nbp, seq, 2 * SWA_KVW)[:, seq - SWA_WINDOW:]
    kvs = kv[tp:].reshape(nbs, dec, 2 * SWA_KVW)
    kv_shape = lambda a: a.reshape(a.shape[0], a.shape[1], SWA_KV_HEADS, SWA_HEAD_DIM)[None]
    mem_shape = lambda a: a.reshape(nbp, mem_tokens, MEM_HEADS, d // MEM_HEADS)[None]
    return (y_prompt, y_sample,
            st_p[None],
            kv_shape(kvp[..., :SWA_KVW]), kv_shape(kvp[..., SWA_KVW:]),
            mem_shape(mk), mem_shape(mv),
            st_s[None],
            kv_shape(kvs[..., :SWA_KVW]), kv_shape(kvs[..., SWA_KVW:]))
```

```python
import functools
import math

import jax
import jax.numpy as jnp
from jax import lax
from jax.experimental import pallas as pl
from jax.experimental.pallas import tpu as pltpu

F32 = jnp.float32
BF16 = jnp.bfloat16
I32 = jnp.int32

CHUNK = 64
PAST_LEN = 2048
GLA_HEADS = 8
GLA_DK = 64
GLA_DV = 128
GLA_QK = GLA_HEADS * GLA_DK
GLA_VW = GLA_HEADS * GLA_DV
GLA_GATE_RANK = 16
GLA_TAU = 16.0
SWA_HEADS = 16
SWA_KV_HEADS = 2
SWA_GROUP = SWA_HEADS // SWA_KV_HEADS
SWA_HEAD_DIM = 64
SWA_QW = SWA_HEADS * SWA_HEAD_DIM
SWA_KVW = SWA_KV_HEADS * SWA_HEAD_DIM
SWA_WINDOW = 128
WINDOW_CHUNKS = SWA_WINDOW // CHUNK
MEM_HEADS = 4
N_EXPERTS = 64
TOP_K = 8
N_GROUPS = 8
GROUP_SIZE = N_EXPERTS // N_GROUPS
TOPK_GROUPS = 4
ROUTED_SCALE = 2.5
DEPTH = 1
DEEPNORM_ALPHA = (2 * DEPTH) ** 0.25
LN_EPS = 1e-5
RMS_EPS = 1e-6

LANES = 128
SUBLANES = 8
VMEM_LIMIT_BYTES = 56 * 1024 * 1024

COL_GQ = 0
COL_GK = COL_GQ + GLA_QK
COL_GV = COL_GK + GLA_QK
COL_GR = COL_GV + GLA_VW
COL_SQ = COL_GR + GLA_VW
COL_SK = COL_SQ + SWA_QW
COL_SV = COL_SK + SWA_KVW
COL_GG = COL_SV + SWA_KVW
MXU_COLS = 256
IN_COLS_PADDED = -(-(COL_GG + LANES) // (2 * MXU_COLS)) * (2 * MXU_COLS)

EXPERT_ROWS = 256
COMBINE_GROUP = SUBLANES
LN_SUB_ROWS = 128
GLA_STEP_ROWS = 256
GLA_SAMPLE_SEQS = 8
COPY_SLOTS = 3

NT_DIMS = (((1,), (1,)), ((), ()))
TN_DIMS = (((0,), (0,)), ((), ()))


def _params(n_axes):
    return pltpu.CompilerParams(dimension_semantics=("arbitrary",) * n_axes,
                                vmem_limit_bytes=VMEM_LIMIT_BYTES)


def _sigmoid(x):
    return 1.0 / (1.0 + jnp.exp(-x))


def _silu(x):
    return x * _sigmoid(x)


def _layer_norm(x, g, b):
    mu = jnp.mean(x, axis=-1, keepdims=True)
    xc = x - mu
    var = jnp.mean(xc * xc, axis=-1, keepdims=True)
    return xc * lax.rsqrt(var + LN_EPS) * g + b


HI_MASK = -65536


def _round_to_bf16_bits(x):
    b = lax.bitcast_convert_type(x, I32)
    return b + (0x7FFF + (lax.shift_right_logical(b, 16) & 1))


def _pack_halves(x):
    c = x.shape[1] // 2
    lo = lax.shift_right_logical(_round_to_bf16_bits(x[:, :c]), 16)
    hi = _round_to_bf16_bits(x[:, c:]) & HI_MASK
    return lo | hi


def _unpack_halves(w):
    return (lax.bitcast_convert_type(lax.shift_left(w, 16), F32),
            lax.bitcast_convert_type(w & HI_MASK, F32))


def _divisor_tile(n, pref, mult):
    t = min(pref, n)
    while t > mult and (n % t or t % mult):
        t -= mult
    assert n % t == 0 and t % mult == 0, (n, pref, mult)
    return t


def _mm_kernel(x_ref, w_ref, o_ref, xb_ref):
    @pl.when(pl.program_id(1) == 0)
    def _():
        xb_ref[...] = x_ref[...].astype(BF16)

    o_ref[...] = jnp.dot(xb_ref[...], w_ref[...], preferred_element_type=F32).astype(o_ref.dtype)


def _matmul(x, w, tm, tn, out_dtype):
    m, k = x.shape
    n = w.shape[1]
    return pl.pallas_call(
        _mm_kernel,
        grid=(m // tm, n // tn),
        in_specs=[pl.BlockSpec((tm, k), lambda i, j: (i, 0)),
                  pl.BlockSpec((k, tn), lambda i, j: (0, j))],
        out_specs=pl.BlockSpec((tm, tn), lambda i, j: (i, j)),
        out_shape=jax.ShapeDtypeStruct((m, n), out_dtype),
        scratch_shapes=[pltpu.VMEM((tm, k), BF16)],
        compiler_params=_params(2),
        name="matmul",
    )(x, w)


def _split_rows_specs(tm, k, n_first_tiles, n_grid_axes):
    if n_grid_axes == 1:
        return [pl.BlockSpec((tm, k), lambda i: (jnp.minimum(i, n_first_tiles - 1), 0)),
                pl.BlockSpec((tm, k), lambda i: (jnp.maximum(i - n_first_tiles, 0), 0))]
    return [pl.BlockSpec((tm, k), lambda j, i: (jnp.minimum(i, n_first_tiles - 1), 0)),
            pl.BlockSpec((tm, k), lambda j, i: (jnp.maximum(i - n_first_tiles, 0), 0))]


def _mm_split_kernel(xa_ref, xb_ref, wt_ref, o_ref, *, n_first_tiles):
    i = pl.program_id(1)

    def product(x_ref):
        o_ref[...] = lax.dot_general(x_ref[...].astype(BF16), wt_ref[...], NT_DIMS,
                                     preferred_element_type=F32).astype(o_ref.dtype)

    @pl.when(i < n_first_tiles)
    def _():
        product(xa_ref)

    @pl.when(i >= n_first_tiles)
    def _():
        product(xb_ref)


def _matmul_split(xa, xb, w_t, tm, tn, out_dtype):
    ma, k = xa.shape
    m = ma + xb.shape[0]
    n = w_t.shape[0]
    return pl.pallas_call(
        functools.partial(_mm_split_kernel, n_first_tiles=ma // tm),
        grid=(n // tn, m // tm),
        in_specs=_split_rows_specs(tm, k, ma // tm, 2) + [pl.BlockSpec((tn, k), lambda j, i: (j, 0))],
        out_specs=pl.BlockSpec((tm, tn), lambda j, i: (i, j)),
        out_shape=jax.ShapeDtypeStruct((m, n), out_dtype),
        compiler_params=_params(2),
        name="matmul_split",
    )(xa, xb, w_t)


def _reorder_cast_kernel(src_hbm, dst_hbm, stage_ref, out_ref, in_sems, out_sems, *, chunk_pieces, chunk):
    n = len(chunk_pieces)
    slots = stage_ref.shape[0]

    def in_copies(c):
        return [pltpu.make_async_copy(src_hbm.at[pl.ds(s0, ln)], stage_ref.at[c % slots, pl.ds(o0, ln)],
                                      in_sems.at[c % slots]) for s0, o0, ln in chunk_pieces[c]]

    def out_copy(c):
        return pltpu.make_async_copy(out_ref.at[c % slots], dst_hbm.at[pl.ds(c * chunk, chunk)],
                                     out_sems.at[c % slots])

    for c in range(min(slots - 1, n)):
        for cp in in_copies(c):
            cp.start()
    for c in range(n):
        if c + slots - 1 < n:
            for cp in in_copies(c + slots - 1):
                cp.start()
        for cp in in_copies(c):
            cp.wait()
        if c >= slots:
            out_copy(c - slots).wait()
        covered = sum(ln for _, _, ln in chunk_pieces[c])
        if covered < chunk:
            stage_ref[c % slots, covered:, :] = jnp.zeros((chunk - covered, stage_ref.shape[-1]),
                                                          stage_ref.dtype)
        out_ref[c % slots] = stage_ref[c % slots].astype(out_ref.dtype)
        out_copy(c).start()
    for c in range(max(n - slots, 0), n):
        out_copy(c).wait()


def _reorder_cast(src, runs, n_out_rows, chunk):
    k = src.shape[1]
    chunk_pieces = []
    for c0 in range(0, n_out_rows, chunk):
        pieces = []
        for s0, d0, ln in runs:
            lo, hi = max(d0, c0), min(d0 + ln, c0 + chunk)
            if lo < hi:
                assert (lo - c0) % 16 == 0 and (s0 + lo - d0) % SUBLANES == 0 and (hi - lo) % 16 == 0
                pieces.append((s0 + lo - d0, lo - c0, hi - lo))
        chunk_pieces.append(pieces)
    hbm = pl.BlockSpec(memory_space=pl.ANY)
    return pl.pallas_call(
        functools.partial(_reorder_cast_kernel, chunk_pieces=chunk_pieces, chunk=chunk),
        in_specs=[hbm],
        out_specs=hbm,
        out_shape=jax.ShapeDtypeStruct((n_out_rows, k), BF16),
        scratch_shapes=[pltpu.VMEM((COPY_SLOTS, chunk, k), src.dtype), pltpu.VMEM((COPY_SLOTS, chunk, k), BF16),
                        pltpu.SemaphoreType.DMA((COPY_SLOTS,)), pltpu.SemaphoreType.DMA((COPY_SLOTS,))],
        compiler_params=pltpu.CompilerParams(vmem_limit_bytes=VMEM_LIMIT_BYTES),
        name="reorder_cast",
    )(src)


def _mm_res_ln_kernel(x_ref, w_ref, *refs, n_res, n_first_tiles):
    res_refs, (g_ref, b_ref, o_ref, *packed_ref) = refs[:n_res], refs[n_res:]
    tm = x_ref.shape[0]
    sub = min(tm, LN_SUB_ROWS)
    use_first = pl.program_id(0) < n_first_tiles
    for r in range(0, tm, sub):
        rows = slice(r, r + sub)
        y = jnp.dot(x_ref[rows, :], w_ref[...], preferred_element_type=F32)
        res = res_refs[0][rows, :]
        if n_res == 2:
            res = jnp.where(use_first, res, res_refs[1][rows, :])
        h = _layer_norm(DEEPNORM_ALPHA * res + y, g_ref[...], b_ref[...])
        o_ref[rows, :] = h
        if packed_ref:
            groups = slice(r // SUBLANES, (r + sub) // SUBLANES)
            packed_ref[0][groups] = _pack_halves(h).reshape(sub // SUBLANES, SUBLANES, h.shape[1] // 2)


def _matmul_res_ln(x, w, res, g, b, tm, emit_packed=False):
    m, k = x.shape
    n = w.shape[1]
    if isinstance(res, tuple):
        n_first_tiles = res[0].shape[0] // tm
        res_specs = _split_rows_specs(tm, n, n_first_tiles, 1)
    else:
        n_first_tiles = 0
        res = (res,)
        res_specs = [pl.BlockSpec((tm, n), lambda i: (i, 0))]
    out_specs = [pl.BlockSpec((tm, n), lambda i: (i, 0))]
    out_shape = [jax.ShapeDtypeStruct((m, n), F32)]
    if emit_packed:
        out_specs.append(pl.BlockSpec((tm // SUBLANES, SUBLANES, n // 2), lambda i: (i, 0, 0)))
        out_shape.append(jax.ShapeDtypeStruct((m // SUBLANES, SUBLANES, n // 2), I32))
    outs = pl.pallas_call(
        functools.partial(_mm_res_ln_kernel, n_res=len(res), n_first_tiles=n_first_tiles),
        grid=(m // tm,),
        in_specs=[pl.BlockSpec((tm, k), lambda i: (i, 0)),
                  pl.BlockSpec((k, n), lambda i: (0, 0))] + res_specs + [
                  pl.BlockSpec((1, n), lambda i: (0, 0)),
                  pl.BlockSpec((1, n), lambda i: (0, 0))],
        out_specs=out_specs,
        out_shape=out_shape,
        compiler_params=_params(1),
        name="matmul_res_ln",
    )(x, w, *res, g.reshape(1, n), b.reshape(1, n))
    return outs if emit_packed else outs[0]


def _split3_bf16(x):
    hi = x.astype(BF16)
    r1 = x - hi.astype(F32)
    mid = r1.astype(BF16)
    lo = (r1 - mid.astype(F32)).astype(BF16)
    return jnp.concatenate([hi, mid, lo], axis=-1)


def _gla_kernel(q_ref, k_ref, v_ref, r_ref, gg_ref, wa2_ref, ba_ref, ng_ref, s0_ref, mix_in_ref,
                o_ref, sfin_ref, st_ref, *, rows, chunk, chained):
    del mix_in_ref
    n_sub = rows // chunk
    shift = chunk.bit_length() - 1
    assert chunk == 1 << shift

    if chained:
        @pl.when(pl.program_id(1) == 0)
        def _():
            for h in range(GLA_HEADS):
                st_ref[h] = s0_ref[0, h].T

    gg = gg_ref[...]
    gg_hi = gg.astype(BF16)
    gg_lo = (gg - gg_hi.astype(F32)).astype(BF16)
    gate = jnp.dot(jnp.concatenate([gg_hi, gg_hi, gg_lo], axis=-1), wa2_ref[...],
                   preferred_element_type=F32) + ba_ref[...]
    log_a = (jnp.minimum(gate, 0.0) - jnp.log1p(jnp.exp(-jnp.abs(gate)))) / GLA_TAU

    ri = lax.broadcasted_iota(I32, (rows, rows), 0)
    ci = lax.broadcasted_iota(I32, (rows, rows), 1)
    same_chunk = lax.shift_right_logical(ri, shift) == lax.shift_right_logical(ci, shift)
    causal = same_chunk & (ri >= ci)
    ones = jnp.concatenate([jnp.where(causal, 1.0, 0.0), jnp.where(same_chunk, 1.0, 0.0)], axis=0).astype(BF16)
    sums = jnp.dot(ones, _split3_bf16(log_a), preferred_element_type=F32)
    sums = sums[:, :GLA_QK] + sums[:, GLA_QK:2 * GLA_QK] + sums[:, 2 * GLA_QK:]
    bcum = sums[:rows]
    b_last = sums[rows:]
    q = q_ref[...] * (GLA_DK ** -0.5)
    k = k_ref[...]
    q_in = (q * jnp.exp(bcum)).astype(BF16)
    k_in = (k * jnp.exp(-bcum)).astype(BF16)
    k_out = (k * jnp.exp(b_last - bcum)).astype(BF16)
    decay = jnp.exp(b_last)
    ng = ng_ref[...]
    blk_r = lax.shift_right_logical(lax.broadcasted_iota(I32, (rows, n_sub * GLA_DK), 0), shift)
    blk_c = lax.shift_right_logical(lax.broadcasted_iota(I32, (rows, n_sub * GLA_DK), 1),
                                    GLA_DK.bit_length() - 1)
    for h in range(GLA_HEADS):
        ks = slice(h * GLA_DK, (h + 1) * GLA_DK)
        vs = slice(h * GLA_DV, (h + 1) * GLA_DV)
        vh = v_ref[:, vs].astype(BF16)
        qh = q_in[:, ks]
        attn = lax.dot_general(qh, k_in[:, ks], NT_DIMS, preferred_element_type=F32)
        attn = jnp.where(causal, attn, 0.0).astype(BF16)
        o = jnp.dot(attn, vh, preferred_element_type=F32)
        ko = k_out[:, ks]
        if n_sub > 1:
            ko = jnp.where(blk_r == blk_c, jnp.concatenate([ko] * n_sub, axis=-1), 0.0)
        upd_t = lax.dot_general(vh, ko, TN_DIMS, preferred_element_type=F32)
        states = []
        if chained:
            st = st_ref[h]
        for c in range(n_sub):
            if not chained:
                st = s0_ref[c, h].T
            states.append(st.astype(BF16))
            st = st * decay[c * chunk:c * chunk + 1, ks] + upd_t[:, c * GLA_DK:(c + 1) * GLA_DK]
            if not chained:
                sfin_ref[c, h] = st.T
        if chained:
            st_ref[h] = st
        o_all = lax.dot_general(qh, jnp.concatenate(states, axis=0), NT_DIMS, preferred_element_type=F32)
        o = o + jnp.concatenate(
            [o_all[c * chunk:(c + 1) * chunk, c * GLA_DV:(c + 1) * GLA_DV] for c in range(n_sub)], axis=0)
        o = o * lax.rsqrt(jnp.mean(o * o, axis=-1, keepdims=True) + RMS_EPS) * ng
        o_ref[:, vs] = (o * _silu(r_ref[:, vs])).astype(o_ref.dtype)
    if chained:
        @pl.when(pl.program_id(1) == pl.num_programs(1) - 1)
        def _():
            for h in range(GLA_HEADS):
                sfin_ref[0, h] = st_ref[h].T


def _gla(z, wa2_stack, ba, norm_g, s0, mix_in, *, nb, n_chunks, rows, chunk, row0, chained):
    t = z.shape[0]
    rb0 = row0 // rows
    seq_per_step = 1 if chained else rows // chunk
    assert chained or n_chunks == 1
    rmap = lambda colblk: (lambda b, c: (rb0 + b * n_chunks + c, colblk))
    in_specs = [
        pl.BlockSpec((rows, GLA_QK), rmap(COL_GQ // GLA_QK)),
        pl.BlockSpec((rows, GLA_QK), rmap(COL_GK // GLA_QK)),
        pl.BlockSpec((rows, GLA_VW), rmap(COL_GV // GLA_VW)),
        pl.BlockSpec((rows, GLA_VW), rmap(COL_GR // GLA_VW)),
        pl.BlockSpec((rows, LANES), rmap(COL_GG // LANES)),
        pl.BlockSpec((3 * LANES, GLA_QK), lambda b, c: (0, 0)),
        pl.BlockSpec((1, GLA_QK), lambda b, c: (0, 0)),
        pl.BlockSpec((1, GLA_DV), lambda b, c: (0, 0)),
        pl.BlockSpec((seq_per_step, GLA_HEADS, GLA_DK, GLA_DV), lambda b, c: (b, 0, 0, 0)),
    ]
    args = [z, z, z, z, z, wa2_stack, ba.reshape(1, GLA_QK), norm_g.reshape(1, GLA_DV), s0]
    aliases = {}
    if mix_in is None:
        mix_in = jnp.zeros((SUBLANES, LANES), BF16)
        in_specs.append(pl.BlockSpec(memory_space=pl.ANY))
    else:
        in_specs.append(pl.BlockSpec(memory_space=pl.ANY))
        aliases = {len(args): 0}
    args.append(mix_in)
    mix, s_fin = pl.pallas_call(
        functools.partial(_gla_kernel, rows=rows, chunk=chunk, chained=chained),
        grid=(nb, n_chunks),
        in_specs=in_specs,
        out_specs=[pl.BlockSpec((rows, GLA_VW), rmap(0)),
                   pl.BlockSpec((seq_per_step, GLA_HEADS, GLA_DK, GLA_DV), lambda b, c: (b, 0, 0, 0))],
        out_shape=[jax.ShapeDtypeStruct((t, GLA_VW + SWA_QW), BF16),
                   jax.ShapeDtypeStruct((nb * seq_per_step, GLA_HEADS, GLA_DK, GLA_DV), F32)],
        scratch_shapes=[pltpu.VMEM((GLA_HEADS, GLA_DV, GLA_DK), F32)],
        input_output_aliases=aliases,
        compiler_params=_params(2),
        name="gla",
    )(*args)
    return mix, s_fin


def _swa_kernel(sinks_ref, q_ref, kc_ref, vc_ref, kp_ref, vp_ref, mix_in_ref, o_ref, bias_ref, *,
                rows, q_base, q_stride):
    del mix_in_ref
    i = pl.program_id(1)
    n_tables = bias_ref.shape[0]
    nk = SWA_WINDOW + rows
    sr = SWA_GROUP * rows
    assert rows & (rows - 1) == 0 and q_base % CHUNK == 0 and q_stride % CHUNK == 0
    assert n_tables == 1 or q_base + q_stride >= SWA_WINDOW
    head_in_group = lax.shift_right_logical(lax.broadcasted_iota(I32, (sr, 1), 0), rows.bit_length() - 1)

    @pl.when(jnp.logical_and(pl.program_id(0) == 0, i == 0))
    def _():
        row = lax.broadcasted_iota(I32, (sr, nk), 0)
        col = lax.broadcasted_iota(I32, (sr, nk), 1)
        for t in range(n_tables):
            q0 = q_base + t * q_stride
            qpos = q0 + (row & (rows - 1))
            kpos = q0 - SWA_WINDOW + col
            dist = jnp.abs(qpos - kpos).astype(F32)
            qc = lax.shift_right_arithmetic(qpos, CHUNK.bit_length() - 1)
            kc = lax.shift_right_arithmetic(kpos, CHUNK.bit_length() - 1)
            allowed = (kpos >= 0) & (kc <= qc) & (kc >= qc - WINDOW_CHUNKS)
            for g in range(SWA_KV_HEADS):
                slope = jnp.zeros((sr, 1), F32)
                for j in range(SWA_GROUP):
                    h = g * SWA_GROUP + j
                    slope = jnp.where(head_in_group == j, 2.0 ** (-8.0 * (h + 1) / SWA_HEADS), slope)
                bias_ref[t, g] = jnp.where(allowed, -(slope * dist), -jnp.inf)

    table = jnp.minimum(i, n_tables - 1)
    keys = jnp.concatenate([kp_ref[...], kc_ref[...]], axis=0).astype(BF16)
    vals = jnp.concatenate([vp_ref[...], vc_ref[...]], axis=0).astype(BF16)
    for g in range(SWA_KV_HEADS):
        sink = jnp.zeros((sr, 1), F32)
        for j in range(SWA_GROUP):
            sink = jnp.where(head_in_group == j, sinks_ref[g * SWA_GROUP + j], sink)
        kv = slice(g * SWA_HEAD_DIM, (g + 1) * SWA_HEAD_DIM)
        qg = jnp.concatenate(
            [q_ref[:, (g * SWA_GROUP + j) * SWA_HEAD_DIM:(g * SWA_GROUP + j + 1) * SWA_HEAD_DIM]
             for j in range(SWA_GROUP)], axis=0).astype(BF16)
        s = lax.dot_general(qg, keys[:, kv], NT_DIMS, preferred_element_type=F32)
        s = s * (SWA_HEAD_DIM ** -0.5) + bias_ref[table, g]
        m = jnp.maximum(jnp.max(s, axis=-1, keepdims=True), sink)
        p = jnp.exp(s - m)
        denom = jnp.sum(p, axis=-1, keepdims=True) + jnp.exp(sink - m)
        o = jnp.dot(p.astype(BF16), vals[:, kv], preferred_element_type=F32) / denom
        for j in range(0, SWA_GROUP, 2):
            c0 = (g * SWA_GROUP + j) * SWA_HEAD_DIM
            o_ref[:, c0:c0 + LANES] = jnp.concatenate(
                [o[j * rows:(j + 1) * rows], o[(j + 1) * rows:(j + 2) * rows]], axis=-1).astype(o_ref.dtype)


def _swa(z, sinks, k_prev, v_prev, prev_map, mix_in, *, nb, n_blocks, rows, row0, q_base, q_stride):
    rb0 = row0 // rows
    rmap = lambda colblk: (lambda b, i: (rb0 + b * n_blocks + i, colblk))
    return pl.pallas_call(
        functools.partial(_swa_kernel, rows=rows, q_base=q_base, q_stride=q_stride),
        grid=(nb, n_blocks),
        in_specs=[pl.BlockSpec(memory_space=pltpu.SMEM),
                  pl.BlockSpec((rows, SWA_QW), rmap(COL_SQ // SWA_QW)),
                  pl.BlockSpec((rows, SWA_KVW), rmap(COL_SK // SWA_KVW)),
                  pl.BlockSpec((rows, SWA_KVW), rmap(COL_SV // SWA_KVW)),
                  pl.BlockSpec((SWA_WINDOW, SWA_KVW), prev_map[0]),
                  pl.BlockSpec((SWA_WINDOW, SWA_KVW), prev_map[1]),
                  pl.BlockSpec(memory_space=pl.ANY)],
        out_specs=pl.BlockSpec((rows, SWA_QW), rmap(GLA_VW // SWA_QW)),
        out_shape=jax.ShapeDtypeStruct(mix_in.shape, mix_in.dtype),
        scratch_shapes=[pltpu.VMEM((min(2, n_blocks), SWA_KV_HEADS, SWA_GROUP * rows, SWA_WINDOW + rows), F32)],
        input_output_aliases={6: 0},
        compiler_params=_params(2),
        name="swa",
    )(sinks, z, z, z, k_prev, v_prev, mix_in)


def _mem_attn_kernel(q_ref, k_ref, v_ref, o_in_ref, o_ref, *, head_dim):
    del o_in_ref
    for h in range(MEM_HEADS):
        hs = slice(h * head_dim, (h + 1) * head_dim)
        kh = k_ref[:, hs].astype(BF16)
        vh = v_ref[:, hs].astype(BF16)
        s = lax.dot_general(q_ref[:, hs], kh, NT_DIMS, preferred_element_type=F32) * (head_dim ** -0.5)
        m = jnp.max(s, axis=-1, keepdims=True)
        p = jnp.exp(s - m)
        denom = jnp.sum(p, axis=-1, keepdims=True)
        o = jnp.dot(p.astype(BF16), vh, preferred_element_type=F32) / denom
        o_ref[:, hs] = o.astype(o_ref.dtype)


def _mem_attn(q, mk, mv, o_in, *, nb, n_tiles, rows, row0, mem_tokens):
    t, d = q.shape
    rb0 = row0 // rows
    qmap = lambda b, i: (rb0 + b * n_tiles + i, 0)
    in_specs = [pl.BlockSpec((rows, d), qmap),
                pl.BlockSpec((mem_tokens, d), lambda b, i: (b, 0)),
                pl.BlockSpec((mem_tokens, d), lambda b, i: (b, 0)),
                pl.BlockSpec(memory_space=pl.ANY)]
    aliases = {}
    if o_in is None:
        o_in = jnp.zeros((SUBLANES, LANES), BF16)
    else:
        aliases = {3: 0}
    return pl.pallas_call(
        functools.partial(_mem_attn_kernel, head_dim=d // MEM_HEADS),
        grid=(nb, n_tiles),
        in_specs=in_specs,
        out_specs=pl.BlockSpec((rows, d), qmap),
        out_shape=jax.ShapeDtypeStruct((t, d), BF16),
        input_output_aliases=aliases,
        compiler_params=_params(2),
        name="mem_attn",
    )(q, mk, mv, o_in)


def _pick_first_max(cur, iota, sentinel):
    mx = jnp.max(cur, axis=0, keepdims=True)
    first = jnp.min(jnp.where(cur == mx, iota, sentinel), axis=0, keepdims=True)
    return iota == first, first


def _router_kernel(x_ref, wh_ref, wl_ref, bias_ref, idx_ref, gate_ref, rank_ref, cnt_ref, run_ref, *, tm):
    i = pl.program_id(0)

    @pl.when(i == 0)
    def _():
        run_ref[...] = jnp.zeros_like(run_ref)

    x = x_ref[...]
    xh = x.astype(BF16)
    xl = (x - xh.astype(F32)).astype(BF16)
    wh = wh_ref[...]
    logits = (lax.dot_general(wh, xh, NT_DIMS, preferred_element_type=F32)
              + lax.dot_general(wh, xl, NT_DIMS, preferred_element_type=F32)
              + lax.dot_general(wl_ref[...], xh, NT_DIMS, preferred_element_type=F32))
    scores = _sigmoid(logits)
    sel = scores + bias_ref[...]
    neg_inf = -jnp.inf

    li = lax.broadcasted_iota(I32, (GROUP_SIZE, tm), 0)
    grp_rows = []
    for g in range(N_GROUPS):
        blk = sel[g * GROUP_SIZE:(g + 1) * GROUP_SIZE, :]
        pick, _ = _pick_first_max(blk, li, GROUP_SIZE)
        m1 = jnp.max(blk, axis=0, keepdims=True)
        m2 = jnp.max(jnp.where(pick, neg_inf, blk), axis=0, keepdims=True)
        grp_rows.append(m1 + m2)
    grp = jnp.concatenate(grp_rows, axis=0)

    gi = lax.broadcasted_iota(I32, (N_GROUPS, tm), 0)
    gsel = jnp.zeros((N_GROUPS, tm), F32)
    cur = grp
    for _ in range(TOPK_GROUPS):
        pick, _ = _pick_first_max(cur, gi, N_GROUPS)
        gsel = jnp.where(pick, 1.0, gsel)
        cur = jnp.where(pick, neg_inf, cur)
    emask = jnp.concatenate(
        [jnp.broadcast_to(gsel[g:g + 1, :], (GROUP_SIZE, tm)) for g in range(N_GROUPS)], axis=0)

    ei = lax.broadcasted_iota(I32, (N_EXPERTS, tm), 0)
    cur = jnp.where(emask > 0.5, sel, neg_inf)
    chosen = jnp.zeros((N_EXPERTS, tm), F32)
    idx_rows, w_rows = [], []
    for _ in range(TOP_K):
        pick, first = _pick_first_max(cur, ei, N_EXPERTS)
        idx_rows.append(first)
        w_rows.append(jnp.sum(jnp.where(pick, scores, 0.0), axis=0, keepdims=True))
        chosen = jnp.where(pick, 1.0, chosen)
        cur = jnp.where(pick, neg_inf, cur)
    idx = jnp.concatenate(idx_rows, axis=0)
    w = jnp.concatenate(w_rows, axis=0)
    gate_ref[...] = w / jnp.sum(w, axis=0, keepdims=True) * ROUTED_SCALE
    idx_ref[...] = idx

    ti = lax.broadcasted_iota(I32, (tm, tm), 0)
    tj = lax.broadcasted_iota(I32, (tm, tm), 1)
    before = jnp.where(ti < tj, 1.0, 0.0).astype(BF16)
    local = jnp.dot(chosen.astype(BF16), before, preferred_element_type=F32)
    total = local + run_ref[:, 0:1]
    rank_rows = [jnp.sum(jnp.where(ei == idx_rows[k], total, 0.0), axis=0, keepdims=True)
                 for k in range(TOP_K)]
    rank_ref[...] = jnp.concatenate(rank_rows, axis=0).astype(I32)
    run_ref[...] = run_ref[...] + jnp.sum(chosen, axis=1, keepdims=True)
    cnt_ref[...] = run_ref[...].astype(I32)


def _router(x, w_hi_t, w_lo_t, bias, tm):
    t, d = x.shape
    return pl.pallas_call(
        functools.partial(_router_kernel, tm=tm),
        grid=(t // tm,),
        in_specs=[pl.BlockSpec((tm, d), lambda i: (i, 0)),
                  pl.BlockSpec((N_EXPERTS, d), lambda i: (0, 0)),
                  pl.BlockSpec((N_EXPERTS, d), lambda i: (0, 0)),
                  pl.BlockSpec((N_EXPERTS, 1), lambda i: (0, 0))],
        out_specs=[pl.BlockSpec((TOP_K, tm), lambda i: (0, i)),
                   pl.BlockSpec((TOP_K, tm), lambda i: (0, i)),
                   pl.BlockSpec((TOP_K, tm), lambda i: (0, i)),
                   pl.BlockSpec((N_EXPERTS, LANES), lambda i: (0, 0))],
        out_shape=[jax.ShapeDtypeStruct((TOP_K, t), I32),
                   jax.ShapeDtypeStruct((TOP_K, t), F32),
                   jax.ShapeDtypeStruct((TOP_K, t), I32),
                   jax.ShapeDtypeStruct((N_EXPERTS, LANES), I32)],
        scratch_shapes=[pltpu.VMEM((N_EXPERTS, LANES), F32)],
        compiler_params=_params(1),
        name="router",
    )(x, w_hi_t, w_lo_t, bias.reshape(N_EXPERTS, 1))


PAD_BITS = tuple(1 << s for s in reversed(range(EXPERT_ROWS.bit_length() - 1)))


def _dispatch_kernel(zstart_ref, zcount_ref, dest_ref, x_ref, swg_ref, swu_ref, swd_ref, xs_ref, shared_ref,
                     zero_ref, sem, zsem, *, tm):
    i = pl.program_id(0)

    def start_group(g, carry):
        for j in range(SUBLANES):
            for k in range(TOP_K):
                d = dest_ref[k * tm + g * SUBLANES + j]
                pltpu.make_async_copy(x_ref.at[g, pl.ds(j, 1), :], xs_ref.at[d], sem).start(priority=k % 2)
        return carry

    lax.fori_loop(0, tm // SUBLANES, start_group, 0)

    half = x_ref.shape[-1]
    x_lo, x_hi = _unpack_halves(x_ref[...].reshape(tm, half))
    xb = jnp.concatenate([x_lo.astype(BF16), x_hi.astype(BF16)], axis=-1)
    hg = jnp.dot(xb, swg_ref[...], preferred_element_type=F32)
    hu = jnp.dot(xb, swu_ref[...], preferred_element_type=F32)
    shared_ref[...] = jnp.dot((_silu(hg) * hu).astype(BF16), swd_ref[...],
                              preferred_element_type=F32).astype(shared_ref.dtype)

    for k in range(TOP_K):
        pltpu.make_async_copy(xs_ref.at[pl.ds(0, tm)], xs_ref.at[pl.ds(0, tm)], sem).wait()

    @pl.when(i == pl.num_programs(0) - 1)
    def _():
        zero_ref[...] = jnp.zeros_like(zero_ref)

        def fill(e, wait):
            pos = zstart_ref[e]
            cnt = zcount_ref[e]
            for bit in PAD_BITS:
                has = (cnt & bit) != 0

                @pl.when(has)
                def _():
                    cp = pltpu.make_async_copy(zero_ref.at[pl.ds(0, bit)], xs_ref.at[pl.ds(pos, bit)], zsem)
                    if wait:
                        cp.wait()
                    else:
                        cp.start()

                pos = pos + jnp.where(has, bit, 0)

        def fill_start(e, carry):
            fill(e, False)
            return carry

        def fill_wait(e, carry):
            fill(e, True)
            return carry

        lax.fori_loop(0, N_EXPERTS, fill_start, 0)
        lax.fori_loop(0, N_EXPERTS, fill_wait, 0)


def _dispatch(x, dest_tiles, zstart, zcount, swg, swu, swd, n_rows, tm):
    t = x.shape[0] * SUBLANES
    d = x.shape[2]
    ds = swg.shape[1]
    const = lambda i, zs, zc: (0, 0)
    return pl.pallas_call(
        functools.partial(_dispatch_kernel, tm=tm),
        grid_spec=pltpu.PrefetchScalarGridSpec(
            num_scalar_prefetch=2,
            grid=(t // tm,),
            in_specs=[pl.BlockSpec((TOP_K * tm,), lambda i, zs, zc: (i,), memory_space=pltpu.SMEM),
                      pl.BlockSpec((tm // SUBLANES, SUBLANES, d), lambda i, zs, zc: (i, 0, 0)),
                      pl.BlockSpec((2 * d, ds), const),
                      pl.BlockSpec((2 * d, ds), const),
                      pl.BlockSpec((ds, 2 * d), const)],
            out_specs=[pl.BlockSpec(memory_space=pl.ANY),
                       pl.BlockSpec((tm, 2 * d), lambda i, zs, zc: (i, 0))],
            scratch_shapes=[pltpu.VMEM((EXPERT_ROWS // 2, 1, d), x.dtype),
                            pltpu.SemaphoreType.DMA(()),
                            pltpu.SemaphoreType.DMA(())]),
        out_shape=[jax.ShapeDtypeStruct((n_rows, 1, d), x.dtype),
                   jax.ShapeDtypeStruct((t, 2 * d), BF16)],
        compiler_params=_params(1),
        name="moe_dispatch",
    )(zstart, zcount, dest_tiles, x, swg, swu, swd)


def _expert_kernel(be_ref, nxt_ref, short_ref, nv_ref, xs_hbm, wg_hbm, wu_hbm, wd_hbm, ys_hbm,
                   wgs_ref, wus_ref, wds_ref, wgb_ref, wub_ref, wdb_ref, xbuf_ref, ybuf_ref,
                   wsems, xsems, ysems):
    b = pl.program_id(0)
    n_valid = nv_ref[0]
    e = be_ref[b]
    prev = be_ref[jnp.maximum(b - 1, 0)]
    fresh = jnp.logical_or(b == 0, e != prev)
    slot = lax.rem(b, 2)

    def weight_copies(expert):
        return (pltpu.make_async_copy(wg_hbm.at[expert], wgs_ref, wsems.at[0]),
                pltpu.make_async_copy(wu_hbm.at[expert], wus_ref, wsems.at[1]),
                pltpu.make_async_copy(wd_hbm.at[expert], wds_ref, wsems.at[2]))

    def rows_in(blk, to_slot):
        r0 = pl.multiple_of(blk * EXPERT_ROWS, EXPERT_ROWS)
        return pltpu.make_async_copy(xs_hbm.at[pl.ds(r0, EXPERT_ROWS), 0, :], xbuf_ref.at[to_slot],
                                     xsems.at[to_slot])

    def rows_out(blk, from_slot):
        r0 = pl.multiple_of(blk * EXPERT_ROWS, EXPERT_ROWS)
        return pltpu.make_async_copy(ybuf_ref.at[from_slot], ys_hbm.at[pl.ds(r0, EXPERT_ROWS), 0, :],
                                     ysems.at[from_slot])

    @pl.when(b == 0)
    def _():
        for cp in weight_copies(e):
            cp.start(priority=1)
        rows_in(b, slot).start()

    @pl.when(fresh)
    def _():
        for cp in weight_copies(e):
            cp.wait()
        wgb_ref[...] = wgs_ref[...].astype(BF16)
        wub_ref[...] = wus_ref[...].astype(BF16)
        wdb_ref[...] = wds_ref[...].astype(BF16)
        nxt = nxt_ref[b]

        @pl.when(nxt >= 0)
        def _():
            for cp in weight_copies(nxt):
                cp.start(priority=1)

    @pl.when(b < n_valid)
    def _():
        rows_in(b, slot).wait()

        @pl.when(b + 1 < n_valid)
        def _():
            rows_in(b + 1, 1 - slot).start()

        @pl.when(b >= 2)
        def _():
            rows_out(b - 2, slot).wait()

        def swiglu_rows(rows):
            half = xbuf_ref.shape[-1]
            x_lo, x_hi = _unpack_halves(xbuf_ref[slot, :rows, :])
            xb = jnp.concatenate([x_lo.astype(BF16), x_hi.astype(BF16)], axis=-1)
            hg = jnp.dot(xb, wgb_ref[...], preferred_element_type=F32)
            hu = jnp.dot(xb, wub_ref[...], preferred_element_type=F32)
            act = (_silu(hg) * hu).astype(BF16)
            ybuf_ref[slot, :rows, :] = _pack_halves(jnp.dot(act, wdb_ref[...], preferred_element_type=F32))
            if rows < EXPERT_ROWS:
                ybuf_ref[slot, rows:, :] = jnp.zeros((EXPERT_ROWS - rows, half), ybuf_ref.dtype)

        pl.when(short_ref[b] == 1)(functools.partial(swiglu_rows, EXPERT_ROWS // 2))
        pl.when(short_ref[b] == 0)(functools.partial(swiglu_rows, EXPERT_ROWS))
        rows_out(b, slot).start()

        @pl.when(b == n_valid - 1)
        def _():
            rows_out(b, slot).wait()

            @pl.when(b >= 1)
            def _():
                rows_out(b - 1, 1 - slot).wait()


def _experts(xs, wg, wu, wd, block_expert, next_expert, short_block, n_valid):
    n_rows = xs.shape[0]
    d = wg.shape[1]
    de = wg.shape[2]
    nb = n_rows // EXPERT_ROWS
    hbm = pl.BlockSpec(memory_space=pl.ANY)
    return pl.pallas_call(
        _expert_kernel,
        grid_spec=pltpu.PrefetchScalarGridSpec(
            num_scalar_prefetch=4,
            grid=(nb,),
            in_specs=[hbm, hbm, hbm, hbm],
            out_specs=hbm,
            scratch_shapes=[pltpu.VMEM((d, de), F32), pltpu.VMEM((d, de), F32), pltpu.VMEM((de, d), F32),
                            pltpu.VMEM((d, de), BF16), pltpu.VMEM((d, de), BF16), pltpu.VMEM((de, d), BF16),
                            pltpu.VMEM((2, EXPERT_ROWS, d // 2), I32), pltpu.VMEM((2, EXPERT_ROWS, d // 2), I32),
                            pltpu.SemaphoreType.DMA((3,)), pltpu.SemaphoreType.DMA((2,)),
                            pltpu.SemaphoreType.DMA((2,))]),
        out_shape=jax.ShapeDtypeStruct((n_rows, 1, d // 2), I32),
        compiler_params=_params(1),
        name="moe_experts",
    )(block_expert, next_expert, short_block, n_valid, xs, wg, wu, wd)


def _combine_kernel(dest0_ref, destn_ref, x_ref, shared_ref, gate_ref, ys_ref, g_ref, b_ref,
                    op_ref, os_ref, buf_ref, acc_ref, sems, *, tm, n_prompt_tiles):
    i = pl.program_id(0)
    n = pl.num_programs(0)
    half = buf_ref.shape[-1]
    n_groups = tm // COMBINE_GROUP

    def issue_group(dest_ref, to_slot, g):
        for j in range(COMBINE_GROUP):
            t = g * COMBINE_GROUP + j
            for k in range(TOP_K):
                d = dest_ref[k * tm + t]
                pltpu.make_async_copy(ys_ref.at[d], buf_ref.at[to_slot, k, g, pl.ds(j, 1), :],
                                      sems.at[to_slot]).start(priority=k % 2)

    @pl.when(i == 0)
    def _():
        def first(g, carry):
            issue_group(dest0_ref, 0, g)
            return carry
        lax.fori_loop(0, n_groups, first, 0)

    def run_tile(slot):
        for k in range(TOP_K):
            pltpu.make_async_copy(ys_ref.at[pl.ds(0, tm)], ys_ref.at[pl.ds(0, tm)], sems.at[slot]).wait()

        def reduce_group(g):
            r0 = pl.multiple_of(g * COMBINE_GROUP, COMBINE_GROUP)
            gate = gate_ref[pl.ds(r0, COMBINE_GROUP), :]
            acc_lo = acc_hi = None
            for k in range(TOP_K):
                y_lo, y_hi = _unpack_halves(buf_ref[slot, k, g])
                gk = gate[:, k:k + 1]
                acc_lo = gk * y_lo if k == 0 else acc_lo + gk * y_lo
                acc_hi = gk * y_hi if k == 0 else acc_hi + gk * y_hi
            acc_ref[pl.ds(r0, COMBINE_GROUP), :half] = acc_lo
            acc_ref[pl.ds(r0, COMBINE_GROUP), half:] = acc_hi

        @pl.when(i + 1 < n)
        def _():
            def body(g, carry):
                issue_group(destn_ref, 1 - slot, g)
                reduce_group(g)
                return carry
            lax.fori_loop(0, n_groups, body, 0)

        @pl.when(i + 1 >= n)
        def _():
            def body(g, carry):
                reduce_group(g)
                return carry
            lax.fori_loop(0, n_groups, body, 0)

    for s in range(2):
        pl.when(lax.rem(i, 2) == s)(functools.partial(run_tile, s))

    x = x_ref[...]
    out = _layer_norm(DEEPNORM_ALPHA * x + (acc_ref[...] + shared_ref[...].astype(F32)),
                      g_ref[...], b_ref[...])

    @pl.when(i < n_prompt_tiles)
    def _():
        op_ref[...] = out

    @pl.when(i >= n_prompt_tiles)
    def _():
        os_ref[...] = out


def _combine(x, shared, gate_t, dest_tiles, ys, g, b, tm, n_prompt_rows):
    t, d = x.shape
    n_tiles = t // tm
    npt = n_prompt_rows // tm
    const = lambda i: (0, 0)
    return pl.pallas_call(
        functools.partial(_combine_kernel, tm=tm, n_prompt_tiles=npt),
        grid=(n_tiles,),
        in_specs=[pl.BlockSpec((TOP_K * tm,), lambda i: (0,), memory_space=pltpu.SMEM),
                  pl.BlockSpec((TOP_K * tm,), lambda i: (jnp.minimum(i + 1, n_tiles - 1),),
                               memory_space=pltpu.SMEM),
                  pl.BlockSpec((tm, d), lambda i: (i, 0)),
                  pl.BlockSpec((tm, d), lambda i: (i, 0)),
                  pl.BlockSpec((tm, TOP_K), lambda i: (i, 0)),
                  pl.BlockSpec(memory_space=pl.ANY),
                  pl.BlockSpec((1, d), const),
                  pl.BlockSpec((1, d), const)],
        out_specs=[pl.BlockSpec((tm, d), lambda i: (jnp.minimum(i, npt - 1), 0)),
                   pl.BlockSpec((tm, d), lambda i: (jnp.maximum(i - npt, 0), 0))],
        out_shape=[jax.ShapeDtypeStruct((n_prompt_rows, d), F32),
                   jax.ShapeDtypeStruct((t - n_prompt_rows, d), F32)],
        scratch_shapes=[pltpu.VMEM((2, TOP_K, tm // COMBINE_GROUP, COMBINE_GROUP, d // 2), I32),
                        pltpu.VMEM((tm, d), F32),
                        pltpu.SemaphoreType.DMA((2,))],
        compiler_params=_params(1),
        name="moe_combine",
    )(dest_tiles, dest_tiles, x, shared, gate_t, ys, g.reshape(1, d), b.reshape(1, d))


def _tile_major(a, tm):
    k, t = a.shape
    return a.reshape(k, t // tm, tm).transpose(1, 0, 2).reshape(-1)


def _moe_ln(x, x_packed, router_w, router_bias, wg, wu, wd, swg, swu, swd, g, b, n_prompt_rows):
    t, d = x.shape
    w_t = router_w.T
    w_hi = w_t.astype(BF16)
    w_lo = (w_t - w_hi.astype(F32)).astype(BF16)
    tr = _divisor_tile(t, 512, LANES)
    idx, gate, rank, cnt = _router(x, w_hi, w_lo, router_bias, tr)

    counts = cnt[:, 0]
    padded = (counts + EXPERT_ROWS - 1) // EXPERT_ROWS * EXPERT_ROWS
    pad_end = jnp.cumsum(padded)
    pad_start = pad_end - padded
    experts = jnp.arange(N_EXPERTS, dtype=I32)
    onehot = idx[None] == experts[:, None, None]
    dest = jnp.sum(jnp.where(onehot, pad_start[:, None, None], 0), axis=0) + rank
    n_blocks = (t * TOP_K + N_EXPERTS * (EXPERT_ROWS - 1)) // EXPERT_ROWS + 1
    n_valid = (pad_end[-1] // EXPERT_ROWS).astype(I32)
    blk = jnp.minimum(jnp.arange(n_blocks, dtype=I32), n_valid - 1)
    block_expert = jnp.minimum(
        jnp.sum(pad_end[None, :] <= (blk * EXPERT_ROWS)[:, None], axis=1), N_EXPERTS - 1).astype(I32)
    later_used = (experts[None, :] > experts[:, None]) & (counts[None, :] > 0)
    next_used = jnp.min(jnp.where(later_used, experts[None, :], N_EXPERTS), axis=1)
    next_used = jnp.where(next_used == N_EXPERTS, -1, next_used)
    per_block = lambda table: jnp.sum(
        jnp.where(block_expert[:, None] == experts[None, :], table[None, :], 0), axis=1).astype(I32)
    next_expert = per_block(next_used)
    rows_used = per_block(pad_start + counts) - blk * EXPERT_ROWS
    short_block = (rows_used <= EXPERT_ROWS // 2).astype(I32)

    td = _divisor_tile(t, 256, SUBLANES)
    xs, shared = _dispatch(x_packed, _tile_major(dest, td), (pad_start + counts).astype(I32),
                           (padded - counts).astype(I32), swg.astype(BF16), swu.astype(BF16),
                           swd.astype(BF16), n_blocks * EXPERT_ROWS, td)
    ys = _experts(xs, wg, wu, wd, block_expert, next_expert, short_block, n_valid.reshape(1))
    tc = _divisor_tile(math.gcd(n_prompt_rows, t - n_prompt_rows), 128, SUBLANES)
    return _combine(x, shared, gate.T, _tile_major(dest, tc), ys, g, b, tc, n_prompt_rows)


def kernel(x_prompt, x_sample, mem_prompt, state_gla, cache_swa_k, cache_swa_v, cache_mem_k, cache_mem_v,
           w_in, gla_w_gate2, gla_b_gate, gla_norm_g, swa_sinks, w_mix_out, ln1_g, ln1_b,
           mem_wq, mem_wk, mem_wv, mem_wo, ln2_g, ln2_b,
           router_w, router_bias, exp_w_gate, exp_w_up, exp_w_down, sh_w_gate, sh_w_up, sh_w_down,
           ln3_g, ln3_b):
    nbp, seq, d = x_prompt.shape
    nbs, dec, _ = x_sample.shape
    tp, ts = nbp * seq, nbs * dec
    t = tp + ts
    mem_tokens = mem_prompt.shape[1]
    assert state_gla.shape[0] == DEPTH and seq % SWA_WINDOW == 0 and dec % SUBLANES == 0
    assert cache_swa_k.shape[2] == SWA_WINDOW and tp % dec == 0
    tm = _divisor_tile(math.gcd(tp, ts), 512, 16)
    l = 0

    xp = x_prompt.reshape(tp, d)
    xs = x_sample.reshape(ts, d)

    gate_src = GLA_QK + GLA_QK + GLA_VW + GLA_VW
    swa_rows = SWA_QW + 2 * SWA_KVW
    w_in_rt = _reorder_cast(
        w_in[l].T,
        [(0, 0, gate_src),
         (gate_src + GLA_GATE_RANK, COL_SQ, swa_rows),
         (gate_src, COL_GG, GLA_GATE_RANK)],
        IN_COLS_PADDED, _divisor_tile(IN_COLS_PADDED, 512, 16))
    z = _matmul_split(xp, xs, w_in_rt, tm, IN_COLS_PADDED // 2, F32)

    wa2_pad = jnp.concatenate(
        [gla_w_gate2[l], jnp.zeros((LANES - GLA_GATE_RANK, GLA_QK), F32)], axis=0)
    wa2_hi = wa2_pad.astype(BF16)
    wa2_lo = (wa2_pad - wa2_hi.astype(F32)).astype(BF16)
    wa2_stack = jnp.concatenate([wa2_hi, wa2_lo, wa2_hi], axis=0)
    zero_state = jnp.zeros((nbp, GLA_HEADS, GLA_DK, GLA_DV), F32)
    gla_rows = _divisor_tile(seq, GLA_STEP_ROWS, CHUNK)
    mix, st_p = _gla(z, wa2_stack, gla_b_gate[l], gla_norm_g[l], zero_state, None,
                     nb=nbp, n_chunks=seq // gla_rows, rows=gla_rows, chunk=CHUNK, row0=0, chained=True)
    seqs = _divisor_tile(nbs, GLA_SAMPLE_SEQS, 1)
    assert tp % (seqs * dec) == 0
    mix, st_s = _gla(z, wa2_stack, gla_b_gate[l], gla_norm_g[l], state_gla[l], mix,
                     nb=nbs // seqs, n_chunks=1, rows=seqs * dec, chunk=dec, row0=tp, chained=False)
    nblk = seq // SWA_WINDOW
    prev_p = lambda col: (lambda b, i: (b * nblk + jnp.maximum(i - 1, 0), col))
    mix = _swa(z, swa_sinks[l], z, z, (prev_p(COL_SK // SWA_KVW), prev_p(COL_SV // SWA_KVW)), mix,
               nb=nbp, n_blocks=nblk, rows=SWA_WINDOW, row0=0, q_base=0, q_stride=SWA_WINDOW)
    ck = cache_swa_k[l].reshape(nbs * SWA_WINDOW, SWA_KVW)
    cv = cache_swa_v[l].reshape(nbs * SWA_WINDOW, SWA_KVW)
    prev_s = lambda b, i: (b, 0)
    mix = _swa(z, swa_sinks[l], ck, cv, (prev_s, prev_s), mix,
               nb=nbs, n_blocks=1, rows=dec, row0=tp, q_base=PAST_LEN, q_stride=0)
    h1 = _matmul_res_ln(mix, w_mix_out[l].astype(BF16), (xp, xs), ln1_g[l], ln1_b[l], tm)

    mem = mem_prompt.reshape(nbp * mem_tokens, d)
    tmem = _divisor_tile(nbp * mem_tokens, 512, 16)
    mk = _matmul(mem, mem_wk[l].astype(BF16), tmem, _divisor_tile(d, 1024, LANES), F32)
    mv = _matmul(mem, mem_wv[l].astype(BF16), tmem, _divisor_tile(d, 1024, LANES), F32)
    q = _matmul(h1, mem_wq[l].astype(BF16), tm, d, BF16)
    tq = _divisor_tile(seq, 256, 16)
    att = _mem_attn(q, mk, mv, None, nb=nbp, n_tiles=seq // tq, rows=tq, row0=0, mem_tokens=mem_tokens)
    att = _mem_attn(q, cache_mem_k[l].reshape(nbs * mem_tokens, d), cache_mem_v[l].reshape(nbs * mem_tokens, d),
                    att, nb=nbs, n_tiles=1, rows=dec, row0=tp, mem_tokens=mem_tokens)
    h2, h2_packed = _matmul_res_ln(att, mem_wo[l].astype(BF16), h1, ln2_g[l], ln2_b[l], tm, emit_packed=True)

    out_p, out_s = _moe_ln(h2, h2_packed, router_w[l], router_bias[l], exp_w_gate[l], exp_w_up[l],
                           exp_w_down[l], sh_w_gate[l], sh_w_up[l], sh_w_down[l], ln3_g[l], ln3_b[l], tp)

    y_prompt = out_p.reshape(nbp, seq, d)
    y_sample = out_s.reshape(nbs, dec, d)
    kv = z[:, COL_SK:COL_SK + 2 * SWA_KVW]
    kvp = kv[:tp].reshape(nbp, seq, 2 * SWA_KVW)[:, seq - SWA_WINDOW:]
    kvs = kv[tp:].reshape(nbs, dec, 2 * SWA_KVW)
    kv_shape = lambda a: a.reshape(a.shape[0], a.shape[1], SWA_KV_HEADS, SWA_HEAD_DIM)[None]
    mem_shape = lambda a: a.reshape(nbp, mem_tokens, MEM_HEADS, d // MEM_HEADS)[None]
    return (y_prompt, y_sample,
            st_p[None],
            kv_shape(kvp[..., :SWA_KVW]), kv_shape(kvp[..., SWA_KVW:]),
            mem_shape(mk), mem_shape(mv),
            st_s[None],
            kv_shape(kvs[..., :SWA_KVW]), kv_shape(kvs[..., SWA_KVW:]))
```

```python
import functools
import math

import jax
import jax.numpy as jnp
from jax import lax
from jax.experimental import pallas as pl
from jax.experimental.pallas import tpu as pltpu

F32 = jnp.float32
BF16 = jnp.bfloat16
I32 = jnp.int32

CHUNK = 64
PAST_LEN = 2048
GLA_HEADS = 8
GLA_DK = 64
GLA_DV = 128
GLA_QK = GLA_HEADS * GLA_DK
GLA_VW = GLA_HEADS * GLA_DV
GLA_GATE_RANK = 16
GLA_TAU = 16.0
SWA_HEADS = 16
SWA_KV_HEADS = 2
SWA_GROUP = SWA_HEADS // SWA_KV_HEADS
SWA_HEAD_DIM = 64
SWA_QW = SWA_HEADS * SWA_HEAD_DIM
SWA_KVW = SWA_KV_HEADS * SWA_HEAD_DIM
SWA_WINDOW = 128
WINDOW_CHUNKS = SWA_WINDOW // CHUNK
MEM_HEADS = 4
N_EXPERTS = 64
TOP_K = 8
N_GROUPS = 8
GROUP_SIZE = N_EXPERTS // N_GROUPS
TOPK_GROUPS = 4
ROUTED_SCALE = 2.5
DEPTH = 1
DEEPNORM_ALPHA = (2 * DEPTH) ** 0.25
LN_EPS = 1e-5
RMS_EPS = 1e-6

LANES = 128
SUBLANES = 8
VMEM_LIMIT_BYTES = 56 * 1024 * 1024

COL_GQ = 0
COL_GK = COL_GQ + GLA_QK
COL_GV = COL_GK + GLA_QK
COL_GR = COL_GV + GLA_VW
COL_SQ = COL_GR + GLA_VW
COL_SK = COL_SQ + SWA_QW
COL_SV = COL_SK + SWA_KVW
COL_GG = COL_SV + SWA_KVW
MXU_COLS = 256
IN_COLS_PADDED = -(-(COL_GG + LANES) // (2 * MXU_COLS)) * (2 * MXU_COLS)

EXPERT_ROWS = 256
COMBINE_GROUP = SUBLANES
LN_SUB_ROWS = 128
GLA_STEP_ROWS = 256
GLA_SAMPLE_SEQS = 8
COPY_SLOTS = 3

NT_DIMS = (((1,), (1,)), ((), ()))
TN_DIMS = (((0,), (0,)), ((), ()))


def _params(n_axes):
    return pltpu.CompilerParams(dimension_semantics=("arbitrary",) * n_axes,
                                vmem_limit_bytes=VMEM_LIMIT_BYTES)


def _sigmoid(x):
    return 1.0 / (1.0 + jnp.exp(-x))


def _silu(x):
    return x * _sigmoid(x)


def _layer_norm(x, g, b):
    mu = jnp.mean(x, axis=-1, keepdims=True)
    xc = x - mu
    var = jnp.mean(xc * xc, axis=-1, keepdims=True)
    return xc * lax.rsqrt(var + LN_EPS) * g + b


HI_MASK = -65536


def _round_to_bf16_bits(x):
    b = lax.bitcast_convert_type(x, I32)
    return b + (0x7FFF + (lax.shift_right_logical(b, 16) & 1))


def _pack_halves(x):
    c = x.shape[1] // 2
    lo = lax.shift_right_logical(_round_to_bf16_bits(x[:, :c]), 16)
    hi = _round_to_bf16_bits(x[:, c:]) & HI_MASK
    return lo | hi


def _unpack_halves(w):
    return (lax.bitcast_convert_type(lax.shift_left(w, 16), F32),
            lax.bitcast_convert_type(w & HI_MASK, F32))


def _divisor_tile(n, pref, mult):
    t = min(pref, n)
    while t > mult and (n % t or t % mult):
        t -= mult
    assert n % t == 0 and t % mult == 0, (n, pref, mult)
    return t


def _mm_kernel(x_ref, w_ref, o_ref, xb_ref):
    @pl.when(pl.program_id(1) == 0)
    def _():
        xb_ref[...] = x_ref[...].astype(BF16)

    o_ref[...] = jnp.dot(xb_ref[...], w_ref[...].astype(BF16), preferred_element_type=F32).astype(o_ref.dtype)


def _matmul(x, w, tm, tn, out_dtype):
    m, k = x.shape
    n = w.shape[1]
    return pl.pallas_call(
        _mm_kernel,
        grid=(m // tm, n // tn),
        in_specs=[pl.BlockSpec((tm, k), lambda i, j: (i, 0)),
                  pl.BlockSpec((k, tn), lambda i, j: (0, j))],
        out_specs=pl.BlockSpec((tm, tn), lambda i, j: (i, j)),
        out_shape=jax.ShapeDtypeStruct((m, n), out_dtype),
        scratch_shapes=[pltpu.VMEM((tm, k), BF16)],
        compiler_params=_params(2),
        name="matmul",
    )(x, w)


def _split_rows_specs(tm, k, n_first_tiles, n_grid_axes):
    if n_grid_axes == 1:
        return [pl.BlockSpec((tm, k), lambda i: (jnp.minimum(i, n_first_tiles - 1), 0)),
                pl.BlockSpec((tm, k), lambda i: (jnp.maximum(i - n_first_tiles, 0), 0))]
    return [pl.BlockSpec((tm, k), lambda j, i: (jnp.minimum(i, n_first_tiles - 1), 0)),
            pl.BlockSpec((tm, k), lambda j, i: (jnp.maximum(i - n_first_tiles, 0), 0))]


def _mm_split_kernel(xa_ref, xb_ref, wt_ref, o_ref, *, n_first_tiles):
    i = pl.program_id(1)

    def product(x_ref):
        o_ref[...] = lax.dot_general(x_ref[...].astype(BF16), wt_ref[...], NT_DIMS,
                                     preferred_element_type=F32).astype(o_ref.dtype)

    @pl.when(i < n_first_tiles)
    def _():
        product(xa_ref)

    @pl.when(i >= n_first_tiles)
    def _():
        product(xb_ref)


def _matmul_split(xa, xb, w_t, tm, tn, out_dtype):
    ma, k = xa.shape
    m = ma + xb.shape[0]
    n = w_t.shape[0]
    return pl.pallas_call(
        functools.partial(_mm_split_kernel, n_first_tiles=ma // tm),
        grid=(n // tn, m // tm),
        in_specs=_split_rows_specs(tm, k, ma // tm, 2) + [pl.BlockSpec((tn, k), lambda j, i: (j, 0))],
        out_specs=pl.BlockSpec((tm, tn), lambda j, i: (i, j)),
        out_shape=jax.ShapeDtypeStruct((m, n), out_dtype),
        compiler_params=_params(2),
        name="matmul_split",
    )(xa, xb, w_t)


def _reorder_cast_kernel(src_hbm, dst_hbm, stage_ref, out_ref, in_sems, out_sems, *, chunk_pieces, chunk):
    n = len(chunk_pieces)
    slots = stage_ref.shape[0]

    def in_copies(c):
        return [pltpu.make_async_copy(src_hbm.at[pl.ds(s0, ln)], stage_ref.at[c % slots, pl.ds(o0, ln)],
                                      in_sems.at[c % slots]) for s0, o0, ln in chunk_pieces[c]]

    def out_copy(c):
        return pltpu.make_async_copy(out_ref.at[c % slots], dst_hbm.at[pl.ds(c * chunk, chunk)],
                                     out_sems.at[c % slots])

    for c in range(min(slots - 1, n)):
        for cp in in_copies(c):
            cp.start()
    for c in range(n):
        if c + slots - 1 < n:
            for cp in in_copies(c + slots - 1):
                cp.start()
        for cp in in_copies(c):
            cp.wait()
        if c >= slots:
            out_copy(c - slots).wait()
        covered = sum(ln for _, _, ln in chunk_pieces[c])
        if covered < chunk:
            stage_ref[c % slots, covered:, :] = jnp.zeros((chunk - covered, stage_ref.shape[-1]),
                                                          stage_ref.dtype)
        out_ref[c % slots] = stage_ref[c % slots].astype(out_ref.dtype)
        out_copy(c).start()
    for c in range(max(n - slots, 0), n):
        out_copy(c).wait()


def _reorder_cast(src, runs, n_out_rows, chunk):
    k = src.shape[1]
    chunk_pieces = []
    for c0 in range(0, n_out_rows, chunk):
        pieces = []
        for s0, d0, ln in runs:
            lo, hi = max(d0, c0), min(d0 + ln, c0 + chunk)
            if lo < hi:
                assert (lo - c0) % 16 == 0 and (s0 + lo - d0) % SUBLANES == 0 and (hi - lo) % 16 == 0
                pieces.append((s0 + lo - d0, lo - c0, hi - lo))
        chunk_pieces.append(pieces)
    hbm = pl.BlockSpec(memory_space=pl.ANY)
    return pl.pallas_call(
        functools.partial(_reorder_cast_kernel, chunk_pieces=chunk_pieces, chunk=chunk),
        in_specs=[hbm],
        out_specs=hbm,
        out_shape=jax.ShapeDtypeStruct((n_out_rows, k), BF16),
        scratch_shapes=[pltpu.VMEM((COPY_SLOTS, chunk, k), src.dtype), pltpu.VMEM((COPY_SLOTS, chunk, k), BF16),
                        pltpu.SemaphoreType.DMA((COPY_SLOTS,)), pltpu.SemaphoreType.DMA((COPY_SLOTS,))],
        compiler_params=pltpu.CompilerParams(vmem_limit_bytes=VMEM_LIMIT_BYTES),
        name="reorder_cast",
    )(src)


def _mm_res_ln_kernel(x_ref, w_ref, *refs, n_res, n_first_tiles):
    res_refs, (g_ref, b_ref, o_ref, *packed_ref) = refs[:n_res], refs[n_res:]
    tm = x_ref.shape[0]
    sub = min(tm, LN_SUB_ROWS)
    use_first = pl.program_id(0) < n_first_tiles
    for r in range(0, tm, sub):
        rows = slice(r, r + sub)
        y = jnp.dot(x_ref[rows, :], w_ref[...], preferred_element_type=F32)
        res = res_refs[0][rows, :]
        if n_res == 2:
            res = jnp.where(use_first, res, res_refs[1][rows, :])
        h = _layer_norm(DEEPNORM_ALPHA * res + y, g_ref[...], b_ref[...])
        o_ref[rows, :] = h
        if packed_ref:
            groups = slice(r // SUBLANES, (r + sub) // SUBLANES)
            packed_ref[0][groups] = _pack_halves(h).reshape(sub // SUBLANES, SUBLANES, h.shape[1] // 2)


def _matmul_res_ln(x, w, res, g, b, tm, emit_packed=False):
    m, k = x.shape
    n = w.shape[1]
    if isinstance(res, tuple):
        n_first_tiles = res[0].shape[0] // tm
        res_specs = _split_rows_specs(tm, n, n_first_tiles, 1)
    else:
        n_first_tiles = 0
        res = (res,)
        res_specs = [pl.BlockSpec((tm, n), lambda i: (i, 0))]
    out_specs = [pl.BlockSpec((tm, n), lambda i: (i, 0))]
    out_shape = [jax.ShapeDtypeStruct((m, n), F32)]
    if emit_packed:
        out_specs.append(pl.BlockSpec((tm // SUBLANES, SUBLANES, n // 2), lambda i: (i, 0, 0)))
        out_shape.append(jax.ShapeDtypeStruct((m // SUBLANES, SUBLANES, n // 2), I32))
    outs = pl.pallas_call(
        functools.partial(_mm_res_ln_kernel, n_res=len(res), n_first_tiles=n_first_tiles),
        grid=(m // tm,),
        in_specs=[pl.BlockSpec((tm, k), lambda i: (i, 0)),
                  pl.BlockSpec((k, n), lambda i: (0, 0))] + res_specs + [
                  pl.BlockSpec((1, n), lambda i: (0, 0)),
                  pl.BlockSpec((1, n), lambda i: (0, 0))],
        out_specs=out_specs,
        out_shape=out_shape,
        compiler_params=_params(1),
        name="matmul_res_ln",
    )(x, w, *res, g.reshape(1, n), b.reshape(1, n))
    return outs if emit_packed else outs[0]


def _split3_bf16(x):
    hi = x.astype(BF16)
    r1 = x - hi.astype(F32)
    mid = r1.astype(BF16)
    lo = (r1 - mid.astype(F32)).astype(BF16)
    return jnp.concatenate([hi, mid, lo], axis=-1)


def _gla_kernel(q_ref, k_ref, v_ref, r_ref, gg_ref, wa2_ref, ba_ref, ng_ref, s0_ref, mix_in_ref,
                o_ref, sfin_ref, st_ref, *, rows, chunk, chained):
    del mix_in_ref
    n_sub = rows // chunk
    shift = chunk.bit_length() - 1
    assert chunk == 1 << shift

    if chained:
        @pl.when(pl.program_id(1) == 0)
        def _():
            for h in range(GLA_HEADS):
                st_ref[h] = s0_ref[0, h].T

    gg = gg_ref[...]
    gg_hi = gg.astype(BF16)
    gg_lo = (gg - gg_hi.astype(F32)).astype(BF16)
    gate = jnp.dot(jnp.concatenate([gg_hi, gg_hi, gg_lo], axis=-1), wa2_ref[...],
                   preferred_element_type=F32) + ba_ref[...]
    log_a = (jnp.minimum(gate, 0.0) - jnp.log1p(jnp.exp(-jnp.abs(gate)))) / GLA_TAU

    ri = lax.broadcasted_iota(I32, (rows, rows), 0)
    ci = lax.broadcasted_iota(I32, (rows, rows), 1)
    same_chunk = lax.shift_right_logical(ri, shift) == lax.shift_right_logical(ci, shift)
    causal = same_chunk & (ri >= ci)
    ones = jnp.concatenate([jnp.where(causal, 1.0, 0.0), jnp.where(same_chunk, 1.0, 0.0)], axis=0).astype(BF16)
    sums = jnp.dot(ones, _split3_bf16(log_a), preferred_element_type=F32)
    sums = sums[:, :GLA_QK] + sums[:, GLA_QK:2 * GLA_QK] + sums[:, 2 * GLA_QK:]
    bcum = sums[:rows]
    b_last = sums[rows:]
    q = q_ref[...] * (GLA_DK ** -0.5)
    k = k_ref[...]
    q_in = (q * jnp.exp(bcum)).astype(BF16)
    k_in = (k * jnp.exp(-bcum)).astype(BF16)
    k_out = (k * jnp.exp(b_last - bcum)).astype(BF16)
    decay = jnp.exp(b_last)
    ng = ng_ref[...]
    blk_r = lax.shift_right_logical(lax.broadcasted_iota(I32, (rows, n_sub * GLA_DK), 0), shift)
    blk_c = lax.shift_right_logical(lax.broadcasted_iota(I32, (rows, n_sub * GLA_DK), 1),
                                    GLA_DK.bit_length() - 1)
    for h in range(GLA_HEADS):
        ks = slice(h * GLA_DK, (h + 1) * GLA_DK)
        vs = slice(h * GLA_DV, (h + 1) * GLA_DV)
        vh = v_ref[:, vs].astype(BF16)
        qh = q_in[:, ks]
        attn = lax.dot_general(qh, k_in[:, ks], NT_DIMS, preferred_element_type=F32)
        attn = jnp.where(causal, attn, 0.0).astype(BF16)
        o = jnp.dot(attn, vh, preferred_element_type=F32)
        ko = k_out[:, ks]
        if n_sub > 1:
            ko = jnp.where(blk_r == blk_c, jnp.concatenate([ko] * n_sub, axis=-1), 0.0)
        upd_t = lax.dot_general(vh, ko, TN_DIMS, preferred_element_type=F32)
        states = []
        if chained:
            st = st_ref[h]
        for c in range(n_sub):
            if not chained:
                st = s0_ref[c, h].T
            states.append(st.astype(BF16))
            st = st * decay[c * chunk:c * chunk + 1, ks] + upd_t[:, c * GLA_DK:(c + 1) * GLA_DK]
            if not chained:
                sfin_ref[c, h] = st.T
        if chained:
            st_ref[h] = st
        o_all = lax.dot_general(qh, jnp.concatenate(states, axis=0), NT_DIMS, preferred_element_type=F32)
        o = o + jnp.concatenate(
            [o_all[c * chunk:(c + 1) * chunk, c * GLA_DV:(c + 1) * GLA_DV] for c in range(n_sub)], axis=0)
        o = o * lax.rsqrt(jnp.mean(o * o, axis=-1, keepdims=True) + RMS_EPS) * ng
        o_ref[:, vs] = (o * _silu(r_ref[:, vs])).astype(o_ref.dtype)
    if chained:
        @pl.when(pl.program_id(1) == pl.num_programs(1) - 1)
        def _():
            for h in range(GLA_HEADS):
                sfin_ref[0, h] = st_ref[h].T


def _gla(z, wa2_stack, ba, norm_g, s0, mix_in, *, nb, n_chunks, rows, chunk, row0, chained):
    t = z.shape[0]
    rb0 = row0 // rows
    seq_per_step = 1 if chained else rows // chunk
    assert chained or n_chunks == 1
    rmap = lambda colblk: (lambda b, c: (rb0 + b * n_chunks + c, colblk))
    in_specs = [
        pl.BlockSpec((rows, GLA_QK), rmap(COL_GQ // GLA_QK)),
        pl.BlockSpec((rows, GLA_QK), rmap(COL_GK // GLA_QK)),
        pl.BlockSpec((rows, GLA_VW), rmap(COL_GV // GLA_VW)),
        pl.BlockSpec((rows, GLA_VW), rmap(COL_GR // GLA_VW)),
        pl.BlockSpec((rows, LANES), rmap(COL_GG // LANES)),
        pl.BlockSpec((3 * LANES, GLA_QK), lambda b, c: (0, 0)),
        pl.BlockSpec((1, GLA_QK), lambda b, c: (0, 0)),
        pl.BlockSpec((1, GLA_DV), lambda b, c: (0, 0)),
        pl.BlockSpec((seq_per_step, GLA_HEADS, GLA_DK, GLA_DV), lambda b, c: (b, 0, 0, 0)),
    ]
    args = [z, z, z, z, z, wa2_stack, ba.reshape(1, GLA_QK), norm_g.reshape(1, GLA_DV), s0]
    aliases = {}
    if mix_in is None:
        mix_in = jnp.zeros((SUBLANES, LANES), BF16)
        in_specs.append(pl.BlockSpec(memory_space=pl.ANY))
    else:
        in_specs.append(pl.BlockSpec(memory_space=pl.ANY))
        aliases = {len(args): 0}
    args.append(mix_in)
    mix, s_fin = pl.pallas_call(
        functools.partial(_gla_kernel, rows=rows, chunk=chunk, chained=chained),
        grid=(nb, n_chunks),
        in_specs=in_specs,
        out_specs=[pl.BlockSpec((rows, GLA_VW), rmap(0)),
                   pl.BlockSpec((seq_per_step, GLA_HEADS, GLA_DK, GLA_DV), lambda b, c: (b, 0, 0, 0))],
        out_shape=[jax.ShapeDtypeStruct((t, GLA_VW + SWA_QW), BF16),
                   jax.ShapeDtypeStruct((nb * seq_per_step, GLA_HEADS, GLA_DK, GLA_DV), F32)],
        scratch_shapes=[pltpu.VMEM((GLA_HEADS, GLA_DV, GLA_DK), F32)],
        input_output_aliases=aliases,
        compiler_params=_params(2),
        name="gla",
    )(*args)
    return mix, s_fin


def _swa_kernel(sinks_ref, q_ref, kc_ref, vc_ref, kp_ref, vp_ref, mix_in_ref, o_ref, bias_ref, *,
                rows, q_base, q_stride):
    del mix_in_ref
    i = pl.program_id(1)
    n_tables = bias_ref.shape[0]
    nk = SWA_WINDOW + rows
    sr = SWA_GROUP * rows
    assert rows & (rows - 1) == 0 and q_base % CHUNK == 0 and q_stride % CHUNK == 0
    assert n_tables == 1 or q_base + q_stride >= SWA_WINDOW
    head_in_group = lax.shift_right_logical(lax.broadcasted_iota(I32, (sr, 1), 0), rows.bit_length() - 1)

    @pl.when(jnp.logical_and(pl.program_id(0) == 0, i == 0))
    def _():
        row = lax.broadcasted_iota(I32, (sr, nk), 0)
        col = lax.broadcasted_iota(I32, (sr, nk), 1)
        for t in range(n_tables):
            q0 = q_base + t * q_stride
            qpos = q0 + (row & (rows - 1))
            kpos = q0 - SWA_WINDOW + col
            dist = jnp.abs(qpos - kpos).astype(F32)
            qc = lax.shift_right_arithmetic(qpos, CHUNK.bit_length() - 1)
            kc = lax.shift_right_arithmetic(kpos, CHUNK.bit_length() - 1)
            allowed = (kpos >= 0) & (kc <= qc) & (kc >= qc - WINDOW_CHUNKS)
            for g in range(SWA_KV_HEADS):
                slope = jnp.zeros((sr, 1), F32)
                for j in range(SWA_GROUP):
                    h = g * SWA_GROUP + j
                    slope = jnp.where(head_in_group == j, 2.0 ** (-8.0 * (h + 1) / SWA_HEADS), slope)
                bias_ref[t, g] = jnp.where(allowed, -(slope * dist), -jnp.inf)

    table = jnp.minimum(i, n_tables - 1)
    keys = jnp.concatenate([kp_ref[...], kc_ref[...]], axis=0).astype(BF16)
    vals = jnp.concatenate([vp_ref[...], vc_ref[...]], axis=0).astype(BF16)
    for g in range(SWA_KV_HEADS):
        sink = jnp.zeros((sr, 1), F32)
        for j in range(SWA_GROUP):
            sink = jnp.where(head_in_group == j, sinks_ref[g * SWA_GROUP + j], sink)
        kv = slice(g * SWA_HEAD_DIM, (g + 1) * SWA_HEAD_DIM)
        qg = jnp.concatenate(
            [q_ref[:, (g * SWA_GROUP + j) * SWA_HEAD_DIM:(g * SWA_GROUP + j + 1) * SWA_HEAD_DIM]
             for j in range(SWA_GROUP)], axis=0).astype(BF16)
        s = lax.dot_general(qg, keys[:, kv], NT_DIMS, preferred_element_type=F32)
        s = s * (SWA_HEAD_DIM ** -0.5) + bias_ref[table, g]
        m = jnp.maximum(jnp.max(s, axis=-1, keepdims=True), sink)
        p = jnp.exp(s - m)
        denom = jnp.sum(p, axis=-1, keepdims=True) + jnp.exp(sink - m)
        o = jnp.dot(p.astype(BF16), vals[:, kv], preferred_element_type=F32) / denom
        for j in range(0, SWA_GROUP, 2):
            c0 = (g * SWA_GROUP + j) * SWA_HEAD_DIM
            o_ref[:, c0:c0 + LANES] = jnp.concatenate(
                [o[j * rows:(j + 1) * rows], o[(j + 1) * rows:(j + 2) * rows]], axis=-1).astype(o_ref.dtype)


def _swa(z, sinks, k_prev, v_prev, prev_map, mix_in, *, nb, n_blocks, rows, row0, q_base, q_stride):
    rb0 = row0 // rows
    rmap = lambda colblk: (lambda b, i: (rb0 + b * n_blocks + i, colblk))
    return pl.pallas_call(
        functools.partial(_swa_kernel, rows=rows, q_base=q_base, q_stride=q_stride),
        grid=(nb, n_blocks),
        in_specs=[pl.BlockSpec(memory_space=pltpu.SMEM),
                  pl.BlockSpec((rows, SWA_QW), rmap(COL_SQ // SWA_QW)),
                  pl.BlockSpec((rows, SWA_KVW), rmap(COL_SK // SWA_KVW)),
                  pl.BlockSpec((rows, SWA_KVW), rmap(COL_SV // SWA_KVW)),
                  pl.BlockSpec((SWA_WINDOW, SWA_KVW), prev_map[0]),
                  pl.BlockSpec((SWA_WINDOW, SWA_KVW), prev_map[1]),
                  pl.BlockSpec(memory_space=pl.ANY)],
        out_specs=pl.BlockSpec((rows, SWA_QW), rmap(GLA_VW // SWA_QW)),
        out_shape=jax.ShapeDtypeStruct(mix_in.shape, mix_in.dtype),
        scratch_shapes=[pltpu.VMEM((min(2, n_blocks), SWA_KV_HEADS, SWA_GROUP * rows, SWA_WINDOW + rows), F32)],
        input_output_aliases={6: 0},
        compiler_params=_params(2),
        name="swa",
    )(sinks, z, z, z, k_prev, v_prev, mix_in)


def _mem_attn_kernel(q_ref, k_ref, v_ref, o_in_ref, o_ref, *, head_dim):
    del o_in_ref
    for h in range(MEM_HEADS):
        hs = slice(h * head_dim, (h + 1) * head_dim)
        kh = k_ref[:, hs].astype(BF16)
        vh = v_ref[:, hs].astype(BF16)
        s = lax.dot_general(q_ref[:, hs], kh, NT_DIMS, preferred_element_type=F32) * (head_dim ** -0.5)
        m = jnp.max(s, axis=-1, keepdims=True)
        p = jnp.exp(s - m)
        denom = jnp.sum(p, axis=-1, keepdims=True)
        o = jnp.dot(p.astype(BF16), vh, preferred_element_type=F32) / denom
        o_ref[:, hs] = o.astype(o_ref.dtype)


def _mem_attn(q, mk, mv, o_in, *, nb, n_tiles, rows, row0, mem_tokens):
    t, d = q.shape
    rb0 = row0 // rows
    qmap = lambda b, i: (rb0 + b * n_tiles + i, 0)
    in_specs = [pl.BlockSpec((rows, d), qmap),
                pl.BlockSpec((mem_tokens, d), lambda b, i: (b, 0)),
                pl.BlockSpec((mem_tokens, d), lambda b, i: (b, 0)),
                pl.BlockSpec(memory_space=pl.ANY)]
    aliases = {}
    if o_in is None:
        o_in = jnp.zeros((SUBLANES, LANES), BF16)
    else:
        aliases = {3: 0}
    return pl.pallas_call(
        functools.partial(_mem_attn_kernel, head_dim=d // MEM_HEADS),
        grid=(nb, n_tiles),
        in_specs=in_specs,
        out_specs=pl.BlockSpec((rows, d), qmap),
        out_shape=jax.ShapeDtypeStruct((t, d), BF16),
        input_output_aliases=aliases,
        compiler_params=_params(2),
        name="mem_attn",
    )(q, mk, mv, o_in)


def _pick_first_max(cur, iota, sentinel):
    mx = jnp.max(cur, axis=0, keepdims=True)
    first = jnp.min(jnp.where(cur == mx, iota, sentinel), axis=0, keepdims=True)
    return iota == first, first


def _router_kernel(x_ref, wh_ref, wl_ref, bias_ref, idx_ref, gate_ref, rank_ref, cnt_ref, run_ref, before_ref,
                   *, tm):
    i = pl.program_id(0)

    @pl.when(i == 0)
    def _():
        run_ref[...] = jnp.zeros_like(run_ref)
        ti = lax.broadcasted_iota(I32, (tm, tm), 0)
        tj = lax.broadcasted_iota(I32, (tm, tm), 1)
        before_ref[...] = jnp.where(ti < tj, 1.0, 0.0).astype(BF16)

    x = x_ref[...]
    xh = x.astype(BF16)
    xl = (x - xh.astype(F32)).astype(BF16)
    wh = wh_ref[...]
    logits = (lax.dot_general(wh, xh, NT_DIMS, preferred_element_type=F32)
              + lax.dot_general(wh, xl, NT_DIMS, preferred_element_type=F32)
              + lax.dot_general(wl_ref[...], xh, NT_DIMS, preferred_element_type=F32))
    scores = _sigmoid(logits)
    sel = scores + bias_ref[...]
    neg_inf = -jnp.inf

    li = lax.broadcasted_iota(I32, (GROUP_SIZE, tm), 0)
    grp_rows = []
    for g in range(N_GROUPS):
        blk = sel[g * GROUP_SIZE:(g + 1) * GROUP_SIZE, :]
        pick, _ = _pick_first_max(blk, li, GROUP_SIZE)
        m1 = jnp.max(blk, axis=0, keepdims=True)
        m2 = jnp.max(jnp.where(pick, neg_inf, blk), axis=0, keepdims=True)
        grp_rows.append(m1 + m2)
    grp = jnp.concatenate(grp_rows, axis=0)

    gi = lax.broadcasted_iota(I32, (N_GROUPS, tm), 0)
    gsel = jnp.zeros((N_GROUPS, tm), F32)
    cur = grp
    for _ in range(TOPK_GROUPS):
        pick, _ = _pick_first_max(cur, gi, N_GROUPS)
        gsel = jnp.where(pick, 1.0, gsel)
        cur = jnp.where(pick, neg_inf, cur)
    emask = jnp.concatenate(
        [jnp.broadcast_to(gsel[g:g + 1, :], (GROUP_SIZE, tm)) for g in range(N_GROUPS)], axis=0)

    ei = lax.broadcasted_iota(I32, (N_EXPERTS, tm), 0)
    cur = jnp.where(emask > 0.5, sel, neg_inf)
    chosen = jnp.zeros((N_EXPERTS, tm), F32)
    idx_rows, w_rows = [], []
    for _ in range(TOP_K):
        pick, first = _pick_first_max(cur, ei, N_EXPERTS)
        idx_rows.append(first)
        w_rows.append(jnp.sum(jnp.where(pick, scores, 0.0), axis=0, keepdims=True))
        chosen = jnp.where(pick, 1.0, chosen)
        cur = jnp.where(pick, neg_inf, cur)
    idx = jnp.concatenate(idx_rows, axis=0)
    w = jnp.concatenate(w_rows, axis=0)
    gate_ref[...] = w / jnp.sum(w, axis=0, keepdims=True) * ROUTED_SCALE
    idx_ref[...] = idx

    local = jnp.dot(chosen.astype(BF16), before_ref[...], preferred_element_type=F32)
    total = local + run_ref[:, 0:1]
    rank_rows = [jnp.sum(jnp.where(ei == idx_rows[k], total, 0.0), axis=0, keepdims=True)
                 for k in range(TOP_K)]
    rank_ref[...] = jnp.concatenate(rank_rows, axis=0).astype(I32)
    run_ref[...] = run_ref[...] + jnp.sum(chosen, axis=1, keepdims=True)
    cnt_ref[...] = run_ref[...].astype(I32)


def _router(x, w_hi_t, w_lo_t, bias, tm):
    t, d = x.shape
    return pl.pallas_call(
        functools.partial(_router_kernel, tm=tm),
        grid=(t // tm,),
        in_specs=[pl.BlockSpec((tm, d), lambda i: (i, 0)),
                  pl.BlockSpec((N_EXPERTS, d), lambda i: (0, 0)),
                  pl.BlockSpec((N_EXPERTS, d), lambda i: (0, 0)),
                  pl.BlockSpec((N_EXPERTS, 1), lambda i: (0, 0))],
        out_specs=[pl.BlockSpec((TOP_K, tm), lambda i: (0, i)),
                   pl.BlockSpec((TOP_K, tm), lambda i: (0, i)),
                   pl.BlockSpec((TOP_K, tm), lambda i: (0, i)),
                   pl.BlockSpec((N_EXPERTS, LANES), lambda i: (0, 0))],
        out_shape=[jax.ShapeDtypeStruct((TOP_K, t), I32),
                   jax.ShapeDtypeStruct((TOP_K, t), F32),
                   jax.ShapeDtypeStruct((TOP_K, t), I32),
                   jax.ShapeDtypeStruct((N_EXPERTS, LANES), I32)],
        scratch_shapes=[pltpu.VMEM((N_EXPERTS, LANES), F32), pltpu.VMEM((tm, tm), BF16)],
        compiler_params=_params(1),
        name="router",
    )(x, w_hi_t, w_lo_t, bias.reshape(N_EXPERTS, 1))


PAD_BITS = tuple(1 << s for s in reversed(range(EXPERT_ROWS.bit_length() - 1)))


def _dispatch_kernel(zstart_ref, zcount_ref, dest_ref, x_ref, swg_ref, swu_ref, swd_ref, xs_ref, shared_ref,
                     zero_ref, sem, zsem, *, tm):
    i = pl.program_id(0)

    def start_group(g, carry):
        for j in range(SUBLANES):
            for k in range(TOP_K):
                d = dest_ref[k * tm + g * SUBLANES + j]
                pltpu.make_async_copy(x_ref.at[g, pl.ds(j, 1), :], xs_ref.at[d], sem).start(priority=k % 2)
        return carry

    lax.fori_loop(0, tm // SUBLANES, start_group, 0)

    half = x_ref.shape[-1]
    x_lo, x_hi = _unpack_halves(x_ref[...].reshape(tm, half))
    xb = jnp.concatenate([x_lo.astype(BF16), x_hi.astype(BF16)], axis=-1)
    hg = jnp.dot(xb, swg_ref[...], preferred_element_type=F32)
    hu = jnp.dot(xb, swu_ref[...], preferred_element_type=F32)
    shared_ref[...] = jnp.dot((_silu(hg) * hu).astype(BF16), swd_ref[...],
                              preferred_element_type=F32).astype(shared_ref.dtype)

    for k in range(TOP_K):
        pltpu.make_async_copy(xs_ref.at[pl.ds(0, tm)], xs_ref.at[pl.ds(0, tm)], sem).wait()

    @pl.when(i == pl.num_programs(0) - 1)
    def _():
        zero_ref[...] = jnp.zeros_like(zero_ref)

        def fill(e, wait):
            pos = zstart_ref[e]
            cnt = zcount_ref[e]
            for bit in PAD_BITS:
                has = (cnt & bit) != 0

                @pl.when(has)
                def _():
                    cp = pltpu.make_async_copy(zero_ref.at[pl.ds(0, bit)], xs_ref.at[pl.ds(pos, bit)], zsem)
                    if wait:
                        cp.wait()
                    else:
                        cp.start()

                pos = pos + jnp.where(has, bit, 0)

        def fill_start(e, carry):
            fill(e, False)
            return carry

        def fill_wait(e, carry):
            fill(e, True)
            return carry

        lax.fori_loop(0, N_EXPERTS, fill_start, 0)
        lax.fori_loop(0, N_EXPERTS, fill_wait, 0)


def _dispatch(x, dest_tiles, zstart, zcount, swg, swu, swd, n_rows, tm):
    t = x.shape[0] * SUBLANES
    d = x.shape[2]
    ds = swg.shape[1]
    const = lambda i, zs, zc: (0, 0)
    return pl.pallas_call(
        functools.partial(_dispatch_kernel, tm=tm),
        grid_spec=pltpu.PrefetchScalarGridSpec(
            num_scalar_prefetch=2,
            grid=(t // tm,),
            in_specs=[pl.BlockSpec((TOP_K * tm,), lambda i, zs, zc: (i,), memory_space=pltpu.SMEM),
                      pl.BlockSpec((tm // SUBLANES, SUBLANES, d), lambda i, zs, zc: (i, 0, 0)),
                      pl.BlockSpec((2 * d, ds), const),
                      pl.BlockSpec((2 * d, ds), const),
                      pl.BlockSpec((ds, 2 * d), const)],
            out_specs=[pl.BlockSpec(memory_space=pl.ANY),
                       pl.BlockSpec((tm, 2 * d), lambda i, zs, zc: (i, 0))],
            scratch_shapes=[pltpu.VMEM((EXPERT_ROWS // 2, 1, d), x.dtype),
                            pltpu.SemaphoreType.DMA(()),
                            pltpu.SemaphoreType.DMA(())]),
        out_shape=[jax.ShapeDtypeStruct((n_rows, 1, d), x.dtype),
                   jax.ShapeDtypeStruct((t, 2 * d), BF16)],
        compiler_params=_params(1),
        name="moe_dispatch",
    )(zstart, zcount, dest_tiles, x, swg, swu, swd)


def _expert_kernel(be_ref, nxt_ref, short_ref, nv_ref, xs_hbm, wg_hbm, wu_hbm, wd_hbm, ys_hbm,
                   wgs_ref, wus_ref, wds_ref, wgb_ref, wub_ref, wdb_ref, xbuf_ref, ybuf_ref,
                   wsems, xsems, ysems):
    b = pl.program_id(0)
    n_valid = nv_ref[0]
    e = be_ref[b]
    prev = be_ref[jnp.maximum(b - 1, 0)]
    fresh = jnp.logical_or(b == 0, e != prev)
    slot = lax.rem(b, 2)

    def weight_copies(expert):
        return (pltpu.make_async_copy(wg_hbm.at[expert], wgs_ref, wsems.at[0]),
                pltpu.make_async_copy(wu_hbm.at[expert], wus_ref, wsems.at[1]),
                pltpu.make_async_copy(wd_hbm.at[expert], wds_ref, wsems.at[2]))

    def rows_in(blk, to_slot):
        r0 = pl.multiple_of(blk * EXPERT_ROWS, EXPERT_ROWS)
        return pltpu.make_async_copy(xs_hbm.at[pl.ds(r0, EXPERT_ROWS), 0, :], xbuf_ref.at[to_slot],
                                     xsems.at[to_slot])

    def rows_out(blk, from_slot):
        r0 = pl.multiple_of(blk * EXPERT_ROWS, EXPERT_ROWS)
        return pltpu.make_async_copy(ybuf_ref.at[from_slot], ys_hbm.at[pl.ds(r0, EXPERT_ROWS), 0, :],
                                     ysems.at[from_slot])

    @pl.when(b == 0)
    def _():
        for cp in weight_copies(e):
            cp.start(priority=1)
        rows_in(b, slot).start()

    @pl.when(fresh)
    def _():
        for cp in weight_copies(e):
            cp.wait()
        wgb_ref[...] = wgs_ref[...].astype(BF16)
        wub_ref[...] = wus_ref[...].astype(BF16)
        wdb_ref[...] = wds_ref[...].astype(BF16)
        nxt = nxt_ref[b]

        @pl.when(nxt >= 0)
        def _():
            for cp in weight_copies(nxt):
                cp.start(priority=1)

    @pl.when(b < n_valid)
    def _():
        rows_in(b, slot).wait()

        @pl.when(b + 1 < n_valid)
        def _():
            rows_in(b + 1, 1 - slot).start()

        @pl.when(b >= 2)
        def _():
            rows_out(b - 2, slot).wait()

        def swiglu_rows(rows):
            half = xbuf_ref.shape[-1]
            x_lo, x_hi = _unpack_halves(xbuf_ref[slot, :rows, :])
            xb = jnp.concatenate([x_lo.astype(BF16), x_hi.astype(BF16)], axis=-1)
            hg = jnp.dot(xb, wgb_ref[...], preferred_element_type=F32)
            hu = jnp.dot(xb, wub_ref[...], preferred_element_type=F32)
            act = (_silu(hg) * hu).astype(BF16)
            ybuf_ref[slot, :rows, :] = _pack_halves(jnp.dot(act, wdb_ref[...], preferred_element_type=F32))
            if rows < EXPERT_ROWS:
                ybuf_ref[slot, rows:, :] = jnp.zeros((EXPERT_ROWS - rows, half), ybuf_ref.dtype)

        pl.when(short_ref[b] == 1)(functools.partial(swiglu_rows, EXPERT_ROWS // 2))
        pl.when(short_ref[b] == 0)(functools.partial(swiglu_rows, EXPERT_ROWS))
        rows_out(b, slot).start()

        @pl.when(b == n_valid - 1)
        def _():
            rows_out(b, slot).wait()

            @pl.when(b >= 1)
            def _():
                rows_out(b - 1, 1 - slot).wait()


def _experts(xs, wg, wu, wd, block_expert, next_expert, short_block, n_valid):
    n_rows = xs.shape[0]
    d = wg.shape[1]
    de = wg.shape[2]
    nb = n_rows // EXPERT_ROWS
    hbm = pl.BlockSpec(memory_space=pl.ANY)
    return pl.pallas_call(
        _expert_kernel,
        grid_spec=pltpu.PrefetchScalarGridSpec(
            num_scalar_prefetch=4,
            grid=(nb,),
            in_specs=[hbm, hbm, hbm, hbm],
            out_specs=hbm,
            scratch_shapes=[pltpu.VMEM((d, de), F32), pltpu.VMEM((d, de), F32), pltpu.VMEM((de, d), F32),
                            pltpu.VMEM((d, de), BF16), pltpu.VMEM((d, de), BF16), pltpu.VMEM((de, d), BF16),
                            pltpu.VMEM((2, EXPERT_ROWS, d // 2), I32), pltpu.VMEM((2, EXPERT_ROWS, d // 2), I32),
                            pltpu.SemaphoreType.DMA((3,)), pltpu.SemaphoreType.DMA((2,)),
                            pltpu.SemaphoreType.DMA((2,))]),
        out_shape=jax.ShapeDtypeStruct((n_rows, 1, d // 2), I32),
        compiler_params=_params(1),
        name="moe_experts",
    )(block_expert, next_expert, short_block, n_valid, xs, wg, wu, wd)


def _combine_kernel(dest0_ref, destn_ref, x_ref, shared_ref, gate_ref, ys_ref, g_ref, b_ref,
                    op_ref, os_ref, buf_ref, acc_ref, sems, *, tm, n_prompt_tiles):
    i = pl.program_id(0)
    n = pl.num_programs(0)
    half = buf_ref.shape[-1]
    n_groups = tm // COMBINE_GROUP

    def issue_group(dest_ref, to_slot, g):
        for j in range(COMBINE_GROUP):
            t = g * COMBINE_GROUP + j
            for k in range(TOP_K):
                d = dest_ref[k * tm + t]
                pltpu.make_async_copy(ys_ref.at[d], buf_ref.at[to_slot, k, g, pl.ds(j, 1), :],
                                      sems.at[to_slot]).start(priority=k % 2)

    @pl.when(i == 0)
    def _():
        def first(g, carry):
            issue_group(dest0_ref, 0, g)
            return carry
        lax.fori_loop(0, n_groups, first, 0)

    def run_tile(slot):
        for k in range(TOP_K):
            pltpu.make_async_copy(ys_ref.at[pl.ds(0, tm)], ys_ref.at[pl.ds(0, tm)], sems.at[slot]).wait()

        def reduce_group(g):
            r0 = pl.multiple_of(g * COMBINE_GROUP, COMBINE_GROUP)
            gate = gate_ref[pl.ds(r0, COMBINE_GROUP), :]
            acc_lo = acc_hi = None
            for k in range(TOP_K):
                y_lo, y_hi = _unpack_halves(buf_ref[slot, k, g])
                gk = gate[:, k:k + 1]
                acc_lo = gk * y_lo if k == 0 else acc_lo + gk * y_lo
                acc_hi = gk * y_hi if k == 0 else acc_hi + gk * y_hi
            acc_ref[pl.ds(r0, COMBINE_GROUP), :half] = acc_lo
            acc_ref[pl.ds(r0, COMBINE_GROUP), half:] = acc_hi

        @pl.when(i + 1 < n)
        def _():
            def body(g, carry):
                issue_group(destn_ref, 1 - slot, g)
                reduce_group(g)
                return carry
            lax.fori_loop(0, n_groups, body, 0)

        @pl.when(i + 1 >= n)
        def _():
            def body(g, carry):
                reduce_group(g)
                return carry
            lax.fori_loop(0, n_groups, body, 0)

    for s in range(2):
        pl.when(lax.rem(i, 2) == s)(functools.partial(run_tile, s))

    x = x_ref[...]
    out = _layer_norm(DEEPNORM_ALPHA * x + (acc_ref[...] + shared_ref[...].astype(F32)),
                      g_ref[...], b_ref[...])

    @pl.when(i < n_prompt_tiles)
    def _():
        op_ref[...] = out

    @pl.when(i >= n_prompt_tiles)
    def _():
        os_ref[...] = out


def _combine(x, shared, gate_t, dest_tiles, ys, g, b, tm, n_prompt_rows):
    t, d = x.shape
    n_tiles = t // tm
    npt = n_prompt_rows // tm
    const = lambda i: (0, 0)
    return pl.pallas_call(
        functools.partial(_combine_kernel, tm=tm, n_prompt_tiles=npt),
        grid=(n_tiles,),
        in_specs=[pl.BlockSpec((TOP_K * tm,), lambda i: (0,), memory_space=pltpu.SMEM),
                  pl.BlockSpec((TOP_K * tm,), lambda i: (jnp.minimum(i + 1, n_tiles - 1),),
                               memory_space=pltpu.SMEM),
                  pl.BlockSpec((tm, d), lambda i: (i, 0)),
                  pl.BlockSpec((tm, d), lambda i: (i, 0)),
                  pl.BlockSpec((tm, TOP_K), lambda i: (i, 0)),
                  pl.BlockSpec(memory_space=pl.ANY),
                  pl.BlockSpec((1, d), const),
                  pl.BlockSpec((1, d), const)],
        out_specs=[pl.BlockSpec((tm, d), lambda i: (jnp.minimum(i, npt - 1), 0)),
                   pl.BlockSpec((tm, d), lambda i: (jnp.maximum(i - npt, 0), 0))],
        out_shape=[jax.ShapeDtypeStruct((n_prompt_rows, d), F32),
                   jax.ShapeDtypeStruct((t - n_prompt_rows, d), F32)],
        scratch_shapes=[pltpu.VMEM((2, TOP_K, tm // COMBINE_GROUP, COMBINE_GROUP, d // 2), I32),
                        pltpu.VMEM((tm, d), F32),
                        pltpu.SemaphoreType.DMA((2,))],
        compiler_params=_params(1),
        name="moe_combine",
    )(dest_tiles, dest_tiles, x, shared, gate_t, ys, g.reshape(1, d), b.reshape(1, d))


def _tile_major(a, tm):
    k, t = a.shape
    return a.reshape(k, t // tm, tm).transpose(1, 0, 2).reshape(-1)


def _moe_ln(x, x_packed, router_w, router_bias, wg, wu, wd, swg, swu, swd, g, b, n_prompt_rows):
    t, d = x.shape
    w_t = router_w.T
    w_hi = w_t.astype(BF16)
    w_lo = (w_t - w_hi.astype(F32)).astype(BF16)
    tr = _divisor_tile(t, 512, LANES)
    idx, gate, rank, cnt = _router(x, w_hi, w_lo, router_bias, tr)

    counts = cnt[:, 0]
    padded = (counts + EXPERT_ROWS - 1) // EXPERT_ROWS * EXPERT_ROWS
    pad_end = jnp.cumsum(padded)
    pad_start = pad_end - padded
    experts = jnp.arange(N_EXPERTS, dtype=I32)
    onehot = idx[None] == experts[:, None, None]
    dest = jnp.sum(jnp.where(onehot, pad_start[:, None, None], 0), axis=0) + rank
    n_blocks = (t * TOP_K + N_EXPERTS * (EXPERT_ROWS - 1)) // EXPERT_ROWS + 1
    n_valid = (pad_end[-1] // EXPERT_ROWS).astype(I32)
    blk = jnp.minimum(jnp.arange(n_blocks, dtype=I32), n_valid - 1)
    block_expert = jnp.minimum(
        jnp.sum(pad_end[None, :] <= (blk * EXPERT_ROWS)[:, None], axis=1), N_EXPERTS - 1).astype(I32)
    later_used = (experts[None, :] > experts[:, None]) & (counts[None, :] > 0)
    next_used = jnp.min(jnp.where(later_used, experts[None, :], N_EXPERTS), axis=1)
    next_used = jnp.where(next_used == N_EXPERTS, -1, next_used)
    per_block = lambda table: jnp.sum(
        jnp.where(block_expert[:, None] == experts[None, :], table[None, :], 0), axis=1).astype(I32)
    next_expert = per_block(next_used)
    rows_used = per_block(pad_start + counts) - blk * EXPERT_ROWS
    short_block = (rows_used <= EXPERT_ROWS // 2).astype(I32)

    td = _divisor_tile(t, 256, SUBLANES)
    xs, shared = _dispatch(x_packed, _tile_major(dest, td), (pad_start + counts).astype(I32),
                           (padded - counts).astype(I32), swg.astype(BF16), swu.astype(BF16),
                           swd.astype(BF16), n_blocks * EXPERT_ROWS, td)
    ys = _experts(xs, wg, wu, wd, block_expert, next_expert, short_block, n_valid.reshape(1))
    tc = _divisor_tile(math.gcd(n_prompt_rows, t - n_prompt_rows), 128, SUBLANES)
    return _combine(x, shared, gate.T, _tile_major(dest, tc), ys, g, b, tc, n_prompt_rows)


def kernel(x_prompt, x_sample, mem_prompt, state_gla, cache_swa_k, cache_swa_v, cache_mem_k, cache_mem_v,
           w_in, gla_w_gate2, gla_b_gate, gla_norm_g, swa_sinks, w_mix_out, ln1_g, ln1_b,
           mem_wq, mem_wk, mem_wv, mem_wo, ln2_g, ln2_b,
           router_w, router_bias, exp_w_gate, exp_w_up, exp_w_down, sh_w_gate, sh_w_up, sh_w_down,
           ln3_g, ln3_b):
    nbp, seq, d = x_prompt.shape
    nbs, dec, _ = x_sample.shape
    tp, ts = nbp * seq, nbs * dec
    t = tp + ts
    mem_tokens = mem_prompt.shape[1]
    assert state_gla.shape[0] == DEPTH and seq % SWA_WINDOW == 0 and dec % SUBLANES == 0
    assert cache_swa_k.shape[2] == SWA_WINDOW and tp % dec == 0
    tm = _divisor_tile(math.gcd(tp, ts), 512, 16)
    l = 0

    xp = x_prompt.reshape(tp, d)
    xs = x_sample.reshape(ts, d)

    gate_src = GLA_QK + GLA_QK + GLA_VW + GLA_VW
    swa_rows = SWA_QW + 2 * SWA_KVW
    w_in_rt = _reorder_cast(
        w_in[l].T,
        [(0, 0, gate_src),
         (gate_src + GLA_GATE_RANK, COL_SQ, swa_rows),
         (gate_src, COL_GG, GLA_GATE_RANK)],
        IN_COLS_PADDED, _divisor_tile(IN_COLS_PADDED, 512, 16))
    z = _matmul_split(xp, xs, w_in_rt, tm, IN_COLS_PADDED // 2, F32)

    wa2_pad = jnp.concatenate(
        [gla_w_gate2[l], jnp.zeros((LANES - GLA_GATE_RANK, GLA_QK), F32)], axis=0)
    wa2_hi = wa2_pad.astype(BF16)
    wa2_lo = (wa2_pad - wa2_hi.astype(F32)).astype(BF16)
    wa2_stack = jnp.concatenate([wa2_hi, wa2_lo, wa2_hi], axis=0)
    zero_state = jnp.zeros((nbp, GLA_HEADS, GLA_DK, GLA_DV), F32)
    gla_rows = _divisor_tile(seq, GLA_STEP_ROWS, CHUNK)
    mix, st_p = _gla(z, wa2_stack, gla_b_gate[l], gla_norm_g[l], zero_state, None,
                     nb=nbp, n_chunks=seq // gla_rows, rows=gla_rows, chunk=CHUNK, row0=0, chained=True)
    seqs = _divisor_tile(nbs, GLA_SAMPLE_SEQS, 1)
    assert tp % (seqs * dec) == 0
    mix, st_s = _gla(z, wa2_stack, gla_b_gate[l], gla_norm_g[l], state_gla[l], mix,
                     nb=nbs // seqs, n_chunks=1, rows=seqs * dec, chunk=dec, row0=tp, chained=False)
    nblk = seq // SWA_WINDOW
    prev_p = lambda col: (lambda b, i: (b * nblk + jnp.maximum(i - 1, 0), col))
    mix = _swa(z, swa_sinks[l], z, z, (prev_p(COL_SK // SWA_KVW), prev_p(COL_SV // SWA_KVW)), mix,
               nb=nbp, n_blocks=nblk, rows=SWA_WINDOW, row0=0, q_base=0, q_stride=SWA_WINDOW)
    ck = cache_swa_k[l].reshape(nbs * SWA_WINDOW, SWA_KVW)
    cv = cache_swa_v[l].reshape(nbs * SWA_WINDOW, SWA_KVW)
    prev_s = lambda b, i: (b, 0)
    mix = _swa(z, swa_sinks[l], ck, cv, (prev_s, prev_s), mix,
               nb=nbs, n_blocks=1, rows=dec, row0=tp, q_base=PAST_LEN, q_stride=0)
    h1 = _matmul_res_ln(mix, w_mix_out[l].astype(BF16), (xp, xs), ln1_g[l], ln1_b[l], tm)

    mem = mem_prompt.reshape(nbp * mem_tokens, d)
    tmem = _divisor_tile(nbp * mem_tokens, 512, 16)
    mk = _matmul(mem, mem_wk[l], tmem, _divisor_tile(d, 512, LANES), F32)
    mv = _matmul(mem, mem_wv[l], tmem, _divisor_tile(d, 512, LANES), F32)
    q = _matmul(h1, mem_wq[l].astype(BF16), tm, d, BF16)
    tq = _divisor_tile(seq, 512, 16)
    att = _mem_attn(q, mk, mv, None, nb=nbp, n_tiles=seq // tq, rows=tq, row0=0, mem_tokens=mem_tokens)
    att = _mem_attn(q, cache_mem_k[l].reshape(nbs * mem_tokens, d), cache_mem_v[l].reshape(nbs * mem_tokens, d),
                    att, nb=nbs, n_tiles=1, rows=dec, row0=tp, mem_tokens=mem_tokens)
    h2, h2_packed = _matmul_res_ln(att, mem_wo[l].astype(BF16), h1, ln2_g[l], ln2_b[l], tm, emit_packed=True)

    out_p, out_s = _moe_ln(h2, h2_packed, router_w[l], router_bias[l], exp_w_gate[l], exp_w_up[l],
                           exp_w_down[l], sh_w_gate[l], sh_w_up[l], sh_w_down[l], ln3_g[l], ln3_b[l], tp)

    y_prompt = out_p.reshape(nbp, seq, d)
    y_sample = out_s.reshape(nbs, dec, d)
    kv = z[:, COL_SK:COL_SK + 2 * SWA_KVW]
    kvp = kv[:tp].reshape(nbp, seq, 2 * SWA_KVW)[:, seq - SWA_WINDOW:]
    kvs = kv[tp:].reshape(nbs, dec, 2 * SWA_KVW)
    kv_shape = lambda a: a.reshape(a.shape[0], a.shape[1], SWA_KV_HEADS, SWA_HEAD_DIM)[None]
    mem_shape = lambda a: a.reshape(nbp, mem_tokens, MEM_HEADS, d // MEM_HEADS)[None]
    return (y_prompt, y_sample,
            st_p[None],
            kv_shape(kvp[..., :SWA_KVW]), kv_shape(kvp[..., SWA_KVW:]),
            mem_shape(mk), mem_shape(mv),
            st_s[None],
            kv_shape(kvs[..., :SWA_KVW]), kv_shape(kvs[..., SWA_KVW:]))
```

```python
import functools
import math

import jax
import jax.numpy as jnp
from jax import lax
from jax.experimental import pallas as pl
from jax.experimental.pallas import tpu as pltpu

F32 = jnp.float32
BF16 = jnp.bfloat16
I32 = jnp.int32

CHUNK = 64
PAST_LEN = 2048
GLA_HEADS = 8
GLA_DK = 64
GLA_DV = 128
GLA_QK = GLA_HEADS * GLA_DK
GLA_VW = GLA_HEADS * GLA_DV
GLA_GATE_RANK = 16
GLA_TAU = 16.0
SWA_HEADS = 16
SWA_KV_HEADS = 2
SWA_GROUP = SWA_HEADS // SWA_KV_HEADS
SWA_HEAD_DIM = 64
SWA_QW = SWA_HEADS * SWA_HEAD_DIM
SWA_KVW = SWA_KV_HEADS * SWA_HEAD_DIM
SWA_WINDOW = 128
WINDOW_CHUNKS = SWA_WINDOW // CHUNK
MEM_HEADS = 4
N_EXPERTS = 64
TOP_K = 8
N_GROUPS = 8
GROUP_SIZE = N_EXPERTS // N_GROUPS
TOPK_GROUPS = 4
ROUTED_SCALE = 2.5
DEPTH = 1
DEEPNORM_ALPHA = (2 * DEPTH) ** 0.25
LN_EPS = 1e-5
RMS_EPS = 1e-6

LANES = 128
SUBLANES = 8
VMEM_LIMIT_BYTES = 56 * 1024 * 1024

COL_GQ = 0
COL_GK = COL_GQ + GLA_QK
COL_GV = COL_GK + GLA_QK
COL_GR = COL_GV + GLA_VW
COL_SQ = COL_GR + GLA_VW
COL_SK = COL_SQ + SWA_QW
COL_SV = COL_SK + SWA_KVW
COL_GG = COL_SV + SWA_KVW
MXU_COLS = 256
IN_COLS_PADDED = -(-(COL_GG + LANES) // (2 * MXU_COLS)) * (2 * MXU_COLS)

EXPERT_ROWS = 256
EXPERT_X_SLOTS = 3
COMBINE_GROUP = SUBLANES
LN_SUB_ROWS = 128
GLA_STEP_ROWS = 256
GLA_SAMPLE_SEQS = 8
COPY_SLOTS = 3

NT_DIMS = (((1,), (1,)), ((), ()))
TN_DIMS = (((0,), (0,)), ((), ()))


def _params(n_axes):
    return pltpu.CompilerParams(dimension_semantics=("arbitrary",) * n_axes,
                                vmem_limit_bytes=VMEM_LIMIT_BYTES)


def _sigmoid(x):
    return 1.0 / (1.0 + jnp.exp(-x))


def _silu(x):
    return x * _sigmoid(x)


def _layer_norm(x, g, b):
    mu = jnp.mean(x, axis=-1, keepdims=True)
    xc = x - mu
    var = jnp.mean(xc * xc, axis=-1, keepdims=True)
    return xc * lax.rsqrt(var + LN_EPS) * g + b


HI_MASK = -65536


def _round_to_bf16_bits(x):
    b = lax.bitcast_convert_type(x, I32)
    return b + (0x7FFF + (lax.shift_right_logical(b, 16) & 1))


def _pack_halves(x):
    c = x.shape[1] // 2
    lo = lax.shift_right_logical(_round_to_bf16_bits(x[:, :c]), 16)
    hi = _round_to_bf16_bits(x[:, c:]) & HI_MASK
    return lo | hi


def _unpack_halves(w):
    return (lax.bitcast_convert_type(lax.shift_left(w, 16), F32),
            lax.bitcast_convert_type(w & HI_MASK, F32))


def _divisor_tile(n, pref, mult):
    t = min(pref, n)
    while t > mult and (n % t or t % mult):
        t -= mult
    assert n % t == 0 and t % mult == 0, (n, pref, mult)
    return t


def _mm_kernel(x_ref, w_ref, o_ref, xb_ref):
    @pl.when(pl.program_id(1) == 0)
    def _():
        xb_ref[...] = x_ref[...].astype(BF16)

    o_ref[...] = jnp.dot(xb_ref[...], w_ref[...].astype(BF16), preferred_element_type=F32).astype(o_ref.dtype)


def _matmul(x, w, tm, tn, out_dtype):
    m, k = x.shape
    n = w.shape[1]
    return pl.pallas_call(
        _mm_kernel,
        grid=(m // tm, n // tn),
        in_specs=[pl.BlockSpec((tm, k), lambda i, j: (i, 0)),
                  pl.BlockSpec((k, tn), lambda i, j: (0, j))],
        out_specs=pl.BlockSpec((tm, tn), lambda i, j: (i, j)),
        out_shape=jax.ShapeDtypeStruct((m, n), out_dtype),
        scratch_shapes=[pltpu.VMEM((tm, k), BF16)],
        compiler_params=_params(2),
        name="matmul",
    )(x, w)


def _split_rows_specs(tm, k, n_first_tiles, n_grid_axes):
    if n_grid_axes == 1:
        return [pl.BlockSpec((tm, k), lambda i: (jnp.minimum(i, n_first_tiles - 1), 0)),
                pl.BlockSpec((tm, k), lambda i: (jnp.maximum(i - n_first_tiles, 0), 0))]
    return [pl.BlockSpec((tm, k), lambda j, i: (jnp.minimum(i, n_first_tiles - 1), 0)),
            pl.BlockSpec((tm, k), lambda j, i: (jnp.maximum(i - n_first_tiles, 0), 0))]


def _mm_split_kernel(xa_ref, xb_ref, wt_ref, o_ref, *, n_first_tiles):
    i = pl.program_id(1)

    def product(x_ref):
        o_ref[...] = lax.dot_general(x_ref[...].astype(BF16), wt_ref[...], NT_DIMS,
                                     preferred_element_type=F32).astype(o_ref.dtype)

    @pl.when(i < n_first_tiles)
    def _():
        product(xa_ref)

    @pl.when(i >= n_first_tiles)
    def _():
        product(xb_ref)


def _matmul_split(xa, xb, w_t, tm, tn, out_dtype):
    ma, k = xa.shape
    m = ma + xb.shape[0]
    n = w_t.shape[0]
    return pl.pallas_call(
        functools.partial(_mm_split_kernel, n_first_tiles=ma // tm),
        grid=(n // tn, m // tm),
        in_specs=_split_rows_specs(tm, k, ma // tm, 2) + [pl.BlockSpec((tn, k), lambda j, i: (j, 0))],
        out_specs=pl.BlockSpec((tm, tn), lambda j, i: (i, j)),
        out_shape=jax.ShapeDtypeStruct((m, n), out_dtype),
        compiler_params=_params(2),
        name="matmul_split",
    )(xa, xb, w_t)


def _reorder_cast_kernel(src_hbm, dst_hbm, stage_ref, out_ref, in_sems, out_sems, *, chunk_pieces, chunk):
    n = len(chunk_pieces)
    slots = stage_ref.shape[0]

    def in_copies(c):
        return [pltpu.make_async_copy(src_hbm.at[pl.ds(s0, ln)], stage_ref.at[c % slots, pl.ds(o0, ln)],
                                      in_sems.at[c % slots]) for s0, o0, ln in chunk_pieces[c]]

    def out_copy(c):
        return pltpu.make_async_copy(out_ref.at[c % slots], dst_hbm.at[pl.ds(c * chunk, chunk)],
                                     out_sems.at[c % slots])

    for c in range(min(slots - 1, n)):
        for cp in in_copies(c):
            cp.start()
    for c in range(n):
        if c + slots - 1 < n:
            for cp in in_copies(c + slots - 1):
                cp.start()
        for cp in in_copies(c):
            cp.wait()
        if c >= slots:
            out_copy(c - slots).wait()
        covered = sum(ln for _, _, ln in chunk_pieces[c])
        if covered < chunk:
            stage_ref[c % slots, covered:, :] = jnp.zeros((chunk - covered, stage_ref.shape[-1]),
                                                          stage_ref.dtype)
        out_ref[c % slots] = stage_ref[c % slots].astype(out_ref.dtype)
        out_copy(c).start()
    for c in range(max(n - slots, 0), n):
        out_copy(c).wait()


def _reorder_cast(src, runs, n_out_rows, chunk):
    k = src.shape[1]
    chunk_pieces = []
    for c0 in range(0, n_out_rows, chunk):
        pieces = []
        for s0, d0, ln in runs:
            lo, hi = max(d0, c0), min(d0 + ln, c0 + chunk)
            if lo < hi:
                assert (lo - c0) % 16 == 0 and (s0 + lo - d0) % SUBLANES == 0 and (hi - lo) % 16 == 0
                pieces.append((s0 + lo - d0, lo - c0, hi - lo))
        chunk_pieces.append(pieces)
    hbm = pl.BlockSpec(memory_space=pl.ANY)
    return pl.pallas_call(
        functools.partial(_reorder_cast_kernel, chunk_pieces=chunk_pieces, chunk=chunk),
        in_specs=[hbm],
        out_specs=hbm,
        out_shape=jax.ShapeDtypeStruct((n_out_rows, k), BF16),
        scratch_shapes=[pltpu.VMEM((COPY_SLOTS, chunk, k), src.dtype), pltpu.VMEM((COPY_SLOTS, chunk, k), BF16),
                        pltpu.SemaphoreType.DMA((COPY_SLOTS,)), pltpu.SemaphoreType.DMA((COPY_SLOTS,))],
        compiler_params=pltpu.CompilerParams(vmem_limit_bytes=VMEM_LIMIT_BYTES),
        name="reorder_cast",
    )(src)


def _mm_res_ln_kernel(x_ref, w_ref, *refs, n_res, n_first_tiles):
    res_refs, (g_ref, b_ref, o_ref, *packed_ref) = refs[:n_res], refs[n_res:]
    tm = x_ref.shape[0]
    sub = min(tm, LN_SUB_ROWS)
    use_first = pl.program_id(0) < n_first_tiles
    for r in range(0, tm, sub):
        rows = slice(r, r + sub)
        y = jnp.dot(x_ref[rows, :], w_ref[...], preferred_element_type=F32)
        res = res_refs[0][rows, :]
        if n_res == 2:
            res = jnp.where(use_first, res, res_refs[1][rows, :])
        h = _layer_norm(DEEPNORM_ALPHA * res + y, g_ref[...], b_ref[...])
        o_ref[rows, :] = h
        if packed_ref:
            groups = slice(r // SUBLANES, (r + sub) // SUBLANES)
            packed_ref[0][groups] = _pack_halves(h).reshape(sub // SUBLANES, SUBLANES, h.shape[1] // 2)


def _matmul_res_ln(x, w, res, g, b, tm, emit_packed=False):
    m, k = x.shape
    n = w.shape[1]
    if isinstance(res, tuple):
        n_first_tiles = res[0].shape[0] // tm
        res_specs = _split_rows_specs(tm, n, n_first_tiles, 1)
    else:
        n_first_tiles = 0
        res = (res,)
        res_specs = [pl.BlockSpec((tm, n), lambda i: (i, 0))]
    out_specs = [pl.BlockSpec((tm, n), lambda i: (i, 0))]
    out_shape = [jax.ShapeDtypeStruct((m, n), F32)]
    if emit_packed:
        out_specs.append(pl.BlockSpec((tm // SUBLANES, SUBLANES, n // 2), lambda i: (i, 0, 0)))
        out_shape.append(jax.ShapeDtypeStruct((m // SUBLANES, SUBLANES, n // 2), I32))
    outs = pl.pallas_call(
        functools.partial(_mm_res_ln_kernel, n_res=len(res), n_first_tiles=n_first_tiles),
        grid=(m // tm,),
        in_specs=[pl.BlockSpec((tm, k), lambda i: (i, 0)),
                  pl.BlockSpec((k, n), lambda i: (0, 0))] + res_specs + [
                  pl.BlockSpec((1, n), lambda i: (0, 0)),
                  pl.BlockSpec((1, n), lambda i: (0, 0))],
        out_specs=out_specs,
        out_shape=out_shape,
        compiler_params=_params(1),
        name="matmul_res_ln",
    )(x, w, *res, g.reshape(1, n), b.reshape(1, n))
    return outs if emit_packed else outs[0]


def _split3_bf16(x):
    hi = x.astype(BF16)
    r1 = x - hi.astype(F32)
    mid = r1.astype(BF16)
    lo = (r1 - mid.astype(F32)).astype(BF16)
    return jnp.concatenate([hi, mid, lo], axis=-1)


def _gla_kernel(q_ref, k_ref, v_ref, r_ref, gg_ref, wa2_ref, ba_ref, ng_ref, s0_ref, mix_in_ref,
                o_ref, sfin_ref, st_ref, *, rows, chunk, chained):
    del mix_in_ref
    n_sub = rows // chunk
    shift = chunk.bit_length() - 1
    assert chunk == 1 << shift

    if chained:
        @pl.when(pl.program_id(1) == 0)
        def _():
            for h in range(GLA_HEADS):
                st_ref[h] = s0_ref[0, h].T

    gg = gg_ref[...]
    gg_hi = gg.astype(BF16)
    gg_lo = (gg - gg_hi.astype(F32)).astype(BF16)
    gate = jnp.dot(jnp.concatenate([gg_hi, gg_hi, gg_lo], axis=-1), wa2_ref[...],
                   preferred_element_type=F32) + ba_ref[...]
    log_a = (jnp.minimum(gate, 0.0) - jnp.log1p(jnp.exp(-jnp.abs(gate)))) / GLA_TAU

    ri = lax.broadcasted_iota(I32, (rows, rows), 0)
    ci = lax.broadcasted_iota(I32, (rows, rows), 1)
    same_chunk = lax.shift_right_logical(ri, shift) == lax.shift_right_logical(ci, shift)
    causal = same_chunk & (ri >= ci)
    ones = jnp.concatenate([jnp.where(causal, 1.0, 0.0), jnp.where(same_chunk, 1.0, 0.0)], axis=0).astype(BF16)
    sums = jnp.dot(ones, _split3_bf16(log_a), preferred_element_type=F32)
    sums = sums[:, :GLA_QK] + sums[:, GLA_QK:2 * GLA_QK] + sums[:, 2 * GLA_QK:]
    bcum = sums[:rows]
    b_last = sums[rows:]
    q = q_ref[...] * (GLA_DK ** -0.5)
    k = k_ref[...]
    q_in = (q * jnp.exp(bcum)).astype(BF16)
    k_in = (k * jnp.exp(-bcum)).astype(BF16)
    k_out = (k * jnp.exp(b_last - bcum)).astype(BF16)
    decay = jnp.exp(b_last)
    ng = ng_ref[...]
    blk_r = lax.shift_right_logical(lax.broadcasted_iota(I32, (rows, n_sub * GLA_DK), 0), shift)
    blk_c = lax.shift_right_logical(lax.broadcasted_iota(I32, (rows, n_sub * GLA_DK), 1),
                                    GLA_DK.bit_length() - 1)
    for h in range(GLA_HEADS):
        ks = slice(h * GLA_DK, (h + 1) * GLA_DK)
        vs = slice(h * GLA_DV, (h + 1) * GLA_DV)
        vh = v_ref[:, vs].astype(BF16)
        qh = q_in[:, ks]
        attn = lax.dot_general(qh, k_in[:, ks], NT_DIMS, preferred_element_type=F32)
        attn = jnp.where(causal, attn, 0.0).astype(BF16)
        o = jnp.dot(attn, vh, preferred_element_type=F32)
        ko = k_out[:, ks]
        if n_sub > 1:
            ko = jnp.where(blk_r == blk_c, jnp.concatenate([ko] * n_sub, axis=-1), 0.0)
        upd_t = lax.dot_general(vh, ko, TN_DIMS, preferred_element_type=F32)
        states = []
        if chained:
            st = st_ref[h]
        for c in range(n_sub):
            if not chained:
                st = s0_ref[c, h].T
            states.append(st.astype(BF16))
            st = st * decay[c * chunk:c * chunk + 1, ks] + upd_t[:, c * GLA_DK:(c + 1) * GLA_DK]
            if not chained:
                sfin_ref[c, h] = st.T
        if chained:
            st_ref[h] = st
        o_all = lax.dot_general(qh, jnp.concatenate(states, axis=0), NT_DIMS, preferred_element_type=F32)
        o = o + jnp.concatenate(
            [o_all[c * chunk:(c + 1) * chunk, c * GLA_DV:(c + 1) * GLA_DV] for c in range(n_sub)], axis=0)
        o = o * lax.rsqrt(jnp.mean(o * o, axis=-1, keepdims=True) + RMS_EPS) * ng
        o_ref[:, vs] = (o * _silu(r_ref[:, vs])).astype(o_ref.dtype)
    if chained:
        @pl.when(pl.program_id(1) == pl.num_programs(1) - 1)
        def _():
            for h in range(GLA_HEADS):
                sfin_ref[0, h] = st_ref[h].T


def _gla(z, wa2_stack, ba, norm_g, s0, mix_in, *, nb, n_chunks, rows, chunk, row0, chained):
    t = z.shape[0]
    rb0 = row0 // rows
    seq_per_step = 1 if chained else rows // chunk
    assert chained or n_chunks == 1
    rmap = lambda colblk: (lambda b, c: (rb0 + b * n_chunks + c, colblk))
    in_specs = [
        pl.BlockSpec((rows, GLA_QK), rmap(COL_GQ // GLA_QK)),
        pl.BlockSpec((rows, GLA_QK), rmap(COL_GK // GLA_QK)),
        pl.BlockSpec((rows, GLA_VW), rmap(COL_GV // GLA_VW)),
        pl.BlockSpec((rows, GLA_VW), rmap(COL_GR // GLA_VW)),
        pl.BlockSpec((rows, LANES), rmap(COL_GG // LANES)),
        pl.BlockSpec((3 * LANES, GLA_QK), lambda b, c: (0, 0)),
        pl.BlockSpec((1, GLA_QK), lambda b, c: (0, 0)),
        pl.BlockSpec((1, GLA_DV), lambda b, c: (0, 0)),
        pl.BlockSpec((seq_per_step, GLA_HEADS, GLA_DK, GLA_DV), lambda b, c: (b, 0, 0, 0)),
    ]
    args = [z, z, z, z, z, wa2_stack, ba.reshape(1, GLA_QK), norm_g.reshape(1, GLA_DV), s0]
    aliases = {}
    if mix_in is None:
        mix_in = jnp.zeros((SUBLANES, LANES), BF16)
        in_specs.append(pl.BlockSpec(memory_space=pl.ANY))
    else:
        in_specs.append(pl.BlockSpec(memory_space=pl.ANY))
        aliases = {len(args): 0}
    args.append(mix_in)
    mix, s_fin = pl.pallas_call(
        functools.partial(_gla_kernel, rows=rows, chunk=chunk, chained=chained),
        grid=(nb, n_chunks),
        in_specs=in_specs,
        out_specs=[pl.BlockSpec((rows, GLA_VW), rmap(0)),
                   pl.BlockSpec((seq_per_step, GLA_HEADS, GLA_DK, GLA_DV), lambda b, c: (b, 0, 0, 0))],
        out_shape=[jax.ShapeDtypeStruct((t, GLA_VW + SWA_QW), BF16),
                   jax.ShapeDtypeStruct((nb * seq_per_step, GLA_HEADS, GLA_DK, GLA_DV), F32)],
        scratch_shapes=[pltpu.VMEM((GLA_HEADS, GLA_DV, GLA_DK), F32)],
        input_output_aliases=aliases,
        compiler_params=_params(2),
        name="gla",
    )(*args)
    return mix, s_fin


def _swa_kernel(sinks_ref, q_ref, kc_ref, vc_ref, kp_ref, vp_ref, mix_in_ref, o_ref, bias_ref, *,
                rows, q_base, q_stride):
    del mix_in_ref
    i = pl.program_id(1)
    n_tables = bias_ref.shape[0]
    nk = SWA_WINDOW + rows
    sr = SWA_GROUP * rows
    assert rows & (rows - 1) == 0 and q_base % CHUNK == 0 and q_stride % CHUNK == 0
    assert n_tables == 1 or q_base + q_stride >= SWA_WINDOW
    head_in_group = lax.shift_right_logical(lax.broadcasted_iota(I32, (sr, 1), 0), rows.bit_length() - 1)

    @pl.when(jnp.logical_and(pl.program_id(0) == 0, i == 0))
    def _():
        row = lax.broadcasted_iota(I32, (sr, nk), 0)
        col = lax.broadcasted_iota(I32, (sr, nk), 1)
        for t in range(n_tables):
            q0 = q_base + t * q_stride
            qpos = q0 + (row & (rows - 1))
            kpos = q0 - SWA_WINDOW + col
            dist = jnp.abs(qpos - kpos).astype(F32)
            qc = lax.shift_right_arithmetic(qpos, CHUNK.bit_length() - 1)
            kc = lax.shift_right_arithmetic(kpos, CHUNK.bit_length() - 1)
            allowed = (kpos >= 0) & (kc <= qc) & (kc >= qc - WINDOW_CHUNKS)
            for g in range(SWA_KV_HEADS):
                slope = jnp.zeros((sr, 1), F32)
                for j in range(SWA_GROUP):
                    h = g * SWA_GROUP + j
                    slope = jnp.where(head_in_group == j, 2.0 ** (-8.0 * (h + 1) / SWA_HEADS), slope)
                bias_ref[t, g] = jnp.where(allowed, -(slope * dist), -jnp.inf)

    table = jnp.minimum(i, n_tables - 1)
    keys = jnp.concatenate([kp_ref[...], kc_ref[...]], axis=0).astype(BF16)
    vals = jnp.concatenate([vp_ref[...], vc_ref[...]], axis=0).astype(BF16)
    for g in range(SWA_KV_HEADS):
        sink = jnp.zeros((sr, 1), F32)
        for j in range(SWA_GROUP):
            sink = jnp.where(head_in_group == j, sinks_ref[g * SWA_GROUP + j], sink)
        kv = slice(g * SWA_HEAD_DIM, (g + 1) * SWA_HEAD_DIM)
        qg = jnp.concatenate(
            [q_ref[:, (g * SWA_GROUP + j) * SWA_HEAD_DIM:(g * SWA_GROUP + j + 1) * SWA_HEAD_DIM]
             for j in range(SWA_GROUP)], axis=0).astype(BF16)
        s = lax.dot_general(qg, keys[:, kv], NT_DIMS, preferred_element_type=F32)
        s = s * (SWA_HEAD_DIM ** -0.5) + bias_ref[table, g]
        m = jnp.maximum(jnp.max(s, axis=-1, keepdims=True), sink)
        p = jnp.exp(s - m)
        denom = jnp.sum(p, axis=-1, keepdims=True) + jnp.exp(sink - m)
        o = jnp.dot(p.astype(BF16), vals[:, kv], preferred_element_type=F32) / denom
        for j in range(0, SWA_GROUP, 2):
            c0 = (g * SWA_GROUP + j) * SWA_HEAD_DIM
            o_ref[:, c0:c0 + LANES] = jnp.concatenate(
                [o[j * rows:(j + 1) * rows], o[(j + 1) * rows:(j + 2) * rows]], axis=-1).astype(o_ref.dtype)


def _swa(z, sinks, k_prev, v_prev, prev_map, mix_in, *, nb, n_blocks, rows, row0, q_base, q_stride):
    rb0 = row0 // rows
    rmap = lambda colblk: (lambda b, i: (rb0 + b * n_blocks + i, colblk))
    return pl.pallas_call(
        functools.partial(_swa_kernel, rows=rows, q_base=q_base, q_stride=q_stride),
        grid=(nb, n_blocks),
        in_specs=[pl.BlockSpec(memory_space=pltpu.SMEM),
                  pl.BlockSpec((rows, SWA_QW), rmap(COL_SQ // SWA_QW)),
                  pl.BlockSpec((rows, SWA_KVW), rmap(COL_SK // SWA_KVW)),
                  pl.BlockSpec((rows, SWA_KVW), rmap(COL_SV // SWA_KVW)),
                  pl.BlockSpec((SWA_WINDOW, SWA_KVW), prev_map[0]),
                  pl.BlockSpec((SWA_WINDOW, SWA_KVW), prev_map[1]),
                  pl.BlockSpec(memory_space=pl.ANY)],
        out_specs=pl.BlockSpec((rows, SWA_QW), rmap(GLA_VW // SWA_QW)),
        out_shape=jax.ShapeDtypeStruct(mix_in.shape, mix_in.dtype),
        scratch_shapes=[pltpu.VMEM((min(2, n_blocks), SWA_KV_HEADS, SWA_GROUP * rows, SWA_WINDOW + rows), F32)],
        input_output_aliases={6: 0},
        compiler_params=_params(2),
        name="swa",
    )(sinks, z, z, z, k_prev, v_prev, mix_in)


def _mem_attn_kernel(q_ref, k_ref, v_ref, o_in_ref, o_ref, *, head_dim):
    del o_in_ref
    for h in range(MEM_HEADS):
        hs = slice(h * head_dim, (h + 1) * head_dim)
        kh = k_ref[:, hs].astype(BF16)
        vh = v_ref[:, hs].astype(BF16)
        s = lax.dot_general(q_ref[:, hs], kh, NT_DIMS, preferred_element_type=F32) * (head_dim ** -0.5)
        m = jnp.max(s, axis=-1, keepdims=True)
        p = jnp.exp(s - m)
        denom = jnp.sum(p, axis=-1, keepdims=True)
        o = jnp.dot(p.astype(BF16), vh, preferred_element_type=F32) / denom
        o_ref[:, hs] = o.astype(o_ref.dtype)


def _mem_attn(q, mk, mv, o_in, *, nb, n_tiles, rows, row0, mem_tokens):
    t, d = q.shape
    rb0 = row0 // rows
    qmap = lambda b, i: (rb0 + b * n_tiles + i, 0)
    in_specs = [pl.BlockSpec((rows, d), qmap),
                pl.BlockSpec((mem_tokens, d), lambda b, i: (b, 0)),
                pl.BlockSpec((mem_tokens, d), lambda b, i: (b, 0)),
                pl.BlockSpec(memory_space=pl.ANY)]
    aliases = {}
    if o_in is None:
        o_in = jnp.zeros((SUBLANES, LANES), BF16)
    else:
        aliases = {3: 0}
    return pl.pallas_call(
        functools.partial(_mem_attn_kernel, head_dim=d // MEM_HEADS),
        grid=(nb, n_tiles),
        in_specs=in_specs,
        out_specs=pl.BlockSpec((rows, d), qmap),
        out_shape=jax.ShapeDtypeStruct((t, d), BF16),
        input_output_aliases=aliases,
        compiler_params=_params(2),
        name="mem_attn",
    )(q, mk, mv, o_in)


def _pick_first_max(cur, iota, sentinel):
    mx = jnp.max(cur, axis=0, keepdims=True)
    first = jnp.min(jnp.where(cur == mx, iota, sentinel), axis=0, keepdims=True)
    return iota == first, first


def _router_kernel(x_ref, wh_ref, wl_ref, bias_ref, idx_ref, gate_ref, rank_ref, cnt_ref, run_ref, before_ref,
                   *, tm):
    i = pl.program_id(0)

    @pl.when(i == 0)
    def _():
        run_ref[...] = jnp.zeros_like(run_ref)
        ti = lax.broadcasted_iota(I32, (tm, tm), 0)
        tj = lax.broadcasted_iota(I32, (tm, tm), 1)
        before_ref[...] = jnp.where(ti < tj, 1.0, 0.0).astype(BF16)

    x = x_ref[...]
    xh = x.astype(BF16)
    xl = (x - xh.astype(F32)).astype(BF16)
    wh = wh_ref[...]
    logits = (lax.dot_general(wh, xh, NT_DIMS, preferred_element_type=F32)
              + lax.dot_general(wh, xl, NT_DIMS, preferred_element_type=F32)
              + lax.dot_general(wl_ref[...], xh, NT_DIMS, preferred_element_type=F32))
    scores = _sigmoid(logits)
    sel = scores + bias_ref[...]
    neg_inf = -jnp.inf

    li = lax.broadcasted_iota(I32, (GROUP_SIZE, tm), 0)
    grp_rows = []
    for g in range(N_GROUPS):
        blk = sel[g * GROUP_SIZE:(g + 1) * GROUP_SIZE, :]
        pick, _ = _pick_first_max(blk, li, GROUP_SIZE)
        m1 = jnp.max(blk, axis=0, keepdims=True)
        m2 = jnp.max(jnp.where(pick, neg_inf, blk), axis=0, keepdims=True)
        grp_rows.append(m1 + m2)
    grp = jnp.concatenate(grp_rows, axis=0)

    gi = lax.broadcasted_iota(I32, (N_GROUPS, tm), 0)
    gsel = jnp.zeros((N_GROUPS, tm), F32)
    cur = grp
    for _ in range(TOPK_GROUPS):
        pick, _ = _pick_first_max(cur, gi, N_GROUPS)
        gsel = jnp.where(pick, 1.0, gsel)
        cur = jnp.where(pick, neg_inf, cur)
    emask = jnp.concatenate(
        [jnp.broadcast_to(gsel[g:g + 1, :], (GROUP_SIZE, tm)) for g in range(N_GROUPS)], axis=0)

    ei = lax.broadcasted_iota(I32, (N_EXPERTS, tm), 0)
    cur = jnp.where(emask > 0.5, sel, neg_inf)
    chosen = jnp.zeros((N_EXPERTS, tm), F32)
    idx_rows, w_rows = [], []
    for _ in range(TOP_K):
        pick, first = _pick_first_max(cur, ei, N_EXPERTS)
        idx_rows.append(first)
        w_rows.append(jnp.sum(jnp.where(pick, scores, 0.0), axis=0, keepdims=True))
        chosen = jnp.where(pick, 1.0, chosen)
        cur = jnp.where(pick, neg_inf, cur)
    idx = jnp.concatenate(idx_rows, axis=0)
    w = jnp.concatenate(w_rows, axis=0)
    gate_ref[...] = w / jnp.sum(w, axis=0, keepdims=True) * ROUTED_SCALE
    idx_ref[...] = idx

    local = jnp.dot(chosen.astype(BF16), before_ref[...], preferred_element_type=F32)
    total = local + run_ref[:, 0:1]
    rank_rows = [jnp.sum(jnp.where(ei == idx_rows[k], total, 0.0), axis=0, keepdims=True)
                 for k in range(TOP_K)]
    rank_ref[...] = jnp.concatenate(rank_rows, axis=0).astype(I32)
    run_ref[...] = run_ref[...] + jnp.sum(chosen, axis=1, keepdims=True)
    cnt_ref[...] = run_ref[...].astype(I32)


def _router(x, w_hi_t, w_lo_t, bias, tm):
    t, d = x.shape
    return pl.pallas_call(
        functools.partial(_router_kernel, tm=tm),
        grid=(t // tm,),
        in_specs=[pl.BlockSpec((tm, d), lambda i: (i, 0)),
                  pl.BlockSpec((N_EXPERTS, d), lambda i: (0, 0)),
                  pl.BlockSpec((N_EXPERTS, d), lambda i: (0, 0)),
                  pl.BlockSpec((N_EXPERTS, 1), lambda i: (0, 0))],
        out_specs=[pl.BlockSpec((TOP_K, tm), lambda i: (0, i)),
                   pl.BlockSpec((TOP_K, tm), lambda i: (0, i)),
                   pl.BlockSpec((TOP_K, tm), lambda i: (0, i)),
                   pl.BlockSpec((N_EXPERTS, LANES), lambda i: (0, 0))],
        out_shape=[jax.ShapeDtypeStruct((TOP_K, t), I32),
                   jax.ShapeDtypeStruct((TOP_K, t), F32),
                   jax.ShapeDtypeStruct((TOP_K, t), I32),
                   jax.ShapeDtypeStruct((N_EXPERTS, LANES), I32)],
        scratch_shapes=[pltpu.VMEM((N_EXPERTS, LANES), F32), pltpu.VMEM((tm, tm), BF16)],
        compiler_params=_params(1),
        name="router",
    )(x, w_hi_t, w_lo_t, bias.reshape(N_EXPERTS, 1))


PAD_BITS = tuple(1 << s for s in reversed(range(EXPERT_ROWS.bit_length() - 1)))


def _dispatch_kernel(zstart_ref, zcount_ref, dest_ref, x_ref, swg_ref, swu_ref, swd_ref, xs_ref, shared_ref,
                     zero_ref, sem, zsem, *, tm):
    i = pl.program_id(0)

    def start_group(g, carry):
        for j in range(SUBLANES):
            for k in range(TOP_K):
                d = dest_ref[k * tm + g * SUBLANES + j]
                pltpu.make_async_copy(x_ref.at[g, pl.ds(j, 1), :], xs_ref.at[d], sem).start(priority=k % 2)
        return carry

    lax.fori_loop(0, tm // SUBLANES, start_group, 0)

    half = x_ref.shape[-1]
    x_lo, x_hi = _unpack_halves(x_ref[...].reshape(tm, half))
    xb = jnp.concatenate([x_lo.astype(BF16), x_hi.astype(BF16)], axis=-1)
    hg = jnp.dot(xb, swg_ref[...], preferred_element_type=F32)
    hu = jnp.dot(xb, swu_ref[...], preferred_element_type=F32)
    shared_ref[...] = jnp.dot((_silu(hg) * hu).astype(BF16), swd_ref[...],
                              preferred_element_type=F32).astype(shared_ref.dtype)

    for k in range(TOP_K):
        pltpu.make_async_copy(xs_ref.at[pl.ds(0, tm)], xs_ref.at[pl.ds(0, tm)], sem).wait()

    @pl.when(i == pl.num_programs(0) - 1)
    def _():
        zero_ref[...] = jnp.zeros_like(zero_ref)

        def fill(e, wait):
            pos = zstart_ref[e]
            cnt = zcount_ref[e]
            for bit in PAD_BITS:
                has = (cnt & bit) != 0

                @pl.when(has)
                def _():
                    cp = pltpu.make_async_copy(zero_ref.at[pl.ds(0, bit)], xs_ref.at[pl.ds(pos, bit)], zsem)
                    if wait:
                        cp.wait()
                    else:
                        cp.start()

                pos = pos + jnp.where(has, bit, 0)

        def fill_start(e, carry):
            fill(e, False)
            return carry

        def fill_wait(e, carry):
            fill(e, True)
            return carry

        lax.fori_loop(0, N_EXPERTS, fill_start, 0)
        lax.fori_loop(0, N_EXPERTS, fill_wait, 0)


def _dispatch(x, dest_tiles, zstart, zcount, swg, swu, swd, n_rows, tm):
    t = x.shape[0] * SUBLANES
    d = x.shape[2]
    ds = swg.shape[1]
    const = lambda i, zs, zc: (0, 0)
    return pl.pallas_call(
        functools.partial(_dispatch_kernel, tm=tm),
        grid_spec=pltpu.PrefetchScalarGridSpec(
            num_scalar_prefetch=2,
            grid=(t // tm,),
            in_specs=[pl.BlockSpec((TOP_K * tm,), lambda i, zs, zc: (i,), memory_space=pltpu.SMEM),
                      pl.BlockSpec((tm // SUBLANES, SUBLANES, d), lambda i, zs, zc: (i, 0, 0)),
                      pl.BlockSpec((2 * d, ds), const),
                      pl.BlockSpec((2 * d, ds), const),
                      pl.BlockSpec((ds, 2 * d), const)],
            out_specs=[pl.BlockSpec(memory_space=pl.ANY),
                       pl.BlockSpec((tm, 2 * d), lambda i, zs, zc: (i, 0))],
            scratch_shapes=[pltpu.VMEM((EXPERT_ROWS // 2, 1, d), x.dtype),
                            pltpu.SemaphoreType.DMA(()),
                            pltpu.SemaphoreType.DMA(())]),
        out_shape=[jax.ShapeDtypeStruct((n_rows, 1, d), x.dtype),
                   jax.ShapeDtypeStruct((t, 2 * d), BF16)],
        compiler_params=_params(1),
        name="moe_dispatch",
    )(zstart, zcount, dest_tiles, x, swg, swu, swd)


def _expert_kernel(be_ref, nxt_ref, short_ref, nv_ref, xs_hbm, wg_hbm, wu_hbm, wd_hbm, ys_hbm,
                   wgs_ref, wus_ref, wds_ref, wgb_ref, wub_ref, wdb_ref, xbuf_ref, ybuf_ref,
                   wsems, xsems, ysems):
    b = pl.program_id(0)
    n_valid = nv_ref[0]
    e = be_ref[b]
    prev = be_ref[jnp.maximum(b - 1, 0)]
    fresh = jnp.logical_or(b == 0, e != prev)
    slot = lax.rem(b, 2)

    def weight_copies(expert):
        return (pltpu.make_async_copy(wg_hbm.at[expert], wgs_ref, wsems.at[0]),
                pltpu.make_async_copy(wu_hbm.at[expert], wus_ref, wsems.at[1]),
                pltpu.make_async_copy(wd_hbm.at[expert], wds_ref, wsems.at[2]))

    def rows_in(blk, to_slot):
        r0 = pl.multiple_of(blk * EXPERT_ROWS, EXPERT_ROWS)
        return pltpu.make_async_copy(xs_hbm.at[pl.ds(r0, EXPERT_ROWS), 0, :], xbuf_ref.at[to_slot],
                                     xsems.at[to_slot])

    def rows_out(blk, from_slot):
        r0 = pl.multiple_of(blk * EXPERT_ROWS, EXPERT_ROWS)
        return pltpu.make_async_copy(ybuf_ref.at[from_slot], ys_hbm.at[pl.ds(r0, EXPERT_ROWS), 0, :],
                                     ysems.at[from_slot])

    x_slots = xbuf_ref.shape[0]
    xslot = lax.rem(b, x_slots)

    @pl.when(b == 0)
    def _():
        for cp in weight_copies(e):
            cp.start(priority=1)
        for ahead in range(x_slots - 1):
            @pl.when(ahead < n_valid)
            def _():
                rows_in(ahead, ahead).start()

    @pl.when(fresh)
    def _():
        for cp in weight_copies(e):
            cp.wait()
        wgb_ref[...] = wgs_ref[...].astype(BF16)
        wub_ref[...] = wus_ref[...].astype(BF16)
        wdb_ref[...] = wds_ref[...].astype(BF16)
        nxt = nxt_ref[b]

        @pl.when(nxt >= 0)
        def _():
            for cp in weight_copies(nxt):
                cp.start(priority=1)

    @pl.when(b < n_valid)
    def _():
        rows_in(b, xslot).wait()

        @pl.when(b + x_slots - 1 < n_valid)
        def _():
            rows_in(b + x_slots - 1, lax.rem(b + x_slots - 1, x_slots)).start()

        @pl.when(b >= 2)
        def _():
            rows_out(b - 2, slot).wait()

        def swiglu_rows(rows):
            half = xbuf_ref.shape[-1]
            x_lo, x_hi = _unpack_halves(xbuf_ref[xslot, :rows, :])
            xb = jnp.concatenate([x_lo.astype(BF16), x_hi.astype(BF16)], axis=-1)
            hg = jnp.dot(xb, wgb_ref[...], preferred_element_type=F32)
            hu = jnp.dot(xb, wub_ref[...], preferred_element_type=F32)
            act = (_silu(hg) * hu).astype(BF16)
            ybuf_ref[slot, :rows, :] = _pack_halves(jnp.dot(act, wdb_ref[...], preferred_element_type=F32))
            if rows < EXPERT_ROWS:
                ybuf_ref[slot, rows:, :] = jnp.zeros((EXPERT_ROWS - rows, half), ybuf_ref.dtype)

        pl.when(short_ref[b] == 1)(functools.partial(swiglu_rows, EXPERT_ROWS // 2))
        pl.when(short_ref[b] == 0)(functools.partial(swiglu_rows, EXPERT_ROWS))
        rows_out(b, slot).start()

        @pl.when(b == n_valid - 1)
        def _():
            rows_out(b, slot).wait()

            @pl.when(b >= 1)
            def _():
                rows_out(b - 1, 1 - slot).wait()


def _experts(xs, wg, wu, wd, block_expert, next_expert, short_block, n_valid):
    n_rows = xs.shape[0]
    d = wg.shape[1]
    de = wg.shape[2]
    nb = n_rows // EXPERT_ROWS
    hbm = pl.BlockSpec(memory_space=pl.ANY)
    return pl.pallas_call(
        _expert_kernel,
        grid_spec=pltpu.PrefetchScalarGridSpec(
            num_scalar_prefetch=4,
            grid=(nb,),
            in_specs=[hbm, hbm, hbm, hbm],
            out_specs=hbm,
            scratch_shapes=[pltpu.VMEM((d, de), F32), pltpu.VMEM((d, de), F32), pltpu.VMEM((de, d), F32),
                            pltpu.VMEM((d, de), BF16), pltpu.VMEM((d, de), BF16), pltpu.VMEM((de, d), BF16),
                            pltpu.VMEM((EXPERT_X_SLOTS, EXPERT_ROWS, d // 2), I32),
                            pltpu.VMEM((2, EXPERT_ROWS, d // 2), I32),
                            pltpu.SemaphoreType.DMA((3,)), pltpu.SemaphoreType.DMA((EXPERT_X_SLOTS,)),
                            pltpu.SemaphoreType.DMA((2,))]),
        out_shape=jax.ShapeDtypeStruct((n_rows, 1, d // 2), I32),
        compiler_params=_params(1),
        name="moe_experts",
    )(block_expert, next_expert, short_block, n_valid, xs, wg, wu, wd)


def _combine_kernel(dest0_ref, destn_ref, x_ref, shared_ref, gate_ref, ys_ref, g_ref, b_ref,
                    op_ref, os_ref, buf_ref, acc_ref, sems, *, tm, n_prompt_tiles):
    i = pl.program_id(0)
    n = pl.num_programs(0)
    half = buf_ref.shape[-1]
    n_groups = tm // COMBINE_GROUP

    def issue_group(dest_ref, to_slot, g):
        for j in range(COMBINE_GROUP):
            t = g * COMBINE_GROUP + j
            for k in range(TOP_K):
                d = dest_ref[k * tm + t]
                pltpu.make_async_copy(ys_ref.at[d], buf_ref.at[to_slot, k, g, pl.ds(j, 1), :],
                                      sems.at[to_slot]).start(priority=k % 2)

    @pl.when(i == 0)
    def _():
        def first(g, carry):
            issue_group(dest0_ref, 0, g)
            return carry
        lax.fori_loop(0, n_groups, first, 0)

    def run_tile(slot):
        for k in range(TOP_K):
            pltpu.make_async_copy(ys_ref.at[pl.ds(0, tm)], ys_ref.at[pl.ds(0, tm)], sems.at[slot]).wait()

        def reduce_group(g):
            r0 = pl.multiple_of(g * COMBINE_GROUP, COMBINE_GROUP)
            gate = gate_ref[pl.ds(r0, COMBINE_GROUP), :]
            acc_lo = acc_hi = None
            for k in range(TOP_K):
                y_lo, y_hi = _unpack_halves(buf_ref[slot, k, g])
                gk = gate[:, k:k + 1]
                acc_lo = gk * y_lo if k == 0 else acc_lo + gk * y_lo
                acc_hi = gk * y_hi if k == 0 else acc_hi + gk * y_hi
            acc_ref[pl.ds(r0, COMBINE_GROUP), :half] = acc_lo
            acc_ref[pl.ds(r0, COMBINE_GROUP), half:] = acc_hi

        @pl.when(i + 1 < n)
        def _():
            def body(g, carry):
                issue_group(destn_ref, 1 - slot, g)
                reduce_group(g)
                return carry
            lax.fori_loop(0, n_groups, body, 0)

        @pl.when(i + 1 >= n)
        def _():
            def body(g, carry):
                reduce_group(g)
                return carry
            lax.fori_loop(0, n_groups, body, 0)

    for s in range(2):
        pl.when(lax.rem(i, 2) == s)(functools.partial(run_tile, s))

    x = x_ref[...]
    out = _layer_norm(DEEPNORM_ALPHA * x + (acc_ref[...] + shared_ref[...].astype(F32)),
                      g_ref[...], b_ref[...])

    @pl.when(i < n_prompt_tiles)
    def _():
        op_ref[...] = out

    @pl.when(i >= n_prompt_tiles)
    def _():
        os_ref[...] = out


def _combine(x, shared, gate_t, dest_tiles, ys, g, b, tm, n_prompt_rows):
    t, d = x.shape
    n_tiles = t // tm
    npt = n_prompt_rows // tm
    const = lambda i: (0, 0)
    return pl.pallas_call(
        functools.partial(_combine_kernel, tm=tm, n_prompt_tiles=npt),
        grid=(n_tiles,),
        in_specs=[pl.BlockSpec((TOP_K * tm,), lambda i: (0,), memory_space=pltpu.SMEM),
                  pl.BlockSpec((TOP_K * tm,), lambda i: (jnp.minimum(i + 1, n_tiles - 1),),
                               memory_space=pltpu.SMEM),
                  pl.BlockSpec((tm, d), lambda i: (i, 0)),
                  pl.BlockSpec((tm, d), lambda i: (i, 0)),
                  pl.BlockSpec((tm, TOP_K), lambda i: (i, 0)),
                  pl.BlockSpec(memory_space=pl.ANY),
                  pl.BlockSpec((1, d), const),
                  pl.BlockSpec((1, d), const)],
        out_specs=[pl.BlockSpec((tm, d), lambda i: (jnp.minimum(i, npt - 1), 0)),
                   pl.BlockSpec((tm, d), lambda i: (jnp.maximum(i - npt, 0), 0))],
        out_shape=[jax.ShapeDtypeStruct((n_prompt_rows, d), F32),
                   jax.ShapeDtypeStruct((t - n_prompt_rows, d), F32)],
        scratch_shapes=[pltpu.VMEM((2, TOP_K, tm // COMBINE_GROUP, COMBINE_GROUP, d // 2), I32),
                        pltpu.VMEM((tm, d), F32),
                        pltpu.SemaphoreType.DMA((2,))],
        compiler_params=_params(1),
        name="moe_combine",
    )(dest_tiles, dest_tiles, x, shared, gate_t, ys, g.reshape(1, d), b.reshape(1, d))


def _tile_major(a, tm):
    k, t = a.shape
    return a.reshape(k, t // tm, tm).transpose(1, 0, 2).reshape(-1)


def _moe_ln(x, x_packed, router_w, router_bias, wg, wu, wd, swg, swu, swd, g, b, n_prompt_rows):
    t, d = x.shape
    w_t = router_w.T
    w_hi = w_t.astype(BF16)
    w_lo = (w_t - w_hi.astype(F32)).astype(BF16)
    tr = _divisor_tile(t, 512, LANES)
    idx, gate, rank, cnt = _router(x, w_hi, w_lo, router_bias, tr)

    counts = cnt[:, 0]
    padded = (counts + EXPERT_ROWS - 1) // EXPERT_ROWS * EXPERT_ROWS
    pad_end = jnp.cumsum(padded)
    pad_start = pad_end - padded
    experts = jnp.arange(N_EXPERTS, dtype=I32)
    onehot = idx[None] == experts[:, None, None]
    dest = jnp.sum(jnp.where(onehot, pad_start[:, None, None], 0), axis=0) + rank
    n_blocks = (t * TOP_K + N_EXPERTS * (EXPERT_ROWS - 1)) // EXPERT_ROWS + 1
    n_valid = (pad_end[-1] // EXPERT_ROWS).astype(I32)
    blk = jnp.minimum(jnp.arange(n_blocks, dtype=I32), n_valid - 1)
    block_expert = jnp.minimum(
        jnp.sum(pad_end[None, :] <= (blk * EXPERT_ROWS)[:, None], axis=1), N_EXPERTS - 1).astype(I32)
    later_used = (experts[None, :] > experts[:, None]) & (counts[None, :] > 0)
    next_used = jnp.min(jnp.where(later_used, experts[None, :], N_EXPERTS), axis=1)
    next_used = jnp.where(next_used == N_EXPERTS, -1, next_used)
    per_block = lambda table: jnp.sum(
        jnp.where(block_expert[:, None] == experts[None, :], table[None, :], 0), axis=1).astype(I32)
    next_expert = per_block(next_used)
    rows_used = per_block(pad_start + counts) - blk * EXPERT_ROWS
    short_block = (rows_used <= EXPERT_ROWS // 2).astype(I32)

    td = _divisor_tile(t, 256, SUBLANES)
    xs, shared = _dispatch(x_packed, _tile_major(dest, td), (pad_start + counts).astype(I32),
                           (padded - counts).astype(I32), swg.astype(BF16), swu.astype(BF16),
                           swd.astype(BF16), n_blocks * EXPERT_ROWS, td)
    ys = _experts(xs, wg, wu, wd, block_expert, next_expert, short_block, n_valid.reshape(1))
    tc = _divisor_tile(math.gcd(n_prompt_rows, t - n_prompt_rows), 128, SUBLANES)
    return _combine(x, shared, gate.T, _tile_major(dest, tc), ys, g, b, tc, n_prompt_rows)


def kernel(x_prompt, x_sample, mem_prompt, state_gla, cache_swa_k, cache_swa_v, cache_mem_k, cache_mem_v,
           w_in, gla_w_gate2, gla_b_gate, gla_norm_g, swa_sinks, w_mix_out, ln1_g, ln1_b,
           mem_wq, mem_wk, mem_wv, mem_wo, ln2_g, ln2_b,
           router_w, router_bias, exp_w_gate, exp_w_up, exp_w_down, sh_w_gate, sh_w_up, sh_w_down,
           ln3_g, ln3_b):
    nbp, seq, d = x_prompt.shape
    nbs, dec, _ = x_sample.shape
    tp, ts = nbp * seq, nbs * dec
    t = tp + ts
    mem_tokens = mem_prompt.shape[1]
    assert state_gla.shape[0] == DEPTH and seq % SWA_WINDOW == 0 and dec % SUBLANES == 0
    assert cache_swa_k.shape[2] == SWA_WINDOW and tp % dec == 0
    tm = _divisor_tile(math.gcd(tp, ts), 512, 16)
    l = 0

    xp = x_prompt.reshape(tp, d)
    xs = x_sample.reshape(ts, d)

    gate_src = GLA_QK + GLA_QK + GLA_VW + GLA_VW
    swa_rows = SWA_QW + 2 * SWA_KVW
    w_in_rt = _reorder_cast(
        w_in[l].T,
        [(0, 0, gate_src),
         (gate_src + GLA_GATE_RANK, COL_SQ, swa_rows),
         (gate_src, COL_GG, GLA_GATE_RANK)],
        IN_COLS_PADDED, _divisor_tile(IN_COLS_PADDED, 512, 16))
    z = _matmul_split(xp, xs, w_in_rt, tm, IN_COLS_PADDED // 2, F32)

    wa2_pad = jnp.concatenate(
        [gla_w_gate2[l], jnp.zeros((LANES - GLA_GATE_RANK, GLA_QK), F32)], axis=0)
    wa2_hi = wa2_pad.astype(BF16)
    wa2_lo = (wa2_pad - wa2_hi.astype(F32)).astype(BF16)
    wa2_stack = jnp.concatenate([wa2_hi, wa2_lo, wa2_hi], axis=0)
    zero_state = jnp.zeros((nbp, GLA_HEADS, GLA_DK, GLA_DV), F32)
    gla_rows = _divisor_tile(seq, GLA_STEP_ROWS, CHUNK)
    mix, st_p = _gla(z, wa2_stack, gla_b_gate[l], gla_norm_g[l], zero_state, None,
                     nb=nbp, n_chunks=seq // gla_rows, rows=gla_rows, chunk=CHUNK, row0=0, chained=True)
    seqs = _divisor_tile(nbs, GLA_SAMPLE_SEQS, 1)
    assert tp % (seqs * dec) == 0
    mix, st_s = _gla(z, wa2_stack, gla_b_gate[l], gla_norm_g[l], state_gla[l], mix,
                     nb=nbs // seqs, n_chunks=1, rows=seqs * dec, chunk=dec, row0=tp, chained=False)
    nblk = seq // SWA_WINDOW
    prev_p = lambda col: (lambda b, i: (b * nblk + jnp.maximum(i - 1, 0), col))
    mix = _swa(z, swa_sinks[l], z, z, (prev_p(COL_SK // SWA_KVW), prev_p(COL_SV // SWA_KVW)), mix,
               nb=nbp, n_blocks=nblk, rows=SWA_WINDOW, row0=0, q_base=0, q_stride=SWA_WINDOW)
    ck = cache_swa_k[l].reshape(nbs * SWA_WINDOW, SWA_KVW)
    cv = cache_swa_v[l].reshape(nbs * SWA_WINDOW, SWA_KVW)
    prev_s = lambda b, i: (b, 0)
    mix = _swa(z, swa_sinks[l], ck, cv, (prev_s, prev_s), mix,
               nb=nbs, n_blocks=1, rows=dec, row0=tp, q_base=PAST_LEN, q_stride=0)
    h1 = _matmul_res_ln(mix, w_mix_out[l].astype(BF16), (xp, xs), ln1_g[l], ln1_b[l], tm)

    mem = mem_prompt.reshape(nbp * mem_tokens, d)
    tmem = _divisor_tile(nbp * mem_tokens, 512, 16)
    mk = _matmul(mem, mem_wk[l], tmem, _divisor_tile(d, 512, LANES), F32)
    mv = _matmul(mem, mem_wv[l], tmem, _divisor_tile(d, 512, LANES), F32)
    q = _matmul(h1, mem_wq[l].astype(BF16), tm, d, BF16)
    tq = _divisor_tile(seq, 512, 16)
    att = _mem_attn(q, mk, mv, None, nb=nbp, n_tiles=seq // tq, rows=tq, row0=0, mem_tokens=mem_tokens)
    att = _mem_attn(q, cache_mem_k[l].reshape(nbs * mem_tokens, d), cache_mem_v[l].reshape(nbs * mem_tokens, d),
                    att, nb=nbs, n_tiles=1, rows=dec, row0=tp, mem_tokens=mem_tokens)
    h2, h2_packed = _matmul_res_ln(att, mem_wo[l].astype(BF16), h1, ln2_g[l], ln2_b[l], tm, emit_packed=True)

    out_p, out_s = _moe_ln(h2, h2_packed, router_w[l], router_bias[l], exp_w_gate[l], exp_w_up[l],
                           exp_w_down[l], sh_w_gate[l], sh_w_up[l], sh_w_down[l], ln3_g[l], ln3_b[l], tp)

    y_prompt = out_p.reshape(nbp, seq, d)
    y_sample = out_s.reshape(nbs, dec, d)
    kv = z[:, COL_SK:COL_SK + 2 * SWA_KVW]
    kvp = kv[:tp].reshape(nbp, seq, 2 * SWA_KVW)[:, seq - SWA_WINDOW:]
    kvs = kv[tp:].reshape(nbs, dec, 2 * SWA_KVW)
    kv_shape = lambda a: a.reshape(a.shape[0], a.shape[1], SWA_KV_HEADS, SWA_HEAD_DIM)[None]
    mem_shape = lambda a: a.reshape(nbp, mem_tokens, MEM_HEADS, d // MEM_HEADS)[None]
    return (y_prompt, y_sample,
            st_p[None],
            kv_shape(kvp[..., :SWA_KVW]), kv_shape(kvp[..., SWA_KVW:]),
            mem_shape(mk), mem_shape(mv),
            st_s[None],
            kv_shape(kvs[..., :SWA_KVW]), kv_shape(kvs[..., SWA_KVW:]))
```

```python
import functools
import math

import jax
import jax.numpy as jnp
from jax import lax
from jax.experimental import pallas as pl
from jax.experimental.pallas import tpu as pltpu

F32 = jnp.float32
BF16 = jnp.bfloat16
I32 = jnp.int32

CHUNK = 64
PAST_LEN = 2048
GLA_HEADS = 8
GLA_DK = 64
GLA_DV = 128
GLA_QK = GLA_HEADS * GLA_DK
GLA_VW = GLA_HEADS * GLA_DV
GLA_GATE_RANK = 16
GLA_TAU = 16.0
SWA_HEADS = 16
SWA_KV_HEADS = 2
SWA_GROUP = SWA_HEADS // SWA_KV_HEADS
SWA_HEAD_DIM = 64
SWA_QW = SWA_HEADS * SWA_HEAD_DIM
SWA_KVW = SWA_KV_HEADS * SWA_HEAD_DIM
SWA_WINDOW = 128
WINDOW_CHUNKS = SWA_WINDOW // CHUNK
MEM_HEADS = 4
N_EXPERTS = 64
TOP_K = 8
N_GROUPS = 8
GROUP_SIZE = N_EXPERTS // N_GROUPS
TOPK_GROUPS = 4
ROUTED_SCALE = 2.5
DEPTH = 1
DEEPNORM_ALPHA = (2 * DEPTH) ** 0.25
LN_EPS = 1e-5
RMS_EPS = 1e-6

LANES = 128
SUBLANES = 8
VMEM_LIMIT_BYTES = 56 * 1024 * 1024

COL_GQ = 0
COL_GK = COL_GQ + GLA_QK
COL_GV = COL_GK + GLA_QK
COL_GR = COL_GV + GLA_VW
COL_SQ = COL_GR + GLA_VW
COL_SK = COL_SQ + SWA_QW
COL_SV = COL_SK + SWA_KVW
COL_GG = COL_SV + SWA_KVW
MXU_COLS = 256
IN_COLS_PADDED = -(-(COL_GG + LANES) // (2 * MXU_COLS)) * (2 * MXU_COLS)

EXPERT_ROWS = 256
EXPERT_X_SLOTS = 3
COMBINE_GROUP = SUBLANES
LN_SUB_ROWS = 128
GLA_STEP_ROWS = 256
GLA_SAMPLE_SEQS = 8
COPY_SLOTS = 3

NT_DIMS = (((1,), (1,)), ((), ()))
TN_DIMS = (((0,), (0,)), ((), ()))


def _params(n_axes):
    return pltpu.CompilerParams(dimension_semantics=("arbitrary",) * n_axes,
                                vmem_limit_bytes=VMEM_LIMIT_BYTES)


def _sigmoid(x):
    return 1.0 / (1.0 + jnp.exp(-x))


def _silu(x):
    return x * _sigmoid(x)


def _layer_norm(x, g, b):
    mu = jnp.mean(x, axis=-1, keepdims=True)
    xc = x - mu
    var = jnp.mean(xc * xc, axis=-1, keepdims=True)
    return xc * lax.rsqrt(var + LN_EPS) * g + b


HI_MASK = -65536


def _round_to_bf16_bits(x):
    b = lax.bitcast_convert_type(x, I32)
    return b + (0x7FFF + (lax.shift_right_logical(b, 16) & 1))


def _pack_halves(x):
    c = x.shape[1] // 2
    lo = lax.shift_right_logical(_round_to_bf16_bits(x[:, :c]), 16)
    hi = _round_to_bf16_bits(x[:, c:]) & HI_MASK
    return lo | hi


def _unpack_halves(w):
    return (lax.bitcast_convert_type(lax.shift_left(w, 16), F32),
            lax.bitcast_convert_type(w & HI_MASK, F32))


def _divisor_tile(n, pref, mult):
    t = min(pref, n)
    while t > mult and (n % t or t % mult):
        t -= mult
    assert n % t == 0 and t % mult == 0, (n, pref, mult)
    return t


def _mm_kernel(x_ref, w_ref, o_ref, xb_ref):
    @pl.when(pl.program_id(1) == 0)
    def _():
        xb_ref[...] = x_ref[...].astype(BF16)

    o_ref[...] = jnp.dot(xb_ref[...], w_ref[...].astype(BF16), preferred_element_type=F32).astype(o_ref.dtype)


def _matmul(x, w, tm, tn, out_dtype):
    m, k = x.shape
    n = w.shape[1]
    return pl.pallas_call(
        _mm_kernel,
        grid=(m // tm, n // tn),
        in_specs=[pl.BlockSpec((tm, k), lambda i, j: (i, 0)),
                  pl.BlockSpec((k, tn), lambda i, j: (0, j))],
        out_specs=pl.BlockSpec((tm, tn), lambda i, j: (i, j)),
        out_shape=jax.ShapeDtypeStruct((m, n), out_dtype),
        scratch_shapes=[pltpu.VMEM((tm, k), BF16)],
        compiler_params=_params(2),
        name="matmul",
    )(x, w)


def _split_rows_specs(tm, k, n_first_tiles, n_grid_axes):
    if n_grid_axes == 1:
        return [pl.BlockSpec((tm, k), lambda i: (jnp.minimum(i, n_first_tiles - 1), 0)),
                pl.BlockSpec((tm, k), lambda i: (jnp.maximum(i - n_first_tiles, 0), 0))]
    return [pl.BlockSpec((tm, k), lambda j, i: (jnp.minimum(i, n_first_tiles - 1), 0)),
            pl.BlockSpec((tm, k), lambda j, i: (jnp.maximum(i - n_first_tiles, 0), 0))]


def _mm_split_kernel(xa_ref, xb_ref, wt_ref, o_ref, *, n_first_tiles):
    i = pl.program_id(1)

    def product(x_ref):
        o_ref[...] = lax.dot_general(x_ref[...].astype(BF16), wt_ref[...], NT_DIMS,
                                     preferred_element_type=F32).astype(o_ref.dtype)

    @pl.when(i < n_first_tiles)
    def _():
        product(xa_ref)

    @pl.when(i >= n_first_tiles)
    def _():
        product(xb_ref)


def _matmul_split(xa, xb, w_t, tm, tn, out_dtype):
    ma, k = xa.shape
    m = ma + xb.shape[0]
    n = w_t.shape[0]
    return pl.pallas_call(
        functools.partial(_mm_split_kernel, n_first_tiles=ma // tm),
        grid=(n // tn, m // tm),
        in_specs=_split_rows_specs(tm, k, ma // tm, 2) + [pl.BlockSpec((tn, k), lambda j, i: (j, 0))],
        out_specs=pl.BlockSpec((tm, tn), lambda j, i: (i, j)),
        out_shape=jax.ShapeDtypeStruct((m, n), out_dtype),
        compiler_params=_params(2),
        name="matmul_split",
    )(xa, xb, w_t)


def _reorder_cast_kernel(src_hbm, dst_hbm, stage_ref, out_ref, in_sems, out_sems, *, chunk_pieces, chunk):
    n = len(chunk_pieces)
    slots = stage_ref.shape[0]

    def in_copies(c):
        return [pltpu.make_async_copy(src_hbm.at[pl.ds(s0, ln)], stage_ref.at[c % slots, pl.ds(o0, ln)],
                                      in_sems.at[c % slots]) for s0, o0, ln in chunk_pieces[c]]

    def out_copy(c):
        return pltpu.make_async_copy(out_ref.at[c % slots], dst_hbm.at[pl.ds(c * chunk, chunk)],
                                     out_sems.at[c % slots])

    for c in range(min(slots - 1, n)):
        for cp in in_copies(c):
            cp.start()
    for c in range(n):
        if c + slots - 1 < n:
            for cp in in_copies(c + slots - 1):
                cp.start()
        for cp in in_copies(c):
            cp.wait()
        if c >= slots:
            out_copy(c - slots).wait()
        covered = sum(ln for _, _, ln in chunk_pieces[c])
        if covered < chunk:
            stage_ref[c % slots, covered:, :] = jnp.zeros((chunk - covered, stage_ref.shape[-1]),
                                                          stage_ref.dtype)
        out_ref[c % slots] = stage_ref[c % slots].astype(out_ref.dtype)
        out_copy(c).start()
    for c in range(max(n - slots, 0), n):
        out_copy(c).wait()


def _reorder_cast(src, runs, n_out_rows, chunk):
    k = src.shape[1]
    chunk_pieces = []
    for c0 in range(0, n_out_rows, chunk):
        pieces = []
        for s0, d0, ln in runs:
            lo, hi = max(d0, c0), min(d0 + ln, c0 + chunk)
            if lo < hi:
                assert (lo - c0) % 16 == 0 and (s0 + lo - d0) % SUBLANES == 0 and (hi - lo) % 16 == 0
                pieces.append((s0 + lo - d0, lo - c0, hi - lo))
        chunk_pieces.append(pieces)
    hbm = pl.BlockSpec(memory_space=pl.ANY)
    return pl.pallas_call(
        functools.partial(_reorder_cast_kernel, chunk_pieces=chunk_pieces, chunk=chunk),
        in_specs=[hbm],
        out_specs=hbm,
        out_shape=jax.ShapeDtypeStruct((n_out_rows, k), BF16),
        scratch_shapes=[pltpu.VMEM((COPY_SLOTS, chunk, k), src.dtype), pltpu.VMEM((COPY_SLOTS, chunk, k), BF16),
                        pltpu.SemaphoreType.DMA((COPY_SLOTS,)), pltpu.SemaphoreType.DMA((COPY_SLOTS,))],
        compiler_params=pltpu.CompilerParams(vmem_limit_bytes=VMEM_LIMIT_BYTES),
        name="reorder_cast",
    )(src)


def _mm_res_ln_kernel(x_ref, w_ref, *refs, n_res, n_first_tiles):
    res_refs, (g_ref, b_ref, o_ref, *packed_ref) = refs[:n_res], refs[n_res:]
    tm = x_ref.shape[0]
    sub = min(tm, LN_SUB_ROWS)
    use_first = pl.program_id(0) < n_first_tiles
    for r in range(0, tm, sub):
        rows = slice(r, r + sub)
        y = jnp.dot(x_ref[rows, :], w_ref[...], preferred_element_type=F32)
        res = res_refs[0][rows, :]
        if n_res == 2:
            res = jnp.where(use_first, res, res_refs[1][rows, :])
        h = _layer_norm(DEEPNORM_ALPHA * res + y, g_ref[...], b_ref[...])
        o_ref[rows, :] = h
        if packed_ref:
            groups = slice(r // SUBLANES, (r + sub) // SUBLANES)
            packed_ref[0][groups] = _pack_halves(h).reshape(sub // SUBLANES, SUBLANES, h.shape[1] // 2)


def _matmul_res_ln(x, w, res, g, b, tm, emit_packed=False):
    m, k = x.shape
    n = w.shape[1]
    if isinstance(res, tuple):
        n_first_tiles = res[0].shape[0] // tm
        res_specs = _split_rows_specs(tm, n, n_first_tiles, 1)
    else:
        n_first_tiles = 0
        res = (res,)
        res_specs = [pl.BlockSpec((tm, n), lambda i: (i, 0))]
    out_specs = [pl.BlockSpec((tm, n), lambda i: (i, 0))]
    out_shape = [jax.ShapeDtypeStruct((m, n), F32)]
    if emit_packed:
        out_specs.append(pl.BlockSpec((tm // SUBLANES, SUBLANES, n // 2), lambda i: (i, 0, 0)))
        out_shape.append(jax.ShapeDtypeStruct((m // SUBLANES, SUBLANES, n // 2), I32))
    outs = pl.pallas_call(
        functools.partial(_mm_res_ln_kernel, n_res=len(res), n_first_tiles=n_first_tiles),
        grid=(m // tm,),
        in_specs=[pl.BlockSpec((tm, k), lambda i: (i, 0)),
                  pl.BlockSpec((k, n), lambda i: (0, 0))] + res_specs + [
                  pl.BlockSpec((1, n), lambda i: (0, 0)),
                  pl.BlockSpec((1, n), lambda i: (0, 0))],
        out_specs=out_specs,
        out_shape=out_shape,
        compiler_params=_params(1),
        name="matmul_res_ln",
    )(x, w, *res, g.reshape(1, n), b.reshape(1, n))
    return outs if emit_packed else outs[0]


def _split3_bf16(x):
    hi = x.astype(BF16)
    r1 = x - hi.astype(F32)
    mid = r1.astype(BF16)
    lo = (r1 - mid.astype(F32)).astype(BF16)
    return jnp.concatenate([hi, mid, lo], axis=-1)


def _gla_kernel(q_ref, k_ref, v_ref, r_ref, gg_ref, wa2_ref, ba_ref, ng_ref, s0_ref, mix_in_ref,
                o_ref, sfin_ref, st_ref, *, rows, chunk, chained):
    del mix_in_ref
    n_sub = rows // chunk
    shift = chunk.bit_length() - 1
    assert chunk == 1 << shift

    if chained:
        @pl.when(pl.program_id(1) == 0)
        def _():
            for h in range(GLA_HEADS):
                st_ref[h] = s0_ref[0, h].T

    gg = gg_ref[...]
    gg_hi = gg.astype(BF16)
    gg_lo = (gg - gg_hi.astype(F32)).astype(BF16)
    gate = jnp.dot(jnp.concatenate([gg_hi, gg_hi, gg_lo], axis=-1), wa2_ref[...],
                   preferred_element_type=F32) + ba_ref[...]
    log_a = (jnp.minimum(gate, 0.0) - jnp.log1p(jnp.exp(-jnp.abs(gate)))) / GLA_TAU

    ri = lax.broadcasted_iota(I32, (rows, rows), 0)
    ci = lax.broadcasted_iota(I32, (rows, rows), 1)
    same_chunk = lax.shift_right_logical(ri, shift) == lax.shift_right_logical(ci, shift)
    causal = same_chunk & (ri >= ci)
    ones = jnp.concatenate([jnp.where(causal, 1.0, 0.0), jnp.where(same_chunk, 1.0, 0.0)], axis=0).astype(BF16)
    sums = jnp.dot(ones, _split3_bf16(log_a), preferred_element_type=F32)
    sums = sums[:, :GLA_QK] + sums[:, GLA_QK:2 * GLA_QK] + sums[:, 2 * GLA_QK:]
    bcum = sums[:rows]
    b_last = sums[rows:]
    q = q_ref[...] * (GLA_DK ** -0.5)
    k = k_ref[...]
    q_in = (q * jnp.exp(bcum)).astype(BF16)
    k_in = (k * jnp.exp(-bcum)).astype(BF16)
    k_out = (k * jnp.exp(b_last - bcum)).astype(BF16)
    decay = jnp.exp(b_last)
    ng = ng_ref[...]
    blk_r = lax.shift_right_logical(lax.broadcasted_iota(I32, (rows, n_sub * GLA_DK), 0), shift)
    blk_c = lax.shift_right_logical(lax.broadcasted_iota(I32, (rows, n_sub * GLA_DK), 1),
                                    GLA_DK.bit_length() - 1)
    for h in range(GLA_HEADS):
        ks = slice(h * GLA_DK, (h + 1) * GLA_DK)
        vs = slice(h * GLA_DV, (h + 1) * GLA_DV)
        vh = v_ref[:, vs].astype(BF16)
        qh = q_in[:, ks]
        attn = lax.dot_general(qh, k_in[:, ks], NT_DIMS, preferred_element_type=F32)
        attn = jnp.where(causal, attn, 0.0).astype(BF16)
        o = jnp.dot(attn, vh, preferred_element_type=F32)
        ko = k_out[:, ks]
        if n_sub > 1:
            ko = jnp.where(blk_r == blk_c, jnp.concatenate([ko] * n_sub, axis=-1), 0.0)
        upd_t = lax.dot_general(vh, ko, TN_DIMS, preferred_element_type=F32)
        states = []
        if chained:
            st = st_ref[h]
        for c in range(n_sub):
            if not chained:
                st = s0_ref[c, h].T
            states.append(st.astype(BF16))
            st = st * decay[c * chunk:c * chunk + 1, ks] + upd_t[:, c * GLA_DK:(c + 1) * GLA_DK]
            if not chained:
                sfin_ref[c, h] = st.T
        if chained:
            st_ref[h] = st
        o_all = lax.dot_general(qh, jnp.concatenate(states, axis=0), NT_DIMS, preferred_element_type=F32)
        o = o + jnp.concatenate(
            [o_all[c * chunk:(c + 1) * chunk, c * GLA_DV:(c + 1) * GLA_DV] for c in range(n_sub)], axis=0)
        o = o * lax.rsqrt(jnp.mean(o * o, axis=-1, keepdims=True) + RMS_EPS) * ng
        o_ref[:, vs] = (o * _silu(r_ref[:, vs])).astype(o_ref.dtype)
    if chained:
        @pl.when(pl.program_id(1) == pl.num_programs(1) - 1)
        def _():
            for h in range(GLA_HEADS):
                sfin_ref[0, h] = st_ref[h].T


def _gla(z, wa2_stack, ba, norm_g, s0, mix_in, *, nb, n_chunks, rows, chunk, row0, chained):
    t = z.shape[0]
    rb0 = row0 // rows
    seq_per_step = 1 if chained else rows // chunk
    assert chained or n_chunks == 1
    rmap = lambda colblk: (lambda b, c: (rb0 + b * n_chunks + c, colblk))
    in_specs = [
        pl.BlockSpec((rows, GLA_QK), rmap(COL_GQ // GLA_QK)),
        pl.BlockSpec((rows, GLA_QK), rmap(COL_GK // GLA_QK)),
        pl.BlockSpec((rows, GLA_VW), rmap(COL_GV // GLA_VW)),
        pl.BlockSpec((rows, GLA_VW), rmap(COL_GR // GLA_VW)),
        pl.BlockSpec((rows, LANES), rmap(COL_GG // LANES)),
        pl.BlockSpec((3 * LANES, GLA_QK), lambda b, c: (0, 0)),
        pl.BlockSpec((1, GLA_QK), lambda b, c: (0, 0)),
        pl.BlockSpec((1, GLA_DV), lambda b, c: (0, 0)),
        pl.BlockSpec((seq_per_step, GLA_HEADS, GLA_DK, GLA_DV), lambda b, c: (b, 0, 0, 0)),
    ]
    args = [z, z, z, z, z, wa2_stack, ba.reshape(1, GLA_QK), norm_g.reshape(1, GLA_DV), s0]
    aliases = {}
    if mix_in is None:
        mix_in = jnp.zeros((SUBLANES, LANES), BF16)
        in_specs.append(pl.BlockSpec(memory_space=pl.ANY))
    else:
        in_specs.append(pl.BlockSpec(memory_space=pl.ANY))
        aliases = {len(args): 0}
    args.append(mix_in)
    mix, s_fin = pl.pallas_call(
        functools.partial(_gla_kernel, rows=rows, chunk=chunk, chained=chained),
        grid=(nb, n_chunks),
        in_specs=in_specs,
        out_specs=[pl.BlockSpec((rows, GLA_VW), rmap(0)),
                   pl.BlockSpec((seq_per_step, GLA_HEADS, GLA_DK, GLA_DV), lambda b, c: (b, 0, 0, 0))],
        out_shape=[jax.ShapeDtypeStruct((t, GLA_VW + SWA_QW), BF16),
                   jax.ShapeDtypeStruct((nb * seq_per_step, GLA_HEADS, GLA_DK, GLA_DV), F32)],
        scratch_shapes=[pltpu.VMEM((GLA_HEADS, GLA_DV, GLA_DK), F32)],
        input_output_aliases=aliases,
        compiler_params=_params(2),
        name="gla",
    )(*args)
    return mix, s_fin


def _swa_kernel(sinks_ref, q_ref, kc_ref, vc_ref, kp_ref, vp_ref, mix_in_ref, o_ref, bias_ref, *,
                rows, q_base, q_stride):
    del mix_in_ref
    i = pl.program_id(1)
    n_tables = bias_ref.shape[0]
    nk = SWA_WINDOW + rows
    sr = SWA_GROUP * rows
    assert rows & (rows - 1) == 0 and q_base % CHUNK == 0 and q_stride % CHUNK == 0
    assert n_tables == 1 or q_base + q_stride >= SWA_WINDOW
    head_in_group = lax.shift_right_logical(lax.broadcasted_iota(I32, (sr, 1), 0), rows.bit_length() - 1)

    @pl.when(jnp.logical_and(pl.program_id(0) == 0, i == 0))
    def _():
        row = lax.broadcasted_iota(I32, (sr, nk), 0)
        col = lax.broadcasted_iota(I32, (sr, nk), 1)
        for t in range(n_tables):
            q0 = q_base + t * q_stride
            qpos = q0 + (row & (rows - 1))
            kpos = q0 - SWA_WINDOW + col
            dist = jnp.abs(qpos - kpos).astype(F32)
            qc = lax.shift_right_arithmetic(qpos, CHUNK.bit_length() - 1)
            kc = lax.shift_right_arithmetic(kpos, CHUNK.bit_length() - 1)
            allowed = (kpos >= 0) & (kc <= qc) & (kc >= qc - WINDOW_CHUNKS)
            for g in range(SWA_KV_HEADS):
                slope = jnp.zeros((sr, 1), F32)
                for j in range(SWA_GROUP):
                    h = g * SWA_GROUP + j
                    slope = jnp.where(head_in_group == j, 2.0 ** (-8.0 * (h + 1) / SWA_HEADS), slope)
                bias_ref[t, g] = jnp.where(allowed, -(slope * dist), -jnp.inf)

    table = jnp.minimum(i, n_tables - 1)
    keys = jnp.concatenate([kp_ref[...], kc_ref[...]], axis=0).astype(BF16)
    vals = jnp.concatenate([vp_ref[...], vc_ref[...]], axis=0).astype(BF16)
    for g in range(SWA_KV_HEADS):
        sink = jnp.zeros((sr, 1), F32)
        for j in range(SWA_GROUP):
            sink = jnp.where(head_in_group == j, sinks_ref[g * SWA_GROUP + j], sink)
        kv = slice(g * SWA_HEAD_DIM, (g + 1) * SWA_HEAD_DIM)
        qg = jnp.concatenate(
            [q_ref[:, (g * SWA_GROUP + j) * SWA_HEAD_DIM:(g * SWA_GROUP + j + 1) * SWA_HEAD_DIM]
             for j in range(SWA_GROUP)], axis=0).astype(BF16)
        s = lax.dot_general(qg, keys[:, kv], NT_DIMS, preferred_element_type=F32)
        s = s * (SWA_HEAD_DIM ** -0.5) + bias_ref[table, g]
        m = jnp.maximum(jnp.max(s, axis=-1, keepdims=True), sink)
        p = jnp.exp(s - m)
        denom = jnp.sum(p, axis=-1, keepdims=True) + jnp.exp(sink - m)
        o = jnp.dot(p.astype(BF16), vals[:, kv], preferred_element_type=F32) / denom
        for j in range(0, SWA_GROUP, 2):
            c0 = (g * SWA_GROUP + j) * SWA_HEAD_DIM
            o_ref[:, c0:c0 + LANES] = jnp.concatenate(
                [o[j * rows:(j + 1) * rows], o[(j + 1) * rows:(j + 2) * rows]], axis=-1).astype(o_ref.dtype)


def _swa(z, sinks, k_prev, v_prev, prev_map, mix_in, *, nb, n_blocks, rows, row0, q_base, q_stride):
    rb0 = row0 // rows
    rmap = lambda colblk: (lambda b, i: (rb0 + b * n_blocks + i, colblk))
    return pl.pallas_call(
        functools.partial(_swa_kernel, rows=rows, q_base=q_base, q_stride=q_stride),
        grid=(nb, n_blocks),
        in_specs=[pl.BlockSpec(memory_space=pltpu.SMEM),
                  pl.BlockSpec((rows, SWA_QW), rmap(COL_SQ // SWA_QW)),
                  pl.BlockSpec((rows, SWA_KVW), rmap(COL_SK // SWA_KVW)),
                  pl.BlockSpec((rows, SWA_KVW), rmap(COL_SV // SWA_KVW)),
                  pl.BlockSpec((SWA_WINDOW, SWA_KVW), prev_map[0]),
                  pl.BlockSpec((SWA_WINDOW, SWA_KVW), prev_map[1]),
                  pl.BlockSpec(memory_space=pl.ANY)],
        out_specs=pl.BlockSpec((rows, SWA_QW), rmap(GLA_VW // SWA_QW)),
        out_shape=jax.ShapeDtypeStruct(mix_in.shape, mix_in.dtype),
        scratch_shapes=[pltpu.VMEM((min(2, n_blocks), SWA_KV_HEADS, SWA_GROUP * rows, SWA_WINDOW + rows), F32)],
        input_output_aliases={6: 0},
        compiler_params=_params(2),
        name="swa",
    )(sinks, z, z, z, k_prev, v_prev, mix_in)


def _mem_attn_kernel(q_ref, k_ref, v_ref, o_in_ref, o_ref, *, head_dim):
    del o_in_ref
    for h in range(MEM_HEADS):
        hs = slice(h * head_dim, (h + 1) * head_dim)
        kh = k_ref[:, hs].astype(BF16)
        vh = v_ref[:, hs].astype(BF16)
        s = lax.dot_general(q_ref[:, hs], kh, NT_DIMS, preferred_element_type=F32) * (head_dim ** -0.5)
        m = jnp.max(s, axis=-1, keepdims=True)
        p = jnp.exp(s - m)
        denom = jnp.sum(p, axis=-1, keepdims=True)
        o = jnp.dot(p.astype(BF16), vh, preferred_element_type=F32) / denom
        o_ref[:, hs] = o.astype(o_ref.dtype)


def _mem_attn(q, mk, mv, o_in, *, nb, n_tiles, rows, row0, mem_tokens):
    t, d = q.shape
    rb0 = row0 // rows
    qmap = lambda b, i: (rb0 + b * n_tiles + i, 0)
    in_specs = [pl.BlockSpec((rows, d), qmap),
                pl.BlockSpec((mem_tokens, d), lambda b, i: (b, 0)),
                pl.BlockSpec((mem_tokens, d), lambda b, i: (b, 0)),
                pl.BlockSpec(memory_space=pl.ANY)]
    aliases = {}
    if o_in is None:
        o_in = jnp.zeros((SUBLANES, LANES), BF16)
    else:
        aliases = {3: 0}
    return pl.pallas_call(
        functools.partial(_mem_attn_kernel, head_dim=d // MEM_HEADS),
        grid=(nb, n_tiles),
        in_specs=in_specs,
        out_specs=pl.BlockSpec((rows, d), qmap),
        out_shape=jax.ShapeDtypeStruct((t, d), BF16),
        input_output_aliases=aliases,
        compiler_params=_params(2),
        name="mem_attn",
    )(q, mk, mv, o_in)


def _pick_first_max(cur, iota, sentinel):
    mx = jnp.max(cur, axis=0, keepdims=True)
    first = jnp.min(jnp.where(cur == mx, iota, sentinel), axis=0, keepdims=True)
    return iota == first, first


def _router_kernel(x_ref, wh_ref, wl_ref, bias_ref, idx_ref, gate_ref, rank_ref, cnt_ref, run_ref, before_ref,
                   *, tm):
    i = pl.program_id(0)

    @pl.when(i == 0)
    def _():
        run_ref[...] = jnp.zeros_like(run_ref)
        ti = lax.broadcasted_iota(I32, (tm, tm), 0)
        tj = lax.broadcasted_iota(I32, (tm, tm), 1)
        before_ref[...] = jnp.where(ti < tj, 1.0, 0.0).astype(BF16)

    x = x_ref[...]
    xh = x.astype(BF16)
    xl = (x - xh.astype(F32)).astype(BF16)
    wh = wh_ref[...]
    logits = (lax.dot_general(wh, xh, NT_DIMS, preferred_element_type=F32)
              + lax.dot_general(wh, xl, NT_DIMS, preferred_element_type=F32)
              + lax.dot_general(wl_ref[...], xh, NT_DIMS, preferred_element_type=F32))
    scores = _sigmoid(logits)
    sel = scores + bias_ref[...]
    neg_inf = -jnp.inf

    li = lax.broadcasted_iota(I32, (GROUP_SIZE, tm), 0)
    grp_rows = []
    for g in range(N_GROUPS):
        blk = sel[g * GROUP_SIZE:(g + 1) * GROUP_SIZE, :]
        pick, _ = _pick_first_max(blk, li, GROUP_SIZE)
        m1 = jnp.max(blk, axis=0, keepdims=True)
        m2 = jnp.max(jnp.where(pick, neg_inf, blk), axis=0, keepdims=True)
        grp_rows.append(m1 + m2)
    grp = jnp.concatenate(grp_rows, axis=0)

    gi = lax.broadcasted_iota(I32, (N_GROUPS, tm), 0)
    gsel = jnp.zeros((N_GROUPS, tm), F32)
    cur = grp
    for _ in range(TOPK_GROUPS):
        pick, _ = _pick_first_max(cur, gi, N_GROUPS)
        gsel = jnp.where(pick, 1.0, gsel)
        cur = jnp.where(pick, neg_inf, cur)
    emask = jnp.concatenate(
        [jnp.broadcast_to(gsel[g:g + 1, :], (GROUP_SIZE, tm)) for g in range(N_GROUPS)], axis=0)

    ei = lax.broadcasted_iota(I32, (N_EXPERTS, tm), 0)
    cur = jnp.where(emask > 0.5, sel, neg_inf)
    chosen = jnp.zeros((N_EXPERTS, tm), F32)
    idx_rows, w_rows = [], []
    for _ in range(TOP_K):
        pick, first = _pick_first_max(cur, ei, N_EXPERTS)
        idx_rows.append(first)
        w_rows.append(jnp.sum(jnp.where(pick, scores, 0.0), axis=0, keepdims=True))
        chosen = jnp.where(pick, 1.0, chosen)
        cur = jnp.where(pick, neg_inf, cur)
    idx = jnp.concatenate(idx_rows, axis=0)
    w = jnp.concatenate(w_rows, axis=0)
    gate_ref[...] = w / jnp.sum(w, axis=0, keepdims=True) * ROUTED_SCALE
    idx_ref[...] = idx

    local = jnp.dot(chosen.astype(BF16), before_ref[...], preferred_element_type=F32)
    total = local + run_ref[:, 0:1]
    rank_rows = [jnp.sum(jnp.where(ei == idx_rows[k], total, 0.0), axis=0, keepdims=True)
                 for k in range(TOP_K)]
    rank_ref[...] = jnp.concatenate(rank_rows, axis=0).astype(I32)
    run_ref[...] = run_ref[...] + jnp.sum(chosen, axis=1, keepdims=True)
    cnt_ref[...] = run_ref[...].astype(I32)


def _router(x, w_hi_t, w_lo_t, bias, tm):
    t, d = x.shape
    return pl.pallas_call(
        functools.partial(_router_kernel, tm=tm),
        grid=(t // tm,),
        in_specs=[pl.BlockSpec((tm, d), lambda i: (i, 0)),
                  pl.BlockSpec((N_EXPERTS, d), lambda i: (0, 0)),
                  pl.BlockSpec((N_EXPERTS, d), lambda i: (0, 0)),
                  pl.BlockSpec((N_EXPERTS, 1), lambda i: (0, 0))],
        out_specs=[pl.BlockSpec((TOP_K, tm), lambda i: (0, i)),
                   pl.BlockSpec((TOP_K, tm), lambda i: (0, i)),
                   pl.BlockSpec((TOP_K, tm), lambda i: (0, i)),
                   pl.BlockSpec((N_EXPERTS, LANES), lambda i: (0, 0))],
        out_shape=[jax.ShapeDtypeStruct((TOP_K, t), I32),
                   jax.ShapeDtypeStruct((TOP_K, t), F32),
                   jax.ShapeDtypeStruct((TOP_K, t), I32),
                   jax.ShapeDtypeStruct((N_EXPERTS, LANES), I32)],
        scratch_shapes=[pltpu.VMEM((N_EXPERTS, LANES), F32), pltpu.VMEM((tm, tm), BF16)],
        compiler_params=_params(1),
        name="router",
    )(x, w_hi_t, w_lo_t, bias.reshape(N_EXPERTS, 1))


PAD_BITS = tuple(1 << s for s in reversed(range(EXPERT_ROWS.bit_length() - 1)))


def _dispatch_kernel(zstart_ref, zcount_ref, dest_ref, x_ref, swg_ref, swu_ref, swd_ref, xs_ref, shared_ref,
                     zero_ref, sem, zsem, *, tm):
    i = pl.program_id(0)

    def start_group(g, carry):
        for j in range(SUBLANES):
            for k in range(TOP_K):
                d = dest_ref[k * tm + g * SUBLANES + j]
                pltpu.make_async_copy(x_ref.at[g, pl.ds(j, 1), :], xs_ref.at[d], sem).start(priority=k % 2)
        return carry

    lax.fori_loop(0, tm // SUBLANES, start_group, 0)

    half = x_ref.shape[-1]
    x_lo, x_hi = _unpack_halves(x_ref[...].reshape(tm, half))
    xb = jnp.concatenate([x_lo.astype(BF16), x_hi.astype(BF16)], axis=-1)
    hg = jnp.dot(xb, swg_ref[...], preferred_element_type=F32)
    hu = jnp.dot(xb, swu_ref[...], preferred_element_type=F32)
    shared_ref[...] = jnp.dot((_silu(hg) * hu).astype(BF16), swd_ref[...],
                              preferred_element_type=F32).astype(shared_ref.dtype)

    for k in range(TOP_K):
        pltpu.make_async_copy(xs_ref.at[pl.ds(0, tm)], xs_ref.at[pl.ds(0, tm)], sem).wait()

    @pl.when(i == pl.num_programs(0) - 1)
    def _():
        zero_ref[...] = jnp.zeros_like(zero_ref)

        def fill(e, wait):
            pos = zstart_ref[e]
            cnt = zcount_ref[e]
            for bit in PAD_BITS:
                has = (cnt & bit) != 0

                @pl.when(has)
                def _():
                    cp = pltpu.make_async_copy(zero_ref.at[pl.ds(0, bit)], xs_ref.at[pl.ds(pos, bit)], zsem)
                    if wait:
                        cp.wait()
                    else:
                        cp.start()

                pos = pos + jnp.where(has, bit, 0)

        def fill_start(e, carry):
            fill(e, False)
            return carry

        def fill_wait(e, carry):
            fill(e, True)
            return carry

        lax.fori_loop(0, N_EXPERTS, fill_start, 0)
        lax.fori_loop(0, N_EXPERTS, fill_wait, 0)


def _dispatch(x, dest_tiles, zstart, zcount, swg, swu, swd, n_rows, tm):
    t = x.shape[0] * SUBLANES
    d = x.shape[2]
    ds = swg.shape[1]
    const = lambda i, zs, zc: (0, 0)
    return pl.pallas_call(
        functools.partial(_dispatch_kernel, tm=tm),
        grid_spec=pltpu.PrefetchScalarGridSpec(
            num_scalar_prefetch=2,
            grid=(t // tm,),
            in_specs=[pl.BlockSpec((TOP_K * tm,), lambda i, zs, zc: (i,), memory_space=pltpu.SMEM),
                      pl.BlockSpec((tm // SUBLANES, SUBLANES, d), lambda i, zs, zc: (i, 0, 0)),
                      pl.BlockSpec((2 * d, ds), const),
                      pl.BlockSpec((2 * d, ds), const),
                      pl.BlockSpec((ds, 2 * d), const)],
            out_specs=[pl.BlockSpec(memory_space=pl.ANY),
                       pl.BlockSpec((tm, 2 * d), lambda i, zs, zc: (i, 0))],
            scratch_shapes=[pltpu.VMEM((EXPERT_ROWS // 2, 1, d), x.dtype),
                            pltpu.SemaphoreType.DMA(()),
                            pltpu.SemaphoreType.DMA(())]),
        out_shape=[jax.ShapeDtypeStruct((n_rows, 1, d), x.dtype),
                   jax.ShapeDtypeStruct((t, 2 * d), BF16)],
        compiler_params=_params(1),
        name="moe_dispatch",
    )(zstart, zcount, dest_tiles, x, swg, swu, swd)


def _expert_kernel(be_ref, nxt_ref, nxt2_ref, par_ref, short_ref, nv_ref, xs_hbm, wg_hbm, wu_hbm, wd_hbm, ys_hbm,
                   wgs_ref, wus_ref, wds_ref, wgb_ref, wub_ref, wdb_ref, xbuf_ref, ybuf_ref,
                   wsems, xsems, ysems):
    b = pl.program_id(0)
    n_valid = nv_ref[0]
    e = be_ref[b]
    prev = be_ref[jnp.maximum(b - 1, 0)]
    fresh = jnp.logical_or(b == 0, e != prev)
    slot = lax.rem(b, 2)
    wset = par_ref[b]

    def weight_copies(expert, to_set):
        return (pltpu.make_async_copy(wg_hbm.at[expert], wgs_ref.at[to_set], wsems.at[to_set, 0]),
                pltpu.make_async_copy(wu_hbm.at[expert], wus_ref.at[to_set], wsems.at[to_set, 1]),
                pltpu.make_async_copy(wd_hbm.at[expert], wds_ref.at[to_set], wsems.at[to_set, 2]))

    def rows_in(blk, to_slot):
        r0 = pl.multiple_of(blk * EXPERT_ROWS, EXPERT_ROWS)
        return pltpu.make_async_copy(xs_hbm.at[pl.ds(r0, EXPERT_ROWS), 0, :], xbuf_ref.at[to_slot],
                                     xsems.at[to_slot])

    def rows_out(blk, from_slot):
        r0 = pl.multiple_of(blk * EXPERT_ROWS, EXPERT_ROWS)
        return pltpu.make_async_copy(ybuf_ref.at[from_slot], ys_hbm.at[pl.ds(r0, EXPERT_ROWS), 0, :],
                                     ysems.at[from_slot])

    x_slots = xbuf_ref.shape[0]
    xslot = lax.rem(b, x_slots)

    @pl.when(b == 0)
    def _():
        for cp in weight_copies(e, wset):
            cp.start(priority=1)
        nxt = nxt_ref[b]

        @pl.when(nxt >= 0)
        def _():
            for cp in weight_copies(nxt, 1 - wset):
                cp.start(priority=1)

        for ahead in range(x_slots - 1):
            @pl.when(ahead < n_valid)
            def _():
                rows_in(ahead, ahead).start()

    @pl.when(fresh)
    def _():
        for cp in weight_copies(e, wset):
            cp.wait()
        wgb_ref[...] = wgs_ref[wset].astype(BF16)
        wub_ref[...] = wus_ref[wset].astype(BF16)
        wdb_ref[...] = wds_ref[wset].astype(BF16)
        nxt2 = nxt2_ref[b]

        @pl.when(nxt2 >= 0)
        def _():
            for cp in weight_copies(nxt2, wset):
                cp.start(priority=1)

    @pl.when(b < n_valid)
    def _():
        rows_in(b, xslot).wait()

        @pl.when(b + x_slots - 1 < n_valid)
        def _():
            rows_in(b + x_slots - 1, lax.rem(b + x_slots - 1, x_slots)).start()

        @pl.when(b >= 2)
        def _():
            rows_out(b - 2, slot).wait()

        def swiglu_rows(rows):
            half = xbuf_ref.shape[-1]
            x_lo, x_hi = _unpack_halves(xbuf_ref[xslot, :rows, :])
            xb = jnp.concatenate([x_lo.astype(BF16), x_hi.astype(BF16)], axis=-1)
            hg = jnp.dot(xb, wgb_ref[...], preferred_element_type=F32)
            hu = jnp.dot(xb, wub_ref[...], preferred_element_type=F32)
            act = (_silu(hg) * hu).astype(BF16)
            ybuf_ref[slot, :rows, :] = _pack_halves(jnp.dot(act, wdb_ref[...], preferred_element_type=F32))
            if rows < EXPERT_ROWS:
                ybuf_ref[slot, rows:, :] = jnp.zeros((EXPERT_ROWS - rows, half), ybuf_ref.dtype)

        pl.when(short_ref[b] == 1)(functools.partial(swiglu_rows, EXPERT_ROWS // 2))
        pl.when(short_ref[b] == 0)(functools.partial(swiglu_rows, EXPERT_ROWS))
        rows_out(b, slot).start()

        @pl.when(b == n_valid - 1)
        def _():
            rows_out(b, slot).wait()

            @pl.when(b >= 1)
            def _():
                rows_out(b - 1, 1 - slot).wait()


def _experts(xs, wg, wu, wd, block_expert, next_expert, next2_expert, set_parity, short_block, n_valid):
    n_rows = xs.shape[0]
    d = wg.shape[1]
    de = wg.shape[2]
    nb = n_rows // EXPERT_ROWS
    hbm = pl.BlockSpec(memory_space=pl.ANY)
    return pl.pallas_call(
        _expert_kernel,
        grid_spec=pltpu.PrefetchScalarGridSpec(
            num_scalar_prefetch=6,
            grid=(nb,),
            in_specs=[hbm, hbm, hbm, hbm],
            out_specs=hbm,
            scratch_shapes=[pltpu.VMEM((2, d, de), F32), pltpu.VMEM((2, d, de), F32), pltpu.VMEM((2, de, d), F32),
                            pltpu.VMEM((d, de), BF16), pltpu.VMEM((d, de), BF16), pltpu.VMEM((de, d), BF16),
                            pltpu.VMEM((EXPERT_X_SLOTS, EXPERT_ROWS, d // 2), I32),
                            pltpu.VMEM((2, EXPERT_ROWS, d // 2), I32),
                            pltpu.SemaphoreType.DMA((2, 3)), pltpu.SemaphoreType.DMA((EXPERT_X_SLOTS,)),
                            pltpu.SemaphoreType.DMA((2,))]),
        out_shape=jax.ShapeDtypeStruct((n_rows, 1, d // 2), I32),
        compiler_params=_params(1),
        name="moe_experts",
    )(block_expert, next_expert, next2_expert, set_parity, short_block, n_valid, xs, wg, wu, wd)


def _combine_kernel(dest0_ref, destn_ref, x_ref, shared_ref, gate_ref, ys_ref, g_ref, b_ref,
                    op_ref, os_ref, buf_ref, acc_ref, sems, *, tm, n_prompt_tiles):
    i = pl.program_id(0)
    n = pl.num_programs(0)
    half = buf_ref.shape[-1]
    n_groups = tm // COMBINE_GROUP

    def issue_group(dest_ref, to_slot, g):
        for j in range(COMBINE_GROUP):
            t = g * COMBINE_GROUP + j
            for k in range(TOP_K):
                d = dest_ref[k * tm + t]
                pltpu.make_async_copy(ys_ref.at[d], buf_ref.at[to_slot, k, g, pl.ds(j, 1), :],
                                      sems.at[to_slot]).start(priority=k % 2)

    @pl.when(i == 0)
    def _():
        def first(g, carry):
            issue_group(dest0_ref, 0, g)
            return carry
        lax.fori_loop(0, n_groups, first, 0)

    def run_tile(slot):
        for k in range(TOP_K):
            pltpu.make_async_copy(ys_ref.at[pl.ds(0, tm)], ys_ref.at[pl.ds(0, tm)], sems.at[slot]).wait()

        def reduce_group(g):
            r0 = pl.multiple_of(g * COMBINE_GROUP, COMBINE_GROUP)
            gate = gate_ref[pl.ds(r0, COMBINE_GROUP), :]
            acc_lo = acc_hi = None
            for k in range(TOP_K):
                y_lo, y_hi = _unpack_halves(buf_ref[slot, k, g])
                gk = gate[:, k:k + 1]
                acc_lo = gk * y_lo if k == 0 else acc_lo + gk * y_lo
                acc_hi = gk * y_hi if k == 0 else acc_hi + gk * y_hi
            acc_ref[pl.ds(r0, COMBINE_GROUP), :half] = acc_lo
            acc_ref[pl.ds(r0, COMBINE_GROUP), half:] = acc_hi

        @pl.when(i + 1 < n)
        def _():
            def body(g, carry):
                issue_group(destn_ref, 1 - slot, g)
                reduce_group(g)
                return carry
            lax.fori_loop(0, n_groups, body, 0)

        @pl.when(i + 1 >= n)
        def _():
            def body(g, carry):
                reduce_group(g)
                return carry
            lax.fori_loop(0, n_groups, body, 0)

    for s in range(2):
        pl.when(lax.rem(i, 2) == s)(functools.partial(run_tile, s))

    x = x_ref[...]
    out = _layer_norm(DEEPNORM_ALPHA * x + (acc_ref[...] + shared_ref[...].astype(F32)),
                      g_ref[...], b_ref[...])

    @pl.when(i < n_prompt_tiles)
    def _():
        op_ref[...] = out

    @pl.when(i >= n_prompt_tiles)
    def _():
        os_ref[...] = out


def _combine(x, shared, gate_t, dest_tiles, ys, g, b, tm, n_prompt_rows):
    t, d = x.shape
    n_tiles = t // tm
    npt = n_prompt_rows // tm
    const = lambda i: (0, 0)
    return pl.pallas_call(
        functools.partial(_combine_kernel, tm=tm, n_prompt_tiles=npt),
        grid=(n_tiles,),
        in_specs=[pl.BlockSpec((TOP_K * tm,), lambda i: (0,), memory_space=pltpu.SMEM),
                  pl.BlockSpec((TOP_K * tm,), lambda i: (jnp.minimum(i + 1, n_tiles - 1),),
                               memory_space=pltpu.SMEM),
                  pl.BlockSpec((tm, d), lambda i: (i, 0)),
                  pl.BlockSpec((tm, d), lambda i: (i, 0)),
                  pl.BlockSpec((tm, TOP_K), lambda i: (i, 0)),
                  pl.BlockSpec(memory_space=pl.ANY),
                  pl.BlockSpec((1, d), const),
                  pl.BlockSpec((1, d), const)],
        out_specs=[pl.BlockSpec((tm, d), lambda i: (jnp.minimum(i, npt - 1), 0)),
                   pl.BlockSpec((tm, d), lambda i: (jnp.maximum(i - npt, 0), 0))],
        out_shape=[jax.ShapeDtypeStruct((n_prompt_rows, d), F32),
                   jax.ShapeDtypeStruct((t - n_prompt_rows, d), F32)],
        scratch_shapes=[pltpu.VMEM((2, TOP_K, tm // COMBINE_GROUP, COMBINE_GROUP, d // 2), I32),
                        pltpu.VMEM((tm, d), F32),
                        pltpu.SemaphoreType.DMA((2,))],
        compiler_params=_params(1),
        name="moe_combine",
    )(dest_tiles, dest_tiles, x, shared, gate_t, ys, g.reshape(1, d), b.reshape(1, d))


def _tile_major(a, tm):
    k, t = a.shape
    return a.reshape(k, t // tm, tm).transpose(1, 0, 2).reshape(-1)


def _moe_ln(x, x_packed, router_w, router_bias, wg, wu, wd, swg, swu, swd, g, b, n_prompt_rows):
    t, d = x.shape
    w_t = router_w.T
    w_hi = w_t.astype(BF16)
    w_lo = (w_t - w_hi.astype(F32)).astype(BF16)
    tr = _divisor_tile(t, 512, LANES)
    idx, gate, rank, cnt = _router(x, w_hi, w_lo, router_bias, tr)

    counts = cnt[:, 0]
    padded = (counts + EXPERT_ROWS - 1) // EXPERT_ROWS * EXPERT_ROWS
    pad_end = jnp.cumsum(padded)
    pad_start = pad_end - padded
    experts = jnp.arange(N_EXPERTS, dtype=I32)
    onehot = idx[None] == experts[:, None, None]
    dest = jnp.sum(jnp.where(onehot, pad_start[:, None, None], 0), axis=0) + rank
    n_blocks = (t * TOP_K + N_EXPERTS * (EXPERT_ROWS - 1)) // EXPERT_ROWS + 1
    n_valid = (pad_end[-1] // EXPERT_ROWS).astype(I32)
    blk = jnp.minimum(jnp.arange(n_blocks, dtype=I32), n_valid - 1)
    block_expert = jnp.minimum(
        jnp.sum(pad_end[None, :] <= (blk * EXPERT_ROWS)[:, None], axis=1), N_EXPERTS - 1).astype(I32)
    later_used = (experts[None, :] > experts[:, None]) & (counts[None, :] > 0)
    next_used = jnp.min(jnp.where(later_used, experts[None, :], N_EXPERTS), axis=1)
    next_used = jnp.where(next_used == N_EXPERTS, -1, next_used)
    per_block = lambda table: jnp.sum(
        jnp.where(block_expert[:, None] == experts[None, :], table[None, :], 0), axis=1).astype(I32)
    next2_used = jnp.sum(jnp.where(next_used[:, None] == experts[None, :], next_used[None, :], 0), axis=1)
    next2_used = jnp.where(next_used >= 0, next2_used, -1)
    parity = (jnp.cumsum((counts > 0).astype(I32)) - 1) & 1
    next_expert, next2_expert, set_parity = per_block(next_used), per_block(next2_used), per_block(parity)
    rows_used = per_block(pad_start + counts) - blk * EXPERT_ROWS
    short_block = (rows_used <= EXPERT_ROWS // 2).astype(I32)

    td = _divisor_tile(t, 256, SUBLANES)
    xs, shared = _dispatch(x_packed, _tile_major(dest, td), (pad_start + counts).astype(I32),
                           (padded - counts).astype(I32), swg.astype(BF16), swu.astype(BF16),
                           swd.astype(BF16), n_blocks * EXPERT_ROWS, td)
    ys = _experts(xs, wg, wu, wd, block_expert, next_expert, next2_expert, set_parity, short_block,
                  n_valid.reshape(1))
    tc = _divisor_tile(math.gcd(n_prompt_rows, t - n_prompt_rows), 128, SUBLANES)
    return _combine(x, shared, gate.T, _tile_major(dest, tc), ys, g, b, tc, n_prompt_rows)


def kernel(x_prompt, x_sample, mem_prompt, state_gla, cache_swa_k, cache_swa_v, cache_mem_k, cache_mem_v,
           w_in, gla_w_gate2, gla_b_gate, gla_norm_g, swa_sinks, w_mix_out, ln1_g, ln1_b,
           mem_wq, mem_wk, mem_wv, mem_wo, ln2_g, ln2_b,
           router_w, router_bias, exp_w_gate, exp_w_up, exp_w_down, sh_w_gate, sh_w_up, sh_w_down,
           ln3_g, ln3_b):
    nbp, seq, d = x_prompt.shape
    nbs, dec, _ = x_sample.shape
    tp, ts = nbp * seq, nbs * dec
    t = tp + ts
    mem_tokens = mem_prompt.shape[1]
    assert state_gla.shape[0] == DEPTH and seq % SWA_WINDOW == 0 and dec % SUBLANES == 0
    assert cache_swa_k.shape[2] == SWA_WINDOW and tp % dec == 0
    tm = _divisor_tile(math.gcd(tp, ts), 512, 16)
    l = 0

    xp = x_prompt.reshape(tp, d)
    xs = x_sample.reshape(ts, d)

    gate_src = GLA_QK + GLA_QK + GLA_VW + GLA_VW
    swa_rows = SWA_QW + 2 * SWA_KVW
    w_in_rt = _reorder_cast(
        w_in[l].T,
        [(0, 0, gate_src),
         (gate_src + GLA_GATE_RANK, COL_SQ, swa_rows),
         (gate_src, COL_GG, GLA_GATE_RANK)],
        IN_COLS_PADDED, _divisor_tile(IN_COLS_PADDED, 512, 16))
    z = _matmul_split(xp, xs, w_in_rt, tm, IN_COLS_PADDED // 2, F32)

    wa2_pad = jnp.concatenate(
        [gla_w_gate2[l], jnp.zeros((LANES - GLA_GATE_RANK, GLA_QK), F32)], axis=0)
    wa2_hi = wa2_pad.astype(BF16)
    wa2_lo = (wa2_pad - wa2_hi.astype(F32)).astype(BF16)
    wa2_stack = jnp.concatenate([wa2_hi, wa2_lo, wa2_hi], axis=0)
    zero_state = jnp.zeros((nbp, GLA_HEADS, GLA_DK, GLA_DV), F32)
    gla_rows = _divisor_tile(seq, GLA_STEP_ROWS, CHUNK)
    mix, st_p = _gla(z, wa2_stack, gla_b_gate[l], gla_norm_g[l], zero_state, None,
                     nb=nbp, n_chunks=seq // gla_rows, rows=gla_rows, chunk=CHUNK, row0=0, chained=True)
    seqs = _divisor_tile(nbs, GLA_SAMPLE_SEQS, 1)
    assert tp % (seqs * dec) == 0
    mix, st_s = _gla(z, wa2_stack, gla_b_gate[l], gla_norm_g[l], state_gla[l], mix,
                     nb=nbs // seqs, n_chunks=1, rows=seqs * dec, chunk=dec, row0=tp, chained=False)
    nblk = seq // SWA_WINDOW
    prev_p = lambda col: (lambda b, i: (b * nblk + jnp.maximum(i - 1, 0), col))
    mix = _swa(z, swa_sinks[l], z, z, (prev_p(COL_SK // SWA_KVW), prev_p(COL_SV // SWA_KVW)), mix,
               nb=nbp, n_blocks=nblk, rows=SWA_WINDOW, row0=0, q_base=0, q_stride=SWA_WINDOW)
    ck = cache_swa_k[l].reshape(nbs * SWA_WINDOW, SWA_KVW)
    cv = cache_swa_v[l].reshape(nbs * SWA_WINDOW, SWA_KVW)
    prev_s = lambda b, i: (b, 0)
    mix = _swa(z, swa_sinks[l], ck, cv, (prev_s, prev_s), mix,
               nb=nbs, n_blocks=1, rows=dec, row0=tp, q_base=PAST_LEN, q_stride=0)
    h1 = _matmul_res_ln(mix, w_mix_out[l].astype(BF16), (xp, xs), ln1_g[l], ln1_b[l], tm)

    mem = mem_prompt.reshape(nbp * mem_tokens, d)
    tmem = _divisor_tile(nbp * mem_tokens, 512, 16)
    mk = _matmul(mem, mem_wk[l], tmem, _divisor_tile(d, 512, LANES), F32)
    mv = _matmul(mem, mem_wv[l], tmem, _divisor_tile(d, 512, LANES), F32)
    q = _matmul(h1, mem_wq[l].astype(BF16), tm, d, BF16)
    tq = _divisor_tile(seq, 512, 16)
    att = _mem_attn(q, mk, mv, None, nb=nbp, n_tiles=seq // tq, rows=tq, row0=0, mem_tokens=mem_tokens)
    att = _mem_attn(q, cache_mem_k[l].reshape(nbs * mem_tokens, d), cache_mem_v[l].reshape(nbs * mem_tokens, d),
                    att, nb=nbs, n_tiles=1, rows=dec, row0=tp, mem_tokens=mem_tokens)
    h2, h2_packed = _matmul_res_ln(att, mem_wo[l].astype(BF16), h1, ln2_g[l], ln2_b[l], tm, emit_packed=True)

    out_p, out_s = _moe_ln(h2, h2_packed, router_w[l], router_bias[l], exp_w_gate[l], exp_w_up[l],
                           exp_w_down[l], sh_w_gate[l], sh_w_up[l], sh_w_down[l], ln3_g[l], ln3_b[l], tp)

    y_prompt = out_p.reshape(nbp, seq, d)
    y_sample = out_s.reshape(nbs, dec, d)
    kv = z[:, COL_SK:COL_SK + 2 * SWA_KVW]
    kvp = kv[:tp].reshape(nbp, seq, 2 * SWA_KVW)[:, seq - SWA_WINDOW:]
    kvs = kv[tp:].reshape(nbs, dec, 2 * SWA_KVW)
    kv_shape = lambda a: a.reshape(a.shape[0], a.shape[1], SWA_KV_HEADS, SWA_HEAD_DIM)[None]
    mem_shape = lambda a: a.reshape(nbp, mem_tokens, MEM_HEADS, d // MEM_HEADS)[None]
    return (y_prompt, y_sample,
            st_p[None],
            kv_shape(kvp[..., :SWA_KVW]), kv_shape(kvp[..., SWA_KVW:]),
            mem_shape(mk), mem_shape(mv),
            st_s[None],
            kv_shape(kvs[..., :SWA_KVW]), kv_shape(kvs[..., SWA_KVW:]))
```

```python
import functools
import math

import jax
import jax.numpy as jnp
from jax import lax
from jax.experimental import pallas as pl
from jax.experimental.pallas import tpu as pltpu

F32 = jnp.float32
BF16 = jnp.bfloat16
I32 = jnp.int32

CHUNK = 64
PAST_LEN = 2048
GLA_HEADS = 8
GLA_DK = 64
GLA_DV = 128
GLA_QK = GLA_HEADS * GLA_DK
GLA_VW = GLA_HEADS * GLA_DV
GLA_GATE_RANK = 16
GLA_TAU = 16.0
SWA_HEADS = 16
SWA_KV_HEADS = 2
SWA_GROUP = SWA_HEADS // SWA_KV_HEADS
SWA_HEAD_DIM = 64
SWA_QW = SWA_HEADS * SWA_HEAD_DIM
SWA_KVW = SWA_KV_HEADS * SWA_HEAD_DIM
SWA_WINDOW = 128
WINDOW_CHUNKS = SWA_WINDOW // CHUNK
MEM_HEADS = 4
N_EXPERTS = 64
TOP_K = 8
N_GROUPS = 8
GROUP_SIZE = N_EXPERTS // N_GROUPS
TOPK_GROUPS = 4
ROUTED_SCALE = 2.5
DEPTH = 1
DEEPNORM_ALPHA = (2 * DEPTH) ** 0.25
LN_EPS = 1e-5
RMS_EPS = 1e-6

LANES = 128
SUBLANES = 8
VMEM_LIMIT_BYTES = 56 * 1024 * 1024

COL_GQ = 0
COL_GK = COL_GQ + GLA_QK
COL_GV = COL_GK + GLA_QK
COL_GR = COL_GV + GLA_VW
COL_SQ = COL_GR + GLA_VW
COL_SK = COL_SQ + SWA_QW
COL_SV = COL_SK + SWA_KVW
COL_GG = COL_SV + SWA_KVW
MXU_COLS = 256
IN_COLS_PADDED = -(-(COL_GG + LANES) // (2 * MXU_COLS)) * (2 * MXU_COLS)

EXPERT_ROWS = 256
EXPERT_X_SLOTS = 3
COMBINE_GROUP = SUBLANES
LN_SUB_ROWS = 128
GLA_STEP_ROWS = 256
GLA_SAMPLE_SEQS = 8
MEM_SAMPLE_SEQS = 2
COPY_SLOTS = 3

NT_DIMS = (((1,), (1,)), ((), ()))
TN_DIMS = (((0,), (0,)), ((), ()))


def _params(n_axes):
    return pltpu.CompilerParams(dimension_semantics=("arbitrary",) * n_axes,
                                vmem_limit_bytes=VMEM_LIMIT_BYTES)


def _sigmoid(x):
    return 1.0 / (1.0 + jnp.exp(-x))


def _silu(x):
    return x * _sigmoid(x)


def _layer_norm(x, g, b):
    mu = jnp.mean(x, axis=-1, keepdims=True)
    xc = x - mu
    var = jnp.mean(xc * xc, axis=-1, keepdims=True)
    return xc * lax.rsqrt(var + LN_EPS) * g + b


HI_MASK = -65536


def _round_to_bf16_bits(x):
    b = lax.bitcast_convert_type(x, I32)
    return b + (0x7FFF + (lax.shift_right_logical(b, 16) & 1))


def _pack_halves(x):
    c = x.shape[1] // 2
    lo = lax.shift_right_logical(_round_to_bf16_bits(x[:, :c]), 16)
    hi = _round_to_bf16_bits(x[:, c:]) & HI_MASK
    return lo | hi


def _unpack_halves(w):
    return (lax.bitcast_convert_type(lax.shift_left(w, 16), F32),
            lax.bitcast_convert_type(w & HI_MASK, F32))


def _divisor_tile(n, pref, mult):
    t = min(pref, n)
    while t > mult and (n % t or t % mult):
        t -= mult
    assert n % t == 0 and t % mult == 0, (n, pref, mult)
    return t


def _mm_kernel(x_ref, w_ref, o_ref, xb_ref):
    @pl.when(pl.program_id(1) == 0)
    def _():
        xb_ref[...] = x_ref[...].astype(BF16)

    o_ref[...] = jnp.dot(xb_ref[...], w_ref[...].astype(BF16), preferred_element_type=F32).astype(o_ref.dtype)


def _matmul(x, w, tm, tn, out_dtype):
    m, k = x.shape
    n = w.shape[1]
    return pl.pallas_call(
        _mm_kernel,
        grid=(m // tm, n // tn),
        in_specs=[pl.BlockSpec((tm, k), lambda i, j: (i, 0)),
                  pl.BlockSpec((k, tn), lambda i, j: (0, j))],
        out_specs=pl.BlockSpec((tm, tn), lambda i, j: (i, j)),
        out_shape=jax.ShapeDtypeStruct((m, n), out_dtype),
        scratch_shapes=[pltpu.VMEM((tm, k), BF16)],
        compiler_params=_params(2),
        name="matmul",
    )(x, w)


def _split_rows_specs(tm, k, n_first_tiles, n_grid_axes):
    if n_grid_axes == 1:
        return [pl.BlockSpec((tm, k), lambda i: (jnp.minimum(i, n_first_tiles - 1), 0)),
                pl.BlockSpec((tm, k), lambda i: (jnp.maximum(i - n_first_tiles, 0), 0))]
    return [pl.BlockSpec((tm, k), lambda j, i: (jnp.minimum(i, n_first_tiles - 1), 0)),
            pl.BlockSpec((tm, k), lambda j, i: (jnp.maximum(i - n_first_tiles, 0), 0))]


def _mm_split_kernel(xa_ref, xb_ref, wt_ref, o_ref, *, n_first_tiles):
    i = pl.program_id(1)

    def product(x_ref):
        o_ref[...] = lax.dot_general(x_ref[...].astype(BF16), wt_ref[...], NT_DIMS,
                                     preferred_element_type=F32).astype(o_ref.dtype)

    @pl.when(i < n_first_tiles)
    def _():
        product(xa_ref)

    @pl.when(i >= n_first_tiles)
    def _():
        product(xb_ref)


def _matmul_split(xa, xb, w_t, tm, tn, out_dtype):
    ma, k = xa.shape
    m = ma + xb.shape[0]
    n = w_t.shape[0]
    return pl.pallas_call(
        functools.partial(_mm_split_kernel, n_first_tiles=ma // tm),
        grid=(n // tn, m // tm),
        in_specs=_split_rows_specs(tm, k, ma // tm, 2) + [pl.BlockSpec((tn, k), lambda j, i: (j, 0))],
        out_specs=pl.BlockSpec((tm, tn), lambda j, i: (i, j)),
        out_shape=jax.ShapeDtypeStruct((m, n), out_dtype),
        compiler_params=_params(2),
        name="matmul_split",
    )(xa, xb, w_t)


def _reorder_cast_kernel(src_hbm, dst_hbm, stage_ref, out_ref, in_sems, out_sems, *, chunk_pieces, chunk):
    n = len(chunk_pieces)
    slots = stage_ref.shape[0]

    def in_copies(c):
        return [pltpu.make_async_copy(src_hbm.at[pl.ds(s0, ln)], stage_ref.at[c % slots, pl.ds(o0, ln)],
                                      in_sems.at[c % slots]) for s0, o0, ln in chunk_pieces[c]]

    def out_copy(c):
        return pltpu.make_async_copy(out_ref.at[c % slots], dst_hbm.at[pl.ds(c * chunk, chunk)],
                                     out_sems.at[c % slots])

    for c in range(min(slots - 1, n)):
        for cp in in_copies(c):
            cp.start()
    for c in range(n):
        if c + slots - 1 < n:
            for cp in in_copies(c + slots - 1):
                cp.start()
        for cp in in_copies(c):
            cp.wait()
        if c >= slots:
            out_copy(c - slots).wait()
        covered = sum(ln for _, _, ln in chunk_pieces[c])
        if covered < chunk:
            stage_ref[c % slots, covered:, :] = jnp.zeros((chunk - covered, stage_ref.shape[-1]),
                                                          stage_ref.dtype)
        out_ref[c % slots] = stage_ref[c % slots].astype(out_ref.dtype)
        out_copy(c).start()
    for c in range(max(n - slots, 0), n):
        out_copy(c).wait()


def _reorder_cast(src, runs, n_out_rows, chunk):
    k = src.shape[1]
    chunk_pieces = []
    for c0 in range(0, n_out_rows, chunk):
        pieces = []
        for s0, d0, ln in runs:
            lo, hi = max(d0, c0), min(d0 + ln, c0 + chunk)
            if lo < hi:
                assert (lo - c0) % 16 == 0 and (s0 + lo - d0) % SUBLANES == 0 and (hi - lo) % 16 == 0
                pieces.append((s0 + lo - d0, lo - c0, hi - lo))
        chunk_pieces.append(pieces)
    hbm = pl.BlockSpec(memory_space=pl.ANY)
    return pl.pallas_call(
        functools.partial(_reorder_cast_kernel, chunk_pieces=chunk_pieces, chunk=chunk),
        in_specs=[hbm],
        out_specs=hbm,
        out_shape=jax.ShapeDtypeStruct((n_out_rows, k), BF16),
        scratch_shapes=[pltpu.VMEM((COPY_SLOTS, chunk, k), src.dtype), pltpu.VMEM((COPY_SLOTS, chunk, k), BF16),
                        pltpu.SemaphoreType.DMA((COPY_SLOTS,)), pltpu.SemaphoreType.DMA((COPY_SLOTS,))],
        compiler_params=pltpu.CompilerParams(vmem_limit_bytes=VMEM_LIMIT_BYTES),
        name="reorder_cast",
    )(src)


def _mm_res_ln_kernel(x_ref, w_ref, *refs, n_res, n_first_tiles):
    res_refs, (g_ref, b_ref, o_ref, *packed_ref) = refs[:n_res], refs[n_res:]
    tm = x_ref.shape[0]
    sub = min(tm, LN_SUB_ROWS)
    use_first = pl.program_id(0) < n_first_tiles
    for r in range(0, tm, sub):
        rows = slice(r, r + sub)
        y = jnp.dot(x_ref[rows, :], w_ref[...], preferred_element_type=F32)
        res = res_refs[0][rows, :]
        if n_res == 2:
            res = jnp.where(use_first, res, res_refs[1][rows, :])
        h = _layer_norm(DEEPNORM_ALPHA * res + y, g_ref[...], b_ref[...])
        o_ref[rows, :] = h
        if packed_ref:
            groups = slice(r // SUBLANES, (r + sub) // SUBLANES)
            packed_ref[0][groups] = _pack_halves(h).reshape(sub // SUBLANES, SUBLANES, h.shape[1] // 2)


def _matmul_res_ln(x, w, res, g, b, tm, emit_packed=False):
    m, k = x.shape
    n = w.shape[1]
    if isinstance(res, tuple):
        n_first_tiles = res[0].shape[0] // tm
        res_specs = _split_rows_specs(tm, n, n_first_tiles, 1)
    else:
        n_first_tiles = 0
        res = (res,)
        res_specs = [pl.BlockSpec((tm, n), lambda i: (i, 0))]
    out_specs = [pl.BlockSpec((tm, n), lambda i: (i, 0))]
    out_shape = [jax.ShapeDtypeStruct((m, n), F32)]
    if emit_packed:
        out_specs.append(pl.BlockSpec((tm // SUBLANES, SUBLANES, n // 2), lambda i: (i, 0, 0)))
        out_shape.append(jax.ShapeDtypeStruct((m // SUBLANES, SUBLANES, n // 2), I32))
    outs = pl.pallas_call(
        functools.partial(_mm_res_ln_kernel, n_res=len(res), n_first_tiles=n_first_tiles),
        grid=(m // tm,),
        in_specs=[pl.BlockSpec((tm, k), lambda i: (i, 0)),
                  pl.BlockSpec((k, n), lambda i: (0, 0))] + res_specs + [
                  pl.BlockSpec((1, n), lambda i: (0, 0)),
                  pl.BlockSpec((1, n), lambda i: (0, 0))],
        out_specs=out_specs,
        out_shape=out_shape,
        compiler_params=_params(1),
        name="matmul_res_ln",
    )(x, w, *res, g.reshape(1, n), b.reshape(1, n))
    return outs if emit_packed else outs[0]


def _split3_bf16(x):
    hi = x.astype(BF16)
    r1 = x - hi.astype(F32)
    mid = r1.astype(BF16)
    lo = (r1 - mid.astype(F32)).astype(BF16)
    return jnp.concatenate([hi, mid, lo], axis=-1)


def _gla_kernel(q_ref, k_ref, v_ref, r_ref, gg_ref, wa2_ref, ba_ref, ng_ref, s0_ref, mix_in_ref,
                o_ref, sfin_ref, st_ref, *, rows, chunk, chained):
    del mix_in_ref
    n_sub = rows // chunk
    shift = chunk.bit_length() - 1
    assert chunk == 1 << shift

    if chained:
        @pl.when(pl.program_id(1) == 0)
        def _():
            for h in range(GLA_HEADS):
                st_ref[h] = s0_ref[0, h].T

    gg = gg_ref[...]
    gg_hi = gg.astype(BF16)
    gg_lo = (gg - gg_hi.astype(F32)).astype(BF16)
    gate = jnp.dot(jnp.concatenate([gg_hi, gg_hi, gg_lo], axis=-1), wa2_ref[...],
                   preferred_element_type=F32) + ba_ref[...]
    log_a = (jnp.minimum(gate, 0.0) - jnp.log1p(jnp.exp(-jnp.abs(gate)))) / GLA_TAU

    ri = lax.broadcasted_iota(I32, (rows, rows), 0)
    ci = lax.broadcasted_iota(I32, (rows, rows), 1)
    same_chunk = lax.shift_right_logical(ri, shift) == lax.shift_right_logical(ci, shift)
    causal = same_chunk & (ri >= ci)
    ones = jnp.concatenate([jnp.where(causal, 1.0, 0.0), jnp.where(same_chunk, 1.0, 0.0)], axis=0).astype(BF16)
    sums = jnp.dot(ones, _split3_bf16(log_a), preferred_element_type=F32)
    sums = sums[:, :GLA_QK] + sums[:, GLA_QK:2 * GLA_QK] + sums[:, 2 * GLA_QK:]
    bcum = sums[:rows]
    b_last = sums[rows:]
    q = q_ref[...] * (GLA_DK ** -0.5)
    k = k_ref[...]
    q_in = (q * jnp.exp(bcum)).astype(BF16)
    k_in = (k * jnp.exp(-bcum)).astype(BF16)
    k_out = (k * jnp.exp(b_last - bcum)).astype(BF16)
    decay = jnp.exp(b_last)
    ng = ng_ref[...]
    blk_r = lax.shift_right_logical(lax.broadcasted_iota(I32, (rows, n_sub * GLA_DK), 0), shift)
    blk_c = lax.shift_right_logical(lax.broadcasted_iota(I32, (rows, n_sub * GLA_DK), 1),
                                    GLA_DK.bit_length() - 1)
    for h in range(GLA_HEADS):
        ks = slice(h * GLA_DK, (h + 1) * GLA_DK)
        vs = slice(h * GLA_DV, (h + 1) * GLA_DV)
        vh = v_ref[:, vs].astype(BF16)
        qh = q_in[:, ks]
        attn = lax.dot_general(qh, k_in[:, ks], NT_DIMS, preferred_element_type=F32)
        attn = jnp.where(causal, attn, 0.0).astype(BF16)
        o = jnp.dot(attn, vh, preferred_element_type=F32)
        ko = k_out[:, ks]
        if n_sub > 1:
            ko = jnp.where(blk_r == blk_c, jnp.concatenate([ko] * n_sub, axis=-1), 0.0)
        upd_t = lax.dot_general(vh, ko, TN_DIMS, preferred_element_type=F32)
        states = []
        if chained:
            st = st_ref[h]
        for c in range(n_sub):
            if not chained:
                st = s0_ref[c, h].T
            states.append(st.astype(BF16))
            st = st * decay[c * chunk:c * chunk + 1, ks] + upd_t[:, c * GLA_DK:(c + 1) * GLA_DK]
            if not chained:
                sfin_ref[c, h] = st.T
        if chained:
            st_ref[h] = st
        o_all = lax.dot_general(qh, jnp.concatenate(states, axis=0), NT_DIMS, preferred_element_type=F32)
        o = o + jnp.concatenate(
            [o_all[c * chunk:(c + 1) * chunk, c * GLA_DV:(c + 1) * GLA_DV] for c in range(n_sub)], axis=0)
        o = o * lax.rsqrt(jnp.mean(o * o, axis=-1, keepdims=True) + RMS_EPS) * ng
        o_ref[:, vs] = (o * _silu(r_ref[:, vs])).astype(o_ref.dtype)
    if chained:
        @pl.when(pl.program_id(1) == pl.num_programs(1) - 1)
        def _():
            for h in range(GLA_HEADS):
                sfin_ref[0, h] = st_ref[h].T


def _gla(z, wa2_stack, ba, norm_g, s0, mix_in, *, nb, n_chunks, rows, chunk, row0, chained):
    t = z.shape[0]
    rb0 = row0 // rows
    seq_per_step = 1 if chained else rows // chunk
    assert chained or n_chunks == 1
    rmap = lambda colblk: (lambda b, c: (rb0 + b * n_chunks + c, colblk))
    in_specs = [
        pl.BlockSpec((rows, GLA_QK), rmap(COL_GQ // GLA_QK)),
        pl.BlockSpec((rows, GLA_QK), rmap(COL_GK // GLA_QK)),
        pl.BlockSpec((rows, GLA_VW), rmap(COL_GV // GLA_VW)),
        pl.BlockSpec((rows, GLA_VW), rmap(COL_GR // GLA_VW)),
        pl.BlockSpec((rows, LANES), rmap(COL_GG // LANES)),
        pl.BlockSpec((3 * LANES, GLA_QK), lambda b, c: (0, 0)),
        pl.BlockSpec((1, GLA_QK), lambda b, c: (0, 0)),
        pl.BlockSpec((1, GLA_DV), lambda b, c: (0, 0)),
        pl.BlockSpec((seq_per_step, GLA_HEADS, GLA_DK, GLA_DV), lambda b, c: (b, 0, 0, 0)),
    ]
    args = [z, z, z, z, z, wa2_stack, ba.reshape(1, GLA_QK), norm_g.reshape(1, GLA_DV), s0]
    aliases = {}
    if mix_in is None:
        mix_in = jnp.zeros((SUBLANES, LANES), BF16)
        in_specs.append(pl.BlockSpec(memory_space=pl.ANY))
    else:
        in_specs.append(pl.BlockSpec(memory_space=pl.ANY))
        aliases = {len(args): 0}
    args.append(mix_in)
    mix, s_fin = pl.pallas_call(
        functools.partial(_gla_kernel, rows=rows, chunk=chunk, chained=chained),
        grid=(nb, n_chunks),
        in_specs=in_specs,
        out_specs=[pl.BlockSpec((rows, GLA_VW), rmap(0)),
                   pl.BlockSpec((seq_per_step, GLA_HEADS, GLA_DK, GLA_DV), lambda b, c: (b, 0, 0, 0))],
        out_shape=[jax.ShapeDtypeStruct((t, GLA_VW + SWA_QW), BF16),
                   jax.ShapeDtypeStruct((nb * seq_per_step, GLA_HEADS, GLA_DK, GLA_DV), F32)],
        scratch_shapes=[pltpu.VMEM((GLA_HEADS, GLA_DV, GLA_DK), F32)],
        input_output_aliases=aliases,
        compiler_params=_params(2),
        name="gla",
    )(*args)
    return mix, s_fin


def _swa_kernel(sinks_ref, q_ref, kc_ref, vc_ref, kp_ref, vp_ref, mix_in_ref, o_ref, bias_ref, *,
                rows, q_base, q_stride):
    del mix_in_ref
    i = pl.program_id(1)
    n_tables = bias_ref.shape[0]
    nk = SWA_WINDOW + rows
    sr = SWA_GROUP * rows
    assert rows & (rows - 1) == 0 and q_base % CHUNK == 0 and q_stride % CHUNK == 0
    assert n_tables == 1 or q_base + q_stride >= SWA_WINDOW
    head_in_group = lax.shift_right_logical(lax.broadcasted_iota(I32, (sr, 1), 0), rows.bit_length() - 1)

    @pl.when(jnp.logical_and(pl.program_id(0) == 0, i == 0))
    def _():
        row = lax.broadcasted_iota(I32, (sr, nk), 0)
        col = lax.broadcasted_iota(I32, (sr, nk), 1)
        for t in range(n_tables):
            q0 = q_base + t * q_stride
            qpos = q0 + (row & (rows - 1))
            kpos = q0 - SWA_WINDOW + col
            dist = jnp.abs(qpos - kpos).astype(F32)
            qc = lax.shift_right_arithmetic(qpos, CHUNK.bit_length() - 1)
            kc = lax.shift_right_arithmetic(kpos, CHUNK.bit_length() - 1)
            allowed = (kpos >= 0) & (kc <= qc) & (kc >= qc - WINDOW_CHUNKS)
            for g in range(SWA_KV_HEADS):
                slope = jnp.zeros((sr, 1), F32)
                for j in range(SWA_GROUP):
                    h = g * SWA_GROUP + j
                    slope = jnp.where(head_in_group == j, 2.0 ** (-8.0 * (h + 1) / SWA_HEADS), slope)
                bias_ref[t, g] = jnp.where(allowed, -(slope * dist), -jnp.inf)

    table = jnp.minimum(i, n_tables - 1)
    keys = jnp.concatenate([kp_ref[...], kc_ref[...]], axis=0).astype(BF16)
    vals = jnp.concatenate([vp_ref[...], vc_ref[...]], axis=0).astype(BF16)
    for g in range(SWA_KV_HEADS):
        sink = jnp.zeros((sr, 1), F32)
        for j in range(SWA_GROUP):
            sink = jnp.where(head_in_group == j, sinks_ref[g * SWA_GROUP + j], sink)
        kv = slice(g * SWA_HEAD_DIM, (g + 1) * SWA_HEAD_DIM)
        qg = jnp.concatenate(
            [q_ref[:, (g * SWA_GROUP + j) * SWA_HEAD_DIM:(g * SWA_GROUP + j + 1) * SWA_HEAD_DIM]
             for j in range(SWA_GROUP)], axis=0).astype(BF16)
        s = lax.dot_general(qg, keys[:, kv], NT_DIMS, preferred_element_type=F32)
        s = s * (SWA_HEAD_DIM ** -0.5) + bias_ref[table, g]
        m = jnp.maximum(jnp.max(s, axis=-1, keepdims=True), sink)
        p = jnp.exp(s - m)
        denom = jnp.sum(p, axis=-1, keepdims=True) + jnp.exp(sink - m)
        o = jnp.dot(p.astype(BF16), vals[:, kv], preferred_element_type=F32) / denom
        for j in range(0, SWA_GROUP, 2):
            c0 = (g * SWA_GROUP + j) * SWA_HEAD_DIM
            o_ref[:, c0:c0 + LANES] = jnp.concatenate(
                [o[j * rows:(j + 1) * rows], o[(j + 1) * rows:(j + 2) * rows]], axis=-1).astype(o_ref.dtype)


def _swa(z, sinks, k_prev, v_prev, prev_map, mix_in, *, nb, n_blocks, rows, row0, q_base, q_stride):
    rb0 = row0 // rows
    rmap = lambda colblk: (lambda b, i: (rb0 + b * n_blocks + i, colblk))
    return pl.pallas_call(
        functools.partial(_swa_kernel, rows=rows, q_base=q_base, q_stride=q_stride),
        grid=(nb, n_blocks),
        in_specs=[pl.BlockSpec(memory_space=pltpu.SMEM),
                  pl.BlockSpec((rows, SWA_QW), rmap(COL_SQ // SWA_QW)),
                  pl.BlockSpec((rows, SWA_KVW), rmap(COL_SK // SWA_KVW)),
                  pl.BlockSpec((rows, SWA_KVW), rmap(COL_SV // SWA_KVW)),
                  pl.BlockSpec((SWA_WINDOW, SWA_KVW), prev_map[0]),
                  pl.BlockSpec((SWA_WINDOW, SWA_KVW), prev_map[1]),
                  pl.BlockSpec(memory_space=pl.ANY)],
        out_specs=pl.BlockSpec((rows, SWA_QW), rmap(GLA_VW // SWA_QW)),
        out_shape=jax.ShapeDtypeStruct(mix_in.shape, mix_in.dtype),
        scratch_shapes=[pltpu.VMEM((min(2, n_blocks), SWA_KV_HEADS, SWA_GROUP * rows, SWA_WINDOW + rows), F32)],
        input_output_aliases={6: 0},
        compiler_params=_params(2),
        name="swa",
    )(sinks, z, z, z, k_prev, v_prev, mix_in)


def _mem_attn_kernel(q_ref, k_ref, v_ref, o_in_ref, o_ref, *, head_dim, n_seq):
    del o_in_ref
    rq = q_ref.shape[0] // n_seq
    rm = k_ref.shape[0] // n_seq
    for s_i in range(n_seq):
        qr = slice(s_i * rq, (s_i + 1) * rq)
        mr = slice(s_i * rm, (s_i + 1) * rm)
        heads = [slice(h * head_dim, (h + 1) * head_dim) for h in range(MEM_HEADS)]
        q3 = jnp.stack([q_ref[qr, hs] for hs in heads], axis=0)
        k3 = jnp.stack([k_ref[mr, hs] for hs in heads], axis=0).astype(BF16)
        v3 = jnp.stack([v_ref[mr, hs] for hs in heads], axis=0).astype(BF16)
        s = jnp.einsum("hqd,hkd->hqk", q3, k3, preferred_element_type=F32) * (head_dim ** -0.5)
        m = jnp.max(s, axis=-1, keepdims=True)
        p = jnp.exp(s - m)
        denom = jnp.sum(p, axis=-1, keepdims=True)
        o = jnp.einsum("hqk,hkd->hqd", p.astype(BF16), v3, preferred_element_type=F32) / denom
        for h, hs in enumerate(heads):
            o_ref[qr, hs] = o[h].astype(o_ref.dtype)


def _mem_attn(q, mk, mv, o_in, *, nb, n_tiles, rows, row0, mem_tokens, n_seq=1):
    t, d = q.shape
    assert n_seq == 1 or n_tiles == 1
    rb0 = row0 // rows
    qmap = lambda b, i: (rb0 + b * n_tiles + i, 0)
    in_specs = [pl.BlockSpec((rows, d), qmap),
                pl.BlockSpec((n_seq * mem_tokens, d), lambda b, i: (b, 0)),
                pl.BlockSpec((n_seq * mem_tokens, d), lambda b, i: (b, 0)),
                pl.BlockSpec(memory_space=pl.ANY)]
    aliases = {}
    if o_in is None:
        o_in = jnp.zeros((SUBLANES, LANES), BF16)
    else:
        aliases = {3: 0}
    return pl.pallas_call(
        functools.partial(_mem_attn_kernel, head_dim=d // MEM_HEADS, n_seq=n_seq),
        grid=(nb, n_tiles),
        in_specs=in_specs,
        out_specs=pl.BlockSpec((rows, d), qmap),
        out_shape=jax.ShapeDtypeStruct((t, d), BF16),
        input_output_aliases=aliases,
        compiler_params=_params(2),
        name="mem_attn",
    )(q, mk, mv, o_in)


def _pick_first_max(cur, iota, sentinel):
    mx = jnp.max(cur, axis=0, keepdims=True)
    first = jnp.min(jnp.where(cur == mx, iota, sentinel), axis=0, keepdims=True)
    return iota == first, first


def _router_kernel(x_ref, wh_ref, wl_ref, bias_ref, idx_ref, gate_ref, rank_ref, cnt_ref, run_ref, before_ref,
                   *, tm):
    i = pl.program_id(0)

    @pl.when(i == 0)
    def _():
        run_ref[...] = jnp.zeros_like(run_ref)
        ti = lax.broadcasted_iota(I32, (tm, tm), 0)
        tj = lax.broadcasted_iota(I32, (tm, tm), 1)
        before_ref[...] = jnp.where(ti < tj, 1.0, 0.0).astype(BF16)

    x = x_ref[...]
    xh = x.astype(BF16)
    xl = (x - xh.astype(F32)).astype(BF16)
    wh = wh_ref[...]
    logits = (lax.dot_general(wh, xh, NT_DIMS, preferred_element_type=F32)
              + lax.dot_general(wh, xl, NT_DIMS, preferred_element_type=F32)
              + lax.dot_general(wl_ref[...], xh, NT_DIMS, preferred_element_type=F32))
    scores = _sigmoid(logits)
    sel = scores + bias_ref[...]
    neg_inf = -jnp.inf

    li = lax.broadcasted_iota(I32, (GROUP_SIZE, tm), 0)
    grp_rows = []
    for g in range(N_GROUPS):
        blk = sel[g * GROUP_SIZE:(g + 1) * GROUP_SIZE, :]
        pick, _ = _pick_first_max(blk, li, GROUP_SIZE)
        m1 = jnp.max(blk, axis=0, keepdims=True)
        m2 = jnp.max(jnp.where(pick, neg_inf, blk), axis=0, keepdims=True)
        grp_rows.append(m1 + m2)
    grp = jnp.concatenate(grp_rows, axis=0)

    gi = lax.broadcasted_iota(I32, (N_GROUPS, tm), 0)
    gsel = jnp.zeros((N_GROUPS, tm), F32)
    cur = grp
    for _ in range(TOPK_GROUPS):
        pick, _ = _pick_first_max(cur, gi, N_GROUPS)
        gsel = jnp.where(pick, 1.0, gsel)
        cur = jnp.where(pick, neg_inf, cur)
    emask = jnp.concatenate(
        [jnp.broadcast_to(gsel[g:g + 1, :], (GROUP_SIZE, tm)) for g in range(N_GROUPS)], axis=0)

    ei = lax.broadcasted_iota(I32, (N_EXPERTS, tm), 0)
    cur = jnp.where(emask > 0.5, sel, neg_inf)
    chosen = jnp.zeros((N_EXPERTS, tm), F32)
    idx_rows, w_rows = [], []
    for _ in range(TOP_K):
        pick, first = _pick_first_max(cur, ei, N_EXPERTS)
        idx_rows.append(first)
        w_rows.append(jnp.sum(jnp.where(pick, scores, 0.0), axis=0, keepdims=True))
        chosen = jnp.where(pick, 1.0, chosen)
        cur = jnp.where(pick, neg_inf, cur)
    idx = jnp.concatenate(idx_rows, axis=0)
    w = jnp.concatenate(w_rows, axis=0)
    gate_ref[...] = w / jnp.sum(w, axis=0, keepdims=True) * ROUTED_SCALE
    idx_ref[...] = idx

    local = jnp.dot(chosen.astype(BF16), before_ref[...], preferred_element_type=F32)
    total = local + run_ref[:, 0:1]
    rank_rows = [jnp.sum(jnp.where(ei == idx_rows[k], total, 0.0), axis=0, keepdims=True)
                 for k in range(TOP_K)]
    rank_ref[...] = jnp.concatenate(rank_rows, axis=0).astype(I32)
    run_ref[...] = run_ref[...] + jnp.sum(chosen, axis=1, keepdims=True)
    cnt_ref[...] = run_ref[...].astype(I32)


def _router(x, w_hi_t, w_lo_t, bias, tm):
    t, d = x.shape
    return pl.pallas_call(
        functools.partial(_router_kernel, tm=tm),
        grid=(t // tm,),
        in_specs=[pl.BlockSpec((tm, d), lambda i: (i, 0)),
                  pl.BlockSpec((N_EXPERTS, d), lambda i: (0, 0)),
                  pl.BlockSpec((N_EXPERTS, d), lambda i: (0, 0)),
                  pl.BlockSpec((N_EXPERTS, 1), lambda i: (0, 0))],
        out_specs=[pl.BlockSpec((TOP_K, tm), lambda i: (0, i)),
                   pl.BlockSpec((TOP_K, tm), lambda i: (0, i)),
                   pl.BlockSpec((TOP_K, tm), lambda i: (0, i)),
                   pl.BlockSpec((N_EXPERTS, LANES), lambda i: (0, 0))],
        out_shape=[jax.ShapeDtypeStruct((TOP_K, t), I32),
                   jax.ShapeDtypeStruct((TOP_K, t), F32),
                   jax.ShapeDtypeStruct((TOP_K, t), I32),
                   jax.ShapeDtypeStruct((N_EXPERTS, LANES), I32)],
        scratch_shapes=[pltpu.VMEM((N_EXPERTS, LANES), F32), pltpu.VMEM((tm, tm), BF16)],
        compiler_params=_params(1),
        name="router",
    )(x, w_hi_t, w_lo_t, bias.reshape(N_EXPERTS, 1))


PAD_BITS = tuple(1 << s for s in reversed(range(EXPERT_ROWS.bit_length() - 1)))


def _dispatch_kernel(zstart_ref, zcount_ref, dest_ref, x_ref, swg_ref, swu_ref, swd_ref, xs_ref, shared_ref,
                     zero_ref, sem, zsem, *, tm):
    i = pl.program_id(0)

    def start_group(g, carry):
        for j in range(SUBLANES):
            for k in range(TOP_K):
                d = dest_ref[k * tm + g * SUBLANES + j]
                pltpu.make_async_copy(x_ref.at[g, pl.ds(j, 1), :], xs_ref.at[d], sem).start(priority=k % 2)
        return carry

    lax.fori_loop(0, tm // SUBLANES, start_group, 0)

    half = x_ref.shape[-1]
    x_lo, x_hi = _unpack_halves(x_ref[...].reshape(tm, half))
    xb = jnp.concatenate([x_lo.astype(BF16), x_hi.astype(BF16)], axis=-1)
    hg = jnp.dot(xb, swg_ref[...], preferred_element_type=F32)
    hu = jnp.dot(xb, swu_ref[...], preferred_element_type=F32)
    shared_ref[...] = jnp.dot((_silu(hg) * hu).astype(BF16), swd_ref[...],
                              preferred_element_type=F32).astype(shared_ref.dtype)

    for k in range(TOP_K):
        pltpu.make_async_copy(xs_ref.at[pl.ds(0, tm)], xs_ref.at[pl.ds(0, tm)], sem).wait()

    @pl.when(i == pl.num_programs(0) - 1)
    def _():
        zero_ref[...] = jnp.zeros_like(zero_ref)

        def fill(e, wait):
            pos = zstart_ref[e]
            cnt = zcount_ref[e]
            for bit in PAD_BITS:
                has = (cnt & bit) != 0

                @pl.when(has)
                def _():
                    cp = pltpu.make_async_copy(zero_ref.at[pl.ds(0, bit)], xs_ref.at[pl.ds(pos, bit)], zsem)
                    if wait:
                        cp.wait()
                    else:
                        cp.start()

                pos = pos + jnp.where(has, bit, 0)

        def fill_start(e, carry):
            fill(e, False)
            return carry

        def fill_wait(e, carry):
            fill(e, True)
            return carry

        lax.fori_loop(0, N_EXPERTS, fill_start, 0)
        lax.fori_loop(0, N_EXPERTS, fill_wait, 0)


def _dispatch(x, dest_tiles, zstart, zcount, swg, swu, swd, n_rows, tm):
    t = x.shape[0] * SUBLANES
    d = x.shape[2]
    ds = swg.shape[1]
    const = lambda i, zs, zc: (0, 0)
    return pl.pallas_call(
        functools.partial(_dispatch_kernel, tm=tm),
        grid_spec=pltpu.PrefetchScalarGridSpec(
            num_scalar_prefetch=2,
            grid=(t // tm,),
            in_specs=[pl.BlockSpec((TOP_K * tm,), lambda i, zs, zc: (i,), memory_space=pltpu.SMEM),
                      pl.BlockSpec((tm // SUBLANES, SUBLANES, d), lambda i, zs, zc: (i, 0, 0)),
                      pl.BlockSpec((2 * d, ds), const),
                      pl.BlockSpec((2 * d, ds), const),
                      pl.BlockSpec((ds, 2 * d), const)],
            out_specs=[pl.BlockSpec(memory_space=pl.ANY),
                       pl.BlockSpec((tm, 2 * d), lambda i, zs, zc: (i, 0))],
            scratch_shapes=[pltpu.VMEM((EXPERT_ROWS // 2, 1, d), x.dtype),
                            pltpu.SemaphoreType.DMA(()),
                            pltpu.SemaphoreType.DMA(())]),
        out_shape=[jax.ShapeDtypeStruct((n_rows, 1, d), x.dtype),
                   jax.ShapeDtypeStruct((t, 2 * d), BF16)],
        compiler_params=_params(1),
        name="moe_dispatch",
    )(zstart, zcount, dest_tiles, x, swg, swu, swd)


def _expert_kernel(be_ref, nxt_ref, short_ref, nv_ref, xs_hbm, wg_hbm, wu_hbm, wd_hbm, ys_hbm,
                   wgs_ref, wus_ref, wds_ref, wgb_ref, wub_ref, wdb_ref, xbuf_ref, ybuf_ref,
                   wsems, xsems, ysems):
    b = pl.program_id(0)
    n_valid = nv_ref[0]
    e = be_ref[b]
    prev = be_ref[jnp.maximum(b - 1, 0)]
    fresh = jnp.logical_or(b == 0, e != prev)
    slot = lax.rem(b, 2)

    def weight_copies(expert):
        return (pltpu.make_async_copy(wg_hbm.at[expert], wgs_ref, wsems.at[0]),
                pltpu.make_async_copy(wu_hbm.at[expert], wus_ref, wsems.at[1]),
                pltpu.make_async_copy(wd_hbm.at[expert], wds_ref, wsems.at[2]))

    def rows_in(blk, to_slot):
        r0 = pl.multiple_of(blk * EXPERT_ROWS, EXPERT_ROWS)
        return pltpu.make_async_copy(xs_hbm.at[pl.ds(r0, EXPERT_ROWS), 0, :], xbuf_ref.at[to_slot],
                                     xsems.at[to_slot])

    def rows_out(blk, from_slot):
        r0 = pl.multiple_of(blk * EXPERT_ROWS, EXPERT_ROWS)
        return pltpu.make_async_copy(ybuf_ref.at[from_slot], ys_hbm.at[pl.ds(r0, EXPERT_ROWS), 0, :],
                                     ysems.at[from_slot])

    x_slots = xbuf_ref.shape[0]
    xslot = lax.rem(b, x_slots)

    @pl.when(b == 0)
    def _():
        for cp in weight_copies(e):
            cp.start(priority=1)
        for ahead in range(x_slots - 1):
            @pl.when(ahead < n_valid)
            def _():
                rows_in(ahead, ahead).start()

    @pl.when(fresh)
    def _():
        for cp in weight_copies(e):
            cp.wait()
        wgb_ref[...] = wgs_ref[...].astype(BF16)
        wub_ref[...] = wus_ref[...].astype(BF16)
        wdb_ref[...] = wds_ref[...].astype(BF16)
        nxt = nxt_ref[b]

        @pl.when(nxt >= 0)
        def _():
            for cp in weight_copies(nxt):
                cp.start(priority=1)

    @pl.when(b < n_valid)
    def _():
        rows_in(b, xslot).wait()

        @pl.when(b + x_slots - 1 < n_valid)
        def _():
            rows_in(b + x_slots - 1, lax.rem(b + x_slots - 1, x_slots)).start()

        @pl.when(b >= 2)
        def _():
            rows_out(b - 2, slot).wait()

        def swiglu_rows(rows):
            half = xbuf_ref.shape[-1]
            x_lo, x_hi = _unpack_halves(xbuf_ref[xslot, :rows, :])
            xb = jnp.concatenate([x_lo.astype(BF16), x_hi.astype(BF16)], axis=-1)
            hg = jnp.dot(xb, wgb_ref[...], preferred_element_type=F32)
            hu = jnp.dot(xb, wub_ref[...], preferred_element_type=F32)
            act = (_silu(hg) * hu).astype(BF16)
            ybuf_ref[slot, :rows, :] = _pack_halves(jnp.dot(act, wdb_ref[...], preferred_element_type=F32))
            if rows < EXPERT_ROWS:
                ybuf_ref[slot, rows:, :] = jnp.zeros((EXPERT_ROWS - rows, half), ybuf_ref.dtype)

        pl.when(short_ref[b] == 1)(functools.partial(swiglu_rows, EXPERT_ROWS // 2))
        pl.when(short_ref[b] == 0)(functools.partial(swiglu_rows, EXPERT_ROWS))
        rows_out(b, slot).start()

        @pl.when(b == n_valid - 1)
        def _():
            rows_out(b, slot).wait()

            @pl.when(b >= 1)
            def _():
                rows_out(b - 1, 1 - slot).wait()


def _experts(xs, wg, wu, wd, block_expert, next_expert, short_block, n_valid):
    n_rows = xs.shape[0]
    d = wg.shape[1]
    de = wg.shape[2]
    nb = n_rows // EXPERT_ROWS
    hbm = pl.BlockSpec(memory_space=pl.ANY)
    return pl.pallas_call(
        _expert_kernel,
        grid_spec=pltpu.PrefetchScalarGridSpec(
            num_scalar_prefetch=4,
            grid=(nb,),
            in_specs=[hbm, hbm, hbm, hbm],
            out_specs=hbm,
            scratch_shapes=[pltpu.VMEM((d, de), F32), pltpu.VMEM((d, de), F32), pltpu.VMEM((de, d), F32),
                            pltpu.VMEM((d, de), BF16), pltpu.VMEM((d, de), BF16), pltpu.VMEM((de, d), BF16),
                            pltpu.VMEM((EXPERT_X_SLOTS, EXPERT_ROWS, d // 2), I32),
                            pltpu.VMEM((2, EXPERT_ROWS, d // 2), I32),
                            pltpu.SemaphoreType.DMA((3,)), pltpu.SemaphoreType.DMA((EXPERT_X_SLOTS,)),
                            pltpu.SemaphoreType.DMA((2,))]),
        out_shape=jax.ShapeDtypeStruct((n_rows, 1, d // 2), I32),
        compiler_params=_params(1),
        name="moe_experts",
    )(block_expert, next_expert, short_block, n_valid, xs, wg, wu, wd)


def _combine_kernel(dest0_ref, destn_ref, x_ref, shared_ref, gate_ref, ys_ref, g_ref, b_ref,
                    op_ref, os_ref, buf_ref, acc_ref, sems, *, tm, n_prompt_tiles):
    i = pl.program_id(0)
    n = pl.num_programs(0)
    half = buf_ref.shape[-1]
    n_groups = tm // COMBINE_GROUP

    def issue_group(dest_ref, to_slot, g):
        for j in range(COMBINE_GROUP):
            t = g * COMBINE_GROUP + j
            for k in range(TOP_K):
                d = dest_ref[k * tm + t]
                pltpu.make_async_copy(ys_ref.at[d], buf_ref.at[to_slot, k, g, pl.ds(j, 1), :],
                                      sems.at[to_slot]).start(priority=k % 2)

    @pl.when(i == 0)
    def _():
        def first(g, carry):
            issue_group(dest0_ref, 0, g)
            return carry
        lax.fori_loop(0, n_groups, first, 0)

    def run_tile(slot):
        for k in range(TOP_K):
            pltpu.make_async_copy(ys_ref.at[pl.ds(0, tm)], ys_ref.at[pl.ds(0, tm)], sems.at[slot]).wait()

        def reduce_group(g):
            r0 = pl.multiple_of(g * COMBINE_GROUP, COMBINE_GROUP)
            gate = gate_ref[pl.ds(r0, COMBINE_GROUP), :]
            acc_lo = acc_hi = None
            for k in range(TOP_K):
                y_lo, y_hi = _unpack_halves(buf_ref[slot, k, g])
                gk = gate[:, k:k + 1]
                acc_lo = gk * y_lo if k == 0 else acc_lo + gk * y_lo
                acc_hi = gk * y_hi if k == 0 else acc_hi + gk * y_hi
            acc_ref[pl.ds(r0, COMBINE_GROUP), :half] = acc_lo
            acc_ref[pl.ds(r0, COMBINE_GROUP), half:] = acc_hi

        @pl.when(i + 1 < n)
        def _():
            def body(g, carry):
                issue_group(destn_ref, 1 - slot, g)
                reduce_group(g)
                return carry
            lax.fori_loop(0, n_groups, body, 0)

        @pl.when(i + 1 >= n)
        def _():
            def body(g, carry):
                reduce_group(g)
                return carry
            lax.fori_loop(0, n_groups, body, 0)

    for s in range(2):
        pl.when(lax.rem(i, 2) == s)(functools.partial(run_tile, s))

    x = x_ref[...]
    out = _layer_norm(DEEPNORM_ALPHA * x + (acc_ref[...] + shared_ref[...].astype(F32)),
                      g_ref[...], b_ref[...])

    @pl.when(i < n_prompt_tiles)
    def _():
        op_ref[...] = out

    @pl.when(i >= n_prompt_tiles)
    def _():
        os_ref[...] = out


def _combine(x, shared, gate_t, dest_tiles, ys, g, b, tm, n_prompt_rows):
    t, d = x.shape
    n_tiles = t // tm
    npt = n_prompt_rows // tm
    const = lambda i: (0, 0)
    return pl.pallas_call(
        functools.partial(_combine_kernel, tm=tm, n_prompt_tiles=npt),
        grid=(n_tiles,),
        in_specs=[pl.BlockSpec((TOP_K * tm,), lambda i: (0,), memory_space=pltpu.SMEM),
                  pl.BlockSpec((TOP_K * tm,), lambda i: (jnp.minimum(i + 1, n_tiles - 1),),
                               memory_space=pltpu.SMEM),
                  pl.BlockSpec((tm, d), lambda i: (i, 0)),
                  pl.BlockSpec((tm, d), lambda i: (i, 0)),
                  pl.BlockSpec((tm, TOP_K), lambda i: (i, 0)),
                  pl.BlockSpec(memory_space=pl.ANY),
                  pl.BlockSpec((1, d), const),
                  pl.BlockSpec((1, d), const)],
        out_specs=[pl.BlockSpec((tm, d), lambda i: (jnp.minimum(i, npt - 1), 0)),
                   pl.BlockSpec((tm, d), lambda i: (jnp.maximum(i - npt, 0), 0))],
        out_shape=[jax.ShapeDtypeStruct((n_prompt_rows, d), F32),
                   jax.ShapeDtypeStruct((t - n_prompt_rows, d), F32)],
        scratch_shapes=[pltpu.VMEM((2, TOP_K, tm // COMBINE_GROUP, COMBINE_GROUP, d // 2), I32),
                        pltpu.VMEM((tm, d), F32),
                        pltpu.SemaphoreType.DMA((2,))],
        compiler_params=_params(1),
        name="moe_combine",
    )(dest_tiles, dest_tiles, x, shared, gate_t, ys, g.reshape(1, d), b.reshape(1, d))


def _tile_major(a, tm):
    k, t = a.shape
    return a.reshape(k, t // tm, tm).transpose(1, 0, 2).reshape(-1)


def _moe_ln(x, x_packed, router_w, router_bias, wg, wu, wd, swg, swu, swd, g, b, n_prompt_rows):
    t, d = x.shape
    w_t = router_w.T
    w_hi = w_t.astype(BF16)
    w_lo = (w_t - w_hi.astype(F32)).astype(BF16)
    tr = _divisor_tile(t, 512, LANES)
    idx, gate, rank, cnt = _router(x, w_hi, w_lo, router_bias, tr)

    counts = cnt[:, 0]
    padded = (counts + EXPERT_ROWS - 1) // EXPERT_ROWS * EXPERT_ROWS
    pad_end = jnp.cumsum(padded)
    pad_start = pad_end - padded
    experts = jnp.arange(N_EXPERTS, dtype=I32)
    onehot = idx[None] == experts[:, None, None]
    dest = jnp.sum(jnp.where(onehot, pad_start[:, None, None], 0), axis=0) + rank
    n_blocks = (t * TOP_K + N_EXPERTS * (EXPERT_ROWS - 1)) // EXPERT_ROWS + 1
    n_valid = (pad_end[-1] // EXPERT_ROWS).astype(I32)
    blk = jnp.minimum(jnp.arange(n_blocks, dtype=I32), n_valid - 1)
    block_expert = jnp.minimum(
        jnp.sum(pad_end[None, :] <= (blk * EXPERT_ROWS)[:, None], axis=1), N_EXPERTS - 1).astype(I32)
    later_used = (experts[None, :] > experts[:, None]) & (counts[None, :] > 0)
    next_used = jnp.min(jnp.where(later_used, experts[None, :], N_EXPERTS), axis=1)
    next_used = jnp.where(next_used == N_EXPERTS, -1, next_used)
    per_block = lambda table: jnp.sum(
        jnp.where(block_expert[:, None] == experts[None, :], table[None, :], 0), axis=1).astype(I32)
    next_expert = per_block(next_used)
    rows_used = per_block(pad_start + counts) - blk * EXPERT_ROWS
    short_block = (rows_used <= EXPERT_ROWS // 2).astype(I32)

    td = _divisor_tile(t, 256, SUBLANES)
    xs, shared = _dispatch(x_packed, _tile_major(dest, td), (pad_start + counts).astype(I32),
                           (padded - counts).astype(I32), swg.astype(BF16), swu.astype(BF16),
                           swd.astype(BF16), n_blocks * EXPERT_ROWS, td)
    ys = _experts(xs, wg, wu, wd, block_expert, next_expert, short_block, n_valid.reshape(1))
    tc = _divisor_tile(math.gcd(n_prompt_rows, t - n_prompt_rows), 128, SUBLANES)
    return _combine(x, shared, gate.T, _tile_major(dest, tc), ys, g, b, tc, n_prompt_rows)


def kernel(x_prompt, x_sample, mem_prompt, state_gla, cache_swa_k, cache_swa_v, cache_mem_k, cache_mem_v,
           w_in, gla_w_gate2, gla_b_gate, gla_norm_g, swa_sinks, w_mix_out, ln1_g, ln1_b,
           mem_wq, mem_wk, mem_wv, mem_wo, ln2_g, ln2_b,
           router_w, router_bias, exp_w_gate, exp_w_up, exp_w_down, sh_w_gate, sh_w_up, sh_w_down,
           ln3_g, ln3_b):
    nbp, seq, d = x_prompt.shape
    nbs, dec, _ = x_sample.shape
    tp, ts = nbp * seq, nbs * dec
    t = tp + ts
    mem_tokens = mem_prompt.shape[1]
    assert state_gla.shape[0] == DEPTH and seq % SWA_WINDOW == 0 and dec % SUBLANES == 0
    assert cache_swa_k.shape[2] == SWA_WINDOW and tp % dec == 0
    tm = _divisor_tile(math.gcd(tp, ts), 512, 16)
    l = 0

    xp = x_prompt.reshape(tp, d)
    xs = x_sample.reshape(ts, d)

    gate_src = GLA_QK + GLA_QK + GLA_VW + GLA_VW
    swa_rows = SWA_QW + 2 * SWA_KVW
    w_in_rt = _reorder_cast(
        w_in[l].T,
        [(0, 0, gate_src),
         (gate_src + GLA_GATE_RANK, COL_SQ, swa_rows),
         (gate_src, COL_GG, GLA_GATE_RANK)],
        IN_COLS_PADDED, _divisor_tile(IN_COLS_PADDED, 512, 16))
    z = _matmul_split(xp, xs, w_in_rt, tm, IN_COLS_PADDED // 2, F32)

    wa2_pad = jnp.concatenate(
        [gla_w_gate2[l], jnp.zeros((LANES - GLA_GATE_RANK, GLA_QK), F32)], axis=0)
    wa2_hi = wa2_pad.astype(BF16)
    wa2_lo = (wa2_pad - wa2_hi.astype(F32)).astype(BF16)
    wa2_stack = jnp.concatenate([wa2_hi, wa2_lo, wa2_hi], axis=0)
    zero_state = jnp.zeros((nbp, GLA_HEADS, GLA_DK, GLA_DV), F32)
    gla_rows = _divisor_tile(seq, GLA_STEP_ROWS, CHUNK)
    mix, st_p = _gla(z, wa2_stack, gla_b_gate[l], gla_norm_g[l], zero_state, None,
                     nb=nbp, n_chunks=seq // gla_rows, rows=gla_rows, chunk=CHUNK, row0=0, chained=True)
    seqs = _divisor_tile(nbs, GLA_SAMPLE_SEQS, 1)
    assert tp % (seqs * dec) == 0
    mix, st_s = _gla(z, wa2_stack, gla_b_gate[l], gla_norm_g[l], state_gla[l], mix,
                     nb=nbs // seqs, n_chunks=1, rows=seqs * dec, chunk=dec, row0=tp, chained=False)
    nblk = seq // SWA_WINDOW
    prev_p = lambda col: (lambda b, i: (b * nblk + jnp.maximum(i - 1, 0), col))
    mix = _swa(z, swa_sinks[l], z, z, (prev_p(COL_SK // SWA_KVW), prev_p(COL_SV // SWA_KVW)), mix,
               nb=nbp, n_blocks=nblk, rows=SWA_WINDOW, row0=0, q_base=0, q_stride=SWA_WINDOW)
    ck = cache_swa_k[l].reshape(nbs * SWA_WINDOW, SWA_KVW)
    cv = cache_swa_v[l].reshape(nbs * SWA_WINDOW, SWA_KVW)
    prev_s = lambda b, i: (b, 0)
    mix = _swa(z, swa_sinks[l], ck, cv, (prev_s, prev_s), mix,
               nb=nbs, n_blocks=1, rows=dec, row0=tp, q_base=PAST_LEN, q_stride=0)
    h1 = _matmul_res_ln(mix, w_mix_out[l].astype(BF16), (xp, xs), ln1_g[l], ln1_b[l], tm)

    mem = mem_prompt.reshape(nbp * mem_tokens, d)
    tmem = _divisor_tile(nbp * mem_tokens, 512, 16)
    mk = _matmul(mem, mem_wk[l], tmem, _divisor_tile(d, 512, LANES), F32)
    mv = _matmul(mem, mem_wv[l], tmem, _divisor_tile(d, 512, LANES), F32)
    q = _matmul(h1, mem_wq[l].astype(BF16), tm, d, BF16)
    tq = _divisor_tile(seq, 512, 16)
    att = _mem_attn(q, mk, mv, None, nb=nbp, n_tiles=seq // tq, rows=tq, row0=0, mem_tokens=mem_tokens)
    mem_seqs = _divisor_tile(nbs, MEM_SAMPLE_SEQS, 1)
    assert tp % (mem_seqs * dec) == 0
    att = _mem_attn(q, cache_mem_k[l].reshape(nbs * mem_tokens, d), cache_mem_v[l].reshape(nbs * mem_tokens, d),
                    att, nb=nbs // mem_seqs, n_tiles=1, rows=mem_seqs * dec, row0=tp, mem_tokens=mem_tokens,
                    n_seq=mem_seqs)
    h2, h2_packed = _matmul_res_ln(att, mem_wo[l].astype(BF16), h1, ln2_g[l], ln2_b[l], tm, emit_packed=True)

    out_p, out_s = _moe_ln(h2, h2_packed, router_w[l], router_bias[l], exp_w_gate[l], exp_w_up[l],
                           exp_w_down[l], sh_w_gate[l], sh_w_up[l], sh_w_down[l], ln3_g[l], ln3_b[l], tp)

    y_prompt = out_p.reshape(nbp, seq, d)
    y_sample = out_s.reshape(nbs, dec, d)
    kv = z[:, COL_SK:COL_SK + 2 * SWA_KVW]
    kvp = kv[:tp].reshape(nbp, seq, 2 * SWA_KVW)[:, seq - SWA_WINDOW:]
    kvs = kv[tp:].reshape(nbs, dec, 2 * SWA_KVW)
    kv_shape = lambda a: a.reshape(a.shape[0], a.shape[1], SWA_KV_HEADS, SWA_HEAD_DIM)[None]
    mem_shape = lambda a: a.reshape(nbp, mem_tokens, MEM_HEADS, d // MEM_HEADS)[None]
    return (y_prompt, y_sample,
            st_p[None],
            kv_shape(kvp[..., :SWA_KVW]), kv_shape(kvp[..., SWA_KVW:]),
            mem_shape(mk), mem_shape(mv),
            st_s[None],
            kv_shape(kvs[..., :SWA_KVW]), kv_shape(kvs[..., SWA_KVW:]))
```

```python
import functools
import math

import jax
import jax.numpy as jnp
from jax import lax
from jax.experimental import pallas as pl
from jax.experimental.pallas import tpu as pltpu

F32 = jnp.float32
BF16 = jnp.bfloat16
I32 = jnp.int32

CHUNK = 64
PAST_LEN = 2048
GLA_HEADS = 8
GLA_DK = 64
GLA_DV = 128
GLA_QK = GLA_HEADS * GLA_DK
GLA_VW = GLA_HEADS * GLA_DV
GLA_GATE_RANK = 16
GLA_TAU = 16.0
SWA_HEADS = 16
SWA_KV_HEADS = 2
SWA_GROUP = SWA_HEADS // SWA_KV_HEADS
SWA_HEAD_DIM = 64
SWA_QW = SWA_HEADS * SWA_HEAD_DIM
SWA_KVW = SWA_KV_HEADS * SWA_HEAD_DIM
SWA_WINDOW = 128
WINDOW_CHUNKS = SWA_WINDOW // CHUNK
MEM_HEADS = 4
N_EXPERTS = 64
TOP_K = 8
N_GROUPS = 8
GROUP_SIZE = N_EXPERTS // N_GROUPS
TOPK_GROUPS = 4
ROUTED_SCALE = 2.5
DEPTH = 1
DEEPNORM_ALPHA = (2 * DEPTH) ** 0.25
LN_EPS = 1e-5
RMS_EPS = 1e-6

LANES = 128
SUBLANES = 8
VMEM_LIMIT_BYTES = 56 * 1024 * 1024

COL_GQ = 0
COL_GK = COL_GQ + GLA_QK
COL_GV = COL_GK + GLA_QK
COL_GR = COL_GV + GLA_VW
COL_SQ = COL_GR + GLA_VW
COL_SK = COL_SQ + SWA_QW
COL_SV = COL_SK + SWA_KVW
COL_GG = COL_SV + SWA_KVW
MXU_COLS = 256
IN_COLS_PADDED = -(-(COL_GG + LANES) // (2 * MXU_COLS)) * (2 * MXU_COLS)

EXPERT_ROWS = 256
EXPERT_X_SLOTS = 3
COMBINE_GROUP = SUBLANES
LN_SUB_ROWS = 128
GLA_STEP_ROWS = 256
GLA_SAMPLE_SEQS = 8
MEM_SAMPLE_SEQS = 2
COPY_SLOTS = 3

NT_DIMS = (((1,), (1,)), ((), ()))
TN_DIMS = (((0,), (0,)), ((), ()))


def _params(n_axes):
    return pltpu.CompilerParams(dimension_semantics=("arbitrary",) * n_axes,
                                vmem_limit_bytes=VMEM_LIMIT_BYTES)


def _sigmoid(x):
    return 1.0 / (1.0 + jnp.exp(-x))


def _silu(x):
    return x * _sigmoid(x)


def _layer_norm(x, g, b):
    mu = jnp.mean(x, axis=-1, keepdims=True)
    xc = x - mu
    var = jnp.mean(xc * xc, axis=-1, keepdims=True)
    return xc * lax.rsqrt(var + LN_EPS) * g + b


HI_MASK = -65536


def _round_to_bf16_bits(x):
    b = lax.bitcast_convert_type(x, I32)
    return b + (0x7FFF + (lax.shift_right_logical(b, 16) & 1))


def _pack_halves(x):
    c = x.shape[1] // 2
    lo = lax.shift_right_logical(_round_to_bf16_bits(x[:, :c]), 16)
    hi = _round_to_bf16_bits(x[:, c:]) & HI_MASK
    return lo | hi


def _unpack_halves(w):
    return (lax.bitcast_convert_type(lax.shift_left(w, 16), F32),
            lax.bitcast_convert_type(w & HI_MASK, F32))


def _divisor_tile(n, pref, mult):
    t = min(pref, n)
    while t > mult and (n % t or t % mult):
        t -= mult
    assert n % t == 0 and t % mult == 0, (n, pref, mult)
    return t


def _mm_kernel(x_ref, w_ref, o_ref, xb_ref):
    @pl.when(pl.program_id(1) == 0)
    def _():
        xb_ref[...] = x_ref[...].astype(BF16)

    o_ref[...] = jnp.dot(xb_ref[...], w_ref[...].astype(BF16), preferred_element_type=F32).astype(o_ref.dtype)


def _matmul(x, w, tm, tn, out_dtype):
    m, k = x.shape
    n = w.shape[1]
    return pl.pallas_call(
        _mm_kernel,
        grid=(m // tm, n // tn),
        in_specs=[pl.BlockSpec((tm, k), lambda i, j: (i, 0)),
                  pl.BlockSpec((k, tn), lambda i, j: (0, j))],
        out_specs=pl.BlockSpec((tm, tn), lambda i, j: (i, j)),
        out_shape=jax.ShapeDtypeStruct((m, n), out_dtype),
        scratch_shapes=[pltpu.VMEM((tm, k), BF16)],
        compiler_params=_params(2),
        name="matmul",
    )(x, w)


def _split_rows_specs(tm, k, n_first_tiles, n_grid_axes):
    if n_grid_axes == 1:
        return [pl.BlockSpec((tm, k), lambda i: (jnp.minimum(i, n_first_tiles - 1), 0)),
                pl.BlockSpec((tm, k), lambda i: (jnp.maximum(i - n_first_tiles, 0), 0))]
    return [pl.BlockSpec((tm, k), lambda j, i: (jnp.minimum(i, n_first_tiles - 1), 0)),
            pl.BlockSpec((tm, k), lambda j, i: (jnp.maximum(i - n_first_tiles, 0), 0))]


def _mm_split_kernel(xa_ref, xb_ref, wt_ref, o_ref, *, n_first_tiles):
    i = pl.program_id(1)

    def product(x_ref):
        o_ref[...] = lax.dot_general(x_ref[...].astype(BF16), wt_ref[...], NT_DIMS,
                                     preferred_element_type=F32).astype(o_ref.dtype)

    @pl.when(i < n_first_tiles)
    def _():
        product(xa_ref)

    @pl.when(i >= n_first_tiles)
    def _():
        product(xb_ref)


def _matmul_split(xa, xb, w_t, tm, tn, out_dtype):
    ma, k = xa.shape
    m = ma + xb.shape[0]
    n = w_t.shape[0]
    return pl.pallas_call(
        functools.partial(_mm_split_kernel, n_first_tiles=ma // tm),
        grid=(n // tn, m // tm),
        in_specs=_split_rows_specs(tm, k, ma // tm, 2) + [pl.BlockSpec((tn, k), lambda j, i: (j, 0))],
        out_specs=pl.BlockSpec((tm, tn), lambda j, i: (i, j)),
        out_shape=jax.ShapeDtypeStruct((m, n), out_dtype),
        compiler_params=_params(2),
        name="matmul_split",
    )(xa, xb, w_t)


def _reorder_cast_kernel(src_hbm, dst_hbm, stage_ref, out_ref, in_sems, out_sems, *, chunk_pieces, chunk):
    n = len(chunk_pieces)
    slots = stage_ref.shape[0]

    def in_copies(c):
        return [pltpu.make_async_copy(src_hbm.at[pl.ds(s0, ln)], stage_ref.at[c % slots, pl.ds(o0, ln)],
                                      in_sems.at[c % slots]) for s0, o0, ln in chunk_pieces[c]]

    def out_copy(c):
        return pltpu.make_async_copy(out_ref.at[c % slots], dst_hbm.at[pl.ds(c * chunk, chunk)],
                                     out_sems.at[c % slots])

    for c in range(min(slots - 1, n)):
        for cp in in_copies(c):
            cp.start()
    for c in range(n):
        if c + slots - 1 < n:
            for cp in in_copies(c + slots - 1):
                cp.start()
        for cp in in_copies(c):
            cp.wait()
        if c >= slots:
            out_copy(c - slots).wait()
        covered = sum(ln for _, _, ln in chunk_pieces[c])
        if covered < chunk:
            stage_ref[c % slots, covered:, :] = jnp.zeros((chunk - covered, stage_ref.shape[-1]),
                                                          stage_ref.dtype)
        out_ref[c % slots] = stage_ref[c % slots].astype(out_ref.dtype)
        out_copy(c).start()
    for c in range(max(n - slots, 0), n):
        out_copy(c).wait()


def _reorder_cast(src, runs, n_out_rows, chunk):
    k = src.shape[1]
    chunk_pieces = []
    for c0 in range(0, n_out_rows, chunk):
        pieces = []
        for s0, d0, ln in runs:
            lo, hi = max(d0, c0), min(d0 + ln, c0 + chunk)
            if lo < hi:
                assert (lo - c0) % 16 == 0 and (s0 + lo - d0) % SUBLANES == 0 and (hi - lo) % 16 == 0
                pieces.append((s0 + lo - d0, lo - c0, hi - lo))
        chunk_pieces.append(pieces)
    hbm = pl.BlockSpec(memory_space=pl.ANY)
    return pl.pallas_call(
        functools.partial(_reorder_cast_kernel, chunk_pieces=chunk_pieces, chunk=chunk),
        in_specs=[hbm],
        out_specs=hbm,
        out_shape=jax.ShapeDtypeStruct((n_out_rows, k), BF16),
        scratch_shapes=[pltpu.VMEM((COPY_SLOTS, chunk, k), src.dtype), pltpu.VMEM((COPY_SLOTS, chunk, k), BF16),
                        pltpu.SemaphoreType.DMA((COPY_SLOTS,)), pltpu.SemaphoreType.DMA((COPY_SLOTS,))],
        compiler_params=pltpu.CompilerParams(vmem_limit_bytes=VMEM_LIMIT_BYTES),
        name="reorder_cast",
    )(src)


def _mm_res_ln_kernel(x_ref, w_ref, *refs, n_res, n_first_tiles):
    res_refs, (g_ref, b_ref, o_ref, *packed_ref) = refs[:n_res], refs[n_res:]
    tm = x_ref.shape[0]
    sub = min(tm, LN_SUB_ROWS)
    use_first = pl.program_id(0) < n_first_tiles
    for r in range(0, tm, sub):
        rows = slice(r, r + sub)
        y = jnp.dot(x_ref[rows, :], w_ref[...], preferred_element_type=F32)
        res = res_refs[0][rows, :]
        if n_res == 2:
            res = jnp.where(use_first, res, res_refs[1][rows, :])
        h = _layer_norm(DEEPNORM_ALPHA * res + y, g_ref[...], b_ref[...])
        o_ref[rows, :] = h
        if packed_ref:
            groups = slice(r // SUBLANES, (r + sub) // SUBLANES)
            packed_ref[0][groups] = _pack_halves(h).reshape(sub // SUBLANES, SUBLANES, h.shape[1] // 2)


def _matmul_res_ln(x, w, res, g, b, tm, emit_packed=False):
    m, k = x.shape
    n = w.shape[1]
    if isinstance(res, tuple):
        n_first_tiles = res[0].shape[0] // tm
        res_specs = _split_rows_specs(tm, n, n_first_tiles, 1)
    else:
        n_first_tiles = 0
        res = (res,)
        res_specs = [pl.BlockSpec((tm, n), lambda i: (i, 0))]
    out_specs = [pl.BlockSpec((tm, n), lambda i: (i, 0))]
    out_shape = [jax.ShapeDtypeStruct((m, n), F32)]
    if emit_packed:
        out_specs.append(pl.BlockSpec((tm // SUBLANES, SUBLANES, n // 2), lambda i: (i, 0, 0)))
        out_shape.append(jax.ShapeDtypeStruct((m // SUBLANES, SUBLANES, n // 2), I32))
    outs = pl.pallas_call(
        functools.partial(_mm_res_ln_kernel, n_res=len(res), n_first_tiles=n_first_tiles),
        grid=(m // tm,),
        in_specs=[pl.BlockSpec((tm, k), lambda i: (i, 0)),
                  pl.BlockSpec((k, n), lambda i: (0, 0))] + res_specs + [
                  pl.BlockSpec((1, n), lambda i: (0, 0)),
                  pl.BlockSpec((1, n), lambda i: (0, 0))],
        out_specs=out_specs,
        out_shape=out_shape,
        compiler_params=_params(1),
        name="matmul_res_ln",
    )(x, w, *res, g.reshape(1, n), b.reshape(1, n))
    return outs if emit_packed else outs[0]


def _split3_bf16(x):
    hi = x.astype(BF16)
    r1 = x - hi.astype(F32)
    mid = r1.astype(BF16)
    lo = (r1 - mid.astype(F32)).astype(BF16)
    return jnp.concatenate([hi, mid, lo], axis=-1)


def _gla_kernel(q_ref, k_ref, v_ref, r_ref, gg_ref, wa2_ref, ba_ref, ng_ref, s0_ref, mix_in_ref,
                o_ref, sfin_ref, st_ref, *, rows, chunk, chained):
    del mix_in_ref
    n_sub = rows // chunk
    shift = chunk.bit_length() - 1
    assert chunk == 1 << shift

    if chained:
        @pl.when(pl.program_id(1) == 0)
        def _():
            for h in range(GLA_HEADS):
                st_ref[h] = s0_ref[0, h].T

    gg = gg_ref[...]
    gg_hi = gg.astype(BF16)
    gg_lo = (gg - gg_hi.astype(F32)).astype(BF16)
    gate = jnp.dot(jnp.concatenate([gg_hi, gg_hi, gg_lo], axis=-1), wa2_ref[...],
                   preferred_element_type=F32) + ba_ref[...]
    log_a = (jnp.minimum(gate, 0.0) - jnp.log1p(jnp.exp(-jnp.abs(gate)))) / GLA_TAU

    ri = lax.broadcasted_iota(I32, (rows, rows), 0)
    ci = lax.broadcasted_iota(I32, (rows, rows), 1)
    same_chunk = lax.shift_right_logical(ri, shift) == lax.shift_right_logical(ci, shift)
    causal = same_chunk & (ri >= ci)
    ones = jnp.concatenate([jnp.where(causal, 1.0, 0.0), jnp.where(same_chunk, 1.0, 0.0)], axis=0).astype(BF16)
    sums = jnp.dot(ones, _split3_bf16(log_a), preferred_element_type=F32)
    sums = sums[:, :GLA_QK] + sums[:, GLA_QK:2 * GLA_QK] + sums[:, 2 * GLA_QK:]
    bcum = sums[:rows]
    b_last = sums[rows:]
    q = q_ref[...] * (GLA_DK ** -0.5)
    k = k_ref[...]
    q_in = (q * jnp.exp(bcum)).astype(BF16)
    k_in = (k * jnp.exp(-bcum)).astype(BF16)
    k_out = (k * jnp.exp(b_last - bcum)).astype(BF16)
    decay = jnp.exp(b_last)
    ng = ng_ref[...]
    blk_r = lax.shift_right_logical(lax.broadcasted_iota(I32, (rows, n_sub * GLA_DK), 0), shift)
    blk_c = lax.shift_right_logical(lax.broadcasted_iota(I32, (rows, n_sub * GLA_DK), 1),
                                    GLA_DK.bit_length() - 1)
    def heads(x, width):
        return jnp.stack([x[:, h * width:(h + 1) * width] for h in range(GLA_HEADS)], axis=0)

    q3 = heads(q_in, GLA_DK)
    v3 = heads(v_ref[...].astype(BF16), GLA_DV)
    attn = jnp.einsum("hqd,hkd->hqk", q3, heads(k_in, GLA_DK), preferred_element_type=F32)
    attn = jnp.where(causal[None], attn, 0.0).astype(BF16)
    o = jnp.einsum("hqk,hke->hqe", attn, v3, preferred_element_type=F32)
    ko = heads(k_out, GLA_DK)
    if n_sub > 1:
        ko = jnp.where((blk_r == blk_c)[None], jnp.concatenate([ko] * n_sub, axis=-1), 0.0)
    upd_t = jnp.einsum("hre,hrc->hec", v3, ko, preferred_element_type=F32)
    states = []
    if chained:
        st = st_ref[...]
    for c in range(n_sub):
        if not chained:
            st = jnp.stack([s0_ref[c, h].T for h in range(GLA_HEADS)], axis=0)
        states.append(st.astype(BF16))
        st = st * heads(decay[c * chunk:c * chunk + 1, :], GLA_DK) + upd_t[:, :, c * GLA_DK:(c + 1) * GLA_DK]
        if not chained:
            for h in range(GLA_HEADS):
                sfin_ref[c, h] = st[h].T
    if chained:
        st_ref[...] = st
    o_all = jnp.einsum("hqd,hsd->hqs", q3, jnp.concatenate(states, axis=1), preferred_element_type=F32)
    o = o + jnp.concatenate(
        [o_all[:, c * chunk:(c + 1) * chunk, c * GLA_DV:(c + 1) * GLA_DV] for c in range(n_sub)], axis=1)
    o = o * lax.rsqrt(jnp.mean(o * o, axis=-1, keepdims=True) + RMS_EPS) * ng[None]
    out = (o * _silu(heads(r_ref[...], GLA_DV))).astype(o_ref.dtype)
    for h in range(GLA_HEADS):
        o_ref[:, h * GLA_DV:(h + 1) * GLA_DV] = out[h]
    if chained:
        @pl.when(pl.program_id(1) == pl.num_programs(1) - 1)
        def _():
            for h in range(GLA_HEADS):
                sfin_ref[0, h] = st_ref[h].T


def _gla(z, wa2_stack, ba, norm_g, s0, mix_in, *, nb, n_chunks, rows, chunk, row0, chained):
    t = z.shape[0]
    rb0 = row0 // rows
    seq_per_step = 1 if chained else rows // chunk
    assert chained or n_chunks == 1
    rmap = lambda colblk: (lambda b, c: (rb0 + b * n_chunks + c, colblk))
    in_specs = [
        pl.BlockSpec((rows, GLA_QK), rmap(COL_GQ // GLA_QK)),
        pl.BlockSpec((rows, GLA_QK), rmap(COL_GK // GLA_QK)),
        pl.BlockSpec((rows, GLA_VW), rmap(COL_GV // GLA_VW)),
        pl.BlockSpec((rows, GLA_VW), rmap(COL_GR // GLA_VW)),
        pl.BlockSpec((rows, LANES), rmap(COL_GG // LANES)),
        pl.BlockSpec((3 * LANES, GLA_QK), lambda b, c: (0, 0)),
        pl.BlockSpec((1, GLA_QK), lambda b, c: (0, 0)),
        pl.BlockSpec((1, GLA_DV), lambda b, c: (0, 0)),
        pl.BlockSpec((seq_per_step, GLA_HEADS, GLA_DK, GLA_DV), lambda b, c: (b, 0, 0, 0)),
    ]
    args = [z, z, z, z, z, wa2_stack, ba.reshape(1, GLA_QK), norm_g.reshape(1, GLA_DV), s0]
    aliases = {}
    if mix_in is None:
        mix_in = jnp.zeros((SUBLANES, LANES), BF16)
        in_specs.append(pl.BlockSpec(memory_space=pl.ANY))
    else:
        in_specs.append(pl.BlockSpec(memory_space=pl.ANY))
        aliases = {len(args): 0}
    args.append(mix_in)
    mix, s_fin = pl.pallas_call(
        functools.partial(_gla_kernel, rows=rows, chunk=chunk, chained=chained),
        grid=(nb, n_chunks),
        in_specs=in_specs,
        out_specs=[pl.BlockSpec((rows, GLA_VW), rmap(0)),
                   pl.BlockSpec((seq_per_step, GLA_HEADS, GLA_DK, GLA_DV), lambda b, c: (b, 0, 0, 0))],
        out_shape=[jax.ShapeDtypeStruct((t, GLA_VW + SWA_QW), BF16),
                   jax.ShapeDtypeStruct((nb * seq_per_step, GLA_HEADS, GLA_DK, GLA_DV), F32)],
        scratch_shapes=[pltpu.VMEM((GLA_HEADS, GLA_DV, GLA_DK), F32)],
        input_output_aliases=aliases,
        compiler_params=_params(2),
        name="gla",
    )(*args)
    return mix, s_fin


def _swa_kernel(sinks_ref, q_ref, kc_ref, vc_ref, kp_ref, vp_ref, mix_in_ref, o_ref, bias_ref, *,
                rows, q_base, q_stride):
    del mix_in_ref
    i = pl.program_id(1)
    n_tables = bias_ref.shape[0]
    nk = SWA_WINDOW + rows
    sr = SWA_GROUP * rows
    assert rows & (rows - 1) == 0 and q_base % CHUNK == 0 and q_stride % CHUNK == 0
    assert n_tables == 1 or q_base + q_stride >= SWA_WINDOW
    head_in_group = lax.shift_right_logical(lax.broadcasted_iota(I32, (sr, 1), 0), rows.bit_length() - 1)

    @pl.when(jnp.logical_and(pl.program_id(0) == 0, i == 0))
    def _():
        row = lax.broadcasted_iota(I32, (sr, nk), 0)
        col = lax.broadcasted_iota(I32, (sr, nk), 1)
        for t in range(n_tables):
            q0 = q_base + t * q_stride
            qpos = q0 + (row & (rows - 1))
            kpos = q0 - SWA_WINDOW + col
            dist = jnp.abs(qpos - kpos).astype(F32)
            qc = lax.shift_right_arithmetic(qpos, CHUNK.bit_length() - 1)
            kc = lax.shift_right_arithmetic(kpos, CHUNK.bit_length() - 1)
            allowed = (kpos >= 0) & (kc <= qc) & (kc >= qc - WINDOW_CHUNKS)
            for g in range(SWA_KV_HEADS):
                slope = jnp.zeros((sr, 1), F32)
                for j in range(SWA_GROUP):
                    h = g * SWA_GROUP + j
                    slope = jnp.where(head_in_group == j, 2.0 ** (-8.0 * (h + 1) / SWA_HEADS), slope)
                bias_ref[t, g] = jnp.where(allowed, -(slope * dist), -jnp.inf)

    table = jnp.minimum(i, n_tables - 1)
    keys = jnp.concatenate([kp_ref[...], kc_ref[...]], axis=0).astype(BF16)
    vals = jnp.concatenate([vp_ref[...], vc_ref[...]], axis=0).astype(BF16)
    for g in range(SWA_KV_HEADS):
        sink = jnp.zeros((sr, 1), F32)
        for j in range(SWA_GROUP):
            sink = jnp.where(head_in_group == j, sinks_ref[g * SWA_GROUP + j], sink)
        kv = slice(g * SWA_HEAD_DIM, (g + 1) * SWA_HEAD_DIM)
        qg = jnp.concatenate(
            [q_ref[:, (g * SWA_GROUP + j) * SWA_HEAD_DIM:(g * SWA_GROUP + j + 1) * SWA_HEAD_DIM]
             for j in range(SWA_GROUP)], axis=0).astype(BF16)
        s = lax.dot_general(qg, keys[:, kv], NT_DIMS, preferred_element_type=F32)
        s = s * (SWA_HEAD_DIM ** -0.5) + bias_ref[table, g]
        m = jnp.maximum(jnp.max(s, axis=-1, keepdims=True), sink)
        p = jnp.exp(s - m)
        denom = jnp.sum(p, axis=-1, keepdims=True) + jnp.exp(sink - m)
        o = jnp.dot(p.astype(BF16), vals[:, kv], preferred_element_type=F32) / denom
        for j in range(0, SWA_GROUP, 2):
            c0 = (g * SWA_GROUP + j) * SWA_HEAD_DIM
            o_ref[:, c0:c0 + LANES] = jnp.concatenate(
                [o[j * rows:(j + 1) * rows], o[(j + 1) * rows:(j + 2) * rows]], axis=-1).astype(o_ref.dtype)


def _swa(z, sinks, k_prev, v_prev, prev_map, mix_in, *, nb, n_blocks, rows, row0, q_base, q_stride):
    rb0 = row0 // rows
    rmap = lambda colblk: (lambda b, i: (rb0 + b * n_blocks + i, colblk))
    return pl.pallas_call(
        functools.partial(_swa_kernel, rows=rows, q_base=q_base, q_stride=q_stride),
        grid=(nb, n_blocks),
        in_specs=[pl.BlockSpec(memory_space=pltpu.SMEM),
                  pl.BlockSpec((rows, SWA_QW), rmap(COL_SQ // SWA_QW)),
                  pl.BlockSpec((rows, SWA_KVW), rmap(COL_SK // SWA_KVW)),
                  pl.BlockSpec((rows, SWA_KVW), rmap(COL_SV // SWA_KVW)),
                  pl.BlockSpec((SWA_WINDOW, SWA_KVW), prev_map[0]),
                  pl.BlockSpec((SWA_WINDOW, SWA_KVW), prev_map[1]),
                  pl.BlockSpec(memory_space=pl.ANY)],
        out_specs=pl.BlockSpec((rows, SWA_QW), rmap(GLA_VW // SWA_QW)),
        out_shape=jax.ShapeDtypeStruct(mix_in.shape, mix_in.dtype),
        scratch_shapes=[pltpu.VMEM((min(2, n_blocks), SWA_KV_HEADS, SWA_GROUP * rows, SWA_WINDOW + rows), F32)],
        input_output_aliases={6: 0},
        compiler_params=_params(2),
        name="swa",
    )(sinks, z, z, z, k_prev, v_prev, mix_in)


def _mem_attn_kernel(q_ref, k_ref, v_ref, o_in_ref, o_ref, *, head_dim, n_seq):
    del o_in_ref
    rq = q_ref.shape[0] // n_seq
    rm = k_ref.shape[0] // n_seq
    for s_i in range(n_seq):
        qr = slice(s_i * rq, (s_i + 1) * rq)
        mr = slice(s_i * rm, (s_i + 1) * rm)
        heads = [slice(h * head_dim, (h + 1) * head_dim) for h in range(MEM_HEADS)]
        q3 = jnp.stack([q_ref[qr, hs] for hs in heads], axis=0)
        k3 = jnp.stack([k_ref[mr, hs] for hs in heads], axis=0).astype(BF16)
        v3 = jnp.stack([v_ref[mr, hs] for hs in heads], axis=0).astype(BF16)
        s = jnp.einsum("hqd,hkd->hqk", q3, k3, preferred_element_type=F32) * (head_dim ** -0.5)
        m = jnp.max(s, axis=-1, keepdims=True)
        p = jnp.exp(s - m)
        denom = jnp.sum(p, axis=-1, keepdims=True)
        o = jnp.einsum("hqk,hkd->hqd", p.astype(BF16), v3, preferred_element_type=F32) / denom
        for h, hs in enumerate(heads):
            o_ref[qr, hs] = o[h].astype(o_ref.dtype)


def _mem_attn(q, mk, mv, o_in, *, nb, n_tiles, rows, row0, mem_tokens, n_seq=1):
    t, d = q.shape
    assert n_seq == 1 or n_tiles == 1
    rb0 = row0 // rows
    qmap = lambda b, i: (rb0 + b * n_tiles + i, 0)
    in_specs = [pl.BlockSpec((rows, d), qmap),
                pl.BlockSpec((n_seq * mem_tokens, d), lambda b, i: (b, 0)),
                pl.BlockSpec((n_seq * mem_tokens, d), lambda b, i: (b, 0)),
                pl.BlockSpec(memory_space=pl.ANY)]
    aliases = {}
    if o_in is None:
        o_in = jnp.zeros((SUBLANES, LANES), BF16)
    else:
        aliases = {3: 0}
    return pl.pallas_call(
        functools.partial(_mem_attn_kernel, head_dim=d // MEM_HEADS, n_seq=n_seq),
        grid=(nb, n_tiles),
        in_specs=in_specs,
        out_specs=pl.BlockSpec((rows, d), qmap),
        out_shape=jax.ShapeDtypeStruct((t, d), BF16),
        input_output_aliases=aliases,
        compiler_params=_params(2),
        name="mem_attn",
    )(q, mk, mv, o_in)


def _pick_first_max(cur, iota, sentinel):
    mx = jnp.max(cur, axis=0, keepdims=True)
    first = jnp.min(jnp.where(cur == mx, iota, sentinel), axis=0, keepdims=True)
    return iota == first, first


def _router_kernel(x_ref, wh_ref, wl_ref, bias_ref, idx_ref, gate_ref, rank_ref, cnt_ref, run_ref, before_ref,
                   *, tm):
    i = pl.program_id(0)

    @pl.when(i == 0)
    def _():
        run_ref[...] = jnp.zeros_like(run_ref)
        ti = lax.broadcasted_iota(I32, (tm, tm), 0)
        tj = lax.broadcasted_iota(I32, (tm, tm), 1)
        before_ref[...] = jnp.where(ti < tj, 1.0, 0.0).astype(BF16)

    x = x_ref[...]
    xh = x.astype(BF16)
    xl = (x - xh.astype(F32)).astype(BF16)
    wh = wh_ref[...]
    logits = (lax.dot_general(wh, xh, NT_DIMS, preferred_element_type=F32)
              + lax.dot_general(wh, xl, NT_DIMS, preferred_element_type=F32)
              + lax.dot_general(wl_ref[...], xh, NT_DIMS, preferred_element_type=F32))
    scores = _sigmoid(logits)
    sel = scores + bias_ref[...]
    neg_inf = -jnp.inf

    li = lax.broadcasted_iota(I32, (GROUP_SIZE, tm), 0)
    grp_rows = []
    for g in range(N_GROUPS):
        blk = sel[g * GROUP_SIZE:(g + 1) * GROUP_SIZE, :]
        pick, _ = _pick_first_max(blk, li, GROUP_SIZE)
        m1 = jnp.max(blk, axis=0, keepdims=True)
        m2 = jnp.max(jnp.where(pick, neg_inf, blk), axis=0, keepdims=True)
        grp_rows.append(m1 + m2)
    grp = jnp.concatenate(grp_rows, axis=0)

    gi = lax.broadcasted_iota(I32, (N_GROUPS, tm), 0)
    gsel = jnp.zeros((N_GROUPS, tm), F32)
    cur = grp
    for _ in range(TOPK_GROUPS):
        pick, _ = _pick_first_max(cur, gi, N_GROUPS)
        gsel = jnp.where(pick, 1.0, gsel)
        cur = jnp.where(pick, neg_inf, cur)
    emask = jnp.concatenate(
        [jnp.broadcast_to(gsel[g:g + 1, :], (GROUP_SIZE, tm)) for g in range(N_GROUPS)], axis=0)

    ei = lax.broadcasted_iota(I32, (N_EXPERTS, tm), 0)
    cur = jnp.where(emask > 0.5, sel, neg_inf)
    chosen = jnp.zeros((N_EXPERTS, tm), F32)
    idx_rows, w_rows = [], []
    for _ in range(TOP_K):
        pick, first = _pick_first_max(cur, ei, N_EXPERTS)
        idx_rows.append(first)
        w_rows.append(jnp.sum(jnp.where(pick, scores, 0.0), axis=0, keepdims=True))
        chosen = jnp.where(pick, 1.0, chosen)
        cur = jnp.where(pick, neg_inf, cur)
    idx = jnp.concatenate(idx_rows, axis=0)
    w = jnp.concatenate(w_rows, axis=0)
    gate_ref[...] = w / jnp.sum(w, axis=0, keepdims=True) * ROUTED_SCALE
    idx_ref[...] = idx

    local = jnp.dot(chosen.astype(BF16), before_ref[...], preferred_element_type=F32)
    total = local + run_ref[:, 0:1]
    rank_rows = [jnp.sum(jnp.where(ei == idx_rows[k], total, 0.0), axis=0, keepdims=True)
                 for k in range(TOP_K)]
    rank_ref[...] = jnp.concatenate(rank_rows, axis=0).astype(I32)
    run_ref[...] = run_ref[...] + jnp.sum(chosen, axis=1, keepdims=True)
    cnt_ref[...] = run_ref[...].astype(I32)


def _router(x, w_hi_t, w_lo_t, bias, tm):
    t, d = x.shape
    return pl.pallas_call(
        functools.partial(_router_kernel, tm=tm),
        grid=(t // tm,),
        in_specs=[pl.BlockSpec((tm, d), lambda i: (i, 0)),
                  pl.BlockSpec((N_EXPERTS, d), lambda i: (0, 0)),
                  pl.BlockSpec((N_EXPERTS, d), lambda i: (0, 0)),
                  pl.BlockSpec((N_EXPERTS, 1), lambda i: (0, 0))],
        out_specs=[pl.BlockSpec((TOP_K, tm), lambda i: (0, i)),
                   pl.BlockSpec((TOP_K, tm), lambda i: (0, i)),
                   pl.BlockSpec((TOP_K, tm), lambda i: (0, i)),
                   pl.BlockSpec((N_EXPERTS, LANES), lambda i: (0, 0))],
        out_shape=[jax.ShapeDtypeStruct((TOP_K, t), I32),
                   jax.ShapeDtypeStruct((TOP_K, t), F32),
                   jax.ShapeDtypeStruct((TOP_K, t), I32),
                   jax.ShapeDtypeStruct((N_EXPERTS, LANES), I32)],
        scratch_shapes=[pltpu.VMEM((N_EXPERTS, LANES), F32), pltpu.VMEM((tm, tm), BF16)],
        compiler_params=_params(1),
        name="router",
    )(x, w_hi_t, w_lo_t, bias.reshape(N_EXPERTS, 1))


PAD_BITS = tuple(1 << s for s in reversed(range(EXPERT_ROWS.bit_length() - 1)))


def _dispatch_kernel(zstart_ref, zcount_ref, dest_ref, x_ref, swg_ref, swu_ref, swd_ref, xs_ref, shared_ref,
                     zero_ref, sem, zsem, *, tm):
    i = pl.program_id(0)

    def start_group(g, carry):
        for j in range(SUBLANES):
            for k in range(TOP_K):
                d = dest_ref[k * tm + g * SUBLANES + j]
                pltpu.make_async_copy(x_ref.at[g, pl.ds(j, 1), :], xs_ref.at[d], sem).start(priority=k % 2)
        return carry

    lax.fori_loop(0, tm // SUBLANES, start_group, 0)

    half = x_ref.shape[-1]
    x_lo, x_hi = _unpack_halves(x_ref[...].reshape(tm, half))
    xb = jnp.concatenate([x_lo.astype(BF16), x_hi.astype(BF16)], axis=-1)
    hg = jnp.dot(xb, swg_ref[...], preferred_element_type=F32)
    hu = jnp.dot(xb, swu_ref[...], preferred_element_type=F32)
    shared_ref[...] = jnp.dot((_silu(hg) * hu).astype(BF16), swd_ref[...],
                              preferred_element_type=F32).astype(shared_ref.dtype)

    for k in range(TOP_K):
        pltpu.make_async_copy(xs_ref.at[pl.ds(0, tm)], xs_ref.at[pl.ds(0, tm)], sem).wait()

    @pl.when(i == pl.num_programs(0) - 1)
    def _():
        zero_ref[...] = jnp.zeros_like(zero_ref)

        def fill(e, wait):
            pos = zstart_ref[e]
            cnt = zcount_ref[e]
            for bit in PAD_BITS:
                has = (cnt & bit) != 0

                @pl.when(has)
                def _():
                    cp = pltpu.make_async_copy(zero_ref.at[pl.ds(0, bit)], xs_ref.at[pl.ds(pos, bit)], zsem)
                    if wait:
                        cp.wait()
                    else:
                        cp.start()

                pos = pos + jnp.where(has, bit, 0)

        def fill_start(e, carry):
            fill(e, False)
            return carry

        def fill_wait(e, carry):
            fill(e, True)
            return carry

        lax.fori_loop(0, N_EXPERTS, fill_start, 0)
        lax.fori_loop(0, N_EXPERTS, fill_wait, 0)


def _dispatch(x, dest_tiles, zstart, zcount, swg, swu, swd, n_rows, tm):
    t = x.shape[0] * SUBLANES
    d = x.shape[2]
    ds = swg.shape[1]
    const = lambda i, zs, zc: (0, 0)
    return pl.pallas_call(
        functools.partial(_dispatch_kernel, tm=tm),
        grid_spec=pltpu.PrefetchScalarGridSpec(
            num_scalar_prefetch=2,
            grid=(t // tm,),
            in_specs=[pl.BlockSpec((TOP_K * tm,), lambda i, zs, zc: (i,), memory_space=pltpu.SMEM),
                      pl.BlockSpec((tm // SUBLANES, SUBLANES, d), lambda i, zs, zc: (i, 0, 0)),
                      pl.BlockSpec((2 * d, ds), const),
                      pl.BlockSpec((2 * d, ds), const),
                      pl.BlockSpec((ds, 2 * d), const)],
            out_specs=[pl.BlockSpec(memory_space=pl.ANY),
                       pl.BlockSpec((tm, 2 * d), lambda i, zs, zc: (i, 0))],
            scratch_shapes=[pltpu.VMEM((EXPERT_ROWS // 2, 1, d), x.dtype),
                            pltpu.SemaphoreType.DMA(()),
                            pltpu.SemaphoreType.DMA(())]),
        out_shape=[jax.ShapeDtypeStruct((n_rows, 1, d), x.dtype),
                   jax.ShapeDtypeStruct((t, 2 * d), BF16)],
        compiler_params=_params(1),
        name="moe_dispatch",
    )(zstart, zcount, dest_tiles, x, swg, swu, swd)


def _expert_kernel(be_ref, nxt_ref, short_ref, nv_ref, xs_hbm, wg_hbm, wu_hbm, wd_hbm, ys_hbm,
                   wgs_ref, wus_ref, wds_ref, wgb_ref, wub_ref, wdb_ref, xbuf_ref, ybuf_ref,
                   wsems, xsems, ysems):
    b = pl.program_id(0)
    n_valid = nv_ref[0]
    e = be_ref[b]
    prev = be_ref[jnp.maximum(b - 1, 0)]
    fresh = jnp.logical_or(b == 0, e != prev)
    slot = lax.rem(b, 2)

    def weight_copies(expert):
        return (pltpu.make_async_copy(wg_hbm.at[expert], wgs_ref, wsems.at[0]),
                pltpu.make_async_copy(wu_hbm.at[expert], wus_ref, wsems.at[1]),
                pltpu.make_async_copy(wd_hbm.at[expert], wds_ref, wsems.at[2]))

    def rows_in(blk, to_slot):
        r0 = pl.multiple_of(blk * EXPERT_ROWS, EXPERT_ROWS)
        return pltpu.make_async_copy(xs_hbm.at[pl.ds(r0, EXPERT_ROWS), 0, :], xbuf_ref.at[to_slot],
                                     xsems.at[to_slot])

    def rows_out(blk, from_slot):
        r0 = pl.multiple_of(blk * EXPERT_ROWS, EXPERT_ROWS)
        return pltpu.make_async_copy(ybuf_ref.at[from_slot], ys_hbm.at[pl.ds(r0, EXPERT_ROWS), 0, :],
                                     ysems.at[from_slot])

    x_slots = xbuf_ref.shape[0]
    xslot = lax.rem(b, x_slots)

    @pl.when(b == 0)
    def _():
        for cp in weight_copies(e):
            cp.start(priority=1)
        for ahead in range(x_slots - 1):
            @pl.when(ahead < n_valid)
            def _():
                rows_in(ahead, ahead).start()

    @pl.when(fresh)
    def _():
        for cp in weight_copies(e):
            cp.wait()
        wgb_ref[...] = wgs_ref[...].astype(BF16)
        wub_ref[...] = wus_ref[...].astype(BF16)
        wdb_ref[...] = wds_ref[...].astype(BF16)
        nxt = nxt_ref[b]

        @pl.when(nxt >= 0)
        def _():
            for cp in weight_copies(nxt):
                cp.start(priority=1)

    @pl.when(b < n_valid)
    def _():
        rows_in(b, xslot).wait()

        @pl.when(b + x_slots - 1 < n_valid)
        def _():
            rows_in(b + x_slots - 1, lax.rem(b + x_slots - 1, x_slots)).start()

        @pl.when(b >= 2)
        def _():
            rows_out(b - 2, slot).wait()

        def swiglu_rows(rows):
            half = xbuf_ref.shape[-1]
            x_lo, x_hi = _unpack_halves(xbuf_ref[xslot, :rows, :])
            xb = jnp.concatenate([x_lo.astype(BF16), x_hi.astype(BF16)], axis=-1)
            hg = jnp.dot(xb, wgb_ref[...], preferred_element_type=F32)
            hu = jnp.dot(xb, wub_ref[...], preferred_element_type=F32)
            act = (_silu(hg) * hu).astype(BF16)
            ybuf_ref[slot, :rows, :] = _pack_halves(jnp.dot(act, wdb_ref[...], preferred_element_type=F32))
            if rows < EXPERT_ROWS:
                ybuf_ref[slot, rows:, :] = jnp.zeros((EXPERT_ROWS - rows, half), ybuf_ref.dtype)

        pl.when(short_ref[b] == 1)(functools.partial(swiglu_rows, EXPERT_ROWS // 2))
        pl.when(short_ref[b] == 0)(functools.partial(swiglu_rows, EXPERT_ROWS))
        rows_out(b, slot).start()

        @pl.when(b == n_valid - 1)
        def _():
            rows_out(b, slot).wait()

            @pl.when(b >= 1)
            def _():
                rows_out(b - 1, 1 - slot).wait()


def _experts(xs, wg, wu, wd, block_expert, next_expert, short_block, n_valid):
    n_rows = xs.shape[0]
    d = wg.shape[1]
    de = wg.shape[2]
    nb = n_rows // EXPERT_ROWS
    hbm = pl.BlockSpec(memory_space=pl.ANY)
    return pl.pallas_call(
        _expert_kernel,
        grid_spec=pltpu.PrefetchScalarGridSpec(
            num_scalar_prefetch=4,
            grid=(nb,),
            in_specs=[hbm, hbm, hbm, hbm],
            out_specs=hbm,
            scratch_shapes=[pltpu.VMEM((d, de), F32), pltpu.VMEM((d, de), F32), pltpu.VMEM((de, d), F32),
                            pltpu.VMEM((d, de), BF16), pltpu.VMEM((d, de), BF16), pltpu.VMEM((de, d), BF16),
                            pltpu.VMEM((EXPERT_X_SLOTS, EXPERT_ROWS, d // 2), I32),
                            pltpu.VMEM((2, EXPERT_ROWS, d // 2), I32),
                            pltpu.SemaphoreType.DMA((3,)), pltpu.SemaphoreType.DMA((EXPERT_X_SLOTS,)),
                            pltpu.SemaphoreType.DMA((2,))]),
        out_shape=jax.ShapeDtypeStruct((n_rows, 1, d // 2), I32),
        compiler_params=_params(1),
        name="moe_experts",
    )(block_expert, next_expert, short_block, n_valid, xs, wg, wu, wd)


def _combine_kernel(dest0_ref, destn_ref, x_ref, shared_ref, gate_ref, ys_ref, g_ref, b_ref,
                    op_ref, os_ref, buf_ref, acc_ref, sems, *, tm, n_prompt_tiles):
    i = pl.program_id(0)
    n = pl.num_programs(0)
    half = buf_ref.shape[-1]
    n_groups = tm // COMBINE_GROUP

    def issue_group(dest_ref, to_slot, g):
        for j in range(COMBINE_GROUP):
            t = g * COMBINE_GROUP + j
            for k in range(TOP_K):
                d = dest_ref[k * tm + t]
                pltpu.make_async_copy(ys_ref.at[d], buf_ref.at[to_slot, k, g, pl.ds(j, 1), :],
                                      sems.at[to_slot]).start(priority=k % 2)

    @pl.when(i == 0)
    def _():
        def first(g, carry):
            issue_group(dest0_ref, 0, g)
            return carry
        lax.fori_loop(0, n_groups, first, 0)

    def run_tile(slot):
        for k in range(TOP_K):
            pltpu.make_async_copy(ys_ref.at[pl.ds(0, tm)], ys_ref.at[pl.ds(0, tm)], sems.at[slot]).wait()

        def reduce_group(g):
            r0 = pl.multiple_of(g * COMBINE_GROUP, COMBINE_GROUP)
            gate = gate_ref[pl.ds(r0, COMBINE_GROUP), :]
            acc_lo = acc_hi = None
            for k in range(TOP_K):
                y_lo, y_hi = _unpack_halves(buf_ref[slot, k, g])
                gk = gate[:, k:k + 1]
                acc_lo = gk * y_lo if k == 0 else acc_lo + gk * y_lo
                acc_hi = gk * y_hi if k == 0 else acc_hi + gk * y_hi
            acc_ref[pl.ds(r0, COMBINE_GROUP), :half] = acc_lo
            acc_ref[pl.ds(r0, COMBINE_GROUP), half:] = acc_hi

        @pl.when(i + 1 < n)
        def _():
            def body(g, carry):
                issue_group(destn_ref, 1 - slot, g)
                reduce_group(g)
                return carry
            lax.fori_loop(0, n_groups, body, 0)

        @pl.when(i + 1 >= n)
        def _():
            def body(g, carry):
                reduce_group(g)
                return carry
            lax.fori_loop(0, n_groups, body, 0)

    for s in range(2):
        pl.when(lax.rem(i, 2) == s)(functools.partial(run_tile, s))

    x = x_ref[...]
    out = _layer_norm(DEEPNORM_ALPHA * x + (acc_ref[...] + shared_ref[...].astype(F32)),
                      g_ref[...], b_ref[...])

    @pl.when(i < n_prompt_tiles)
    def _():
        op_ref[...] = out

    @pl.when(i >= n_prompt_tiles)
    def _():
        os_ref[...] = out


def _combine(x, shared, gate_t, dest_tiles, ys, g, b, tm, n_prompt_rows):
    t, d = x.shape
    n_tiles = t // tm
    npt = n_prompt_rows // tm
    const = lambda i: (0, 0)
    return pl.pallas_call(
        functools.partial(_combine_kernel, tm=tm, n_prompt_tiles=npt),
        grid=(n_tiles,),
        in_specs=[pl.BlockSpec((TOP_K * tm,), lambda i: (0,), memory_space=pltpu.SMEM),
                  pl.BlockSpec((TOP_K * tm,), lambda i: (jnp.minimum(i + 1, n_tiles - 1),),
                               memory_space=pltpu.SMEM),
                  pl.BlockSpec((tm, d), lambda i: (i, 0)),
                  pl.BlockSpec((tm, d), lambda i: (i, 0)),
                  pl.BlockSpec((tm, TOP_K), lambda i: (i, 0)),
                  pl.BlockSpec(memory_space=pl.ANY),
                  pl.BlockSpec((1, d), const),
                  pl.BlockSpec((1, d), const)],
        out_specs=[pl.BlockSpec((tm, d), lambda i: (jnp.minimum(i, npt - 1), 0)),
                   pl.BlockSpec((tm, d), lambda i: (jnp.maximum(i - npt, 0), 0))],
        out_shape=[jax.ShapeDtypeStruct((n_prompt_rows, d), F32),
                   jax.ShapeDtypeStruct((t - n_prompt_rows, d), F32)],
        scratch_shapes=[pltpu.VMEM((2, TOP_K, tm // COMBINE_GROUP, COMBINE_GROUP, d // 2), I32),
                        pltpu.VMEM((tm, d), F32),
                        pltpu.SemaphoreType.DMA((2,))],
        compiler_params=_params(1),
        name="moe_combine",
    )(dest_tiles, dest_tiles, x, shared, gate_t, ys, g.reshape(1, d), b.reshape(1, d))


def _tile_major(a, tm):
    k, t = a.shape
    return a.reshape(k, t // tm, tm).transpose(1, 0, 2).reshape(-1)


def _moe_ln(x, x_packed, router_w, router_bias, wg, wu, wd, swg, swu, swd, g, b, n_prompt_rows):
    t, d = x.shape
    w_t = router_w.T
    w_hi = w_t.astype(BF16)
    w_lo = (w_t - w_hi.astype(F32)).astype(BF16)
    tr = _divisor_tile(t, 512, LANES)
    idx, gate, rank, cnt = _router(x, w_hi, w_lo, router_bias, tr)

    counts = cnt[:, 0]
    padded = (counts + EXPERT_ROWS - 1) // EXPERT_ROWS * EXPERT_ROWS
    pad_end = jnp.cumsum(padded)
    pad_start = pad_end - padded
    experts = jnp.arange(N_EXPERTS, dtype=I32)
    onehot = idx[None] == experts[:, None, None]
    dest = jnp.sum(jnp.where(onehot, pad_start[:, None, None], 0), axis=0) + rank
    n_blocks = (t * TOP_K + N_EXPERTS * (EXPERT_ROWS - 1)) // EXPERT_ROWS + 1
    n_valid = (pad_end[-1] // EXPERT_ROWS).astype(I32)
    blk = jnp.minimum(jnp.arange(n_blocks, dtype=I32), n_valid - 1)
    block_expert = jnp.minimum(
        jnp.sum(pad_end[None, :] <= (blk * EXPERT_ROWS)[:, None], axis=1), N_EXPERTS - 1).astype(I32)
    later_used = (experts[None, :] > experts[:, None]) & (counts[None, :] > 0)
    next_used = jnp.min(jnp.where(later_used, experts[None, :], N_EXPERTS), axis=1)
    next_used = jnp.where(next_used == N_EXPERTS, -1, next_used)
    per_block = lambda table: jnp.sum(
        jnp.where(block_expert[:, None] == experts[None, :], table[None, :], 0), axis=1).astype(I32)
    next_expert = per_block(next_used)
    rows_used = per_block(pad_start + counts) - blk * EXPERT_ROWS
    short_block = (rows_used <= EXPERT_ROWS // 2).astype(I32)

    td = _divisor_tile(t, 256, SUBLANES)
    xs, shared = _dispatch(x_packed, _tile_major(dest, td), (pad_start + counts).astype(I32),
                           (padded - counts).astype(I32), swg.astype(BF16), swu.astype(BF16),
                           swd.astype(BF16), n_blocks * EXPERT_ROWS, td)
    ys = _experts(xs, wg, wu, wd, block_expert, next_expert, short_block, n_valid.reshape(1))
    tc = _divisor_tile(math.gcd(n_prompt_rows, t - n_prompt_rows), 128, SUBLANES)
    return _combine(x, shared, gate.T, _tile_major(dest, tc), ys, g, b, tc, n_prompt_rows)


def kernel(x_prompt, x_sample, mem_prompt, state_gla, cache_swa_k, cache_swa_v, cache_mem_k, cache_mem_v,
           w_in, gla_w_gate2, gla_b_gate, gla_norm_g, swa_sinks, w_mix_out, ln1_g, ln1_b,
           mem_wq, mem_wk, mem_wv, mem_wo, ln2_g, ln2_b,
           router_w, router_bias, exp_w_gate, exp_w_up, exp_w_down, sh_w_gate, sh_w_up, sh_w_down,
           ln3_g, ln3_b):
    nbp, seq, d = x_prompt.shape
    nbs, dec, _ = x_sample.shape
    tp, ts = nbp * seq, nbs * dec
    t = tp + ts
    mem_tokens = mem_prompt.shape[1]
    assert state_gla.shape[0] == DEPTH and seq % SWA_WINDOW == 0 and dec % SUBLANES == 0
    assert cache_swa_k.shape[2] == SWA_WINDOW and tp % dec == 0
    tm = _divisor_tile(math.gcd(tp, ts), 512, 16)
    l = 0

    xp = x_prompt.reshape(tp, d)
    xs = x_sample.reshape(ts, d)

    gate_src = GLA_QK + GLA_QK + GLA_VW + GLA_VW
    swa_rows = SWA_QW + 2 * SWA_KVW
    w_in_rt = _reorder_cast(
        w_in[l].T,
        [(0, 0, gate_src),
         (gate_src + GLA_GATE_RANK, COL_SQ, swa_rows),
         (gate_src, COL_GG, GLA_GATE_RANK)],
        IN_COLS_PADDED, _divisor_tile(IN_COLS_PADDED, 512, 16))
    z = _matmul_split(xp, xs, w_in_rt, tm, IN_COLS_PADDED // 2, F32)

    wa2_pad = jnp.concatenate(
        [gla_w_gate2[l], jnp.zeros((LANES - GLA_GATE_RANK, GLA_QK), F32)], axis=0)
    wa2_hi = wa2_pad.astype(BF16)
    wa2_lo = (wa2_pad - wa2_hi.astype(F32)).astype(BF16)
    wa2_stack = jnp.concatenate([wa2_hi, wa2_lo, wa2_hi], axis=0)
    zero_state = jnp.zeros((nbp, GLA_HEADS, GLA_DK, GLA_DV), F32)
    gla_rows = _divisor_tile(seq, GLA_STEP_ROWS, CHUNK)
    mix, st_p = _gla(z, wa2_stack, gla_b_gate[l], gla_norm_g[l], zero_state, None,
                     nb=nbp, n_chunks=seq // gla_rows, rows=gla_rows, chunk=CHUNK, row0=0, chained=True)
    seqs = _divisor_tile(nbs, GLA_SAMPLE_SEQS, 1)
    assert tp % (seqs * dec) == 0
    mix, st_s = _gla(z, wa2_stack, gla_b_gate[l], gla_norm_g[l], state_gla[l], mix,
                     nb=nbs // seqs, n_chunks=1, rows=seqs * dec, chunk=dec, row0=tp, chained=False)
    nblk = seq // SWA_WINDOW
    prev_p = lambda col: (lambda b, i: (b * nblk + jnp.maximum(i - 1, 0), col))
    mix = _swa(z, swa_sinks[l], z, z, (prev_p(COL_SK // SWA_KVW), prev_p(COL_SV // SWA_KVW)), mix,
               nb=nbp, n_blocks=nblk, rows=SWA_WINDOW, row0=0, q_base=0, q_stride=SWA_WINDOW)
    ck = cache_swa_k[l].reshape(nbs * SWA_WINDOW, SWA_KVW)
    cv = cache_swa_v[l].reshape(nbs * SWA_WINDOW, SWA_KVW)
    prev_s = lambda b, i: (b, 0)
    mix = _swa(z, swa_sinks[l], ck, cv, (prev_s, prev_s), mix,
               nb=nbs, n_blocks=1, rows=dec, row0=tp, q_base=PAST_LEN, q_stride=0)
    h1 = _matmul_res_ln(mix, w_mix_out[l].astype(BF16), (xp, xs), ln1_g[l], ln1_b[l], tm)

    mem = mem_prompt.reshape(nbp * mem_tokens, d)
    tmem = _divisor_tile(nbp * mem_tokens, 512, 16)
    mk = _matmul(mem, mem_wk[l], tmem, _divisor_tile(d, 512, LANES), F32)
    mv = _matmul(mem, mem_wv[l], tmem, _divisor_tile(d, 512, LANES), F32)
    q = _matmul(h1, mem_wq[l].astype(BF16), tm, d, BF16)
    tq = _divisor_tile(seq, 512, 16)
    att = _mem_attn(q, mk, mv, None, nb=nbp, n_tiles=seq // tq, rows=tq, row0=0, mem_tokens=mem_tokens)
    mem_seqs = _divisor_tile(nbs, MEM_SAMPLE_SEQS, 1)
    assert tp % (mem_seqs * dec) == 0
    att = _mem_attn(q, cache_mem_k[l].reshape(nbs * mem_tokens, d), cache_mem_v[l].reshape(nbs * mem_tokens, d),
                    att, nb=nbs // mem_seqs, n_tiles=1, rows=mem_seqs * dec, row0=tp, mem_tokens=mem_tokens,
                    n_seq=mem_seqs)
    h2, h2_packed = _matmul_res_ln(att, mem_wo[l].astype(BF16), h1, ln2_g[l], ln2_b[l], tm, emit_packed=True)

    out_p, out_s = _moe_ln(h2, h2_packed, router_w[l], router_bias[l], exp_w_gate[l], exp_w_up[l],
                           exp_w_down[l], sh_w_gate[l], sh_w_up[l], sh_w_down[l], ln3_g[l], ln3_b[l], tp)

    y_prompt = out_p.reshape(nbp, seq, d)
    y_sample = out_s.reshape(nbs, dec, d)
    kv = z[:, COL_SK:COL_SK + 2 * SWA_KVW]
    kvp = kv[:tp].reshape(nbp, seq, 2 * SWA_KVW)[:, seq - SWA_WINDOW:]
    kvs = kv[tp:].reshape(nbs, dec, 2 * SWA_KVW)
    kv_shape = lambda a: a.reshape(a.shape[0], a.shape[1], SWA_KV_HEADS, SWA_HEAD_DIM)[None]
    mem_shape = lambda a: a.reshape(nbp, mem_tokens, MEM_HEADS, d // MEM_HEADS)[None]
    return (y_prompt, y_sample,
            st_p[None],
            kv_shape(kvp[..., :SWA_KVW]), kv_shape(kvp[..., SWA_KVW:]),
            mem_shape(mk), mem_shape(mv),
            st_s[None],
            kv_shape(kvs[..., :SWA_KVW]), kv_shape(kvs[..., SWA_KVW:]))
```

```python
import functools
import math

import jax
import jax.numpy as jnp
from jax import lax
from jax.experimental import pallas as pl
from jax.experimental.pallas import tpu as pltpu

F32 = jnp.float32
BF16 = jnp.bfloat16
I32 = jnp.int32

CHUNK = 64
PAST_LEN = 2048
GLA_HEADS = 8
GLA_DK = 64
GLA_DV = 128
GLA_QK = GLA_HEADS * GLA_DK
GLA_VW = GLA_HEADS * GLA_DV
GLA_GATE_RANK = 16
GLA_TAU = 16.0
SWA_HEADS = 16
SWA_KV_HEADS = 2
SWA_GROUP = SWA_HEADS // SWA_KV_HEADS
SWA_HEAD_DIM = 64
SWA_QW = SWA_HEADS * SWA_HEAD_DIM
SWA_KVW = SWA_KV_HEADS * SWA_HEAD_DIM
SWA_WINDOW = 128
WINDOW_CHUNKS = SWA_WINDOW // CHUNK
MEM_HEADS = 4
N_EXPERTS = 64
TOP_K = 8
N_GROUPS = 8
GROUP_SIZE = N_EXPERTS // N_GROUPS
TOPK_GROUPS = 4
ROUTED_SCALE = 2.5
DEPTH = 1
DEEPNORM_ALPHA = (2 * DEPTH) ** 0.25
LN_EPS = 1e-5
RMS_EPS = 1e-6

LANES = 128
SUBLANES = 8
VMEM_LIMIT_BYTES = 56 * 1024 * 1024

COL_GQ = 0
COL_GK = COL_GQ + GLA_QK
COL_GV = COL_GK + GLA_QK
COL_GR = COL_GV + GLA_VW
COL_SQ = COL_GR + GLA_VW
COL_SK = COL_SQ + SWA_QW
COL_SV = COL_SK + SWA_KVW
COL_GG = COL_SV + SWA_KVW
MXU_COLS = 256
IN_COLS_PADDED = -(-(COL_GG + LANES) // (2 * MXU_COLS)) * (2 * MXU_COLS)

EXPERT_ROWS = 256
EXPERT_X_SLOTS = 3
COMBINE_GROUP = SUBLANES
LN_SUB_ROWS = 128
GLA_STEP_ROWS = 512
GLA_SAMPLE_SEQS = 8
MEM_SAMPLE_SEQS = 2
COPY_SLOTS = 3

NT_DIMS = (((1,), (1,)), ((), ()))
TN_DIMS = (((0,), (0,)), ((), ()))


def _params(n_axes):
    return pltpu.CompilerParams(dimension_semantics=("arbitrary",) * n_axes,
                                vmem_limit_bytes=VMEM_LIMIT_BYTES)


def _sigmoid(x):
    return 1.0 / (1.0 + jnp.exp(-x))


def _silu(x):
    return x * _sigmoid(x)


def _layer_norm(x, g, b):
    mu = jnp.mean(x, axis=-1, keepdims=True)
    xc = x - mu
    var = jnp.mean(xc * xc, axis=-1, keepdims=True)
    return xc * lax.rsqrt(var + LN_EPS) * g + b


HI_MASK = -65536


def _round_to_bf16_bits(x):
    b = lax.bitcast_convert_type(x, I32)
    return b + (0x7FFF + (lax.shift_right_logical(b, 16) & 1))


def _pack_halves(x):
    c = x.shape[1] // 2
    lo = lax.shift_right_logical(_round_to_bf16_bits(x[:, :c]), 16)
    hi = _round_to_bf16_bits(x[:, c:]) & HI_MASK
    return lo | hi


def _unpack_halves(w):
    return (lax.bitcast_convert_type(lax.shift_left(w, 16), F32),
            lax.bitcast_convert_type(w & HI_MASK, F32))


def _divisor_tile(n, pref, mult):
    t = min(pref, n)
    while t > mult and (n % t or t % mult):
        t -= mult
    assert n % t == 0 and t % mult == 0, (n, pref, mult)
    return t


def _mm_kernel(x_ref, w_ref, o_ref, xb_ref):
    @pl.when(pl.program_id(1) == 0)
    def _():
        xb_ref[...] = x_ref[...].astype(BF16)

    o_ref[...] = jnp.dot(xb_ref[...], w_ref[...].astype(BF16), preferred_element_type=F32).astype(o_ref.dtype)


def _matmul(x, w, tm, tn, out_dtype):
    m, k = x.shape
    n = w.shape[1]
    return pl.pallas_call(
        _mm_kernel,
        grid=(m // tm, n // tn),
        in_specs=[pl.BlockSpec((tm, k), lambda i, j: (i, 0)),
                  pl.BlockSpec((k, tn), lambda i, j: (0, j))],
        out_specs=pl.BlockSpec((tm, tn), lambda i, j: (i, j)),
        out_shape=jax.ShapeDtypeStruct((m, n), out_dtype),
        scratch_shapes=[pltpu.VMEM((tm, k), BF16)],
        compiler_params=_params(2),
        name="matmul",
    )(x, w)


def _split_rows_specs(tm, k, n_first_tiles, n_grid_axes):
    if n_grid_axes == 1:
        return [pl.BlockSpec((tm, k), lambda i: (jnp.minimum(i, n_first_tiles - 1), 0)),
                pl.BlockSpec((tm, k), lambda i: (jnp.maximum(i - n_first_tiles, 0), 0))]
    return [pl.BlockSpec((tm, k), lambda j, i: (jnp.minimum(i, n_first_tiles - 1), 0)),
            pl.BlockSpec((tm, k), lambda j, i: (jnp.maximum(i - n_first_tiles, 0), 0))]


def _mm_split_kernel(xa_ref, xb_ref, wt_ref, o_ref, *, n_first_tiles):
    i = pl.program_id(1)

    def product(x_ref):
        o_ref[...] = lax.dot_general(x_ref[...].astype(BF16), wt_ref[...], NT_DIMS,
                                     preferred_element_type=F32).astype(o_ref.dtype)

    @pl.when(i < n_first_tiles)
    def _():
        product(xa_ref)

    @pl.when(i >= n_first_tiles)
    def _():
        product(xb_ref)


def _matmul_split(xa, xb, w_t, tm, tn, out_dtype):
    ma, k = xa.shape
    m = ma + xb.shape[0]
    n = w_t.shape[0]
    return pl.pallas_call(
        functools.partial(_mm_split_kernel, n_first_tiles=ma // tm),
        grid=(n // tn, m // tm),
        in_specs=_split_rows_specs(tm, k, ma // tm, 2) + [pl.BlockSpec((tn, k), lambda j, i: (j, 0))],
        out_specs=pl.BlockSpec((tm, tn), lambda j, i: (i, j)),
        out_shape=jax.ShapeDtypeStruct((m, n), out_dtype),
        compiler_params=_params(2),
        name="matmul_split",
    )(xa, xb, w_t)


def _reorder_cast_kernel(src_hbm, dst_hbm, stage_ref, out_ref, in_sems, out_sems, *, chunk_pieces, chunk):
    n = len(chunk_pieces)
    slots = stage_ref.shape[0]

    def in_copies(c):
        return [pltpu.make_async_copy(src_hbm.at[pl.ds(s0, ln)], stage_ref.at[c % slots, pl.ds(o0, ln)],
                                      in_sems.at[c % slots]) for s0, o0, ln in chunk_pieces[c]]

    def out_copy(c):
        return pltpu.make_async_copy(out_ref.at[c % slots], dst_hbm.at[pl.ds(c * chunk, chunk)],
                                     out_sems.at[c % slots])

    for c in range(min(slots - 1, n)):
        for cp in in_copies(c):
            cp.start()
    for c in range(n):
        if c + slots - 1 < n:
            for cp in in_copies(c + slots - 1):
                cp.start()
        for cp in in_copies(c):
            cp.wait()
        if c >= slots:
            out_copy(c - slots).wait()
        covered = sum(ln for _, _, ln in chunk_pieces[c])
        if covered < chunk:
            stage_ref[c % slots, covered:, :] = jnp.zeros((chunk - covered, stage_ref.shape[-1]),
                                                          stage_ref.dtype)
        out_ref[c % slots] = stage_ref[c % slots].astype(out_ref.dtype)
        out_copy(c).start()
    for c in range(max(n - slots, 0), n):
        out_copy(c).wait()


def _reorder_cast(src, runs, n_out_rows, chunk):
    k = src.shape[1]
    chunk_pieces = []
    for c0 in range(0, n_out_rows, chunk):
        pieces = []
        for s0, d0, ln in runs:
            lo, hi = max(d0, c0), min(d0 + ln, c0 + chunk)
            if lo < hi:
                assert (lo - c0) % 16 == 0 and (s0 + lo - d0) % SUBLANES == 0 and (hi - lo) % 16 == 0
                pieces.append((s0 + lo - d0, lo - c0, hi - lo))
        chunk_pieces.append(pieces)
    hbm = pl.BlockSpec(memory_space=pl.ANY)
    return pl.pallas_call(
        functools.partial(_reorder_cast_kernel, chunk_pieces=chunk_pieces, chunk=chunk),
        in_specs=[hbm],
        out_specs=hbm,
        out_shape=jax.ShapeDtypeStruct((n_out_rows, k), BF16),
        scratch_shapes=[pltpu.VMEM((COPY_SLOTS, chunk, k), src.dtype), pltpu.VMEM((COPY_SLOTS, chunk, k), BF16),
                        pltpu.SemaphoreType.DMA((COPY_SLOTS,)), pltpu.SemaphoreType.DMA((COPY_SLOTS,))],
        compiler_params=pltpu.CompilerParams(vmem_limit_bytes=VMEM_LIMIT_BYTES),
        name="reorder_cast",
    )(src)


def _mm_res_ln_kernel(x_ref, w_ref, *refs, n_res, n_first_tiles):
    res_refs, (g_ref, b_ref, o_ref, *packed_ref) = refs[:n_res], refs[n_res:]
    tm = x_ref.shape[0]
    sub = min(tm, LN_SUB_ROWS)
    use_first = pl.program_id(0) < n_first_tiles
    for r in range(0, tm, sub):
        rows = slice(r, r + sub)
        y = jnp.dot(x_ref[rows, :], w_ref[...], preferred_element_type=F32)
        res = res_refs[0][rows, :]
        if n_res == 2:
            res = jnp.where(use_first, res, res_refs[1][rows, :])
        h = _layer_norm(DEEPNORM_ALPHA * res + y, g_ref[...], b_ref[...])
        o_ref[rows, :] = h
        if packed_ref:
            groups = slice(r // SUBLANES, (r + sub) // SUBLANES)
            packed_ref[0][groups] = _pack_halves(h).reshape(sub // SUBLANES, SUBLANES, h.shape[1] // 2)


def _matmul_res_ln(x, w, res, g, b, tm, emit_packed=False):
    m, k = x.shape
    n = w.shape[1]
    if isinstance(res, tuple):
        n_first_tiles = res[0].shape[0] // tm
        res_specs = _split_rows_specs(tm, n, n_first_tiles, 1)
    else:
        n_first_tiles = 0
        res = (res,)
        res_specs = [pl.BlockSpec((tm, n), lambda i: (i, 0))]
    out_specs = [pl.BlockSpec((tm, n), lambda i: (i, 0))]
    out_shape = [jax.ShapeDtypeStruct((m, n), F32)]
    if emit_packed:
        out_specs.append(pl.BlockSpec((tm // SUBLANES, SUBLANES, n // 2), lambda i: (i, 0, 0)))
        out_shape.append(jax.ShapeDtypeStruct((m // SUBLANES, SUBLANES, n // 2), I32))
    outs = pl.pallas_call(
        functools.partial(_mm_res_ln_kernel, n_res=len(res), n_first_tiles=n_first_tiles),
        grid=(m // tm,),
        in_specs=[pl.BlockSpec((tm, k), lambda i: (i, 0)),
                  pl.BlockSpec((k, n), lambda i: (0, 0))] + res_specs + [
                  pl.BlockSpec((1, n), lambda i: (0, 0)),
                  pl.BlockSpec((1, n), lambda i: (0, 0))],
        out_specs=out_specs,
        out_shape=out_shape,
        compiler_params=_params(1),
        name="matmul_res_ln",
    )(x, w, *res, g.reshape(1, n), b.reshape(1, n))
    return outs if emit_packed else outs[0]


def _split3_bf16(x):
    hi = x.astype(BF16)
    r1 = x - hi.astype(F32)
    mid = r1.astype(BF16)
    lo = (r1 - mid.astype(F32)).astype(BF16)
    return jnp.concatenate([hi, mid, lo], axis=-1)


def _gla_kernel(q_ref, k_ref, v_ref, r_ref, gg_ref, wa2_ref, ba_ref, ng_ref, s0_ref, mix_in_ref,
                o_ref, sfin_ref, st_ref, *, rows, chunk, chained):
    del mix_in_ref
    n_sub = rows // chunk
    shift = chunk.bit_length() - 1
    assert chunk == 1 << shift

    if chained:
        @pl.when(pl.program_id(1) == 0)
        def _():
            for h in range(GLA_HEADS):
                st_ref[h] = s0_ref[0, h].T

    gg = gg_ref[...]
    gg_hi = gg.astype(BF16)
    gg_lo = (gg - gg_hi.astype(F32)).astype(BF16)
    gate = jnp.dot(jnp.concatenate([gg_hi, gg_hi, gg_lo], axis=-1), wa2_ref[...],
                   preferred_element_type=F32) + ba_ref[...]
    log_a = (jnp.minimum(gate, 0.0) - jnp.log1p(jnp.exp(-jnp.abs(gate)))) / GLA_TAU

    ri_c = lax.broadcasted_iota(I32, (chunk, chunk), 0)
    ci_c = lax.broadcasted_iota(I32, (chunk, chunk), 1)
    causal_c = ri_c >= ci_c
    ones = jnp.concatenate([jnp.where(causal_c, 1.0, 0.0), jnp.ones((chunk, chunk), F32)], axis=0).astype(BF16)
    sums = jnp.einsum("cij,cjd->cid", jnp.broadcast_to(ones[None], (n_sub, 2 * chunk, chunk)),
                      _split3_bf16(log_a).reshape(n_sub, chunk, 3 * GLA_QK),
                      preferred_element_type=F32)
    sums = sums[..., :GLA_QK] + sums[..., GLA_QK:2 * GLA_QK] + sums[..., 2 * GLA_QK:]
    bcum = sums[:, :chunk].reshape(rows, GLA_QK)
    b_last = sums[:, chunk:].reshape(rows, GLA_QK)
    q = q_ref[...] * (GLA_DK ** -0.5)
    k = k_ref[...]
    q_in = (q * jnp.exp(bcum)).astype(BF16)
    k_in = (k * jnp.exp(-bcum)).astype(BF16)
    k_out = (k * jnp.exp(b_last - bcum)).astype(BF16)
    decay = jnp.exp(b_last)
    ng = ng_ref[...]
    def heads(x, width):
        return jnp.stack([x[:, h * width:(h + 1) * width] for h in range(GLA_HEADS)], axis=0)

    def head_chunks(x, width):
        return heads(x, width).reshape(GLA_HEADS * n_sub, chunk, width)

    q4 = head_chunks(q_in, GLA_DK)
    v4 = head_chunks(v_ref[...].astype(BF16), GLA_DV)
    attn = jnp.einsum("bqd,bkd->bqk", q4, head_chunks(k_in, GLA_DK), preferred_element_type=F32)
    attn = jnp.where(causal_c[None], attn, 0.0).astype(BF16)
    o = jnp.einsum("bqk,bke->bqe", attn, v4, preferred_element_type=F32)
    upd_t = jnp.einsum("bre,brc->bec", v4, head_chunks(k_out, GLA_DK), preferred_element_type=F32)
    upd_t = upd_t.reshape(GLA_HEADS, n_sub, GLA_DV, GLA_DK)
    states = []
    if chained:
        st = st_ref[...]
    for c in range(n_sub):
        if not chained:
            st = jnp.stack([s0_ref[c, h].T for h in range(GLA_HEADS)], axis=0)
        states.append(st.astype(BF16))
        st = st * heads(decay[c * chunk:c * chunk + 1, :], GLA_DK) + upd_t[:, c]
        if not chained:
            for h in range(GLA_HEADS):
                sfin_ref[c, h] = st[h].T
    if chained:
        st_ref[...] = st
    entering = jnp.stack(states, axis=1).reshape(GLA_HEADS * n_sub, GLA_DV, GLA_DK)
    o = o + jnp.einsum("bqd,bsd->bqs", q4, entering, preferred_element_type=F32)
    o = o.reshape(GLA_HEADS, rows, GLA_DV)
    o = o * lax.rsqrt(jnp.mean(o * o, axis=-1, keepdims=True) + RMS_EPS) * ng[None]
    out = (o * _silu(heads(r_ref[...], GLA_DV))).astype(o_ref.dtype)
    for h in range(GLA_HEADS):
        o_ref[:, h * GLA_DV:(h + 1) * GLA_DV] = out[h]
    if chained:
        @pl.when(pl.program_id(1) == pl.num_programs(1) - 1)
        def _():
            for h in range(GLA_HEADS):
                sfin_ref[0, h] = st_ref[h].T


def _gla(z, wa2_stack, ba, norm_g, s0, mix_in, *, nb, n_chunks, rows, chunk, row0, chained):
    t = z.shape[0]
    rb0 = row0 // rows
    seq_per_step = 1 if chained else rows // chunk
    assert chained or n_chunks == 1
    rmap = lambda colblk: (lambda b, c: (rb0 + b * n_chunks + c, colblk))
    in_specs = [
        pl.BlockSpec((rows, GLA_QK), rmap(COL_GQ // GLA_QK)),
        pl.BlockSpec((rows, GLA_QK), rmap(COL_GK // GLA_QK)),
        pl.BlockSpec((rows, GLA_VW), rmap(COL_GV // GLA_VW)),
        pl.BlockSpec((rows, GLA_VW), rmap(COL_GR // GLA_VW)),
        pl.BlockSpec((rows, LANES), rmap(COL_GG // LANES)),
        pl.BlockSpec((3 * LANES, GLA_QK), lambda b, c: (0, 0)),
        pl.BlockSpec((1, GLA_QK), lambda b, c: (0, 0)),
        pl.BlockSpec((1, GLA_DV), lambda b, c: (0, 0)),
        pl.BlockSpec((seq_per_step, GLA_HEADS, GLA_DK, GLA_DV), lambda b, c: (b, 0, 0, 0)),
    ]
    args = [z, z, z, z, z, wa2_stack, ba.reshape(1, GLA_QK), norm_g.reshape(1, GLA_DV), s0]
    aliases = {}
    if mix_in is None:
        mix_in = jnp.zeros((SUBLANES, LANES), BF16)
        in_specs.append(pl.BlockSpec(memory_space=pl.ANY))
    else:
        in_specs.append(pl.BlockSpec(memory_space=pl.ANY))
        aliases = {len(args): 0}
    args.append(mix_in)
    mix, s_fin = pl.pallas_call(
        functools.partial(_gla_kernel, rows=rows, chunk=chunk, chained=chained),
        grid=(nb, n_chunks),
        in_specs=in_specs,
        out_specs=[pl.BlockSpec((rows, GLA_VW), rmap(0)),
                   pl.BlockSpec((seq_per_step, GLA_HEADS, GLA_DK, GLA_DV), lambda b, c: (b, 0, 0, 0))],
        out_shape=[jax.ShapeDtypeStruct((t, GLA_VW + SWA_QW), BF16),
                   jax.ShapeDtypeStruct((nb * seq_per_step, GLA_HEADS, GLA_DK, GLA_DV), F32)],
        scratch_shapes=[pltpu.VMEM((GLA_HEADS, GLA_DV, GLA_DK), F32)],
        input_output_aliases=aliases,
        compiler_params=_params(2),
        name="gla",
    )(*args)
    return mix, s_fin


def _swa_kernel(sinks_ref, q_ref, kc_ref, vc_ref, kp_ref, vp_ref, mix_in_ref, o_ref, bias_ref, *,
                rows, q_base, q_stride):
    del mix_in_ref
    i = pl.program_id(1)
    n_tables = bias_ref.shape[0]
    nk = SWA_WINDOW + rows
    sr = SWA_GROUP * rows
    assert rows & (rows - 1) == 0 and q_base % CHUNK == 0 and q_stride % CHUNK == 0
    assert n_tables == 1 or q_base + q_stride >= SWA_WINDOW
    head_in_group = lax.shift_right_logical(lax.broadcasted_iota(I32, (sr, 1), 0), rows.bit_length() - 1)

    @pl.when(jnp.logical_and(pl.program_id(0) == 0, i == 0))
    def _():
        row = lax.broadcasted_iota(I32, (sr, nk), 0)
        col = lax.broadcasted_iota(I32, (sr, nk), 1)
        for t in range(n_tables):
            q0 = q_base + t * q_stride
            qpos = q0 + (row & (rows - 1))
            kpos = q0 - SWA_WINDOW + col
            dist = jnp.abs(qpos - kpos).astype(F32)
            qc = lax.shift_right_arithmetic(qpos, CHUNK.bit_length() - 1)
            kc = lax.shift_right_arithmetic(kpos, CHUNK.bit_length() - 1)
            allowed = (kpos >= 0) & (kc <= qc) & (kc >= qc - WINDOW_CHUNKS)
            for g in range(SWA_KV_HEADS):
                slope = jnp.zeros((sr, 1), F32)
                for j in range(SWA_GROUP):
                    h = g * SWA_GROUP + j
                    slope = jnp.where(head_in_group == j, 2.0 ** (-8.0 * (h + 1) / SWA_HEADS), slope)
                bias_ref[t, g] = jnp.where(allowed, -(slope * dist), -jnp.inf)

    table = jnp.minimum(i, n_tables - 1)
    keys = jnp.concatenate([kp_ref[...], kc_ref[...]], axis=0).astype(BF16)
    vals = jnp.concatenate([vp_ref[...], vc_ref[...]], axis=0).astype(BF16)
    for g in range(SWA_KV_HEADS):
        sink = jnp.zeros((sr, 1), F32)
        for j in range(SWA_GROUP):
            sink = jnp.where(head_in_group == j, sinks_ref[g * SWA_GROUP + j], sink)
        kv = slice(g * SWA_HEAD_DIM, (g + 1) * SWA_HEAD_DIM)
        qg = jnp.concatenate(
            [q_ref[:, (g * SWA_GROUP + j) * SWA_HEAD_DIM:(g * SWA_GROUP + j + 1) * SWA_HEAD_DIM]
             for j in range(SWA_GROUP)], axis=0).astype(BF16)
        s = lax.dot_general(qg, keys[:, kv], NT_DIMS, preferred_element_type=F32)
        s = s * (SWA_HEAD_DIM ** -0.5) + bias_ref[table, g]
        m = jnp.maximum(jnp.max(s, axis=-1, keepdims=True), sink)
        p = jnp.exp(s - m)
        denom = jnp.sum(p, axis=-1, keepdims=True) + jnp.exp(sink - m)
        o = jnp.dot(p.astype(BF16), vals[:, kv], preferred_element_type=F32) / denom
        for j in range(0, SWA_GROUP, 2):
            c0 = (g * SWA_GROUP + j) * SWA_HEAD_DIM
            o_ref[:, c0:c0 + LANES] = jnp.concatenate(
                [o[j * rows:(j + 1) * rows], o[(j + 1) * rows:(j + 2) * rows]], axis=-1).astype(o_ref.dtype)


def _swa(z, sinks, k_prev, v_prev, prev_map, mix_in, *, nb, n_blocks, rows, row0, q_base, q_stride):
    rb0 = row0 // rows
    rmap = lambda colblk: (lambda b, i: (rb0 + b * n_blocks + i, colblk))
    return pl.pallas_call(
        functools.partial(_swa_kernel, rows=rows, q_base=q_base, q_stride=q_stride),
        grid=(nb, n_blocks),
        in_specs=[pl.BlockSpec(memory_space=pltpu.SMEM),
                  pl.BlockSpec((rows, SWA_QW), rmap(COL_SQ // SWA_QW)),
                  pl.BlockSpec((rows, SWA_KVW), rmap(COL_SK // SWA_KVW)),
                  pl.BlockSpec((rows, SWA_KVW), rmap(COL_SV // SWA_KVW)),
                  pl.BlockSpec((SWA_WINDOW, SWA_KVW), prev_map[0]),
                  pl.BlockSpec((SWA_WINDOW, SWA_KVW), prev_map[1]),
                  pl.BlockSpec(memory_space=pl.ANY)],
        out_specs=pl.BlockSpec((rows, SWA_QW), rmap(GLA_VW // SWA_QW)),
        out_shape=jax.ShapeDtypeStruct(mix_in.shape, mix_in.dtype),
        scratch_shapes=[pltpu.VMEM((min(2, n_blocks), SWA_KV_HEADS, SWA_GROUP * rows, SWA_WINDOW + rows), F32)],
        input_output_aliases={6: 0},
        compiler_params=_params(2),
        name="swa",
    )(sinks, z, z, z, k_prev, v_prev, mix_in)


def _mem_attn_kernel(q_ref, k_ref, v_ref, o_in_ref, o_ref, *, head_dim, n_seq):
    del o_in_ref
    rq = q_ref.shape[0] // n_seq
    rm = k_ref.shape[0] // n_seq
    for s_i in range(n_seq):
        qr = slice(s_i * rq, (s_i + 1) * rq)
        mr = slice(s_i * rm, (s_i + 1) * rm)
        heads = [slice(h * head_dim, (h + 1) * head_dim) for h in range(MEM_HEADS)]
        q3 = jnp.stack([q_ref[qr, hs] for hs in heads], axis=0)
        k3 = jnp.stack([k_ref[mr, hs] for hs in heads], axis=0).astype(BF16)
        v3 = jnp.stack([v_ref[mr, hs] for hs in heads], axis=0).astype(BF16)
        s = jnp.einsum("hqd,hkd->hqk", q3, k3, preferred_element_type=F32) * (head_dim ** -0.5)
        m = jnp.max(s, axis=-1, keepdims=True)
        p = jnp.exp(s - m)
        denom = jnp.sum(p, axis=-1, keepdims=True)
        o = jnp.einsum("hqk,hkd->hqd", p.astype(BF16), v3, preferred_element_type=F32) / denom
        for h, hs in enumerate(heads):
            o_ref[qr, hs] = o[h].astype(o_ref.dtype)


def _mem_attn(q, mk, mv, o_in, *, nb, n_tiles, rows, row0, mem_tokens, n_seq=1):
    t, d = q.shape
    assert n_seq == 1 or n_tiles == 1
    rb0 = row0 // rows
    qmap = lambda b, i: (rb0 + b * n_tiles + i, 0)
    in_specs = [pl.BlockSpec((rows, d), qmap),
                pl.BlockSpec((n_seq * mem_tokens, d), lambda b, i: (b, 0)),
                pl.BlockSpec((n_seq * mem_tokens, d), lambda b, i: (b, 0)),
                pl.BlockSpec(memory_space=pl.ANY)]
    aliases = {}
    if o_in is None:
        o_in = jnp.zeros((SUBLANES, LANES), BF16)
    else:
        aliases = {3: 0}
    return pl.pallas_call(
        functools.partial(_mem_attn_kernel, head_dim=d // MEM_HEADS, n_seq=n_seq),
        grid=(nb, n_tiles),
        in_specs=in_specs,
        out_specs=pl.BlockSpec((rows, d), qmap),
        out_shape=jax.ShapeDtypeStruct((t, d), BF16),
        input_output_aliases=aliases,
        compiler_params=_params(2),
        name="mem_attn",
    )(q, mk, mv, o_in)


def _pick_first_max(cur, iota, sentinel):
    mx = jnp.max(cur, axis=0, keepdims=True)
    first = jnp.min(jnp.where(cur == mx, iota, sentinel), axis=0, keepdims=True)
    return iota == first, first


def _router_kernel(x_ref, wh_ref, wl_ref, bias_ref, idx_ref, gate_ref, rank_ref, cnt_ref, run_ref, before_ref,
                   *, tm):
    i = pl.program_id(0)

    @pl.when(i == 0)
    def _():
        run_ref[...] = jnp.zeros_like(run_ref)
        ti = lax.broadcasted_iota(I32, (tm, tm), 0)
        tj = lax.broadcasted_iota(I32, (tm, tm), 1)
        before_ref[...] = jnp.where(ti < tj, 1.0, 0.0).astype(BF16)

    x = x_ref[...]
    xh = x.astype(BF16)
    xl = (x - xh.astype(F32)).astype(BF16)
    wh = wh_ref[...]
    logits = (lax.dot_general(wh, xh, NT_DIMS, preferred_element_type=F32)
              + lax.dot_general(wh, xl, NT_DIMS, preferred_element_type=F32)
              + lax.dot_general(wl_ref[...], xh, NT_DIMS, preferred_element_type=F32))
    scores = _sigmoid(logits)
    sel = scores + bias_ref[...]
    neg_inf = -jnp.inf

    li = lax.broadcasted_iota(I32, (GROUP_SIZE, tm), 0)
    grp_rows = []
    for g in range(N_GROUPS):
        blk = sel[g * GROUP_SIZE:(g + 1) * GROUP_SIZE, :]
        pick, _ = _pick_first_max(blk, li, GROUP_SIZE)
        m1 = jnp.max(blk, axis=0, keepdims=True)
        m2 = jnp.max(jnp.where(pick, neg_inf, blk), axis=0, keepdims=True)
        grp_rows.append(m1 + m2)
    grp = jnp.concatenate(grp_rows, axis=0)

    gi = lax.broadcasted_iota(I32, (N_GROUPS, tm), 0)
    gsel = jnp.zeros((N_GROUPS, tm), F32)
    cur = grp
    for _ in range(TOPK_GROUPS):
        pick, _ = _pick_first_max(cur, gi, N_GROUPS)
        gsel = jnp.where(pick, 1.0, gsel)
        cur = jnp.where(pick, neg_inf, cur)
    emask = jnp.concatenate(
        [jnp.broadcast_to(gsel[g:g + 1, :], (GROUP_SIZE, tm)) for g in range(N_GROUPS)], axis=0)

    ei = lax.broadcasted_iota(I32, (N_EXPERTS, tm), 0)
    cur = jnp.where(emask > 0.5, sel, neg_inf)
    chosen = jnp.zeros((N_EXPERTS, tm), F32)
    idx_rows, w_rows = [], []
    for _ in range(TOP_K):
        pick, first = _pick_first_max(cur, ei, N_EXPERTS)
        idx_rows.append(first)
        w_rows.append(jnp.sum(jnp.where(pick, scores, 0.0), axis=0, keepdims=True))
        chosen = jnp.where(pick, 1.0, chosen)
        cur = jnp.where(pick, neg_inf, cur)
    idx = jnp.concatenate(idx_rows, axis=0)
    w = jnp.concatenate(w_rows, axis=0)
    gate_ref[...] = w / jnp.sum(w, axis=0, keepdims=True) * ROUTED_SCALE
    idx_ref[...] = idx

    local = jnp.dot(chosen.astype(BF16), before_ref[...], preferred_element_type=F32)
    total = local + run_ref[:, 0:1]
    rank_rows = [jnp.sum(jnp.where(ei == idx_rows[k], total, 0.0), axis=0, keepdims=True)
                 for k in range(TOP_K)]
    rank_ref[...] = jnp.concatenate(rank_rows, axis=0).astype(I32)
    run_ref[...] = run_ref[...] + jnp.sum(chosen, axis=1, keepdims=True)
    cnt_ref[...] = run_ref[...].astype(I32)


def _router(x, w_hi_t, w_lo_t, bias, tm):
    t, d = x.shape
    return pl.pallas_call(
        functools.partial(_router_kernel, tm=tm),
        grid=(t // tm,),
        in_specs=[pl.BlockSpec((tm, d), lambda i: (i, 0)),
                  pl.BlockSpec((N_EXPERTS, d), lambda i: (0, 0)),
                  pl.BlockSpec((N_EXPERTS, d), lambda i: (0, 0)),
                  pl.BlockSpec((N_EXPERTS, 1), lambda i: (0, 0))],
        out_specs=[pl.BlockSpec((TOP_K, tm), lambda i: (0, i)),
                   pl.BlockSpec((TOP_K, tm), lambda i: (0, i)),
                   pl.BlockSpec((TOP_K, tm), lambda i: (0, i)),
                   pl.BlockSpec((N_EXPERTS, LANES), lambda i: (0, 0))],
        out_shape=[jax.ShapeDtypeStruct((TOP_K, t), I32),
                   jax.ShapeDtypeStruct((TOP_K, t), F32),
                   jax.ShapeDtypeStruct((TOP_K, t), I32),
                   jax.ShapeDtypeStruct((N_EXPERTS, LANES), I32)],
        scratch_shapes=[pltpu.VMEM((N_EXPERTS, LANES), F32), pltpu.VMEM((tm, tm), BF16)],
        compiler_params=_params(1),
        name="router",
    )(x, w_hi_t, w_lo_t, bias.reshape(N_EXPERTS, 1))


PAD_BITS = tuple(1 << s for s in reversed(range(EXPERT_ROWS.bit_length() - 1)))


def _dispatch_kernel(zstart_ref, zcount_ref, dest_ref, x_ref, swg_ref, swu_ref, swd_ref, xs_ref, shared_ref,
                     zero_ref, sem, zsem, *, tm):
    i = pl.program_id(0)

    def start_group(g, carry):
        for j in range(SUBLANES):
            for k in range(TOP_K):
                d = dest_ref[k * tm + g * SUBLANES + j]
                pltpu.make_async_copy(x_ref.at[g, pl.ds(j, 1), :], xs_ref.at[d], sem).start(priority=k % 2)
        return carry

    lax.fori_loop(0, tm // SUBLANES, start_group, 0)

    half = x_ref.shape[-1]
    x_lo, x_hi = _unpack_halves(x_ref[...].reshape(tm, half))
    xb = jnp.concatenate([x_lo.astype(BF16), x_hi.astype(BF16)], axis=-1)
    hg = jnp.dot(xb, swg_ref[...], preferred_element_type=F32)
    hu = jnp.dot(xb, swu_ref[...], preferred_element_type=F32)
    shared_ref[...] = jnp.dot((_silu(hg) * hu).astype(BF16), swd_ref[...],
                              preferred_element_type=F32).astype(shared_ref.dtype)

    for k in range(TOP_K):
        pltpu.make_async_copy(xs_ref.at[pl.ds(0, tm)], xs_ref.at[pl.ds(0, tm)], sem).wait()

    @pl.when(i == pl.num_programs(0) - 1)
    def _():
        zero_ref[...] = jnp.zeros_like(zero_ref)

        def fill(e, wait):
            pos = zstart_ref[e]
            cnt = zcount_ref[e]
            for bit in PAD_BITS:
                has = (cnt & bit) != 0

                @pl.when(has)
                def _():
                    cp = pltpu.make_async_copy(zero_ref.at[pl.ds(0, bit)], xs_ref.at[pl.ds(pos, bit)], zsem)
                    if wait:
                        cp.wait()
                    else:
                        cp.start()

                pos = pos + jnp.where(has, bit, 0)

        def fill_start(e, carry):
            fill(e, False)
            return carry

        def fill_wait(e, carry):
            fill(e, True)
            return carry

        lax.fori_loop(0, N_EXPERTS, fill_start, 0)
        lax.fori_loop(0, N_EXPERTS, fill_wait, 0)


def _dispatch(x, dest_tiles, zstart, zcount, swg, swu, swd, n_rows, tm):
    t = x.shape[0] * SUBLANES
    d = x.shape[2]
    ds = swg.shape[1]
    const = lambda i, zs, zc: (0, 0)
    return pl.pallas_call(
        functools.partial(_dispatch_kernel, tm=tm),
        grid_spec=pltpu.PrefetchScalarGridSpec(
            num_scalar_prefetch=2,
            grid=(t // tm,),
            in_specs=[pl.BlockSpec((TOP_K * tm,), lambda i, zs, zc: (i,), memory_space=pltpu.SMEM),
                      pl.BlockSpec((tm // SUBLANES, SUBLANES, d), lambda i, zs, zc: (i, 0, 0)),
                      pl.BlockSpec((2 * d, ds), const),
                      pl.BlockSpec((2 * d, ds), const),
                      pl.BlockSpec((ds, 2 * d), const)],
            out_specs=[pl.BlockSpec(memory_space=pl.ANY),
                       pl.BlockSpec((tm, 2 * d), lambda i, zs, zc: (i, 0))],
            scratch_shapes=[pltpu.VMEM((EXPERT_ROWS // 2, 1, d), x.dtype),
                            pltpu.SemaphoreType.DMA(()),
                            pltpu.SemaphoreType.DMA(())]),
        out_shape=[jax.ShapeDtypeStruct((n_rows, 1, d), x.dtype),
                   jax.ShapeDtypeStruct((t, 2 * d), BF16)],
        compiler_params=_params(1),
        name="moe_dispatch",
    )(zstart, zcount, dest_tiles, x, swg, swu, swd)


def _expert_kernel(be_ref, nxt_ref, short_ref, nv_ref, xs_hbm, wg_hbm, wu_hbm, wd_hbm, ys_hbm,
                   wgs_ref, wus_ref, wds_ref, wgb_ref, wub_ref, wdb_ref, xbuf_ref, ybuf_ref,
                   wsems, xsems, ysems):
    b = pl.program_id(0)
    n_valid = nv_ref[0]
    e = be_ref[b]
    prev = be_ref[jnp.maximum(b - 1, 0)]
    fresh = jnp.logical_or(b == 0, e != prev)
    slot = lax.rem(b, 2)

    def weight_copies(expert):
        return (pltpu.make_async_copy(wg_hbm.at[expert], wgs_ref, wsems.at[0]),
                pltpu.make_async_copy(wu_hbm.at[expert], wus_ref, wsems.at[1]),
                pltpu.make_async_copy(wd_hbm.at[expert], wds_ref, wsems.at[2]))

    def rows_in(blk, to_slot):
        r0 = pl.multiple_of(blk * EXPERT_ROWS, EXPERT_ROWS)
        return pltpu.make_async_copy(xs_hbm.at[pl.ds(r0, EXPERT_ROWS), 0, :], xbuf_ref.at[to_slot],
                                     xsems.at[to_slot])

    def rows_out(blk, from_slot):
        r0 = pl.multiple_of(blk * EXPERT_ROWS, EXPERT_ROWS)
        return pltpu.make_async_copy(ybuf_ref.at[from_slot], ys_hbm.at[pl.ds(r0, EXPERT_ROWS), 0, :],
                                     ysems.at[from_slot])

    x_slots = xbuf_ref.shape[0]
    xslot = lax.rem(b, x_slots)

    @pl.when(b == 0)
    def _():
        for cp in weight_copies(e):
            cp.start(priority=1)
        for ahead in range(x_slots - 1):
            @pl.when(ahead < n_valid)
            def _():
                rows_in(ahead, ahead).start()

    @pl.when(fresh)
    def _():
        for cp in weight_copies(e):
            cp.wait()
        wgb_ref[...] = wgs_ref[...].astype(BF16)
        wub_ref[...] = wus_ref[...].astype(BF16)
        wdb_ref[...] = wds_ref[...].astype(BF16)
        nxt = nxt_ref[b]

        @pl.when(nxt >= 0)
        def _():
            for cp in weight_copies(nxt):
                cp.start(priority=1)

    @pl.when(b < n_valid)
    def _():
        rows_in(b, xslot).wait()

        @pl.when(b + x_slots - 1 < n_valid)
        def _():
            rows_in(b + x_slots - 1, lax.rem(b + x_slots - 1, x_slots)).start()

        @pl.when(b >= 2)
        def _():
            rows_out(b - 2, slot).wait()

        def swiglu_rows(rows):
            half = xbuf_ref.shape[-1]
            x_lo, x_hi = _unpack_halves(xbuf_ref[xslot, :rows, :])
            xb = jnp.concatenate([x_lo.astype(BF16), x_hi.astype(BF16)], axis=-1)
            hg = jnp.dot(xb, wgb_ref[...], preferred_element_type=F32)
            hu = jnp.dot(xb, wub_ref[...], preferred_element_type=F32)
            act = (_silu(hg) * hu).astype(BF16)
            ybuf_ref[slot, :rows, :] = _pack_halves(jnp.dot(act, wdb_ref[...], preferred_element_type=F32))
            if rows < EXPERT_ROWS:
                ybuf_ref[slot, rows:, :] = jnp.zeros((EXPERT_ROWS - rows, half), ybuf_ref.dtype)

        pl.when(short_ref[b] == 1)(functools.partial(swiglu_rows, EXPERT_ROWS // 2))
        pl.when(short_ref[b] == 0)(functools.partial(swiglu_rows, EXPERT_ROWS))
        rows_out(b, slot).start()

        @pl.when(b == n_valid - 1)
        def _():
            rows_out(b, slot).wait()

            @pl.when(b >= 1)
            def _():
                rows_out(b - 1, 1 - slot).wait()


def _experts(xs, wg, wu, wd, block_expert, next_expert, short_block, n_valid):
    n_rows = xs.shape[0]
    d = wg.shape[1]
    de = wg.shape[2]
    nb = n_rows // EXPERT_ROWS
    hbm = pl.BlockSpec(memory_space=pl.ANY)
    return pl.pallas_call(
        _expert_kernel,
        grid_spec=pltpu.PrefetchScalarGridSpec(
            num_scalar_prefetch=4,
            grid=(nb,),
            in_specs=[hbm, hbm, hbm, hbm],
            out_specs=hbm,
            scratch_shapes=[pltpu.VMEM((d, de), F32), pltpu.VMEM((d, de), F32), pltpu.VMEM((de, d), F32),
                            pltpu.VMEM((d, de), BF16), pltpu.VMEM((d, de), BF16), pltpu.VMEM((de, d), BF16),
                            pltpu.VMEM((EXPERT_X_SLOTS, EXPERT_ROWS, d // 2), I32),
                            pltpu.VMEM((2, EXPERT_ROWS, d // 2), I32),
                            pltpu.SemaphoreType.DMA((3,)), pltpu.SemaphoreType.DMA((EXPERT_X_SLOTS,)),
                            pltpu.SemaphoreType.DMA((2,))]),
        out_shape=jax.ShapeDtypeStruct((n_rows, 1, d // 2), I32),
        compiler_params=_params(1),
        name="moe_experts",
    )(block_expert, next_expert, short_block, n_valid, xs, wg, wu, wd)


def _combine_kernel(dest0_ref, destn_ref, x_ref, shared_ref, gate_ref, ys_ref, g_ref, b_ref,
                    op_ref, os_ref, buf_ref, acc_ref, sems, *, tm, n_prompt_tiles):
    i = pl.program_id(0)
    n = pl.num_programs(0)
    half = buf_ref.shape[-1]
    n_groups = tm // COMBINE_GROUP

    def issue_group(dest_ref, to_slot, g):
        for j in range(COMBINE_GROUP):
            t = g * COMBINE_GROUP + j
            for k in range(TOP_K):
                d = dest_ref[k * tm + t]
                pltpu.make_async_copy(ys_ref.at[d], buf_ref.at[to_slot, k, g, pl.ds(j, 1), :],
                                      sems.at[to_slot]).start(priority=k % 2)

    @pl.when(i == 0)
    def _():
        def first(g, carry):
            issue_group(dest0_ref, 0, g)
            return carry
        lax.fori_loop(0, n_groups, first, 0)

    def run_tile(slot):
        for k in range(TOP_K):
            pltpu.make_async_copy(ys_ref.at[pl.ds(0, tm)], ys_ref.at[pl.ds(0, tm)], sems.at[slot]).wait()

        def reduce_group(g):
            r0 = pl.multiple_of(g * COMBINE_GROUP, COMBINE_GROUP)
            gate = gate_ref[pl.ds(r0, COMBINE_GROUP), :]
            acc_lo = acc_hi = None
            for k in range(TOP_K):
                y_lo, y_hi = _unpack_halves(buf_ref[slot, k, g])
                gk = gate[:, k:k + 1]
                acc_lo = gk * y_lo if k == 0 else acc_lo + gk * y_lo
                acc_hi = gk * y_hi if k == 0 else acc_hi + gk * y_hi
            acc_ref[pl.ds(r0, COMBINE_GROUP), :half] = acc_lo
            acc_ref[pl.ds(r0, COMBINE_GROUP), half:] = acc_hi

        @pl.when(i + 1 < n)
        def _():
            def body(g, carry):
                issue_group(destn_ref, 1 - slot, g)
                reduce_group(g)
                return carry
            lax.fori_loop(0, n_groups, body, 0)

        @pl.when(i + 1 >= n)
        def _():
            def body(g, carry):
                reduce_group(g)
                return carry
            lax.fori_loop(0, n_groups, body, 0)

    for s in range(2):
        pl.when(lax.rem(i, 2) == s)(functools.partial(run_tile, s))

    x = x_ref[...]
    out = _layer_norm(DEEPNORM_ALPHA * x + (acc_ref[...] + shared_ref[...].astype(F32)),
                      g_ref[...], b_ref[...])

    @pl.when(i < n_prompt_tiles)
    def _():
        op_ref[...] = out

    @pl.when(i >= n_prompt_tiles)
    def _():
        os_ref[...] = out


def _combine(x, shared, gate_t, dest_tiles, ys, g, b, tm, n_prompt_rows):
    t, d = x.shape
    n_tiles = t // tm
    npt = n_prompt_rows // tm
    const = lambda i: (0, 0)
    return pl.pallas_call(
        functools.partial(_combine_kernel, tm=tm, n_prompt_tiles=npt),
        grid=(n_tiles,),
        in_specs=[pl.BlockSpec((TOP_K * tm,), lambda i: (0,), memory_space=pltpu.SMEM),
                  pl.BlockSpec((TOP_K * tm,), lambda i: (jnp.minimum(i + 1, n_tiles - 1),),
                               memory_space=pltpu.SMEM),
                  pl.BlockSpec((tm, d), lambda i: (i, 0)),
                  pl.BlockSpec((tm, d), lambda i: (i, 0)),
                  pl.BlockSpec((tm, TOP_K), lambda i: (i, 0)),
                  pl.BlockSpec(memory_space=pl.ANY),
                  pl.BlockSpec((1, d), const),
                  pl.BlockSpec((1, d), const)],
        out_specs=[pl.BlockSpec((tm, d), lambda i: (jnp.minimum(i, npt - 1), 0)),
                   pl.BlockSpec((tm, d), lambda i: (jnp.maximum(i - npt, 0), 0))],
        out_shape=[jax.ShapeDtypeStruct((n_prompt_rows, d), F32),
                   jax.ShapeDtypeStruct((t - n_prompt_rows, d), F32)],
        scratch_shapes=[pltpu.VMEM((2, TOP_K, tm // COMBINE_GROUP, COMBINE_GROUP, d // 2), I32),
                        pltpu.VMEM((tm, d), F32),
                        pltpu.SemaphoreType.DMA((2,))],
        compiler_params=_params(1),
        name="moe_combine",
    )(dest_tiles, dest_tiles, x, shared, gate_t, ys, g.reshape(1, d), b.reshape(1, d))


def _tile_major(a, tm):
    k, t = a.shape
    return a.reshape(k, t // tm, tm).transpose(1, 0, 2).reshape(-1)


def _moe_ln(x, x_packed, router_w, router_bias, wg, wu, wd, swg, swu, swd, g, b, n_prompt_rows):
    t, d = x.shape
    w_t = router_w.T
    w_hi = w_t.astype(BF16)
    w_lo = (w_t - w_hi.astype(F32)).astype(BF16)
    tr = _divisor_tile(t, 512, LANES)
    idx, gate, rank, cnt = _router(x, w_hi, w_lo, router_bias, tr)

    counts = cnt[:, 0]
    padded = (counts + EXPERT_ROWS - 1) // EXPERT_ROWS * EXPERT_ROWS
    pad_end = jnp.cumsum(padded)
    pad_start = pad_end - padded
    experts = jnp.arange(N_EXPERTS, dtype=I32)
    onehot = idx[None] == experts[:, None, None]
    dest = jnp.sum(jnp.where(onehot, pad_start[:, None, None], 0), axis=0) + rank
    n_blocks = (t * TOP_K + N_EXPERTS * (EXPERT_ROWS - 1)) // EXPERT_ROWS + 1
    n_valid = (pad_end[-1] // EXPERT_ROWS).astype(I32)
    blk = jnp.minimum(jnp.arange(n_blocks, dtype=I32), n_valid - 1)
    block_expert = jnp.minimum(
        jnp.sum(pad_end[None, :] <= (blk * EXPERT_ROWS)[:, None], axis=1), N_EXPERTS - 1).astype(I32)
    later_used = (experts[None, :] > experts[:, None]) & (counts[None, :] > 0)
    next_used = jnp.min(jnp.where(later_used, experts[None, :], N_EXPERTS), axis=1)
    next_used = jnp.where(next_used == N_EXPERTS, -1, next_used)
    per_block = lambda table: jnp.sum(
        jnp.where(block_expert[:, None] == experts[None, :], table[None, :], 0), axis=1).astype(I32)
    next_expert = per_block(next_used)
    rows_used = per_block(pad_start + counts) - blk * EXPERT_ROWS
    short_block = (rows_used <= EXPERT_ROWS // 2).astype(I32)

    td = _divisor_tile(t, 256, SUBLANES)
    xs, shared = _dispatch(x_packed, _tile_major(dest, td), (pad_start + counts).astype(I32),
                           (padded - counts).astype(I32), swg.astype(BF16), swu.astype(BF16),
                           swd.astype(BF16), n_blocks * EXPERT_ROWS, td)
    ys = _experts(xs, wg, wu, wd, block_expert, next_expert, short_block, n_valid.reshape(1))
    tc = _divisor_tile(math.gcd(n_prompt_rows, t - n_prompt_rows), 128, SUBLANES)
    return _combine(x, shared, gate.T, _tile_major(dest, tc), ys, g, b, tc, n_prompt_rows)


def kernel(x_prompt, x_sample, mem_prompt, state_gla, cache_swa_k, cache_swa_v, cache_mem_k, cache_mem_v,
           w_in, gla_w_gate2, gla_b_gate, gla_norm_g, swa_sinks, w_mix_out, ln1_g, ln1_b,
           mem_wq, mem_wk, mem_wv, mem_wo, ln2_g, ln2_b,
           router_w, router_bias, exp_w_gate, exp_w_up, exp_w_down, sh_w_gate, sh_w_up, sh_w_down,
           ln3_g, ln3_b):
    nbp, seq, d = x_prompt.shape
    nbs, dec, _ = x_sample.shape
    tp, ts = nbp * seq, nbs * dec
    t = tp + ts
    mem_tokens = mem_prompt.shape[1]
    assert state_gla.shape[0] == DEPTH and seq % SWA_WINDOW == 0 and dec % SUBLANES == 0
    assert cache_swa_k.shape[2] == SWA_WINDOW and tp % dec == 0
    tm = _divisor_tile(math.gcd(tp, ts), 512, 16)
    l = 0

    xp = x_prompt.reshape(tp, d)
    xs = x_sample.reshape(ts, d)

    gate_src = GLA_QK + GLA_QK + GLA_VW + GLA_VW
    swa_rows = SWA_QW + 2 * SWA_KVW
    w_in_rt = _reorder_cast(
        w_in[l].T,
        [(0, 0, gate_src),
         (gate_src + GLA_GATE_RANK, COL_SQ, swa_rows),
         (gate_src, COL_GG, GLA_GATE_RANK)],
        IN_COLS_PADDED, _divisor_tile(IN_COLS_PADDED, 512, 16))
    z = _matmul_split(xp, xs, w_in_rt, tm, IN_COLS_PADDED // 2, F32)

    wa2_pad = jnp.concatenate(
        [gla_w_gate2[l], jnp.zeros((LANES - GLA_GATE_RANK, GLA_QK), F32)], axis=0)
    wa2_hi = wa2_pad.astype(BF16)
    wa2_lo = (wa2_pad - wa2_hi.astype(F32)).astype(BF16)
    wa2_stack = jnp.concatenate([wa2_hi, wa2_lo, wa2_hi], axis=0)
    zero_state = jnp.zeros((nbp, GLA_HEADS, GLA_DK, GLA_DV), F32)
    gla_rows = _divisor_tile(seq, GLA_STEP_ROWS, CHUNK)
    mix, st_p = _gla(z, wa2_stack, gla_b_gate[l], gla_norm_g[l], zero_state, None,
                     nb=nbp, n_chunks=seq // gla_rows, rows=gla_rows, chunk=CHUNK, row0=0, chained=True)
    seqs = _divisor_tile(nbs, GLA_SAMPLE_SEQS, 1)
    assert tp % (seqs * dec) == 0
    mix, st_s = _gla(z, wa2_stack, gla_b_gate[l], gla_norm_g[l], state_gla[l], mix,
                     nb=nbs // seqs, n_chunks=1, rows=seqs * dec, chunk=dec, row0=tp, chained=False)
    nblk = seq // SWA_WINDOW
    prev_p = lambda col: (lambda b, i: (b * nblk + jnp.maximum(i - 1, 0), col))
    mix = _swa(z, swa_sinks[l], z, z, (prev_p(COL_SK // SWA_KVW), prev_p(COL_SV // SWA_KVW)), mix,
               nb=nbp, n_blocks=nblk, rows=SWA_WINDOW, row0=0, q_base=0, q_stride=SWA_WINDOW)
    ck = cache_swa_k[l].reshape(nbs * SWA_WINDOW, SWA_KVW)
    cv = cache_swa_v[l].reshape(nbs * SWA_WINDOW, SWA_KVW)
    prev_s = lambda b, i: (b, 0)
    mix = _swa(z, swa_sinks[l], ck, cv, (prev_s, prev_s), mix,
               nb=nbs, n_blocks=1, rows=dec, row0=tp, q_base=PAST_LEN, q_stride=0)
    h1 = _matmul_res_ln(mix, w_mix_out[l].astype(BF16), (xp, xs), ln1_g[l], ln1_b[l], tm)

    mem = mem_prompt.reshape(nbp * mem_tokens, d)
    tmem = _divisor_tile(nbp * mem_tokens, 512, 16)
    mk = _matmul(mem, mem_wk[l], tmem, _divisor_tile(d, 512, LANES), F32)
    mv = _matmul(mem, mem_wv[l], tmem, _divisor_tile(d, 512, LANES), F32)
    q = _matmul(h1, mem_wq[l].astype(BF16), tm, d, BF16)
    tq = _divisor_tile(seq, 512, 16)
    att = _mem_attn(q, mk, mv, None, nb=nbp, n_tiles=seq // tq, rows=tq, row0=0, mem_tokens=mem_tokens)
    mem_seqs = _divisor_tile(nbs, MEM_SAMPLE_SEQS, 1)
    assert tp % (mem_seqs * dec) == 0
    att = _mem_attn(q, cache_mem_k[l].reshape(nbs * mem_tokens, d), cache_mem_v[l].reshape(nbs * mem_tokens, d),
                    att, nb=nbs // mem_seqs, n_tiles=1, rows=mem_seqs * dec, row0=tp, mem_tokens=mem_tokens,
                    n_seq=mem_seqs)
    h2, h2_packed = _matmul_res_ln(att, mem_wo[l].astype(BF16), h1, ln2_g[l], ln2_b[l], tm, emit_packed=True)

    out_p, out_s = _moe_ln(h2, h2_packed, router_w[l], router_bias[l], exp_w_gate[l], exp_w_up[l],
                           exp_w_down[l], sh_w_gate[l], sh_w_up[l], sh_w_down[l], ln3_g[l], ln3_b[l], tp)

    y_prompt = out_p.reshape(nbp, seq, d)
    y_sample = out_s.reshape(nbs, dec, d)
    kv = z[:, COL_SK:COL_SK + 2 * SWA_KVW]
    kvp = kv[:tp].reshape(nbp, seq, 2 * SWA_KVW)[:, seq - SWA_WINDOW:]
    kvs = kv[tp:].reshape(nbs, dec, 2 * SWA_KVW)
    kv_shape = lambda a: a.reshape(a.shape[0], a.shape[1], SWA_KV_HEADS, SWA_HEAD_DIM)[None]
    mem_shape = lambda a: a.reshape(nbp, mem_tokens, MEM_HEADS, d // MEM_HEADS)[None]
    return (y_prompt, y_sample,
            st_p[None],
            kv_shape(kvp[..., :SWA_KVW]), kv_shape(kvp[..., SWA_KVW:]),
            mem_shape(mk), mem_shape(mv),
            st_s[None],
            kv_shape(kvs[..., :SWA_KVW]), kv_shape(kvs[..., SWA_KVW:]))
```

```python
import functools
import math

import jax
import jax.numpy as jnp
from jax import lax
from jax.experimental import pallas as pl
from jax.experimental.pallas import tpu as pltpu

F32 = jnp.float32
BF16 = jnp.bfloat16
I32 = jnp.int32

CHUNK = 64
PAST_LEN = 2048
GLA_HEADS = 8
GLA_DK = 64
GLA_DV = 128
GLA_QK = GLA_HEADS * GLA_DK
GLA_VW = GLA_HEADS * GLA_DV
GLA_GATE_RANK = 16
GLA_TAU = 16.0
SWA_HEADS = 16
SWA_KV_HEADS = 2
SWA_GROUP = SWA_HEADS // SWA_KV_HEADS
SWA_HEAD_DIM = 64
SWA_QW = SWA_HEADS * SWA_HEAD_DIM
SWA_KVW = SWA_KV_HEADS * SWA_HEAD_DIM
SWA_WINDOW = 128
WINDOW_CHUNKS = SWA_WINDOW // CHUNK
MEM_HEADS = 4
N_EXPERTS = 64
TOP_K = 8
N_GROUPS = 8
GROUP_SIZE = N_EXPERTS // N_GROUPS
TOPK_GROUPS = 4
ROUTED_SCALE = 2.5
DEPTH = 1
DEEPNORM_ALPHA = (2 * DEPTH) ** 0.25
LN_EPS = 1e-5
RMS_EPS = 1e-6

LANES = 128
SUBLANES = 8
VMEM_LIMIT_BYTES = 56 * 1024 * 1024

COL_GQ = 0
COL_GK = COL_GQ + GLA_QK
COL_GV = COL_GK + GLA_QK
COL_GR = COL_GV + GLA_VW
COL_SQ = COL_GR + GLA_VW
COL_SK = COL_SQ + SWA_QW
COL_SV = COL_SK + SWA_KVW
COL_GG = COL_SV + SWA_KVW
MXU_COLS = 256
IN_COLS_PADDED = -(-(COL_GG + LANES) // (2 * MXU_COLS)) * (2 * MXU_COLS)

EXPERT_ROWS = 256
EXPERT_X_SLOTS = 3
COMBINE_GROUP = SUBLANES
LN_SUB_ROWS = 128
GLA_STEP_ROWS = 512
GLA_SAMPLE_SEQS = 8
MEM_SAMPLE_SEQS = 4
COPY_SLOTS = 3

NT_DIMS = (((1,), (1,)), ((), ()))


def _params(n_axes):
    return pltpu.CompilerParams(dimension_semantics=("arbitrary",) * n_axes,
                                vmem_limit_bytes=VMEM_LIMIT_BYTES)


def _sigmoid(x):
    return 1.0 / (1.0 + jnp.exp(-x))


def _silu(x):
    return x * _sigmoid(x)


def _layer_norm(x, g, b):
    mu = jnp.mean(x, axis=-1, keepdims=True)
    xc = x - mu
    var = jnp.mean(xc * xc, axis=-1, keepdims=True)
    return xc * lax.rsqrt(var + LN_EPS) * g + b


HI_MASK = -65536


def _round_to_bf16_bits(x):
    b = lax.bitcast_convert_type(x, I32)
    return b + (0x7FFF + (lax.shift_right_logical(b, 16) & 1))


def _pack_halves(x):
    c = x.shape[1] // 2
    lo = lax.shift_right_logical(_round_to_bf16_bits(x[:, :c]), 16)
    hi = _round_to_bf16_bits(x[:, c:]) & HI_MASK
    return lo | hi


def _unpack_halves(w):
    return (lax.bitcast_convert_type(lax.shift_left(w, 16), F32),
            lax.bitcast_convert_type(w & HI_MASK, F32))


def _divisor_tile(n, pref, mult):
    t = min(pref, n)
    while t > mult and (n % t or t % mult):
        t -= mult
    assert n % t == 0 and t % mult == 0, (n, pref, mult)
    return t


def _mm_kernel(x_ref, w_ref, o_ref, xb_ref):
    @pl.when(pl.program_id(1) == 0)
    def _():
        xb_ref[...] = x_ref[...].astype(BF16)

    o_ref[...] = jnp.dot(xb_ref[...], w_ref[...].astype(BF16), preferred_element_type=F32).astype(o_ref.dtype)


def _matmul(x, w, tm, tn, out_dtype):
    m, k = x.shape
    n = w.shape[1]
    return pl.pallas_call(
        _mm_kernel,
        grid=(m // tm, n // tn),
        in_specs=[pl.BlockSpec((tm, k), lambda i, j: (i, 0)),
                  pl.BlockSpec((k, tn), lambda i, j: (0, j))],
        out_specs=pl.BlockSpec((tm, tn), lambda i, j: (i, j)),
        out_shape=jax.ShapeDtypeStruct((m, n), out_dtype),
        scratch_shapes=[pltpu.VMEM((tm, k), BF16)],
        compiler_params=_params(2),
        name="matmul",
    )(x, w)


def _split_rows_specs(tm, k, n_first_tiles, n_grid_axes):
    if n_grid_axes == 1:
        return [pl.BlockSpec((tm, k), lambda i: (jnp.minimum(i, n_first_tiles - 1), 0)),
                pl.BlockSpec((tm, k), lambda i: (jnp.maximum(i - n_first_tiles, 0), 0))]
    return [pl.BlockSpec((tm, k), lambda j, i: (jnp.minimum(i, n_first_tiles - 1), 0)),
            pl.BlockSpec((tm, k), lambda j, i: (jnp.maximum(i - n_first_tiles, 0), 0))]


def _mm_split_kernel(xa_ref, xb_ref, wt_ref, o_ref, *, n_first_tiles):
    i = pl.program_id(1)

    def product(x_ref):
        o_ref[...] = lax.dot_general(x_ref[...].astype(BF16), wt_ref[...], NT_DIMS,
                                     preferred_element_type=F32).astype(o_ref.dtype)

    @pl.when(i < n_first_tiles)
    def _():
        product(xa_ref)

    @pl.when(i >= n_first_tiles)
    def _():
        product(xb_ref)


def _matmul_split(xa, xb, w_t, tm, tn, out_dtype):
    ma, k = xa.shape
    m = ma + xb.shape[0]
    n = w_t.shape[0]
    return pl.pallas_call(
        functools.partial(_mm_split_kernel, n_first_tiles=ma // tm),
        grid=(n // tn, m // tm),
        in_specs=_split_rows_specs(tm, k, ma // tm, 2) + [pl.BlockSpec((tn, k), lambda j, i: (j, 0))],
        out_specs=pl.BlockSpec((tm, tn), lambda j, i: (i, j)),
        out_shape=jax.ShapeDtypeStruct((m, n), out_dtype),
        compiler_params=_params(2),
        name="matmul_split",
    )(xa, xb, w_t)


def _reorder_cast_kernel(src_hbm, dst_hbm, stage_ref, out_ref, in_sems, out_sems, *, chunk_pieces, chunk):
    n = len(chunk_pieces)
    slots = stage_ref.shape[0]

    def in_copies(c):
        return [pltpu.make_async_copy(src_hbm.at[pl.ds(s0, ln)], stage_ref.at[c % slots, pl.ds(o0, ln)],
                                      in_sems.at[c % slots]) for s0, o0, ln in chunk_pieces[c]]

    def out_copy(c):
        return pltpu.make_async_copy(out_ref.at[c % slots], dst_hbm.at[pl.ds(c * chunk, chunk)],
                                     out_sems.at[c % slots])

    for c in range(min(slots - 1, n)):
        for cp in in_copies(c):
            cp.start()
    for c in range(n):
        if c + slots - 1 < n:
            for cp in in_copies(c + slots - 1):
                cp.start()
        for cp in in_copies(c):
            cp.wait()
        if c >= slots:
            out_copy(c - slots).wait()
        covered = sum(ln for _, _, ln in chunk_pieces[c])
        if covered < chunk:
            stage_ref[c % slots, covered:, :] = jnp.zeros((chunk - covered, stage_ref.shape[-1]),
                                                          stage_ref.dtype)
        out_ref[c % slots] = stage_ref[c % slots].astype(out_ref.dtype)
        out_copy(c).start()
    for c in range(max(n - slots, 0), n):
        out_copy(c).wait()


def _reorder_cast(src, runs, n_out_rows, chunk):
    k = src.shape[1]
    chunk_pieces = []
    for c0 in range(0, n_out_rows, chunk):
        pieces = []
        for s0, d0, ln in runs:
            lo, hi = max(d0, c0), min(d0 + ln, c0 + chunk)
            if lo < hi:
                assert (lo - c0) % 16 == 0 and (s0 + lo - d0) % SUBLANES == 0 and (hi - lo) % 16 == 0
                pieces.append((s0 + lo - d0, lo - c0, hi - lo))
        chunk_pieces.append(pieces)
    hbm = pl.BlockSpec(memory_space=pl.ANY)
    return pl.pallas_call(
        functools.partial(_reorder_cast_kernel, chunk_pieces=chunk_pieces, chunk=chunk),
        in_specs=[hbm],
        out_specs=hbm,
        out_shape=jax.ShapeDtypeStruct((n_out_rows, k), BF16),
        scratch_shapes=[pltpu.VMEM((COPY_SLOTS, chunk, k), src.dtype), pltpu.VMEM((COPY_SLOTS, chunk, k), BF16),
                        pltpu.SemaphoreType.DMA((COPY_SLOTS,)), pltpu.SemaphoreType.DMA((COPY_SLOTS,))],
        compiler_params=pltpu.CompilerParams(vmem_limit_bytes=VMEM_LIMIT_BYTES),
        name="reorder_cast",
    )(src)


def _mm_res_ln_kernel(x_ref, w_ref, *refs, n_res, n_first_tiles):
    res_refs, (g_ref, b_ref, o_ref, *packed_ref) = refs[:n_res], refs[n_res:]
    tm = x_ref.shape[0]
    sub = min(tm, LN_SUB_ROWS)
    use_first = pl.program_id(0) < n_first_tiles
    for r in range(0, tm, sub):
        rows = slice(r, r + sub)
        y = jnp.dot(x_ref[rows, :], w_ref[...], preferred_element_type=F32)
        res = res_refs[0][rows, :]
        if n_res == 2:
            res = jnp.where(use_first, res, res_refs[1][rows, :])
        h = _layer_norm(DEEPNORM_ALPHA * res + y, g_ref[...], b_ref[...])
        o_ref[rows, :] = h
        if packed_ref:
            groups = slice(r // SUBLANES, (r + sub) // SUBLANES)
            packed_ref[0][groups] = _pack_halves(h).reshape(sub // SUBLANES, SUBLANES, h.shape[1] // 2)


def _matmul_res_ln(x, w, res, g, b, tm, emit_packed=False):
    m, k = x.shape
    n = w.shape[1]
    if isinstance(res, tuple):
        n_first_tiles = res[0].shape[0] // tm
        res_specs = _split_rows_specs(tm, n, n_first_tiles, 1)
    else:
        n_first_tiles = 0
        res = (res,)
        res_specs = [pl.BlockSpec((tm, n), lambda i: (i, 0))]
    out_specs = [pl.BlockSpec((tm, n), lambda i: (i, 0))]
    out_shape = [jax.ShapeDtypeStruct((m, n), F32)]
    if emit_packed:
        out_specs.append(pl.BlockSpec((tm // SUBLANES, SUBLANES, n // 2), lambda i: (i, 0, 0)))
        out_shape.append(jax.ShapeDtypeStruct((m // SUBLANES, SUBLANES, n // 2), I32))
    outs = pl.pallas_call(
        functools.partial(_mm_res_ln_kernel, n_res=len(res), n_first_tiles=n_first_tiles),
        grid=(m // tm,),
        in_specs=[pl.BlockSpec((tm, k), lambda i: (i, 0)),
                  pl.BlockSpec((k, n), lambda i: (0, 0))] + res_specs + [
                  pl.BlockSpec((1, n), lambda i: (0, 0)),
                  pl.BlockSpec((1, n), lambda i: (0, 0))],
        out_specs=out_specs,
        out_shape=out_shape,
        compiler_params=_params(1),
        name="matmul_res_ln",
    )(x, w, *res, g.reshape(1, n), b.reshape(1, n))
    return outs if emit_packed else outs[0]


def _split3_bf16(x):
    hi = x.astype(BF16)
    r1 = x - hi.astype(F32)
    mid = r1.astype(BF16)
    lo = (r1 - mid.astype(F32)).astype(BF16)
    return jnp.concatenate([hi, mid, lo], axis=-1)


def _gla_kernel(q_ref, k_ref, v_ref, r_ref, gg_ref, wa2_ref, ba_ref, ng_ref, s0_ref, mix_in_ref,
                o_ref, sfin_ref, st_ref, *, rows, chunk, chained):
    del mix_in_ref
    n_sub = rows // chunk
    assert rows % chunk == 0 and chunk % 16 == 0

    if chained:
        @pl.when(pl.program_id(1) == 0)
        def _():
            for h in range(GLA_HEADS):
                st_ref[h] = s0_ref[0, h].T

    gg = gg_ref[...]
    gg_hi = gg.astype(BF16)
    gg_lo = (gg - gg_hi.astype(F32)).astype(BF16)
    gate = jnp.dot(jnp.concatenate([gg_hi, gg_hi, gg_lo], axis=-1), wa2_ref[...],
                   preferred_element_type=F32) + ba_ref[...]
    log_a = (jnp.minimum(gate, 0.0) - jnp.log1p(jnp.exp(-jnp.abs(gate)))) / GLA_TAU

    ri_c = lax.broadcasted_iota(I32, (chunk, chunk), 0)
    ci_c = lax.broadcasted_iota(I32, (chunk, chunk), 1)
    causal_c = ri_c >= ci_c
    ones = jnp.concatenate([jnp.where(causal_c, 1.0, 0.0), jnp.ones((chunk, chunk), F32)], axis=0).astype(BF16)
    sums = jnp.einsum("cij,cjd->cid", jnp.broadcast_to(ones[None], (n_sub, 2 * chunk, chunk)),
                      _split3_bf16(log_a).reshape(n_sub, chunk, 3 * GLA_QK),
                      preferred_element_type=F32)
    sums = sums[..., :GLA_QK] + sums[..., GLA_QK:2 * GLA_QK] + sums[..., 2 * GLA_QK:]
    bcum = sums[:, :chunk].reshape(rows, GLA_QK)
    b_last = sums[:, chunk:].reshape(rows, GLA_QK)
    q = q_ref[...] * (GLA_DK ** -0.5)
    k = k_ref[...]
    q_in = (q * jnp.exp(bcum)).astype(BF16)
    k_in = (k * jnp.exp(-bcum)).astype(BF16)
    k_out = (k * jnp.exp(b_last - bcum)).astype(BF16)
    decay = jnp.exp(b_last)
    ng = ng_ref[...]
    def heads(x, width):
        return jnp.stack([x[:, h * width:(h + 1) * width] for h in range(GLA_HEADS)], axis=0)

    def head_chunks(x, width):
        return heads(x, width).reshape(GLA_HEADS * n_sub, chunk, width)

    q4 = head_chunks(q_in, GLA_DK)
    v4 = head_chunks(v_ref[...].astype(BF16), GLA_DV)
    attn = jnp.einsum("bqd,bkd->bqk", q4, head_chunks(k_in, GLA_DK), preferred_element_type=F32)
    attn = jnp.where(causal_c[None], attn, 0.0).astype(BF16)
    o = jnp.einsum("bqk,bke->bqe", attn, v4, preferred_element_type=F32)
    upd_t = jnp.einsum("bre,brc->bec", v4, head_chunks(k_out, GLA_DK), preferred_element_type=F32)
    upd_t = upd_t.reshape(GLA_HEADS, n_sub, GLA_DV, GLA_DK)
    states = []
    if chained:
        st = st_ref[...]
    for c in range(n_sub):
        if not chained:
            st = jnp.stack([s0_ref[c, h].T for h in range(GLA_HEADS)], axis=0)
        states.append(st.astype(BF16))
        st = st * heads(decay[c * chunk:c * chunk + 1, :], GLA_DK) + upd_t[:, c]
        if not chained:
            for h in range(GLA_HEADS):
                sfin_ref[c, h] = st[h].T
    if chained:
        st_ref[...] = st
    entering = jnp.stack(states, axis=1).reshape(GLA_HEADS * n_sub, GLA_DV, GLA_DK)
    o = o + jnp.einsum("bqd,bsd->bqs", q4, entering, preferred_element_type=F32)
    o = o.reshape(GLA_HEADS, rows, GLA_DV)
    o = o * lax.rsqrt(jnp.mean(o * o, axis=-1, keepdims=True) + RMS_EPS) * ng[None]
    out = (o * _silu(heads(r_ref[...], GLA_DV))).astype(o_ref.dtype)
    for h in range(GLA_HEADS):
        o_ref[:, h * GLA_DV:(h + 1) * GLA_DV] = out[h]
    if chained:
        @pl.when(pl.program_id(1) == pl.num_programs(1) - 1)
        def _():
            for h in range(GLA_HEADS):
                sfin_ref[0, h] = st_ref[h].T


def _gla(z, wa2_stack, ba, norm_g, s0, mix_in, *, nb, n_chunks, rows, chunk, row0, chained):
    t = z.shape[0]
    rb0 = row0 // rows
    seq_per_step = 1 if chained else rows // chunk
    assert chained or n_chunks == 1
    rmap = lambda colblk: (lambda b, c: (rb0 + b * n_chunks + c, colblk))
    in_specs = [
        pl.BlockSpec((rows, GLA_QK), rmap(COL_GQ // GLA_QK)),
        pl.BlockSpec((rows, GLA_QK), rmap(COL_GK // GLA_QK)),
        pl.BlockSpec((rows, GLA_VW), rmap(COL_GV // GLA_VW)),
        pl.BlockSpec((rows, GLA_VW), rmap(COL_GR // GLA_VW)),
        pl.BlockSpec((rows, LANES), rmap(COL_GG // LANES)),
        pl.BlockSpec((3 * LANES, GLA_QK), lambda b, c: (0, 0)),
        pl.BlockSpec((1, GLA_QK), lambda b, c: (0, 0)),
        pl.BlockSpec((1, GLA_DV), lambda b, c: (0, 0)),
        pl.BlockSpec((seq_per_step, GLA_HEADS, GLA_DK, GLA_DV), lambda b, c: (b, 0, 0, 0)),
    ]
    args = [z, z, z, z, z, wa2_stack, ba.reshape(1, GLA_QK), norm_g.reshape(1, GLA_DV), s0]
    aliases = {}
    if mix_in is None:
        mix_in = jnp.zeros((SUBLANES, LANES), BF16)
        in_specs.append(pl.BlockSpec(memory_space=pl.ANY))
    else:
        in_specs.append(pl.BlockSpec(memory_space=pl.ANY))
        aliases = {len(args): 0}
    args.append(mix_in)
    mix, s_fin = pl.pallas_call(
        functools.partial(_gla_kernel, rows=rows, chunk=chunk, chained=chained),
        grid=(nb, n_chunks),
        in_specs=in_specs,
        out_specs=[pl.BlockSpec((rows, GLA_VW), rmap(0)),
                   pl.BlockSpec((seq_per_step, GLA_HEADS, GLA_DK, GLA_DV), lambda b, c: (b, 0, 0, 0))],
        out_shape=[jax.ShapeDtypeStruct((t, GLA_VW + SWA_QW), BF16),
                   jax.ShapeDtypeStruct((nb * seq_per_step, GLA_HEADS, GLA_DK, GLA_DV), F32)],
        scratch_shapes=[pltpu.VMEM((GLA_HEADS, GLA_DV, GLA_DK), F32)],
        input_output_aliases=aliases,
        compiler_params=_params(2),
        name="gla",
    )(*args)
    return mix, s_fin


def _swa_kernel(sinks_ref, q_ref, kc_ref, vc_ref, kp_ref, vp_ref, mix_in_ref, o_ref, bias_ref, *,
                rows, q_base, q_stride):
    del mix_in_ref
    i = pl.program_id(1)
    n_tables = bias_ref.shape[0]
    nk = SWA_WINDOW + rows
    sr = SWA_GROUP * rows
    assert rows & (rows - 1) == 0 and q_base % CHUNK == 0 and q_stride % CHUNK == 0
    assert n_tables == 1 or q_base + q_stride >= SWA_WINDOW
    head_in_group = lax.shift_right_logical(lax.broadcasted_iota(I32, (sr, 1), 0), rows.bit_length() - 1)

    @pl.when(jnp.logical_and(pl.program_id(0) == 0, i == 0))
    def _():
        row = lax.broadcasted_iota(I32, (sr, nk), 0)
        col = lax.broadcasted_iota(I32, (sr, nk), 1)
        for t in range(n_tables):
            q0 = q_base + t * q_stride
            qpos = q0 + (row & (rows - 1))
            kpos = q0 - SWA_WINDOW + col
            dist = jnp.abs(qpos - kpos).astype(F32)
            qc = lax.shift_right_arithmetic(qpos, CHUNK.bit_length() - 1)
            kc = lax.shift_right_arithmetic(kpos, CHUNK.bit_length() - 1)
            allowed = (kpos >= 0) & (kc <= qc) & (kc >= qc - WINDOW_CHUNKS)
            for g in range(SWA_KV_HEADS):
                slope = jnp.zeros((sr, 1), F32)
                for j in range(SWA_GROUP):
                    h = g * SWA_GROUP + j
                    slope = jnp.where(head_in_group == j, 2.0 ** (-8.0 * (h + 1) / SWA_HEADS), slope)
                bias_ref[t, g] = jnp.where(allowed, -(slope * dist), -jnp.inf)

    table = jnp.minimum(i, n_tables - 1)
    keys = jnp.concatenate([kp_ref[...], kc_ref[...]], axis=0).astype(BF16)
    vals = jnp.concatenate([vp_ref[...], vc_ref[...]], axis=0).astype(BF16)
    for g in range(SWA_KV_HEADS):
        sink = jnp.zeros((sr, 1), F32)
        for j in range(SWA_GROUP):
            sink = jnp.where(head_in_group == j, sinks_ref[g * SWA_GROUP + j], sink)
        kv = slice(g * SWA_HEAD_DIM, (g + 1) * SWA_HEAD_DIM)
        qg = jnp.concatenate(
            [q_ref[:, (g * SWA_GROUP + j) * SWA_HEAD_DIM:(g * SWA_GROUP + j + 1) * SWA_HEAD_DIM]
             for j in range(SWA_GROUP)], axis=0).astype(BF16)
        s = lax.dot_general(qg, keys[:, kv], NT_DIMS, preferred_element_type=F32)
        s = s * (SWA_HEAD_DIM ** -0.5) + bias_ref[table, g]
        m = jnp.maximum(jnp.max(s, axis=-1, keepdims=True), sink)
        p = jnp.exp(s - m)
        denom = jnp.sum(p, axis=-1, keepdims=True) + jnp.exp(sink - m)
        o = jnp.dot(p.astype(BF16), vals[:, kv], preferred_element_type=F32) / denom
        for j in range(0, SWA_GROUP, 2):
            c0 = (g * SWA_GROUP + j) * SWA_HEAD_DIM
            o_ref[:, c0:c0 + LANES] = jnp.concatenate(
                [o[j * rows:(j + 1) * rows], o[(j + 1) * rows:(j + 2) * rows]], axis=-1).astype(o_ref.dtype)


def _swa(z, sinks, k_prev, v_prev, prev_map, mix_in, *, nb, n_blocks, rows, row0, q_base, q_stride):
    rb0 = row0 // rows
    rmap = lambda colblk: (lambda b, i: (rb0 + b * n_blocks + i, colblk))
    return pl.pallas_call(
        functools.partial(_swa_kernel, rows=rows, q_base=q_base, q_stride=q_stride),
        grid=(nb, n_blocks),
        in_specs=[pl.BlockSpec(memory_space=pltpu.SMEM),
                  pl.BlockSpec((rows, SWA_QW), rmap(COL_SQ // SWA_QW)),
                  pl.BlockSpec((rows, SWA_KVW), rmap(COL_SK // SWA_KVW)),
                  pl.BlockSpec((rows, SWA_KVW), rmap(COL_SV // SWA_KVW)),
                  pl.BlockSpec((SWA_WINDOW, SWA_KVW), prev_map[0]),
                  pl.BlockSpec((SWA_WINDOW, SWA_KVW), prev_map[1]),
                  pl.BlockSpec(memory_space=pl.ANY)],
        out_specs=pl.BlockSpec((rows, SWA_QW), rmap(GLA_VW // SWA_QW)),
        out_shape=jax.ShapeDtypeStruct(mix_in.shape, mix_in.dtype),
        scratch_shapes=[pltpu.VMEM((min(2, n_blocks), SWA_KV_HEADS, SWA_GROUP * rows, SWA_WINDOW + rows), F32)],
        input_output_aliases={6: 0},
        compiler_params=_params(2),
        name="swa",
    )(sinks, z, z, z, k_prev, v_prev, mix_in)


def _mem_attn_kernel(q_ref, k_ref, v_ref, o_in_ref, o_ref, *, head_dim, n_seq):
    del o_in_ref
    rq = q_ref.shape[0] // n_seq
    rm = k_ref.shape[0] // n_seq
    for s_i in range(n_seq):
        qr = slice(s_i * rq, (s_i + 1) * rq)
        mr = slice(s_i * rm, (s_i + 1) * rm)
        heads = [slice(h * head_dim, (h + 1) * head_dim) for h in range(MEM_HEADS)]
        q3 = jnp.stack([q_ref[qr, hs] for hs in heads], axis=0)
        k3 = jnp.stack([k_ref[mr, hs] for hs in heads], axis=0).astype(BF16)
        v3 = jnp.stack([v_ref[mr, hs] for hs in heads], axis=0).astype(BF16)
        s = jnp.einsum("hqd,hkd->hqk", q3, k3, preferred_element_type=F32) * (head_dim ** -0.5)
        m = jnp.max(s, axis=-1, keepdims=True)
        p = jnp.exp(s - m)
        denom = jnp.sum(p, axis=-1, keepdims=True)
        o = jnp.einsum("hqk,hkd->hqd", p.astype(BF16), v3, preferred_element_type=F32) / denom
        for h, hs in enumerate(heads):
            o_ref[qr, hs] = o[h].astype(o_ref.dtype)


def _mem_attn(q, mk, mv, o_in, *, nb, n_tiles, rows, row0, mem_tokens, n_seq=1):
    t, d = q.shape
    assert n_seq == 1 or n_tiles == 1
    rb0 = row0 // rows
    qmap = lambda b, i: (rb0 + b * n_tiles + i, 0)
    in_specs = [pl.BlockSpec((rows, d), qmap),
                pl.BlockSpec((n_seq * mem_tokens, d), lambda b, i: (b, 0)),
                pl.BlockSpec((n_seq * mem_tokens, d), lambda b, i: (b, 0)),
                pl.BlockSpec(memory_space=pl.ANY)]
    aliases = {}
    if o_in is None:
        o_in = jnp.zeros((SUBLANES, LANES), BF16)
    else:
        aliases = {3: 0}
    return pl.pallas_call(
        functools.partial(_mem_attn_kernel, head_dim=d // MEM_HEADS, n_seq=n_seq),
        grid=(nb, n_tiles),
        in_specs=in_specs,
        out_specs=pl.BlockSpec((rows, d), qmap),
        out_shape=jax.ShapeDtypeStruct((t, d), BF16),
        input_output_aliases=aliases,
        compiler_params=_params(2),
        name="mem_attn",
    )(q, mk, mv, o_in)


def _pick_first_max(cur, iota, sentinel):
    mx = jnp.max(cur, axis=0, keepdims=True)
    first = jnp.min(jnp.where(cur == mx, iota, sentinel), axis=0, keepdims=True)
    return iota == first, first


def _router_kernel(x_ref, wh_ref, wl_ref, bias_ref, idx_ref, gate_ref, rank_ref, cnt_ref, run_ref, before_ref,
                   *, tm):
    i = pl.program_id(0)

    @pl.when(i == 0)
    def _():
        run_ref[...] = jnp.zeros_like(run_ref)
        ti = lax.broadcasted_iota(I32, (tm, tm), 0)
        tj = lax.broadcasted_iota(I32, (tm, tm), 1)
        before_ref[...] = jnp.where(ti < tj, 1.0, 0.0).astype(BF16)

    x = x_ref[...]
    xh = x.astype(BF16)
    xl = (x - xh.astype(F32)).astype(BF16)
    wh = wh_ref[...]
    logits = (lax.dot_general(wh, xh, NT_DIMS, preferred_element_type=F32)
              + lax.dot_general(wh, xl, NT_DIMS, preferred_element_type=F32)
              + lax.dot_general(wl_ref[...], xh, NT_DIMS, preferred_element_type=F32))
    scores = _sigmoid(logits)
    sel = scores + bias_ref[...]
    neg_inf = -jnp.inf

    li = lax.broadcasted_iota(I32, (GROUP_SIZE, tm), 0)
    grp_rows = []
    for g in range(N_GROUPS):
        blk = sel[g * GROUP_SIZE:(g + 1) * GROUP_SIZE, :]
        pick, _ = _pick_first_max(blk, li, GROUP_SIZE)
        m1 = jnp.max(blk, axis=0, keepdims=True)
        m2 = jnp.max(jnp.where(pick, neg_inf, blk), axis=0, keepdims=True)
        grp_rows.append(m1 + m2)
    grp = jnp.concatenate(grp_rows, axis=0)

    gi = lax.broadcasted_iota(I32, (N_GROUPS, tm), 0)
    gsel = jnp.zeros((N_GROUPS, tm), F32)
    cur = grp
    for _ in range(TOPK_GROUPS):
        pick, _ = _pick_first_max(cur, gi, N_GROUPS)
        gsel = jnp.where(pick, 1.0, gsel)
        cur = jnp.where(pick, neg_inf, cur)
    emask = jnp.concatenate(
        [jnp.broadcast_to(gsel[g:g + 1, :], (GROUP_SIZE, tm)) for g in range(N_GROUPS)], axis=0)

    ei = lax.broadcasted_iota(I32, (N_EXPERTS, tm), 0)
    cur = jnp.where(emask > 0.5, sel, neg_inf)
    chosen = jnp.zeros((N_EXPERTS, tm), F32)
    idx_rows, w_rows = [], []
    for _ in range(TOP_K):
        pick, first = _pick_first_max(cur, ei, N_EXPERTS)
        idx_rows.append(first)
        w_rows.append(jnp.sum(jnp.where(pick, scores, 0.0), axis=0, keepdims=True))
        chosen = jnp.where(pick, 1.0, chosen)
        cur = jnp.where(pick, neg_inf, cur)
    idx = jnp.concatenate(idx_rows, axis=0)
    w = jnp.concatenate(w_rows, axis=0)
    gate_ref[...] = w / jnp.sum(w, axis=0, keepdims=True) * ROUTED_SCALE
    idx_ref[...] = idx

    local = jnp.dot(chosen.astype(BF16), before_ref[...], preferred_element_type=F32)
    total = local + run_ref[:, 0:1]
    rank_rows = [jnp.sum(jnp.where(ei == idx_rows[k], total, 0.0), axis=0, keepdims=True)
                 for k in range(TOP_K)]
    rank_ref[...] = jnp.concatenate(rank_rows, axis=0).astype(I32)
    run_ref[...] = run_ref[...] + jnp.sum(chosen, axis=1, keepdims=True)
    cnt_ref[...] = run_ref[...].astype(I32)


def _router(x, w_hi_t, w_lo_t, bias, tm):
    t, d = x.shape
    return pl.pallas_call(
        functools.partial(_router_kernel, tm=tm),
        grid=(t // tm,),
        in_specs=[pl.BlockSpec((tm, d), lambda i: (i, 0)),
                  pl.BlockSpec((N_EXPERTS, d), lambda i: (0, 0)),
                  pl.BlockSpec((N_EXPERTS, d), lambda i: (0, 0)),
                  pl.BlockSpec((N_EXPERTS, 1), lambda i: (0, 0))],
        out_specs=[pl.BlockSpec((TOP_K, tm), lambda i: (0, i)),
                   pl.BlockSpec((TOP_K, tm), lambda i: (0, i)),
                   pl.BlockSpec((TOP_K, tm), lambda i: (0, i)),
                   pl.BlockSpec((N_EXPERTS, LANES), lambda i: (0, 0))],
        out_shape=[jax.ShapeDtypeStruct((TOP_K, t), I32),
                   jax.ShapeDtypeStruct((TOP_K, t), F32),
                   jax.ShapeDtypeStruct((TOP_K, t), I32),
                   jax.ShapeDtypeStruct((N_EXPERTS, LANES), I32)],
        scratch_shapes=[pltpu.VMEM((N_EXPERTS, LANES), F32), pltpu.VMEM((tm, tm), BF16)],
        compiler_params=_params(1),
        name="router",
    )(x, w_hi_t, w_lo_t, bias.reshape(N_EXPERTS, 1))


PAD_BITS = tuple(1 << s for s in reversed(range(EXPERT_ROWS.bit_length() - 1)))


def _dispatch_kernel(zstart_ref, zcount_ref, dest_ref, x_ref, swg_ref, swu_ref, swd_ref, xs_ref, shared_ref,
                     zero_ref, sem, zsem, *, tm):
    i = pl.program_id(0)

    def start_group(g, carry):
        for j in range(SUBLANES):
            for k in range(TOP_K):
                d = dest_ref[k * tm + g * SUBLANES + j]
                pltpu.make_async_copy(x_ref.at[g, pl.ds(j, 1), :], xs_ref.at[d], sem).start(priority=k % 2)
        return carry

    lax.fori_loop(0, tm // SUBLANES, start_group, 0)

    half = x_ref.shape[-1]
    x_lo, x_hi = _unpack_halves(x_ref[...].reshape(tm, half))
    xb = jnp.concatenate([x_lo.astype(BF16), x_hi.astype(BF16)], axis=-1)
    hg = jnp.dot(xb, swg_ref[...], preferred_element_type=F32)
    hu = jnp.dot(xb, swu_ref[...], preferred_element_type=F32)
    shared_ref[...] = jnp.dot((_silu(hg) * hu).astype(BF16), swd_ref[...],
                              preferred_element_type=F32).astype(shared_ref.dtype)

    for k in range(TOP_K):
        pltpu.make_async_copy(xs_ref.at[pl.ds(0, tm)], xs_ref.at[pl.ds(0, tm)], sem).wait()

    @pl.when(i == pl.num_programs(0) - 1)
    def _():
        zero_ref[...] = jnp.zeros_like(zero_ref)

        def fill(e, wait):
            pos = zstart_ref[e]
            cnt = zcount_ref[e]
            for bit in PAD_BITS:
                has = (cnt & bit) != 0

                @pl.when(has)
                def _():
                    cp = pltpu.make_async_copy(zero_ref.at[pl.ds(0, bit)], xs_ref.at[pl.ds(pos, bit)], zsem)
                    if wait:
                        cp.wait()
                    else:
                        cp.start()

                pos = pos + jnp.where(has, bit, 0)

        def fill_start(e, carry):
            fill(e, False)
            return carry

        def fill_wait(e, carry):
            fill(e, True)
            return carry

        lax.fori_loop(0, N_EXPERTS, fill_start, 0)
        lax.fori_loop(0, N_EXPERTS, fill_wait, 0)


def _dispatch(x, dest_tiles, zstart, zcount, swg, swu, swd, n_rows, tm):
    t = x.shape[0] * SUBLANES
    d = x.shape[2]
    ds = swg.shape[1]
    const = lambda i, zs, zc: (0, 0)
    return pl.pallas_call(
        functools.partial(_dispatch_kernel, tm=tm),
        grid_spec=pltpu.PrefetchScalarGridSpec(
            num_scalar_prefetch=2,
            grid=(t // tm,),
            in_specs=[pl.BlockSpec((TOP_K * tm,), lambda i, zs, zc: (i,), memory_space=pltpu.SMEM),
                      pl.BlockSpec((tm // SUBLANES, SUBLANES, d), lambda i, zs, zc: (i, 0, 0)),
                      pl.BlockSpec((2 * d, ds), const),
                      pl.BlockSpec((2 * d, ds), const),
                      pl.BlockSpec((ds, 2 * d), const)],
            out_specs=[pl.BlockSpec(memory_space=pl.ANY),
                       pl.BlockSpec((tm, 2 * d), lambda i, zs, zc: (i, 0))],
            scratch_shapes=[pltpu.VMEM((EXPERT_ROWS // 2, 1, d), x.dtype),
                            pltpu.SemaphoreType.DMA(()),
                            pltpu.SemaphoreType.DMA(())]),
        out_shape=[jax.ShapeDtypeStruct((n_rows, 1, d), x.dtype),
                   jax.ShapeDtypeStruct((t, 2 * d), BF16)],
        compiler_params=_params(1),
        name="moe_dispatch",
    )(zstart, zcount, dest_tiles, x, swg, swu, swd)


def _expert_kernel(be_ref, nxt_ref, short_ref, nv_ref, xs_hbm, wg_hbm, wu_hbm, wd_hbm, ys_hbm,
                   wgs_ref, wus_ref, wds_ref, wgb_ref, wub_ref, wdb_ref, xbuf_ref, ybuf_ref,
                   wsems, xsems, ysems):
    b = pl.program_id(0)
    n_valid = nv_ref[0]
    e = be_ref[b]
    prev = be_ref[jnp.maximum(b - 1, 0)]
    fresh = jnp.logical_or(b == 0, e != prev)
    slot = lax.rem(b, 2)

    def weight_copies(expert):
        return (pltpu.make_async_copy(wg_hbm.at[expert], wgs_ref, wsems.at[0]),
                pltpu.make_async_copy(wu_hbm.at[expert], wus_ref, wsems.at[1]),
                pltpu.make_async_copy(wd_hbm.at[expert], wds_ref, wsems.at[2]))

    def rows_in(blk, to_slot):
        r0 = pl.multiple_of(blk * EXPERT_ROWS, EXPERT_ROWS)
        return pltpu.make_async_copy(xs_hbm.at[pl.ds(r0, EXPERT_ROWS), 0, :], xbuf_ref.at[to_slot],
                                     xsems.at[to_slot])

    def rows_out(blk, from_slot):
        r0 = pl.multiple_of(blk * EXPERT_ROWS, EXPERT_ROWS)
        return pltpu.make_async_copy(ybuf_ref.at[from_slot], ys_hbm.at[pl.ds(r0, EXPERT_ROWS), 0, :],
                                     ysems.at[from_slot])

    x_slots = xbuf_ref.shape[0]
    xslot = lax.rem(b, x_slots)

    @pl.when(b == 0)
    def _():
        for cp in weight_copies(e):
            cp.start(priority=1)
        for ahead in range(x_slots - 1):
            @pl.when(ahead < n_valid)
            def _():
                rows_in(ahead, ahead).start()

    @pl.when(fresh)
    def _():
        for cp in weight_copies(e):
            cp.wait()
        wgb_ref[...] = wgs_ref[...].astype(BF16)
        wub_ref[...] = wus_ref[...].astype(BF16)
        wdb_ref[...] = wds_ref[...].astype(BF16)
        nxt = nxt_ref[b]

        @pl.when(nxt >= 0)
        def _():
            for cp in weight_copies(nxt):
                cp.start(priority=1)

    @pl.when(b < n_valid)
    def _():
        rows_in(b, xslot).wait()

        @pl.when(b + x_slots - 1 < n_valid)
        def _():
            rows_in(b + x_slots - 1, lax.rem(b + x_slots - 1, x_slots)).start()

        @pl.when(b >= 2)
        def _():
            rows_out(b - 2, slot).wait()

        def swiglu_rows(rows):
            half = xbuf_ref.shape[-1]
            x_lo, x_hi = _unpack_halves(xbuf_ref[xslot, :rows, :])
            xb = jnp.concatenate([x_lo.astype(BF16), x_hi.astype(BF16)], axis=-1)
            hg = jnp.dot(xb, wgb_ref[...], preferred_element_type=F32)
            hu = jnp.dot(xb, wub_ref[...], preferred_element_type=F32)
            act = (_silu(hg) * hu).astype(BF16)
            ybuf_ref[slot, :rows, :] = _pack_halves(jnp.dot(act, wdb_ref[...], preferred_element_type=F32))
            if rows < EXPERT_ROWS:
                ybuf_ref[slot, rows:, :] = jnp.zeros((EXPERT_ROWS - rows, half), ybuf_ref.dtype)

        pl.when(short_ref[b] == 1)(functools.partial(swiglu_rows, EXPERT_ROWS // 2))
        pl.when(short_ref[b] == 0)(functools.partial(swiglu_rows, EXPERT_ROWS))
        rows_out(b, slot).start()

        @pl.when(b == n_valid - 1)
        def _():
            rows_out(b, slot).wait()

            @pl.when(b >= 1)
            def _():
                rows_out(b - 1, 1 - slot).wait()


def _experts(xs, wg, wu, wd, block_expert, next_expert, short_block, n_valid):
    n_rows = xs.shape[0]
    d = wg.shape[1]
    de = wg.shape[2]
    nb = n_rows // EXPERT_ROWS
    hbm = pl.BlockSpec(memory_space=pl.ANY)
    return pl.pallas_call(
        _expert_kernel,
        grid_spec=pltpu.PrefetchScalarGridSpec(
            num_scalar_prefetch=4,
            grid=(nb,),
            in_specs=[hbm, hbm, hbm, hbm],
            out_specs=hbm,
            scratch_shapes=[pltpu.VMEM((d, de), F32), pltpu.VMEM((d, de), F32), pltpu.VMEM((de, d), F32),
                            pltpu.VMEM((d, de), BF16), pltpu.VMEM((d, de), BF16), pltpu.VMEM((de, d), BF16),
                            pltpu.VMEM((EXPERT_X_SLOTS, EXPERT_ROWS, d // 2), I32),
                            pltpu.VMEM((2, EXPERT_ROWS, d // 2), I32),
                            pltpu.SemaphoreType.DMA((3,)), pltpu.SemaphoreType.DMA((EXPERT_X_SLOTS,)),
                            pltpu.SemaphoreType.DMA((2,))]),
        out_shape=jax.ShapeDtypeStruct((n_rows, 1, d // 2), I32),
        compiler_params=_params(1),
        name="moe_experts",
    )(block_expert, next_expert, short_block, n_valid, xs, wg, wu, wd)


def _combine_kernel(dest0_ref, destn_ref, x_ref, shared_ref, gate_ref, ys_ref, g_ref, b_ref,
                    op_ref, os_ref, buf_ref, acc_ref, sems, *, tm, n_prompt_tiles):
    i = pl.program_id(0)
    n = pl.num_programs(0)
    half = buf_ref.shape[-1]
    n_groups = tm // COMBINE_GROUP

    def issue_group(dest_ref, to_slot, g):
        for j in range(COMBINE_GROUP):
            t = g * COMBINE_GROUP + j
            for k in range(TOP_K):
                d = dest_ref[k * tm + t]
                pltpu.make_async_copy(ys_ref.at[d], buf_ref.at[to_slot, k, g, pl.ds(j, 1), :],
                                      sems.at[to_slot]).start(priority=k % 2)

    @pl.when(i == 0)
    def _():
        def first(g, carry):
            issue_group(dest0_ref, 0, g)
            return carry
        lax.fori_loop(0, n_groups, first, 0)

    def run_tile(slot):
        for k in range(TOP_K):
            pltpu.make_async_copy(ys_ref.at[pl.ds(0, tm)], ys_ref.at[pl.ds(0, tm)], sems.at[slot]).wait()

        def reduce_group(g):
            r0 = pl.multiple_of(g * COMBINE_GROUP, COMBINE_GROUP)
            gate = gate_ref[pl.ds(r0, COMBINE_GROUP), :]
            acc_lo = acc_hi = None
            for k in range(TOP_K):
                y_lo, y_hi = _unpack_halves(buf_ref[slot, k, g])
                gk = gate[:, k:k + 1]
                acc_lo = gk * y_lo if k == 0 else acc_lo + gk * y_lo
                acc_hi = gk * y_hi if k == 0 else acc_hi + gk * y_hi
            acc_ref[pl.ds(r0, COMBINE_GROUP), :half] = acc_lo
            acc_ref[pl.ds(r0, COMBINE_GROUP), half:] = acc_hi

        @pl.when(i + 1 < n)
        def _():
            def body(g, carry):
                issue_group(destn_ref, 1 - slot, g)
                reduce_group(g)
                return carry
            lax.fori_loop(0, n_groups, body, 0)

        @pl.when(i + 1 >= n)
        def _():
            def body(g, carry):
                reduce_group(g)
                return carry
            lax.fori_loop(0, n_groups, body, 0)

    for s in range(2):
        pl.when(lax.rem(i, 2) == s)(functools.partial(run_tile, s))

    x = x_ref[...]
    out = _layer_norm(DEEPNORM_ALPHA * x + (acc_ref[...] + shared_ref[...].astype(F32)),
                      g_ref[...], b_ref[...])

    @pl.when(i < n_prompt_tiles)
    def _():
        op_ref[...] = out

    @pl.when(i >= n_prompt_tiles)
    def _():
        os_ref[...] = out


def _combine(x, shared, gate_t, dest_tiles, ys, g, b, tm, n_prompt_rows):
    t, d = x.shape
    n_tiles = t // tm
    npt = n_prompt_rows // tm
    const = lambda i: (0, 0)
    return pl.pallas_call(
        functools.partial(_combine_kernel, tm=tm, n_prompt_tiles=npt),
        grid=(n_tiles,),
        in_specs=[pl.BlockSpec((TOP_K * tm,), lambda i: (0,), memory_space=pltpu.SMEM),
                  pl.BlockSpec((TOP_K * tm,), lambda i: (jnp.minimum(i + 1, n_tiles - 1),),
                               memory_space=pltpu.SMEM),
                  pl.BlockSpec((tm, d), lambda i: (i, 0)),
                  pl.BlockSpec((tm, d), lambda i: (i, 0)),
                  pl.BlockSpec((tm, TOP_K), lambda i: (i, 0)),
                  pl.BlockSpec(memory_space=pl.ANY),
                  pl.BlockSpec((1, d), const),
                  pl.BlockSpec((1, d), const)],
        out_specs=[pl.BlockSpec((tm, d), lambda i: (jnp.minimum(i, npt - 1), 0)),
                   pl.BlockSpec((tm, d), lambda i: (jnp.maximum(i - npt, 0), 0))],
        out_shape=[jax.ShapeDtypeStruct((n_prompt_rows, d), F32),
                   jax.ShapeDtypeStruct((t - n_prompt_rows, d), F32)],
        scratch_shapes=[pltpu.VMEM((2, TOP_K, tm // COMBINE_GROUP, COMBINE_GROUP, d // 2), I32),
                        pltpu.VMEM((tm, d), F32),
                        pltpu.SemaphoreType.DMA((2,))],
        compiler_params=_params(1),
        name="moe_combine",
    )(dest_tiles, dest_tiles, x, shared, gate_t, ys, g.reshape(1, d), b.reshape(1, d))


def _tile_major(a, tm):
    k, t = a.shape
    return a.reshape(k, t // tm, tm).transpose(1, 0, 2).reshape(-1)


def _moe_ln(x, x_packed, router_w, router_bias, wg, wu, wd, swg, swu, swd, g, b, n_prompt_rows):
    t, d = x.shape
    w_t = router_w.T
    w_hi = w_t.astype(BF16)
    w_lo = (w_t - w_hi.astype(F32)).astype(BF16)
    tr = _divisor_tile(t, 512, LANES)
    idx, gate, rank, cnt = _router(x, w_hi, w_lo, router_bias, tr)

    counts = cnt[:, 0]
    padded = (counts + EXPERT_ROWS - 1) // EXPERT_ROWS * EXPERT_ROWS
    pad_end = jnp.cumsum(padded)
    pad_start = pad_end - padded
    experts = jnp.arange(N_EXPERTS, dtype=I32)
    onehot = idx[None] == experts[:, None, None]
    dest = jnp.sum(jnp.where(onehot, pad_start[:, None, None], 0), axis=0) + rank
    n_blocks = (t * TOP_K + N_EXPERTS * (EXPERT_ROWS - 1)) // EXPERT_ROWS + 1
    n_valid = (pad_end[-1] // EXPERT_ROWS).astype(I32)
    blk = jnp.minimum(jnp.arange(n_blocks, dtype=I32), n_valid - 1)
    block_expert = jnp.minimum(
        jnp.sum(pad_end[None, :] <= (blk * EXPERT_ROWS)[:, None], axis=1), N_EXPERTS - 1).astype(I32)
    later_used = (experts[None, :] > experts[:, None]) & (counts[None, :] > 0)
    next_used = jnp.min(jnp.where(later_used, experts[None, :], N_EXPERTS), axis=1)
    next_used = jnp.where(next_used == N_EXPERTS, -1, next_used)
    per_block = lambda table: jnp.sum(
        jnp.where(block_expert[:, None] == experts[None, :], table[None, :], 0), axis=1).astype(I32)
    next_expert = per_block(next_used)
    rows_used = per_block(pad_start + counts) - blk * EXPERT_ROWS
    short_block = (rows_used <= EXPERT_ROWS // 2).astype(I32)

    td = _divisor_tile(t, 512, SUBLANES)
    xs, shared = _dispatch(x_packed, _tile_major(dest, td), (pad_start + counts).astype(I32),
                           (padded - counts).astype(I32), swg.astype(BF16), swu.astype(BF16),
                           swd.astype(BF16), n_blocks * EXPERT_ROWS, td)
    ys = _experts(xs, wg, wu, wd, block_expert, next_expert, short_block, n_valid.reshape(1))
    tc = _divisor_tile(math.gcd(n_prompt_rows, t - n_prompt_rows), 128, SUBLANES)
    return _combine(x, shared, gate.T, _tile_major(dest, tc), ys, g, b, tc, n_prompt_rows)


def kernel(x_prompt, x_sample, mem_prompt, state_gla, cache_swa_k, cache_swa_v, cache_mem_k, cache_mem_v,
           w_in, gla_w_gate2, gla_b_gate, gla_norm_g, swa_sinks, w_mix_out, ln1_g, ln1_b,
           mem_wq, mem_wk, mem_wv, mem_wo, ln2_g, ln2_b,
           router_w, router_bias, exp_w_gate, exp_w_up, exp_w_down, sh_w_gate, sh_w_up, sh_w_down,
           ln3_g, ln3_b):
    nbp, seq, d = x_prompt.shape
    nbs, dec, _ = x_sample.shape
    tp, ts = nbp * seq, nbs * dec
    t = tp + ts
    mem_tokens = mem_prompt.shape[1]
    assert state_gla.shape[0] == DEPTH and seq % SWA_WINDOW == 0 and dec % SUBLANES == 0
    assert cache_swa_k.shape[2] == SWA_WINDOW and tp % dec == 0
    tm = _divisor_tile(math.gcd(tp, ts), 512, 16)
    l = 0

    xp = x_prompt.reshape(tp, d)
    xs = x_sample.reshape(ts, d)

    gate_src = GLA_QK + GLA_QK + GLA_VW + GLA_VW
    swa_rows = SWA_QW + 2 * SWA_KVW
    w_in_rt = _reorder_cast(
        w_in[l].T,
        [(0, 0, gate_src),
         (gate_src + GLA_GATE_RANK, COL_SQ, swa_rows),
         (gate_src, COL_GG, GLA_GATE_RANK)],
        IN_COLS_PADDED, _divisor_tile(IN_COLS_PADDED, 512, 16))
    z = _matmul_split(xp, xs, w_in_rt, tm, IN_COLS_PADDED // 2, F32)

    wa2_pad = jnp.concatenate(
        [gla_w_gate2[l], jnp.zeros((LANES - GLA_GATE_RANK, GLA_QK), F32)], axis=0)
    wa2_hi = wa2_pad.astype(BF16)
    wa2_lo = (wa2_pad - wa2_hi.astype(F32)).astype(BF16)
    wa2_stack = jnp.concatenate([wa2_hi, wa2_lo, wa2_hi], axis=0)
    zero_state = jnp.zeros((nbp, GLA_HEADS, GLA_DK, GLA_DV), F32)
    gla_rows = _divisor_tile(seq, GLA_STEP_ROWS, CHUNK)
    mix, st_p = _gla(z, wa2_stack, gla_b_gate[l], gla_norm_g[l], zero_state, None,
                     nb=nbp, n_chunks=seq // gla_rows, rows=gla_rows, chunk=CHUNK, row0=0, chained=True)
    seqs = _divisor_tile(nbs, GLA_SAMPLE_SEQS, 1)
    assert tp % (seqs * dec) == 0
    mix, st_s = _gla(z, wa2_stack, gla_b_gate[l], gla_norm_g[l], state_gla[l], mix,
                     nb=nbs // seqs, n_chunks=1, rows=seqs * dec, chunk=dec, row0=tp, chained=False)
    nblk = seq // SWA_WINDOW
    prev_p = lambda col: (lambda b, i: (b * nblk + jnp.maximum(i - 1, 0), col))
    mix = _swa(z, swa_sinks[l], z, z, (prev_p(COL_SK // SWA_KVW), prev_p(COL_SV // SWA_KVW)), mix,
               nb=nbp, n_blocks=nblk, rows=SWA_WINDOW, row0=0, q_base=0, q_stride=SWA_WINDOW)
    ck = cache_swa_k[l].reshape(nbs * SWA_WINDOW, SWA_KVW)
    cv = cache_swa_v[l].reshape(nbs * SWA_WINDOW, SWA_KVW)
    prev_s = lambda b, i: (b, 0)
    mix = _swa(z, swa_sinks[l], ck, cv, (prev_s, prev_s), mix,
               nb=nbs, n_blocks=1, rows=dec, row0=tp, q_base=PAST_LEN, q_stride=0)
    h1 = _matmul_res_ln(mix, w_mix_out[l].astype(BF16), (xp, xs), ln1_g[l], ln1_b[l], tm)

    mem = mem_prompt.reshape(nbp * mem_tokens, d)
    tmem = _divisor_tile(nbp * mem_tokens, 512, 16)
    mk = _matmul(mem, mem_wk[l], tmem, _divisor_tile(d, 512, LANES), F32)
    mv = _matmul(mem, mem_wv[l], tmem, _divisor_tile(d, 512, LANES), F32)
    q = _matmul(h1, mem_wq[l].astype(BF16), tm, d, BF16)
    tq = _divisor_tile(seq, 1024, 16)
    att = _mem_attn(q, mk, mv, None, nb=nbp, n_tiles=seq // tq, rows=tq, row0=0, mem_tokens=mem_tokens)
    mem_seqs = _divisor_tile(nbs, MEM_SAMPLE_SEQS, 1)
    assert tp % (mem_seqs * dec) == 0
    att = _mem_attn(q, cache_mem_k[l].reshape(nbs * mem_tokens, d), cache_mem_v[l].reshape(nbs * mem_tokens, d),
                    att, nb=nbs // mem_seqs, n_tiles=1, rows=mem_seqs * dec, row0=tp, mem_tokens=mem_tokens,
                    n_seq=mem_seqs)
    h2, h2_packed = _matmul_res_ln(att, mem_wo[l].astype(BF16), h1, ln2_g[l], ln2_b[l], tm, emit_packed=True)

    out_p, out_s = _moe_ln(h2, h2_packed, router_w[l], router_bias[l], exp_w_gate[l], exp_w_up[l],
                           exp_w_down[l], sh_w_gate[l], sh_w_up[l], sh_w_down[l], ln3_g[l], ln3_b[l], tp)

    y_prompt = out_p.reshape(nbp, seq, d)
    y_sample = out_s.reshape(nbs, dec, d)
    kv = z[:, COL_SK:COL_SK + 2 * SWA_KVW]
    kvp = kv[:tp].reshape(nbp, seq, 2 * SWA_KVW)[:, seq - SWA_WINDOW:]
    kvs = kv[tp:].reshape(nbs, dec, 2 * SWA_KVW)
    kv_shape = lambda a: a.reshape(a.shape[0], a.shape[1], SWA_KV_HEADS, SWA_HEAD_DIM)[None]
    mem_shape = lambda a: a.reshape(nbp, mem_tokens, MEM_HEADS, d // MEM_HEADS)[None]
    return (y_prompt, y_sample,
            st_p[None],
            kv_shape(kvp[..., :SWA_KVW]), kv_shape(kvp[..., SWA_KVW:]),
            mem_shape(mk), mem_shape(mv),
            st_s[None],
            kv_shape(kvs[..., :SWA_KVW]), kv_shape(kvs[..., SWA_KVW:]))
```

```python
import functools
import math

import jax
import jax.numpy as jnp
from jax import lax
from jax.experimental import pallas as pl
from jax.experimental.pallas import tpu as pltpu

F32 = jnp.float32
BF16 = jnp.bfloat16
I32 = jnp.int32

CHUNK = 64
PAST_LEN = 2048
GLA_HEADS = 8
GLA_DK = 64
GLA_DV = 128
GLA_QK = GLA_HEADS * GLA_DK
GLA_VW = GLA_HEADS * GLA_DV
GLA_GATE_RANK = 16
GLA_TAU = 16.0
SWA_HEADS = 16
SWA_KV_HEADS = 2
SWA_GROUP = SWA_HEADS // SWA_KV_HEADS
SWA_HEAD_DIM = 64
SWA_QW = SWA_HEADS * SWA_HEAD_DIM
SWA_KVW = SWA_KV_HEADS * SWA_HEAD_DIM
SWA_WINDOW = 128
WINDOW_CHUNKS = SWA_WINDOW // CHUNK
MEM_HEADS = 4
N_EXPERTS = 64
TOP_K = 8
N_GROUPS = 8
GROUP_SIZE = N_EXPERTS // N_GROUPS
TOPK_GROUPS = 4
ROUTED_SCALE = 2.5
DEPTH = 1
DEEPNORM_ALPHA = (2 * DEPTH) ** 0.25
LN_EPS = 1e-5
RMS_EPS = 1e-6

LANES = 128
SUBLANES = 8
VMEM_LIMIT_BYTES = 56 * 1024 * 1024

COL_GQ = 0
COL_GK = COL_GQ + GLA_QK
COL_GV = COL_GK + GLA_QK
COL_GR = COL_GV + GLA_VW
COL_SQ = COL_GR + GLA_VW
COL_SK = COL_SQ + SWA_QW
COL_SV = COL_SK + SWA_KVW
COL_GG = COL_SV + SWA_KVW
MXU_COLS = 256
IN_COLS_PADDED = -(-(COL_GG + LANES) // (2 * MXU_COLS)) * (2 * MXU_COLS)

EXPERT_ROWS = 256
EXPERT_X_SLOTS = 3
COMBINE_GROUP = SUBLANES
LN_SUB_ROWS = 128
GLA_STEP_ROWS = 512
GLA_SAMPLE_SEQS = 8
MEM_SAMPLE_SEQS = 4
COPY_SLOTS = 3

NT_DIMS = (((1,), (1,)), ((), ()))


def _params(n_axes):
    return pltpu.CompilerParams(dimension_semantics=("arbitrary",) * n_axes,
                                vmem_limit_bytes=VMEM_LIMIT_BYTES)


def _sigmoid(x):
    return 1.0 / (1.0 + jnp.exp(-x))


def _silu(x):
    return x * _sigmoid(x)


def _layer_norm(x, g, b):
    mu = jnp.mean(x, axis=-1, keepdims=True)
    xc = x - mu
    var = jnp.mean(xc * xc, axis=-1, keepdims=True)
    return xc * lax.rsqrt(var + LN_EPS) * g + b


HI_MASK = -65536


def _round_to_bf16_bits(x):
    b = lax.bitcast_convert_type(x, I32)
    return b + (0x7FFF + (lax.shift_right_logical(b, 16) & 1))


def _pack_halves(x):
    c = x.shape[1] // 2
    lo = lax.shift_right_logical(_round_to_bf16_bits(x[:, :c]), 16)
    hi = _round_to_bf16_bits(x[:, c:]) & HI_MASK
    return lo | hi


def _unpack_halves(w):
    return (lax.bitcast_convert_type(lax.shift_left(w, 16), F32),
            lax.bitcast_convert_type(w & HI_MASK, F32))


def _divisor_tile(n, pref, mult):
    t = min(pref, n)
    while t > mult and (n % t or t % mult):
        t -= mult
    assert n % t == 0 and t % mult == 0, (n, pref, mult)
    return t


def _mm_kernel(x_ref, w_ref, o_ref, xb_ref):
    @pl.when(pl.program_id(1) == 0)
    def _():
        xb_ref[...] = x_ref[...].astype(BF16)

    o_ref[...] = jnp.dot(xb_ref[...], w_ref[...].astype(BF16), preferred_element_type=F32).astype(o_ref.dtype)


def _matmul(x, w, tm, tn, out_dtype):
    m, k = x.shape
    n = w.shape[1]
    return pl.pallas_call(
        _mm_kernel,
        grid=(m // tm, n // tn),
        in_specs=[pl.BlockSpec((tm, k), lambda i, j: (i, 0)),
                  pl.BlockSpec((k, tn), lambda i, j: (0, j))],
        out_specs=pl.BlockSpec((tm, tn), lambda i, j: (i, j)),
        out_shape=jax.ShapeDtypeStruct((m, n), out_dtype),
        scratch_shapes=[pltpu.VMEM((tm, k), BF16)],
        compiler_params=_params(2),
        name="matmul",
    )(x, w)


def _split_rows_specs(tm, k, n_first_tiles, n_grid_axes):
    if n_grid_axes == 1:
        return [pl.BlockSpec((tm, k), lambda i: (jnp.minimum(i, n_first_tiles - 1), 0)),
                pl.BlockSpec((tm, k), lambda i: (jnp.maximum(i - n_first_tiles, 0), 0))]
    return [pl.BlockSpec((tm, k), lambda j, i: (jnp.minimum(i, n_first_tiles - 1), 0)),
            pl.BlockSpec((tm, k), lambda j, i: (jnp.maximum(i - n_first_tiles, 0), 0))]


def _mm_split_kernel(xa_ref, xb_ref, wt_ref, o_ref, *, n_first_tiles):
    i = pl.program_id(1)

    def product(x_ref):
        o_ref[...] = lax.dot_general(x_ref[...].astype(BF16), wt_ref[...], NT_DIMS,
                                     preferred_element_type=F32).astype(o_ref.dtype)

    @pl.when(i < n_first_tiles)
    def _():
        product(xa_ref)

    @pl.when(i >= n_first_tiles)
    def _():
        product(xb_ref)


def _matmul_split(xa, xb, w_t, tm, tn, out_dtype):
    ma, k = xa.shape
    m = ma + xb.shape[0]
    n = w_t.shape[0]
    return pl.pallas_call(
        functools.partial(_mm_split_kernel, n_first_tiles=ma // tm),
        grid=(n // tn, m // tm),
        in_specs=_split_rows_specs(tm, k, ma // tm, 2) + [pl.BlockSpec((tn, k), lambda j, i: (j, 0))],
        out_specs=pl.BlockSpec((tm, tn), lambda j, i: (i, j)),
        out_shape=jax.ShapeDtypeStruct((m, n), out_dtype),
        compiler_params=_params(2),
        name="matmul_split",
    )(xa, xb, w_t)


def _reorder_cast_kernel(src_hbm, dst_hbm, stage_ref, out_ref, in_sems, out_sems, *, chunk_pieces, chunk):
    n = len(chunk_pieces)
    slots = stage_ref.shape[0]

    def in_copies(c):
        return [pltpu.make_async_copy(src_hbm.at[pl.ds(s0, ln)], stage_ref.at[c % slots, pl.ds(o0, ln)],
                                      in_sems.at[c % slots]) for s0, o0, ln in chunk_pieces[c]]

    def out_copy(c):
        return pltpu.make_async_copy(out_ref.at[c % slots], dst_hbm.at[pl.ds(c * chunk, chunk)],
                                     out_sems.at[c % slots])

    for c in range(min(slots - 1, n)):
        for cp in in_copies(c):
            cp.start()
    for c in range(n):
        if c + slots - 1 < n:
            for cp in in_copies(c + slots - 1):
                cp.start()
        for cp in in_copies(c):
            cp.wait()
        if c >= slots:
            out_copy(c - slots).wait()
        covered = sum(ln for _, _, ln in chunk_pieces[c])
        if covered < chunk:
            stage_ref[c % slots, covered:, :] = jnp.zeros((chunk - covered, stage_ref.shape[-1]),
                                                          stage_ref.dtype)
        out_ref[c % slots] = stage_ref[c % slots].astype(out_ref.dtype)
        out_copy(c).start()
    for c in range(max(n - slots, 0), n):
        out_copy(c).wait()


def _reorder_cast(src, runs, n_out_rows, chunk):
    k = src.shape[1]
    chunk_pieces = []
    for c0 in range(0, n_out_rows, chunk):
        pieces = []
        for s0, d0, ln in runs:
            lo, hi = max(d0, c0), min(d0 + ln, c0 + chunk)
            if lo < hi:
                assert (lo - c0) % 16 == 0 and (s0 + lo - d0) % SUBLANES == 0 and (hi - lo) % 16 == 0
                pieces.append((s0 + lo - d0, lo - c0, hi - lo))
        chunk_pieces.append(pieces)
    hbm = pl.BlockSpec(memory_space=pl.ANY)
    return pl.pallas_call(
        functools.partial(_reorder_cast_kernel, chunk_pieces=chunk_pieces, chunk=chunk),
        in_specs=[hbm],
        out_specs=hbm,
        out_shape=jax.ShapeDtypeStruct((n_out_rows, k), BF16),
        scratch_shapes=[pltpu.VMEM((COPY_SLOTS, chunk, k), src.dtype), pltpu.VMEM((COPY_SLOTS, chunk, k), BF16),
                        pltpu.SemaphoreType.DMA((COPY_SLOTS,)), pltpu.SemaphoreType.DMA((COPY_SLOTS,))],
        compiler_params=pltpu.CompilerParams(vmem_limit_bytes=VMEM_LIMIT_BYTES),
        name="reorder_cast",
    )(src)


def _mm_res_ln_kernel(x_ref, w_ref, *refs, n_res, n_first_tiles):
    res_refs, (g_ref, b_ref, o_ref, *packed_ref) = refs[:n_res], refs[n_res:]
    tm = x_ref.shape[0]
    sub = min(tm, LN_SUB_ROWS)
    use_first = pl.program_id(0) < n_first_tiles
    for r in range(0, tm, sub):
        rows = slice(r, r + sub)
        y = jnp.dot(x_ref[rows, :], w_ref[...], preferred_element_type=F32)
        res = res_refs[0][rows, :]
        if n_res == 2:
            res = jnp.where(use_first, res, res_refs[1][rows, :])
        h = _layer_norm(DEEPNORM_ALPHA * res + y, g_ref[...], b_ref[...])
        o_ref[rows, :] = h
        if packed_ref:
            groups = slice(r // SUBLANES, (r + sub) // SUBLANES)
            packed_ref[0][groups] = _pack_halves(h).reshape(sub // SUBLANES, SUBLANES, h.shape[1] // 2)


def _matmul_res_ln(x, w, res, g, b, tm, emit_packed=False):
    m, k = x.shape
    n = w.shape[1]
    if isinstance(res, tuple):
        n_first_tiles = res[0].shape[0] // tm
        res_specs = _split_rows_specs(tm, n, n_first_tiles, 1)
    else:
        n_first_tiles = 0
        res = (res,)
        res_specs = [pl.BlockSpec((tm, n), lambda i: (i, 0))]
    out_specs = [pl.BlockSpec((tm, n), lambda i: (i, 0))]
    out_shape = [jax.ShapeDtypeStruct((m, n), F32)]
    if emit_packed:
        out_specs.append(pl.BlockSpec((tm // SUBLANES, SUBLANES, n // 2), lambda i: (i, 0, 0)))
        out_shape.append(jax.ShapeDtypeStruct((m // SUBLANES, SUBLANES, n // 2), I32))
    outs = pl.pallas_call(
        functools.partial(_mm_res_ln_kernel, n_res=len(res), n_first_tiles=n_first_tiles),
        grid=(m // tm,),
        in_specs=[pl.BlockSpec((tm, k), lambda i: (i, 0)),
                  pl.BlockSpec((k, n), lambda i: (0, 0))] + res_specs + [
                  pl.BlockSpec((1, n), lambda i: (0, 0)),
                  pl.BlockSpec((1, n), lambda i: (0, 0))],
        out_specs=out_specs,
        out_shape=out_shape,
        compiler_params=_params(1),
        name="matmul_res_ln",
    )(x, w, *res, g.reshape(1, n), b.reshape(1, n))
    return outs if emit_packed else outs[0]


def _split3_bf16(x):
    hi = x.astype(BF16)
    r1 = x - hi.astype(F32)
    mid = r1.astype(BF16)
    lo = (r1 - mid.astype(F32)).astype(BF16)
    return jnp.concatenate([hi, mid, lo], axis=-1)


def _gla_kernel(q_ref, k_ref, v_ref, r_ref, gg_ref, wa2_ref, ba_ref, ng_ref, s0_ref, mix_in_ref,
                o_ref, sfin_ref, st_ref, *, rows, chunk, chained):
    del mix_in_ref
    n_sub = rows // chunk
    assert rows % chunk == 0 and chunk % 16 == 0

    if chained:
        @pl.when(pl.program_id(1) == 0)
        def _():
            for h in range(GLA_HEADS):
                st_ref[h] = s0_ref[0, h].T

    gg = gg_ref[...]
    gg_hi = gg.astype(BF16)
    gg_lo = (gg - gg_hi.astype(F32)).astype(BF16)
    gate = jnp.dot(jnp.concatenate([gg_hi, gg_hi, gg_lo], axis=-1), wa2_ref[...],
                   preferred_element_type=F32) + ba_ref[...]
    log_a = (jnp.minimum(gate, 0.0) - jnp.log1p(jnp.exp(-jnp.abs(gate)))) / GLA_TAU

    ri_c = lax.broadcasted_iota(I32, (chunk, chunk), 0)
    ci_c = lax.broadcasted_iota(I32, (chunk, chunk), 1)
    causal_c = ri_c >= ci_c
    ones = jnp.concatenate([jnp.where(causal_c, 1.0, 0.0), jnp.ones((chunk, chunk), F32)], axis=0).astype(BF16)
    sums = jnp.einsum("cij,cjd->cid", jnp.broadcast_to(ones[None], (n_sub, 2 * chunk, chunk)),
                      _split3_bf16(log_a).reshape(n_sub, chunk, 3 * GLA_QK),
                      preferred_element_type=F32)
    sums = sums[..., :GLA_QK] + sums[..., GLA_QK:2 * GLA_QK] + sums[..., 2 * GLA_QK:]
    bcum = sums[:, :chunk].reshape(rows, GLA_QK)
    b_last = sums[:, chunk:].reshape(rows, GLA_QK)
    q = q_ref[...] * (GLA_DK ** -0.5)
    k = k_ref[...]
    q_in = (q * jnp.exp(bcum)).astype(BF16)
    k_in = (k * jnp.exp(-bcum)).astype(BF16)
    k_out = (k * jnp.exp(b_last - bcum)).astype(BF16)
    decay = jnp.exp(b_last)
    ng = ng_ref[...]
    def heads(x, width):
        return jnp.stack([x[:, h * width:(h + 1) * width] for h in range(GLA_HEADS)], axis=0)

    def head_chunks(x, width):
        return heads(x, width).reshape(GLA_HEADS * n_sub, chunk, width)

    q4 = head_chunks(q_in, GLA_DK)
    v4 = head_chunks(v_ref[...].astype(BF16), GLA_DV)
    attn = jnp.einsum("bqd,bkd->bqk", q4, head_chunks(k_in, GLA_DK), preferred_element_type=F32)
    attn = jnp.where(causal_c[None], attn, 0.0).astype(BF16)
    o = jnp.einsum("bqk,bke->bqe", attn, v4, preferred_element_type=F32)
    upd_t = jnp.einsum("bre,brc->bec", v4, head_chunks(k_out, GLA_DK), preferred_element_type=F32)
    upd_t = upd_t.reshape(GLA_HEADS, n_sub, GLA_DV, GLA_DK)
    states = []
    if chained:
        st = st_ref[...]
    for c in range(n_sub):
        if not chained:
            st = jnp.stack([s0_ref[c, h].T for h in range(GLA_HEADS)], axis=0)
        states.append(st.astype(BF16))
        st = st * heads(decay[c * chunk:c * chunk + 1, :], GLA_DK) + upd_t[:, c]
        if not chained:
            for h in range(GLA_HEADS):
                sfin_ref[c, h] = st[h].T
    if chained:
        st_ref[...] = st
    entering = jnp.stack(states, axis=1).reshape(GLA_HEADS * n_sub, GLA_DV, GLA_DK)
    o = o + jnp.einsum("bqd,bsd->bqs", q4, entering, preferred_element_type=F32)
    o = o.reshape(GLA_HEADS, rows, GLA_DV)
    o = o * lax.rsqrt(jnp.mean(o * o, axis=-1, keepdims=True) + RMS_EPS) * ng[None]
    out = (o * _silu(heads(r_ref[...], GLA_DV))).astype(o_ref.dtype)
    for h in range(GLA_HEADS):
        o_ref[:, h * GLA_DV:(h + 1) * GLA_DV] = out[h]
    if chained:
        @pl.when(pl.program_id(1) == pl.num_programs(1) - 1)
        def _():
            for h in range(GLA_HEADS):
                sfin_ref[0, h] = st_ref[h].T


def _gla(z, wa2_stack, ba, norm_g, s0, mix_in, *, nb, n_chunks, rows, chunk, row0, chained):
    t = z.shape[0]
    rb0 = row0 // rows
    seq_per_step = 1 if chained else rows // chunk
    assert chained or n_chunks == 1
    rmap = lambda colblk: (lambda b, c: (rb0 + b * n_chunks + c, colblk))
    in_specs = [
        pl.BlockSpec((rows, GLA_QK), rmap(COL_GQ // GLA_QK)),
        pl.BlockSpec((rows, GLA_QK), rmap(COL_GK // GLA_QK)),
        pl.BlockSpec((rows, GLA_VW), rmap(COL_GV // GLA_VW)),
        pl.BlockSpec((rows, GLA_VW), rmap(COL_GR // GLA_VW)),
        pl.BlockSpec((rows, LANES), rmap(COL_GG // LANES)),
        pl.BlockSpec((3 * LANES, GLA_QK), lambda b, c: (0, 0)),
        pl.BlockSpec((1, GLA_QK), lambda b, c: (0, 0)),
        pl.BlockSpec((1, GLA_DV), lambda b, c: (0, 0)),
        pl.BlockSpec((seq_per_step, GLA_HEADS, GLA_DK, GLA_DV), lambda b, c: (b, 0, 0, 0)),
    ]
    args = [z, z, z, z, z, wa2_stack, ba.reshape(1, GLA_QK), norm_g.reshape(1, GLA_DV), s0]
    aliases = {}
    if mix_in is None:
        mix_in = jnp.zeros((SUBLANES, LANES), BF16)
        in_specs.append(pl.BlockSpec(memory_space=pl.ANY))
    else:
        in_specs.append(pl.BlockSpec(memory_space=pl.ANY))
        aliases = {len(args): 0}
    args.append(mix_in)
    mix, s_fin = pl.pallas_call(
        functools.partial(_gla_kernel, rows=rows, chunk=chunk, chained=chained),
        grid=(nb, n_chunks),
        in_specs=in_specs,
        out_specs=[pl.BlockSpec((rows, GLA_VW), rmap(0)),
                   pl.BlockSpec((seq_per_step, GLA_HEADS, GLA_DK, GLA_DV), lambda b, c: (b, 0, 0, 0))],
        out_shape=[jax.ShapeDtypeStruct((t, GLA_VW + SWA_QW), BF16),
                   jax.ShapeDtypeStruct((nb * seq_per_step, GLA_HEADS, GLA_DK, GLA_DV), F32)],
        scratch_shapes=[pltpu.VMEM((GLA_HEADS, GLA_DV, GLA_DK), F32)],
        input_output_aliases=aliases,
        compiler_params=_params(2),
        name="gla",
    )(*args)
    return mix, s_fin


def _swa_kernel(sinks_ref, q_ref, kc_ref, vc_ref, kp_ref, vp_ref, mix_in_ref, o_ref, bias_ref, *,
                rows, q_base, q_stride):
    del mix_in_ref
    i = pl.program_id(1)
    n_tables = bias_ref.shape[0]
    nk = SWA_WINDOW + rows
    sr = SWA_GROUP * rows
    assert rows & (rows - 1) == 0 and q_base % CHUNK == 0 and q_stride % CHUNK == 0
    assert n_tables == 1 or q_base + q_stride >= SWA_WINDOW
    head_in_group = lax.shift_right_logical(lax.broadcasted_iota(I32, (sr, 1), 0), rows.bit_length() - 1)

    @pl.when(jnp.logical_and(pl.program_id(0) == 0, i == 0))
    def _():
        row = lax.broadcasted_iota(I32, (sr, nk), 0)
        col = lax.broadcasted_iota(I32, (sr, nk), 1)
        for t in range(n_tables):
            q0 = q_base + t * q_stride
            qpos = q0 + (row & (rows - 1))
            kpos = q0 - SWA_WINDOW + col
            dist = jnp.abs(qpos - kpos).astype(F32)
            qc = lax.shift_right_arithmetic(qpos, CHUNK.bit_length() - 1)
            kc = lax.shift_right_arithmetic(kpos, CHUNK.bit_length() - 1)
            allowed = (kpos >= 0) & (kc <= qc) & (kc >= qc - WINDOW_CHUNKS)
            for g in range(SWA_KV_HEADS):
                slope = jnp.zeros((sr, 1), F32)
                for j in range(SWA_GROUP):
                    h = g * SWA_GROUP + j
                    slope = jnp.where(head_in_group == j, 2.0 ** (-8.0 * (h + 1) / SWA_HEADS), slope)
                bias_ref[t, g] = jnp.where(allowed, -(slope * dist), -jnp.inf)

    table = jnp.minimum(i, n_tables - 1)
    keys = jnp.concatenate([kp_ref[...], kc_ref[...]], axis=0).astype(BF16)
    vals = jnp.concatenate([vp_ref[...], vc_ref[...]], axis=0).astype(BF16)
    for g in range(SWA_KV_HEADS):
        sink = jnp.zeros((sr, 1), F32)
        for j in range(SWA_GROUP):
            sink = jnp.where(head_in_group == j, sinks_ref[g * SWA_GROUP + j], sink)
        kv = slice(g * SWA_HEAD_DIM, (g + 1) * SWA_HEAD_DIM)
        qg = jnp.concatenate(
            [q_ref[:, (g * SWA_GROUP + j) * SWA_HEAD_DIM:(g * SWA_GROUP + j + 1) * SWA_HEAD_DIM]
             for j in range(SWA_GROUP)], axis=0).astype(BF16)
        s = lax.dot_general(qg, keys[:, kv], NT_DIMS, preferred_element_type=F32)
        s = s * (SWA_HEAD_DIM ** -0.5) + bias_ref[table, g]
        m = jnp.maximum(jnp.max(s, axis=-1, keepdims=True), sink)
        p = jnp.exp(s - m)
        denom = jnp.sum(p, axis=-1, keepdims=True) + jnp.exp(sink - m)
        o = jnp.dot(p.astype(BF16), vals[:, kv], preferred_element_type=F32) / denom
        for j in range(0, SWA_GROUP, 2):
            c0 = (g * SWA_GROUP + j) * SWA_HEAD_DIM
            o_ref[:, c0:c0 + LANES] = jnp.concatenate(
                [o[j * rows:(j + 1) * rows], o[(j + 1) * rows:(j + 2) * rows]], axis=-1).astype(o_ref.dtype)


def _swa(z, sinks, k_prev, v_prev, prev_map, mix_in, *, nb, n_blocks, rows, row0, q_base, q_stride):
    rb0 = row0 // rows
    rmap = lambda colblk: (lambda b, i: (rb0 + b * n_blocks + i, colblk))
    return pl.pallas_call(
        functools.partial(_swa_kernel, rows=rows, q_base=q_base, q_stride=q_stride),
        grid=(nb, n_blocks),
        in_specs=[pl.BlockSpec(memory_space=pltpu.SMEM),
                  pl.BlockSpec((rows, SWA_QW), rmap(COL_SQ // SWA_QW)),
                  pl.BlockSpec((rows, SWA_KVW), rmap(COL_SK // SWA_KVW)),
                  pl.BlockSpec((rows, SWA_KVW), rmap(COL_SV // SWA_KVW)),
                  pl.BlockSpec((SWA_WINDOW, SWA_KVW), prev_map[0]),
                  pl.BlockSpec((SWA_WINDOW, SWA_KVW), prev_map[1]),
                  pl.BlockSpec(memory_space=pl.ANY)],
        out_specs=pl.BlockSpec((rows, SWA_QW), rmap(GLA_VW // SWA_QW)),
        out_shape=jax.ShapeDtypeStruct(mix_in.shape, mix_in.dtype),
        scratch_shapes=[pltpu.VMEM((min(2, n_blocks), SWA_KV_HEADS, SWA_GROUP * rows, SWA_WINDOW + rows), F32)],
        input_output_aliases={6: 0},
        compiler_params=_params(2),
        name="swa",
    )(sinks, z, z, z, k_prev, v_prev, mix_in)


def _mem_attn_kernel(q_ref, k_ref, v_ref, o_in_ref, o_ref, *, head_dim, n_seq):
    del o_in_ref
    rq = q_ref.shape[0] // n_seq
    rm = k_ref.shape[0] // n_seq
    for s_i in range(n_seq):
        qr = slice(s_i * rq, (s_i + 1) * rq)
        mr = slice(s_i * rm, (s_i + 1) * rm)
        heads = [slice(h * head_dim, (h + 1) * head_dim) for h in range(MEM_HEADS)]
        q3 = jnp.stack([q_ref[qr, hs] for hs in heads], axis=0)
        k3 = jnp.stack([k_ref[mr, hs] for hs in heads], axis=0).astype(BF16)
        v3 = jnp.stack([v_ref[mr, hs] for hs in heads], axis=0).astype(BF16)
        s = jnp.einsum("hqd,hkd->hqk", q3, k3, preferred_element_type=F32) * (head_dim ** -0.5)
        m = jnp.max(s, axis=-1, keepdims=True)
        p = jnp.exp(s - m)
        denom = jnp.sum(p, axis=-1, keepdims=True)
        o = jnp.einsum("hqk,hkd->hqd", p.astype(BF16), v3, preferred_element_type=F32) / denom
        for h, hs in enumerate(heads):
            o_ref[qr, hs] = o[h].astype(o_ref.dtype)


def _mem_attn(q, mk, mv, o_in, *, nb, n_tiles, rows, row0, mem_tokens, n_seq=1):
    t, d = q.shape
    assert n_seq == 1 or n_tiles == 1
    rb0 = row0 // rows
    qmap = lambda b, i: (rb0 + b * n_tiles + i, 0)
    in_specs = [pl.BlockSpec((rows, d), qmap),
                pl.BlockSpec((n_seq * mem_tokens, d), lambda b, i: (b, 0)),
                pl.BlockSpec((n_seq * mem_tokens, d), lambda b, i: (b, 0)),
                pl.BlockSpec(memory_space=pl.ANY)]
    aliases = {}
    if o_in is None:
        o_in = jnp.zeros((SUBLANES, LANES), BF16)
    else:
        aliases = {3: 0}
    return pl.pallas_call(
        functools.partial(_mem_attn_kernel, head_dim=d // MEM_HEADS, n_seq=n_seq),
        grid=(nb, n_tiles),
        in_specs=in_specs,
        out_specs=pl.BlockSpec((rows, d), qmap),
        out_shape=jax.ShapeDtypeStruct((t, d), BF16),
        input_output_aliases=aliases,
        compiler_params=_params(2),
        name="mem_attn",
    )(q, mk, mv, o_in)


def _pick_first_max(cur, iota, sentinel):
    mx = jnp.max(cur, axis=0, keepdims=True)
    first = jnp.min(jnp.where(cur == mx, iota, sentinel), axis=0, keepdims=True)
    return iota == first, first


def _router_kernel(x_ref, wh_ref, wl_ref, bias_ref, idx_ref, gate_ref, rank_ref, cnt_ref, run_ref, before_ref,
                   *, tm):
    i = pl.program_id(0)

    @pl.when(i == 0)
    def _():
        run_ref[...] = jnp.zeros_like(run_ref)
        ti = lax.broadcasted_iota(I32, (tm, tm), 0)
        tj = lax.broadcasted_iota(I32, (tm, tm), 1)
        before_ref[...] = jnp.where(ti < tj, 1.0, 0.0).astype(BF16)

    x = x_ref[...]
    xh = x.astype(BF16)
    xl = (x - xh.astype(F32)).astype(BF16)
    wh = wh_ref[...]
    logits = (lax.dot_general(wh, xh, NT_DIMS, preferred_element_type=F32)
              + lax.dot_general(wh, xl, NT_DIMS, preferred_element_type=F32)
              + lax.dot_general(wl_ref[...], xh, NT_DIMS, preferred_element_type=F32))
    scores = _sigmoid(logits)
    sel = scores + bias_ref[...]
    neg_inf = -jnp.inf

    li = lax.broadcasted_iota(I32, (GROUP_SIZE, tm), 0)
    grp_rows = []
    for g in range(N_GROUPS):
        blk = sel[g * GROUP_SIZE:(g + 1) * GROUP_SIZE, :]
        pick, _ = _pick_first_max(blk, li, GROUP_SIZE)
        m1 = jnp.max(blk, axis=0, keepdims=True)
        m2 = jnp.max(jnp.where(pick, neg_inf, blk), axis=0, keepdims=True)
        grp_rows.append(m1 + m2)
    grp = jnp.concatenate(grp_rows, axis=0)

    gi = lax.broadcasted_iota(I32, (N_GROUPS, tm), 0)
    gsel = jnp.zeros((N_GROUPS, tm), F32)
    cur = grp
    for _ in range(TOPK_GROUPS):
        pick, _ = _pick_first_max(cur, gi, N_GROUPS)
        gsel = jnp.where(pick, 1.0, gsel)
        cur = jnp.where(pick, neg_inf, cur)
    emask = jnp.concatenate(
        [jnp.broadcast_to(gsel[g:g + 1, :], (GROUP_SIZE, tm)) for g in range(N_GROUPS)], axis=0)

    ei = lax.broadcasted_iota(I32, (N_EXPERTS, tm), 0)
    cur = jnp.where(emask > 0.5, sel, neg_inf)
    chosen = jnp.zeros((N_EXPERTS, tm), F32)
    idx_rows, w_rows = [], []
    for _ in range(TOP_K):
        pick, first = _pick_first_max(cur, ei, N_EXPERTS)
        idx_rows.append(first)
        w_rows.append(jnp.sum(jnp.where(pick, scores, 0.0), axis=0, keepdims=True))
        chosen = jnp.where(pick, 1.0, chosen)
        cur = jnp.where(pick, neg_inf, cur)
    idx = jnp.concatenate(idx_rows, axis=0)
    w = jnp.concatenate(w_rows, axis=0)
    gate_ref[...] = w / jnp.sum(w, axis=0, keepdims=True) * ROUTED_SCALE
    idx_ref[...] = idx

    local = jnp.dot(chosen.astype(BF16), before_ref[...], preferred_element_type=F32)
    total = local + run_ref[:, 0:1]
    rank_rows = [jnp.sum(jnp.where(ei == idx_rows[k], total, 0.0), axis=0, keepdims=True)
                 for k in range(TOP_K)]
    rank_ref[...] = jnp.concatenate(rank_rows, axis=0).astype(I32)
    run_ref[...] = run_ref[...] + jnp.sum(chosen, axis=1, keepdims=True)
    cnt_ref[...] = run_ref[...].astype(I32)


def _router(x, w_hi_t, w_lo_t, bias, tm):
    t, d = x.shape
    return pl.pallas_call(
        functools.partial(_router_kernel, tm=tm),
        grid=(t // tm,),
        in_specs=[pl.BlockSpec((tm, d), lambda i: (i, 0)),
                  pl.BlockSpec((N_EXPERTS, d), lambda i: (0, 0)),
                  pl.BlockSpec((N_EXPERTS, d), lambda i: (0, 0)),
                  pl.BlockSpec((N_EXPERTS, 1), lambda i: (0, 0))],
        out_specs=[pl.BlockSpec((TOP_K, tm), lambda i: (0, i)),
                   pl.BlockSpec((TOP_K, tm), lambda i: (0, i)),
                   pl.BlockSpec((TOP_K, tm), lambda i: (0, i)),
                   pl.BlockSpec((N_EXPERTS, LANES), lambda i: (0, 0))],
        out_shape=[jax.ShapeDtypeStruct((TOP_K, t), I32),
                   jax.ShapeDtypeStruct((TOP_K, t), F32),
                   jax.ShapeDtypeStruct((TOP_K, t), I32),
                   jax.ShapeDtypeStruct((N_EXPERTS, LANES), I32)],
        scratch_shapes=[pltpu.VMEM((N_EXPERTS, LANES), F32), pltpu.VMEM((tm, tm), BF16)],
        compiler_params=_params(1),
        name="router",
    )(x, w_hi_t, w_lo_t, bias.reshape(N_EXPERTS, 1))


PAD_BITS = tuple(1 << s for s in reversed(range(EXPERT_ROWS.bit_length() - 1)))


def _dispatch_kernel(zstart_ref, zcount_ref, dest_ref, x_ref, swg_ref, swu_ref, swd_ref, xs_ref, shared_ref,
                     zero_ref, sem, zsem, *, tm):
    i = pl.program_id(0)

    def start_group(g, carry):
        for j in range(SUBLANES):
            for k in range(TOP_K):
                d = dest_ref[k * tm + g * SUBLANES + j]
                pltpu.make_async_copy(x_ref.at[g, pl.ds(j, 1), :], xs_ref.at[d], sem).start(priority=k % 2)
        return carry

    lax.fori_loop(0, tm // SUBLANES, start_group, 0)

    half = x_ref.shape[-1]
    x_lo, x_hi = _unpack_halves(x_ref[...].reshape(tm, half))
    xb = jnp.concatenate([x_lo.astype(BF16), x_hi.astype(BF16)], axis=-1)
    hg = jnp.dot(xb, swg_ref[...], preferred_element_type=F32)
    hu = jnp.dot(xb, swu_ref[...], preferred_element_type=F32)
    shared_ref[...] = jnp.dot((_silu(hg) * hu).astype(BF16), swd_ref[...],
                              preferred_element_type=F32).astype(shared_ref.dtype)

    for k in range(TOP_K):
        pltpu.make_async_copy(xs_ref.at[pl.ds(0, tm)], xs_ref.at[pl.ds(0, tm)], sem).wait()

    @pl.when(i == pl.num_programs(0) - 1)
    def _():
        zero_ref[...] = jnp.zeros_like(zero_ref)

        def fill(e, wait):
            pos = zstart_ref[e]
            cnt = zcount_ref[e]
            for bit in PAD_BITS:
                has = (cnt & bit) != 0

                @pl.when(has)
                def _():
                    cp = pltpu.make_async_copy(zero_ref.at[pl.ds(0, bit)], xs_ref.at[pl.ds(pos, bit)], zsem)
                    if wait:
                        cp.wait()
                    else:
                        cp.start()

                pos = pos + jnp.where(has, bit, 0)

        def fill_start(e, carry):
            fill(e, False)
            return carry

        def fill_wait(e, carry):
            fill(e, True)
            return carry

        lax.fori_loop(0, N_EXPERTS, fill_start, 0)
        lax.fori_loop(0, N_EXPERTS, fill_wait, 0)


def _dispatch(x, dest_tiles, zstart, zcount, swg, swu, swd, n_rows, tm):
    t = x.shape[0] * SUBLANES
    d = x.shape[2]
    ds = swg.shape[1]
    const = lambda i, zs, zc: (0, 0)
    return pl.pallas_call(
        functools.partial(_dispatch_kernel, tm=tm),
        grid_spec=pltpu.PrefetchScalarGridSpec(
            num_scalar_prefetch=2,
            grid=(t // tm,),
            in_specs=[pl.BlockSpec((TOP_K * tm,), lambda i, zs, zc: (i,), memory_space=pltpu.SMEM),
                      pl.BlockSpec((tm // SUBLANES, SUBLANES, d), lambda i, zs, zc: (i, 0, 0)),
                      pl.BlockSpec((2 * d, ds), const),
                      pl.BlockSpec((2 * d, ds), const),
                      pl.BlockSpec((ds, 2 * d), const)],
            out_specs=[pl.BlockSpec(memory_space=pl.ANY),
                       pl.BlockSpec((tm, 2 * d), lambda i, zs, zc: (i, 0))],
            scratch_shapes=[pltpu.VMEM((EXPERT_ROWS // 2, 1, d), x.dtype),
                            pltpu.SemaphoreType.DMA(()),
                            pltpu.SemaphoreType.DMA(())]),
        out_shape=[jax.ShapeDtypeStruct((n_rows, 1, d), x.dtype),
                   jax.ShapeDtypeStruct((t, 2 * d), BF16)],
        compiler_params=_params(1),
        name="moe_dispatch",
    )(zstart, zcount, dest_tiles, x, swg, swu, swd)


def _expert_kernel(be_ref, nxt_ref, short_ref, nv_ref, xs_hbm, wg_hbm, wu_hbm, wd_hbm, ys_hbm,
                   wgs_ref, wus_ref, wds_ref, wgb_ref, wub_ref, wdb_ref, xbuf_ref, ybuf_ref,
                   wsems, xsems, ysems):
    b = pl.program_id(0)
    n_valid = nv_ref[0]
    e = be_ref[b]
    prev = be_ref[jnp.maximum(b - 1, 0)]
    fresh = jnp.logical_or(b == 0, e != prev)
    slot = lax.rem(b, 2)

    def weight_copies(expert):
        return (pltpu.make_async_copy(wg_hbm.at[expert], wgs_ref, wsems.at[0]),
                pltpu.make_async_copy(wu_hbm.at[expert], wus_ref, wsems.at[1]),
                pltpu.make_async_copy(wd_hbm.at[expert], wds_ref, wsems.at[2]))

    def rows_in(blk, to_slot):
        r0 = pl.multiple_of(blk * EXPERT_ROWS, EXPERT_ROWS)
        return pltpu.make_async_copy(xs_hbm.at[pl.ds(r0, EXPERT_ROWS), 0, :], xbuf_ref.at[to_slot],
                                     xsems.at[to_slot])

    def rows_out(blk, from_slot):
        r0 = pl.multiple_of(blk * EXPERT_ROWS, EXPERT_ROWS)
        return pltpu.make_async_copy(ybuf_ref.at[from_slot], ys_hbm.at[pl.ds(r0, EXPERT_ROWS), 0, :],
                                     ysems.at[from_slot])

    x_slots = xbuf_ref.shape[0]
    xslot = lax.rem(b, x_slots)

    @pl.when(b == 0)
    def _():
        for cp in weight_copies(e):
            cp.start(priority=1)
        for ahead in range(x_slots - 1):
            @pl.when(ahead < n_valid)
            def _():
                rows_in(ahead, ahead).start()

    @pl.when(fresh)
    def _():
        for cp in weight_copies(e):
            cp.wait()
        wgb_ref[...] = wgs_ref[...].astype(BF16)
        wub_ref[...] = wus_ref[...].astype(BF16)
        wdb_ref[...] = wds_ref[...].astype(BF16)
        nxt = nxt_ref[b]

        @pl.when(nxt >= 0)
        def _():
            for cp in weight_copies(nxt):
                cp.start(priority=1)

    @pl.when(b < n_valid)
    def _():
        rows_in(b, xslot).wait()

        @pl.when(b + x_slots - 1 < n_valid)
        def _():
            rows_in(b + x_slots - 1, lax.rem(b + x_slots - 1, x_slots)).start()

        @pl.when(b >= 2)
        def _():
            rows_out(b - 2, slot).wait()

        def swiglu_rows(rows):
            half = xbuf_ref.shape[-1]
            x_lo, x_hi = _unpack_halves(xbuf_ref[xslot, :rows, :])
            xb = jnp.concatenate([x_lo.astype(BF16), x_hi.astype(BF16)], axis=-1)
            hg = jnp.dot(xb, wgb_ref[...], preferred_element_type=F32)
            hu = jnp.dot(xb, wub_ref[...], preferred_element_type=F32)
            act = (_silu(hg) * hu).astype(BF16)
            ybuf_ref[slot, :rows, :] = _pack_halves(jnp.dot(act, wdb_ref[...], preferred_element_type=F32))
            if rows < EXPERT_ROWS:
                ybuf_ref[slot, rows:, :] = jnp.zeros((EXPERT_ROWS - rows, half), ybuf_ref.dtype)

        pl.when(short_ref[b] == 1)(functools.partial(swiglu_rows, EXPERT_ROWS // 2))
        pl.when(short_ref[b] == 0)(functools.partial(swiglu_rows, EXPERT_ROWS))
        rows_out(b, slot).start()

        @pl.when(b == n_valid - 1)
        def _():
            rows_out(b, slot).wait()

            @pl.when(b >= 1)
            def _():
                rows_out(b - 1, 1 - slot).wait()


def _experts(xs, wg, wu, wd, block_expert, next_expert, short_block, n_valid):
    n_rows = xs.shape[0]
    d = wg.shape[1]
    de = wg.shape[2]
    nb = n_rows // EXPERT_ROWS
    hbm = pl.BlockSpec(memory_space=pl.ANY)
    return pl.pallas_call(
        _expert_kernel,
        grid_spec=pltpu.PrefetchScalarGridSpec(
            num_scalar_prefetch=4,
            grid=(nb,),
            in_specs=[hbm, hbm, hbm, hbm],
            out_specs=hbm,
            scratch_shapes=[pltpu.VMEM((d, de), F32), pltpu.VMEM((d, de), F32), pltpu.VMEM((de, d), F32),
                            pltpu.VMEM((d, de), BF16), pltpu.VMEM((d, de), BF16), pltpu.VMEM((de, d), BF16),
                            pltpu.VMEM((EXPERT_X_SLOTS, EXPERT_ROWS, d // 2), I32),
                            pltpu.VMEM((2, EXPERT_ROWS, d // 2), I32),
                            pltpu.SemaphoreType.DMA((3,)), pltpu.SemaphoreType.DMA((EXPERT_X_SLOTS,)),
                            pltpu.SemaphoreType.DMA((2,))]),
        out_shape=jax.ShapeDtypeStruct((n_rows, 1, d // 2), I32),
        compiler_params=_params(1),
        name="moe_experts",
    )(block_expert, next_expert, short_block, n_valid, xs, wg, wu, wd)


def _combine_kernel(dest0_ref, destn_ref, x_ref, shared_ref, gate_ref, ys_ref, g_ref, b_ref,
                    op_ref, os_ref, buf_ref, acc_ref, sems, *, tm, n_prompt_tiles):
    i = pl.program_id(0)
    n = pl.num_programs(0)
    half = buf_ref.shape[-1]
    n_groups = tm // COMBINE_GROUP

    def issue_group(dest_ref, to_slot, g):
        for j in range(COMBINE_GROUP):
            t = g * COMBINE_GROUP + j
            for k in range(TOP_K):
                d = dest_ref[k * tm + t]
                pltpu.make_async_copy(ys_ref.at[d], buf_ref.at[to_slot, k, g, pl.ds(j, 1), :],
                                      sems.at[to_slot]).start(priority=k % 2)

    @pl.when(i == 0)
    def _():
        def first(g, carry):
            issue_group(dest0_ref, 0, g)
            return carry
        lax.fori_loop(0, n_groups, first, 0)

    def run_tile(slot):
        for k in range(TOP_K):
            pltpu.make_async_copy(ys_ref.at[pl.ds(0, tm)], ys_ref.at[pl.ds(0, tm)], sems.at[slot]).wait()

        def reduce_group(g):
            r0 = pl.multiple_of(g * COMBINE_GROUP, COMBINE_GROUP)
            gate = gate_ref[pl.ds(r0, COMBINE_GROUP), :]
            acc_lo = acc_hi = None
            for k in range(TOP_K):
                y_lo, y_hi = _unpack_halves(buf_ref[slot, k, g])
                gk = gate[:, k:k + 1]
                acc_lo = gk * y_lo if k == 0 else acc_lo + gk * y_lo
                acc_hi = gk * y_hi if k == 0 else acc_hi + gk * y_hi
            acc_ref[pl.ds(r0, COMBINE_GROUP), :half] = acc_lo
            acc_ref[pl.ds(r0, COMBINE_GROUP), half:] = acc_hi

        @pl.when(i + 1 < n)
        def _():
            def body(g, carry):
                issue_group(destn_ref, 1 - slot, g)
                reduce_group(g)
                return carry
            lax.fori_loop(0, n_groups, body, 0)

        @pl.when(i + 1 >= n)
        def _():
            def body(g, carry):
                reduce_group(g)
                return carry
            lax.fori_loop(0, n_groups, body, 0)

    for s in range(2):
        pl.when(lax.rem(i, 2) == s)(functools.partial(run_tile, s))

    x = x_ref[...]
    out = _layer_norm(DEEPNORM_ALPHA * x + (acc_ref[...] + shared_ref[...].astype(F32)),
                      g_ref[...], b_ref[...])

    @pl.when(i < n_prompt_tiles)
    def _():
        op_ref[...] = out

    @pl.when(i >= n_prompt_tiles)
    def _():
        os_ref[...] = out


def _combine(x, shared, gate_t, dest_tiles, ys, g, b, tm, n_prompt_rows):
    t, d = x.shape
    n_tiles = t // tm
    npt = n_prompt_rows // tm
    const = lambda i: (0, 0)
    return pl.pallas_call(
        functools.partial(_combine_kernel, tm=tm, n_prompt_tiles=npt),
        grid=(n_tiles,),
        in_specs=[pl.BlockSpec((TOP_K * tm,), lambda i: (0,), memory_space=pltpu.SMEM),
                  pl.BlockSpec((TOP_K * tm,), lambda i: (jnp.minimum(i + 1, n_tiles - 1),),
                               memory_space=pltpu.SMEM),
                  pl.BlockSpec((tm, d), lambda i: (i, 0)),
                  pl.BlockSpec((tm, d), lambda i: (i, 0)),
                  pl.BlockSpec((tm, TOP_K), lambda i: (i, 0)),
                  pl.BlockSpec(memory_space=pl.ANY),
                  pl.BlockSpec((1, d), const),
                  pl.BlockSpec((1, d), const)],
        out_specs=[pl.BlockSpec((tm, d), lambda i: (jnp.minimum(i, npt - 1), 0)),
                   pl.BlockSpec((tm, d), lambda i: (jnp.maximum(i - npt, 0), 0))],
        out_shape=[jax.ShapeDtypeStruct((n_prompt_rows, d), F32),
                   jax.ShapeDtypeStruct((t - n_prompt_rows, d), F32)],
        scratch_shapes=[pltpu.VMEM((2, TOP_K, tm // COMBINE_GROUP, COMBINE_GROUP, d // 2), I32),
                        pltpu.VMEM((tm, d), F32),
                        pltpu.SemaphoreType.DMA((2,))],
        compiler_params=_params(1),
        name="moe_combine",
    )(dest_tiles, dest_tiles, x, shared, gate_t, ys, g.reshape(1, d), b.reshape(1, d))


def _tile_major(a, tm):
    k, t = a.shape
    return a.reshape(k, t // tm, tm).transpose(1, 0, 2).reshape(-1)


def _moe_ln(x, x_packed, router_w, router_bias, wg, wu, wd, swg, swu, swd, g, b, n_prompt_rows):
    t, d = x.shape
    w_t = router_w.T
    w_hi = w_t.astype(BF16)
    w_lo = (w_t - w_hi.astype(F32)).astype(BF16)
    tr = _divisor_tile(t, 512, LANES)
    idx, gate, rank, cnt = _router(x, w_hi, w_lo, router_bias, tr)

    counts = cnt[:, 0]
    padded = (counts + EXPERT_ROWS - 1) // EXPERT_ROWS * EXPERT_ROWS
    pad_end = jnp.cumsum(padded)
    pad_start = pad_end - padded
    experts = jnp.arange(N_EXPERTS, dtype=I32)
    onehot = idx[None] == experts[:, None, None]
    dest = jnp.sum(jnp.where(onehot, pad_start[:, None, None], 0), axis=0) + rank
    n_blocks = (t * TOP_K + N_EXPERTS * (EXPERT_ROWS - 1)) // EXPERT_ROWS + 1
    n_valid = (pad_end[-1] // EXPERT_ROWS).astype(I32)
    blk = jnp.minimum(jnp.arange(n_blocks, dtype=I32), n_valid - 1)
    block_expert = jnp.minimum(
        jnp.sum(pad_end[None, :] <= (blk * EXPERT_ROWS)[:, None], axis=1), N_EXPERTS - 1).astype(I32)
    later_used = (experts[None, :] > experts[:, None]) & (counts[None, :] > 0)
    next_used = jnp.min(jnp.where(later_used, experts[None, :], N_EXPERTS), axis=1)
    next_used = jnp.where(next_used == N_EXPERTS, -1, next_used)
    per_block = lambda table: jnp.sum(
        jnp.where(block_expert[:, None] == experts[None, :], table[None, :], 0), axis=1).astype(I32)
    next_expert = per_block(next_used)
    rows_used = per_block(pad_start + counts) - blk * EXPERT_ROWS
    short_block = (rows_used <= EXPERT_ROWS // 2).astype(I32)

    td = _divisor_tile(t, 512, SUBLANES)
    xs, shared = _dispatch(x_packed, _tile_major(dest, td), (pad_start + counts).astype(I32),
                           (padded - counts).astype(I32), swg.astype(BF16), swu.astype(BF16),
                           swd.astype(BF16), n_blocks * EXPERT_ROWS, td)
    ys = _experts(xs, wg, wu, wd, block_expert, next_expert, short_block, n_valid.reshape(1))
    tc = _divisor_tile(math.gcd(n_prompt_rows, t - n_prompt_rows), 256, SUBLANES)
    return _combine(x, shared, gate.T, _tile_major(dest, tc), ys, g, b, tc, n_prompt_rows)


def kernel(x_prompt, x_sample, mem_prompt, state_gla, cache_swa_k, cache_swa_v, cache_mem_k, cache_mem_v,
           w_in, gla_w_gate2, gla_b_gate, gla_norm_g, swa_sinks, w_mix_out, ln1_g, ln1_b,
           mem_wq, mem_wk, mem_wv, mem_wo, ln2_g, ln2_b,
           router_w, router_bias, exp_w_gate, exp_w_up, exp_w_down, sh_w_gate, sh_w_up, sh_w_down,
           ln3_g, ln3_b):
    nbp, seq, d = x_prompt.shape
    nbs, dec, _ = x_sample.shape
    tp, ts = nbp * seq, nbs * dec
    t = tp + ts
    mem_tokens = mem_prompt.shape[1]
    assert state_gla.shape[0] == DEPTH and seq % SWA_WINDOW == 0 and dec % SUBLANES == 0
    assert cache_swa_k.shape[2] == SWA_WINDOW and tp % dec == 0
    tm = _divisor_tile(math.gcd(tp, ts), 512, 16)
    l = 0

    xp = x_prompt.reshape(tp, d)
    xs = x_sample.reshape(ts, d)

    gate_src = GLA_QK + GLA_QK + GLA_VW + GLA_VW
    swa_rows = SWA_QW + 2 * SWA_KVW
    w_in_rt = _reorder_cast(
        w_in[l].T,
        [(0, 0, gate_src),
         (gate_src + GLA_GATE_RANK, COL_SQ, swa_rows),
         (gate_src, COL_GG, GLA_GATE_RANK)],
        IN_COLS_PADDED, _divisor_tile(IN_COLS_PADDED, 512, 16))
    z = _matmul_split(xp, xs, w_in_rt, tm, IN_COLS_PADDED // 2, F32)

    wa2_pad = jnp.concatenate(
        [gla_w_gate2[l], jnp.zeros((LANES - GLA_GATE_RANK, GLA_QK), F32)], axis=0)
    wa2_hi = wa2_pad.astype(BF16)
    wa2_lo = (wa2_pad - wa2_hi.astype(F32)).astype(BF16)
    wa2_stack = jnp.concatenate([wa2_hi, wa2_lo, wa2_hi], axis=0)
    zero_state = jnp.zeros((nbp, GLA_HEADS, GLA_DK, GLA_DV), F32)
    gla_rows = _divisor_tile(seq, GLA_STEP_ROWS, CHUNK)
    mix, st_p = _gla(z, wa2_stack, gla_b_gate[l], gla_norm_g[l], zero_state, None,
                     nb=nbp, n_chunks=seq // gla_rows, rows=gla_rows, chunk=CHUNK, row0=0, chained=True)
    seqs = _divisor_tile(nbs, GLA_SAMPLE_SEQS, 1)
    assert tp % (seqs * dec) == 0
    mix, st_s = _gla(z, wa2_stack, gla_b_gate[l], gla_norm_g[l], state_gla[l], mix,
                     nb=nbs // seqs, n_chunks=1, rows=seqs * dec, chunk=dec, row0=tp, chained=False)
    nblk = seq // SWA_WINDOW
    prev_p = lambda col: (lambda b, i: (b * nblk + jnp.maximum(i - 1, 0), col))
    mix = _swa(z, swa_sinks[l], z, z, (prev_p(COL_SK // SWA_KVW), prev_p(COL_SV // SWA_KVW)), mix,
               nb=nbp, n_blocks=nblk, rows=SWA_WINDOW, row0=0, q_base=0, q_stride=SWA_WINDOW)
    ck = cache_swa_k[l].reshape(nbs * SWA_WINDOW, SWA_KVW)
    cv = cache_swa_v[l].reshape(nbs * SWA_WINDOW, SWA_KVW)
    prev_s = lambda b, i: (b, 0)
    mix = _swa(z, swa_sinks[l], ck, cv, (prev_s, prev_s), mix,
               nb=nbs, n_blocks=1, rows=dec, row0=tp, q_base=PAST_LEN, q_stride=0)
    h1 = _matmul_res_ln(mix, w_mix_out[l].astype(BF16), (xp, xs), ln1_g[l], ln1_b[l], tm)

    mem = mem_prompt.reshape(nbp * mem_tokens, d)
    tmem = _divisor_tile(nbp * mem_tokens, 512, 16)
    mk = _matmul(mem, mem_wk[l], tmem, _divisor_tile(d, 512, LANES), F32)
    mv = _matmul(mem, mem_wv[l], tmem, _divisor_tile(d, 512, LANES), F32)
    q = _matmul(h1, mem_wq[l].astype(BF16), tm, d, BF16)
    tq = _divisor_tile(seq, 1024, 16)
    att = _mem_attn(q, mk, mv, None, nb=nbp, n_tiles=seq // tq, rows=tq, row0=0, mem_tokens=mem_tokens)
    mem_seqs = _divisor_tile(nbs, MEM_SAMPLE_SEQS, 1)
    assert tp % (mem_seqs * dec) == 0
    att = _mem_attn(q, cache_mem_k[l].reshape(nbs * mem_tokens, d), cache_mem_v[l].reshape(nbs * mem_tokens, d),
                    att, nb=nbs // mem_seqs, n_tiles=1, rows=mem_seqs * dec, row0=tp, mem_tokens=mem_tokens,
                    n_seq=mem_seqs)
    h2, h2_packed = _matmul_res_ln(att, mem_wo[l].astype(BF16), h1, ln2_g[l], ln2_b[l], tm, emit_packed=True)

    out_p, out_s = _moe_ln(h2, h2_packed, router_w[l], router_bias[l], exp_w_gate[l], exp_w_up[l],
                           exp_w_down[l], sh_w_gate[l], sh_w_up[l], sh_w_down[l], ln3_g[l], ln3_b[l], tp)

    y_prompt = out_p.reshape(nbp, seq, d)
    y_sample = out_s.reshape(nbs, dec, d)
    kv = z[:, COL_SK:COL_SK + 2 * SWA_KVW]
    kvp = kv[:tp].reshape(nbp, seq, 2 * SWA_KVW)[:, seq - SWA_WINDOW:]
    kvs = kv[tp:].reshape(nbs, dec, 2 * SWA_KVW)
    kv_shape = lambda a: a.reshape(a.shape[0], a.shape[1], SWA_KV_HEADS, SWA_HEAD_DIM)[None]
    mem_shape = lambda a: a.reshape(nbp, mem_tokens, MEM_HEADS, d // MEM_HEADS)[None]
    return (y_prompt, y_sample,
            st_p[None],
            kv_shape(kvp[..., :SWA_KVW]), kv_shape(kvp[..., SWA_KVW:]),
            mem_shape(mk), mem_shape(mv),
            st_s[None],
            kv_shape(kvs[..., :SWA_KVW]), kv_shape(kvs[..., SWA_KVW:]))
```

```python
import functools
import math

import jax
import jax.numpy as jnp
from jax import lax
from jax.experimental import pallas as pl
from jax.experimental.pallas import tpu as pltpu

F32 = jnp.float32
BF16 = jnp.bfloat16
I32 = jnp.int32

CHUNK = 64
PAST_LEN = 2048
GLA_HEADS = 8
GLA_DK = 64
GLA_DV = 128
GLA_QK = GLA_HEADS * GLA_DK
GLA_VW = GLA_HEADS * GLA_DV
GLA_GATE_RANK = 16
GLA_TAU = 16.0
SWA_HEADS = 16
SWA_KV_HEADS = 2
SWA_GROUP = SWA_HEADS // SWA_KV_HEADS
SWA_HEAD_DIM = 64
SWA_QW = SWA_HEADS * SWA_HEAD_DIM
SWA_KVW = SWA_KV_HEADS * SWA_HEAD_DIM
SWA_WINDOW = 128
WINDOW_CHUNKS = SWA_WINDOW // CHUNK
MEM_HEADS = 4
N_EXPERTS = 64
TOP_K = 8
N_GROUPS = 8
GROUP_SIZE = N_EXPERTS // N_GROUPS
TOPK_GROUPS = 4
ROUTED_SCALE = 2.5
DEPTH = 1
DEEPNORM_ALPHA = (2 * DEPTH) ** 0.25
LN_EPS = 1e-5
RMS_EPS = 1e-6

LANES = 128
SUBLANES = 8
VMEM_LIMIT_BYTES = 56 * 1024 * 1024

COL_GQ = 0
COL_GK = COL_GQ + GLA_QK
COL_GV = COL_GK + GLA_QK
COL_GR = COL_GV + GLA_VW
COL_SQ = COL_GR + GLA_VW
COL_SK = COL_SQ + SWA_QW
COL_SV = COL_SK + SWA_KVW
COL_GG = COL_SV + SWA_KVW
MXU_COLS = 256
IN_COLS_PADDED = -(-(COL_GG + LANES) // (2 * MXU_COLS)) * (2 * MXU_COLS)

EXPERT_ROWS = 256
EXPERT_X_SLOTS = 3
COMBINE_GROUP = SUBLANES
LN_SUB_ROWS = 128
GLA_STEP_ROWS = 512
GLA_SAMPLE_SEQS = 8
MEM_SAMPLE_SEQS = 4
COPY_SLOTS = 3
ROW_RING_SLOTS = 3

NT_DIMS = (((1,), (1,)), ((), ()))


def _params(n_axes):
    return pltpu.CompilerParams(dimension_semantics=("arbitrary",) * n_axes,
                                vmem_limit_bytes=VMEM_LIMIT_BYTES)


def _sigmoid(x):
    return 1.0 / (1.0 + jnp.exp(-x))


def _silu(x):
    return x * _sigmoid(x)


def _layer_norm(x, g, b):
    mu = jnp.mean(x, axis=-1, keepdims=True)
    xc = x - mu
    var = jnp.mean(xc * xc, axis=-1, keepdims=True)
    return xc * lax.rsqrt(var + LN_EPS) * g + b


HI_MASK = -65536


def _round_to_bf16_bits(x):
    b = lax.bitcast_convert_type(x, I32)
    return b + (0x7FFF + (lax.shift_right_logical(b, 16) & 1))


def _pack_halves(x):
    c = x.shape[1] // 2
    lo = lax.shift_right_logical(_round_to_bf16_bits(x[:, :c]), 16)
    hi = _round_to_bf16_bits(x[:, c:]) & HI_MASK
    return lo | hi


def _unpack_halves(w):
    return (lax.bitcast_convert_type(lax.shift_left(w, 16), F32),
            lax.bitcast_convert_type(w & HI_MASK, F32))


def _divisor_tile(n, pref, mult):
    t = min(pref, n)
    while t > mult and (n % t or t % mult):
        t -= mult
    assert n % t == 0 and t % mult == 0, (n, pref, mult)
    return t


def _mm_kernel(x_ref, w_ref, o_ref, xb_ref):
    @pl.when(pl.program_id(1) == 0)
    def _():
        xb_ref[...] = x_ref[...].astype(BF16)

    o_ref[...] = jnp.dot(xb_ref[...], w_ref[...].astype(BF16), preferred_element_type=F32).astype(o_ref.dtype)


def _matmul(x, w, tm, tn, out_dtype):
    m, k = x.shape
    n = w.shape[1]
    return pl.pallas_call(
        _mm_kernel,
        grid=(m // tm, n // tn),
        in_specs=[pl.BlockSpec((tm, k), lambda i, j: (i, 0)),
                  pl.BlockSpec((k, tn), lambda i, j: (0, j))],
        out_specs=pl.BlockSpec((tm, tn), lambda i, j: (i, j)),
        out_shape=jax.ShapeDtypeStruct((m, n), out_dtype),
        scratch_shapes=[pltpu.VMEM((tm, k), BF16)],
        compiler_params=_params(2),
        name="matmul",
    )(x, w)


def _split_rows_specs(tm, k, n_first_tiles, n_grid_axes):
    if n_grid_axes == 1:
        return [pl.BlockSpec((tm, k), lambda i: (jnp.minimum(i, n_first_tiles - 1), 0)),
                pl.BlockSpec((tm, k), lambda i: (jnp.maximum(i - n_first_tiles, 0), 0))]
    return [pl.BlockSpec((tm, k), lambda j, i: (jnp.minimum(i, n_first_tiles - 1), 0)),
            pl.BlockSpec((tm, k), lambda j, i: (jnp.maximum(i - n_first_tiles, 0), 0))]


def _mm_split_kernel(xa_hbm, xb_hbm, wt_ref, o_ref, xbuf_ref, sems, *, n_first_tiles, tm):
    n_rows = pl.num_programs(1)
    total = pl.num_programs(0) * n_rows
    slots = xbuf_ref.shape[0]
    s = pl.program_id(0) * n_rows + pl.program_id(1)

    def row_copy(step, start):
        r = lax.rem(step, n_rows)
        slot = lax.rem(step, slots)
        for pred, src, tile in ((r < n_first_tiles, xa_hbm, r), (r >= n_first_tiles, xb_hbm, r - n_first_tiles)):
            @pl.when(pred)
            def _():
                cp = pltpu.make_async_copy(src.at[pl.ds(pl.multiple_of(tile * tm, tm), tm), :],
                                           xbuf_ref.at[slot], sems.at[slot])
                if start:
                    cp.start()
                else:
                    cp.wait()

    @pl.when(s == 0)
    def _():
        for ahead in range(slots - 1):
            row_copy(s + ahead, True)

    @pl.when(s + slots - 1 < total)
    def _():
        row_copy(s + slots - 1, True)

    row_copy(s, False)
    o_ref[...] = lax.dot_general(xbuf_ref[lax.rem(s, slots)].astype(BF16), wt_ref[...], NT_DIMS,
                                 preferred_element_type=F32).astype(o_ref.dtype)


def _matmul_split(xa, xb, w_t, tm, tn, out_dtype):
    ma, k = xa.shape
    m = ma + xb.shape[0]
    n = w_t.shape[0]
    assert (m // tm) * (n // tn) >= ROW_RING_SLOTS
    hbm = pl.BlockSpec(memory_space=pl.ANY)
    return pl.pallas_call(
        functools.partial(_mm_split_kernel, n_first_tiles=ma // tm, tm=tm),
        grid=(n // tn, m // tm),
        in_specs=[hbm, hbm, pl.BlockSpec((tn, k), lambda j, i: (j, 0))],
        out_specs=pl.BlockSpec((tm, tn), lambda j, i: (i, j)),
        out_shape=jax.ShapeDtypeStruct((m, n), out_dtype),
        scratch_shapes=[pltpu.VMEM((ROW_RING_SLOTS, tm, k), xa.dtype),
                        pltpu.SemaphoreType.DMA((ROW_RING_SLOTS,))],
        compiler_params=_params(2),
        name="matmul_split",
    )(xa, xb, w_t)


def _reorder_cast_kernel(src_hbm, dst_hbm, stage_ref, out_ref, in_sems, out_sems, *, chunk_pieces, chunk):
    n = len(chunk_pieces)
    slots = stage_ref.shape[0]

    def in_copies(c):
        return [pltpu.make_async_copy(src_hbm.at[pl.ds(s0, ln)], stage_ref.at[c % slots, pl.ds(o0, ln)],
                                      in_sems.at[c % slots]) for s0, o0, ln in chunk_pieces[c]]

    def out_copy(c):
        return pltpu.make_async_copy(out_ref.at[c % slots], dst_hbm.at[pl.ds(c * chunk, chunk)],
                                     out_sems.at[c % slots])

    for c in range(min(slots - 1, n)):
        for cp in in_copies(c):
            cp.start()
    for c in range(n):
        if c + slots - 1 < n:
            for cp in in_copies(c + slots - 1):
                cp.start()
        for cp in in_copies(c):
            cp.wait()
        if c >= slots:
            out_copy(c - slots).wait()
        covered = sum(ln for _, _, ln in chunk_pieces[c])
        if covered < chunk:
            stage_ref[c % slots, covered:, :] = jnp.zeros((chunk - covered, stage_ref.shape[-1]),
                                                          stage_ref.dtype)
        out_ref[c % slots] = stage_ref[c % slots].astype(out_ref.dtype)
        out_copy(c).start()
    for c in range(max(n - slots, 0), n):
        out_copy(c).wait()


def _reorder_cast(src, runs, n_out_rows, chunk):
    k = src.shape[1]
    chunk_pieces = []
    for c0 in range(0, n_out_rows, chunk):
        pieces = []
        for s0, d0, ln in runs:
            lo, hi = max(d0, c0), min(d0 + ln, c0 + chunk)
            if lo < hi:
                assert (lo - c0) % 16 == 0 and (s0 + lo - d0) % SUBLANES == 0 and (hi - lo) % 16 == 0
                pieces.append((s0 + lo - d0, lo - c0, hi - lo))
        chunk_pieces.append(pieces)
    hbm = pl.BlockSpec(memory_space=pl.ANY)
    return pl.pallas_call(
        functools.partial(_reorder_cast_kernel, chunk_pieces=chunk_pieces, chunk=chunk),
        in_specs=[hbm],
        out_specs=hbm,
        out_shape=jax.ShapeDtypeStruct((n_out_rows, k), BF16),
        scratch_shapes=[pltpu.VMEM((COPY_SLOTS, chunk, k), src.dtype), pltpu.VMEM((COPY_SLOTS, chunk, k), BF16),
                        pltpu.SemaphoreType.DMA((COPY_SLOTS,)), pltpu.SemaphoreType.DMA((COPY_SLOTS,))],
        compiler_params=pltpu.CompilerParams(vmem_limit_bytes=VMEM_LIMIT_BYTES),
        name="reorder_cast",
    )(src)


def _mm_res_ln_kernel(x_ref, w_ref, *refs, n_res, n_first_tiles):
    res_refs, (g_ref, b_ref, o_ref, *packed_ref) = refs[:n_res], refs[n_res:]
    tm = x_ref.shape[0]
    sub = min(tm, LN_SUB_ROWS)
    use_first = pl.program_id(0) < n_first_tiles
    for r in range(0, tm, sub):
        rows = slice(r, r + sub)
        y = jnp.dot(x_ref[rows, :], w_ref[...], preferred_element_type=F32)
        res = res_refs[0][rows, :]
        if n_res == 2:
            res = jnp.where(use_first, res, res_refs[1][rows, :])
        h = _layer_norm(DEEPNORM_ALPHA * res + y, g_ref[...], b_ref[...])
        o_ref[rows, :] = h
        if packed_ref:
            groups = slice(r // SUBLANES, (r + sub) // SUBLANES)
            packed_ref[0][groups] = _pack_halves(h).reshape(sub // SUBLANES, SUBLANES, h.shape[1] // 2)


def _matmul_res_ln(x, w, res, g, b, tm, emit_packed=False):
    m, k = x.shape
    n = w.shape[1]
    if isinstance(res, tuple):
        n_first_tiles = res[0].shape[0] // tm
        res_specs = _split_rows_specs(tm, n, n_first_tiles, 1)
    else:
        n_first_tiles = 0
        res = (res,)
        res_specs = [pl.BlockSpec((tm, n), lambda i: (i, 0))]
    out_specs = [pl.BlockSpec((tm, n), lambda i: (i, 0))]
    out_shape = [jax.ShapeDtypeStruct((m, n), F32)]
    if emit_packed:
        out_specs.append(pl.BlockSpec((tm // SUBLANES, SUBLANES, n // 2), lambda i: (i, 0, 0)))
        out_shape.append(jax.ShapeDtypeStruct((m // SUBLANES, SUBLANES, n // 2), I32))
    outs = pl.pallas_call(
        functools.partial(_mm_res_ln_kernel, n_res=len(res), n_first_tiles=n_first_tiles),
        grid=(m // tm,),
        in_specs=[pl.BlockSpec((tm, k), lambda i: (i, 0)),
                  pl.BlockSpec((k, n), lambda i: (0, 0))] + res_specs + [
                  pl.BlockSpec((1, n), lambda i: (0, 0)),
                  pl.BlockSpec((1, n), lambda i: (0, 0))],
        out_specs=out_specs,
        out_shape=out_shape,
        compiler_params=_params(1),
        name="matmul_res_ln",
    )(x, w, *res, g.reshape(1, n), b.reshape(1, n))
    return outs if emit_packed else outs[0]


def _split3_bf16(x):
    hi = x.astype(BF16)
    r1 = x - hi.astype(F32)
    mid = r1.astype(BF16)
    lo = (r1 - mid.astype(F32)).astype(BF16)
    return jnp.concatenate([hi, mid, lo], axis=-1)


def _gla_kernel(q_ref, k_ref, v_ref, r_ref, gg_ref, wa2_ref, ba_ref, ng_ref, s0_ref, mix_in_ref,
                o_ref, sfin_ref, st_ref, *, rows, chunk, chained):
    del mix_in_ref
    n_sub = rows // chunk
    assert rows % chunk == 0 and chunk % 16 == 0

    if chained:
        @pl.when(pl.program_id(1) == 0)
        def _():
            for h in range(GLA_HEADS):
                st_ref[h] = s0_ref[0, h].T

    gg = gg_ref[...]
    gg_hi = gg.astype(BF16)
    gg_lo = (gg - gg_hi.astype(F32)).astype(BF16)
    gate = jnp.dot(jnp.concatenate([gg_hi, gg_hi, gg_lo], axis=-1), wa2_ref[...],
                   preferred_element_type=F32) + ba_ref[...]
    log_a = (jnp.minimum(gate, 0.0) - jnp.log1p(jnp.exp(-jnp.abs(gate)))) / GLA_TAU

    ri_c = lax.broadcasted_iota(I32, (chunk, chunk), 0)
    ci_c = lax.broadcasted_iota(I32, (chunk, chunk), 1)
    causal_c = ri_c >= ci_c
    ones = jnp.concatenate([jnp.where(causal_c, 1.0, 0.0), jnp.ones((chunk, chunk), F32)], axis=0).astype(BF16)
    sums = jnp.einsum("cij,cjd->cid", jnp.broadcast_to(ones[None], (n_sub, 2 * chunk, chunk)),
                      _split3_bf16(log_a).reshape(n_sub, chunk, 3 * GLA_QK),
                      preferred_element_type=F32)
    sums = sums[..., :GLA_QK] + sums[..., GLA_QK:2 * GLA_QK] + sums[..., 2 * GLA_QK:]
    bcum = sums[:, :chunk].reshape(rows, GLA_QK)
    b_last = sums[:, chunk:].reshape(rows, GLA_QK)
    q = q_ref[...] * (GLA_DK ** -0.5)
    k = k_ref[...]
    q_in = (q * jnp.exp(bcum)).astype(BF16)
    k_in = (k * jnp.exp(-bcum)).astype(BF16)
    k_out = (k * jnp.exp(b_last - bcum)).astype(BF16)
    decay = jnp.exp(b_last)
    ng = ng_ref[...]
    def heads(x, width):
        return jnp.stack([x[:, h * width:(h + 1) * width] for h in range(GLA_HEADS)], axis=0)

    def head_chunks(x, width):
        return heads(x, width).reshape(GLA_HEADS * n_sub, chunk, width)

    q4 = head_chunks(q_in, GLA_DK)
    v4 = head_chunks(v_ref[...].astype(BF16), GLA_DV)
    attn = jnp.einsum("bqd,bkd->bqk", q4, head_chunks(k_in, GLA_DK), preferred_element_type=F32)
    attn = jnp.where(causal_c[None], attn, 0.0).astype(BF16)
    o = jnp.einsum("bqk,bke->bqe", attn, v4, preferred_element_type=F32)
    upd_t = jnp.einsum("bre,brc->bec", v4, head_chunks(k_out, GLA_DK), preferred_element_type=F32)
    upd_t = upd_t.reshape(GLA_HEADS, n_sub, GLA_DV, GLA_DK)
    states = []
    if chained:
        st = st_ref[...]
    for c in range(n_sub):
        if not chained:
            st = jnp.stack([s0_ref[c, h].T for h in range(GLA_HEADS)], axis=0)
        states.append(st.astype(BF16))
        st = st * heads(decay[c * chunk:c * chunk + 1, :], GLA_DK) + upd_t[:, c]
        if not chained:
            for h in range(GLA_HEADS):
                sfin_ref[c, h] = st[h].T
    if chained:
        st_ref[...] = st
    entering = jnp.stack(states, axis=1).reshape(GLA_HEADS * n_sub, GLA_DV, GLA_DK)
    o = o + jnp.einsum("bqd,bsd->bqs", q4, entering, preferred_element_type=F32)
    o = o.reshape(GLA_HEADS, rows, GLA_DV)
    o = o * lax.rsqrt(jnp.mean(o * o, axis=-1, keepdims=True) + RMS_EPS) * ng[None]
    out = (o * _silu(heads(r_ref[...], GLA_DV))).astype(o_ref.dtype)
    for h in range(GLA_HEADS):
        o_ref[:, h * GLA_DV:(h + 1) * GLA_DV] = out[h]
    if chained:
        @pl.when(pl.program_id(1) == pl.num_programs(1) - 1)
        def _():
            for h in range(GLA_HEADS):
                sfin_ref[0, h] = st_ref[h].T


def _gla(z, wa2_stack, ba, norm_g, s0, mix_in, *, nb, n_chunks, rows, chunk, row0, chained):
    t = z.shape[0]
    rb0 = row0 // rows
    seq_per_step = 1 if chained else rows // chunk
    assert chained or n_chunks == 1
    rmap = lambda colblk: (lambda b, c: (rb0 + b * n_chunks + c, colblk))
    in_specs = [
        pl.BlockSpec((rows, GLA_QK), rmap(COL_GQ // GLA_QK)),
        pl.BlockSpec((rows, GLA_QK), rmap(COL_GK // GLA_QK)),
        pl.BlockSpec((rows, GLA_VW), rmap(COL_GV // GLA_VW)),
        pl.BlockSpec((rows, GLA_VW), rmap(COL_GR // GLA_VW)),
        pl.BlockSpec((rows, LANES), rmap(COL_GG // LANES)),
        pl.BlockSpec((3 * LANES, GLA_QK), lambda b, c: (0, 0)),
        pl.BlockSpec((1, GLA_QK), lambda b, c: (0, 0)),
        pl.BlockSpec((1, GLA_DV), lambda b, c: (0, 0)),
        pl.BlockSpec((seq_per_step, GLA_HEADS, GLA_DK, GLA_DV), lambda b, c: (b, 0, 0, 0)),
    ]
    args = [z, z, z, z, z, wa2_stack, ba.reshape(1, GLA_QK), norm_g.reshape(1, GLA_DV), s0]
    aliases = {}
    if mix_in is None:
        mix_in = jnp.zeros((SUBLANES, LANES), BF16)
        in_specs.append(pl.BlockSpec(memory_space=pl.ANY))
    else:
        in_specs.append(pl.BlockSpec(memory_space=pl.ANY))
        aliases = {len(args): 0}
    args.append(mix_in)
    mix, s_fin = pl.pallas_call(
        functools.partial(_gla_kernel, rows=rows, chunk=chunk, chained=chained),
        grid=(nb, n_chunks),
        in_specs=in_specs,
        out_specs=[pl.BlockSpec((rows, GLA_VW), rmap(0)),
                   pl.BlockSpec((seq_per_step, GLA_HEADS, GLA_DK, GLA_DV), lambda b, c: (b, 0, 0, 0))],
        out_shape=[jax.ShapeDtypeStruct((t, GLA_VW + SWA_QW), BF16),
                   jax.ShapeDtypeStruct((nb * seq_per_step, GLA_HEADS, GLA_DK, GLA_DV), F32)],
        scratch_shapes=[pltpu.VMEM((GLA_HEADS, GLA_DV, GLA_DK), F32)],
        input_output_aliases=aliases,
        compiler_params=_params(2),
        name="gla",
    )(*args)
    return mix, s_fin


def _swa_kernel(sinks_ref, q_ref, kc_ref, vc_ref, kp_ref, vp_ref, mix_in_ref, o_ref, bias_ref, *,
                rows, q_base, q_stride):
    del mix_in_ref
    i = pl.program_id(1)
    n_tables = bias_ref.shape[0]
    nk = SWA_WINDOW + rows
    sr = SWA_GROUP * rows
    assert rows & (rows - 1) == 0 and q_base % CHUNK == 0 and q_stride % CHUNK == 0
    assert n_tables == 1 or q_base + q_stride >= SWA_WINDOW
    head_in_group = lax.shift_right_logical(lax.broadcasted_iota(I32, (sr, 1), 0), rows.bit_length() - 1)

    @pl.when(jnp.logical_and(pl.program_id(0) == 0, i == 0))
    def _():
        row = lax.broadcasted_iota(I32, (sr, nk), 0)
        col = lax.broadcasted_iota(I32, (sr, nk), 1)
        for t in range(n_tables):
            q0 = q_base + t * q_stride
            qpos = q0 + (row & (rows - 1))
            kpos = q0 - SWA_WINDOW + col
            dist = jnp.abs(qpos - kpos).astype(F32)
            qc = lax.shift_right_arithmetic(qpos, CHUNK.bit_length() - 1)
            kc = lax.shift_right_arithmetic(kpos, CHUNK.bit_length() - 1)
            allowed = (kpos >= 0) & (kc <= qc) & (kc >= qc - WINDOW_CHUNKS)
            for g in range(SWA_KV_HEADS):
                slope = jnp.zeros((sr, 1), F32)
                for j in range(SWA_GROUP):
                    h = g * SWA_GROUP + j
                    slope = jnp.where(head_in_group == j, 2.0 ** (-8.0 * (h + 1) / SWA_HEADS), slope)
                bias_ref[t, g] = jnp.where(allowed, -(slope * dist), -jnp.inf)

    table = jnp.minimum(i, n_tables - 1)
    keys = jnp.concatenate([kp_ref[...], kc_ref[...]], axis=0).astype(BF16)
    vals = jnp.concatenate([vp_ref[...], vc_ref[...]], axis=0).astype(BF16)
    for g in range(SWA_KV_HEADS):
        sink = jnp.zeros((sr, 1), F32)
        for j in range(SWA_GROUP):
            sink = jnp.where(head_in_group == j, sinks_ref[g * SWA_GROUP + j], sink)
        kv = slice(g * SWA_HEAD_DIM, (g + 1) * SWA_HEAD_DIM)
        qg = jnp.concatenate(
            [q_ref[:, (g * SWA_GROUP + j) * SWA_HEAD_DIM:(g * SWA_GROUP + j + 1) * SWA_HEAD_DIM]
             for j in range(SWA_GROUP)], axis=0).astype(BF16)
        s = lax.dot_general(qg, keys[:, kv], NT_DIMS, preferred_element_type=F32)
        s = s * (SWA_HEAD_DIM ** -0.5) + bias_ref[table, g]
        m = jnp.maximum(jnp.max(s, axis=-1, keepdims=True), sink)
        p = jnp.exp(s - m)
        denom = jnp.sum(p, axis=-1, keepdims=True) + jnp.exp(sink - m)
        o = jnp.dot(p.astype(BF16), vals[:, kv], preferred_element_type=F32) / denom
        for j in range(0, SWA_GROUP, 2):
            c0 = (g * SWA_GROUP + j) * SWA_HEAD_DIM
            o_ref[:, c0:c0 + LANES] = jnp.concatenate(
                [o[j * rows:(j + 1) * rows], o[(j + 1) * rows:(j + 2) * rows]], axis=-1).astype(o_ref.dtype)


def _swa(z, sinks, k_prev, v_prev, prev_map, mix_in, *, nb, n_blocks, rows, row0, q_base, q_stride):
    rb0 = row0 // rows
    rmap = lambda colblk: (lambda b, i: (rb0 + b * n_blocks + i, colblk))
    return pl.pallas_call(
        functools.partial(_swa_kernel, rows=rows, q_base=q_base, q_stride=q_stride),
        grid=(nb, n_blocks),
        in_specs=[pl.BlockSpec(memory_space=pltpu.SMEM),
                  pl.BlockSpec((rows, SWA_QW), rmap(COL_SQ // SWA_QW)),
                  pl.BlockSpec((rows, SWA_KVW), rmap(COL_SK // SWA_KVW)),
                  pl.BlockSpec((rows, SWA_KVW), rmap(COL_SV // SWA_KVW)),
                  pl.BlockSpec((SWA_WINDOW, SWA_KVW), prev_map[0]),
                  pl.BlockSpec((SWA_WINDOW, SWA_KVW), prev_map[1]),
                  pl.BlockSpec(memory_space=pl.ANY)],
        out_specs=pl.BlockSpec((rows, SWA_QW), rmap(GLA_VW // SWA_QW)),
        out_shape=jax.ShapeDtypeStruct(mix_in.shape, mix_in.dtype),
        scratch_shapes=[pltpu.VMEM((min(2, n_blocks), SWA_KV_HEADS, SWA_GROUP * rows, SWA_WINDOW + rows), F32)],
        input_output_aliases={6: 0},
        compiler_params=_params(2),
        name="swa",
    )(sinks, z, z, z, k_prev, v_prev, mix_in)


def _mem_attn_kernel(q_ref, k_ref, v_ref, o_in_ref, o_ref, *, head_dim, n_seq):
    del o_in_ref
    rq = q_ref.shape[0] // n_seq
    rm = k_ref.shape[0] // n_seq
    for s_i in range(n_seq):
        qr = slice(s_i * rq, (s_i + 1) * rq)
        mr = slice(s_i * rm, (s_i + 1) * rm)
        heads = [slice(h * head_dim, (h + 1) * head_dim) for h in range(MEM_HEADS)]
        q3 = jnp.stack([q_ref[qr, hs] for hs in heads], axis=0)
        k3 = jnp.stack([k_ref[mr, hs] for hs in heads], axis=0).astype(BF16)
        v3 = jnp.stack([v_ref[mr, hs] for hs in heads], axis=0).astype(BF16)
        s = jnp.einsum("hqd,hkd->hqk", q3, k3, preferred_element_type=F32) * (head_dim ** -0.5)
        m = jnp.max(s, axis=-1, keepdims=True)
        p = jnp.exp(s - m)
        denom = jnp.sum(p, axis=-1, keepdims=True)
        o = jnp.einsum("hqk,hkd->hqd", p.astype(BF16), v3, preferred_element_type=F32) / denom
        for h, hs in enumerate(heads):
            o_ref[qr, hs] = o[h].astype(o_ref.dtype)


def _mem_attn(q, mk, mv, o_in, *, nb, n_tiles, rows, row0, mem_tokens, n_seq=1):
    t, d = q.shape
    assert n_seq == 1 or n_tiles == 1
    rb0 = row0 // rows
    qmap = lambda b, i: (rb0 + b * n_tiles + i, 0)
    in_specs = [pl.BlockSpec((rows, d), qmap),
                pl.BlockSpec((n_seq * mem_tokens, d), lambda b, i: (b, 0)),
                pl.BlockSpec((n_seq * mem_tokens, d), lambda b, i: (b, 0)),
                pl.BlockSpec(memory_space=pl.ANY)]
    aliases = {}
    if o_in is None:
        o_in = jnp.zeros((SUBLANES, LANES), BF16)
    else:
        aliases = {3: 0}
    return pl.pallas_call(
        functools.partial(_mem_attn_kernel, head_dim=d // MEM_HEADS, n_seq=n_seq),
        grid=(nb, n_tiles),
        in_specs=in_specs,
        out_specs=pl.BlockSpec((rows, d), qmap),
        out_shape=jax.ShapeDtypeStruct((t, d), BF16),
        input_output_aliases=aliases,
        compiler_params=_params(2),
        name="mem_attn",
    )(q, mk, mv, o_in)


def _pick_first_max(cur, iota, sentinel):
    mx = jnp.max(cur, axis=0, keepdims=True)
    first = jnp.min(jnp.where(cur == mx, iota, sentinel), axis=0, keepdims=True)
    return iota == first, first


def _router_kernel(x_ref, wh_ref, wl_ref, bias_ref, idx_ref, gate_ref, rank_ref, cnt_ref, run_ref, before_ref,
                   *, tm):
    i = pl.program_id(0)

    @pl.when(i == 0)
    def _():
        run_ref[...] = jnp.zeros_like(run_ref)
        ti = lax.broadcasted_iota(I32, (tm, tm), 0)
        tj = lax.broadcasted_iota(I32, (tm, tm), 1)
        before_ref[...] = jnp.where(ti < tj, 1.0, 0.0).astype(BF16)

    x = x_ref[...]
    xh = x.astype(BF16)
    xl = (x - xh.astype(F32)).astype(BF16)
    wh = wh_ref[...]
    logits = (lax.dot_general(wh, xh, NT_DIMS, preferred_element_type=F32)
              + lax.dot_general(wh, xl, NT_DIMS, preferred_element_type=F32)
              + lax.dot_general(wl_ref[...], xh, NT_DIMS, preferred_element_type=F32))
    scores = _sigmoid(logits)
    sel = scores + bias_ref[...]
    neg_inf = -jnp.inf

    li = lax.broadcasted_iota(I32, (GROUP_SIZE, tm), 0)
    grp_rows = []
    for g in range(N_GROUPS):
        blk = sel[g * GROUP_SIZE:(g + 1) * GROUP_SIZE, :]
        pick, _ = _pick_first_max(blk, li, GROUP_SIZE)
        m1 = jnp.max(blk, axis=0, keepdims=True)
        m2 = jnp.max(jnp.where(pick, neg_inf, blk), axis=0, keepdims=True)
        grp_rows.append(m1 + m2)
    grp = jnp.concatenate(grp_rows, axis=0)

    gi = lax.broadcasted_iota(I32, (N_GROUPS, tm), 0)
    gsel = jnp.zeros((N_GROUPS, tm), F32)
    cur = grp
    for _ in range(TOPK_GROUPS):
        pick, _ = _pick_first_max(cur, gi, N_GROUPS)
        gsel = jnp.where(pick, 1.0, gsel)
        cur = jnp.where(pick, neg_inf, cur)
    emask = jnp.concatenate(
        [jnp.broadcast_to(gsel[g:g + 1, :], (GROUP_SIZE, tm)) for g in range(N_GROUPS)], axis=0)

    ei = lax.broadcasted_iota(I32, (N_EXPERTS, tm), 0)
    cur = jnp.where(emask > 0.5, sel, neg_inf)
    chosen = jnp.zeros((N_EXPERTS, tm), F32)
    idx_rows, w_rows = [], []
    for _ in range(TOP_K):
        pick, first = _pick_first_max(cur, ei, N_EXPERTS)
        idx_rows.append(first)
        w_rows.append(jnp.sum(jnp.where(pick, scores, 0.0), axis=0, keepdims=True))
        chosen = jnp.where(pick, 1.0, chosen)
        cur = jnp.where(pick, neg_inf, cur)
    idx = jnp.concatenate(idx_rows, axis=0)
    w = jnp.concatenate(w_rows, axis=0)
    gate_ref[...] = w / jnp.sum(w, axis=0, keepdims=True) * ROUTED_SCALE
    idx_ref[...] = idx

    local = jnp.dot(chosen.astype(BF16), before_ref[...], preferred_element_type=F32)
    total = local + run_ref[:, 0:1]
    rank_rows = [jnp.sum(jnp.where(ei == idx_rows[k], total, 0.0), axis=0, keepdims=True)
                 for k in range(TOP_K)]
    rank_ref[...] = jnp.concatenate(rank_rows, axis=0).astype(I32)
    run_ref[...] = run_ref[...] + jnp.sum(chosen, axis=1, keepdims=True)
    cnt_ref[...] = run_ref[...].astype(I32)


def _router(x, w_hi_t, w_lo_t, bias, tm):
    t, d = x.shape
    return pl.pallas_call(
        functools.partial(_router_kernel, tm=tm),
        grid=(t // tm,),
        in_specs=[pl.BlockSpec((tm, d), lambda i: (i, 0)),
                  pl.BlockSpec((N_EXPERTS, d), lambda i: (0, 0)),
                  pl.BlockSpec((N_EXPERTS, d), lambda i: (0, 0)),
                  pl.BlockSpec((N_EXPERTS, 1), lambda i: (0, 0))],
        out_specs=[pl.BlockSpec((TOP_K, tm), lambda i: (0, i)),
                   pl.BlockSpec((TOP_K, tm), lambda i: (0, i)),
                   pl.BlockSpec((TOP_K, tm), lambda i: (0, i)),
                   pl.BlockSpec((N_EXPERTS, LANES), lambda i: (0, 0))],
        out_shape=[jax.ShapeDtypeStruct((TOP_K, t), I32),
                   jax.ShapeDtypeStruct((TOP_K, t), F32),
                   jax.ShapeDtypeStruct((TOP_K, t), I32),
                   jax.ShapeDtypeStruct((N_EXPERTS, LANES), I32)],
        scratch_shapes=[pltpu.VMEM((N_EXPERTS, LANES), F32), pltpu.VMEM((tm, tm), BF16)],
        compiler_params=_params(1),
        name="router",
    )(x, w_hi_t, w_lo_t, bias.reshape(N_EXPERTS, 1))


PAD_BITS = tuple(1 << s for s in reversed(range(EXPERT_ROWS.bit_length() - 1)))


def _dispatch_kernel(zstart_ref, zcount_ref, dest_ref, x_ref, swg_ref, swu_ref, swd_ref, xs_ref, shared_ref,
                     zero_ref, sem, zsem, *, tm):
    i = pl.program_id(0)

    def start_group(g, carry):
        for j in range(SUBLANES):
            for k in range(TOP_K):
                d = dest_ref[k * tm + g * SUBLANES + j]
                pltpu.make_async_copy(x_ref.at[g, pl.ds(j, 1), :], xs_ref.at[d], sem).start(priority=k % 2)
        return carry

    lax.fori_loop(0, tm // SUBLANES, start_group, 0)

    half = x_ref.shape[-1]
    x_lo, x_hi = _unpack_halves(x_ref[...].reshape(tm, half))
    xb = jnp.concatenate([x_lo.astype(BF16), x_hi.astype(BF16)], axis=-1)
    hg = jnp.dot(xb, swg_ref[...], preferred_element_type=F32)
    hu = jnp.dot(xb, swu_ref[...], preferred_element_type=F32)
    shared_ref[...] = jnp.dot((_silu(hg) * hu).astype(BF16), swd_ref[...],
                              preferred_element_type=F32).astype(shared_ref.dtype)

    for k in range(TOP_K):
        pltpu.make_async_copy(xs_ref.at[pl.ds(0, tm)], xs_ref.at[pl.ds(0, tm)], sem).wait()

    @pl.when(i == pl.num_programs(0) - 1)
    def _():
        zero_ref[...] = jnp.zeros_like(zero_ref)

        def fill(e, wait):
            pos = zstart_ref[e]
            cnt = zcount_ref[e]
            for bit in PAD_BITS:
                has = (cnt & bit) != 0

                @pl.when(has)
                def _():
                    cp = pltpu.make_async_copy(zero_ref.at[pl.ds(0, bit)], xs_ref.at[pl.ds(pos, bit)], zsem)
                    if wait:
                        cp.wait()
                    else:
                        cp.start()

                pos = pos + jnp.where(has, bit, 0)

        def fill_start(e, carry):
            fill(e, False)
            return carry

        def fill_wait(e, carry):
            fill(e, True)
            return carry

        lax.fori_loop(0, N_EXPERTS, fill_start, 0)
        lax.fori_loop(0, N_EXPERTS, fill_wait, 0)


def _dispatch(x, dest_tiles, zstart, zcount, swg, swu, swd, n_rows, tm):
    t = x.shape[0] * SUBLANES
    d = x.shape[2]
    ds = swg.shape[1]
    const = lambda i, zs, zc: (0, 0)
    return pl.pallas_call(
        functools.partial(_dispatch_kernel, tm=tm),
        grid_spec=pltpu.PrefetchScalarGridSpec(
            num_scalar_prefetch=2,
            grid=(t // tm,),
            in_specs=[pl.BlockSpec((TOP_K * tm,), lambda i, zs, zc: (i,), memory_space=pltpu.SMEM),
                      pl.BlockSpec((tm // SUBLANES, SUBLANES, d), lambda i, zs, zc: (i, 0, 0)),
                      pl.BlockSpec((2 * d, ds), const),
                      pl.BlockSpec((2 * d, ds), const),
                      pl.BlockSpec((ds, 2 * d), const)],
            out_specs=[pl.BlockSpec(memory_space=pl.ANY),
                       pl.BlockSpec((tm, 2 * d), lambda i, zs, zc: (i, 0))],
            scratch_shapes=[pltpu.VMEM((EXPERT_ROWS // 2, 1, d), x.dtype),
                            pltpu.SemaphoreType.DMA(()),
                            pltpu.SemaphoreType.DMA(())]),
        out_shape=[jax.ShapeDtypeStruct((n_rows, 1, d), x.dtype),
                   jax.ShapeDtypeStruct((t, 2 * d), BF16)],
        compiler_params=_params(1),
        name="moe_dispatch",
    )(zstart, zcount, dest_tiles, x, swg, swu, swd)


def _expert_kernel(be_ref, nxt_ref, short_ref, nv_ref, xs_hbm, wg_hbm, wu_hbm, wd_hbm, ys_hbm,
                   wgs_ref, wus_ref, wds_ref, wgb_ref, wub_ref, wdb_ref, xbuf_ref, ybuf_ref,
                   wsems, xsems, ysems):
    b = pl.program_id(0)
    n_valid = nv_ref[0]
    e = be_ref[b]
    prev = be_ref[jnp.maximum(b - 1, 0)]
    fresh = jnp.logical_or(b == 0, e != prev)
    slot = lax.rem(b, 2)

    def weight_copies(expert):
        return (pltpu.make_async_copy(wg_hbm.at[expert], wgs_ref, wsems.at[0]),
                pltpu.make_async_copy(wu_hbm.at[expert], wus_ref, wsems.at[1]),
                pltpu.make_async_copy(wd_hbm.at[expert], wds_ref, wsems.at[2]))

    def rows_in(blk, to_slot):
        r0 = pl.multiple_of(blk * EXPERT_ROWS, EXPERT_ROWS)
        return pltpu.make_async_copy(xs_hbm.at[pl.ds(r0, EXPERT_ROWS), 0, :], xbuf_ref.at[to_slot],
                                     xsems.at[to_slot])

    def rows_out(blk, from_slot):
        r0 = pl.multiple_of(blk * EXPERT_ROWS, EXPERT_ROWS)
        return pltpu.make_async_copy(ybuf_ref.at[from_slot], ys_hbm.at[pl.ds(r0, EXPERT_ROWS), 0, :],
                                     ysems.at[from_slot])

    x_slots = xbuf_ref.shape[0]
    xslot = lax.rem(b, x_slots)

    @pl.when(b == 0)
    def _():
        for cp in weight_copies(e):
            cp.start(priority=1)
        for ahead in range(x_slots - 1):
            @pl.when(ahead < n_valid)
            def _():
                rows_in(ahead, ahead).start()

    @pl.when(fresh)
    def _():
        for cp in weight_copies(e):
            cp.wait()
        wgb_ref[...] = wgs_ref[...].astype(BF16)
        wub_ref[...] = wus_ref[...].astype(BF16)
        wdb_ref[...] = wds_ref[...].astype(BF16)
        nxt = nxt_ref[b]

        @pl.when(nxt >= 0)
        def _():
            for cp in weight_copies(nxt):
                cp.start(priority=1)

    @pl.when(b < n_valid)
    def _():
        rows_in(b, xslot).wait()

        @pl.when(b + x_slots - 1 < n_valid)
        def _():
            rows_in(b + x_slots - 1, lax.rem(b + x_slots - 1, x_slots)).start()

        @pl.when(b >= 2)
        def _():
            rows_out(b - 2, slot).wait()

        def swiglu_rows(rows):
            half = xbuf_ref.shape[-1]
            x_lo, x_hi = _unpack_halves(xbuf_ref[xslot, :rows, :])
            xb = jnp.concatenate([x_lo.astype(BF16), x_hi.astype(BF16)], axis=-1)
            hg = jnp.dot(xb, wgb_ref[...], preferred_element_type=F32)
            hu = jnp.dot(xb, wub_ref[...], preferred_element_type=F32)
            act = (_silu(hg) * hu).astype(BF16)
            ybuf_ref[slot, :rows, :] = _pack_halves(jnp.dot(act, wdb_ref[...], preferred_element_type=F32))
            if rows < EXPERT_ROWS:
                ybuf_ref[slot, rows:, :] = jnp.zeros((EXPERT_ROWS - rows, half), ybuf_ref.dtype)

        pl.when(short_ref[b] == 1)(functools.partial(swiglu_rows, EXPERT_ROWS // 2))
        pl.when(short_ref[b] == 0)(functools.partial(swiglu_rows, EXPERT_ROWS))
        rows_out(b, slot).start()

        @pl.when(b == n_valid - 1)
        def _():
            rows_out(b, slot).wait()

            @pl.when(b >= 1)
            def _():
                rows_out(b - 1, 1 - slot).wait()


def _experts(xs, wg, wu, wd, block_expert, next_expert, short_block, n_valid):
    n_rows = xs.shape[0]
    d = wg.shape[1]
    de = wg.shape[2]
    nb = n_rows // EXPERT_ROWS
    hbm = pl.BlockSpec(memory_space=pl.ANY)
    return pl.pallas_call(
        _expert_kernel,
        grid_spec=pltpu.PrefetchScalarGridSpec(
            num_scalar_prefetch=4,
            grid=(nb,),
            in_specs=[hbm, hbm, hbm, hbm],
            out_specs=hbm,
            scratch_shapes=[pltpu.VMEM((d, de), F32), pltpu.VMEM((d, de), F32), pltpu.VMEM((de, d), F32),
                            pltpu.VMEM((d, de), BF16), pltpu.VMEM((d, de), BF16), pltpu.VMEM((de, d), BF16),
                            pltpu.VMEM((EXPERT_X_SLOTS, EXPERT_ROWS, d // 2), I32),
                            pltpu.VMEM((2, EXPERT_ROWS, d // 2), I32),
                            pltpu.SemaphoreType.DMA((3,)), pltpu.SemaphoreType.DMA((EXPERT_X_SLOTS,)),
                            pltpu.SemaphoreType.DMA((2,))]),
        out_shape=jax.ShapeDtypeStruct((n_rows, 1, d // 2), I32),
        compiler_params=_params(1),
        name="moe_experts",
    )(block_expert, next_expert, short_block, n_valid, xs, wg, wu, wd)


def _combine_kernel(dest0_ref, destn_ref, x_ref, shared_ref, gate_ref, ys_ref, g_ref, b_ref,
                    op_ref, os_ref, buf_ref, acc_ref, sems, *, tm, n_prompt_tiles):
    i = pl.program_id(0)
    n = pl.num_programs(0)
    half = buf_ref.shape[-1]
    n_groups = tm // COMBINE_GROUP

    def issue_group(dest_ref, to_slot, g):
        for j in range(COMBINE_GROUP):
            t = g * COMBINE_GROUP + j
            for k in range(TOP_K):
                d = dest_ref[k * tm + t]
                pltpu.make_async_copy(ys_ref.at[d], buf_ref.at[to_slot, k, g, pl.ds(j, 1), :],
                                      sems.at[to_slot]).start(priority=k % 2)

    @pl.when(i == 0)
    def _():
        def first(g, carry):
            issue_group(dest0_ref, 0, g)
            return carry
        lax.fori_loop(0, n_groups, first, 0)

    def run_tile(slot):
        for k in range(TOP_K):
            pltpu.make_async_copy(ys_ref.at[pl.ds(0, tm)], ys_ref.at[pl.ds(0, tm)], sems.at[slot]).wait()

        def reduce_group(g):
            r0 = pl.multiple_of(g * COMBINE_GROUP, COMBINE_GROUP)
            gate = gate_ref[pl.ds(r0, COMBINE_GROUP), :]
            acc_lo = acc_hi = None
            for k in range(TOP_K):
                y_lo, y_hi = _unpack_halves(buf_ref[slot, k, g])
                gk = gate[:, k:k + 1]
                acc_lo = gk * y_lo if k == 0 else acc_lo + gk * y_lo
                acc_hi = gk * y_hi if k == 0 else acc_hi + gk * y_hi
            acc_ref[pl.ds(r0, COMBINE_GROUP), :half] = acc_lo
            acc_ref[pl.ds(r0, COMBINE_GROUP), half:] = acc_hi

        @pl.when(i + 1 < n)
        def _():
            def body(g, carry):
                issue_group(destn_ref, 1 - slot, g)
                reduce_group(g)
                return carry
            lax.fori_loop(0, n_groups, body, 0)

        @pl.when(i + 1 >= n)
        def _():
            def body(g, carry):
                reduce_group(g)
                return carry
            lax.fori_loop(0, n_groups, body, 0)

    for s in range(2):
        pl.when(lax.rem(i, 2) == s)(functools.partial(run_tile, s))

    x = x_ref[...]
    out = _layer_norm(DEEPNORM_ALPHA * x + (acc_ref[...] + shared_ref[...].astype(F32)),
                      g_ref[...], b_ref[...])

    @pl.when(i < n_prompt_tiles)
    def _():
        op_ref[...] = out

    @pl.when(i >= n_prompt_tiles)
    def _():
        os_ref[...] = out


def _combine(x, shared, gate_t, dest_tiles, ys, g, b, tm, n_prompt_rows):
    t, d = x.shape
    n_tiles = t // tm
    npt = n_prompt_rows // tm
    const = lambda i: (0, 0)
    return pl.pallas_call(
        functools.partial(_combine_kernel, tm=tm, n_prompt_tiles=npt),
        grid=(n_tiles,),
        in_specs=[pl.BlockSpec((TOP_K * tm,), lambda i: (0,), memory_space=pltpu.SMEM),
                  pl.BlockSpec((TOP_K * tm,), lambda i: (jnp.minimum(i + 1, n_tiles - 1),),
                               memory_space=pltpu.SMEM),
                  pl.BlockSpec((tm, d), lambda i: (i, 0)),
                  pl.BlockSpec((tm, d), lambda i: (i, 0)),
                  pl.BlockSpec((tm, TOP_K), lambda i: (i, 0)),
                  pl.BlockSpec(memory_space=pl.ANY),
                  pl.BlockSpec((1, d), const),
                  pl.BlockSpec((1, d), const)],
        out_specs=[pl.BlockSpec((tm, d), lambda i: (jnp.minimum(i, npt - 1), 0)),
                   pl.BlockSpec((tm, d), lambda i: (jnp.maximum(i - npt, 0), 0))],
        out_shape=[jax.ShapeDtypeStruct((n_prompt_rows, d), F32),
                   jax.ShapeDtypeStruct((t - n_prompt_rows, d), F32)],
        scratch_shapes=[pltpu.VMEM((2, TOP_K, tm // COMBINE_GROUP, COMBINE_GROUP, d // 2), I32),
                        pltpu.VMEM((tm, d), F32),
                        pltpu.SemaphoreType.DMA((2,))],
        compiler_params=_params(1),
        name="moe_combine",
    )(dest_tiles, dest_tiles, x, shared, gate_t, ys, g.reshape(1, d), b.reshape(1, d))


def _tile_major(a, tm):
    k, t = a.shape
    return a.reshape(k, t // tm, tm).transpose(1, 0, 2).reshape(-1)


def _moe_ln(x, x_packed, router_w, router_bias, wg, wu, wd, swg, swu, swd, g, b, n_prompt_rows):
    t, d = x.shape
    w_t = router_w.T
    w_hi = w_t.astype(BF16)
    w_lo = (w_t - w_hi.astype(F32)).astype(BF16)
    tr = _divisor_tile(t, 512, LANES)
    idx, gate, rank, cnt = _router(x, w_hi, w_lo, router_bias, tr)

    counts = cnt[:, 0]
    padded = (counts + EXPERT_ROWS - 1) // EXPERT_ROWS * EXPERT_ROWS
    pad_end = jnp.cumsum(padded)
    pad_start = pad_end - padded
    experts = jnp.arange(N_EXPERTS, dtype=I32)
    onehot = idx[None] == experts[:, None, None]
    dest = jnp.sum(jnp.where(onehot, pad_start[:, None, None], 0), axis=0) + rank
    n_blocks = (t * TOP_K + N_EXPERTS * (EXPERT_ROWS - 1)) // EXPERT_ROWS + 1
    n_valid = (pad_end[-1] // EXPERT_ROWS).astype(I32)
    blk = jnp.minimum(jnp.arange(n_blocks, dtype=I32), n_valid - 1)
    block_expert = jnp.minimum(
        jnp.sum(pad_end[None, :] <= (blk * EXPERT_ROWS)[:, None], axis=1), N_EXPERTS - 1).astype(I32)
    later_used = (experts[None, :] > experts[:, None]) & (counts[None, :] > 0)
    next_used = jnp.min(jnp.where(later_used, experts[None, :], N_EXPERTS), axis=1)
    next_used = jnp.where(next_used == N_EXPERTS, -1, next_used)
    per_block = lambda table: jnp.sum(
        jnp.where(block_expert[:, None] == experts[None, :], table[None, :], 0), axis=1).astype(I32)
    next_expert = per_block(next_used)
    rows_used = per_block(pad_start + counts) - blk * EXPERT_ROWS
    short_block = (rows_used <= EXPERT_ROWS // 2).astype(I32)

    td = _divisor_tile(t, 512, SUBLANES)
    xs, shared = _dispatch(x_packed, _tile_major(dest, td), (pad_start + counts).astype(I32),
                           (padded - counts).astype(I32), swg.astype(BF16), swu.astype(BF16),
                           swd.astype(BF16), n_blocks * EXPERT_ROWS, td)
    ys = _experts(xs, wg, wu, wd, block_expert, next_expert, short_block, n_valid.reshape(1))
    tc = _divisor_tile(math.gcd(n_prompt_rows, t - n_prompt_rows), 256, SUBLANES)
    return _combine(x, shared, gate.T, _tile_major(dest, tc), ys, g, b, tc, n_prompt_rows)


def kernel(x_prompt, x_sample, mem_prompt, state_gla, cache_swa_k, cache_swa_v, cache_mem_k, cache_mem_v,
           w_in, gla_w_gate2, gla_b_gate, gla_norm_g, swa_sinks, w_mix_out, ln1_g, ln1_b,
           mem_wq, mem_wk, mem_wv, mem_wo, ln2_g, ln2_b,
           router_w, router_bias, exp_w_gate, exp_w_up, exp_w_down, sh_w_gate, sh_w_up, sh_w_down,
           ln3_g, ln3_b):
    nbp, seq, d = x_prompt.shape
    nbs, dec, _ = x_sample.shape
    tp, ts = nbp * seq, nbs * dec
    t = tp + ts
    mem_tokens = mem_prompt.shape[1]
    assert state_gla.shape[0] == DEPTH and seq % SWA_WINDOW == 0 and dec % SUBLANES == 0
    assert cache_swa_k.shape[2] == SWA_WINDOW and tp % dec == 0
    tm = _divisor_tile(math.gcd(tp, ts), 512, 16)
    l = 0

    xp = x_prompt.reshape(tp, d)
    xs = x_sample.reshape(ts, d)

    gate_src = GLA_QK + GLA_QK + GLA_VW + GLA_VW
    swa_rows = SWA_QW + 2 * SWA_KVW
    w_in_rt = _reorder_cast(
        w_in[l].T,
        [(0, 0, gate_src),
         (gate_src + GLA_GATE_RANK, COL_SQ, swa_rows),
         (gate_src, COL_GG, GLA_GATE_RANK)],
        IN_COLS_PADDED, _divisor_tile(IN_COLS_PADDED, 512, 16))
    z = _matmul_split(xp, xs, w_in_rt, tm, IN_COLS_PADDED // 2, F32)

    wa2_pad = jnp.concatenate(
        [gla_w_gate2[l], jnp.zeros((LANES - GLA_GATE_RANK, GLA_QK), F32)], axis=0)
    wa2_hi = wa2_pad.astype(BF16)
    wa2_lo = (wa2_pad - wa2_hi.astype(F32)).astype(BF16)
    wa2_stack = jnp.concatenate([wa2_hi, wa2_lo, wa2_hi], axis=0)
    zero_state = jnp.zeros((nbp, GLA_HEADS, GLA_DK, GLA_DV), F32)
    gla_rows = _divisor_tile(seq, GLA_STEP_ROWS, CHUNK)
    mix, st_p = _gla(z, wa2_stack, gla_b_gate[l], gla_norm_g[l], zero_state, None,
                     nb=nbp, n_chunks=seq // gla_rows, rows=gla_rows, chunk=CHUNK, row0=0, chained=True)
    seqs = _divisor_tile(nbs, GLA_SAMPLE_SEQS, 1)
    assert tp % (seqs * dec) == 0
    mix, st_s = _gla(z, wa2_stack, gla_b_gate[l], gla_norm_g[l], state_gla[l], mix,
                     nb=nbs // seqs, n_chunks=1, rows=seqs * dec, chunk=dec, row0=tp, chained=False)
    nblk = seq // SWA_WINDOW
    prev_p = lambda col: (lambda b, i: (b * nblk + jnp.maximum(i - 1, 0), col))
    mix = _swa(z, swa_sinks[l], z, z, (prev_p(COL_SK // SWA_KVW), prev_p(COL_SV // SWA_KVW)), mix,
               nb=nbp, n_blocks=nblk, rows=SWA_WINDOW, row0=0, q_base=0, q_stride=SWA_WINDOW)
    ck = cache_swa_k[l].reshape(nbs * SWA_WINDOW, SWA_KVW)
    cv = cache_swa_v[l].reshape(nbs * SWA_WINDOW, SWA_KVW)
    prev_s = lambda b, i: (b, 0)
    mix = _swa(z, swa_sinks[l], ck, cv, (prev_s, prev_s), mix,
               nb=nbs, n_blocks=1, rows=dec, row0=tp, q_base=PAST_LEN, q_stride=0)
    h1 = _matmul_res_ln(mix, w_mix_out[l].astype(BF16), (xp, xs), ln1_g[l], ln1_b[l], tm)

    mem = mem_prompt.reshape(nbp * mem_tokens, d)
    tmem = _divisor_tile(nbp * mem_tokens, 512, 16)
    mk = _matmul(mem, mem_wk[l], tmem, _divisor_tile(d, 512, LANES), F32)
    mv = _matmul(mem, mem_wv[l], tmem, _divisor_tile(d, 512, LANES), F32)
    q = _matmul(h1, mem_wq[l].astype(BF16), tm, d, BF16)
    tq = _divisor_tile(seq, 1024, 16)
    att = _mem_attn(q, mk, mv, None, nb=nbp, n_tiles=seq // tq, rows=tq, row0=0, mem_tokens=mem_tokens)
    mem_seqs = _divisor_tile(nbs, MEM_SAMPLE_SEQS, 1)
    assert tp % (mem_seqs * dec) == 0
    att = _mem_attn(q, cache_mem_k[l].reshape(nbs * mem_tokens, d), cache_mem_v[l].reshape(nbs * mem_tokens, d),
                    att, nb=nbs // mem_seqs, n_tiles=1, rows=mem_seqs * dec, row0=tp, mem_tokens=mem_tokens,
                    n_seq=mem_seqs)
    h2, h2_packed = _matmul_res_ln(att, mem_wo[l].astype(BF16), h1, ln2_g[l], ln2_b[l], tm, emit_packed=True)

    out_p, out_s = _moe_ln(h2, h2_packed, router_w[l], router_bias[l], exp_w_gate[l], exp_w_up[l],
                           exp_w_down[l], sh_w_gate[l], sh_w_up[l], sh_w_down[l], ln3_g[l], ln3_b[l], tp)

    y_prompt = out_p.reshape(nbp, seq, d)
    y_sample = out_s.reshape(nbs, dec, d)
    kv = z[:, COL_SK:COL_SK + 2 * SWA_KVW]
    kvp = kv[:tp].reshape(nbp, seq, 2 * SWA_KVW)[:, seq - SWA_WINDOW:]
    kvs = kv[tp:].reshape(nbs, dec, 2 * SWA_KVW)
    kv_shape = lambda a: a.reshape(a.shape[0], a.shape[1], SWA_KV_HEADS, SWA_HEAD_DIM)[None]
    mem_shape = lambda a: a.reshape(nbp, mem_tokens, MEM_HEADS, d // MEM_HEADS)[None]
    return (y_prompt, y_sample,
            st_p[None],
            kv_shape(kvp[..., :SWA_KVW]), kv_shape(kvp[..., SWA_KVW:]),
            mem_shape(mk), mem_shape(mv),
            st_s[None],
            kv_shape(kvs[..., :SWA_KVW]), kv_shape(kvs[..., SWA_KVW:]))
```
